```python
import math
import jax, jax.numpy as jnp
from jax import lax
import numpy as np

D_MODEL = 1024
BATCH = 8
SEQ = 8192
DEPTH = 1

D_POOL = D_MODEL
POOL_WINDOWS = (2, 4, 8, 16)
POOL_GROUPS = len(POOL_WINDOWS)
POOL_GROUP_WIDTH = D_POOL // POOL_GROUPS
D_SSM = D_MODEL
SSM_GROUP = 16
SSM_GROUPS = D_SSM // SSM_GROUP
SSM_STATE = 64
DT_MIN = 1e-3
DT_MAX = 1e-1
RMS_EPS = 1e-6
IN_SPLITS = (D_POOL, 2 * D_POOL, 2 * D_POOL + D_SSM, 2 * D_POOL + 2 * D_SSM,
             2 * D_POOL + 2 * D_SSM + D_MODEL)
IN_COLS = 2 * D_POOL + 2 * D_SSM + 2 * D_MODEL

kernel_name = 'hybrid_pool_s5_gated_block'


def rmsnorm(x, gain):
    xf = x.astype(jnp.float32)
    y = xf * lax.rsqrt(jnp.mean(xf * xf, axis=-1, keepdims=True) + RMS_EPS)
    return (y * gain.astype(jnp.float32)).astype(x.dtype)


def causal_pool_mixer(u, pool_w, pool_scale):
    bsz, seq, _ = u.shape
    uf = u.astype(jnp.float32)
    csum = jnp.cumsum(uf, axis=1)
    pos = jnp.arange(1, seq + 1, dtype=jnp.float32)[None, :, None]
    pooled = []
    for g, w in enumerate(POOL_WINDOWS):
        cs = csum[..., g * POOL_GROUP_WIDTH:(g + 1) * POOL_GROUP_WIDTH]
        lagged = jnp.pad(cs, ((0, 0), (w, 0), (0, 0)))[:, :seq]
        count = jnp.minimum(pos, float(w))
        pooled.append((cs - lagged) / count)
    pooled = (jnp.concatenate(pooled, axis=-1) - uf).astype(u.dtype)
    pooled = pooled.reshape(bsz, seq, POOL_GROUPS, POOL_GROUP_WIDTH)
    mixed = jnp.einsum('blgc,gcd->blgd', pooled, pool_w).reshape(bsz, seq, D_POOL)
    return mixed * pool_scale


def _complex_linear_combine(e1, e2):
    ar1, ai1, br1, bi1 = e1
    ar2, ai2, br2, bi2 = e2
    ar = ar2 * ar1 - ai2 * ai1
    ai = ar2 * ai1 + ai2 * ar1
    br = ar2 * br1 - ai2 * bi1 + br2
    bi = ar2 * bi1 + ai2 * br1 + bi2
    return (ar, ai, br, bi)


def s5_mixer(u, a_re, a_im, log_dt, b_re, b_im, c_re, c_im, d_skip, glu_w, glu_b):
    bsz, seq, _ = u.shape
    uf = u.astype(jnp.float32).reshape(bsz, seq, SSM_GROUPS, SSM_GROUP)
    dt = jnp.exp(log_dt.astype(jnp.float32))[:, None]
    lam_re = jnp.minimum(a_re.astype(jnp.float32), -1e-4)
    lam_im = a_im.astype(jnp.float32)
    mag = jnp.exp(lam_re * dt)
    abar_re = mag * jnp.cos(lam_im * dt)
    abar_im = mag * jnp.sin(lam_im * dt)
    den = lam_re * lam_re + lam_im * lam_im
    num_re = abar_re - 1.0
    f_re = (num_re * lam_re + abar_im * lam_im) / den
    f_im = (abar_im * lam_re - num_re * lam_im) / den
    f_re, f_im = f_re[:, :, None], f_im[:, :, None]
    bb_re = f_re * b_re - f_im * b_im
    bb_im = f_re * b_im + f_im * b_re
    bu_re = jnp.einsum('blgh,gph->blgp', uf, bb_re)
    bu_im = jnp.einsum('blgh,gph->blgp', uf, bb_im)
    a_seq_re = jnp.broadcast_to(abar_re[None, None], (1, seq, SSM_GROUPS, SSM_STATE))
    a_seq_im = jnp.broadcast_to(abar_im[None, None], (1, seq, SSM_GROUPS, SSM_STATE))
    _, _, s_re, s_im = lax.associative_scan(
        _complex_linear_combine, (a_seq_re, a_seq_im, bu_re, bu_im), axis=1)
    y = (jnp.einsum('blgp,ghp->blgh', s_re, c_re)
         - jnp.einsum('blgp,ghp->blgh', s_im, c_im)
         + d_skip * uf)
    y = jax.nn.gelu(y.reshape(bsz, seq, D_SSM).astype(u.dtype))
    return y * jax.nn.sigmoid(y @ glu_w + glu_b)


def hybrid_layer(x, c, w_ada, b_ada, norm_pre, norm_post, w_in, pool_w, pool_scale,
                 a_re, a_im, log_dt, b_re, b_im, c_re, c_im, d_skip, glu_w, glu_b,
                 w_branch_pool, w_branch_ssm, w_out):
    mod = jax.nn.silu(c) @ w_ada + b_ada
    shift, scale, gate = jnp.split(mod, 3, axis=-1)
    h = rmsnorm(x, norm_pre) * (1.0 + scale[:, None, :]) + shift[:, None, :]
    proj = h @ w_in
    u_pool, z_pool, u_ssm, z_ssm, g_pool, g_ssm = jnp.split(proj, list(IN_SPLITS), axis=-1)
    y_pool = causal_pool_mixer(u_pool, pool_w, pool_scale) * jax.nn.silu(z_pool)
    y_ssm = s5_mixer(u_ssm, a_re, a_im, log_dt, b_re, b_im, c_re, c_im, d_skip,
                     glu_w, glu_b) * jax.nn.silu(z_ssm)
    merged = (jax.nn.sigmoid(g_pool) * (y_pool @ w_branch_pool)
              + jax.nn.sigmoid(g_ssm) * (y_ssm @ w_branch_ssm))
    out = merged @ w_out
    return x + gate[:, None, :] * rmsnorm(out, norm_post)


def _fwd_setup_inputs(seed: int = 0) -> dict:
    key = jax.random.key(seed)
    ks = jax.random.split(key, 24)
    f32 = jnp.float32

    def nrm(k, shape, s):
        return jax.random.normal(k, shape, f32) * s

    G, P, H = SSM_GROUPS, SSM_STATE, SSM_GROUP
    n_idx = jnp.arange(P, dtype=f32)
    return {
        'x': nrm(ks[0], (BATCH, SEQ, D_MODEL), 1.0),
        'c': nrm(ks[1], (BATCH, D_MODEL), 1.0),
        'w_ada': nrm(ks[2], (DEPTH, D_MODEL, 3 * D_MODEL), 0.5 * D_MODEL ** -0.5),
        'b_ada': nrm(ks[3], (DEPTH, 3 * D_MODEL), 0.02),
        'norm_pre': 1.0 + nrm(ks[4], (DEPTH, D_MODEL), 0.02),
        'norm_post': 1.0 + nrm(ks[5], (DEPTH, D_MODEL), 0.02),
        'w_in': nrm(ks[6], (DEPTH, D_MODEL, IN_COLS), D_MODEL ** -0.5),
        'pool_w': nrm(ks[7], (DEPTH, POOL_GROUPS, POOL_GROUP_WIDTH, POOL_GROUP_WIDTH),
                      POOL_GROUP_WIDTH ** -0.5),
        'pool_scale': 1.0 + nrm(ks[8], (DEPTH, D_POOL), 0.02),
        'ssm_a_re': -0.5 + nrm(ks[9], (DEPTH, G, P), 0.01),
        'ssm_a_im': math.pi * n_idx + nrm(ks[10], (DEPTH, G, P), 0.01),
        'ssm_log_dt': jax.random.uniform(ks[11], (DEPTH, G), f32,
                                         math.log(DT_MIN), math.log(DT_MAX)),
        'ssm_b_re': nrm(ks[12], (DEPTH, G, P, H), (2 * H) ** -0.5),
        'ssm_b_im': nrm(ks[13], (DEPTH, G, P, H), (2 * H) ** -0.5),
        'ssm_c_re': nrm(ks[14], (DEPTH, G, H, P), P ** -0.5),
        'ssm_c_im': nrm(ks[15], (DEPTH, G, H, P), P ** -0.5),
        'ssm_d': nrm(ks[16], (DEPTH, G, H), 0.5),
        'glu_w': nrm(ks[17], (DEPTH, D_SSM, D_SSM), D_SSM ** -0.5),
        'glu_b': nrm(ks[18], (DEPTH, D_SSM), 0.02),
        'w_branch_pool': nrm(ks[19], (DEPTH, D_POOL, D_MODEL), D_POOL ** -0.5),
        'w_branch_ssm': nrm(ks[20], (DEPTH, D_SSM, D_MODEL), D_SSM ** -0.5),
        'w_out': nrm(ks[21], (DEPTH, D_MODEL, D_MODEL), D_MODEL ** -0.5),
    }


def _fwd_reference(x, c, w_ada, b_ada, norm_pre, norm_post, w_in, pool_w, pool_scale,
              ssm_a_re, ssm_a_im, ssm_log_dt, ssm_b_re, ssm_b_im, ssm_c_re, ssm_c_im,
              ssm_d, glu_w, glu_b, w_branch_pool, w_branch_ssm, w_out):
    for layer in range(DEPTH):
        x = hybrid_layer(x, c, w_ada[layer], b_ada[layer], norm_pre[layer], norm_post[layer],
                         w_in[layer], pool_w[layer], pool_scale[layer],
                         ssm_a_re[layer], ssm_a_im[layer], ssm_log_dt[layer],
                         ssm_b_re[layer], ssm_b_im[layer], ssm_c_re[layer], ssm_c_im[layer],
                         ssm_d[layer], glu_w[layer], glu_b[layer],
                         w_branch_pool[layer], w_branch_ssm[layer], w_out[layer])
    return x


import jax as _jax
import jax.numpy as _jnp

TWIN_FORMAT = 'train_step'
FWD_PARAMS = ['x', 'c', 'w_ada', 'b_ada', 'norm_pre', 'norm_post', 'w_in', 'pool_w', 'pool_scale', 'ssm_a_re', 'ssm_a_im', 'ssm_log_dt', 'ssm_b_re', 'ssm_b_im', 'ssm_c_re', 'ssm_c_im', 'ssm_d', 'glu_w', 'glu_b', 'w_branch_pool', 'w_branch_ssm', 'w_out']
TWIN_WEIGHTS = ['w_ada', 'b_ada', 'norm_pre', 'norm_post', 'w_in', 'pool_w', 'pool_scale', 'ssm_a_re', 'ssm_a_im', 'ssm_log_dt', 'ssm_b_re', 'ssm_b_im', 'ssm_c_re', 'ssm_c_im', 'ssm_d', 'glu_w', 'glu_b', 'w_branch_pool', 'w_branch_ssm', 'w_out']
TWIN_DIFF_INPUT = 'x'
TWIN_INPUTS = ['x', 'c', 'w_ada', 'b_ada', 'norm_pre', 'norm_post', 'w_in', 'pool_w', 'pool_scale', 'ssm_a_re', 'ssm_a_im', 'ssm_log_dt', 'ssm_b_re', 'ssm_b_im', 'ssm_c_re', 'ssm_c_im', 'ssm_d', 'glu_w', 'glu_b', 'w_branch_pool', 'w_branch_ssm', 'w_out', 'loss_target', 'm_w_ada', 'm_b_ada', 'm_norm_pre', 'm_norm_post', 'm_w_in', 'm_pool_w', 'm_pool_scale', 'm_ssm_a_re', 'm_ssm_a_im', 'm_ssm_log_dt', 'm_ssm_b_re', 'm_ssm_b_im', 'm_ssm_c_re', 'm_ssm_c_im', 'm_ssm_d', 'm_glu_w', 'm_glu_b', 'm_w_branch_pool', 'm_w_branch_ssm', 'm_w_out', 'v_w_ada', 'v_b_ada', 'v_norm_pre', 'v_norm_post', 'v_w_in', 'v_pool_w', 'v_pool_scale', 'v_ssm_a_re', 'v_ssm_a_im', 'v_ssm_log_dt', 'v_ssm_b_re', 'v_ssm_b_im', 'v_ssm_c_re', 'v_ssm_c_im', 'v_ssm_d', 'v_glu_w', 'v_glu_b', 'v_w_branch_pool', 'v_w_branch_ssm', 'v_w_out']
TWIN_OUTPUTS = ['loss', 'grad_x', 'grad_w_ada', 'grad_b_ada', 'grad_norm_pre', 'grad_norm_post', 'grad_w_in', 'grad_pool_w', 'grad_pool_scale', 'grad_ssm_a_re', 'grad_ssm_a_im', 'grad_ssm_log_dt', 'grad_ssm_b_re', 'grad_ssm_b_im', 'grad_ssm_c_re', 'grad_ssm_c_im', 'grad_ssm_d', 'grad_glu_w', 'grad_glu_b', 'grad_w_branch_pool', 'grad_w_branch_ssm', 'grad_w_out', 'delta_w_ada', 'delta_b_ada', 'delta_norm_pre', 'delta_norm_post', 'delta_w_in', 'delta_pool_w', 'delta_pool_scale', 'delta_ssm_a_re', 'delta_ssm_a_im', 'delta_ssm_log_dt', 'delta_ssm_b_re', 'delta_ssm_b_im', 'delta_ssm_c_re', 'delta_ssm_c_im', 'delta_ssm_d', 'delta_glu_w', 'delta_glu_b', 'delta_w_branch_pool', 'delta_w_branch_ssm', 'delta_w_out', 'new_m_w_ada', 'new_m_b_ada', 'new_m_norm_pre', 'new_m_norm_post', 'new_m_w_in', 'new_m_pool_w', 'new_m_pool_scale', 'new_m_ssm_a_re', 'new_m_ssm_a_im', 'new_m_ssm_log_dt', 'new_m_ssm_b_re', 'new_m_ssm_b_im', 'new_m_ssm_c_re', 'new_m_ssm_c_im', 'new_m_ssm_d', 'new_m_glu_w', 'new_m_glu_b', 'new_m_w_branch_pool', 'new_m_w_branch_ssm', 'new_m_w_out', 'new_v_w_ada', 'new_v_b_ada', 'new_v_norm_pre', 'new_v_norm_post', 'new_v_w_in', 'new_v_pool_w', 'new_v_pool_scale', 'new_v_ssm_a_re', 'new_v_ssm_a_im', 'new_v_ssm_log_dt', 'new_v_ssm_b_re', 'new_v_ssm_b_im', 'new_v_ssm_c_re', 'new_v_ssm_c_im', 'new_v_ssm_d', 'new_v_glu_w', 'new_v_glu_b', 'new_v_w_branch_pool', 'new_v_w_branch_ssm', 'new_v_w_out']
TWIN_LEAF_KINDS = {'loss': 'loss', 'grad_x': 'grad_x', 'grad_w_ada': 'grad_w', 'grad_b_ada': 'grad_w', 'grad_norm_pre': 'grad_w', 'grad_norm_post': 'grad_w', 'grad_w_in': 'grad_w', 'grad_pool_w': 'grad_w', 'grad_pool_scale': 'grad_w', 'grad_ssm_a_re': 'grad_w', 'grad_ssm_a_im': 'grad_w', 'grad_ssm_log_dt': 'grad_w', 'grad_ssm_b_re': 'grad_w', 'grad_ssm_b_im': 'grad_w', 'grad_ssm_c_re': 'grad_w', 'grad_ssm_c_im': 'grad_w', 'grad_ssm_d': 'grad_w', 'grad_glu_w': 'grad_w', 'grad_glu_b': 'grad_w', 'grad_w_branch_pool': 'grad_w', 'grad_w_branch_ssm': 'grad_w', 'grad_w_out': 'grad_w', 'delta_w_ada': 'delta_w', 'delta_b_ada': 'delta_w', 'delta_norm_pre': 'delta_w', 'delta_norm_post': 'delta_w', 'delta_w_in': 'delta_w', 'delta_pool_w': 'delta_w', 'delta_pool_scale': 'delta_w', 'delta_ssm_a_re': 'delta_w', 'delta_ssm_a_im': 'delta_w', 'delta_ssm_log_dt': 'delta_w', 'delta_ssm_b_re': 'delta_w', 'delta_ssm_b_im': 'delta_w', 'delta_ssm_c_re': 'delta_w', 'delta_ssm_c_im': 'delta_w', 'delta_ssm_d': 'delta_w', 'delta_glu_w': 'delta_w', 'delta_glu_b': 'delta_w', 'delta_w_branch_pool': 'delta_w', 'delta_w_branch_ssm': 'delta_w', 'delta_w_out': 'delta_w', 'new_m_w_ada': 'new_m', 'new_m_b_ada': 'new_m', 'new_m_norm_pre': 'new_m', 'new_m_norm_post': 'new_m', 'new_m_w_in': 'new_m', 'new_m_pool_w': 'new_m', 'new_m_pool_scale': 'new_m', 'new_m_ssm_a_re': 'new_m', 'new_m_ssm_a_im': 'new_m', 'new_m_ssm_log_dt': 'new_m', 'new_m_ssm_b_re': 'new_m', 'new_m_ssm_b_im': 'new_m', 'new_m_ssm_c_re': 'new_m', 'new_m_ssm_c_im': 'new_m', 'new_m_ssm_d': 'new_m', 'new_m_glu_w': 'new_m', 'new_m_glu_b': 'new_m', 'new_m_w_branch_pool': 'new_m', 'new_m_w_branch_ssm': 'new_m', 'new_m_w_out': 'new_m', 'new_v_w_ada': 'new_v', 'new_v_b_ada': 'new_v', 'new_v_norm_pre': 'new_v', 'new_v_norm_post': 'new_v', 'new_v_w_in': 'new_v', 'new_v_pool_w': 'new_v', 'new_v_pool_scale': 'new_v', 'new_v_ssm_a_re': 'new_v', 'new_v_ssm_a_im': 'new_v', 'new_v_ssm_log_dt': 'new_v', 'new_v_ssm_b_re': 'new_v', 'new_v_ssm_b_im': 'new_v', 'new_v_ssm_c_re': 'new_v', 'new_v_ssm_c_im': 'new_v', 'new_v_ssm_d': 'new_v', 'new_v_glu_w': 'new_v', 'new_v_glu_b': 'new_v', 'new_v_w_branch_pool': 'new_v', 'new_v_w_branch_ssm': 'new_v', 'new_v_w_out': 'new_v'}


def _forward(args):
    return _fwd_reference(*[args[k] for k in FWD_PARAMS])


def _output_shape():
    def fwd():
        inp = _fwd_setup_inputs(0)
        return _fwd_reference(*[inp[k] for k in FWD_PARAMS])
    out = _jax.eval_shape(fwd)
    return out.shape, out.dtype

N_MICROBATCH = 1
ADAM_LR = 0.001
ADAM_B1 = 0.9
ADAM_B2 = 0.999
ADAM_EPS = 1e-08
ADAM_WD = 0.01
ADAM_STEP = 10
PER_EXAMPLE_BATCH_AXIS = {'x': 0, 'c': 0, 'loss_target': 0}
SHARED_INPUTS = []
_WEIGHT_DTYPES = {'w_ada': _jnp.float32, 'b_ada': _jnp.float32, 'norm_pre': _jnp.float32, 'norm_post': _jnp.float32, 'w_in': _jnp.float32, 'pool_w': _jnp.float32, 'pool_scale': _jnp.float32, 'ssm_a_re': _jnp.float32, 'ssm_a_im': _jnp.float32, 'ssm_log_dt': _jnp.float32, 'ssm_b_re': _jnp.float32, 'ssm_b_im': _jnp.float32, 'ssm_c_re': _jnp.float32, 'ssm_c_im': _jnp.float32, 'ssm_d': _jnp.float32, 'glu_w': _jnp.float32, 'glu_b': _jnp.float32, 'w_branch_pool': _jnp.float32, 'w_branch_ssm': _jnp.float32, 'w_out': _jnp.float32}
MOMENT_SCALE = {'w_ada': 2.402460e+00, 'b_ada': 5.049299e+00, 'norm_pre': 1.737805e-01, 'norm_post': 6.663476e+00, 'w_in': 7.227453e-02, 'pool_w': 1.201788e-01, 'pool_scale': 1.239828e-01, 'ssm_a_re': 6.197239e-03, 'ssm_a_im': 5.079427e-03, 'ssm_log_dt': 3.932012e+00, 'ssm_b_re': 2.806893e-03, 'ssm_b_im': 2.589648e-03, 'ssm_c_re': 3.682444e-03, 'ssm_c_im': 3.739392e-03, 'ssm_d': 4.853146e-02, 'glu_w': 3.701373e-03, 'glu_b': 1.244258e-02, 'w_branch_pool': 1.301209e-01, 'w_branch_ssm': 2.405557e-02, 'w_out': 1.452312e-01}


def _to_microbatches(a, axis):
    t = _jnp.moveaxis(a, axis, 0)
    t = t.reshape((N_MICROBATCH, t.shape[0] // N_MICROBATCH) + t.shape[1:])
    return _jnp.moveaxis(t, 1, axis + 1)


def setup_inputs(seed: int = 0) -> dict:
    inp = _fwd_setup_inputs(seed)
    key = _jax.random.fold_in(_jax.random.key(seed), 7919)
    shape, _ = _output_shape()
    out = dict(inp)
    out["loss_target"] = _jax.random.normal(_jax.random.fold_in(key, 0), shape, _jnp.float32)
    for i, name in enumerate(TWIN_WEIGHTS):
        w = inp[name].astype(_jnp.float32)
        if MOMENT_SCALE is None:
            s = _jnp.sqrt(_jnp.mean(_jnp.square(w)) + 1e-30)
        else:
            s = MOMENT_SCALE[name]
        km, kv = _jax.random.split(_jax.random.fold_in(key, i + 1))
        out[name] = w
        out["m_" + name] = s * _jax.random.normal(km, w.shape, _jnp.float32)
        out["v_" + name] = (s * s) * _jax.random.uniform(kv, w.shape, _jnp.float32, 0.5, 1.5)
    if N_MICROBATCH > 1:
        for name, axis in PER_EXAMPLE_BATCH_AXIS.items():
            out[name] = _to_microbatches(out[name], axis)
    return {'x': out['x'], 'c': out['c'], 'w_ada': out['w_ada'], 'b_ada': out['b_ada'], 'norm_pre': out['norm_pre'], 'norm_post': out['norm_post'], 'w_in': out['w_in'], 'pool_w': out['pool_w'], 'pool_scale': out['pool_scale'], 'ssm_a_re': out['ssm_a_re'], 'ssm_a_im': out['ssm_a_im'], 'ssm_log_dt': out['ssm_log_dt'], 'ssm_b_re': out['ssm_b_re'], 'ssm_b_im': out['ssm_b_im'], 'ssm_c_re': out['ssm_c_re'], 'ssm_c_im': out['ssm_c_im'], 'ssm_d': out['ssm_d'], 'glu_w': out['glu_w'], 'glu_b': out['glu_b'], 'w_branch_pool': out['w_branch_pool'], 'w_branch_ssm': out['w_branch_ssm'], 'w_out': out['w_out'], 'loss_target': out['loss_target'], 'm_w_ada': out['m_w_ada'], 'm_b_ada': out['m_b_ada'], 'm_norm_pre': out['m_norm_pre'], 'm_norm_post': out['m_norm_post'], 'm_w_in': out['m_w_in'], 'm_pool_w': out['m_pool_w'], 'm_pool_scale': out['m_pool_scale'], 'm_ssm_a_re': out['m_ssm_a_re'], 'm_ssm_a_im': out['m_ssm_a_im'], 'm_ssm_log_dt': out['m_ssm_log_dt'], 'm_ssm_b_re': out['m_ssm_b_re'], 'm_ssm_b_im': out['m_ssm_b_im'], 'm_ssm_c_re': out['m_ssm_c_re'], 'm_ssm_c_im': out['m_ssm_c_im'], 'm_ssm_d': out['m_ssm_d'], 'm_glu_w': out['m_glu_w'], 'm_glu_b': out['m_glu_b'], 'm_w_branch_pool': out['m_w_branch_pool'], 'm_w_branch_ssm': out['m_w_branch_ssm'], 'm_w_out': out['m_w_out'], 'v_w_ada': out['v_w_ada'], 'v_b_ada': out['v_b_ada'], 'v_norm_pre': out['v_norm_pre'], 'v_norm_post': out['v_norm_post'], 'v_w_in': out['v_w_in'], 'v_pool_w': out['v_pool_w'], 'v_pool_scale': out['v_pool_scale'], 'v_ssm_a_re': out['v_ssm_a_re'], 'v_ssm_a_im': out['v_ssm_a_im'], 'v_ssm_log_dt': out['v_ssm_log_dt'], 'v_ssm_b_re': out['v_ssm_b_re'], 'v_ssm_b_im': out['v_ssm_b_im'], 'v_ssm_c_re': out['v_ssm_c_re'], 'v_ssm_c_im': out['v_ssm_c_im'], 'v_ssm_d': out['v_ssm_d'], 'v_glu_w': out['v_glu_w'], 'v_glu_b': out['v_glu_b'], 'v_w_branch_pool': out['v_w_branch_pool'], 'v_w_branch_ssm': out['v_w_branch_ssm'], 'v_w_out': out['v_w_out']}


def _loss(weights, diff, rest, loss_target):
    with _jax.named_scope("forward"):
        args = {**rest, TWIN_DIFF_INPUT: diff, **{k: w.astype(_WEIGHT_DTYPES[k]) for k, w in weights.items()}}
        y = _forward(args)
    with _jax.named_scope("loss_head"):
        err = _jnp.square(y.astype(_jnp.float32) - loss_target)
        return 0.5 * _jnp.sum(_jnp.mean(err, axis=-1)) if err.ndim else 0.5 * err


def _adamw(w, g, m, v):
    m = ADAM_B1 * m + (1.0 - ADAM_B1) * g
    v = ADAM_B2 * v + (1.0 - ADAM_B2) * _jnp.square(g)
    m_hat = m / (1.0 - ADAM_B1 ** ADAM_STEP)
    v_hat = v / (1.0 - ADAM_B2 ** ADAM_STEP)
    delta = -ADAM_LR * (m_hat / (_jnp.sqrt(v_hat) + ADAM_EPS) + ADAM_WD * w)
    return delta, m, v


def reference(x, c, w_ada, b_ada, norm_pre, norm_post, w_in, pool_w, pool_scale, ssm_a_re, ssm_a_im, ssm_log_dt, ssm_b_re, ssm_b_im, ssm_c_re, ssm_c_im, ssm_d, glu_w, glu_b, w_branch_pool, w_branch_ssm, w_out, loss_target, m_w_ada, m_b_ada, m_norm_pre, m_norm_post, m_w_in, m_pool_w, m_pool_scale, m_ssm_a_re, m_ssm_a_im, m_ssm_log_dt, m_ssm_b_re, m_ssm_b_im, m_ssm_c_re, m_ssm_c_im, m_ssm_d, m_glu_w, m_glu_b, m_w_branch_pool, m_w_branch_ssm, m_w_out, v_w_ada, v_b_ada, v_norm_pre, v_norm_post, v_w_in, v_pool_w, v_pool_scale, v_ssm_a_re, v_ssm_a_im, v_ssm_log_dt, v_ssm_b_re, v_ssm_b_im, v_ssm_c_re, v_ssm_c_im, v_ssm_d, v_glu_w, v_glu_b, v_w_branch_pool, v_w_branch_ssm, v_w_out):
    given = dict(x=x, c=c, w_ada=w_ada, b_ada=b_ada, norm_pre=norm_pre, norm_post=norm_post, w_in=w_in, pool_w=pool_w, pool_scale=pool_scale, ssm_a_re=ssm_a_re, ssm_a_im=ssm_a_im, ssm_log_dt=ssm_log_dt, ssm_b_re=ssm_b_re, ssm_b_im=ssm_b_im, ssm_c_re=ssm_c_re, ssm_c_im=ssm_c_im, ssm_d=ssm_d, glu_w=glu_w, glu_b=glu_b, w_branch_pool=w_branch_pool, w_branch_ssm=w_branch_ssm, w_out=w_out, loss_target=loss_target, m_w_ada=m_w_ada, m_b_ada=m_b_ada, m_norm_pre=m_norm_pre, m_norm_post=m_norm_post, m_w_in=m_w_in, m_pool_w=m_pool_w, m_pool_scale=m_pool_scale, m_ssm_a_re=m_ssm_a_re, m_ssm_a_im=m_ssm_a_im, m_ssm_log_dt=m_ssm_log_dt, m_ssm_b_re=m_ssm_b_re, m_ssm_b_im=m_ssm_b_im, m_ssm_c_re=m_ssm_c_re, m_ssm_c_im=m_ssm_c_im, m_ssm_d=m_ssm_d, m_glu_w=m_glu_w, m_glu_b=m_glu_b, m_w_branch_pool=m_w_branch_pool, m_w_branch_ssm=m_w_branch_ssm, m_w_out=m_w_out, v_w_ada=v_w_ada, v_b_ada=v_b_ada, v_norm_pre=v_norm_pre, v_norm_post=v_norm_post, v_w_in=v_w_in, v_pool_w=v_pool_w, v_pool_scale=v_pool_scale, v_ssm_a_re=v_ssm_a_re, v_ssm_a_im=v_ssm_a_im, v_ssm_log_dt=v_ssm_log_dt, v_ssm_b_re=v_ssm_b_re, v_ssm_b_im=v_ssm_b_im, v_ssm_c_re=v_ssm_c_re, v_ssm_c_im=v_ssm_c_im, v_ssm_d=v_ssm_d, v_glu_w=v_glu_w, v_glu_b=v_glu_b, v_w_branch_pool=v_w_branch_pool, v_w_branch_ssm=v_w_branch_ssm, v_w_out=v_w_out)
    weights = {n: given[n] for n in TWIN_WEIGHTS}
    shared = {n: given[n] for n in SHARED_INPUTS}
    per_example = {n: given[n] for n in ['x', 'c']}
    grad_fn = _jax.value_and_grad(_loss, argnums=(0, 1))

    def one_microbatch(ex, loss_target):
        ex = dict(ex)
        diff = ex.pop(TWIN_DIFF_INPUT)
        return grad_fn(weights, diff, {**shared, **ex}, loss_target)

    if N_MICROBATCH == 1:
        loss, (grad_w, grad_x) = one_microbatch(per_example, given["loss_target"])
    else:
        def body(carry, xs):
            loss_sum, grad_sum = carry
            l_k, (gw_k, gx_k) = one_microbatch(xs[0], xs[1])
            with _jax.named_scope("update"):
                return (loss_sum + l_k, _jax.tree.map(_jnp.add, grad_sum, gw_k)), gx_k

        init = (_jnp.zeros((), _jnp.float32), _jax.tree.map(_jnp.zeros_like, weights))
        (loss, grad_w), grad_x = _jax.lax.scan(body, init, (per_example, given["loss_target"]))
    with _jax.named_scope("update"):
        delta_w, new_m, new_v = {}, {}, {}
        for n in TWIN_WEIGHTS:
            delta_w[n], new_m[n], new_v[n] = _adamw(weights[n], grad_w[n], given["m_" + n], given["v_" + n])
    return (loss, grad_x, *[grad_w[n] for n in TWIN_WEIGHTS], *[delta_w[n] for n in TWIN_WEIGHTS],
            *[new_m[n] for n in TWIN_WEIGHTS], *[new_v[n] for n in TWIN_WEIGHTS])
```

```python
import functools
import math

import jax
import jax.numpy as jnp
from jax import lax
from jax.experimental import pallas as pl
from jax.experimental.pallas import tpu as pltpu

F32 = jnp.float32
BF16 = jnp.bfloat16
MESH = pl.DeviceIdType.MESH

D = 1024
N_DEV = 8
N_IN = 6 * D
GROUPS = 64
G_H = 16
G_P = 64
N_Q = 4
Q_W = 2 * 16 * G_P
N_STATE = N_Q * Q_W
POOL_WINDOWS = (2, 4, 8, 16)
HALO = 16
RMS_EPS = 1e-6
SUBLANES = 8
LANE_CHUNK = 512
VMEM_LIMIT = 56 * 1024 * 1024

ADAM_LR = 0.001
ADAM_B1 = 0.9
ADAM_B2 = 0.999
ADAM_EPS = 1e-08
ADAM_WD = 0.01
ADAM_STEP = 10

WEIGHTS = ['w_ada', 'b_ada', 'norm_pre', 'norm_post', 'w_in', 'pool_w', 'pool_scale', 'ssm_a_re',
           'ssm_a_im', 'ssm_log_dt', 'ssm_b_re', 'ssm_b_im', 'ssm_c_re', 'ssm_c_im', 'ssm_d', 'glu_w',
           'glu_b', 'w_branch_pool', 'w_branch_ssm', 'w_out']


def _pcall(body, **kw):
    return pl.pallas_call(body, **kw)


def _params(sem=None, vmem=VMEM_LIMIT):
    return pltpu.CompilerParams(dimension_semantics=sem, vmem_limit_bytes=vmem)


def _tb(rows, pref):
    return pref if rows % pref == 0 and rows // pref >= 2 else rows // 2


def _full(shape, single=False):
    nd = len(shape)
    if single:
        return pl.BlockSpec(shape, lambda i: (0,) * nd, pipeline_mode=pl.Buffered(1))
    return pl.BlockSpec(shape, lambda i: (0,) * nd)


ANY = pl.BlockSpec(memory_space=pl.ANY)


def _me():
    return lax.axis_index("x"), lax.axis_index("y"), lax.axis_index("c")


def _flat(p):
    return 4 * p[0] + 2 * p[1] + p[2]


def _peer(k):
    x, y, c = _me()
    return (1 - x if k & 4 else x, 1 - y if k & 2 else y, 1 - c if k & 1 else c)


def _silu_parts(z):
    s = jax.nn.sigmoid(z)
    return z * s, s * (1.0 + z * (1.0 - s))


_GELU_C = math.sqrt(2.0 / math.pi)


def _gelu_parts(x):
    x2 = x * x
    t = jnp.tanh(_GELU_C * (x + 0.044715 * x * x2))
    g = 0.5 * x * (1.0 + t)
    dg = 0.5 * (1.0 + t) + 0.5 * x * (1.0 - t * t) * (_GELU_C * (1.0 + 3.0 * 0.044715 * x2))
    return g, dg


def _dot(a, b):
    return jnp.dot(a, b, preferred_element_type=F32)


def _dot_nt(a, b):
    return lax.dot_general(a, b, (((1,), (1,)), ((), ())), preferred_element_type=F32)


def _dot_tn(a, b):
    return lax.dot_general(a, b, (((0,), (0,)), ((), ())), preferred_element_type=F32)


def _rms_parts(x):
    r = lax.rsqrt(jnp.mean(x * x, axis=-1, keepdims=True) + RMS_EPS)
    return x * r, r


def _rms_bwd(dxn, xn, r):
    return r * (dxn - xn * jnp.mean(dxn * xn, axis=-1, keepdims=True))


def _ada_exchange(c, w_ada_s, b_ada_s):
    cols = w_ada_s.shape[1]

    def body(c_ref, w_ref, b_ref, call_ref, mod_ref, part_ref, ssem, rsem, lsem):
        me3 = _me()
        me = _flat(me3)
        mine = pltpu.make_async_copy(c_ref, call_ref.at[pl.ds(me, 1), :], lsem.at[0])
        mine.start()
        sends = []
        for k in range(1, N_DEV):
            cp = pltpu.make_async_remote_copy(src_ref=c_ref, dst_ref=call_ref.at[pl.ds(me, 1), :],
                                              send_sem=ssem.at[k - 1], recv_sem=rsem.at[k - 1],
                                              device_id=_peer(k), device_id_type=MESH)
            cp.start()
            sends.append(cp)
        mine.wait()
        for k in range(1, N_DEV):
            p = _flat(_peer(k))
            pltpu.make_async_remote_copy(src_ref=c_ref, dst_ref=call_ref.at[pl.ds(p, 1), :],
                                         send_sem=ssem.at[k - 1], recv_sem=rsem.at[k - 1],
                                         device_id=_peer(k), device_id_type=MESH).wait_recv()
        for cp in sends:
            cp.wait_send()
        ca = call_ref[...]
        act = ca * jax.nn.sigmoid(ca)
        part_ref[...] = jnp.dot(act, w_ref[...], preferred_element_type=F32,
                                precision=lax.Precision.HIGHEST) + b_ref[...]
        own = pltpu.make_async_copy(part_ref.at[pl.ds(me, 1), :], mod_ref.at[pl.ds(me, 1), :], lsem.at[1])
        own.start()
        sends = []
        for k in range(1, N_DEV):
            p = _flat(_peer(k))
            s = N_DEV - 1 + k - 1
            cp = pltpu.make_async_remote_copy(src_ref=part_ref.at[pl.ds(p, 1), :],
                                              dst_ref=mod_ref.at[pl.ds(me, 1), :],
                                              send_sem=ssem.at[s], recv_sem=rsem.at[s],
                                              device_id=_peer(k), device_id_type=MESH)
            cp.start()
            sends.append(cp)
        own.wait()
        for k in range(1, N_DEV):
            p = _flat(_peer(k))
            s = N_DEV - 1 + k - 1
            pltpu.make_async_remote_copy(src_ref=part_ref.at[pl.ds(p, 1), :],
                                         dst_ref=mod_ref.at[pl.ds(p, 1), :],
                                         send_sem=ssem.at[s], recv_sem=rsem.at[s],
                                         device_id=_peer(k), device_id_type=MESH).wait_recv()
        for cp in sends:
            cp.wait_send()

    vm = pl.BlockSpec(memory_space=pltpu.VMEM)
    return _pcall(
        body, name="ada_exchange",
        out_shape=(jax.ShapeDtypeStruct((N_DEV, D), F32), jax.ShapeDtypeStruct((N_DEV, cols), F32)),
        in_specs=[vm, vm, vm], out_specs=(vm, vm),
        scratch_shapes=[pltpu.VMEM((N_DEV, cols), F32),
                        pltpu.SemaphoreType.DMA((2 * (N_DEV - 1),)),
                        pltpu.SemaphoreType.DMA((2 * (N_DEV - 1),)),
                        pltpu.SemaphoreType.DMA((2,))],
    )(c, w_ada_s, b_ada_s)


def _exchange(name, srcs, out_structs, src_views, dst_views):
    n = len(srcs)

    def body(*refs):
        src_refs, out_refs = refs[:n], refs[n:2 * n]
        ssem, rsem, lsem = refs[2 * n:]
        me = _flat(_me())
        local = []
        for t in range(n):
            cp = pltpu.make_async_copy(src_views[t](src_refs[t], me), dst_views[t](out_refs[t], me), lsem.at[t])
            cp.start()
            local.append(cp)
        sends = []
        for k in range(1, N_DEV):
            p3 = _peer(k)
            p = _flat(p3)
            for t in range(n):
                s = t * (N_DEV - 1) + k - 1
                cp = pltpu.make_async_remote_copy(src_ref=src_views[t](src_refs[t], p),
                                                  dst_ref=dst_views[t](out_refs[t], me),
                                                  send_sem=ssem.at[s], recv_sem=rsem.at[s],
                                                  device_id=p3, device_id_type=MESH)
                cp.start()
                sends.append(cp)
        for cp in local:
            cp.wait()
        for k in range(1, N_DEV):
            p3 = _peer(k)
            p = _flat(p3)
            for t in range(n):
                s = t * (N_DEV - 1) + k - 1
                pltpu.make_async_remote_copy(src_ref=src_views[t](src_refs[t], p),
                                             dst_ref=dst_views[t](out_refs[t], p),
                                             send_sem=ssem.at[s], recv_sem=rsem.at[s],
                                             device_id=p3, device_id_type=MESH).wait_recv()
        for cp in sends:
            cp.wait_send()

    return _pcall(
        body, name=name, out_shape=tuple(out_structs),
        in_specs=[ANY] * n, out_specs=tuple([ANY] * n),
        scratch_shapes=[pltpu.SemaphoreType.DMA((n * (N_DEV - 1),)),
                        pltpu.SemaphoreType.DMA((n * (N_DEV - 1),)),
                        pltpu.SemaphoreType.DMA((n,))],
    )(*srcs)


def _cast_shards(arrs):
    def body(*refs):
        n = len(refs) // 2
        for i in range(n):
            refs[n + i][...] = refs[i][...].astype(BF16)

    vm = pl.BlockSpec(memory_space=pltpu.VMEM)
    return _pcall(body, name="cast_shards",
                  out_shape=tuple(jax.ShapeDtypeStruct(a.shape, BF16) for a in arrs),
                  in_specs=[vm] * len(arrs), out_specs=tuple([vm] * len(arrs)),
                  compiler_params=_params())(*arrs)


def _gather_weights(w_in_s, pool_w_s, glu_s, wbp_s, wbs_s, wout_s):
    rows = glu_s.shape[0]
    cin = w_in_s.shape[1]
    pr = pool_w_s.shape[1]
    cols_view = lambda ref, p: ref.at[:, pl.ds(p * cin, cin)]
    rows_view = lambda ref, p: ref.at[pl.ds(p * rows, rows), :]
    pool_view = lambda ref, p: ref.at[:, pl.ds(p * pr, pr), :]
    whole = lambda ref, p: ref
    outs = [jax.ShapeDtypeStruct((D, N_IN), BF16), jax.ShapeDtypeStruct((4, 256, 256), BF16)] + \
           [jax.ShapeDtypeStruct((D, D), BF16)] * 4
    return _exchange("gather_weights", [w_in_s, pool_w_s, glu_s, wbp_s, wbs_s, wout_s], outs,
                     [whole] * 6, [cols_view, pool_view, rows_view, rows_view, rows_view, rows_view])


def _scatter_grads(dw_in, dpool_w, dglu, dwbp, dwbs, dwout, small):
    cin = N_IN // N_DEV
    rows = D // N_DEV
    pr = 256 // N_DEV
    cols_view = lambda ref, p: ref.at[:, pl.ds(p * cin, cin)]
    rows_view = lambda ref, p: ref.at[pl.ds(p * rows, rows), :]
    pool_view = lambda ref, p: ref.at[:, pl.ds(p * pr, pr), :]
    whole = lambda ref, p: ref
    slot = lambda ref, p: ref.at[p]
    outs = [jax.ShapeDtypeStruct((N_DEV, D, cin), F32), jax.ShapeDtypeStruct((N_DEV, 4, pr, 256), F32)] + \
           [jax.ShapeDtypeStruct((N_DEV, rows, D), F32)] * 4 + \
           [jax.ShapeDtypeStruct((N_DEV,) + small.shape, F32)]
    return _exchange("scatter_grads", [dw_in, dpool_w, dglu, dwbp, dwbs, dwout, small], outs,
                     [cols_view, pool_view, rows_view, rows_view, rows_view, rows_view, whole], [slot] * 7)


def _s5_discretise(a_re, a_im, log_dt, b_re_t, b_im_t):
    dt = jnp.exp(log_dt)
    lam_re = jnp.minimum(a_re, -1e-4)
    lam_im = a_im
    mag = jnp.exp(lam_re * dt)
    abar_re = mag * jnp.cos(lam_im * dt)
    abar_im = mag * jnp.sin(lam_im * dt)
    den = lam_re * lam_re + lam_im * lam_im
    num_re = abar_re - 1.0
    f_re = (num_re * lam_re + abar_im * lam_im) / den
    f_im = (abar_im * lam_re - num_re * lam_im) / den
    f_re, f_im = f_re[:, None, :], f_im[:, None, :]
    bb_re = f_re * b_re_t - f_im * b_im_t
    bb_im = f_re * b_im_t + f_im * b_re_t
    return abar_re, abar_im, bb_re, bb_im


def _s5_prep(a_re, a_im, log_dt, b_re_t, b_im_t, n_pow):
    def body(ar_ref, ai_ref, ld_ref, br_ref, bi_ref, bbr_ref, bbi_ref, pr_ref, pi_ref):
        abar_re, abar_im, bb_re, bb_im = _s5_discretise(ar_ref[...], ai_ref[...], ld_ref[...], br_ref[...], bi_ref[...])
        bbr_ref[...] = bb_re
        bbi_ref[...] = bb_im
        p_re, p_im = abar_re, abar_im
        pr_ref[0] = p_re
        pi_ref[0] = p_im
        for k in range(1, n_pow):
            p_re, p_im = p_re * abar_re - p_im * abar_im, p_re * abar_im + p_im * abar_re
            pr_ref[k] = p_re
            pi_ref[k] = p_im

    vm = pl.BlockSpec(memory_space=pltpu.VMEM)
    return _pcall(body, name="s5_prep",
                  out_shape=(jax.ShapeDtypeStruct(b_re_t.shape, F32), jax.ShapeDtypeStruct(b_re_t.shape, F32),
                             jax.ShapeDtypeStruct((n_pow, GROUPS, G_P), F32),
                             jax.ShapeDtypeStruct((n_pow, GROUPS, G_P), F32)),
                  in_specs=[vm] * 5, out_specs=(vm, vm, vm, vm), compiler_params=_params(),
                  )(a_re, a_im, log_dt, b_re_t, b_im_t)


def _s5_prep_bwd(a_re, a_im, log_dt, b_re_t, b_im_t, d_abar_re, d_abar_im, d_bb_re, d_bb_im):
    def body(ar_ref, ai_ref, ld_ref, br_ref, bi_ref, dar_ref, dai_ref, dbr_ref, dbi_ref,
             gar_ref, gai_ref, gld_ref, gbr_ref, gbi_ref):
        _, vjp = jax.vjp(_s5_discretise, ar_ref[...], ai_ref[...], ld_ref[...], br_ref[...], bi_ref[...])
        g = vjp((dar_ref[...], dai_ref[...], dbr_ref[...], dbi_ref[...]))
        gar_ref[...] = g[0]
        gai_ref[...] = g[1]
        gld_ref[...] = g[2]
        gbr_ref[...] = g[3]
        gbi_ref[...] = g[4]

    vm = pl.BlockSpec(memory_space=pltpu.VMEM)
    ins = (a_re, a_im, log_dt, b_re_t, b_im_t)
    return _pcall(body, name="s5_prep_bwd",
                  out_shape=tuple(jax.ShapeDtypeStruct(a.shape, F32) for a in ins),
                  in_specs=[vm] * 9, out_specs=tuple([vm] * 5), compiler_params=_params(),
                  )(*ins, d_abar_re, d_abar_im, d_bb_re, d_bb_im)


def _blockdiag(t):
    t4 = t.reshape(N_Q, 16, G_H, G_P)
    eye = jnp.eye(16, dtype=t.dtype)
    return (t4[:, :, :, None, :] * eye[None, :, None, :, None]).reshape(N_Q, 16 * G_H, 16 * G_P)


def _diag_blocks(m):
    m5 = m.reshape(N_Q, 16, G_H, 16, G_P)
    eye = jnp.eye(16, dtype=m.dtype)
    return (m5 * eye[None, :, None, :, None]).sum(axis=3).reshape(GROUPS, G_H, G_P)


def _state_layout(re, im):
    lead = re.shape[:-2]
    r = re.reshape(lead + (N_Q, 1, 16 * G_P))
    i = im.reshape(lead + (N_Q, 1, 16 * G_P))
    return jnp.concatenate([r, i], axis=-2).reshape(lead + (N_STATE,))


def _state_unlayout(v):
    v4 = v.reshape(N_Q, 2, 16, G_P)
    return v4[:, 0].reshape(GROUPS, G_P), v4[:, 1].reshape(GROUPS, G_P)


def _perm_matrix(tb):
    k_steps = tb // SUBLANES
    r = jnp.arange(tb)
    src = (r % SUBLANES) * k_steps + r // SUBLANES
    return (src[:, None] == jnp.arange(tb)[None, :]).astype(BF16)


def _lane_chunks():
    for q in range(N_Q):
        for lc in range(Q_W // 2 // LANE_CHUNK):
            re = q * Q_W + lc * LANE_CHUNK
            yield re, re + Q_W // 2


def _tile(k):
    return pl.ds(pl.multiple_of(k * SUBLANES, SUBLANES), SUBLANES)


def _scan_forward(s_ref, p_ref, carry_ref, enter_ref, fin_ref, k_steps):
    for re, im in _lane_chunks():
        lr, li = pl.ds(re, LANE_CHUNK), pl.ds(im, LANE_CHUNK)
        a_re = jnp.broadcast_to(p_ref[0:1, lr], (SUBLANES, LANE_CHUNK))
        a_im = jnp.broadcast_to(p_ref[0:1, li], (SUBLANES, LANE_CHUNK))

        def local(k, st):
            sr, si = st
            rows = _tile(k)
            nr = a_re * sr - a_im * si + s_ref[rows, lr]
            ni = a_re * si + a_im * sr + s_ref[rows, li]
            s_ref[rows, lr] = nr
            s_ref[rows, li] = ni
            return nr, ni

        zero = jnp.zeros((SUBLANES, LANE_CHUNK), F32)
        fr, fi = lax.fori_loop(0, k_steps, local, (zero, zero))
        fin_ref[:, lr] = fr
        fin_ref[:, li] = fi
        ak_re, ak_im = p_ref[k_steps - 1:k_steps, lr], p_ref[k_steps - 1:k_steps, li]
        c_re, c_im = carry_ref[:, lr], carry_ref[:, li]
        for seg in range(SUBLANES):
            enter_ref[seg:seg + 1, lr] = c_re
            enter_ref[seg:seg + 1, li] = c_im
            f_re, f_im = fin_ref[seg:seg + 1, lr], fin_ref[seg:seg + 1, li]
            c_re, c_im = f_re + ak_re * c_re - ak_im * c_im, f_im + ak_re * c_im + ak_im * c_re
        carry_ref[:, lr] = c_re
        carry_ref[:, li] = c_im
        e_re, e_im = enter_ref[:, lr], enter_ref[:, li]

        def fix(k, _):
            rows = _tile(k)
            p_re = p_ref[pl.ds(k, 1), lr]
            p_im = p_ref[pl.ds(k, 1), li]
            s_ref[rows, lr] = s_ref[rows, lr] + (p_re * e_re - p_im * e_im)
            s_ref[rows, li] = s_ref[rows, li] + (p_re * e_im + p_im * e_re)
            return 0

        lax.fori_loop(0, k_steps, fix, 0)


def _scan_backward(g_ref, s_ref, p_ref, carry_ref, enter_ref, fin_ref, da_ref, k_steps):
    for re, im in _lane_chunks():
        lr, li = pl.ds(re, LANE_CHUNK), pl.ds(im, LANE_CHUNK)
        a_re = jnp.broadcast_to(p_ref[0:1, lr], (SUBLANES, LANE_CHUNK))
        a_im = jnp.broadcast_to(p_ref[0:1, li], (SUBLANES, LANE_CHUNK))

        def local(j, st):
            sr, si = st
            rows = _tile(k_steps - 1 - j)
            nr = a_re * sr + a_im * si + g_ref[rows, lr]
            ni = a_re * si - a_im * sr + g_ref[rows, li]
            g_ref[rows, lr] = nr
            g_ref[rows, li] = ni
            return nr, ni

        zero = jnp.zeros((SUBLANES, LANE_CHUNK), F32)
        fr, fi = lax.fori_loop(0, k_steps, local, (zero, zero))
        fin_ref[:, lr] = fr
        fin_ref[:, li] = fi
        ak_re, ak_im = p_ref[k_steps - 1:k_steps, lr], p_ref[k_steps - 1:k_steps, li]
        c_re, c_im = carry_ref[:, lr], carry_ref[:, li]
        lam_in = [None] * SUBLANES
        for seg in reversed(range(SUBLANES)):
            lam_in[seg] = (c_re, c_im)
            f_re, f_im = fin_ref[seg:seg + 1, lr], fin_ref[seg:seg + 1, li]
            c_re, c_im = f_re + ak_re * c_re + ak_im * c_im, f_im + ak_re * c_im - ak_im * c_re
        carry_ref[:, lr] = c_re
        carry_ref[:, li] = c_im
        for seg in range(SUBLANES):
            fin_ref[seg:seg + 1, lr] = lam_in[seg][0]
            fin_ref[seg:seg + 1, li] = lam_in[seg][1]
        e_re, e_im = fin_ref[:, lr], fin_ref[:, li]

        def fix(k, acc):
            acc_re, acc_im = acc
            rows = _tile(k)
            p_re = p_ref[pl.ds(k_steps - 1 - k, 1), lr]
            p_im = p_ref[pl.ds(k_steps - 1 - k, 1), li]
            l_re = g_ref[rows, lr] + (p_re * e_re + p_im * e_im)
            l_im = g_ref[rows, li] + (p_re * e_im - p_im * e_re)
            g_ref[rows, lr] = l_re
            g_ref[rows, li] = l_im
            prev = _tile(jnp.maximum(k - 1, 0))
            first = k == 0
            sp_re = jnp.where(first, enter_ref[:, lr], s_ref[prev, lr])
            sp_im = jnp.where(first, enter_ref[:, li], s_ref[prev, li])
            return acc_re + (l_re * sp_re + l_im * sp_im), acc_im + (l_im * sp_re - l_re * sp_im)

        acc_re, acc_im = lax.fori_loop(0, k_steps, fix, (zero, zero))
        da_ref[:, lr] = da_ref[:, lr] + jnp.sum(acc_re, axis=0, keepdims=True)
        da_ref[:, li] = da_ref[:, li] + jnp.sum(acc_im, axis=0, keepdims=True)


def _prenorm(x, mod3, norm_pre):
    xn, r = _rms_parts(x)
    return xn, r, xn * norm_pre * (1.0 + mod3[1:2, :]) + mod3[0:1, :]


def _in_proj(x, mod3, norm_pre, w_in):
    rows = x.shape[0]
    tb = _tb(rows, 512)

    def body(x_ref, mod_ref, np_ref, w_ref, proj_ref):
        _, _, h = _prenorm(x_ref[...], mod_ref[...], np_ref[...])
        hb = h.astype(BF16)
        for j in range(N_IN // D):
            cols = slice(j * D, (j + 1) * D)
            proj_ref[:, cols] = _dot(hb, w_ref[:, cols]).astype(BF16)

    return _pcall(body, name="in_proj", grid=(rows // tb,),
                  out_shape=jax.ShapeDtypeStruct((rows, N_IN), BF16),
                  in_specs=[pl.BlockSpec((tb, D), lambda i: (i, 0)), _full((3, D)), _full((1, D)),
                            _full((D, N_IN), single=True)],
                  out_specs=pl.BlockSpec((tb, N_IN), lambda i: (i, 0)),
                  compiler_params=_params(("arbitrary",)))(x, mod3, norm_pre, w_in)


def _pool_windows(ext, tb, first_row):
    pos = (first_row + lax.broadcasted_iota(jnp.int32, (tb, 1), 0) + 1).astype(F32)
    pooled, counts = [], []
    for g, w in enumerate(POOL_WINDOWS):
        acc = ext[:, g * 256:(g + 1) * 256]
        tok = acc[HALO:, :]
        s = 1
        while s < w:
            acc = acc + pltpu.roll(acc, s, axis=0)
            s *= 2
        cnt = jnp.minimum(pos, float(w))
        pooled.append(acc[HALO:, :] / cnt - tok)
        counts.append(cnt)
    return pooled, counts


def _pool_fwd(proj, pool_w, pool_scale):
    rows = proj.shape[0]
    tb = _tb(rows, 512)
    hb = tb // HALO

    def body(u_ref, halo_ref, z_ref, pw_ref, ps_ref, y_ref):
        i = pl.program_id(0)
        u = u_ref[...].astype(F32)
        halo = jnp.where(i > 0, halo_ref[...].astype(F32), 0.0)
        pooled, _ = _pool_windows(jnp.concatenate([halo, u], axis=0), tb, i * tb)
        silu_z, _ = _silu_parts(z_ref[...].astype(F32))
        for g in range(4):
            cols = slice(g * 256, (g + 1) * 256)
            mixed = _dot(pooled[g].astype(BF16), pw_ref[g])
            y_ref[:, cols] = (mixed * ps_ref[:, cols] * silu_z[:, cols]).astype(BF16)

    return _pcall(body, name="pool_fwd", grid=(rows // tb,),
                  out_shape=jax.ShapeDtypeStruct((rows, D), BF16),
                  in_specs=[pl.BlockSpec((tb, D), lambda i: (i, 0)),
                            pl.BlockSpec((HALO, D), lambda i: (jnp.maximum(i * hb - 1, 0), 0)),
                            pl.BlockSpec((tb, D), lambda i: (i, 1)),
                            _full((4, 256, 256)), _full((1, D))],
                  out_specs=pl.BlockSpec((tb, D), lambda i: (i, 0)),
                  compiler_params=_params(("arbitrary",)))(proj, proj, proj, pool_w, pool_scale)


def _ssm_fwd(proj, pm, pmt, wb, wct, ptab, dvec, glu_w, glu_b):
    rows = proj.shape[0]
    tb = pm.shape[0]
    k_steps = tb // SUBLANES
    nblk = rows // tb

    def body(u_ref, z_ref, pm_ref, pmt_ref, wb_ref, wct_ref, p_ref, d_ref, gw_ref, gb_ref,
             y_ref, ys_ref, carry_out_ref, s_ref, carry_ref, enter_ref, fin_ref):
        @pl.when(pl.program_id(0) == 0)
        def _():
            carry_ref[...] = jnp.zeros_like(carry_ref)

        carry_out_ref[...] = carry_ref[...]
        up = _dot(pm_ref[...], u_ref[...]).astype(BF16)
        for q in range(N_Q):
            s_ref[:, q * Q_W:(q + 1) * Q_W] = _dot(up[:, q * 256:(q + 1) * 256], wb_ref[q])
        _scan_forward(s_ref, p_ref, carry_ref, enter_ref, fin_ref, k_steps)
        for q in range(N_Q):
            cols = slice(q * 256, (q + 1) * 256)
            y = _dot_nt(s_ref[:, q * Q_W:(q + 1) * Q_W].astype(BF16), wct_ref[q])
            ys_ref[:, cols] = y + d_ref[:, cols] * up[:, cols].astype(F32)
        yg, _ = _gelu_parts(ys_ref[...])
        gate = jax.nn.sigmoid(_dot(yg.astype(BF16), gw_ref[...]) + gb_ref[...])
        zp = _dot(pm_ref[...], z_ref[...])
        silu_z, _ = _silu_parts(zp)
        y_ref[...] = _dot(pmt_ref[...], (yg * gate * silu_z).astype(BF16)).astype(BF16)

    return _pcall(body, name="ssm_fwd", grid=(nblk,),
                  out_shape=(jax.ShapeDtypeStruct((rows, D), BF16), jax.ShapeDtypeStruct((rows, D), F32),
                             jax.ShapeDtypeStruct((nblk, 1, N_STATE), F32)),
                  in_specs=[pl.BlockSpec((tb, D), lambda i: (i, 2)), pl.BlockSpec((tb, D), lambda i: (i, 3)),
                            _full((tb, tb)), _full((tb, tb)),
                            _full((N_Q, 256, Q_W), single=True), _full((N_Q, 256, Q_W), single=True),
                            _full((k_steps, N_STATE)), _full((1, D)), _full((D, D), single=True), _full((1, D))],
                  out_specs=(pl.BlockSpec((tb, D), lambda i: (i, 0)), pl.BlockSpec((tb, D), lambda i: (i, 0)),
                             pl.BlockSpec((None, 1, N_STATE), lambda i: (i, 0, 0))),
                  scratch_shapes=[pltpu.VMEM((tb, N_STATE), F32), pltpu.VMEM((1, N_STATE), F32),
                                  pltpu.VMEM((SUBLANES, N_STATE), F32), pltpu.VMEM((SUBLANES, N_STATE), F32)],
                  compiler_params=_params(("arbitrary",)))(proj, proj, pm, pmt, wb, wct, ptab, dvec, glu_w, glu_b)


def _head(x, target, proj, y_pool, y_ssm, mod3, norm_post, wbp, wbs, wout):
    rows = x.shape[0]
    tb = _tb(rows, 256)
    nblk = rows // tb
    n_feat = float(D)

    def body(x_ref, t_ref, gp_ref, gs_ref, yp_ref, ys_ref, mod_ref, npost_ref, wbp_ref, wbs_ref, wout_ref,
             loss_ref, dy_ref, dyp_ref, dys_ref, dg_ref, dwbp_hbm, dwbs_hbm, dwout_hbm, vec_ref,
             acc_bp, acc_bs, acc_out, acc_loss, acc_vec):
        i = pl.program_id(0)

        @pl.when(i == 0)
        def _():
            acc_bp[...] = jnp.zeros_like(acc_bp)
            acc_bs[...] = jnp.zeros_like(acc_bs)
            acc_out[...] = jnp.zeros_like(acc_out)
            acc_loss[...] = jnp.zeros_like(acc_loss)
            acc_vec[...] = jnp.zeros_like(acc_vec)

        yp, ys = yp_ref[...], ys_ref[...]
        sgp = jax.nn.sigmoid(gp_ref[...].astype(F32))
        sgs = jax.nn.sigmoid(gs_ref[...].astype(F32))
        pb = _dot(yp, wbp_ref[...])
        psm = _dot(ys, wbs_ref[...])
        mb = (sgp * pb + sgs * psm).astype(BF16)
        out = _dot(mb, wout_ref[...])
        on, r = _rms_parts(out)
        gate = mod_ref[2:3, :]
        npost = npost_ref[...]
        normed = on * npost
        diff = x_ref[...] + gate * normed - t_ref[...]
        acc_loss[...] += jnp.sum(diff * diff, axis=0, keepdims=True)
        dy = diff * (1.0 / n_feat)
        dy_ref[...] = dy
        acc_vec[0:1, :] += jnp.sum(dy * normed, axis=0, keepdims=True)
        dn = dy * gate
        acc_vec[1:2, :] += jnp.sum(dn * on, axis=0, keepdims=True)
        dout = _rms_bwd(dn * npost, on, r).astype(BF16)
        acc_out[...] += _dot_tn(mb, dout)
        dm = _dot_nt(dout, wout_ref[...])
        dpb = (dm * sgp).astype(BF16)
        dps = (dm * sgs).astype(BF16)
        dg_ref[:, :D] = (dm * pb * sgp * (1.0 - sgp)).astype(BF16)
        dg_ref[:, D:] = (dm * psm * sgs * (1.0 - sgs)).astype(BF16)
        acc_bp[...] += _dot_tn(yp, dpb)
        acc_bs[...] += _dot_tn(ys, dps)
        dyp_ref[...] = _dot_nt(dpb, wbp_ref[...]).astype(BF16)
        dys_ref[...] = _dot_nt(dps, wbs_ref[...]).astype(BF16)

        @pl.when(i == nblk - 1)
        def _():
            loss_ref[...] = 0.5 / n_feat * jnp.sum(acc_loss[...], axis=1, keepdims=True)
            vec_ref[...] = acc_vec[...]
            pltpu.sync_copy(acc_bp, dwbp_hbm)
            pltpu.sync_copy(acc_bs, dwbs_hbm)
            pltpu.sync_copy(acc_out, dwout_hbm)

    row = lambda c: pl.BlockSpec((tb, D), lambda i: (i, c))
    w = _full((D, D), single=True)
    return _pcall(body, name="head", grid=(nblk,),
                  out_shape=(jax.ShapeDtypeStruct((1, 1), F32), jax.ShapeDtypeStruct((rows, D), F32),
                             jax.ShapeDtypeStruct((rows, D), BF16), jax.ShapeDtypeStruct((rows, D), BF16),
                             jax.ShapeDtypeStruct((rows, 2 * D), BF16),
                             jax.ShapeDtypeStruct((D, D), F32), jax.ShapeDtypeStruct((D, D), F32),
                             jax.ShapeDtypeStruct((D, D), F32), jax.ShapeDtypeStruct((2, D), F32)),
                  in_specs=[row(0), row(0), row(4), row(5), row(0), row(0), _full((3, D)), _full((1, D)), w, w, w],
                  out_specs=(_full((1, 1)), row(0), row(0), row(0), pl.BlockSpec((tb, 2 * D), lambda i: (i, 0)),
                             ANY, ANY, ANY, _full((2, D))),
                  scratch_shapes=[pltpu.VMEM((D, D), F32), pltpu.VMEM((D, D), F32), pltpu.VMEM((D, D), F32),
                                  pltpu.VMEM((1, D), F32), pltpu.VMEM((2, D), F32)],
                  compiler_params=_params(("arbitrary",)))(x, target, proj, proj, y_pool, y_ssm, mod3, norm_post,
                                                           wbp, wbs, wout)


def _glu_bwd(dys, proj, ys_pre, pm, pmt, glu_w, glu_b):
    rows = dys.shape[0]
    tb = pm.shape[0]
    nblk = rows // tb

    def body(dys_ref, z_ref, ysp_ref, pm_ref, pmt_ref, gw_ref, gb_ref, dyp_ref, dz_ref, dgw_hbm, dgb_ref,
             acc_w, acc_b):
        i = pl.program_id(0)

        @pl.when(i == 0)
        def _():
            acc_w[...] = jnp.zeros_like(acc_w)
            acc_b[...] = jnp.zeros_like(acc_b)

        d_out = _dot(pm_ref[...], dys_ref[...])
        z = _dot(pm_ref[...], z_ref[...])
        yg, dgelu = _gelu_parts(ysp_ref[...])
        ygb = yg.astype(BF16)
        sg = jax.nn.sigmoid(_dot(ygb, gw_ref[...]) + gb_ref[...])
        silu_z, dsilu_z = _silu_parts(z)
        dz = d_out * (yg * sg) * dsilu_z
        dz_ref[...] = _dot(pmt_ref[...], dz.astype(BF16)).astype(BF16)
        dglu = d_out * silu_z
        dq = dglu * yg * sg * (1.0 - sg)
        dqb = dq.astype(BF16)
        acc_b[...] += jnp.sum(dq, axis=0, keepdims=True)
        acc_w[...] += _dot_tn(ygb, dqb)
        dyg = dglu * sg + _dot_nt(dqb, gw_ref[...])
        dyp_ref[...] = (dyg * dgelu).astype(BF16)

        @pl.when(i == nblk - 1)
        def _():
            dgb_ref[...] = acc_b[...]
            pltpu.sync_copy(acc_w, dgw_hbm)

    row = lambda c: pl.BlockSpec((tb, D), lambda i: (i, c))
    return _pcall(body, name="glu_bwd", grid=(nblk,),
                  out_shape=(jax.ShapeDtypeStruct((rows, D), BF16), jax.ShapeDtypeStruct((rows, D), BF16),
                             jax.ShapeDtypeStruct((D, D), F32), jax.ShapeDtypeStruct((1, D), F32)),
                  in_specs=[row(0), row(3), row(0), _full((tb, tb)), _full((tb, tb)),
                            _full((D, D), single=True), _full((1, D))],
                  out_specs=(row(0), row(0), ANY, _full((1, D))),
                  scratch_shapes=[pltpu.VMEM((D, D), F32), pltpu.VMEM((1, D), F32)],
                  compiler_params=_params(("arbitrary",)))(dys, proj, ys_pre, pm, pmt, glu_w, glu_b)


def _ssm_bwd(dyp, proj, carries, pm, pmt, wb, wct, ptab, dvec):
    rows = dyp.shape[0]
    tb = pm.shape[0]
    k_steps = tb // SUBLANES
    nblk = rows // tb

    def body(dyp_ref, u_ref, cin_ref, pm_ref, pmt_ref, wb_ref, wct_ref, p_ref, d_ref,
             du_ref, dwb_hbm, dwct_hbm, da_ref, dd_ref,
             s_ref, g_ref, carry_f, carry_b, enter_ref, fin_ref, acc_wb, acc_wct, acc_da, acc_dd, dup_ref):
        i = pl.program_id(0)

        @pl.when(i == 0)
        def _():
            carry_b[...] = jnp.zeros_like(carry_b)
            acc_wb[...] = jnp.zeros_like(acc_wb)
            acc_wct[...] = jnp.zeros_like(acc_wct)
            acc_da[...] = jnp.zeros_like(acc_da)
            acc_dd[...] = jnp.zeros_like(acc_dd)

        dy = dyp_ref[...]
        up = _dot(pm_ref[...], u_ref[...]).astype(BF16)
        acc_dd[...] += jnp.sum(dy.astype(F32) * up.astype(F32), axis=0, keepdims=True)
        carry_f[...] = cin_ref[...]
        for q in range(N_Q):
            cols = slice(q * 256, (q + 1) * 256)
            s_ref[:, q * Q_W:(q + 1) * Q_W] = _dot(up[:, cols], wb_ref[q])
            g_ref[:, q * Q_W:(q + 1) * Q_W] = _dot(dy[:, cols], wct_ref[q])
        _scan_forward(s_ref, p_ref, carry_f, enter_ref, fin_ref, k_steps)
        for q in range(N_Q):
            acc_wct[q] += _dot_tn(dy[:, q * 256:(q + 1) * 256], s_ref[:, q * Q_W:(q + 1) * Q_W].astype(BF16))
        _scan_backward(g_ref, s_ref, p_ref, carry_b, enter_ref, fin_ref, acc_da, k_steps)
        for q in range(N_Q):
            cols = slice(q * 256, (q + 1) * 256)
            lam = g_ref[:, q * Q_W:(q + 1) * Q_W].astype(BF16)
            acc_wb[q] += _dot_tn(up[:, cols], lam)
            dup_ref[:, cols] = (_dot_nt(lam, wb_ref[q]) + d_ref[:, cols] * dy[:, cols].astype(F32)).astype(BF16)
        du_ref[...] = _dot(pmt_ref[...], dup_ref[...]).astype(BF16)

        @pl.when(i == nblk - 1)
        def _():
            da_ref[...] = acc_da[...]
            dd_ref[...] = acc_dd[...]
            pltpu.sync_copy(acc_wb, dwb_hbm)
            pltpu.sync_copy(acc_wct, dwct_hbm)

    rev = lambda c: pl.BlockSpec((tb, D), lambda i: (nblk - 1 - i, c))
    return _pcall(body, name="ssm_bwd", grid=(nblk,),
                  out_shape=(jax.ShapeDtypeStruct((rows, D), BF16),
                             jax.ShapeDtypeStruct((N_Q, 256, Q_W), F32), jax.ShapeDtypeStruct((N_Q, 256, Q_W), F32),
                             jax.ShapeDtypeStruct((1, N_STATE), F32), jax.ShapeDtypeStruct((1, D), F32)),
                  in_specs=[rev(0), rev(2), pl.BlockSpec((None, 1, N_STATE), lambda i: (nblk - 1 - i, 0, 0)),
                            _full((tb, tb)), _full((tb, tb)),
                            _full((N_Q, 256, Q_W), single=True), _full((N_Q, 256, Q_W), single=True),
                            _full((k_steps, N_STATE)), _full((1, D))],
                  out_specs=(rev(0), ANY, ANY, _full((1, N_STATE)), _full((1, D))),
                  scratch_shapes=[pltpu.VMEM((tb, N_STATE), F32), pltpu.VMEM((tb, N_STATE), F32),
                                  pltpu.VMEM((1, N_STATE), F32), pltpu.VMEM((1, N_STATE), F32),
                                  pltpu.VMEM((SUBLANES, N_STATE), F32), pltpu.VMEM((SUBLANES, N_STATE), F32),
                                  pltpu.VMEM((N_Q, 256, Q_W), F32), pltpu.VMEM((N_Q, 256, Q_W), F32),
                                  pltpu.VMEM((1, N_STATE), F32), pltpu.VMEM((1, D), F32),
                                  pltpu.VMEM((tb, D), BF16)],
                  compiler_params=_params(("arbitrary",), vmem=60 * 1024 * 1024),
                  )(dyp, proj, carries, pm, pmt, wb, wct, ptab, dvec)


def _pool_bwd(dyp, proj, pool_w, pool_scale):
    rows = dyp.shape[0]
    tb = _tb(rows, 512)
    nblk = rows // tb
    hb = tb // HALO

    def body(dy_ref, u_ref, halo_ref, z_ref, pw_ref, ps_ref, dp_ref, dpw_ref, dps_ref, ahead_ref):
        i = pl.program_id(0)
        blk = nblk - 1 - i

        @pl.when(i == 0)
        def _():
            ahead_ref[...] = jnp.zeros_like(ahead_ref)
            dpw_ref[...] = jnp.zeros_like(dpw_ref)
            dps_ref[...] = jnp.zeros_like(dps_ref)

        u = u_ref[...].astype(F32)
        halo = jnp.where(blk > 0, halo_ref[...].astype(F32), 0.0)
        pooled, counts = _pool_windows(jnp.concatenate([halo, u], axis=0), tb, blk * tb)
        silu_z, dsilu_z = _silu_parts(z_ref[...].astype(F32))
        dy = dy_ref[...].astype(F32)
        for g, w in enumerate(POOL_WINDOWS):
            cols = slice(g * 256, (g + 1) * 256)
            pooled_b = pooled[g].astype(BF16)
            mixed = _dot(pooled_b, pw_ref[g])
            scale = ps_ref[:, cols]
            dp_ref[:, D + g * 256:D + (g + 1) * 256] = (dy[:, cols] * (mixed * scale) * dsilu_z[:, cols]).astype(BF16)
            dms = dy[:, cols] * silu_z[:, cols]
            dps_ref[:, cols] += jnp.sum(dms * mixed, axis=0, keepdims=True)
            dmixed = (dms * scale).astype(BF16)
            dpw_ref[g] += _dot_tn(pooled_b, dmixed)
            dpooled = _dot_nt(dmixed, pw_ref[g])
            ratio = dpooled / counts[g]
            acc = jnp.concatenate([ratio, ahead_ref[:, cols]], axis=0)
            ahead_ref[:, cols] = ratio[:HALO, :]
            s = 1
            while s < w:
                acc = acc + pltpu.roll(acc, tb + HALO - s, axis=0)
                s *= 2
            dp_ref[:, cols] = (acc[:tb, :] - dpooled).astype(BF16)

    rev = lambda c: pl.BlockSpec((tb, D), lambda i: (nblk - 1 - i, c))
    return _pcall(body, name="pool_bwd", grid=(nblk,),
                  out_shape=(jax.ShapeDtypeStruct((rows, 2 * D), BF16), jax.ShapeDtypeStruct((4, 256, 256), F32),
                             jax.ShapeDtypeStruct((1, D), F32)),
                  in_specs=[rev(0), rev(0),
                            pl.BlockSpec((HALO, D), lambda i: (jnp.maximum((nblk - 1 - i) * hb - 1, 0), 0)),
                            rev(1), _full((4, 256, 256)), _full((1, D))],
                  out_specs=(pl.BlockSpec((tb, 2 * D), lambda i: (nblk - 1 - i, 0)), _full((4, 256, 256)),
                             _full((1, D))),
                  scratch_shapes=[pltpu.VMEM((HALO, D), F32)],
                  compiler_params=_params(("arbitrary",)))(dyp, proj, proj, proj, pool_w, pool_scale)


def _dproj_specs(tb):
    return [pl.BlockSpec((tb, 2 * D), lambda i: (i, 0)), pl.BlockSpec((tb, D), lambda i: (i, 0)),
            pl.BlockSpec((tb, D), lambda i: (i, 0)), pl.BlockSpec((tb, 2 * D), lambda i: (i, 0))]


def _in_proj_bwd_x(x, dy, dpp, dus, dzs, dpg, mod3, norm_pre, w_in):
    rows = x.shape[0]
    tb = _tb(rows, 256)
    nblk = rows // tb

    def body(x_ref, dy_ref, dpp_ref, dus_ref, dzs_ref, dpg_ref, mod_ref, np_ref, w_ref, gx_ref, vec_ref):
        @pl.when(pl.program_id(0) == 0)
        def _():
            vec_ref[...] = jnp.zeros_like(vec_ref)

        dh = _dot_nt(dpp_ref[...], w_ref[:, 0:2 * D])
        dh += _dot_nt(dus_ref[...], w_ref[:, 2 * D:3 * D])
        dh += _dot_nt(dzs_ref[...], w_ref[:, 3 * D:4 * D])
        dh += _dot_nt(dpg_ref[...], w_ref[:, 4 * D:6 * D])
        xn, r, _ = _prenorm(x_ref[...], mod_ref[...], np_ref[...])
        one_scale = 1.0 + mod_ref[1:2, :]
        vec_ref[0:1, :] += jnp.sum(dh, axis=0, keepdims=True)
        vec_ref[1:2, :] += jnp.sum(dh * xn, axis=0, keepdims=True) * np_ref[...]
        vec_ref[2:3, :] += jnp.sum(dh * xn, axis=0, keepdims=True) * one_scale
        gx_ref[...] = dy_ref[...] + _rms_bwd(dh * (np_ref[...] * one_scale), xn, r)

    row = pl.BlockSpec((tb, D), lambda i: (i, 0))
    return _pcall(body, name="in_proj_bwd_x", grid=(nblk,),
                  out_shape=(jax.ShapeDtypeStruct((rows, D), F32), jax.ShapeDtypeStruct((3, D), F32)),
                  in_specs=[row, row] + _dproj_specs(tb) + [_full((3, D)), _full((1, D)),
                                                            _full((D, N_IN), single=True)],
                  out_specs=(row, _full((3, D))),
                  compiler_params=_params(("arbitrary",)))(x, dy, dpp, dus, dzs, dpg, mod3, norm_pre, w_in)


def _in_proj_bwd_w(x, dpp, dus, dzs, dpg, mod3, norm_pre):
    rows = x.shape[0]
    tb = _tb(rows, 256)
    nblk = rows // tb

    def body(x_ref, dpp_ref, dus_ref, dzs_ref, dpg_ref, mod_ref, np_ref, dw_hbm, acc):
        i = pl.program_id(0)

        @pl.when(i == 0)
        def _():
            acc[...] = jnp.zeros_like(acc)

        _, _, h = _prenorm(x_ref[...], mod_ref[...], np_ref[...])
        ht = h.astype(BF16)
        acc[:, 0:2 * D] += _dot_tn(ht, dpp_ref[...])
        acc[:, 2 * D:3 * D] += _dot_tn(ht, dus_ref[...])
        acc[:, 3 * D:4 * D] += _dot_tn(ht, dzs_ref[...])
        acc[:, 4 * D:6 * D] += _dot_tn(ht, dpg_ref[...])

        @pl.when(i == nblk - 1)
        def _():
            pltpu.sync_copy(acc, dw_hbm)

    row = pl.BlockSpec((tb, D), lambda i: (i, 0))
    return _pcall(body, name="in_proj_bwd_w", grid=(nblk,),
                  out_shape=jax.ShapeDtypeStruct((D, N_IN), F32),
                  in_specs=[row] + _dproj_specs(tb) + [_full((3, D)), _full((1, D))],
                  out_specs=ANY,
                  scratch_shapes=[pltpu.VMEM((D, N_IN), F32)],
                  compiler_params=_params(("arbitrary",)))(x, dpp, dus, dzs, dpg, mod3, norm_pre)


def _adamw_math(w, g, m, v):
    m = ADAM_B1 * m + (1.0 - ADAM_B1) * g
    v = ADAM_B2 * v + (1.0 - ADAM_B2) * (g * g)
    m_hat = m / (1.0 - ADAM_B1 ** ADAM_STEP)
    v_hat = v / (1.0 - ADAM_B2 ** ADAM_STEP)
    delta = -ADAM_LR * (m_hat / (jnp.sqrt(v_hat) + ADAM_EPS) + ADAM_WD * w)
    return delta, m, v


def _sum_sources(ref):
    g = ref[0]
    for s in range(1, N_DEV):
        g = g + ref[s]
    return g


def _adamw_reduce(name, parts, w, m, v):
    r, c = w.shape
    tr = r if r * c <= 256 * 1024 else max(8, (256 * 1024 // c) // 8 * 8)
    while r % tr:
        tr -= 8

    def body(p_ref, w_ref, m_ref, v_ref, g_ref, d_ref, nm_ref, nv_ref):
        g = _sum_sources(p_ref)
        g_ref[...] = g
        d_ref[...], nm_ref[...], nv_ref[...] = _adamw_math(w_ref[...], g, m_ref[...], v_ref[...])

    blk = pl.BlockSpec((tr, c), lambda i: (i, 0))
    return _pcall(body, name=name, grid=(r // tr,),
                  out_shape=tuple([jax.ShapeDtypeStruct((r, c), F32)] * 4),
                  in_specs=[pl.BlockSpec((N_DEV, tr, c), lambda i: (0, i, 0)), blk, blk, blk],
                  out_specs=(blk, blk, blk, blk),
                  compiler_params=_params(("arbitrary",)))(parts, w, m, v)


def _adamw_plain(name, g, w, m, v):
    def body(g_ref, w_ref, m_ref, v_ref, d_ref, nm_ref, nv_ref):
        d_ref[...], nm_ref[...], nv_ref[...] = _adamw_math(w_ref[...], g_ref[...], m_ref[...], v_ref[...])

    vm = pl.BlockSpec(memory_space=pltpu.VMEM)
    return _pcall(body, name=name, out_shape=tuple([jax.ShapeDtypeStruct(w.shape, F32)] * 3),
                  in_specs=[vm] * 4, out_specs=(vm, vm, vm), compiler_params=_params())(g, w, m, v)


def _sum_small(parts):
    _, r, c = parts.shape

    def body(p_ref, o_ref):
        o_ref[...] = _sum_sources(p_ref)

    vm = pl.BlockSpec(memory_space=pltpu.VMEM)
    return _pcall(body, name="sum_small", out_shape=jax.ShapeDtypeStruct((r, c), F32),
                  in_specs=[vm], out_specs=vm, compiler_params=_params())(parts)


def _ada_update(c_all, dmod_cols, w, m, v):
    def body(c_ref, dm_ref, w_ref, m_ref, v_ref, g_ref, d_ref, nm_ref, nv_ref):
        ca = c_ref[...]
        g = lax.dot_general(ca * jax.nn.sigmoid(ca), dm_ref[...], (((0,), (0,)), ((), ())),
                            preferred_element_type=F32, precision=lax.Precision.HIGHEST)
        g_ref[...] = g
        d_ref[...], nm_ref[...], nv_ref[...] = _adamw_math(w_ref[...], g, m_ref[...], v_ref[...])

    vm = pl.BlockSpec(memory_space=pltpu.VMEM)
    return _pcall(body, name="ada_update", out_shape=tuple([jax.ShapeDtypeStruct(w.shape, F32)] * 4),
                  in_specs=[vm] * 5, out_specs=(vm, vm, vm, vm), compiler_params=_params())(c_all, dmod_cols, w, m, v)


SMALL_ROWS = 272


def kernel(x, c, w_ada, b_ada, norm_pre, norm_post, w_in, pool_w, pool_scale, ssm_a_re, ssm_a_im, ssm_log_dt, ssm_b_re, ssm_b_im, ssm_c_re, ssm_c_im, ssm_d, glu_w, glu_b, w_branch_pool, w_branch_ssm, w_out, loss_target, m_w_ada, m_b_ada, m_norm_pre, m_norm_post, m_w_in, m_pool_w, m_pool_scale, m_ssm_a_re, m_ssm_a_im, m_ssm_log_dt, m_ssm_b_re, m_ssm_b_im, m_ssm_c_re, m_ssm_c_im, m_ssm_d, m_glu_w, m_glu_b, m_w_branch_pool, m_w_branch_ssm, m_w_out, v_w_ada, v_b_ada, v_norm_pre, v_norm_post, v_w_in, v_pool_w, v_pool_scale, v_ssm_a_re, v_ssm_a_im, v_ssm_log_dt, v_ssm_b_re, v_ssm_b_im, v_ssm_c_re, v_ssm_c_im, v_ssm_d, v_glu_w, v_glu_b, v_w_branch_pool, v_w_branch_ssm, v_w_out):
    given = dict(locals())
    me = _flat(_me())
    rows = x.shape[1]
    x2 = x[0]
    target = loss_target[0]
    ada_cols = w_ada.shape[2]

    b_ada_s = lax.dynamic_slice(b_ada, (0, me * ada_cols), (1, ada_cols))
    c_all, mod_rows = _ada_exchange(c, w_ada[0], b_ada_s)
    mod3 = mod_rows.reshape(3, D)

    shards = _cast_shards([w_in[0], pool_w[0], glu_w[0], w_branch_pool[0], w_branch_ssm[0], w_out[0]])
    w_in_g, pool_w_g, glu_g, wbp_g, wbs_g, wout_g = _gather_weights(*shards)

    tb_ssm = _tb(rows, 256)
    k_steps = tb_ssm // SUBLANES
    a_re, a_im = ssm_a_re[0], ssm_a_im[0]
    log_dt = ssm_log_dt[0].reshape(GROUPS, 1)
    b_re_t, b_im_t = ssm_b_re[0].transpose(0, 2, 1), ssm_b_im[0].transpose(0, 2, 1)
    bb_re, bb_im, pow_re, pow_im = _s5_prep(a_re, a_im, log_dt, b_re_t, b_im_t, k_steps)
    wb = jnp.concatenate([_blockdiag(bb_re), _blockdiag(bb_im)], axis=-1).astype(BF16)
    wct = jnp.concatenate([_blockdiag(ssm_c_re[0]), _blockdiag(-ssm_c_im[0])], axis=-1).astype(BF16)
    ptab = _state_layout(pow_re, pow_im)
    dvec = ssm_d[0].reshape(1, D)
    pm = _perm_matrix(tb_ssm)
    pmt = pm.T

    proj = _in_proj(x2, mod3, norm_pre, w_in_g)
    y_pool = _pool_fwd(proj, pool_w_g, pool_scale)
    y_ssm, ys_pre, carries = _ssm_fwd(proj, pm, pmt, wb, wct, ptab, dvec, glu_g, glu_b)
    loss_part, dy, dyp, dys, dpg, dwbp, dwbs, dwout, head_vec = _head(
        x2, target, proj, y_pool, y_ssm, mod3, norm_post, wbp_g, wbs_g, wout_g)

    dy_pre, dzs, dglu_w, dglu_b = _glu_bwd(dys, proj, ys_pre, pm, pmt, glu_g, glu_b)
    dus, dwb, dwct, dabar, dd = _ssm_bwd(dy_pre, proj, carries, pm, pmt, wb, wct, ptab, dvec)
    dpp, dpool_w, dpool_scale = _pool_bwd(dyp, proj, pool_w_g, pool_scale)
    grad_x, pre_vec = _in_proj_bwd_x(x2, dy, dpp, dus, dzs, dpg, mod3, norm_pre, w_in_g)
    dw_in = _in_proj_bwd_w(x2, dpp, dus, dzs, dpg, mod3, norm_pre)

    small = jnp.concatenate([
        pre_vec[0:2], head_vec[0:1], pre_vec[2:3], head_vec[1:2], dpool_scale, dglu_b, dd,
        dabar.reshape(8, D),
        _diag_blocks(dwb[:, :, :Q_W // 2]).reshape(64, D), _diag_blocks(dwb[:, :, Q_W // 2:]).reshape(64, D),
        _diag_blocks(dwct[:, :, :Q_W // 2]).reshape(64, D), _diag_blocks(dwct[:, :, Q_W // 2:]).reshape(64, D),
    ], axis=0)

    p_w_in, p_pool_w, p_glu, p_wbp, p_wbs, p_wout, p_small = _scatter_grads(
        dw_in, dpool_w, dglu_w, dwbp, dwbs, dwout, small)

    tot = _sum_small(p_small)
    d_abar_re, d_abar_im = _state_unlayout(tot[8:16].reshape(N_STATE))
    d_bb_re, d_bb_im = tot[16:80].reshape(GROUPS, G_H, G_P), tot[80:144].reshape(GROUPS, G_H, G_P)
    g_a_re, g_a_im, g_log_dt, g_b_re_t, g_b_im_t = _s5_prep_bwd(
        a_re, a_im, log_dt, b_re_t, b_im_t, d_abar_re, d_abar_im, d_bb_re, d_bb_im)

    grads, deltas, new_m, new_v = {}, {}, {}, {}

    def small_update(name, g2):
        shape = given[name].shape
        w2, m2, v2 = (given[p + name].reshape(g2.shape) for p in ("", "m_", "v_"))
        d2, nm2, nv2 = _adamw_plain("adamw_" + name, g2, w2, m2, v2)
        grads[name], deltas[name], new_m[name], new_v[name] = (a.reshape(shape) for a in (g2, d2, nm2, nv2))

    def shard_update(name, parts):
        shape = given[name].shape
        r2 = parts.shape[1:] if parts.ndim == 3 else (parts.shape[1] * parts.shape[2], parts.shape[3])
        w2, m2, v2 = (given[p + name].reshape(r2) for p in ("", "m_", "v_"))
        out = _adamw_reduce("adamw_" + name, parts.reshape((N_DEV,) + tuple(r2)), w2, m2, v2)
        grads[name], deltas[name], new_m[name], new_v[name] = (a.reshape(shape) for a in out)

    dmod_all = p_small[:, 0:3, :].reshape(N_DEV, 3 * D)
    dmod_cols = lax.dynamic_slice(dmod_all, (0, me * ada_cols), (N_DEV, ada_cols))
    out = _ada_update(c_all, dmod_cols, w_ada[0], m_w_ada[0], v_w_ada[0])
    grads['w_ada'], deltas['w_ada'], new_m['w_ada'], new_v['w_ada'] = (a.reshape(w_ada.shape) for a in out)

    small_update('b_ada', tot[0:3].reshape(1, 3 * D))
    small_update('norm_pre', tot[3:4])
    small_update('norm_post', tot[4:5])
    small_update('pool_scale', tot[5:6])
    small_update('glu_b', tot[6:7])
    small_update('ssm_d', tot[7:8])
    small_update('ssm_a_re', g_a_re)
    small_update('ssm_a_im', g_a_im)
    small_update('ssm_log_dt', g_log_dt.reshape(1, GROUPS))
    small_update('ssm_b_re', g_b_re_t.transpose(0, 2, 1).reshape(GROUPS, G_P * G_H))
    small_update('ssm_b_im', g_b_im_t.transpose(0, 2, 1).reshape(GROUPS, G_P * G_H))
    small_update('ssm_c_re', tot[144:208])
    small_update('ssm_c_im', -tot[208:272])
    shard_update('w_in', p_w_in)
    shard_update('pool_w', p_pool_w)
    shard_update('glu_w', p_glu)
    shard_update('w_branch_pool', p_wbp)
    shard_update('w_branch_ssm', p_wbs)
    shard_update('w_out', p_wout)

    loss = lax.psum(loss_part[0, 0], ("x", "y", "c"))
    return (loss, grad_x[None], *[grads[n] for n in WEIGHTS], *[deltas[n] for n in WEIGHTS],
            *[new_m[n] for n in WEIGHTS], *[new_v[n] for n in WEIGHTS])
```

```python
import functools
import math
from typing import Callable, NamedTuple, Optional

import jax
import jax.numpy as jnp
from jax import lax
from jax.experimental import pallas as pl
from jax.experimental.pallas import tpu as pltpu

F32 = jnp.float32
BF16 = jnp.bfloat16
MESH = pl.DeviceIdType.MESH

D = 1024
N_DEV = 8
N_IN = 6 * D
GROUPS = 64
G_H = 16
G_P = 64
N_Q = 4
Q_W = 2 * 16 * G_P
N_STATE = N_Q * Q_W
POOL_WINDOWS = (2, 4, 8, 16)
HALO = 16
RMS_EPS = 1e-6
SUBLANES = 8
LANE_CHUNK = 512
VMEM_LIMIT = 56 * 1024 * 1024

ADAM_LR = 0.001
ADAM_B1 = 0.9
ADAM_B2 = 0.999
ADAM_EPS = 1e-08
ADAM_WD = 0.01
ADAM_STEP = 10

WEIGHTS = ['w_ada', 'b_ada', 'norm_pre', 'norm_post', 'w_in', 'pool_w', 'pool_scale', 'ssm_a_re',
           'ssm_a_im', 'ssm_log_dt', 'ssm_b_re', 'ssm_b_im', 'ssm_c_re', 'ssm_c_im', 'ssm_d', 'glu_w',
           'glu_b', 'w_branch_pool', 'w_branch_ssm', 'w_out']


def _pcall(body, **kw):
    return pl.pallas_call(body, **kw)


def _params(sem=None, vmem=VMEM_LIMIT):
    return pltpu.CompilerParams(dimension_semantics=sem, vmem_limit_bytes=vmem)


def _tb(rows, pref):
    return pref if rows % pref == 0 and rows // pref >= 2 else rows // 2


def _full(shape, single=False):
    nd = len(shape)
    if single:
        return pl.BlockSpec(shape, lambda i: (0,) * nd, pipeline_mode=pl.Buffered(1))
    return pl.BlockSpec(shape, lambda i: (0,) * nd)


ANY = pl.BlockSpec(memory_space=pl.ANY)


def _me():
    return lax.axis_index("x"), lax.axis_index("y"), lax.axis_index("c")


def _flat(p):
    return 4 * p[0] + 2 * p[1] + p[2]


def _peer(k):
    x, y, c = _me()
    return (1 - x if k & 4 else x, 1 - y if k & 2 else y, 1 - c if k & 1 else c)


def _silu_parts(z):
    s = jax.nn.sigmoid(z)
    return z * s, s * (1.0 + z * (1.0 - s))


_GELU_C = math.sqrt(2.0 / math.pi)


def _gelu_parts(x):
    x2 = x * x
    t = jnp.tanh(_GELU_C * (x + 0.044715 * x * x2))
    g = 0.5 * x * (1.0 + t)
    dg = 0.5 * (1.0 + t) + 0.5 * x * (1.0 - t * t) * (_GELU_C * (1.0 + 3.0 * 0.044715 * x2))
    return g, dg


def _dot(a, b):
    return jnp.dot(a, b, preferred_element_type=F32)


def _dot_nt(a, b):
    return lax.dot_general(a, b, (((1,), (1,)), ((), ())), preferred_element_type=F32)


def _dot_tn(a, b):
    return lax.dot_general(a, b, (((0,), (0,)), ((), ())), preferred_element_type=F32)


def _rms_parts(x):
    r = lax.rsqrt(jnp.mean(x * x, axis=-1, keepdims=True) + RMS_EPS)
    return x * r, r


def _rms_bwd(dxn, xn, r):
    return r * (dxn - xn * jnp.mean(dxn * xn, axis=-1, keepdims=True))


def _ada_exchange(c, w_ada_s, b_ada_s):
    cols = w_ada_s.shape[1]

    def body(c_ref, w_ref, b_ref, call_ref, mod_ref, part_ref, ssem, rsem, lsem):
        me3 = _me()
        me = _flat(me3)
        mine = pltpu.make_async_copy(c_ref, call_ref.at[pl.ds(me, 1), :], lsem.at[0])
        mine.start()
        sends = []
        for k in range(1, N_DEV):
            cp = pltpu.make_async_remote_copy(src_ref=c_ref, dst_ref=call_ref.at[pl.ds(me, 1), :],
                                              send_sem=ssem.at[k - 1], recv_sem=rsem.at[k - 1],
                                              device_id=_peer(k), device_id_type=MESH)
            cp.start()
            sends.append(cp)
        mine.wait()
        for k in range(1, N_DEV):
            p = _flat(_peer(k))
            pltpu.make_async_remote_copy(src_ref=c_ref, dst_ref=call_ref.at[pl.ds(p, 1), :],
                                         send_sem=ssem.at[k - 1], recv_sem=rsem.at[k - 1],
                                         device_id=_peer(k), device_id_type=MESH).wait_recv()
        for cp in sends:
            cp.wait_send()
        ca = call_ref[...]
        act = ca * jax.nn.sigmoid(ca)
        part_ref[...] = jnp.dot(act, w_ref[...], preferred_element_type=F32,
                                precision=lax.Precision.HIGHEST) + b_ref[...]
        own = pltpu.make_async_copy(part_ref.at[pl.ds(me, 1), :], mod_ref.at[pl.ds(me, 1), :], lsem.at[1])
        own.start()
        sends = []
        for k in range(1, N_DEV):
            p = _flat(_peer(k))
            s = N_DEV - 1 + k - 1
            cp = pltpu.make_async_remote_copy(src_ref=part_ref.at[pl.ds(p, 1), :],
                                              dst_ref=mod_ref.at[pl.ds(me, 1), :],
                                              send_sem=ssem.at[s], recv_sem=rsem.at[s],
                                              device_id=_peer(k), device_id_type=MESH)
            cp.start()
            sends.append(cp)
        own.wait()
        for k in range(1, N_DEV):
            p = _flat(_peer(k))
            s = N_DEV - 1 + k - 1
            pltpu.make_async_remote_copy(src_ref=part_ref.at[pl.ds(p, 1), :],
                                         dst_ref=mod_ref.at[pl.ds(p, 1), :],
                                         send_sem=ssem.at[s], recv_sem=rsem.at[s],
                                         device_id=_peer(k), device_id_type=MESH).wait_recv()
        for cp in sends:
            cp.wait_send()

    vm = pl.BlockSpec(memory_space=pltpu.VMEM)
    return _pcall(
        body, name="ada_exchange",
        out_shape=(jax.ShapeDtypeStruct((N_DEV, D), F32), jax.ShapeDtypeStruct((N_DEV, cols), F32)),
        in_specs=[vm, vm, vm], out_specs=(vm, vm),
        scratch_shapes=[pltpu.VMEM((N_DEV, cols), F32),
                        pltpu.SemaphoreType.DMA((2 * (N_DEV - 1),)),
                        pltpu.SemaphoreType.DMA((2 * (N_DEV - 1),)),
                        pltpu.SemaphoreType.DMA((2,))],
    )(c, w_ada_s, b_ada_s)


class _Item(NamedTuple):
    src: int
    out: int
    src_view: Callable
    dst_view: Callable
    pred: Optional[Callable] = None


def _when(pred, dest, fn):
    if pred is None:
        fn()
    else:
        pl.when(pred(dest))(fn)


def _n_sems(items):
    return len(items) * (N_DEV - 1)


def _hosted_copies(items, srcs, outs, ssem, rsem, lsem, act):
    me = _flat(_me())
    for t, it in enumerate(items):
        local = lambda t=t, it=it: pltpu.make_async_copy(
            it.src_view(srcs[it.src], me), it.dst_view(outs[it.out], me), lsem.at[t])
        if act == "start":
            _when(it.pred, me, lambda local=local: local().start())
        else:
            _when(it.pred, me, lambda local=local: local().wait())
    for k in range(1, N_DEV):
        p3 = _peer(k)
        p = _flat(p3)
        for t, it in enumerate(items):
            s = t * (N_DEV - 1) + k - 1
            send = lambda it=it, s=s, p=p, p3=p3: pltpu.make_async_remote_copy(
                src_ref=it.src_view(srcs[it.src], p), dst_ref=it.dst_view(outs[it.out], me),
                send_sem=ssem.at[s], recv_sem=rsem.at[s], device_id=p3, device_id_type=MESH)
            recv = lambda it=it, s=s, p=p, p3=p3: pltpu.make_async_remote_copy(
                src_ref=it.src_view(srcs[it.src], p), dst_ref=it.dst_view(outs[it.out], p),
                send_sem=ssem.at[s], recv_sem=rsem.at[s], device_id=p3, device_id_type=MESH)
            if act == "start":
                _when(it.pred, p, lambda send=send: send().start())
            else:
                _when(it.pred, me, lambda recv=recv: recv().wait_recv())
                _when(it.pred, p, lambda send=send: send().wait_send())


def _sem_scratch(items):
    return [pltpu.SemaphoreType.DMA((_n_sems(items),)), pltpu.SemaphoreType.DMA((_n_sems(items),)),
            pltpu.SemaphoreType.DMA((len(items),))]


def _exchange(name, srcs, out_structs, items):
    n_src, n_out = len(srcs), len(out_structs)

    def body(*refs):
        src_refs, out_refs = refs[:n_src], refs[n_src:n_src + n_out]
        sems = refs[n_src + n_out:]
        _hosted_copies(items, src_refs, out_refs, *sems, act="start")
        _hosted_copies(items, src_refs, out_refs, *sems, act="wait")

    return _pcall(body, name=name, out_shape=tuple(out_structs),
                  in_specs=[ANY] * n_src, out_specs=tuple([ANY] * n_out),
                  scratch_shapes=_sem_scratch(items))(*srcs)


def _whole(ref, dest):
    return ref


def _slot(ref, sender):
    return ref.at[sender]


def _rows_of(rows):
    return lambda ref, dev: ref.at[pl.ds(dev * rows, rows), :]


def _cols_of(cols):
    return lambda ref, dev: ref.at[:, pl.ds(dev * cols, cols)]


def _pool_rows_of(rows):
    return lambda ref, dev: ref.at[:, pl.ds(dev * rows, rows), :]


def _gather_item(src, out, dst_view):
    return _Item(src, out, _whole, dst_view)


def _scatter_item(src, out, src_view):
    return _Item(src, out, src_view, _slot)


W_IN_BLOCK = 256
W_IN_SHARD = N_IN // N_DEV
SSM_BLOCKS = (2 * D // W_IN_BLOCK, 4 * D // W_IN_BLOCK)


def _w_in_block_item(src, out, j, ssm_part):
    def block(dest):
        return (W_IN_SHARD // W_IN_BLOCK) * dest + j

    def in_ssm(dest):
        b = block(dest)
        return (b >= SSM_BLOCKS[0]) & (b < SSM_BLOCKS[1])

    def src_view(ref, dest):
        b = block(dest)
        local = b - SSM_BLOCKS[0] if ssm_part else jnp.where(b < SSM_BLOCKS[0], b, b - (SSM_BLOCKS[1] - SSM_BLOCKS[0]))
        local = jnp.clip(local, 0, ref.shape[1] // W_IN_BLOCK - 1)
        return ref.at[:, pl.ds(local * W_IN_BLOCK, W_IN_BLOCK)]

    def dst_view(ref, sender):
        return ref.at[sender, :, pl.ds(j * W_IN_BLOCK, W_IN_BLOCK)]

    pred = in_ssm if ssm_part else (lambda dest: jnp.logical_not(in_ssm(dest)))
    return _Item(src, out, src_view, dst_view, pred)


def _cast_shards(arrs):
    def body(*refs):
        n = len(refs) // 2
        for i in range(n):
            refs[n + i][...] = refs[i][...].astype(BF16)

    vm = pl.BlockSpec(memory_space=pltpu.VMEM)
    return _pcall(body, name="cast_shards",
                  out_shape=tuple(jax.ShapeDtypeStruct(a.shape, BF16) for a in arrs),
                  in_specs=[vm] * len(arrs), out_specs=tuple([vm] * len(arrs)),
                  compiler_params=_params())(*arrs)


def _s5_discretise(a_re, a_im, log_dt, b_re_t, b_im_t):
    dt = jnp.exp(log_dt)
    lam_re = jnp.minimum(a_re, -1e-4)
    lam_im = a_im
    mag = jnp.exp(lam_re * dt)
    abar_re = mag * jnp.cos(lam_im * dt)
    abar_im = mag * jnp.sin(lam_im * dt)
    den = lam_re * lam_re + lam_im * lam_im
    num_re = abar_re - 1.0
    f_re = (num_re * lam_re + abar_im * lam_im) / den
    f_im = (abar_im * lam_re - num_re * lam_im) / den
    f_re, f_im = f_re[:, None, :], f_im[:, None, :]
    bb_re = f_re * b_re_t - f_im * b_im_t
    bb_im = f_re * b_im_t + f_im * b_re_t
    return abar_re, abar_im, bb_re, bb_im


def _s5_prep(a_re, a_im, log_dt, b_re_t, b_im_t, n_pow):
    def body(ar_ref, ai_ref, ld_ref, br_ref, bi_ref, bbr_ref, bbi_ref, pr_ref, pi_ref):
        abar_re, abar_im, bb_re, bb_im = _s5_discretise(ar_ref[...], ai_ref[...], ld_ref[...], br_ref[...], bi_ref[...])
        bbr_ref[...] = bb_re
        bbi_ref[...] = bb_im
        p_re, p_im = abar_re, abar_im
        pr_ref[0] = p_re
        pi_ref[0] = p_im
        for k in range(1, n_pow):
            p_re, p_im = p_re * abar_re - p_im * abar_im, p_re * abar_im + p_im * abar_re
            pr_ref[k] = p_re
            pi_ref[k] = p_im

    vm = pl.BlockSpec(memory_space=pltpu.VMEM)
    return _pcall(body, name="s5_prep",
                  out_shape=(jax.ShapeDtypeStruct(b_re_t.shape, F32), jax.ShapeDtypeStruct(b_re_t.shape, F32),
                             jax.ShapeDtypeStruct((n_pow, GROUPS, G_P), F32),
                             jax.ShapeDtypeStruct((n_pow, GROUPS, G_P), F32)),
                  in_specs=[vm] * 5, out_specs=(vm, vm, vm, vm), compiler_params=_params(),
                  )(a_re, a_im, log_dt, b_re_t, b_im_t)


def _s5_prep_bwd(a_re, a_im, log_dt, b_re_t, b_im_t, d_abar_re, d_abar_im, d_bb_re, d_bb_im):
    def body(ar_ref, ai_ref, ld_ref, br_ref, bi_ref, dar_ref, dai_ref, dbr_ref, dbi_ref,
             gar_ref, gai_ref, gld_ref, gbr_ref, gbi_ref):
        _, vjp = jax.vjp(_s5_discretise, ar_ref[...], ai_ref[...], ld_ref[...], br_ref[...], bi_ref[...])
        g = vjp((dar_ref[...], dai_ref[...], dbr_ref[...], dbi_ref[...]))
        gar_ref[...] = g[0]
        gai_ref[...] = g[1]
        gld_ref[...] = g[2]
        gbr_ref[...] = g[3]
        gbi_ref[...] = g[4]

    vm = pl.BlockSpec(memory_space=pltpu.VMEM)
    ins = (a_re, a_im, log_dt, b_re_t, b_im_t)
    return _pcall(body, name="s5_prep_bwd",
                  out_shape=tuple(jax.ShapeDtypeStruct(a.shape, F32) for a in ins),
                  in_specs=[vm] * 9, out_specs=tuple([vm] * 5), compiler_params=_params(),
                  )(*ins, d_abar_re, d_abar_im, d_bb_re, d_bb_im)


def _blockdiag(t):
    t4 = t.reshape(N_Q, 16, G_H, G_P)
    eye = jnp.eye(16, dtype=t.dtype)
    return (t4[:, :, :, None, :] * eye[None, :, None, :, None]).reshape(N_Q, 16 * G_H, 16 * G_P)


def _diag_blocks(m):
    m5 = m.reshape(N_Q, 16, G_H, 16, G_P)
    eye = jnp.eye(16, dtype=m.dtype)
    return (m5 * eye[None, :, None, :, None]).sum(axis=3).reshape(GROUPS, G_H, G_P)


def _state_layout(re, im):
    lead = re.shape[:-2]
    r = re.reshape(lead + (N_Q, 1, 16 * G_P))
    i = im.reshape(lead + (N_Q, 1, 16 * G_P))
    return jnp.concatenate([r, i], axis=-2).reshape(lead + (N_STATE,))


def _state_unlayout(v):
    v4 = v.reshape(N_Q, 2, 16, G_P)
    return v4[:, 0].reshape(GROUPS, G_P), v4[:, 1].reshape(GROUPS, G_P)


def _perm_matrix(tb):
    k_steps = tb // SUBLANES
    r = jnp.arange(tb)
    src = (r % SUBLANES) * k_steps + r // SUBLANES
    return (src[:, None] == jnp.arange(tb)[None, :]).astype(BF16)


def _lane_chunks():
    for q in range(N_Q):
        for lc in range(Q_W // 2 // LANE_CHUNK):
            re = q * Q_W + lc * LANE_CHUNK
            yield re, re + Q_W // 2


def _tile(k):
    return pl.ds(pl.multiple_of(k * SUBLANES, SUBLANES), SUBLANES)


def _scan_forward(s_ref, p_ref, carry_ref, enter_ref, fin_ref, k_steps):
    for re, im in _lane_chunks():
        lr, li = pl.ds(re, LANE_CHUNK), pl.ds(im, LANE_CHUNK)
        a_re = jnp.broadcast_to(p_ref[0:1, lr], (SUBLANES, LANE_CHUNK))
        a_im = jnp.broadcast_to(p_ref[0:1, li], (SUBLANES, LANE_CHUNK))

        def local(k, st):
            sr, si = st
            rows = _tile(k)
            nr = a_re * sr - a_im * si + s_ref[rows, lr]
            ni = a_re * si + a_im * sr + s_ref[rows, li]
            s_ref[rows, lr] = nr
            s_ref[rows, li] = ni
            return nr, ni

        zero = jnp.zeros((SUBLANES, LANE_CHUNK), F32)
        fr, fi = lax.fori_loop(0, k_steps, local, (zero, zero))
        fin_ref[:, lr] = fr
        fin_ref[:, li] = fi
        ak_re, ak_im = p_ref[k_steps - 1:k_steps, lr], p_ref[k_steps - 1:k_steps, li]
        c_re, c_im = carry_ref[:, lr], carry_ref[:, li]
        for seg in range(SUBLANES):
            enter_ref[seg:seg + 1, lr] = c_re
            enter_ref[seg:seg + 1, li] = c_im
            f_re, f_im = fin_ref[seg:seg + 1, lr], fin_ref[seg:seg + 1, li]
            c_re, c_im = f_re + ak_re * c_re - ak_im * c_im, f_im + ak_re * c_im + ak_im * c_re
        carry_ref[:, lr] = c_re
        carry_ref[:, li] = c_im
        e_re, e_im = enter_ref[:, lr], enter_ref[:, li]

        def fix(k, _):
            rows = _tile(k)
            p_re = p_ref[pl.ds(k, 1), lr]
            p_im = p_ref[pl.ds(k, 1), li]
            s_ref[rows, lr] = s_ref[rows, lr] + (p_re * e_re - p_im * e_im)
            s_ref[rows, li] = s_ref[rows, li] + (p_re * e_im + p_im * e_re)
            return 0

        lax.fori_loop(0, k_steps, fix, 0)


def _scan_backward(g_ref, s_ref, p_ref, carry_ref, enter_ref, fin_ref, da_ref, k_steps):
    for re, im in _lane_chunks():
        lr, li = pl.ds(re, LANE_CHUNK), pl.ds(im, LANE_CHUNK)
        a_re = jnp.broadcast_to(p_ref[0:1, lr], (SUBLANES, LANE_CHUNK))
        a_im = jnp.broadcast_to(p_ref[0:1, li], (SUBLANES, LANE_CHUNK))

        def local(j, st):
            sr, si = st
            rows = _tile(k_steps - 1 - j)
            nr = a_re * sr + a_im * si + g_ref[rows, lr]
            ni = a_re * si - a_im * sr + g_ref[rows, li]
            g_ref[rows, lr] = nr
            g_ref[rows, li] = ni
            return nr, ni

        zero = jnp.zeros((SUBLANES, LANE_CHUNK), F32)
        fr, fi = lax.fori_loop(0, k_steps, local, (zero, zero))
        fin_ref[:, lr] = fr
        fin_ref[:, li] = fi
        ak_re, ak_im = p_ref[k_steps - 1:k_steps, lr], p_ref[k_steps - 1:k_steps, li]
        c_re, c_im = carry_ref[:, lr], carry_ref[:, li]
        lam_in = [None] * SUBLANES
        for seg in reversed(range(SUBLANES)):
            lam_in[seg] = (c_re, c_im)
            f_re, f_im = fin_ref[seg:seg + 1, lr], fin_ref[seg:seg + 1, li]
            c_re, c_im = f_re + ak_re * c_re + ak_im * c_im, f_im + ak_re * c_im - ak_im * c_re
        carry_ref[:, lr] = c_re
        carry_ref[:, li] = c_im
        for seg in range(SUBLANES):
            fin_ref[seg:seg + 1, lr] = lam_in[seg][0]
            fin_ref[seg:seg + 1, li] = lam_in[seg][1]
        e_re, e_im = fin_ref[:, lr], fin_ref[:, li]

        def fix(k, acc):
            acc_re, acc_im = acc
            rows = _tile(k)
            p_re = p_ref[pl.ds(k_steps - 1 - k, 1), lr]
            p_im = p_ref[pl.ds(k_steps - 1 - k, 1), li]
            l_re = g_ref[rows, lr] + (p_re * e_re + p_im * e_im)
            l_im = g_ref[rows, li] + (p_re * e_im - p_im * e_re)
            g_ref[rows, lr] = l_re
            g_ref[rows, li] = l_im
            prev = _tile(jnp.maximum(k - 1, 0))
            first = k == 0
            sp_re = jnp.where(first, enter_ref[:, lr], s_ref[prev, lr])
            sp_im = jnp.where(first, enter_ref[:, li], s_ref[prev, li])
            return acc_re + (l_re * sp_re + l_im * sp_im), acc_im + (l_im * sp_re - l_re * sp_im)

        acc_re, acc_im = lax.fori_loop(0, k_steps, fix, (zero, zero))
        da_ref[:, lr] = da_ref[:, lr] + jnp.sum(acc_re, axis=0, keepdims=True)
        da_ref[:, li] = da_ref[:, li] + jnp.sum(acc_im, axis=0, keepdims=True)


def _prenorm(x, mod3, norm_pre):
    xn, r = _rms_parts(x)
    return xn, r, xn * norm_pre * (1.0 + mod3[1:2, :]) + mod3[0:1, :]


def _in_proj(x, mod3, norm_pre, w_in, shards):
    rows = x.shape[0]
    tb = _tb(rows, 512)
    nblk = rows // tb
    n_sh = len(shards)
    pool_rows = shards[0].shape[1]
    w_rows = shards[1].shape[0]
    items = [_gather_item(0, 0, _pool_rows_of(pool_rows))] + \
            [_gather_item(t, t, _rows_of(w_rows)) for t in range(1, n_sh)]

    def body(x_ref, mod_ref, np_ref, w_ref, *rest):
        src_refs, proj_ref, out_refs, sems = rest[:n_sh], rest[n_sh], rest[n_sh + 1:2 * n_sh + 1], rest[2 * n_sh + 1:]
        i = pl.program_id(0)

        @pl.when(i == 0)
        def _():
            _hosted_copies(items, src_refs, out_refs, *sems, act="start")

        _, _, h = _prenorm(x_ref[...], mod_ref[...], np_ref[...])
        hb = h.astype(BF16)
        for j in range(N_IN // D):
            cols = slice(j * D, (j + 1) * D)
            proj_ref[:, cols] = _dot(hb, w_ref[:, cols]).astype(BF16)

        @pl.when(i == nblk - 1)
        def _():
            _hosted_copies(items, src_refs, out_refs, *sems, act="wait")

    full = [jax.ShapeDtypeStruct((4, 256, 256), BF16)] + [jax.ShapeDtypeStruct((D, D), BF16)] * (n_sh - 1)
    return _pcall(body, name="in_proj", grid=(nblk,),
                  out_shape=(jax.ShapeDtypeStruct((rows, N_IN), BF16), *full),
                  in_specs=[pl.BlockSpec((tb, D), lambda i: (i, 0)), _full((3, D)), _full((1, D)),
                            _full((D, N_IN), single=True)] + [ANY] * n_sh,
                  out_specs=(pl.BlockSpec((tb, N_IN), lambda i: (i, 0)), *([ANY] * n_sh)),
                  scratch_shapes=_sem_scratch(items),
                  compiler_params=_params(("arbitrary",)))(x, mod3, norm_pre, w_in, *shards)


def _pool_windows(ext, tb, first_row):
    pos = (first_row + lax.broadcasted_iota(jnp.int32, (tb, 1), 0) + 1).astype(F32)
    pooled, counts = [], []
    for g, w in enumerate(POOL_WINDOWS):
        acc = ext[:, g * 256:(g + 1) * 256]
        tok = acc[HALO:, :]
        s = 1
        while s < w:
            acc = acc + pltpu.roll(acc, s, axis=0)
            s *= 2
        cnt = jnp.minimum(pos, float(w))
        pooled.append(acc[HALO:, :] / cnt - tok)
        counts.append(cnt)
    return pooled, counts


def _pool_fwd(proj, pool_w, pool_scale):
    rows = proj.shape[0]
    tb = _tb(rows, 512)
    hb = tb // HALO

    def body(u_ref, halo_ref, z_ref, pw_ref, ps_ref, y_ref):
        i = pl.program_id(0)
        u = u_ref[...].astype(F32)
        halo = jnp.where(i > 0, halo_ref[...].astype(F32), 0.0)
        pooled, _ = _pool_windows(jnp.concatenate([halo, u], axis=0), tb, i * tb)
        silu_z, _ = _silu_parts(z_ref[...].astype(F32))
        for g in range(4):
            cols = slice(g * 256, (g + 1) * 256)
            mixed = _dot(pooled[g].astype(BF16), pw_ref[g])
            y_ref[:, cols] = (mixed * ps_ref[:, cols] * silu_z[:, cols]).astype(BF16)

    return _pcall(body, name="pool_fwd", grid=(rows // tb,),
                  out_shape=jax.ShapeDtypeStruct((rows, D), BF16),
                  in_specs=[pl.BlockSpec((tb, D), lambda i: (i, 0)),
                            pl.BlockSpec((HALO, D), lambda i: (jnp.maximum(i * hb - 1, 0), 0)),
                            pl.BlockSpec((tb, D), lambda i: (i, 1)),
                            _full((4, 256, 256)), _full((1, D))],
                  out_specs=pl.BlockSpec((tb, D), lambda i: (i, 0)),
                  compiler_params=_params(("arbitrary",)))(proj, proj, proj, pool_w, pool_scale)


def _ssm_fwd(proj, pm, pmt, wb, wct, ptab, dvec, glu_w, glu_b):
    rows = proj.shape[0]
    tb = pm.shape[0]
    k_steps = tb // SUBLANES
    nblk = rows // tb

    def body(u_ref, z_ref, pm_ref, pmt_ref, wb_ref, wct_ref, p_ref, d_ref, gw_ref, gb_ref,
             y_ref, ys_ref, carry_out_ref, s_ref, carry_ref, enter_ref, fin_ref):
        @pl.when(pl.program_id(0) == 0)
        def _():
            carry_ref[...] = jnp.zeros_like(carry_ref)

        carry_out_ref[...] = carry_ref[...]
        up = _dot(pm_ref[...], u_ref[...]).astype(BF16)
        for q in range(N_Q):
            s_ref[:, q * Q_W:(q + 1) * Q_W] = _dot(up[:, q * 256:(q + 1) * 256], wb_ref[q])
        _scan_forward(s_ref, p_ref, carry_ref, enter_ref, fin_ref, k_steps)
        for q in range(N_Q):
            cols = slice(q * 256, (q + 1) * 256)
            y = _dot_nt(s_ref[:, q * Q_W:(q + 1) * Q_W].astype(BF16), wct_ref[q])
            ys_ref[:, cols] = y + d_ref[:, cols] * up[:, cols].astype(F32)
        yg, _ = _gelu_parts(ys_ref[...])
        gate = jax.nn.sigmoid(_dot(yg.astype(BF16), gw_ref[...]) + gb_ref[...])
        zp = _dot(pm_ref[...], z_ref[...])
        silu_z, _ = _silu_parts(zp)
        y_ref[...] = _dot(pmt_ref[...], (yg * gate * silu_z).astype(BF16)).astype(BF16)

    return _pcall(body, name="ssm_fwd", grid=(nblk,),
                  out_shape=(jax.ShapeDtypeStruct((rows, D), BF16), jax.ShapeDtypeStruct((rows, D), F32),
                             jax.ShapeDtypeStruct((nblk, 1, N_STATE), F32)),
                  in_specs=[pl.BlockSpec((tb, D), lambda i: (i, 2)), pl.BlockSpec((tb, D), lambda i: (i, 3)),
                            _full((tb, tb)), _full((tb, tb)),
                            _full((N_Q, 256, Q_W), single=True), _full((N_Q, 256, Q_W), single=True),
                            _full((k_steps, N_STATE)), _full((1, D)), _full((D, D), single=True), _full((1, D))],
                  out_specs=(pl.BlockSpec((tb, D), lambda i: (i, 0)), pl.BlockSpec((tb, D), lambda i: (i, 0)),
                             pl.BlockSpec((None, 1, N_STATE), lambda i: (i, 0, 0))),
                  scratch_shapes=[pltpu.VMEM((tb, N_STATE), F32), pltpu.VMEM((1, N_STATE), F32),
                                  pltpu.VMEM((SUBLANES, N_STATE), F32), pltpu.VMEM((SUBLANES, N_STATE), F32)],
                  compiler_params=_params(("arbitrary",)))(proj, proj, pm, pmt, wb, wct, ptab, dvec, glu_w, glu_b)


def _head(x, target, proj, y_pool, y_ssm, mod3, norm_post, wbp, wbs, wout):
    rows = x.shape[0]
    tb = _tb(rows, 256)
    nblk = rows // tb
    n_feat = float(D)

    def body(x_ref, t_ref, gp_ref, gs_ref, yp_ref, ys_ref, mod_ref, npost_ref, wbp_ref, wbs_ref, wout_ref,
             loss_ref, dy_ref, dyp_ref, dys_ref, dg_ref, dwbp_hbm, dwbs_hbm, dwout_hbm, vec_ref,
             acc_bp, acc_bs, acc_out, acc_loss, acc_vec):
        i = pl.program_id(0)

        @pl.when(i == 0)
        def _():
            acc_bp[...] = jnp.zeros_like(acc_bp)
            acc_bs[...] = jnp.zeros_like(acc_bs)
            acc_out[...] = jnp.zeros_like(acc_out)
            acc_loss[...] = jnp.zeros_like(acc_loss)
            acc_vec[...] = jnp.zeros_like(acc_vec)

        yp, ys = yp_ref[...], ys_ref[...]
        sgp = jax.nn.sigmoid(gp_ref[...].astype(F32))
        sgs = jax.nn.sigmoid(gs_ref[...].astype(F32))
        pb = _dot(yp, wbp_ref[...])
        psm = _dot(ys, wbs_ref[...])
        mb = (sgp * pb + sgs * psm).astype(BF16)
        out = _dot(mb, wout_ref[...])
        on, r = _rms_parts(out)
        gate = mod_ref[2:3, :]
        npost = npost_ref[...]
        normed = on * npost
        diff = x_ref[...] + gate * normed - t_ref[...]
        acc_loss[...] += jnp.sum(diff * diff, axis=0, keepdims=True)
        dy = diff * (1.0 / n_feat)
        dy_ref[...] = dy
        acc_vec[0:1, :] += jnp.sum(dy * normed, axis=0, keepdims=True)
        dn = dy * gate
        acc_vec[1:2, :] += jnp.sum(dn * on, axis=0, keepdims=True)
        dout = _rms_bwd(dn * npost, on, r).astype(BF16)
        acc_out[...] += _dot_tn(mb, dout)
        dm = _dot_nt(dout, wout_ref[...])
        dpb = (dm * sgp).astype(BF16)
        dps = (dm * sgs).astype(BF16)
        dg_ref[:, :D] = (dm * pb * sgp * (1.0 - sgp)).astype(BF16)
        dg_ref[:, D:] = (dm * psm * sgs * (1.0 - sgs)).astype(BF16)
        acc_bp[...] += _dot_tn(yp, dpb)
        acc_bs[...] += _dot_tn(ys, dps)
        dyp_ref[...] = _dot_nt(dpb, wbp_ref[...]).astype(BF16)
        dys_ref[...] = _dot_nt(dps, wbs_ref[...]).astype(BF16)

        @pl.when(i == nblk - 1)
        def _():
            loss_ref[...] = 0.5 / n_feat * jnp.sum(acc_loss[...], axis=1, keepdims=True)
            vec_ref[...] = acc_vec[...]
            pltpu.sync_copy(acc_bp, dwbp_hbm)
            pltpu.sync_copy(acc_bs, dwbs_hbm)
            pltpu.sync_copy(acc_out, dwout_hbm)

    row = lambda c: pl.BlockSpec((tb, D), lambda i: (i, c))
    w = _full((D, D), single=True)
    return _pcall(body, name="head", grid=(nblk,),
                  out_shape=(jax.ShapeDtypeStruct((1, 1), F32), jax.ShapeDtypeStruct((rows, D), F32),
                             jax.ShapeDtypeStruct((rows, D), BF16), jax.ShapeDtypeStruct((rows, D), BF16),
                             jax.ShapeDtypeStruct((rows, 2 * D), BF16),
                             jax.ShapeDtypeStruct((D, D), F32), jax.ShapeDtypeStruct((D, D), F32),
                             jax.ShapeDtypeStruct((D, D), F32), jax.ShapeDtypeStruct((2, D), F32)),
                  in_specs=[row(0), row(0), row(4), row(5), row(0), row(0), _full((3, D)), _full((1, D)), w, w, w],
                  out_specs=(_full((1, 1)), row(0), row(0), row(0), pl.BlockSpec((tb, 2 * D), lambda i: (i, 0)),
                             ANY, ANY, ANY, _full((2, D))),
                  scratch_shapes=[pltpu.VMEM((D, D), F32), pltpu.VMEM((D, D), F32), pltpu.VMEM((D, D), F32),
                                  pltpu.VMEM((1, D), F32), pltpu.VMEM((2, D), F32)],
                  compiler_params=_params(("arbitrary",)))(x, target, proj, proj, y_pool, y_ssm, mod3, norm_post,
                                                           wbp, wbs, wout)


def _glu_bwd(dys, proj, ys_pre, pm, pmt, glu_w, glu_b):
    rows = dys.shape[0]
    tb = pm.shape[0]
    nblk = rows // tb

    def body(dys_ref, z_ref, ysp_ref, pm_ref, pmt_ref, gw_ref, gb_ref, dyp_ref, dz_ref, dgw_hbm, dgb_ref,
             acc_w, acc_b):
        i = pl.program_id(0)

        @pl.when(i == 0)
        def _():
            acc_w[...] = jnp.zeros_like(acc_w)
            acc_b[...] = jnp.zeros_like(acc_b)

        d_out = _dot(pm_ref[...], dys_ref[...])
        z = _dot(pm_ref[...], z_ref[...])
        yg, dgelu = _gelu_parts(ysp_ref[...])
        ygb = yg.astype(BF16)
        sg = jax.nn.sigmoid(_dot(ygb, gw_ref[...]) + gb_ref[...])
        silu_z, dsilu_z = _silu_parts(z)
        dz = d_out * (yg * sg) * dsilu_z
        dz_ref[...] = _dot(pmt_ref[...], dz.astype(BF16)).astype(BF16)
        dglu = d_out * silu_z
        dq = dglu * yg * sg * (1.0 - sg)
        dqb = dq.astype(BF16)
        acc_b[...] += jnp.sum(dq, axis=0, keepdims=True)
        acc_w[...] += _dot_tn(ygb, dqb)
        dyg = dglu * sg + _dot_nt(dqb, gw_ref[...])
        dyp_ref[...] = (dyg * dgelu).astype(BF16)

        @pl.when(i == nblk - 1)
        def _():
            dgb_ref[...] = acc_b[...]
            pltpu.sync_copy(acc_w, dgw_hbm)

    row = lambda c: pl.BlockSpec((tb, D), lambda i: (i, c))
    return _pcall(body, name="glu_bwd", grid=(nblk,),
                  out_shape=(jax.ShapeDtypeStruct((rows, D), BF16), jax.ShapeDtypeStruct((rows, D), BF16),
                             jax.ShapeDtypeStruct((D, D), F32), jax.ShapeDtypeStruct((1, D), F32)),
                  in_specs=[row(0), row(3), row(0), _full((tb, tb)), _full((tb, tb)),
                            _full((D, D), single=True), _full((1, D))],
                  out_specs=(row(0), row(0), ANY, _full((1, D))),
                  scratch_shapes=[pltpu.VMEM((D, D), F32), pltpu.VMEM((1, D), F32)],
                  compiler_params=_params(("arbitrary",)))(dys, proj, ys_pre, pm, pmt, glu_w, glu_b)


def _ssm_bwd(dyp, proj, carries, pm, pmt, wb, wct, ptab, dvec, mat_grads, dpool_w, dw_in_rest):
    rows = dyp.shape[0]
    tb = pm.shape[0]
    k_steps = tb // SUBLANES
    nblk = rows // tb
    n_mat = len(mat_grads)
    hosted = [*mat_grads, dpool_w, dw_in_rest]
    n_h = len(hosted)
    shard_rows = D // N_DEV
    pool_rows = dpool_w.shape[1] // N_DEV
    items = [_scatter_item(t, t, _rows_of(shard_rows)) for t in range(n_mat)] + \
            [_scatter_item(n_mat, n_mat, _pool_rows_of(pool_rows))] + \
            [_w_in_block_item(n_mat + 1, n_mat + 1, j, ssm_part=False) for j in range(W_IN_SHARD // W_IN_BLOCK)]
    n_in, n_out = 9, 5

    def body(*refs):
        dyp_ref, u_ref, cin_ref, pm_ref, pmt_ref, wb_ref, wct_ref, p_ref, d_ref = refs[:n_in]
        src_refs = refs[n_in:n_in + n_h]
        du_ref, dwb_hbm, dwct_hbm, da_ref, dd_ref = refs[n_in + n_h:n_in + n_h + n_out]
        recv_refs = refs[n_in + n_h + n_out:n_in + 2 * n_h + n_out]
        (s_ref, g_ref, carry_f, carry_b, enter_ref, fin_ref, acc_wb, acc_wct, acc_da, acc_dd, dup_ref,
         *sems) = refs[n_in + 2 * n_h + n_out:]
        i = pl.program_id(0)

        @pl.when(i == 0)
        def _():
            _hosted_copies(items, src_refs, recv_refs, *sems, act="start")
            carry_b[...] = jnp.zeros_like(carry_b)
            acc_wb[...] = jnp.zeros_like(acc_wb)
            acc_wct[...] = jnp.zeros_like(acc_wct)
            acc_da[...] = jnp.zeros_like(acc_da)
            acc_dd[...] = jnp.zeros_like(acc_dd)

        dy = dyp_ref[...]
        up = _dot(pm_ref[...], u_ref[...]).astype(BF16)
        acc_dd[...] += jnp.sum(dy.astype(F32) * up.astype(F32), axis=0, keepdims=True)
        carry_f[...] = cin_ref[...]
        for q in range(N_Q):
            cols = slice(q * 256, (q + 1) * 256)
            s_ref[:, q * Q_W:(q + 1) * Q_W] = _dot(up[:, cols], wb_ref[q])
            g_ref[:, q * Q_W:(q + 1) * Q_W] = _dot(dy[:, cols], wct_ref[q])
        _scan_forward(s_ref, p_ref, carry_f, enter_ref, fin_ref, k_steps)
        for q in range(N_Q):
            acc_wct[q] += _dot_tn(dy[:, q * 256:(q + 1) * 256], s_ref[:, q * Q_W:(q + 1) * Q_W].astype(BF16))
        _scan_backward(g_ref, s_ref, p_ref, carry_b, enter_ref, fin_ref, acc_da, k_steps)
        for q in range(N_Q):
            cols = slice(q * 256, (q + 1) * 256)
            lam = g_ref[:, q * Q_W:(q + 1) * Q_W].astype(BF16)
            acc_wb[q] += _dot_tn(up[:, cols], lam)
            dup_ref[:, cols] = (_dot_nt(lam, wb_ref[q]) + d_ref[:, cols] * dy[:, cols].astype(F32)).astype(BF16)
        du_ref[...] = _dot(pmt_ref[...], dup_ref[...]).astype(BF16)

        @pl.when(i == nblk - 1)
        def _():
            da_ref[...] = acc_da[...]
            dd_ref[...] = acc_dd[...]
            pltpu.sync_copy(acc_wb, dwb_hbm)
            pltpu.sync_copy(acc_wct, dwct_hbm)
            _hosted_copies(items, src_refs, recv_refs, *sems, act="wait")

    rev = lambda c: pl.BlockSpec((tb, D), lambda i: (nblk - 1 - i, c))
    recv = [jax.ShapeDtypeStruct((N_DEV, shard_rows, D), F32)] * n_mat + \
           [jax.ShapeDtypeStruct((N_DEV, dpool_w.shape[0], pool_rows, dpool_w.shape[2]), F32),
            jax.ShapeDtypeStruct((N_DEV, D, W_IN_SHARD), BF16)]
    return _pcall(body, name="ssm_bwd", grid=(nblk,),
                  out_shape=(jax.ShapeDtypeStruct((rows, D), BF16),
                             jax.ShapeDtypeStruct((N_Q, 256, Q_W), F32), jax.ShapeDtypeStruct((N_Q, 256, Q_W), F32),
                             jax.ShapeDtypeStruct((1, N_STATE), F32), jax.ShapeDtypeStruct((1, D), F32), *recv),
                  in_specs=[rev(0), rev(2), pl.BlockSpec((None, 1, N_STATE), lambda i: (nblk - 1 - i, 0, 0)),
                            _full((tb, tb)), _full((tb, tb)),
                            _full((N_Q, 256, Q_W), single=True), _full((N_Q, 256, Q_W), single=True),
                            _full((k_steps, N_STATE)), _full((1, D))] + [ANY] * n_h,
                  out_specs=(rev(0), ANY, ANY, _full((1, N_STATE)), _full((1, D)), *([ANY] * n_h)),
                  scratch_shapes=[pltpu.VMEM((tb, N_STATE), F32), pltpu.VMEM((tb, N_STATE), F32),
                                  pltpu.VMEM((1, N_STATE), F32), pltpu.VMEM((1, N_STATE), F32),
                                  pltpu.VMEM((SUBLANES, N_STATE), F32), pltpu.VMEM((SUBLANES, N_STATE), F32),
                                  pltpu.VMEM((N_Q, 256, Q_W), F32), pltpu.VMEM((N_Q, 256, Q_W), F32),
                                  pltpu.VMEM((1, N_STATE), F32), pltpu.VMEM((1, D), F32),
                                  pltpu.VMEM((tb, D), BF16)] + _sem_scratch(items),
                  compiler_params=_params(("arbitrary",), vmem=60 * 1024 * 1024),
                  )(dyp, proj, carries, pm, pmt, wb, wct, ptab, dvec, *hosted)


def _pool_bwd(dyp, proj, pool_w, pool_scale):
    rows = dyp.shape[0]
    tb = _tb(rows, 512)
    nblk = rows // tb
    hb = tb // HALO

    def body(dy_ref, u_ref, halo_ref, z_ref, pw_ref, ps_ref, dp_ref, dpw_ref, dps_ref, ahead_ref):
        i = pl.program_id(0)
        blk = nblk - 1 - i

        @pl.when(i == 0)
        def _():
            ahead_ref[...] = jnp.zeros_like(ahead_ref)
            dpw_ref[...] = jnp.zeros_like(dpw_ref)
            dps_ref[...] = jnp.zeros_like(dps_ref)

        u = u_ref[...].astype(F32)
        halo = jnp.where(blk > 0, halo_ref[...].astype(F32), 0.0)
        pooled, counts = _pool_windows(jnp.concatenate([halo, u], axis=0), tb, blk * tb)
        silu_z, dsilu_z = _silu_parts(z_ref[...].astype(F32))
        dy = dy_ref[...].astype(F32)
        for g, w in enumerate(POOL_WINDOWS):
            cols = slice(g * 256, (g + 1) * 256)
            pooled_b = pooled[g].astype(BF16)
            mixed = _dot(pooled_b, pw_ref[g])
            scale = ps_ref[:, cols]
            dp_ref[:, D + g * 256:D + (g + 1) * 256] = (dy[:, cols] * (mixed * scale) * dsilu_z[:, cols]).astype(BF16)
            dms = dy[:, cols] * silu_z[:, cols]
            dps_ref[:, cols] += jnp.sum(dms * mixed, axis=0, keepdims=True)
            dmixed = (dms * scale).astype(BF16)
            dpw_ref[g] += _dot_tn(pooled_b, dmixed)
            dpooled = _dot_nt(dmixed, pw_ref[g])
            ratio = dpooled / counts[g]
            acc = jnp.concatenate([ratio, ahead_ref[:, cols]], axis=0)
            ahead_ref[:, cols] = ratio[:HALO, :]
            s = 1
            while s < w:
                acc = acc + pltpu.roll(acc, tb + HALO - s, axis=0)
                s *= 2
            dp_ref[:, cols] = (acc[:tb, :] - dpooled).astype(BF16)

    rev = lambda c: pl.BlockSpec((tb, D), lambda i: (nblk - 1 - i, c))
    return _pcall(body, name="pool_bwd", grid=(nblk,),
                  out_shape=(jax.ShapeDtypeStruct((rows, 2 * D), BF16), jax.ShapeDtypeStruct((4, 256, 256), F32),
                             jax.ShapeDtypeStruct((1, D), F32)),
                  in_specs=[rev(0), rev(0),
                            pl.BlockSpec((HALO, D), lambda i: (jnp.maximum((nblk - 1 - i) * hb - 1, 0), 0)),
                            rev(1), _full((4, 256, 256)), _full((1, D))],
                  out_specs=(pl.BlockSpec((tb, 2 * D), lambda i: (nblk - 1 - i, 0)), _full((4, 256, 256)),
                             _full((1, D))),
                  scratch_shapes=[pltpu.VMEM((HALO, D), F32)],
                  compiler_params=_params(("arbitrary",)))(dyp, proj, proj, proj, pool_w, pool_scale)


def _dproj_specs(tb):
    return [pl.BlockSpec((tb, 2 * D), lambda i: (i, 0)), pl.BlockSpec((tb, D), lambda i: (i, 0)),
            pl.BlockSpec((tb, D), lambda i: (i, 0)), pl.BlockSpec((tb, 2 * D), lambda i: (i, 0))]


def _in_proj_bwd_x(x, dy, dpp, dus, dzs, dpg, mod3, norm_pre, w_in, dw_in_ssm, small32, small16, recv_w_in):
    rows = x.shape[0]
    tb = _tb(rows, 256)
    nblk = rows // tb
    items = [_w_in_block_item(0, 0, j, ssm_part=True) for j in range(W_IN_SHARD // W_IN_BLOCK)] + \
            [_Item(1, 1, _whole, _slot), _Item(2, 2, _whole, _slot)]

    def body(x_ref, dy_ref, dpp_ref, dus_ref, dzs_ref, dpg_ref, mod_ref, np_ref, w_ref,
             dw_src, s32_src, s16_src, _, gx_ref, vec_ref, recv_w, recv32, recv16, *sems):
        src_refs, recv_refs = (dw_src, s32_src, s16_src), (recv_w, recv32, recv16)

        @pl.when(pl.program_id(0) == 0)
        def _():
            _hosted_copies(items, src_refs, recv_refs, *sems, act="start")
            vec_ref[...] = jnp.zeros_like(vec_ref)

        dh = _dot_nt(dpp_ref[...], w_ref[:, 0:2 * D])
        dh += _dot_nt(dus_ref[...], w_ref[:, 2 * D:3 * D])
        dh += _dot_nt(dzs_ref[...], w_ref[:, 3 * D:4 * D])
        dh += _dot_nt(dpg_ref[...], w_ref[:, 4 * D:6 * D])
        xn, r, _ = _prenorm(x_ref[...], mod_ref[...], np_ref[...])
        one_scale = 1.0 + mod_ref[1:2, :]
        vec_ref[0:1, :] += jnp.sum(dh, axis=0, keepdims=True)
        vec_ref[1:2, :] += jnp.sum(dh * xn, axis=0, keepdims=True) * np_ref[...]
        vec_ref[2:3, :] += jnp.sum(dh * xn, axis=0, keepdims=True) * one_scale
        gx_ref[...] = dy_ref[...] + _rms_bwd(dh * (np_ref[...] * one_scale), xn, r)

        @pl.when(pl.program_id(0) == nblk - 1)
        def _():
            _hosted_copies(items, src_refs, recv_refs, *sems, act="wait")

    row = pl.BlockSpec((tb, D), lambda i: (i, 0))
    recv = (jax.ShapeDtypeStruct(recv_w_in.shape, recv_w_in.dtype),
            jax.ShapeDtypeStruct((N_DEV,) + small32.shape, small32.dtype),
            jax.ShapeDtypeStruct((N_DEV,) + small16.shape, small16.dtype))
    return _pcall(body, name="in_proj_bwd_x", grid=(nblk,),
                  out_shape=(jax.ShapeDtypeStruct((rows, D), F32), jax.ShapeDtypeStruct((3, D), F32), *recv),
                  in_specs=[row, row] + _dproj_specs(tb) + [_full((3, D)), _full((1, D)),
                                                            _full((D, N_IN), single=True)] + [ANY] * 4,
                  out_specs=(row, _full((3, D)), ANY, ANY, ANY),
                  input_output_aliases={12: 2},
                  scratch_shapes=_sem_scratch(items),
                  compiler_params=_params(("arbitrary",)))(x, dy, dpp, dus, dzs, dpg, mod3, norm_pre, w_in,
                                                           dw_in_ssm, small32, small16, recv_w_in)


def _in_proj_bwd_w(name, x, dparts, mod3, norm_pre):
    rows = x.shape[0]
    tb = _tb(rows, 256)
    nblk = rows // tb
    widths = [p.shape[1] for p in dparts]
    n_p = len(dparts)

    def body(x_ref, *rest):
        part_refs, (mod_ref, np_ref, dw_ref, acc) = rest[:n_p], rest[n_p:]
        i = pl.program_id(0)

        @pl.when(i == 0)
        def _():
            acc[...] = jnp.zeros_like(acc)

        _, _, h = _prenorm(x_ref[...], mod_ref[...], np_ref[...])
        ht = h.astype(BF16)
        lo = 0
        for ref, w in zip(part_refs, widths):
            acc[:, lo:lo + w] += _dot_tn(ht, ref[...])
            lo += w

        @pl.when(i == nblk - 1)
        def _():
            dw_ref[...] = acc[...].astype(BF16)

    row = pl.BlockSpec((tb, D), lambda i: (i, 0))
    return _pcall(body, name=name, grid=(nblk,),
                  out_shape=jax.ShapeDtypeStruct((D, sum(widths)), BF16),
                  in_specs=[row] + [pl.BlockSpec((tb, w), lambda i: (i, 0)) for w in widths] +
                           [_full((3, D)), _full((1, D))],
                  out_specs=_full((D, sum(widths))),
                  scratch_shapes=[pltpu.VMEM((D, sum(widths)), F32)],
                  compiler_params=_params(("arbitrary",)))(x, *dparts, mod3, norm_pre)


def _adamw_math(w, g, m, v):
    m = ADAM_B1 * m + (1.0 - ADAM_B1) * g
    v = ADAM_B2 * v + (1.0 - ADAM_B2) * (g * g)
    m_hat = m / (1.0 - ADAM_B1 ** ADAM_STEP)
    v_hat = v / (1.0 - ADAM_B2 ** ADAM_STEP)
    delta = -ADAM_LR * (m_hat / (jnp.sqrt(v_hat) + ADAM_EPS) + ADAM_WD * w)
    return delta, m, v


def _sum_sources(ref):
    g = ref[0].astype(F32)
    for s in range(1, N_DEV):
        g = g + ref[s].astype(F32)
    return g


def _adamw_reduce(name, parts, w, m, v):
    r, c = w.shape
    tr = r if r * c <= 256 * 1024 else max(8, (256 * 1024 // c) // 8 * 8)
    while r % tr:
        tr -= 8

    def body(p_ref, w_ref, m_ref, v_ref, g_ref, d_ref, nm_ref, nv_ref):
        g = _sum_sources(p_ref)
        g_ref[...] = g
        d_ref[...], nm_ref[...], nv_ref[...] = _adamw_math(w_ref[...], g, m_ref[...], v_ref[...])

    blk = pl.BlockSpec((tr, c), lambda i: (i, 0))
    return _pcall(body, name=name, grid=(r // tr,),
                  out_shape=tuple([jax.ShapeDtypeStruct((r, c), F32)] * 4),
                  in_specs=[pl.BlockSpec((N_DEV, tr, c), lambda i: (0, i, 0)), blk, blk, blk],
                  out_specs=(blk, blk, blk, blk),
                  compiler_params=_params(("arbitrary",)))(parts, w, m, v)


def _adamw_plain(name, g, w, m, v):
    def body(g_ref, w_ref, m_ref, v_ref, d_ref, nm_ref, nv_ref):
        d_ref[...], nm_ref[...], nv_ref[...] = _adamw_math(w_ref[...], g_ref[...], m_ref[...], v_ref[...])

    vm = pl.BlockSpec(memory_space=pltpu.VMEM)
    return _pcall(body, name=name, out_shape=tuple([jax.ShapeDtypeStruct(w.shape, F32)] * 3),
                  in_specs=[vm] * 4, out_specs=(vm, vm, vm), compiler_params=_params())(g, w, m, v)


def _sum_small(parts):
    n = len(parts)

    def body(*refs):
        for t in range(n):
            refs[n + t][...] = _sum_sources(refs[t])

    vm = pl.BlockSpec(memory_space=pltpu.VMEM)
    return _pcall(body, name="sum_small",
                  out_shape=tuple(jax.ShapeDtypeStruct(p.shape[1:], F32) for p in parts),
                  in_specs=[vm] * n, out_specs=tuple([vm] * n), compiler_params=_params())(*parts)


def _ada_update(c_all, dmod_cols, w, m, v):
    def body(c_ref, dm_ref, w_ref, m_ref, v_ref, g_ref, d_ref, nm_ref, nv_ref):
        ca = c_ref[...]
        g = lax.dot_general(ca * jax.nn.sigmoid(ca), dm_ref[...], (((0,), (0,)), ((), ())),
                            preferred_element_type=F32, precision=lax.Precision.HIGHEST)
        g_ref[...] = g
        d_ref[...], nm_ref[...], nv_ref[...] = _adamw_math(w_ref[...], g, m_ref[...], v_ref[...])

    vm = pl.BlockSpec(memory_space=pltpu.VMEM)
    return _pcall(body, name="ada_update", out_shape=tuple([jax.ShapeDtypeStruct(w.shape, F32)] * 4),
                  in_specs=[vm] * 5, out_specs=(vm, vm, vm, vm), compiler_params=_params())(c_all, dmod_cols, w, m, v)


def kernel(x, c, w_ada, b_ada, norm_pre, norm_post, w_in, pool_w, pool_scale, ssm_a_re, ssm_a_im, ssm_log_dt, ssm_b_re, ssm_b_im, ssm_c_re, ssm_c_im, ssm_d, glu_w, glu_b, w_branch_pool, w_branch_ssm, w_out, loss_target, m_w_ada, m_b_ada, m_norm_pre, m_norm_post, m_w_in, m_pool_w, m_pool_scale, m_ssm_a_re, m_ssm_a_im, m_ssm_log_dt, m_ssm_b_re, m_ssm_b_im, m_ssm_c_re, m_ssm_c_im, m_ssm_d, m_glu_w, m_glu_b, m_w_branch_pool, m_w_branch_ssm, m_w_out, v_w_ada, v_b_ada, v_norm_pre, v_norm_post, v_w_in, v_pool_w, v_pool_scale, v_ssm_a_re, v_ssm_a_im, v_ssm_log_dt, v_ssm_b_re, v_ssm_b_im, v_ssm_c_re, v_ssm_c_im, v_ssm_d, v_glu_w, v_glu_b, v_w_branch_pool, v_w_branch_ssm, v_w_out):
    given = dict(locals())
    me = _flat(_me())
    rows = x.shape[1]
    x2 = x[0]
    target = loss_target[0]
    ada_cols = w_ada.shape[2]

    b_ada_s = lax.dynamic_slice(b_ada, (0, me * ada_cols), (1, ada_cols))
    c_all, mod_rows = _ada_exchange(c, w_ada[0], b_ada_s)
    mod3 = mod_rows.reshape(3, D)

    shards = _cast_shards([w_in[0], pool_w[0], glu_w[0], w_branch_pool[0], w_branch_ssm[0], w_out[0]])
    (w_in_g,) = _exchange("gather_w_in", [shards[0]], [jax.ShapeDtypeStruct((D, N_IN), BF16)],
                          [_gather_item(0, 0, _cols_of(W_IN_SHARD))])

    tb_ssm = _tb(rows, 256)
    k_steps = tb_ssm // SUBLANES
    a_re, a_im = ssm_a_re[0], ssm_a_im[0]
    log_dt = ssm_log_dt[0].reshape(GROUPS, 1)
    b_re_t, b_im_t = ssm_b_re[0].transpose(0, 2, 1), ssm_b_im[0].transpose(0, 2, 1)
    bb_re, bb_im, pow_re, pow_im = _s5_prep(a_re, a_im, log_dt, b_re_t, b_im_t, k_steps)
    wb = jnp.concatenate([_blockdiag(bb_re), _blockdiag(bb_im)], axis=-1).astype(BF16)
    wct = jnp.concatenate([_blockdiag(ssm_c_re[0]), _blockdiag(-ssm_c_im[0])], axis=-1).astype(BF16)
    ptab = _state_layout(pow_re, pow_im)
    dvec = ssm_d[0].reshape(1, D)
    pm = _perm_matrix(tb_ssm)
    pmt = pm.T

    proj, pool_w_g, glu_g, wbp_g, wbs_g, wout_g = _in_proj(x2, mod3, norm_pre, w_in_g, shards[1:])
    y_pool = _pool_fwd(proj, pool_w_g, pool_scale)
    y_ssm, ys_pre, carries = _ssm_fwd(proj, pm, pmt, wb, wct, ptab, dvec, glu_g, glu_b)
    loss_part, dy, dyp, dys, dpg, dwbp, dwbs, dwout, head_vec = _head(
        x2, target, proj, y_pool, y_ssm, mod3, norm_post, wbp_g, wbs_g, wout_g)

    dpp, dpool_w, dpool_scale = _pool_bwd(dyp, proj, pool_w_g, pool_scale)
    dw_in_rest = _in_proj_bwd_w("in_proj_bwd_w_rest", x2, [dpp, dpg], mod3, norm_pre)
    dy_pre, dzs, dglu_w, dglu_b = _glu_bwd(dys, proj, ys_pre, pm, pmt, glu_g, glu_b)
    dus, dwb, dwct, dabar, dd, p_glu, p_wbp, p_wbs, p_wout, p_pool_w, p_w_in = _ssm_bwd(
        dy_pre, proj, carries, pm, pmt, wb, wct, ptab, dvec, [dglu_w, dwbp, dwbs, dwout], dpool_w, dw_in_rest)
    dw_in_ssm = _in_proj_bwd_w("in_proj_bwd_w_ssm", x2, [dus, dzs], mod3, norm_pre)

    small32 = jnp.concatenate([head_vec, dpool_scale, dglu_b, dd, jnp.zeros((3, D), F32), dabar.reshape(8, D)], axis=0)
    small16 = jnp.concatenate([
        _diag_blocks(dwb[:, :, :Q_W // 2]).reshape(64, D), _diag_blocks(dwb[:, :, Q_W // 2:]).reshape(64, D),
        _diag_blocks(dwct[:, :, :Q_W // 2]).reshape(64, D), _diag_blocks(dwct[:, :, Q_W // 2:]).reshape(64, D),
    ], axis=0).astype(BF16)
    grad_x, pre_vec, p_w_in, p_small32, p_small16 = _in_proj_bwd_x(
        x2, dy, dpp, dus, dzs, dpg, mod3, norm_pre, w_in_g, dw_in_ssm, small32, small16, p_w_in)
    small_pre = jnp.concatenate([pre_vec, jnp.zeros((5, D), F32)], axis=0)
    (p_pre,) = _exchange("gather_prenorm_sums", [small_pre], [jax.ShapeDtypeStruct((N_DEV, 8, D), F32)],
                         [_Item(0, 0, _whole, _slot)])

    tot32, tot16, tot_pre = _sum_small([p_small32, p_small16, p_pre])
    d_abar_re, d_abar_im = _state_unlayout(tot32[8:16].reshape(N_STATE))
    d_bb_re, d_bb_im = tot16[0:64].reshape(GROUPS, G_H, G_P), tot16[64:128].reshape(GROUPS, G_H, G_P)
    g_a_re, g_a_im, g_log_dt, g_b_re_t, g_b_im_t = _s5_prep_bwd(
        a_re, a_im, log_dt, b_re_t, b_im_t, d_abar_re, d_abar_im, d_bb_re, d_bb_im)

    grads, deltas, new_m, new_v = {}, {}, {}, {}

    def small_update(name, g2):
        shape = given[name].shape
        w2, m2, v2 = (given[p + name].reshape(g2.shape) for p in ("", "m_", "v_"))
        d2, nm2, nv2 = _adamw_plain("adamw_" + name, g2, w2, m2, v2)
        grads[name], deltas[name], new_m[name], new_v[name] = (a.reshape(shape) for a in (g2, d2, nm2, nv2))

    def shard_update(name, parts):
        shape = given[name].shape
        r2 = parts.shape[1:] if parts.ndim == 3 else (parts.shape[1] * parts.shape[2], parts.shape[3])
        w2, m2, v2 = (given[p + name].reshape(r2) for p in ("", "m_", "v_"))
        out = _adamw_reduce("adamw_" + name, parts.reshape((N_DEV,) + tuple(r2)), w2, m2, v2)
        grads[name], deltas[name], new_m[name], new_v[name] = (a.reshape(shape) for a in out)

    dmod_all = jnp.concatenate([p_pre[:, 0:2, :], p_small32[:, 0:1, :]], axis=1).reshape(N_DEV, 3 * D)
    dmod_cols = lax.dynamic_slice(dmod_all, (0, me * ada_cols), (N_DEV, ada_cols))
    out = _ada_update(c_all, dmod_cols, w_ada[0], m_w_ada[0], v_w_ada[0])
    grads['w_ada'], deltas['w_ada'], new_m['w_ada'], new_v['w_ada'] = (a.reshape(w_ada.shape) for a in out)

    small_update('b_ada', jnp.concatenate([tot_pre[0:2], tot32[0:1]], axis=0).reshape(1, 3 * D))
    small_update('norm_pre', tot_pre[2:3])
    small_update('norm_post', tot32[1:2])
    small_update('pool_scale', tot32[2:3])
    small_update('glu_b', tot32[3:4])
    small_update('ssm_d', tot32[4:5])
    small_update('ssm_a_re', g_a_re)
    small_update('ssm_a_im', g_a_im)
    small_update('ssm_log_dt', g_log_dt.reshape(1, GROUPS))
    small_update('ssm_b_re', g_b_re_t.transpose(0, 2, 1).reshape(GROUPS, G_P * G_H))
    small_update('ssm_b_im', g_b_im_t.transpose(0, 2, 1).reshape(GROUPS, G_P * G_H))
    small_update('ssm_c_re', tot16[128:192])
    small_update('ssm_c_im', -tot16[192:256])
    shard_update('w_in', p_w_in)
    shard_update('pool_w', p_pool_w)
    shard_update('glu_w', p_glu)
    shard_update('w_branch_pool', p_wbp)
    shard_update('w_branch_ssm', p_wbs)
    shard_update('w_out', p_wout)

    loss = lax.psum(loss_part[0, 0], ("x", "y", "c"))
    return (loss, grad_x[None], *[grads[n] for n in WEIGHTS], *[deltas[n] for n in WEIGHTS],
            *[new_m[n] for n in WEIGHTS], *[new_v[n] for n in WEIGHTS])
```

```python
import functools
import math
from typing import Callable, NamedTuple, Optional

import jax
import jax.numpy as jnp
from jax import lax
from jax.experimental import pallas as pl
from jax.experimental.pallas import tpu as pltpu

F32 = jnp.float32
BF16 = jnp.bfloat16
MESH = pl.DeviceIdType.MESH

D = 1024
N_DEV = 8
N_IN = 6 * D
GROUPS = 64
G_H = 16
G_P = 64
N_Q = 4
Q_W = 2 * 16 * G_P
N_STATE = N_Q * Q_W
POOL_WINDOWS = (2, 4, 8, 16)
HALO = 16
RMS_EPS = 1e-6
SUBLANES = 8
LANE_CHUNK = 512
SCAN_UNROLL = 2
VMEM_LIMIT = 56 * 1024 * 1024

ADAM_LR = 0.001
ADAM_B1 = 0.9
ADAM_B2 = 0.999
ADAM_EPS = 1e-08
ADAM_WD = 0.01
ADAM_STEP = 10

WEIGHTS = ['w_ada', 'b_ada', 'norm_pre', 'norm_post', 'w_in', 'pool_w', 'pool_scale', 'ssm_a_re',
           'ssm_a_im', 'ssm_log_dt', 'ssm_b_re', 'ssm_b_im', 'ssm_c_re', 'ssm_c_im', 'ssm_d', 'glu_w',
           'glu_b', 'w_branch_pool', 'w_branch_ssm', 'w_out']


def _pcall(body, **kw):
    return pl.pallas_call(body, **kw)


def _params(sem=None, vmem=VMEM_LIMIT):
    return pltpu.CompilerParams(dimension_semantics=sem, vmem_limit_bytes=vmem)


def _tb(rows, pref):
    return pref if rows % pref == 0 and rows // pref >= 2 else rows // 2


def _full(shape, single=False):
    nd = len(shape)
    if single:
        return pl.BlockSpec(shape, lambda i: (0,) * nd, pipeline_mode=pl.Buffered(1))
    return pl.BlockSpec(shape, lambda i: (0,) * nd)


ANY = pl.BlockSpec(memory_space=pl.ANY)


def _me():
    return lax.axis_index("x"), lax.axis_index("y"), lax.axis_index("c")


def _flat(p):
    return 4 * p[0] + 2 * p[1] + p[2]


def _peer(k):
    x, y, c = _me()
    return (1 - x if k & 4 else x, 1 - y if k & 2 else y, 1 - c if k & 1 else c)


def _silu_parts(z):
    s = jax.nn.sigmoid(z)
    return z * s, s * (1.0 + z * (1.0 - s))


_GELU_C = math.sqrt(2.0 / math.pi)


def _gelu_parts(x):
    x2 = x * x
    t = jnp.tanh(_GELU_C * (x + 0.044715 * x * x2))
    g = 0.5 * x * (1.0 + t)
    dg = 0.5 * (1.0 + t) + 0.5 * x * (1.0 - t * t) * (_GELU_C * (1.0 + 3.0 * 0.044715 * x2))
    return g, dg


def _dot(a, b):
    return jnp.dot(a, b, preferred_element_type=F32)


def _dot_nt(a, b):
    return lax.dot_general(a, b, (((1,), (1,)), ((), ())), preferred_element_type=F32)


def _dot_tn(a, b):
    return lax.dot_general(a, b, (((0,), (0,)), ((), ())), preferred_element_type=F32)


def _rms_parts(x):
    r = lax.rsqrt(jnp.mean(x * x, axis=-1, keepdims=True) + RMS_EPS)
    return x * r, r


def _rms_bwd(dxn, xn, r):
    return r * (dxn - xn * jnp.mean(dxn * xn, axis=-1, keepdims=True))


def _ada_exchange(c, w_ada_s, b_ada_s):
    cols = w_ada_s.shape[1]

    def body(c_ref, w_ref, b_ref, call_ref, mod_ref, part_ref, ssem, rsem, lsem):
        me3 = _me()
        me = _flat(me3)
        mine = pltpu.make_async_copy(c_ref, call_ref.at[pl.ds(me, 1), :], lsem.at[0])
        mine.start()
        sends = []
        for k in range(1, N_DEV):
            cp = pltpu.make_async_remote_copy(src_ref=c_ref, dst_ref=call_ref.at[pl.ds(me, 1), :],
                                              send_sem=ssem.at[k - 1], recv_sem=rsem.at[k - 1],
                                              device_id=_peer(k), device_id_type=MESH)
            cp.start()
            sends.append(cp)
        mine.wait()
        for k in range(1, N_DEV):
            p = _flat(_peer(k))
            pltpu.make_async_remote_copy(src_ref=c_ref, dst_ref=call_ref.at[pl.ds(p, 1), :],
                                         send_sem=ssem.at[k - 1], recv_sem=rsem.at[k - 1],
                                         device_id=_peer(k), device_id_type=MESH).wait_recv()
        for cp in sends:
            cp.wait_send()
        ca = call_ref[...]
        act = ca * jax.nn.sigmoid(ca)
        part_ref[...] = jnp.dot(act, w_ref[...], preferred_element_type=F32,
                                precision=lax.Precision.HIGHEST) + b_ref[...]
        own = pltpu.make_async_copy(part_ref.at[pl.ds(me, 1), :], mod_ref.at[pl.ds(me, 1), :], lsem.at[1])
        own.start()
        sends = []
        for k in range(1, N_DEV):
            p = _flat(_peer(k))
            s = N_DEV - 1 + k - 1
            cp = pltpu.make_async_remote_copy(src_ref=part_ref.at[pl.ds(p, 1), :],
                                              dst_ref=mod_ref.at[pl.ds(me, 1), :],
                                              send_sem=ssem.at[s], recv_sem=rsem.at[s],
                                              device_id=_peer(k), device_id_type=MESH)
            cp.start()
            sends.append(cp)
        own.wait()
        for k in range(1, N_DEV):
            p = _flat(_peer(k))
            s = N_DEV - 1 + k - 1
            pltpu.make_async_remote_copy(src_ref=part_ref.at[pl.ds(p, 1), :],
                                         dst_ref=mod_ref.at[pl.ds(p, 1), :],
                                         send_sem=ssem.at[s], recv_sem=rsem.at[s],
                                         device_id=_peer(k), device_id_type=MESH).wait_recv()
        for cp in sends:
            cp.wait_send()

    vm = pl.BlockSpec(memory_space=pltpu.VMEM)
    return _pcall(
        body, name="ada_exchange",
        out_shape=(jax.ShapeDtypeStruct((N_DEV, D), F32), jax.ShapeDtypeStruct((N_DEV, cols), F32)),
        in_specs=[vm, vm, vm], out_specs=(vm, vm),
        scratch_shapes=[pltpu.VMEM((N_DEV, cols), F32),
                        pltpu.SemaphoreType.DMA((2 * (N_DEV - 1),)),
                        pltpu.SemaphoreType.DMA((2 * (N_DEV - 1),)),
                        pltpu.SemaphoreType.DMA((2,))],
    )(c, w_ada_s, b_ada_s)


class _Item(NamedTuple):
    src: int
    out: int
    src_view: Callable
    dst_view: Callable
    pred: Optional[Callable] = None


def _when(pred, dest, fn):
    if pred is None:
        fn()
    else:
        pl.when(pred(dest))(fn)


def _n_sems(items):
    return len(items) * (N_DEV - 1)


def _hosted_copies(items, srcs, outs, ssem, rsem, lsem, act):
    me = _flat(_me())
    for t, it in enumerate(items):
        local = lambda t=t, it=it: pltpu.make_async_copy(
            it.src_view(srcs[it.src], me), it.dst_view(outs[it.out], me), lsem.at[t])
        if act == "start":
            _when(it.pred, me, lambda local=local: local().start())
        else:
            _when(it.pred, me, lambda local=local: local().wait())
    for k in range(1, N_DEV):
        p3 = _peer(k)
        p = _flat(p3)
        for t, it in enumerate(items):
            s = t * (N_DEV - 1) + k - 1
            send = lambda it=it, s=s, p=p, p3=p3: pltpu.make_async_remote_copy(
                src_ref=it.src_view(srcs[it.src], p), dst_ref=it.dst_view(outs[it.out], me),
                send_sem=ssem.at[s], recv_sem=rsem.at[s], device_id=p3, device_id_type=MESH)
            recv = lambda it=it, s=s, p=p, p3=p3: pltpu.make_async_remote_copy(
                src_ref=it.src_view(srcs[it.src], p), dst_ref=it.dst_view(outs[it.out], p),
                send_sem=ssem.at[s], recv_sem=rsem.at[s], device_id=p3, device_id_type=MESH)
            if act == "start":
                _when(it.pred, p, lambda send=send: send().start())
            else:
                _when(it.pred, me, lambda recv=recv: recv().wait_recv())
                _when(it.pred, p, lambda send=send: send().wait_send())


def _sem_scratch(items):
    return [pltpu.SemaphoreType.DMA((_n_sems(items),)), pltpu.SemaphoreType.DMA((_n_sems(items),)),
            pltpu.SemaphoreType.DMA((len(items),))]


def _exchange(name, srcs, out_structs, items):
    n_src, n_out = len(srcs), len(out_structs)

    def body(*refs):
        src_refs, out_refs = refs[:n_src], refs[n_src:n_src + n_out]
        sems = refs[n_src + n_out:]
        _hosted_copies(items, src_refs, out_refs, *sems, act="start")
        _hosted_copies(items, src_refs, out_refs, *sems, act="wait")

    return _pcall(body, name=name, out_shape=tuple(out_structs),
                  in_specs=[ANY] * n_src, out_specs=tuple([ANY] * n_out),
                  scratch_shapes=_sem_scratch(items))(*srcs)


def _whole(ref, dest):
    return ref


def _slot(ref, sender):
    return ref.at[sender]


def _rows_of(rows):
    return lambda ref, dev: ref.at[pl.ds(dev * rows, rows), :]


def _cols_of(cols):
    return lambda ref, dev: ref.at[:, pl.ds(dev * cols, cols)]


def _pool_rows_of(rows):
    return lambda ref, dev: ref.at[:, pl.ds(dev * rows, rows), :]


def _gather_item(src, out, dst_view):
    return _Item(src, out, _whole, dst_view)


def _scatter_item(src, out, src_view):
    return _Item(src, out, src_view, _slot)


W_IN_BLOCK = 256
W_IN_SHARD = N_IN // N_DEV
SSM_BLOCKS = (2 * D // W_IN_BLOCK, 4 * D // W_IN_BLOCK)


def _w_in_block_item(src, out, j, ssm_part):
    def block(dest):
        return (W_IN_SHARD // W_IN_BLOCK) * dest + j

    def in_ssm(dest):
        b = block(dest)
        return (b >= SSM_BLOCKS[0]) & (b < SSM_BLOCKS[1])

    def src_view(ref, dest):
        b = block(dest)
        local = b - SSM_BLOCKS[0] if ssm_part else jnp.where(b < SSM_BLOCKS[0], b, b - (SSM_BLOCKS[1] - SSM_BLOCKS[0]))
        local = jnp.clip(local, 0, ref.shape[1] // W_IN_BLOCK - 1)
        return ref.at[:, pl.ds(local * W_IN_BLOCK, W_IN_BLOCK)]

    def dst_view(ref, sender):
        return ref.at[sender, :, pl.ds(j * W_IN_BLOCK, W_IN_BLOCK)]

    pred = in_ssm if ssm_part else (lambda dest: jnp.logical_not(in_ssm(dest)))
    return _Item(src, out, src_view, dst_view, pred)


def _cast_shards(arrs):
    def body(*refs):
        n = len(refs) // 2
        for i in range(n):
            refs[n + i][...] = refs[i][...].astype(BF16)

    vm = pl.BlockSpec(memory_space=pltpu.VMEM)
    return _pcall(body, name="cast_shards",
                  out_shape=tuple(jax.ShapeDtypeStruct(a.shape, BF16) for a in arrs),
                  in_specs=[vm] * len(arrs), out_specs=tuple([vm] * len(arrs)),
                  compiler_params=_params())(*arrs)


def _s5_discretise(a_re, a_im, log_dt, b_re_t, b_im_t):
    dt = jnp.exp(log_dt)
    lam_re = jnp.minimum(a_re, -1e-4)
    lam_im = a_im
    mag = jnp.exp(lam_re * dt)
    abar_re = mag * jnp.cos(lam_im * dt)
    abar_im = mag * jnp.sin(lam_im * dt)
    den = lam_re * lam_re + lam_im * lam_im
    num_re = abar_re - 1.0
    f_re = (num_re * lam_re + abar_im * lam_im) / den
    f_im = (abar_im * lam_re - num_re * lam_im) / den
    f_re, f_im = f_re[:, None, :], f_im[:, None, :]
    bb_re = f_re * b_re_t - f_im * b_im_t
    bb_im = f_re * b_im_t + f_im * b_re_t
    return abar_re, abar_im, bb_re, bb_im


def _s5_prep(a_re, a_im, log_dt, b_re_t, b_im_t, n_pow):
    def body(ar_ref, ai_ref, ld_ref, br_ref, bi_ref, bbr_ref, bbi_ref, pr_ref, pi_ref):
        abar_re, abar_im, bb_re, bb_im = _s5_discretise(ar_ref[...], ai_ref[...], ld_ref[...], br_ref[...], bi_ref[...])
        bbr_ref[...] = bb_re
        bbi_ref[...] = bb_im
        p_re, p_im = abar_re, abar_im
        pr_ref[0] = p_re
        pi_ref[0] = p_im
        for k in range(1, n_pow):
            p_re, p_im = p_re * abar_re - p_im * abar_im, p_re * abar_im + p_im * abar_re
            pr_ref[k] = p_re
            pi_ref[k] = p_im

    vm = pl.BlockSpec(memory_space=pltpu.VMEM)
    return _pcall(body, name="s5_prep",
                  out_shape=(jax.ShapeDtypeStruct(b_re_t.shape, F32), jax.ShapeDtypeStruct(b_re_t.shape, F32),
                             jax.ShapeDtypeStruct((n_pow, GROUPS, G_P), F32),
                             jax.ShapeDtypeStruct((n_pow, GROUPS, G_P), F32)),
                  in_specs=[vm] * 5, out_specs=(vm, vm, vm, vm), compiler_params=_params(),
                  )(a_re, a_im, log_dt, b_re_t, b_im_t)


def _s5_prep_bwd(a_re, a_im, log_dt, b_re_t, b_im_t, d_abar_re, d_abar_im, d_bb_re, d_bb_im):
    def body(ar_ref, ai_ref, ld_ref, br_ref, bi_ref, dar_ref, dai_ref, dbr_ref, dbi_ref,
             gar_ref, gai_ref, gld_ref, gbr_ref, gbi_ref):
        _, vjp = jax.vjp(_s5_discretise, ar_ref[...], ai_ref[...], ld_ref[...], br_ref[...], bi_ref[...])
        g = vjp((dar_ref[...], dai_ref[...], dbr_ref[...], dbi_ref[...]))
        gar_ref[...] = g[0]
        gai_ref[...] = g[1]
        gld_ref[...] = g[2]
        gbr_ref[...] = g[3]
        gbi_ref[...] = g[4]

    vm = pl.BlockSpec(memory_space=pltpu.VMEM)
    ins = (a_re, a_im, log_dt, b_re_t, b_im_t)
    return _pcall(body, name="s5_prep_bwd",
                  out_shape=tuple(jax.ShapeDtypeStruct(a.shape, F32) for a in ins),
                  in_specs=[vm] * 9, out_specs=tuple([vm] * 5), compiler_params=_params(),
                  )(*ins, d_abar_re, d_abar_im, d_bb_re, d_bb_im)


def _blockdiag(t):
    t4 = t.reshape(N_Q, 16, G_H, G_P)
    eye = jnp.eye(16, dtype=t.dtype)
    return (t4[:, :, :, None, :] * eye[None, :, None, :, None]).reshape(N_Q, 16 * G_H, 16 * G_P)


def _diag_blocks(m):
    m5 = m.reshape(N_Q, 16, G_H, 16, G_P)
    eye = jnp.eye(16, dtype=m.dtype)
    return (m5 * eye[None, :, None, :, None]).sum(axis=3).reshape(GROUPS, G_H, G_P)


def _state_layout(re, im):
    lead = re.shape[:-2]
    r = re.reshape(lead + (N_Q, 1, 16 * G_P))
    i = im.reshape(lead + (N_Q, 1, 16 * G_P))
    return jnp.concatenate([r, i], axis=-2).reshape(lead + (N_STATE,))


def _state_unlayout(v):
    v4 = v.reshape(N_Q, 2, 16, G_P)
    return v4[:, 0].reshape(GROUPS, G_P), v4[:, 1].reshape(GROUPS, G_P)


def _perm_matrix(tb):
    k_steps = tb // SUBLANES
    r = jnp.arange(tb)
    src = (r % SUBLANES) * k_steps + r // SUBLANES
    return (src[:, None] == jnp.arange(tb)[None, :]).astype(BF16)


def _lane_chunks():
    for q in range(N_Q):
        for lc in range(Q_W // 2 // LANE_CHUNK):
            re = q * Q_W + lc * LANE_CHUNK
            yield re, re + Q_W // 2


def _tile(k):
    if isinstance(k, int):
        return pl.ds(k * SUBLANES, SUBLANES)
    return pl.ds(pl.multiple_of(k * SUBLANES, SUBLANES), SUBLANES)


def _scan_forward(s_ref, p_ref, carry_ref, enter_ref, fin_ref, k_steps):
    for re, im in _lane_chunks():
        lr, li = pl.ds(re, LANE_CHUNK), pl.ds(im, LANE_CHUNK)
        a_re = jnp.broadcast_to(p_ref[0:1, lr], (SUBLANES, LANE_CHUNK))
        a_im = jnp.broadcast_to(p_ref[0:1, li], (SUBLANES, LANE_CHUNK))

        def local(k, st):
            sr, si = st
            rows = _tile(k)
            nr = a_re * sr - a_im * si + s_ref[rows, lr]
            ni = a_re * si + a_im * sr + s_ref[rows, li]
            s_ref[rows, lr] = nr
            s_ref[rows, li] = ni
            return nr, ni

        zero = jnp.zeros((SUBLANES, LANE_CHUNK), F32)
        fr, fi = lax.fori_loop(0, k_steps, local, (zero, zero), unroll=SCAN_UNROLL)
        fin_ref[:, lr] = fr
        fin_ref[:, li] = fi
        ak_re, ak_im = p_ref[k_steps - 1:k_steps, lr], p_ref[k_steps - 1:k_steps, li]
        c_re, c_im = carry_ref[:, lr], carry_ref[:, li]
        for seg in range(SUBLANES):
            enter_ref[seg:seg + 1, lr] = c_re
            enter_ref[seg:seg + 1, li] = c_im
            f_re, f_im = fin_ref[seg:seg + 1, lr], fin_ref[seg:seg + 1, li]
            c_re, c_im = f_re + ak_re * c_re - ak_im * c_im, f_im + ak_re * c_im + ak_im * c_re
        carry_ref[:, lr] = c_re
        carry_ref[:, li] = c_im
        e_re, e_im = enter_ref[:, lr], enter_ref[:, li]

        def fix(k, _):
            rows = _tile(k)
            p_re = p_ref[pl.ds(k, 1), lr]
            p_im = p_ref[pl.ds(k, 1), li]
            s_ref[rows, lr] = s_ref[rows, lr] + (p_re * e_re - p_im * e_im)
            s_ref[rows, li] = s_ref[rows, li] + (p_re * e_im + p_im * e_re)
            return 0

        lax.fori_loop(0, k_steps, fix, 0, unroll=SCAN_UNROLL)


def _scan_backward(g_ref, s_ref, p_ref, carry_ref, s_in_ref, fin_ref, da_ref, k_steps):
    seg_id = lax.broadcasted_iota(jnp.int32, (SUBLANES, LANE_CHUNK), 0)
    for re, im in _lane_chunks():
        lr, li = pl.ds(re, LANE_CHUNK), pl.ds(im, LANE_CHUNK)
        a_re = jnp.broadcast_to(p_ref[0:1, lr], (SUBLANES, LANE_CHUNK))
        a_im = jnp.broadcast_to(p_ref[0:1, li], (SUBLANES, LANE_CHUNK))

        def local(j, st):
            sr, si = st
            rows = _tile(k_steps - 1 - j)
            nr = a_re * sr + a_im * si + g_ref[rows, lr]
            ni = a_re * si - a_im * sr + g_ref[rows, li]
            g_ref[rows, lr] = nr
            g_ref[rows, li] = ni
            return nr, ni

        zero = jnp.zeros((SUBLANES, LANE_CHUNK), F32)
        fr, fi = lax.fori_loop(0, k_steps, local, (zero, zero), unroll=SCAN_UNROLL)
        fin_ref[:, lr] = fr
        fin_ref[:, li] = fi
        ak_re, ak_im = p_ref[k_steps - 1:k_steps, lr], p_ref[k_steps - 1:k_steps, li]
        c_re, c_im = carry_ref[:, lr], carry_ref[:, li]
        lam_in = [None] * SUBLANES
        for seg in reversed(range(SUBLANES)):
            lam_in[seg] = (c_re, c_im)
            f_re, f_im = fin_ref[seg:seg + 1, lr], fin_ref[seg:seg + 1, li]
            c_re, c_im = f_re + ak_re * c_re + ak_im * c_im, f_im + ak_re * c_im - ak_im * c_re
        carry_ref[:, lr] = c_re
        carry_ref[:, li] = c_im
        for seg in range(SUBLANES):
            fin_ref[seg:seg + 1, lr] = lam_in[seg][0]
            fin_ref[seg:seg + 1, li] = lam_in[seg][1]
        e_re, e_im = fin_ref[:, lr], fin_ref[:, li]

        def fix_with(k, acc, sp_re, sp_im):
            acc_re, acc_im = acc
            rows = _tile(k)
            p_re = p_ref[pl.ds(k_steps - 1 - k, 1), lr]
            p_im = p_ref[pl.ds(k_steps - 1 - k, 1), li]
            l_re = g_ref[rows, lr] + (p_re * e_re + p_im * e_im)
            l_im = g_ref[rows, li] + (p_re * e_im - p_im * e_re)
            g_ref[rows, lr] = l_re
            g_ref[rows, li] = l_im
            return acc_re + (l_re * sp_re + l_im * sp_im), acc_im + (l_im * sp_re - l_re * sp_im)

        def fix(k, acc):
            prev = _tile(k - 1)
            return fix_with(k, acc, s_ref[prev, lr], s_ref[prev, li])

        last = _tile(k_steps - 1)
        before_re = jnp.where(seg_id == 0, s_in_ref[:, lr], pltpu.roll(s_ref[last, lr], 1, axis=0))
        before_im = jnp.where(seg_id == 0, s_in_ref[:, li], pltpu.roll(s_ref[last, li], 1, axis=0))
        acc = fix_with(0, (zero, zero), before_re, before_im)
        acc_re, acc_im = lax.fori_loop(1, k_steps, fix, acc)
        da_ref[:, lr] = da_ref[:, lr] + jnp.sum(acc_re, axis=0, keepdims=True)
        da_ref[:, li] = da_ref[:, li] + jnp.sum(acc_im, axis=0, keepdims=True)


def _prenorm(x, mod3, norm_pre):
    xn, r = _rms_parts(x)
    return xn, r, xn * norm_pre * (1.0 + mod3[1:2, :]) + mod3[0:1, :]


def _in_proj(x, mod3, norm_pre, w_in, shards):
    rows = x.shape[0]
    tb = _tb(rows, 512)
    nblk = rows // tb
    n_sh = len(shards)
    pool_rows = shards[0].shape[1]
    w_rows = shards[1].shape[0]
    items = [_gather_item(0, 0, _pool_rows_of(pool_rows))] + \
            [_gather_item(t, t, _rows_of(w_rows)) for t in range(1, n_sh)]

    def body(x_ref, mod_ref, np_ref, w_ref, *rest):
        src_refs, proj_ref, out_refs, sems = rest[:n_sh], rest[n_sh], rest[n_sh + 1:2 * n_sh + 1], rest[2 * n_sh + 1:]
        i = pl.program_id(0)

        @pl.when(i == 0)
        def _():
            _hosted_copies(items, src_refs, out_refs, *sems, act="start")

        _, _, h = _prenorm(x_ref[...], mod_ref[...], np_ref[...])
        hb = h.astype(BF16)
        for j in range(N_IN // D):
            cols = slice(j * D, (j + 1) * D)
            proj_ref[:, cols] = _dot(hb, w_ref[:, cols]).astype(BF16)

        @pl.when(i == nblk - 1)
        def _():
            _hosted_copies(items, src_refs, out_refs, *sems, act="wait")

    full = [jax.ShapeDtypeStruct((4, 256, 256), BF16)] + [jax.ShapeDtypeStruct((D, D), BF16)] * (n_sh - 1)
    return _pcall(body, name="in_proj", grid=(nblk,),
                  out_shape=(jax.ShapeDtypeStruct((rows, N_IN), BF16), *full),
                  in_specs=[pl.BlockSpec((tb, D), lambda i: (i, 0)), _full((3, D)), _full((1, D)),
                            _full((D, N_IN), single=True)] + [ANY] * n_sh,
                  out_specs=(pl.BlockSpec((tb, N_IN), lambda i: (i, 0)), *([ANY] * n_sh)),
                  scratch_shapes=_sem_scratch(items),
                  compiler_params=_params(("arbitrary",)))(x, mod3, norm_pre, w_in, *shards)


def _pool_windows(ext, tb, first_row):
    pos = (first_row + lax.broadcasted_iota(jnp.int32, (tb, 1), 0) + 1).astype(F32)
    pooled, counts = [], []
    for g, w in enumerate(POOL_WINDOWS):
        acc = ext[:, g * 256:(g + 1) * 256]
        tok = acc[HALO:, :]
        s = 1
        while s < w:
            acc = acc + pltpu.roll(acc, s, axis=0)
            s *= 2
        cnt = jnp.minimum(pos, float(w))
        pooled.append(acc[HALO:, :] / cnt - tok)
        counts.append(cnt)
    return pooled, counts


def _pool_fwd(proj, pool_w, pool_scale):
    rows = proj.shape[0]
    tb = _tb(rows, 512)
    hb = tb // HALO

    def body(u_ref, halo_ref, z_ref, pw_ref, ps_ref, y_ref):
        i = pl.program_id(0)
        u = u_ref[...].astype(F32)
        halo = jnp.where(i > 0, halo_ref[...].astype(F32), 0.0)
        pooled, _ = _pool_windows(jnp.concatenate([halo, u], axis=0), tb, i * tb)
        silu_z, _ = _silu_parts(z_ref[...].astype(F32))
        for g in range(4):
            cols = slice(g * 256, (g + 1) * 256)
            mixed = _dot(pooled[g].astype(BF16), pw_ref[g])
            y_ref[:, cols] = (mixed * ps_ref[:, cols] * silu_z[:, cols]).astype(BF16)

    return _pcall(body, name="pool_fwd", grid=(rows // tb,),
                  out_shape=jax.ShapeDtypeStruct((rows, D), BF16),
                  in_specs=[pl.BlockSpec((tb, D), lambda i: (i, 0)),
                            pl.BlockSpec((HALO, D), lambda i: (jnp.maximum(i * hb - 1, 0), 0)),
                            pl.BlockSpec((tb, D), lambda i: (i, 1)),
                            _full((4, 256, 256)), _full((1, D))],
                  out_specs=pl.BlockSpec((tb, D), lambda i: (i, 0)),
                  compiler_params=_params(("arbitrary",)))(proj, proj, proj, pool_w, pool_scale)


def _ssm_fwd(proj, pm, pmt, wb, wct, ptab, dvec, glu_w, glu_b):
    rows = proj.shape[0]
    tb = pm.shape[0]
    k_steps = tb // SUBLANES
    nblk = rows // tb

    def body(u_ref, z_ref, pm_ref, pmt_ref, wb_ref, wct_ref, p_ref, d_ref, gw_ref, gb_ref,
             y_ref, ys_ref, carry_out_ref, s_ref, carry_ref, enter_ref, fin_ref):
        @pl.when(pl.program_id(0) == 0)
        def _():
            carry_ref[...] = jnp.zeros_like(carry_ref)

        carry_out_ref[...] = carry_ref[...]
        up = _dot(pm_ref[...], u_ref[...]).astype(BF16)
        for q in range(N_Q):
            s_ref[:, q * Q_W:(q + 1) * Q_W] = _dot(up[:, q * 256:(q + 1) * 256], wb_ref[q])
        _scan_forward(s_ref, p_ref, carry_ref, enter_ref, fin_ref, k_steps)
        for q in range(N_Q):
            cols = slice(q * 256, (q + 1) * 256)
            y = _dot_nt(s_ref[:, q * Q_W:(q + 1) * Q_W].astype(BF16), wct_ref[q])
            ys_ref[:, cols] = y + d_ref[:, cols] * up[:, cols].astype(F32)
        yg, _ = _gelu_parts(ys_ref[...])
        gate = jax.nn.sigmoid(_dot(yg.astype(BF16), gw_ref[...]) + gb_ref[...])
        zp = _dot(pm_ref[...], z_ref[...])
        silu_z, _ = _silu_parts(zp)
        y_ref[...] = _dot(pmt_ref[...], (yg * gate * silu_z).astype(BF16)).astype(BF16)

    return _pcall(body, name="ssm_fwd", grid=(nblk,),
                  out_shape=(jax.ShapeDtypeStruct((rows, D), BF16), jax.ShapeDtypeStruct((rows, D), F32),
                             jax.ShapeDtypeStruct((nblk, 1, N_STATE), F32),
                             jax.ShapeDtypeStruct((rows, N_STATE), F32)),
                  in_specs=[pl.BlockSpec((tb, D), lambda i: (i, 2)), pl.BlockSpec((tb, D), lambda i: (i, 3)),
                            _full((tb, tb)), _full((tb, tb)),
                            _full((N_Q, 256, Q_W), single=True), _full((N_Q, 256, Q_W), single=True),
                            _full((k_steps, N_STATE)), _full((1, D)), _full((D, D), single=True), _full((1, D))],
                  out_specs=(pl.BlockSpec((tb, D), lambda i: (i, 0)), pl.BlockSpec((tb, D), lambda i: (i, 0)),
                             pl.BlockSpec((None, 1, N_STATE), lambda i: (i, 0, 0)),
                             pl.BlockSpec((tb, N_STATE), lambda i: (i, 0))),
                  scratch_shapes=[pltpu.VMEM((1, N_STATE), F32),
                                  pltpu.VMEM((SUBLANES, N_STATE), F32), pltpu.VMEM((SUBLANES, N_STATE), F32)],
                  compiler_params=_params(("arbitrary",)))(proj, proj, pm, pmt, wb, wct, ptab, dvec, glu_w, glu_b)


def _head(x, target, proj, y_pool, y_ssm, mod3, norm_post, wbp, wbs, wout):
    rows = x.shape[0]
    tb = _tb(rows, 256)
    nblk = rows // tb
    n_feat = float(D)

    def body(x_ref, t_ref, gp_ref, gs_ref, yp_ref, ys_ref, mod_ref, npost_ref, wbp_ref, wbs_ref, wout_ref,
             loss_ref, dy_ref, dyp_ref, dys_ref, dg_ref, dwbp_hbm, dwbs_hbm, dwout_hbm, vec_ref,
             acc_bp, acc_bs, acc_out, acc_loss, acc_vec):
        i = pl.program_id(0)

        @pl.when(i == 0)
        def _():
            acc_bp[...] = jnp.zeros_like(acc_bp)
            acc_bs[...] = jnp.zeros_like(acc_bs)
            acc_out[...] = jnp.zeros_like(acc_out)
            acc_loss[...] = jnp.zeros_like(acc_loss)
            acc_vec[...] = jnp.zeros_like(acc_vec)

        yp, ys = yp_ref[...], ys_ref[...]
        sgp = jax.nn.sigmoid(gp_ref[...].astype(F32))
        sgs = jax.nn.sigmoid(gs_ref[...].astype(F32))
        pb = _dot(yp, wbp_ref[...])
        psm = _dot(ys, wbs_ref[...])
        mb = (sgp * pb + sgs * psm).astype(BF16)
        out = _dot(mb, wout_ref[...])
        on, r = _rms_parts(out)
        gate = mod_ref[2:3, :]
        npost = npost_ref[...]
        normed = on * npost
        diff = x_ref[...] + gate * normed - t_ref[...]
        acc_loss[...] += jnp.sum(diff * diff, axis=0, keepdims=True)
        dy = diff * (1.0 / n_feat)
        dy_ref[...] = dy
        acc_vec[0:1, :] += jnp.sum(dy * normed, axis=0, keepdims=True)
        dn = dy * gate
        acc_vec[1:2, :] += jnp.sum(dn * on, axis=0, keepdims=True)
        dout = _rms_bwd(dn * npost, on, r).astype(BF16)
        acc_out[...] += _dot_tn(mb, dout)
        dm = _dot_nt(dout, wout_ref[...])
        dpb = (dm * sgp).astype(BF16)
        dps = (dm * sgs).astype(BF16)
        dg_ref[:, :D] = (dm * pb * sgp * (1.0 - sgp)).astype(BF16)
        dg_ref[:, D:] = (dm * psm * sgs * (1.0 - sgs)).astype(BF16)
        acc_bp[...] += _dot_tn(yp, dpb)
        acc_bs[...] += _dot_tn(ys, dps)
        dyp_ref[...] = _dot_nt(dpb, wbp_ref[...]).astype(BF16)
        dys_ref[...] = _dot_nt(dps, wbs_ref[...]).astype(BF16)

        @pl.when(i == nblk - 1)
        def _():
            loss_ref[...] = 0.5 / n_feat * jnp.sum(acc_loss[...], axis=1, keepdims=True)
            vec_ref[...] = acc_vec[...]
            pltpu.sync_copy(acc_bp, dwbp_hbm)
            pltpu.sync_copy(acc_bs, dwbs_hbm)
            pltpu.sync_copy(acc_out, dwout_hbm)

    row = lambda c: pl.BlockSpec((tb, D), lambda i: (i, c))
    w = _full((D, D), single=True)
    return _pcall(body, name="head", grid=(nblk,),
                  out_shape=(jax.ShapeDtypeStruct((1, 1), F32), jax.ShapeDtypeStruct((rows, D), F32),
                             jax.ShapeDtypeStruct((rows, D), BF16), jax.ShapeDtypeStruct((rows, D), BF16),
                             jax.ShapeDtypeStruct((rows, 2 * D), BF16),
                             jax.ShapeDtypeStruct((D, D), F32), jax.ShapeDtypeStruct((D, D), F32),
                             jax.ShapeDtypeStruct((D, D), F32), jax.ShapeDtypeStruct((2, D), F32)),
                  in_specs=[row(0), row(0), row(4), row(5), row(0), row(0), _full((3, D)), _full((1, D)), w, w, w],
                  out_specs=(_full((1, 1)), row(0), row(0), row(0), pl.BlockSpec((tb, 2 * D), lambda i: (i, 0)),
                             ANY, ANY, ANY, _full((2, D))),
                  scratch_shapes=[pltpu.VMEM((D, D), F32), pltpu.VMEM((D, D), F32), pltpu.VMEM((D, D), F32),
                                  pltpu.VMEM((1, D), F32), pltpu.VMEM((2, D), F32)],
                  compiler_params=_params(("arbitrary",)))(x, target, proj, proj, y_pool, y_ssm, mod3, norm_post,
                                                           wbp, wbs, wout)


def _glu_bwd(dys, proj, ys_pre, pm, pmt, glu_w, glu_b):
    rows = dys.shape[0]
    tb = pm.shape[0]
    nblk = rows // tb

    def body(dys_ref, z_ref, ysp_ref, pm_ref, pmt_ref, gw_ref, gb_ref, dyp_ref, dz_ref, dgw_hbm, dgb_ref,
             acc_w, acc_b):
        i = pl.program_id(0)

        @pl.when(i == 0)
        def _():
            acc_w[...] = jnp.zeros_like(acc_w)
            acc_b[...] = jnp.zeros_like(acc_b)

        d_out = _dot(pm_ref[...], dys_ref[...])
        z = _dot(pm_ref[...], z_ref[...])
        yg, dgelu = _gelu_parts(ysp_ref[...])
        ygb = yg.astype(BF16)
        sg = jax.nn.sigmoid(_dot(ygb, gw_ref[...]) + gb_ref[...])
        silu_z, dsilu_z = _silu_parts(z)
        dz = d_out * (yg * sg) * dsilu_z
        dz_ref[...] = _dot(pmt_ref[...], dz.astype(BF16)).astype(BF16)
        dglu = d_out * silu_z
        dq = dglu * yg * sg * (1.0 - sg)
        dqb = dq.astype(BF16)
        acc_b[...] += jnp.sum(dq, axis=0, keepdims=True)
        acc_w[...] += _dot_tn(ygb, dqb)
        dyg = dglu * sg + _dot_nt(dqb, gw_ref[...])
        dyp_ref[...] = (dyg * dgelu).astype(BF16)

        @pl.when(i == nblk - 1)
        def _():
            dgb_ref[...] = acc_b[...]
            pltpu.sync_copy(acc_w, dgw_hbm)

    row = lambda c: pl.BlockSpec((tb, D), lambda i: (i, c))
    return _pcall(body, name="glu_bwd", grid=(nblk,),
                  out_shape=(jax.ShapeDtypeStruct((rows, D), BF16), jax.ShapeDtypeStruct((rows, D), BF16),
                             jax.ShapeDtypeStruct((D, D), F32), jax.ShapeDtypeStruct((1, D), F32)),
                  in_specs=[row(0), row(3), row(0), _full((tb, tb)), _full((tb, tb)),
                            _full((D, D), single=True), _full((1, D))],
                  out_specs=(row(0), row(0), ANY, _full((1, D))),
                  scratch_shapes=[pltpu.VMEM((D, D), F32), pltpu.VMEM((1, D), F32)],
                  compiler_params=_params(("arbitrary",)))(dys, proj, ys_pre, pm, pmt, glu_w, glu_b)


def _ssm_bwd(dyp, proj, states, carries, pm, pmt, wb, wct, ptab, dvec, mat_grads, dpool_w, dw_in_rest):
    rows = dyp.shape[0]
    tb = pm.shape[0]
    k_steps = tb // SUBLANES
    nblk = rows // tb
    n_mat = len(mat_grads)
    hosted = [*mat_grads, dpool_w, dw_in_rest]
    n_h = len(hosted)
    shard_rows = D // N_DEV
    pool_rows = dpool_w.shape[1] // N_DEV
    items = [_scatter_item(t, t, _rows_of(shard_rows)) for t in range(n_mat)] + \
            [_scatter_item(n_mat, n_mat, _pool_rows_of(pool_rows))] + \
            [_w_in_block_item(n_mat + 1, n_mat + 1, j, ssm_part=False) for j in range(W_IN_SHARD // W_IN_BLOCK)]
    n_in, n_out = 10, 5

    def body(*refs):
        dyp_ref, u_ref, s_ref, cin_ref, pm_ref, pmt_ref, wb_ref, wct_ref, p_ref, d_ref = refs[:n_in]
        src_refs = refs[n_in:n_in + n_h]
        du_ref, dwb_hbm, dwct_hbm, da_ref, dd_ref = refs[n_in + n_h:n_in + n_h + n_out]
        recv_refs = refs[n_in + n_h + n_out:n_in + 2 * n_h + n_out]
        (g_ref, carry_b, fin_ref, acc_wb, acc_wct, acc_da, acc_dd, dup_ref,
         *sems) = refs[n_in + 2 * n_h + n_out:]
        i = pl.program_id(0)

        @pl.when(i == 0)
        def _():
            _hosted_copies(items, src_refs, recv_refs, *sems, act="start")
            carry_b[...] = jnp.zeros_like(carry_b)
            acc_wb[...] = jnp.zeros_like(acc_wb)
            acc_wct[...] = jnp.zeros_like(acc_wct)
            acc_da[...] = jnp.zeros_like(acc_da)
            acc_dd[...] = jnp.zeros_like(acc_dd)

        dy = dyp_ref[...]
        up = _dot(pm_ref[...], u_ref[...]).astype(BF16)
        acc_dd[...] += jnp.sum(dy.astype(F32) * up.astype(F32), axis=0, keepdims=True)
        for q in range(N_Q):
            cols = slice(q * 256, (q + 1) * 256)
            g_ref[:, q * Q_W:(q + 1) * Q_W] = _dot(dy[:, cols], wct_ref[q])
            acc_wct[q] += _dot_tn(dy[:, cols], s_ref[:, q * Q_W:(q + 1) * Q_W].astype(BF16))
        _scan_backward(g_ref, s_ref, p_ref, carry_b, cin_ref, fin_ref, acc_da, k_steps)
        for q in range(N_Q):
            cols = slice(q * 256, (q + 1) * 256)
            lam = g_ref[:, q * Q_W:(q + 1) * Q_W].astype(BF16)
            acc_wb[q] += _dot_tn(up[:, cols], lam)
            dup_ref[:, cols] = (_dot_nt(lam, wb_ref[q]) + d_ref[:, cols] * dy[:, cols].astype(F32)).astype(BF16)
        du_ref[...] = _dot(pmt_ref[...], dup_ref[...]).astype(BF16)

        @pl.when(i == nblk - 1)
        def _():
            da_ref[...] = acc_da[...]
            dd_ref[...] = acc_dd[...]
            pltpu.sync_copy(acc_wb, dwb_hbm)
            pltpu.sync_copy(acc_wct, dwct_hbm)
            _hosted_copies(items, src_refs, recv_refs, *sems, act="wait")

    rev = lambda c: pl.BlockSpec((tb, D), lambda i: (nblk - 1 - i, c))
    recv = [jax.ShapeDtypeStruct((N_DEV, shard_rows, D), F32)] * n_mat + \
           [jax.ShapeDtypeStruct((N_DEV, dpool_w.shape[0], pool_rows, dpool_w.shape[2]), F32),
            jax.ShapeDtypeStruct((N_DEV, D, W_IN_SHARD), BF16)]
    return _pcall(body, name="ssm_bwd", grid=(nblk,),
                  out_shape=(jax.ShapeDtypeStruct((rows, D), BF16),
                             jax.ShapeDtypeStruct((N_Q, 256, Q_W), F32), jax.ShapeDtypeStruct((N_Q, 256, Q_W), F32),
                             jax.ShapeDtypeStruct((1, N_STATE), F32), jax.ShapeDtypeStruct((1, D), F32), *recv),
                  in_specs=[rev(0), rev(2), pl.BlockSpec((tb, N_STATE), lambda i: (nblk - 1 - i, 0)),
                            pl.BlockSpec((None, 1, N_STATE), lambda i: (nblk - 1 - i, 0, 0)),
                            _full((tb, tb)), _full((tb, tb)),
                            _full((N_Q, 256, Q_W), single=True), _full((N_Q, 256, Q_W), single=True),
                            _full((k_steps, N_STATE)), _full((1, D))] + [ANY] * n_h,
                  out_specs=(rev(0), ANY, ANY, _full((1, N_STATE)), _full((1, D)), *([ANY] * n_h)),
                  scratch_shapes=[pltpu.VMEM((tb, N_STATE), F32), pltpu.VMEM((1, N_STATE), F32),
                                  pltpu.VMEM((SUBLANES, N_STATE), F32),
                                  pltpu.VMEM((N_Q, 256, Q_W), F32), pltpu.VMEM((N_Q, 256, Q_W), F32),
                                  pltpu.VMEM((1, N_STATE), F32), pltpu.VMEM((1, D), F32),
                                  pltpu.VMEM((tb, D), BF16)] + _sem_scratch(items),
                  compiler_params=_params(("arbitrary",), vmem=60 * 1024 * 1024),
                  )(dyp, proj, states, carries, pm, pmt, wb, wct, ptab, dvec, *hosted)


def _pool_bwd(dyp, proj, pool_w, pool_scale):
    rows = dyp.shape[0]
    tb = _tb(rows, 512)
    nblk = rows // tb
    hb = tb // HALO

    def body(dy_ref, u_ref, halo_ref, z_ref, pw_ref, ps_ref, dp_ref, dpw_ref, dps_ref, ahead_ref):
        i = pl.program_id(0)
        blk = nblk - 1 - i

        @pl.when(i == 0)
        def _():
            ahead_ref[...] = jnp.zeros_like(ahead_ref)
            dpw_ref[...] = jnp.zeros_like(dpw_ref)
            dps_ref[...] = jnp.zeros_like(dps_ref)

        u = u_ref[...].astype(F32)
        halo = jnp.where(blk > 0, halo_ref[...].astype(F32), 0.0)
        pooled, counts = _pool_windows(jnp.concatenate([halo, u], axis=0), tb, blk * tb)
        silu_z, dsilu_z = _silu_parts(z_ref[...].astype(F32))
        dy = dy_ref[...].astype(F32)
        for g, w in enumerate(POOL_WINDOWS):
            cols = slice(g * 256, (g + 1) * 256)
            pooled_b = pooled[g].astype(BF16)
            mixed = _dot(pooled_b, pw_ref[g])
            scale = ps_ref[:, cols]
            dp_ref[:, D + g * 256:D + (g + 1) * 256] = (dy[:, cols] * (mixed * scale) * dsilu_z[:, cols]).astype(BF16)
            dms = dy[:, cols] * silu_z[:, cols]
            dps_ref[:, cols] += jnp.sum(dms * mixed, axis=0, keepdims=True)
            dmixed = (dms * scale).astype(BF16)
            dpw_ref[g] += _dot_tn(pooled_b, dmixed)
            dpooled = _dot_nt(dmixed, pw_ref[g])
            ratio = dpooled / counts[g]
            acc = jnp.concatenate([ratio, ahead_ref[:, cols]], axis=0)
            ahead_ref[:, cols] = ratio[:HALO, :]
            s = 1
            while s < w:
                acc = acc + pltpu.roll(acc, tb + HALO - s, axis=0)
                s *= 2
            dp_ref[:, cols] = (acc[:tb, :] - dpooled).astype(BF16)

    rev = lambda c: pl.BlockSpec((tb, D), lambda i: (nblk - 1 - i, c))
    return _pcall(body, name="pool_bwd", grid=(nblk,),
                  out_shape=(jax.ShapeDtypeStruct((rows, 2 * D), BF16), jax.ShapeDtypeStruct((4, 256, 256), F32),
                             jax.ShapeDtypeStruct((1, D), F32)),
                  in_specs=[rev(0), rev(0),
                            pl.BlockSpec((HALO, D), lambda i: (jnp.maximum((nblk - 1 - i) * hb - 1, 0), 0)),
                            rev(1), _full((4, 256, 256)), _full((1, D))],
                  out_specs=(pl.BlockSpec((tb, 2 * D), lambda i: (nblk - 1 - i, 0)), _full((4, 256, 256)),
                             _full((1, D))),
                  scratch_shapes=[pltpu.VMEM((HALO, D), F32)],
                  compiler_params=_params(("arbitrary",)))(dyp, proj, proj, proj, pool_w, pool_scale)


def _dproj_specs(tb):
    return [pl.BlockSpec((tb, 2 * D), lambda i: (i, 0)), pl.BlockSpec((tb, D), lambda i: (i, 0)),
            pl.BlockSpec((tb, D), lambda i: (i, 0)), pl.BlockSpec((tb, 2 * D), lambda i: (i, 0))]


def _in_proj_bwd_x(x, dy, dpp, dus, dzs, dpg, mod3, norm_pre, w_in, dw_in_ssm, small32, small16, recv_w_in):
    rows = x.shape[0]
    tb = _tb(rows, 256)
    nblk = rows // tb
    items = [_w_in_block_item(0, 0, j, ssm_part=True) for j in range(W_IN_SHARD // W_IN_BLOCK)] + \
            [_Item(1, 1, _whole, _slot), _Item(2, 2, _whole, _slot)]

    def body(x_ref, dy_ref, dpp_ref, dus_ref, dzs_ref, dpg_ref, mod_ref, np_ref, w_ref,
             dw_src, s32_src, s16_src, _, gx_ref, vec_ref, recv_w, recv32, recv16, *sems):
        src_refs, recv_refs = (dw_src, s32_src, s16_src), (recv_w, recv32, recv16)

        @pl.when(pl.program_id(0) == 0)
        def _():
            _hosted_copies(items, src_refs, recv_refs, *sems, act="start")
            vec_ref[...] = jnp.zeros_like(vec_ref)

        dh = _dot_nt(dpp_ref[...], w_ref[:, 0:2 * D])
        dh += _dot_nt(dus_ref[...], w_ref[:, 2 * D:3 * D])
        dh += _dot_nt(dzs_ref[...], w_ref[:, 3 * D:4 * D])
        dh += _dot_nt(dpg_ref[...], w_ref[:, 4 * D:6 * D])
        xn, r, _ = _prenorm(x_ref[...], mod_ref[...], np_ref[...])
        one_scale = 1.0 + mod_ref[1:2, :]
        vec_ref[0:1, :] += jnp.sum(dh, axis=0, keepdims=True)
        vec_ref[1:2, :] += jnp.sum(dh * xn, axis=0, keepdims=True) * np_ref[...]
        vec_ref[2:3, :] += jnp.sum(dh * xn, axis=0, keepdims=True) * one_scale
        gx_ref[...] = dy_ref[...] + _rms_bwd(dh * (np_ref[...] * one_scale), xn, r)

        @pl.when(pl.program_id(0) == nblk - 1)
        def _():
            _hosted_copies(items, src_refs, recv_refs, *sems, act="wait")

    row = pl.BlockSpec((tb, D), lambda i: (i, 0))
    recv = (jax.ShapeDtypeStruct(recv_w_in.shape, recv_w_in.dtype),
            jax.ShapeDtypeStruct((N_DEV,) + small32.shape, small32.dtype),
            jax.ShapeDtypeStruct((N_DEV,) + small16.shape, small16.dtype))
    return _pcall(body, name="in_proj_bwd_x", grid=(nblk,),
                  out_shape=(jax.ShapeDtypeStruct((rows, D), F32), jax.ShapeDtypeStruct((3, D), F32), *recv),
                  in_specs=[row, row] + _dproj_specs(tb) + [_full((3, D)), _full((1, D)),
                                                            _full((D, N_IN), single=True)] + [ANY] * 4,
                  out_specs=(row, _full((3, D)), ANY, ANY, ANY),
                  input_output_aliases={12: 2},
                  scratch_shapes=_sem_scratch(items),
                  compiler_params=_params(("arbitrary",)))(x, dy, dpp, dus, dzs, dpg, mod3, norm_pre, w_in,
                                                           dw_in_ssm, small32, small16, recv_w_in)


def _in_proj_bwd_w(name, x, dparts, mod3, norm_pre):
    rows = x.shape[0]
    tb = _tb(rows, 256)
    nblk = rows // tb
    widths = [p.shape[1] for p in dparts]
    n_p = len(dparts)

    def body(x_ref, *rest):
        part_refs, (mod_ref, np_ref, dw_ref, acc) = rest[:n_p], rest[n_p:]
        i = pl.program_id(0)

        @pl.when(i == 0)
        def _():
            acc[...] = jnp.zeros_like(acc)

        _, _, h = _prenorm(x_ref[...], mod_ref[...], np_ref[...])
        ht = h.astype(BF16)
        lo = 0
        for ref, w in zip(part_refs, widths):
            acc[:, lo:lo + w] += _dot_tn(ht, ref[...])
            lo += w

        @pl.when(i == nblk - 1)
        def _():
            dw_ref[...] = acc[...].astype(BF16)

    row = pl.BlockSpec((tb, D), lambda i: (i, 0))
    return _pcall(body, name=name, grid=(nblk,),
                  out_shape=jax.ShapeDtypeStruct((D, sum(widths)), BF16),
                  in_specs=[row] + [pl.BlockSpec((tb, w), lambda i: (i, 0)) for w in widths] +
                           [_full((3, D)), _full((1, D))],
                  out_specs=_full((D, sum(widths))),
                  scratch_shapes=[pltpu.VMEM((D, sum(widths)), F32)],
                  compiler_params=_params(("arbitrary",)))(x, *dparts, mod3, norm_pre)


def _adamw_math(w, g, m, v):
    m = ADAM_B1 * m + (1.0 - ADAM_B1) * g
    v = ADAM_B2 * v + (1.0 - ADAM_B2) * (g * g)
    m_hat = m / (1.0 - ADAM_B1 ** ADAM_STEP)
    v_hat = v / (1.0 - ADAM_B2 ** ADAM_STEP)
    delta = -ADAM_LR * (m_hat / (jnp.sqrt(v_hat) + ADAM_EPS) + ADAM_WD * w)
    return delta, m, v


def _sum_sources(ref):
    g = ref[0].astype(F32)
    for s in range(1, N_DEV):
        g = g + ref[s].astype(F32)
    return g


def _adamw_reduce(name, parts, w, m, v):
    r, c = w.shape
    tr = r if r * c <= 256 * 1024 else max(8, (256 * 1024 // c) // 8 * 8)
    while r % tr:
        tr -= 8

    def body(p_ref, w_ref, m_ref, v_ref, g_ref, d_ref, nm_ref, nv_ref):
        g = _sum_sources(p_ref)
        g_ref[...] = g
        d_ref[...], nm_ref[...], nv_ref[...] = _adamw_math(w_ref[...], g, m_ref[...], v_ref[...])

    blk = pl.BlockSpec((tr, c), lambda i: (i, 0))
    return _pcall(body, name=name, grid=(r // tr,),
                  out_shape=tuple([jax.ShapeDtypeStruct((r, c), F32)] * 4),
                  in_specs=[pl.BlockSpec((N_DEV, tr, c), lambda i: (0, i, 0)), blk, blk, blk],
                  out_specs=(blk, blk, blk, blk),
                  compiler_params=_params(("arbitrary",)))(parts, w, m, v)


def _adamw_plain(name, g, w, m, v):
    def body(g_ref, w_ref, m_ref, v_ref, d_ref, nm_ref, nv_ref):
        d_ref[...], nm_ref[...], nv_ref[...] = _adamw_math(w_ref[...], g_ref[...], m_ref[...], v_ref[...])

    vm = pl.BlockSpec(memory_space=pltpu.VMEM)
    return _pcall(body, name=name, out_shape=tuple([jax.ShapeDtypeStruct(w.shape, F32)] * 3),
                  in_specs=[vm] * 4, out_specs=(vm, vm, vm), compiler_params=_params())(g, w, m, v)


def _sum_small(parts):
    n = len(parts)

    def body(*refs):
        for t in range(n):
            refs[n + t][...] = _sum_sources(refs[t])

    vm = pl.BlockSpec(memory_space=pltpu.VMEM)
    return _pcall(body, name="sum_small",
                  out_shape=tuple(jax.ShapeDtypeStruct(p.shape[1:], F32) for p in parts),
                  in_specs=[vm] * n, out_specs=tuple([vm] * n), compiler_params=_params())(*parts)


def _ada_update(c_all, dmod_cols, w, m, v):
    def body(c_ref, dm_ref, w_ref, m_ref, v_ref, g_ref, d_ref, nm_ref, nv_ref):
        ca = c_ref[...]
        g = lax.dot_general(ca * jax.nn.sigmoid(ca), dm_ref[...], (((0,), (0,)), ((), ())),
                            preferred_element_type=F32, precision=lax.Precision.HIGHEST)
        g_ref[...] = g
        d_ref[...], nm_ref[...], nv_ref[...] = _adamw_math(w_ref[...], g, m_ref[...], v_ref[...])

    vm = pl.BlockSpec(memory_space=pltpu.VMEM)
    return _pcall(body, name="ada_update", out_shape=tuple([jax.ShapeDtypeStruct(w.shape, F32)] * 4),
                  in_specs=[vm] * 5, out_specs=(vm, vm, vm, vm), compiler_params=_params())(c_all, dmod_cols, w, m, v)


def kernel(x, c, w_ada, b_ada, norm_pre, norm_post, w_in, pool_w, pool_scale, ssm_a_re, ssm_a_im, ssm_log_dt, ssm_b_re, ssm_b_im, ssm_c_re, ssm_c_im, ssm_d, glu_w, glu_b, w_branch_pool, w_branch_ssm, w_out, loss_target, m_w_ada, m_b_ada, m_norm_pre, m_norm_post, m_w_in, m_pool_w, m_pool_scale, m_ssm_a_re, m_ssm_a_im, m_ssm_log_dt, m_ssm_b_re, m_ssm_b_im, m_ssm_c_re, m_ssm_c_im, m_ssm_d, m_glu_w, m_glu_b, m_w_branch_pool, m_w_branch_ssm, m_w_out, v_w_ada, v_b_ada, v_norm_pre, v_norm_post, v_w_in, v_pool_w, v_pool_scale, v_ssm_a_re, v_ssm_a_im, v_ssm_log_dt, v_ssm_b_re, v_ssm_b_im, v_ssm_c_re, v_ssm_c_im, v_ssm_d, v_glu_w, v_glu_b, v_w_branch_pool, v_w_branch_ssm, v_w_out):
    given = dict(locals())
    me = _flat(_me())
    rows = x.shape[1]
    x2 = x[0]
    target = loss_target[0]
    ada_cols = w_ada.shape[2]

    b_ada_s = lax.dynamic_slice(b_ada, (0, me * ada_cols), (1, ada_cols))
    c_all, mod_rows = _ada_exchange(c, w_ada[0], b_ada_s)
    mod3 = mod_rows.reshape(3, D)

    shards = _cast_shards([w_in[0], pool_w[0], glu_w[0], w_branch_pool[0], w_branch_ssm[0], w_out[0]])
    (w_in_g,) = _exchange("gather_w_in", [shards[0]], [jax.ShapeDtypeStruct((D, N_IN), BF16)],
                          [_gather_item(0, 0, _cols_of(W_IN_SHARD))])

    tb_ssm = _tb(rows, 256)
    k_steps = tb_ssm // SUBLANES
    a_re, a_im = ssm_a_re[0], ssm_a_im[0]
    log_dt = ssm_log_dt[0].reshape(GROUPS, 1)
    b_re_t, b_im_t = ssm_b_re[0].transpose(0, 2, 1), ssm_b_im[0].transpose(0, 2, 1)
    bb_re, bb_im, pow_re, pow_im = _s5_prep(a_re, a_im, log_dt, b_re_t, b_im_t, k_steps)
    wb = jnp.concatenate([_blockdiag(bb_re), _blockdiag(bb_im)], axis=-1).astype(BF16)
    wct = jnp.concatenate([_blockdiag(ssm_c_re[0]), _blockdiag(-ssm_c_im[0])], axis=-1).astype(BF16)
    ptab = _state_layout(pow_re, pow_im)
    dvec = ssm_d[0].reshape(1, D)
    pm = _perm_matrix(tb_ssm)
    pmt = pm.T

    proj, pool_w_g, glu_g, wbp_g, wbs_g, wout_g = _in_proj(x2, mod3, norm_pre, w_in_g, shards[1:])
    y_pool = _pool_fwd(proj, pool_w_g, pool_scale)
    y_ssm, ys_pre, carries, states = _ssm_fwd(proj, pm, pmt, wb, wct, ptab, dvec, glu_g, glu_b)
    loss_part, dy, dyp, dys, dpg, dwbp, dwbs, dwout, head_vec = _head(
        x2, target, proj, y_pool, y_ssm, mod3, norm_post, wbp_g, wbs_g, wout_g)

    dpp, dpool_w, dpool_scale = _pool_bwd(dyp, proj, pool_w_g, pool_scale)
    dw_in_rest = _in_proj_bwd_w("in_proj_bwd_w_rest", x2, [dpp, dpg], mod3, norm_pre)
    dy_pre, dzs, dglu_w, dglu_b = _glu_bwd(dys, proj, ys_pre, pm, pmt, glu_g, glu_b)
    dus, dwb, dwct, dabar, dd, p_glu, p_wbp, p_wbs, p_wout, p_pool_w, p_w_in = _ssm_bwd(
        dy_pre, proj, states, carries, pm, pmt, wb, wct, ptab, dvec, [dglu_w, dwbp, dwbs, dwout], dpool_w, dw_in_rest)
    dw_in_ssm = _in_proj_bwd_w("in_proj_bwd_w_ssm", x2, [dus, dzs], mod3, norm_pre)

    small32 = jnp.concatenate([head_vec, dpool_scale, dglu_b, dd, jnp.zeros((3, D), F32), dabar.reshape(8, D)], axis=0)
    small16 = jnp.concatenate([
        _diag_blocks(dwb[:, :, :Q_W // 2]).reshape(64, D), _diag_blocks(dwb[:, :, Q_W // 2:]).reshape(64, D),
        _diag_blocks(dwct[:, :, :Q_W // 2]).reshape(64, D), _diag_blocks(dwct[:, :, Q_W // 2:]).reshape(64, D),
    ], axis=0).astype(BF16)
    grad_x, pre_vec, p_w_in, p_small32, p_small16 = _in_proj_bwd_x(
        x2, dy, dpp, dus, dzs, dpg, mod3, norm_pre, w_in_g, dw_in_ssm, small32, small16, p_w_in)
    small_pre = jnp.concatenate([pre_vec, jnp.zeros((5, D), F32)], axis=0)
    (p_pre,) = _exchange("gather_prenorm_sums", [small_pre], [jax.ShapeDtypeStruct((N_DEV, 8, D), F32)],
                         [_Item(0, 0, _whole, _slot)])

    tot32, tot16, tot_pre = _sum_small([p_small32, p_small16, p_pre])
    d_abar_re, d_abar_im = _state_unlayout(tot32[8:16].reshape(N_STATE))
    d_bb_re, d_bb_im = tot16[0:64].reshape(GROUPS, G_H, G_P), tot16[64:128].reshape(GROUPS, G_H, G_P)
    g_a_re, g_a_im, g_log_dt, g_b_re_t, g_b_im_t = _s5_prep_bwd(
        a_re, a_im, log_dt, b_re_t, b_im_t, d_abar_re, d_abar_im, d_bb_re, d_bb_im)

    grads, deltas, new_m, new_v = {}, {}, {}, {}

    def small_update(name, g2):
        shape = given[name].shape
        w2, m2, v2 = (given[p + name].reshape(g2.shape) for p in ("", "m_", "v_"))
        d2, nm2, nv2 = _adamw_plain("adamw_" + name, g2, w2, m2, v2)
        grads[name], deltas[name], new_m[name], new_v[name] = (a.reshape(shape) for a in (g2, d2, nm2, nv2))

    def shard_update(name, parts):
        shape = given[name].shape
        r2 = parts.shape[1:] if parts.ndim == 3 else (parts.shape[1] * parts.shape[2], parts.shape[3])
        w2, m2, v2 = (given[p + name].reshape(r2) for p in ("", "m_", "v_"))
        out = _adamw_reduce("adamw_" + name, parts.reshape((N_DEV,) + tuple(r2)), w2, m2, v2)
        grads[name], deltas[name], new_m[name], new_v[name] = (a.reshape(shape) for a in out)

    dmod_all = jnp.concatenate([p_pre[:, 0:2, :], p_small32[:, 0:1, :]], axis=1).reshape(N_DEV, 3 * D)
    dmod_cols = lax.dynamic_slice(dmod_all, (0, me * ada_cols), (N_DEV, ada_cols))
    out = _ada_update(c_all, dmod_cols, w_ada[0], m_w_ada[0], v_w_ada[0])
    grads['w_ada'], deltas['w_ada'], new_m['w_ada'], new_v['w_ada'] = (a.reshape(w_ada.shape) for a in out)

    small_update('b_ada', jnp.concatenate([tot_pre[0:2], tot32[0:1]], axis=0).reshape(1, 3 * D))
    small_update('norm_pre', tot_pre[2:3])
    small_update('norm_post', tot32[1:2])
    small_update('pool_scale', tot32[2:3])
    small_update('glu_b', tot32[3:4])
    small_update('ssm_d', tot32[4:5])
    small_update('ssm_a_re', g_a_re)
    small_update('ssm_a_im', g_a_im)
    small_update('ssm_log_dt', g_log_dt.reshape(1, GROUPS))
    small_update('ssm_b_re', g_b_re_t.transpose(0, 2, 1).reshape(GROUPS, G_P * G_H))
    small_update('ssm_b_im', g_b_im_t.transpose(0, 2, 1).reshape(GROUPS, G_P * G_H))
    small_update('ssm_c_re', tot16[128:192])
    small_update('ssm_c_im', -tot16[192:256])
    shard_update('w_in', p_w_in)
    shard_update('pool_w', p_pool_w)
    shard_update('glu_w', p_glu)
    shard_update('w_branch_pool', p_wbp)
    shard_update('w_branch_ssm', p_wbs)
    shard_update('w_out', p_wout)

    loss = lax.psum(loss_part[0, 0], ("x", "y", "c"))
    return (loss, grad_x[None], *[grads[n] for n in WEIGHTS], *[deltas[n] for n in WEIGHTS],
            *[new_m[n] for n in WEIGHTS], *[new_v[n] for n in WEIGHTS])
```

```python
import functools
import math
from typing import Callable, NamedTuple, Optional

import jax
import jax.numpy as jnp
from jax import lax
from jax.experimental import pallas as pl
from jax.experimental.pallas import tpu as pltpu

F32 = jnp.float32
BF16 = jnp.bfloat16
MESH = pl.DeviceIdType.MESH

D = 1024
N_DEV = 8
N_IN = 6 * D
GROUPS = 64
G_H = 16
G_P = 64
N_Q = 4
Q_W = 2 * 16 * G_P
N_STATE = N_Q * Q_W
POOL_WINDOWS = (2, 4, 8, 16)
HALO = 16
RMS_EPS = 1e-6
SUBLANES = 8
LANE_CHUNK = 512
SCAN_UNROLL = 2
VMEM_LIMIT = 56 * 1024 * 1024

ADAM_LR = 0.001
ADAM_B1 = 0.9
ADAM_B2 = 0.999
ADAM_EPS = 1e-08
ADAM_WD = 0.01
ADAM_STEP = 10

WEIGHTS = ['w_ada', 'b_ada', 'norm_pre', 'norm_post', 'w_in', 'pool_w', 'pool_scale', 'ssm_a_re',
           'ssm_a_im', 'ssm_log_dt', 'ssm_b_re', 'ssm_b_im', 'ssm_c_re', 'ssm_c_im', 'ssm_d', 'glu_w',
           'glu_b', 'w_branch_pool', 'w_branch_ssm', 'w_out']


def _pcall(body, **kw):
    return pl.pallas_call(body, **kw)


def _params(sem=None, vmem=VMEM_LIMIT):
    return pltpu.CompilerParams(dimension_semantics=sem, vmem_limit_bytes=vmem)


def _tb(rows, pref):
    return pref if rows % pref == 0 and rows // pref >= 2 else rows // 2


def _full(shape, single=False):
    nd = len(shape)
    if single:
        return pl.BlockSpec(shape, lambda i: (0,) * nd, pipeline_mode=pl.Buffered(1))
    return pl.BlockSpec(shape, lambda i: (0,) * nd)


ANY = pl.BlockSpec(memory_space=pl.ANY)


def _me():
    return lax.axis_index("x"), lax.axis_index("y"), lax.axis_index("c")


def _flat(p):
    return 4 * p[0] + 2 * p[1] + p[2]


def _peer(k):
    x, y, c = _me()
    return (1 - x if k & 4 else x, 1 - y if k & 2 else y, 1 - c if k & 1 else c)


def _silu_parts(z):
    s = jax.nn.sigmoid(z)
    return z * s, s * (1.0 + z * (1.0 - s))


_GELU_C = math.sqrt(2.0 / math.pi)


def _gelu_parts(x):
    x2 = x * x
    t = jnp.tanh(_GELU_C * (x + 0.044715 * x * x2))
    g = 0.5 * x * (1.0 + t)
    dg = 0.5 * (1.0 + t) + 0.5 * x * (1.0 - t * t) * (_GELU_C * (1.0 + 3.0 * 0.044715 * x2))
    return g, dg


def _dot(a, b):
    return jnp.dot(a, b, preferred_element_type=F32)


def _dot_nt(a, b):
    return lax.dot_general(a, b, (((1,), (1,)), ((), ())), preferred_element_type=F32)


def _dot_tn(a, b):
    return lax.dot_general(a, b, (((0,), (0,)), ((), ())), preferred_element_type=F32)


def _rms_parts(x):
    r = lax.rsqrt(jnp.mean(x * x, axis=-1, keepdims=True) + RMS_EPS)
    return x * r, r


def _rms_bwd(dxn, xn, r):
    return r * (dxn - xn * jnp.mean(dxn * xn, axis=-1, keepdims=True))


def _ada_exchange(c, w_ada_s, b_ada_s):
    cols = w_ada_s.shape[1]

    def body(c_ref, w_ref, b_ref, call_ref, mod_ref, part_ref, ssem, rsem, lsem):
        me3 = _me()
        me = _flat(me3)
        mine = pltpu.make_async_copy(c_ref, call_ref.at[pl.ds(me, 1), :], lsem.at[0])
        mine.start()
        sends = []
        for k in range(1, N_DEV):
            cp = pltpu.make_async_remote_copy(src_ref=c_ref, dst_ref=call_ref.at[pl.ds(me, 1), :],
                                              send_sem=ssem.at[k - 1], recv_sem=rsem.at[k - 1],
                                              device_id=_peer(k), device_id_type=MESH)
            cp.start()
            sends.append(cp)
        mine.wait()
        for k in range(1, N_DEV):
            p = _flat(_peer(k))
            pltpu.make_async_remote_copy(src_ref=c_ref, dst_ref=call_ref.at[pl.ds(p, 1), :],
                                         send_sem=ssem.at[k - 1], recv_sem=rsem.at[k - 1],
                                         device_id=_peer(k), device_id_type=MESH).wait_recv()
        for cp in sends:
            cp.wait_send()
        ca = call_ref[...]
        act = ca * jax.nn.sigmoid(ca)
        part_ref[...] = jnp.dot(act, w_ref[...], preferred_element_type=F32,
                                precision=lax.Precision.HIGHEST) + b_ref[...]
        own = pltpu.make_async_copy(part_ref.at[pl.ds(me, 1), :], mod_ref.at[pl.ds(me, 1), :], lsem.at[1])
        own.start()
        sends = []
        for k in range(1, N_DEV):
            p = _flat(_peer(k))
            s = N_DEV - 1 + k - 1
            cp = pltpu.make_async_remote_copy(src_ref=part_ref.at[pl.ds(p, 1), :],
                                              dst_ref=mod_ref.at[pl.ds(me, 1), :],
                                              send_sem=ssem.at[s], recv_sem=rsem.at[s],
                                              device_id=_peer(k), device_id_type=MESH)
            cp.start()
            sends.append(cp)
        own.wait()
        for k in range(1, N_DEV):
            p = _flat(_peer(k))
            s = N_DEV - 1 + k - 1
            pltpu.make_async_remote_copy(src_ref=part_ref.at[pl.ds(p, 1), :],
                                         dst_ref=mod_ref.at[pl.ds(p, 1), :],
                                         send_sem=ssem.at[s], recv_sem=rsem.at[s],
                                         device_id=_peer(k), device_id_type=MESH).wait_recv()
        for cp in sends:
            cp.wait_send()

    vm = pl.BlockSpec(memory_space=pltpu.VMEM)
    return _pcall(
        body, name="ada_exchange",
        out_shape=(jax.ShapeDtypeStruct((N_DEV, D), F32), jax.ShapeDtypeStruct((N_DEV, cols), F32)),
        in_specs=[vm, vm, vm], out_specs=(vm, vm),
        scratch_shapes=[pltpu.VMEM((N_DEV, cols), F32),
                        pltpu.SemaphoreType.DMA((2 * (N_DEV - 1),)),
                        pltpu.SemaphoreType.DMA((2 * (N_DEV - 1),)),
                        pltpu.SemaphoreType.DMA((2,))],
    )(c, w_ada_s, b_ada_s)


class _Item(NamedTuple):
    src: int
    out: int
    src_view: Callable
    dst_view: Callable
    pred: Optional[Callable] = None


def _when(pred, dest, fn):
    if pred is None:
        fn()
    else:
        pl.when(pred(dest))(fn)


def _n_sems(items):
    return len(items) * (N_DEV - 1)


def _hosted_copies(items, srcs, outs, ssem, rsem, lsem, act):
    me = _flat(_me())
    for t, it in enumerate(items):
        local = lambda t=t, it=it: pltpu.make_async_copy(
            it.src_view(srcs[it.src], me), it.dst_view(outs[it.out], me), lsem.at[t])
        if act == "start":
            _when(it.pred, me, lambda local=local: local().start())
        else:
            _when(it.pred, me, lambda local=local: local().wait())
    for k in range(1, N_DEV):
        p3 = _peer(k)
        p = _flat(p3)
        for t, it in enumerate(items):
            s = t * (N_DEV - 1) + k - 1
            send = lambda it=it, s=s, p=p, p3=p3: pltpu.make_async_remote_copy(
                src_ref=it.src_view(srcs[it.src], p), dst_ref=it.dst_view(outs[it.out], me),
                send_sem=ssem.at[s], recv_sem=rsem.at[s], device_id=p3, device_id_type=MESH)
            recv = lambda it=it, s=s, p=p, p3=p3: pltpu.make_async_remote_copy(
                src_ref=it.src_view(srcs[it.src], p), dst_ref=it.dst_view(outs[it.out], p),
                send_sem=ssem.at[s], recv_sem=rsem.at[s], device_id=p3, device_id_type=MESH)
            if act == "start":
                _when(it.pred, p, lambda send=send: send().start())
            else:
                _when(it.pred, me, lambda recv=recv: recv().wait_recv())
                _when(it.pred, p, lambda send=send: send().wait_send())


def _sem_scratch(items):
    return [pltpu.SemaphoreType.DMA((_n_sems(items),)), pltpu.SemaphoreType.DMA((_n_sems(items),)),
            pltpu.SemaphoreType.DMA((len(items),))]


def _exchange(name, srcs, out_structs, items):
    n_src, n_out = len(srcs), len(out_structs)

    def body(*refs):
        src_refs, out_refs = refs[:n_src], refs[n_src:n_src + n_out]
        sems = refs[n_src + n_out:]
        _hosted_copies(items, src_refs, out_refs, *sems, act="start")
        _hosted_copies(items, src_refs, out_refs, *sems, act="wait")

    return _pcall(body, name=name, out_shape=tuple(out_structs),
                  in_specs=[ANY] * n_src, out_specs=tuple([ANY] * n_out),
                  scratch_shapes=_sem_scratch(items))(*srcs)


def _whole(ref, dest):
    return ref


def _slot(ref, sender):
    return ref.at[sender]


def _rows_of(rows):
    return lambda ref, dev: ref.at[pl.ds(dev * rows, rows), :]


def _cols_of(cols):
    return lambda ref, dev: ref.at[:, pl.ds(dev * cols, cols)]


def _pool_rows_of(rows):
    return lambda ref, dev: ref.at[:, pl.ds(dev * rows, rows), :]


def _gather_item(src, out, dst_view):
    return _Item(src, out, _whole, dst_view)


def _scatter_item(src, out, src_view):
    return _Item(src, out, src_view, _slot)


W_IN_BLOCK = 256
W_IN_SHARD = N_IN // N_DEV
SSM_BLOCKS = (2 * D // W_IN_BLOCK, 4 * D // W_IN_BLOCK)


def _w_in_block_item(src, out, j, ssm_part):
    def block(dest):
        return (W_IN_SHARD // W_IN_BLOCK) * dest + j

    def in_ssm(dest):
        b = block(dest)
        return (b >= SSM_BLOCKS[0]) & (b < SSM_BLOCKS[1])

    def src_view(ref, dest):
        b = block(dest)
        local = b - SSM_BLOCKS[0] if ssm_part else jnp.where(b < SSM_BLOCKS[0], b, b - (SSM_BLOCKS[1] - SSM_BLOCKS[0]))
        local = jnp.clip(local, 0, ref.shape[1] // W_IN_BLOCK - 1)
        return ref.at[:, pl.ds(local * W_IN_BLOCK, W_IN_BLOCK)]

    def dst_view(ref, sender):
        return ref.at[sender, :, pl.ds(j * W_IN_BLOCK, W_IN_BLOCK)]

    pred = in_ssm if ssm_part else (lambda dest: jnp.logical_not(in_ssm(dest)))
    return _Item(src, out, src_view, dst_view, pred)


def _cast_shards(arrs):
    def body(*refs):
        n = len(refs) // 2
        for i in range(n):
            refs[n + i][...] = refs[i][...].astype(BF16)

    vm = pl.BlockSpec(memory_space=pltpu.VMEM)
    return _pcall(body, name="cast_shards",
                  out_shape=tuple(jax.ShapeDtypeStruct(a.shape, BF16) for a in arrs),
                  in_specs=[vm] * len(arrs), out_specs=tuple([vm] * len(arrs)),
                  compiler_params=_params())(*arrs)


def _s5_discretise(a_re, a_im, log_dt, b_re_t, b_im_t):
    dt = jnp.exp(log_dt)
    lam_re = jnp.minimum(a_re, -1e-4)
    lam_im = a_im
    mag = jnp.exp(lam_re * dt)
    abar_re = mag * jnp.cos(lam_im * dt)
    abar_im = mag * jnp.sin(lam_im * dt)
    den = lam_re * lam_re + lam_im * lam_im
    num_re = abar_re - 1.0
    f_re = (num_re * lam_re + abar_im * lam_im) / den
    f_im = (abar_im * lam_re - num_re * lam_im) / den
    f_re, f_im = f_re[:, None, :], f_im[:, None, :]
    bb_re = f_re * b_re_t - f_im * b_im_t
    bb_im = f_re * b_im_t + f_im * b_re_t
    return abar_re, abar_im, bb_re, bb_im


def _group_masks():
    spread = lax.broadcasted_iota(jnp.int32, (G_P, 16 * G_P), 1) % G_P == lax.broadcasted_iota(
        jnp.int32, (G_P, 16 * G_P), 0)
    own = lax.broadcasted_iota(jnp.int32, (16 * G_H, 16 * G_P), 0) // G_H == lax.broadcasted_iota(
        jnp.int32, (16 * G_H, 16 * G_P), 1) // G_P
    return spread, own


def _s5_prep(a_re, a_im, log_dt, b_re_t, b_im_t, c_re, c_im, n_pow):
    def body(ar_ref, ai_ref, ld_ref, br_ref, bi_ref, cr_ref, ci_ref, wb_ref, wct_ref, pr_ref, pi_ref):
        abar_re, abar_im, bb_re, bb_im = _s5_discretise(ar_ref[...], ai_ref[...], ld_ref[...], br_ref[...], bi_ref[...])
        spread, own = _group_masks()
        spread = spread.astype(BF16)
        for ref, parts in ((wb_ref, (bb_re, bb_im)), (wct_ref, (cr_ref[...], -ci_ref[...]))):
            for half, t in enumerate(parts):
                for q in range(N_Q):
                    blocks = t[q * 16:(q + 1) * 16].reshape(16 * G_H, G_P).astype(BF16)
                    dense = jnp.where(own, _dot(blocks, spread), 0.0)
                    ref[q, :, half * (Q_W // 2):(half + 1) * (Q_W // 2)] = dense.astype(BF16)
        p_re, p_im = abar_re, abar_im
        pr_ref[0] = p_re
        pi_ref[0] = p_im
        for k in range(1, n_pow):
            p_re, p_im = p_re * abar_re - p_im * abar_im, p_re * abar_im + p_im * abar_re
            pr_ref[k] = p_re
            pi_ref[k] = p_im

    vm = pl.BlockSpec(memory_space=pltpu.VMEM)
    return _pcall(body, name="s5_prep",
                  out_shape=(jax.ShapeDtypeStruct((N_Q, 16 * G_H, Q_W), BF16),
                             jax.ShapeDtypeStruct((N_Q, 16 * G_H, Q_W), BF16),
                             jax.ShapeDtypeStruct((n_pow, GROUPS, G_P), F32),
                             jax.ShapeDtypeStruct((n_pow, GROUPS, G_P), F32)),
                  in_specs=[vm] * 7, out_specs=(vm, vm, vm, vm), compiler_params=_params(),
                  )(a_re, a_im, log_dt, b_re_t, b_im_t, c_re, c_im)


def _s5_prep_bwd(a_re, a_im, log_dt, b_re_t, b_im_t, d_abar_re, d_abar_im, d_bb_re, d_bb_im):
    def body(ar_ref, ai_ref, ld_ref, br_ref, bi_ref, dar_ref, dai_ref, dbr_ref, dbi_ref,
             gar_ref, gai_ref, gld_ref, gbr_ref, gbi_ref):
        _, vjp = jax.vjp(_s5_discretise, ar_ref[...], ai_ref[...], ld_ref[...], br_ref[...], bi_ref[...])
        g = vjp((dar_ref[...], dai_ref[...], dbr_ref[...], dbi_ref[...]))
        gar_ref[...] = g[0]
        gai_ref[...] = g[1]
        gld_ref[...] = g[2]
        gbr_ref[...] = g[3]
        gbi_ref[...] = g[4]

    vm = pl.BlockSpec(memory_space=pltpu.VMEM)
    ins = (a_re, a_im, log_dt, b_re_t, b_im_t)
    return _pcall(body, name="s5_prep_bwd",
                  out_shape=tuple(jax.ShapeDtypeStruct(a.shape, F32) for a in ins),
                  in_specs=[vm] * 9, out_specs=tuple([vm] * 5), compiler_params=_params(),
                  )(*ins, d_abar_re, d_abar_im, d_bb_re, d_bb_im)


def _state_layout(re, im):
    lead = re.shape[:-2]
    r = re.reshape(lead + (N_Q, 1, 16 * G_P))
    i = im.reshape(lead + (N_Q, 1, 16 * G_P))
    return jnp.concatenate([r, i], axis=-2).reshape(lead + (N_STATE,))


def _state_unlayout(v):
    v4 = v.reshape(N_Q, 2, 16, G_P)
    return v4[:, 0].reshape(GROUPS, G_P), v4[:, 1].reshape(GROUPS, G_P)


def _perm_matrix(tb):
    k_steps = tb // SUBLANES
    r = jnp.arange(tb)
    src = (r % SUBLANES) * k_steps + r // SUBLANES
    return (src[:, None] == jnp.arange(tb)[None, :]).astype(BF16)


def _lane_chunks():
    for q in range(N_Q):
        for lc in range(Q_W // 2 // LANE_CHUNK):
            re = q * Q_W + lc * LANE_CHUNK
            yield re, re + Q_W // 2


def _steps(lo, hi, body, init):
    trips = (hi - lo) // SCAN_UNROLL

    def trip(j, carry):
        for u in range(SCAN_UNROLL):
            carry = body(lo + j * SCAN_UNROLL + u, carry)
        return carry

    carry = lax.fori_loop(0, trips, trip, init)
    for k in range(lo + trips * SCAN_UNROLL, hi):
        carry = body(k, carry)
    return carry


def _tile(k):
    if isinstance(k, int):
        return pl.ds(k * SUBLANES, SUBLANES)
    return pl.ds(pl.multiple_of(k * SUBLANES, SUBLANES), SUBLANES)


def _scan_forward(s_ref, p_ref, carry_ref, enter_ref, fin_ref, k_steps):
    for re, im in _lane_chunks():
        lr, li = pl.ds(re, LANE_CHUNK), pl.ds(im, LANE_CHUNK)
        a_re = jnp.broadcast_to(p_ref[0:1, lr], (SUBLANES, LANE_CHUNK))
        a_im = jnp.broadcast_to(p_ref[0:1, li], (SUBLANES, LANE_CHUNK))

        def local(k, st):
            sr, si = st
            rows = _tile(k)
            nr = a_re * sr - a_im * si + s_ref[rows, lr]
            ni = a_re * si + a_im * sr + s_ref[rows, li]
            s_ref[rows, lr] = nr
            s_ref[rows, li] = ni
            return nr, ni

        zero = jnp.zeros((SUBLANES, LANE_CHUNK), F32)
        fr, fi = _steps(0, k_steps, local, (zero, zero))
        fin_ref[:, lr] = fr
        fin_ref[:, li] = fi
        ak_re, ak_im = p_ref[k_steps - 1:k_steps, lr], p_ref[k_steps - 1:k_steps, li]
        c_re, c_im = carry_ref[:, lr], carry_ref[:, li]
        for seg in range(SUBLANES):
            enter_ref[seg:seg + 1, lr] = c_re
            enter_ref[seg:seg + 1, li] = c_im
            f_re, f_im = fin_ref[seg:seg + 1, lr], fin_ref[seg:seg + 1, li]
            c_re, c_im = f_re + ak_re * c_re - ak_im * c_im, f_im + ak_re * c_im + ak_im * c_re
        carry_ref[:, lr] = c_re
        carry_ref[:, li] = c_im
        e_re, e_im = enter_ref[:, lr], enter_ref[:, li]

        def fix(k, _):
            rows = _tile(k)
            p_re = p_ref[pl.ds(k, 1), lr]
            p_im = p_ref[pl.ds(k, 1), li]
            s_ref[rows, lr] = s_ref[rows, lr] + (p_re * e_re - p_im * e_im)
            s_ref[rows, li] = s_ref[rows, li] + (p_re * e_im + p_im * e_re)
            return 0

        _steps(0, k_steps, fix, 0)


def _scan_backward(g_ref, s_ref, p_ref, carry_ref, s_in_ref, fin_ref, da_ref, k_steps):
    seg_id = lax.broadcasted_iota(jnp.int32, (SUBLANES, LANE_CHUNK), 0)
    for re, im in _lane_chunks():
        lr, li = pl.ds(re, LANE_CHUNK), pl.ds(im, LANE_CHUNK)
        a_re = jnp.broadcast_to(p_ref[0:1, lr], (SUBLANES, LANE_CHUNK))
        a_im = jnp.broadcast_to(p_ref[0:1, li], (SUBLANES, LANE_CHUNK))

        def local(j, st):
            sr, si = st
            rows = _tile(k_steps - 1 - j)
            nr = a_re * sr + a_im * si + g_ref[rows, lr]
            ni = a_re * si - a_im * sr + g_ref[rows, li]
            g_ref[rows, lr] = nr
            g_ref[rows, li] = ni
            return nr, ni

        zero = jnp.zeros((SUBLANES, LANE_CHUNK), F32)
        fr, fi = _steps(0, k_steps, local, (zero, zero))
        fin_ref[:, lr] = fr
        fin_ref[:, li] = fi
        ak_re, ak_im = p_ref[k_steps - 1:k_steps, lr], p_ref[k_steps - 1:k_steps, li]
        c_re, c_im = carry_ref[:, lr], carry_ref[:, li]
        lam_in = [None] * SUBLANES
        for seg in reversed(range(SUBLANES)):
            lam_in[seg] = (c_re, c_im)
            f_re, f_im = fin_ref[seg:seg + 1, lr], fin_ref[seg:seg + 1, li]
            c_re, c_im = f_re + ak_re * c_re + ak_im * c_im, f_im + ak_re * c_im - ak_im * c_re
        carry_ref[:, lr] = c_re
        carry_ref[:, li] = c_im
        for seg in range(SUBLANES):
            fin_ref[seg:seg + 1, lr] = lam_in[seg][0]
            fin_ref[seg:seg + 1, li] = lam_in[seg][1]
        e_re, e_im = fin_ref[:, lr], fin_ref[:, li]

        def fix_with(k, acc, sp_re, sp_im):
            acc_re, acc_im = acc
            rows = _tile(k)
            p_re = p_ref[pl.ds(k_steps - 1 - k, 1), lr]
            p_im = p_ref[pl.ds(k_steps - 1 - k, 1), li]
            l_re = g_ref[rows, lr] + (p_re * e_re + p_im * e_im)
            l_im = g_ref[rows, li] + (p_re * e_im - p_im * e_re)
            g_ref[rows, lr] = l_re
            g_ref[rows, li] = l_im
            return acc_re + (l_re * sp_re + l_im * sp_im), acc_im + (l_im * sp_re - l_re * sp_im)

        def fix(k, acc):
            prev = _tile(k - 1)
            return fix_with(k, acc, s_ref[prev, lr], s_ref[prev, li])

        last = _tile(k_steps - 1)
        before_re = jnp.where(seg_id == 0, s_in_ref[:, lr], pltpu.roll(s_ref[last, lr], 1, axis=0))
        before_im = jnp.where(seg_id == 0, s_in_ref[:, li], pltpu.roll(s_ref[last, li], 1, axis=0))
        acc = fix_with(0, (zero, zero), before_re, before_im)
        acc_re, acc_im = _steps(1, k_steps, fix, acc)
        da_ref[:, lr] = da_ref[:, lr] + jnp.sum(acc_re, axis=0, keepdims=True)
        da_ref[:, li] = da_ref[:, li] + jnp.sum(acc_im, axis=0, keepdims=True)


def _prenorm(x, mod3, norm_pre):
    xn, r = _rms_parts(x)
    return xn, r, xn * norm_pre * (1.0 + mod3[1:2, :]) + mod3[0:1, :]


def _in_proj(x, mod3, norm_pre, w_in, shards):
    rows = x.shape[0]
    tb = _tb(rows, 512)
    nblk = rows // tb
    n_sh = len(shards)
    pool_rows = shards[0].shape[1]
    w_rows = shards[1].shape[0]
    items = [_gather_item(0, 0, _pool_rows_of(pool_rows))] + \
            [_gather_item(t, t, _rows_of(w_rows)) for t in range(1, n_sh)]

    def body(x_ref, mod_ref, np_ref, w_ref, *rest):
        src_refs, proj_ref, out_refs, sems = rest[:n_sh], rest[n_sh], rest[n_sh + 1:2 * n_sh + 1], rest[2 * n_sh + 1:]
        i = pl.program_id(0)

        @pl.when(i == 0)
        def _():
            _hosted_copies(items, src_refs, out_refs, *sems, act="start")

        _, _, h = _prenorm(x_ref[...], mod_ref[...], np_ref[...])
        hb = h.astype(BF16)
        for j in range(N_IN // D):
            cols = slice(j * D, (j + 1) * D)
            proj_ref[:, cols] = _dot(hb, w_ref[:, cols]).astype(BF16)

        @pl.when(i == nblk - 1)
        def _():
            _hosted_copies(items, src_refs, out_refs, *sems, act="wait")

    full = [jax.ShapeDtypeStruct((4, 256, 256), BF16)] + [jax.ShapeDtypeStruct((D, D), BF16)] * (n_sh - 1)
    return _pcall(body, name="in_proj", grid=(nblk,),
                  out_shape=(jax.ShapeDtypeStruct((rows, N_IN), BF16), *full),
                  in_specs=[pl.BlockSpec((tb, D), lambda i: (i, 0)), _full((3, D)), _full((1, D)),
                            _full((D, N_IN), single=True)] + [ANY] * n_sh,
                  out_specs=(pl.BlockSpec((tb, N_IN), lambda i: (i, 0)), *([ANY] * n_sh)),
                  scratch_shapes=_sem_scratch(items),
                  compiler_params=_params(("arbitrary",)))(x, mod3, norm_pre, w_in, *shards)


def _pool_windows(ext, tb, first_row):
    pos = (first_row + lax.broadcasted_iota(jnp.int32, (tb, 1), 0) + 1).astype(F32)
    pooled, counts = [], []
    for g, w in enumerate(POOL_WINDOWS):
        acc = ext[:, g * 256:(g + 1) * 256]
        tok = acc[HALO:, :]
        s = 1
        while s < w:
            acc = acc + pltpu.roll(acc, s, axis=0)
            s *= 2
        cnt = jnp.minimum(pos, float(w))
        pooled.append(acc[HALO:, :] / cnt - tok)
        counts.append(cnt)
    return pooled, counts


def _pool_fwd(proj, pool_w, pool_scale):
    rows = proj.shape[0]
    tb = _tb(rows, 512)
    hb = tb // HALO

    def body(u_ref, halo_ref, z_ref, pw_ref, ps_ref, y_ref):
        i = pl.program_id(0)
        u = u_ref[...].astype(F32)
        halo = jnp.where(i > 0, halo_ref[...].astype(F32), 0.0)
        pooled, _ = _pool_windows(jnp.concatenate([halo, u], axis=0), tb, i * tb)
        silu_z, _ = _silu_parts(z_ref[...].astype(F32))
        for g in range(4):
            cols = slice(g * 256, (g + 1) * 256)
            mixed = _dot(pooled[g].astype(BF16), pw_ref[g])
            y_ref[:, cols] = (mixed * ps_ref[:, cols] * silu_z[:, cols]).astype(BF16)

    return _pcall(body, name="pool_fwd", grid=(rows // tb,),
                  out_shape=jax.ShapeDtypeStruct((rows, D), BF16),
                  in_specs=[pl.BlockSpec((tb, D), lambda i: (i, 0)),
                            pl.BlockSpec((HALO, D), lambda i: (jnp.maximum(i * hb - 1, 0), 0)),
                            pl.BlockSpec((tb, D), lambda i: (i, 1)),
                            _full((4, 256, 256)), _full((1, D))],
                  out_specs=pl.BlockSpec((tb, D), lambda i: (i, 0)),
                  compiler_params=_params(("arbitrary",)))(proj, proj, proj, pool_w, pool_scale)


def _ssm_fwd(proj, pm, pmt, wb, wct, ptab, dvec, glu_w, glu_b):
    rows = proj.shape[0]
    tb = pm.shape[0]
    k_steps = tb // SUBLANES
    nblk = rows // tb

    def body(u_ref, z_ref, pm_ref, pmt_ref, wb_ref, wct_ref, p_ref, d_ref, gw_ref, gb_ref,
             y_ref, ys_ref, carry_out_ref, s_ref, carry_ref, enter_ref, fin_ref):
        @pl.when(pl.program_id(0) == 0)
        def _():
            carry_ref[...] = jnp.zeros_like(carry_ref)

        carry_out_ref[...] = carry_ref[...]
        up = _dot(pm_ref[...], u_ref[...]).astype(BF16)
        for q in range(N_Q):
            s_ref[:, q * Q_W:(q + 1) * Q_W] = _dot(up[:, q * 256:(q + 1) * 256], wb_ref[q])
        _scan_forward(s_ref, p_ref, carry_ref, enter_ref, fin_ref, k_steps)
        for q in range(N_Q):
            cols = slice(q * 256, (q + 1) * 256)
            y = _dot_nt(s_ref[:, q * Q_W:(q + 1) * Q_W].astype(BF16), wct_ref[q])
            ys_ref[:, cols] = y + d_ref[:, cols] * up[:, cols].astype(F32)
        yg, _ = _gelu_parts(ys_ref[...])
        gate = jax.nn.sigmoid(_dot(yg.astype(BF16), gw_ref[...]) + gb_ref[...])
        zp = _dot(pm_ref[...], z_ref[...])
        silu_z, _ = _silu_parts(zp)
        y_ref[...] = _dot(pmt_ref[...], (yg * gate * silu_z).astype(BF16)).astype(BF16)

    return _pcall(body, name="ssm_fwd", grid=(nblk,),
                  out_shape=(jax.ShapeDtypeStruct((rows, D), BF16), jax.ShapeDtypeStruct((rows, D), F32),
                             jax.ShapeDtypeStruct((nblk, 1, N_STATE), F32),
                             jax.ShapeDtypeStruct((rows, N_STATE), F32)),
                  in_specs=[pl.BlockSpec((tb, D), lambda i: (i, 2)), pl.BlockSpec((tb, D), lambda i: (i, 3)),
                            _full((tb, tb)), _full((tb, tb)),
                            _full((N_Q, 256, Q_W), single=True), _full((N_Q, 256, Q_W), single=True),
                            _full((k_steps, N_STATE)), _full((1, D)), _full((D, D), single=True), _full((1, D))],
                  out_specs=(pl.BlockSpec((tb, D), lambda i: (i, 0)), pl.BlockSpec((tb, D), lambda i: (i, 0)),
                             pl.BlockSpec((None, 1, N_STATE), lambda i: (i, 0, 0)),
                             pl.BlockSpec((tb, N_STATE), lambda i: (i, 0))),
                  scratch_shapes=[pltpu.VMEM((1, N_STATE), F32),
                                  pltpu.VMEM((SUBLANES, N_STATE), F32), pltpu.VMEM((SUBLANES, N_STATE), F32)],
                  compiler_params=_params(("arbitrary",)))(proj, proj, pm, pmt, wb, wct, ptab, dvec, glu_w, glu_b)


def _head(x, target, proj, y_pool, y_ssm, mod3, norm_post, wbp, wbs, wout):
    rows = x.shape[0]
    tb = _tb(rows, 256)
    nblk = rows // tb
    n_feat = float(D)

    def body(x_ref, t_ref, gp_ref, gs_ref, yp_ref, ys_ref, mod_ref, npost_ref, wbp_ref, wbs_ref, wout_ref,
             loss_ref, dy_ref, dyp_ref, dys_ref, dg_ref, dwbp_hbm, dwbs_hbm, dwout_hbm, vec_ref,
             acc_bp, acc_bs, acc_out, acc_loss, acc_vec):
        i = pl.program_id(0)

        @pl.when(i == 0)
        def _():
            acc_bp[...] = jnp.zeros_like(acc_bp)
            acc_bs[...] = jnp.zeros_like(acc_bs)
            acc_out[...] = jnp.zeros_like(acc_out)
            acc_loss[...] = jnp.zeros_like(acc_loss)
            acc_vec[...] = jnp.zeros_like(acc_vec)

        yp, ys = yp_ref[...], ys_ref[...]
        sgp = jax.nn.sigmoid(gp_ref[...].astype(F32))
        sgs = jax.nn.sigmoid(gs_ref[...].astype(F32))
        pb = _dot(yp, wbp_ref[...])
        psm = _dot(ys, wbs_ref[...])
        mb = (sgp * pb + sgs * psm).astype(BF16)
        out = _dot(mb, wout_ref[...])
        on, r = _rms_parts(out)
        gate = mod_ref[2:3, :]
        npost = npost_ref[...]
        normed = on * npost
        diff = x_ref[...] + gate * normed - t_ref[...]
        acc_loss[...] += jnp.sum(diff * diff, axis=0, keepdims=True)
        dy = diff * (1.0 / n_feat)
        dy_ref[...] = dy
        acc_vec[0:1, :] += jnp.sum(dy * normed, axis=0, keepdims=True)
        dn = dy * gate
        acc_vec[1:2, :] += jnp.sum(dn * on, axis=0, keepdims=True)
        dout = _rms_bwd(dn * npost, on, r).astype(BF16)
        acc_out[...] += _dot_tn(mb, dout)
        dm = _dot_nt(dout, wout_ref[...])
        dpb = (dm * sgp).astype(BF16)
        dps = (dm * sgs).astype(BF16)
        dg_ref[:, :D] = (dm * pb * sgp * (1.0 - sgp)).astype(BF16)
        dg_ref[:, D:] = (dm * psm * sgs * (1.0 - sgs)).astype(BF16)
        acc_bp[...] += _dot_tn(yp, dpb)
        acc_bs[...] += _dot_tn(ys, dps)
        dyp_ref[...] = _dot_nt(dpb, wbp_ref[...]).astype(BF16)
        dys_ref[...] = _dot_nt(dps, wbs_ref[...]).astype(BF16)

        @pl.when(i == nblk - 1)
        def _():
            loss_ref[...] = 0.5 / n_feat * jnp.sum(acc_loss[...], axis=1, keepdims=True)
            vec_ref[...] = acc_vec[...]
            pltpu.sync_copy(acc_bp, dwbp_hbm)
            pltpu.sync_copy(acc_bs, dwbs_hbm)
            pltpu.sync_copy(acc_out, dwout_hbm)

    row = lambda c: pl.BlockSpec((tb, D), lambda i: (i, c))
    w = _full((D, D), single=True)
    return _pcall(body, name="head", grid=(nblk,),
                  out_shape=(jax.ShapeDtypeStruct((1, 1), F32), jax.ShapeDtypeStruct((rows, D), F32),
                             jax.ShapeDtypeStruct((rows, D), BF16), jax.ShapeDtypeStruct((rows, D), BF16),
                             jax.ShapeDtypeStruct((rows, 2 * D), BF16),
                             jax.ShapeDtypeStruct((D, D), F32), jax.ShapeDtypeStruct((D, D), F32),
                             jax.ShapeDtypeStruct((D, D), F32), jax.ShapeDtypeStruct((2, D), F32)),
                  in_specs=[row(0), row(0), row(4), row(5), row(0), row(0), _full((3, D)), _full((1, D)), w, w, w],
                  out_specs=(_full((1, 1)), row(0), row(0), row(0), pl.BlockSpec((tb, 2 * D), lambda i: (i, 0)),
                             ANY, ANY, ANY, _full((2, D))),
                  scratch_shapes=[pltpu.VMEM((D, D), F32), pltpu.VMEM((D, D), F32), pltpu.VMEM((D, D), F32),
                                  pltpu.VMEM((1, D), F32), pltpu.VMEM((2, D), F32)],
                  compiler_params=_params(("arbitrary",)))(x, target, proj, proj, y_pool, y_ssm, mod3, norm_post,
                                                           wbp, wbs, wout)


def _glu_bwd(dys, proj, ys_pre, pm, pmt, glu_w, glu_b):
    rows = dys.shape[0]
    tb = pm.shape[0]
    nblk = rows // tb

    def body(dys_ref, z_ref, ysp_ref, pm_ref, pmt_ref, gw_ref, gb_ref, dyp_ref, dz_ref, dgw_hbm, dgb_ref,
             acc_w, acc_b):
        i = pl.program_id(0)

        @pl.when(i == 0)
        def _():
            acc_w[...] = jnp.zeros_like(acc_w)
            acc_b[...] = jnp.zeros_like(acc_b)

        d_out = _dot(pm_ref[...], dys_ref[...])
        z = _dot(pm_ref[...], z_ref[...])
        yg, dgelu = _gelu_parts(ysp_ref[...])
        ygb = yg.astype(BF16)
        sg = jax.nn.sigmoid(_dot(ygb, gw_ref[...]) + gb_ref[...])
        silu_z, dsilu_z = _silu_parts(z)
        dz = d_out * (yg * sg) * dsilu_z
        dz_ref[...] = _dot(pmt_ref[...], dz.astype(BF16)).astype(BF16)
        dglu = d_out * silu_z
        dq = dglu * yg * sg * (1.0 - sg)
        dqb = dq.astype(BF16)
        acc_b[...] += jnp.sum(dq, axis=0, keepdims=True)
        acc_w[...] += _dot_tn(ygb, dqb)
        dyg = dglu * sg + _dot_nt(dqb, gw_ref[...])
        dyp_ref[...] = (dyg * dgelu).astype(BF16)

        @pl.when(i == nblk - 1)
        def _():
            dgb_ref[...] = acc_b[...]
            pltpu.sync_copy(acc_w, dgw_hbm)

    row = lambda c: pl.BlockSpec((tb, D), lambda i: (i, c))
    return _pcall(body, name="glu_bwd", grid=(nblk,),
                  out_shape=(jax.ShapeDtypeStruct((rows, D), BF16), jax.ShapeDtypeStruct((rows, D), BF16),
                             jax.ShapeDtypeStruct((D, D), F32), jax.ShapeDtypeStruct((1, D), F32)),
                  in_specs=[row(0), row(3), row(0), _full((tb, tb)), _full((tb, tb)),
                            _full((D, D), single=True), _full((1, D))],
                  out_specs=(row(0), row(0), ANY, _full((1, D))),
                  scratch_shapes=[pltpu.VMEM((D, D), F32), pltpu.VMEM((1, D), F32)],
                  compiler_params=_params(("arbitrary",)))(dys, proj, ys_pre, pm, pmt, glu_w, glu_b)


def _ssm_bwd(dyp, proj, states, carries, pm, pmt, wb, wct, ptab, dvec, mat_grads, dpool_w, dw_in_rest):
    rows = dyp.shape[0]
    tb = pm.shape[0]
    k_steps = tb // SUBLANES
    nblk = rows // tb
    n_mat = len(mat_grads)
    hosted = [*mat_grads, dpool_w, dw_in_rest]
    n_h = len(hosted)
    shard_rows = D // N_DEV
    pool_rows = dpool_w.shape[1] // N_DEV
    items = [_scatter_item(t, t, _rows_of(shard_rows)) for t in range(n_mat)] + \
            [_scatter_item(n_mat, n_mat, _pool_rows_of(pool_rows))] + \
            [_w_in_block_item(n_mat + 1, n_mat + 1, j, ssm_part=False) for j in range(W_IN_SHARD // W_IN_BLOCK)]
    n_in, n_out = 10, 5

    def body(*refs):
        dyp_ref, u_ref, s_ref, cin_ref, pm_ref, pmt_ref, wb_ref, wct_ref, p_ref, d_ref = refs[:n_in]
        src_refs = refs[n_in:n_in + n_h]
        du_ref, dbb_ref, dcc_ref, da_ref, dd_ref = refs[n_in + n_h:n_in + n_h + n_out]
        recv_refs = refs[n_in + n_h + n_out:n_in + 2 * n_h + n_out]
        (g_ref, carry_b, fin_ref, acc_wb, acc_wct, acc_da, acc_dd, dup_ref,
         *sems) = refs[n_in + 2 * n_h + n_out:]
        i = pl.program_id(0)

        @pl.when(i == 0)
        def _():
            _hosted_copies(items, src_refs, recv_refs, *sems, act="start")
            carry_b[...] = jnp.zeros_like(carry_b)
            acc_wb[...] = jnp.zeros_like(acc_wb)
            acc_wct[...] = jnp.zeros_like(acc_wct)
            acc_da[...] = jnp.zeros_like(acc_da)
            acc_dd[...] = jnp.zeros_like(acc_dd)

        dy = dyp_ref[...]
        up = _dot(pm_ref[...], u_ref[...]).astype(BF16)
        acc_dd[...] += jnp.sum(dy.astype(F32) * up.astype(F32), axis=0, keepdims=True)
        for q in range(N_Q):
            cols = slice(q * 256, (q + 1) * 256)
            g_ref[:, q * Q_W:(q + 1) * Q_W] = _dot(dy[:, cols], wct_ref[q])
            acc_wct[q] += _dot_tn(dy[:, cols], s_ref[:, q * Q_W:(q + 1) * Q_W].astype(BF16))
        _scan_backward(g_ref, s_ref, p_ref, carry_b, cin_ref, fin_ref, acc_da, k_steps)
        for q in range(N_Q):
            cols = slice(q * 256, (q + 1) * 256)
            lam = g_ref[:, q * Q_W:(q + 1) * Q_W].astype(BF16)
            acc_wb[q] += _dot_tn(up[:, cols], lam)
            dup_ref[:, cols] = (_dot_nt(lam, wb_ref[q]) + d_ref[:, cols] * dy[:, cols].astype(F32)).astype(BF16)
        du_ref[...] = _dot(pmt_ref[...], dup_ref[...]).astype(BF16)

        @pl.when(i == nblk - 1)
        def _():
            da_ref[...] = acc_da[...]
            dd_ref[...] = acc_dd[...]
            spread, own = _group_masks()
            spread = spread.astype(F32)
            for acc, out in ((acc_wb, dbb_ref), (acc_wct, dcc_ref)):
                for half in range(2):
                    for q in range(N_Q):
                        kept = jnp.where(own, acc[q, :, half * (Q_W // 2):(half + 1) * (Q_W // 2)], 0.0)
                        out[half, q] = lax.dot_general(kept, spread, (((1,), (1,)), ((), ())),
                                                       preferred_element_type=F32, precision=lax.Precision.HIGHEST)
            _hosted_copies(items, src_refs, recv_refs, *sems, act="wait")

    rev = lambda c: pl.BlockSpec((tb, D), lambda i: (nblk - 1 - i, c))
    recv = [jax.ShapeDtypeStruct((N_DEV, shard_rows, D), F32)] * n_mat + \
           [jax.ShapeDtypeStruct((N_DEV, dpool_w.shape[0], pool_rows, dpool_w.shape[2]), F32),
            jax.ShapeDtypeStruct((N_DEV, D, W_IN_SHARD), BF16)]
    return _pcall(body, name="ssm_bwd", grid=(nblk,),
                  out_shape=(jax.ShapeDtypeStruct((rows, D), BF16),
                             jax.ShapeDtypeStruct((2, N_Q, 16 * G_H, G_P), F32),
                             jax.ShapeDtypeStruct((2, N_Q, 16 * G_H, G_P), F32),
                             jax.ShapeDtypeStruct((1, N_STATE), F32), jax.ShapeDtypeStruct((1, D), F32), *recv),
                  in_specs=[rev(0), rev(2), pl.BlockSpec((tb, N_STATE), lambda i: (nblk - 1 - i, 0)),
                            pl.BlockSpec((None, 1, N_STATE), lambda i: (nblk - 1 - i, 0, 0)),
                            _full((tb, tb)), _full((tb, tb)),
                            _full((N_Q, 256, Q_W), single=True), _full((N_Q, 256, Q_W), single=True),
                            _full((k_steps, N_STATE)), _full((1, D))] + [ANY] * n_h,
                  out_specs=(rev(0), _full((2, N_Q, 16 * G_H, G_P)), _full((2, N_Q, 16 * G_H, G_P)),
                             _full((1, N_STATE)), _full((1, D)), *([ANY] * n_h)),
                  scratch_shapes=[pltpu.VMEM((tb, N_STATE), F32), pltpu.VMEM((1, N_STATE), F32),
                                  pltpu.VMEM((SUBLANES, N_STATE), F32),
                                  pltpu.VMEM((N_Q, 256, Q_W), F32), pltpu.VMEM((N_Q, 256, Q_W), F32),
                                  pltpu.VMEM((1, N_STATE), F32), pltpu.VMEM((1, D), F32),
                                  pltpu.VMEM((tb, D), BF16)] + _sem_scratch(items),
                  compiler_params=_params(("arbitrary",), vmem=60 * 1024 * 1024),
                  )(dyp, proj, states, carries, pm, pmt, wb, wct, ptab, dvec, *hosted)


def _pool_bwd(dyp, proj, pool_w, pool_scale):
    rows = dyp.shape[0]
    tb = _tb(rows, 512)
    nblk = rows // tb
    hb = tb // HALO

    def body(dy_ref, u_ref, halo_ref, z_ref, pw_ref, ps_ref, dp_ref, dpw_ref, dps_ref, ahead_ref):
        i = pl.program_id(0)
        blk = nblk - 1 - i

        @pl.when(i == 0)
        def _():
            ahead_ref[...] = jnp.zeros_like(ahead_ref)
            dpw_ref[...] = jnp.zeros_like(dpw_ref)
            dps_ref[...] = jnp.zeros_like(dps_ref)

        u = u_ref[...].astype(F32)
        halo = jnp.where(blk > 0, halo_ref[...].astype(F32), 0.0)
        pooled, counts = _pool_windows(jnp.concatenate([halo, u], axis=0), tb, blk * tb)
        silu_z, dsilu_z = _silu_parts(z_ref[...].astype(F32))
        dy = dy_ref[...].astype(F32)
        for g, w in enumerate(POOL_WINDOWS):
            cols = slice(g * 256, (g + 1) * 256)
            pooled_b = pooled[g].astype(BF16)
            mixed = _dot(pooled_b, pw_ref[g])
            scale = ps_ref[:, cols]
            dp_ref[:, D + g * 256:D + (g + 1) * 256] = (dy[:, cols] * (mixed * scale) * dsilu_z[:, cols]).astype(BF16)
            dms = dy[:, cols] * silu_z[:, cols]
            dps_ref[:, cols] += jnp.sum(dms * mixed, axis=0, keepdims=True)
            dmixed = (dms * scale).astype(BF16)
            dpw_ref[g] += _dot_tn(pooled_b, dmixed)
            dpooled = _dot_nt(dmixed, pw_ref[g])
            ratio = dpooled / counts[g]
            acc = jnp.concatenate([ratio, ahead_ref[:, cols]], axis=0)
            ahead_ref[:, cols] = ratio[:HALO, :]
            s = 1
            while s < w:
                acc = acc + pltpu.roll(acc, tb + HALO - s, axis=0)
                s *= 2
            dp_ref[:, cols] = (acc[:tb, :] - dpooled).astype(BF16)

    rev = lambda c: pl.BlockSpec((tb, D), lambda i: (nblk - 1 - i, c))
    return _pcall(body, name="pool_bwd", grid=(nblk,),
                  out_shape=(jax.ShapeDtypeStruct((rows, 2 * D), BF16), jax.ShapeDtypeStruct((4, 256, 256), F32),
                             jax.ShapeDtypeStruct((1, D), F32)),
                  in_specs=[rev(0), rev(0),
                            pl.BlockSpec((HALO, D), lambda i: (jnp.maximum((nblk - 1 - i) * hb - 1, 0), 0)),
                            rev(1), _full((4, 256, 256)), _full((1, D))],
                  out_specs=(pl.BlockSpec((tb, 2 * D), lambda i: (nblk - 1 - i, 0)), _full((4, 256, 256)),
                             _full((1, D))),
                  scratch_shapes=[pltpu.VMEM((HALO, D), F32)],
                  compiler_params=_params(("arbitrary",)))(dyp, proj, proj, proj, pool_w, pool_scale)


def _dproj_specs(tb):
    return [pl.BlockSpec((tb, 2 * D), lambda i: (i, 0)), pl.BlockSpec((tb, D), lambda i: (i, 0)),
            pl.BlockSpec((tb, D), lambda i: (i, 0)), pl.BlockSpec((tb, 2 * D), lambda i: (i, 0))]


def _in_proj_bwd_x(x, dy, dpp, dus, dzs, dpg, mod3, norm_pre, w_in, dw_in_ssm, small32, small16, recv_w_in):
    rows = x.shape[0]
    tb = _tb(rows, 256)
    nblk = rows // tb
    items = [_w_in_block_item(0, 0, j, ssm_part=True) for j in range(W_IN_SHARD // W_IN_BLOCK)] + \
            [_Item(1, 1, _whole, _slot), _Item(2, 2, _whole, _slot)]

    def body(x_ref, dy_ref, dpp_ref, dus_ref, dzs_ref, dpg_ref, mod_ref, np_ref, w_ref,
             dw_src, s32_src, s16_src, _, gx_ref, vec_ref, recv_w, recv32, recv16, *sems):
        src_refs, recv_refs = (dw_src, s32_src, s16_src), (recv_w, recv32, recv16)

        @pl.when(pl.program_id(0) == 0)
        def _():
            _hosted_copies(items, src_refs, recv_refs, *sems, act="start")
            vec_ref[...] = jnp.zeros_like(vec_ref)

        dh = _dot_nt(dpp_ref[...], w_ref[:, 0:2 * D])
        dh += _dot_nt(dus_ref[...], w_ref[:, 2 * D:3 * D])
        dh += _dot_nt(dzs_ref[...], w_ref[:, 3 * D:4 * D])
        dh += _dot_nt(dpg_ref[...], w_ref[:, 4 * D:6 * D])
        xn, r, _ = _prenorm(x_ref[...], mod_ref[...], np_ref[...])
        one_scale = 1.0 + mod_ref[1:2, :]
        vec_ref[0:1, :] += jnp.sum(dh, axis=0, keepdims=True)
        vec_ref[1:2, :] += jnp.sum(dh * xn, axis=0, keepdims=True) * np_ref[...]
        vec_ref[2:3, :] += jnp.sum(dh * xn, axis=0, keepdims=True) * one_scale
        gx_ref[...] = dy_ref[...] + _rms_bwd(dh * (np_ref[...] * one_scale), xn, r)

        @pl.when(pl.program_id(0) == nblk - 1)
        def _():
            _hosted_copies(items, src_refs, recv_refs, *sems, act="wait")

    row = pl.BlockSpec((tb, D), lambda i: (i, 0))
    recv = (jax.ShapeDtypeStruct(recv_w_in.shape, recv_w_in.dtype),
            jax.ShapeDtypeStruct((N_DEV,) + small32.shape, small32.dtype),
            jax.ShapeDtypeStruct((N_DEV,) + small16.shape, small16.dtype))
    return _pcall(body, name="in_proj_bwd_x", grid=(nblk,),
                  out_shape=(jax.ShapeDtypeStruct((rows, D), F32), jax.ShapeDtypeStruct((3, D), F32), *recv),
                  in_specs=[row, row] + _dproj_specs(tb) + [_full((3, D)), _full((1, D)),
                                                            _full((D, N_IN), single=True)] + [ANY] * 4,
                  out_specs=(row, _full((3, D)), ANY, ANY, ANY),
                  input_output_aliases={12: 2},
                  scratch_shapes=_sem_scratch(items),
                  compiler_params=_params(("arbitrary",)))(x, dy, dpp, dus, dzs, dpg, mod3, norm_pre, w_in,
                                                           dw_in_ssm, small32, small16, recv_w_in)


def _in_proj_bwd_w(name, x, dparts, mod3, norm_pre):
    rows = x.shape[0]
    tb = _tb(rows, 256)
    nblk = rows // tb
    widths = [p.shape[1] for p in dparts]
    n_p = len(dparts)

    def body(x_ref, *rest):
        part_refs, (mod_ref, np_ref, dw_ref, acc) = rest[:n_p], rest[n_p:]
        i = pl.program_id(0)

        @pl.when(i == 0)
        def _():
            acc[...] = jnp.zeros_like(acc)

        _, _, h = _prenorm(x_ref[...], mod_ref[...], np_ref[...])
        ht = h.astype(BF16)
        lo = 0
        for ref, w in zip(part_refs, widths):
            acc[:, lo:lo + w] += _dot_tn(ht, ref[...])
            lo += w

        @pl.when(i == nblk - 1)
        def _():
            dw_ref[...] = acc[...].astype(BF16)

    row = pl.BlockSpec((tb, D), lambda i: (i, 0))
    return _pcall(body, name=name, grid=(nblk,),
                  out_shape=jax.ShapeDtypeStruct((D, sum(widths)), BF16),
                  in_specs=[row] + [pl.BlockSpec((tb, w), lambda i: (i, 0)) for w in widths] +
                           [_full((3, D)), _full((1, D))],
                  out_specs=_full((D, sum(widths))),
                  scratch_shapes=[pltpu.VMEM((D, sum(widths)), F32)],
                  compiler_params=_params(("arbitrary",)))(x, *dparts, mod3, norm_pre)


def _adamw_math(w, g, m, v):
    m = ADAM_B1 * m + (1.0 - ADAM_B1) * g
    v = ADAM_B2 * v + (1.0 - ADAM_B2) * (g * g)
    m_hat = m / (1.0 - ADAM_B1 ** ADAM_STEP)
    v_hat = v / (1.0 - ADAM_B2 ** ADAM_STEP)
    delta = -ADAM_LR * (m_hat / (jnp.sqrt(v_hat) + ADAM_EPS) + ADAM_WD * w)
    return delta, m, v


def _sum_sources(ref):
    g = ref[0].astype(F32)
    for s in range(1, N_DEV):
        g = g + ref[s].astype(F32)
    return g


def _adamw_reduce(name, parts, w, m, v):
    r, c = w.shape
    tr = r if r * c <= 256 * 1024 else max(8, (256 * 1024 // c) // 8 * 8)
    while r % tr:
        tr -= 8

    def body(p_ref, w_ref, m_ref, v_ref, g_ref, d_ref, nm_ref, nv_ref):
        g = _sum_sources(p_ref)
        g_ref[...] = g
        d_ref[...], nm_ref[...], nv_ref[...] = _adamw_math(w_ref[...], g, m_ref[...], v_ref[...])

    blk = pl.BlockSpec((tr, c), lambda i: (i, 0))
    return _pcall(body, name=name, grid=(r // tr,),
                  out_shape=tuple([jax.ShapeDtypeStruct((r, c), F32)] * 4),
                  in_specs=[pl.BlockSpec((N_DEV, tr, c), lambda i: (0, i, 0)), blk, blk, blk],
                  out_specs=(blk, blk, blk, blk),
                  compiler_params=_params(("arbitrary",)))(parts, w, m, v)


def _adamw_plain(name, g, w, m, v):
    def body(g_ref, w_ref, m_ref, v_ref, d_ref, nm_ref, nv_ref):
        d_ref[...], nm_ref[...], nv_ref[...] = _adamw_math(w_ref[...], g_ref[...], m_ref[...], v_ref[...])

    vm = pl.BlockSpec(memory_space=pltpu.VMEM)
    return _pcall(body, name=name, out_shape=tuple([jax.ShapeDtypeStruct(w.shape, F32)] * 3),
                  in_specs=[vm] * 4, out_specs=(vm, vm, vm), compiler_params=_params())(g, w, m, v)


def _sum_small(parts):
    n = len(parts)

    def body(*refs):
        for t in range(n):
            refs[n + t][...] = _sum_sources(refs[t])

    vm = pl.BlockSpec(memory_space=pltpu.VMEM)
    return _pcall(body, name="sum_small",
                  out_shape=tuple(jax.ShapeDtypeStruct(p.shape[1:], F32) for p in parts),
                  in_specs=[vm] * n, out_specs=tuple([vm] * n), compiler_params=_params())(*parts)


def _ada_update(c_all, dmod_cols, w, m, v):
    def body(c_ref, dm_ref, w_ref, m_ref, v_ref, g_ref, d_ref, nm_ref, nv_ref):
        ca = c_ref[...]
        g = lax.dot_general(ca * jax.nn.sigmoid(ca), dm_ref[...], (((0,), (0,)), ((), ())),
                            preferred_element_type=F32, precision=lax.Precision.HIGHEST)
        g_ref[...] = g
        d_ref[...], nm_ref[...], nv_ref[...] = _adamw_math(w_ref[...], g, m_ref[...], v_ref[...])

    vm = pl.BlockSpec(memory_space=pltpu.VMEM)
    return _pcall(body, name="ada_update", out_shape=tuple([jax.ShapeDtypeStruct(w.shape, F32)] * 4),
                  in_specs=[vm] * 5, out_specs=(vm, vm, vm, vm), compiler_params=_params())(c_all, dmod_cols, w, m, v)


def kernel(x, c, w_ada, b_ada, norm_pre, norm_post, w_in, pool_w, pool_scale, ssm_a_re, ssm_a_im, ssm_log_dt, ssm_b_re, ssm_b_im, ssm_c_re, ssm_c_im, ssm_d, glu_w, glu_b, w_branch_pool, w_branch_ssm, w_out, loss_target, m_w_ada, m_b_ada, m_norm_pre, m_norm_post, m_w_in, m_pool_w, m_pool_scale, m_ssm_a_re, m_ssm_a_im, m_ssm_log_dt, m_ssm_b_re, m_ssm_b_im, m_ssm_c_re, m_ssm_c_im, m_ssm_d, m_glu_w, m_glu_b, m_w_branch_pool, m_w_branch_ssm, m_w_out, v_w_ada, v_b_ada, v_norm_pre, v_norm_post, v_w_in, v_pool_w, v_pool_scale, v_ssm_a_re, v_ssm_a_im, v_ssm_log_dt, v_ssm_b_re, v_ssm_b_im, v_ssm_c_re, v_ssm_c_im, v_ssm_d, v_glu_w, v_glu_b, v_w_branch_pool, v_w_branch_ssm, v_w_out):
    given = dict(locals())
    me = _flat(_me())
    rows = x.shape[1]
    x2 = x[0]
    target = loss_target[0]
    ada_cols = w_ada.shape[2]

    b_ada_s = lax.dynamic_slice(b_ada, (0, me * ada_cols), (1, ada_cols))
    c_all, mod_rows = _ada_exchange(c, w_ada[0], b_ada_s)
    mod3 = mod_rows.reshape(3, D)

    shards = _cast_shards([w_in[0], pool_w[0], glu_w[0], w_branch_pool[0], w_branch_ssm[0], w_out[0]])
    (w_in_g,) = _exchange("gather_w_in", [shards[0]], [jax.ShapeDtypeStruct((D, N_IN), BF16)],
                          [_gather_item(0, 0, _cols_of(W_IN_SHARD))])

    tb_ssm = _tb(rows, 256)
    k_steps = tb_ssm // SUBLANES
    a_re, a_im = ssm_a_re[0], ssm_a_im[0]
    log_dt = ssm_log_dt[0].reshape(GROUPS, 1)
    b_re_t, b_im_t = ssm_b_re[0].transpose(0, 2, 1), ssm_b_im[0].transpose(0, 2, 1)
    wb, wct, pow_re, pow_im = _s5_prep(a_re, a_im, log_dt, b_re_t, b_im_t, ssm_c_re[0], ssm_c_im[0], k_steps)
    ptab = _state_layout(pow_re, pow_im)
    dvec = ssm_d[0].reshape(1, D)
    pm = _perm_matrix(tb_ssm)
    pmt = pm.T

    proj, pool_w_g, glu_g, wbp_g, wbs_g, wout_g = _in_proj(x2, mod3, norm_pre, w_in_g, shards[1:])
    y_pool = _pool_fwd(proj, pool_w_g, pool_scale)
    y_ssm, ys_pre, carries, states = _ssm_fwd(proj, pm, pmt, wb, wct, ptab, dvec, glu_g, glu_b)
    loss_part, dy, dyp, dys, dpg, dwbp, dwbs, dwout, head_vec = _head(
        x2, target, proj, y_pool, y_ssm, mod3, norm_post, wbp_g, wbs_g, wout_g)

    dpp, dpool_w, dpool_scale = _pool_bwd(dyp, proj, pool_w_g, pool_scale)
    dw_in_rest = _in_proj_bwd_w("in_proj_bwd_w_rest", x2, [dpp, dpg], mod3, norm_pre)
    dy_pre, dzs, dglu_w, dglu_b = _glu_bwd(dys, proj, ys_pre, pm, pmt, glu_g, glu_b)
    dus, dbb, dcc, dabar, dd, p_glu, p_wbp, p_wbs, p_wout, p_pool_w, p_w_in = _ssm_bwd(
        dy_pre, proj, states, carries, pm, pmt, wb, wct, ptab, dvec, [dglu_w, dwbp, dwbs, dwout], dpool_w, dw_in_rest)
    dw_in_ssm = _in_proj_bwd_w("in_proj_bwd_w_ssm", x2, [dus, dzs], mod3, norm_pre)

    small32 = jnp.concatenate([head_vec, dpool_scale, dglu_b, dd, jnp.broadcast_to(loss_part, (1, D)),
                               jnp.zeros((2, D), F32), dabar.reshape(8, D)], axis=0)
    small16 = jnp.concatenate([dbb.reshape(2 * GROUPS, D), dcc.reshape(2 * GROUPS, D)], axis=0).astype(BF16)
    grad_x, pre_vec, p_w_in, p_small32, p_small16 = _in_proj_bwd_x(
        x2, dy, dpp, dus, dzs, dpg, mod3, norm_pre, w_in_g, dw_in_ssm, small32, small16, p_w_in)
    small_pre = jnp.concatenate([pre_vec, jnp.zeros((5, D), F32)], axis=0)
    (p_pre,) = _exchange("gather_prenorm_sums", [small_pre], [jax.ShapeDtypeStruct((N_DEV, 8, D), F32)],
                         [_Item(0, 0, _whole, _slot)])

    tot32, tot16, tot_pre = _sum_small([p_small32, p_small16, p_pre])
    d_abar_re, d_abar_im = _state_unlayout(tot32[8:16].reshape(N_STATE))
    d_bb_re, d_bb_im = tot16[0:64].reshape(GROUPS, G_H, G_P), tot16[64:128].reshape(GROUPS, G_H, G_P)
    g_a_re, g_a_im, g_log_dt, g_b_re_t, g_b_im_t = _s5_prep_bwd(
        a_re, a_im, log_dt, b_re_t, b_im_t, d_abar_re, d_abar_im, d_bb_re, d_bb_im)

    grads, deltas, new_m, new_v = {}, {}, {}, {}

    def small_update(name, g2):
        shape = given[name].shape
        w2, m2, v2 = (given[p + name].reshape(g2.shape) for p in ("", "m_", "v_"))
        d2, nm2, nv2 = _adamw_plain("adamw_" + name, g2, w2, m2, v2)
        grads[name], deltas[name], new_m[name], new_v[name] = (a.reshape(shape) for a in (g2, d2, nm2, nv2))

    def shard_update(name, parts):
        shape = given[name].shape
        r2 = parts.shape[1:] if parts.ndim == 3 else (parts.shape[1] * parts.shape[2], parts.shape[3])
        w2, m2, v2 = (given[p + name].reshape(r2) for p in ("", "m_", "v_"))
        out = _adamw_reduce("adamw_" + name, parts.reshape((N_DEV,) + tuple(r2)), w2, m2, v2)
        grads[name], deltas[name], new_m[name], new_v[name] = (a.reshape(shape) for a in out)

    dmod_all = jnp.concatenate([p_pre[:, 0:2, :], p_small32[:, 0:1, :]], axis=1).reshape(N_DEV, 3 * D)
    dmod_cols = lax.dynamic_slice(dmod_all, (0, me * ada_cols), (N_DEV, ada_cols))
    out = _ada_update(c_all, dmod_cols, w_ada[0], m_w_ada[0], v_w_ada[0])
    grads['w_ada'], deltas['w_ada'], new_m['w_ada'], new_v['w_ada'] = (a.reshape(w_ada.shape) for a in out)

    small_update('b_ada', jnp.concatenate([tot_pre[0:2], tot32[0:1]], axis=0).reshape(1, 3 * D))
    small_update('norm_pre', tot_pre[2:3])
    small_update('norm_post', tot32[1:2])
    small_update('pool_scale', tot32[2:3])
    small_update('glu_b', tot32[3:4])
    small_update('ssm_d', tot32[4:5])
    small_update('ssm_a_re', g_a_re)
    small_update('ssm_a_im', g_a_im)
    small_update('ssm_log_dt', g_log_dt.reshape(1, GROUPS))
    small_update('ssm_b_re', g_b_re_t.transpose(0, 2, 1).reshape(GROUPS, G_P * G_H))
    small_update('ssm_b_im', g_b_im_t.transpose(0, 2, 1).reshape(GROUPS, G_P * G_H))
    small_update('ssm_c_re', tot16[128:192])
    small_update('ssm_c_im', -tot16[192:256])
    shard_update('w_in', p_w_in)
    shard_update('pool_w', p_pool_w)
    shard_update('glu_w', p_glu)
    shard_update('w_branch_pool', p_wbp)
    shard_update('w_branch_ssm', p_wbs)
    shard_update('w_out', p_wout)

    return (tot32[5, 0], grad_x[None], *[grads[n] for n in WEIGHTS], *[deltas[n] for n in WEIGHTS],
            *[new_m[n] for n in WEIGHTS], *[new_v[n] for n in WEIGHTS])
```

```python
import functools
import math
from typing import Callable, NamedTuple, Optional

import jax
import jax.numpy as jnp
from jax import lax
from jax.experimental import pallas as pl
from jax.experimental.pallas import tpu as pltpu

F32 = jnp.float32
BF16 = jnp.bfloat16
MESH = pl.DeviceIdType.MESH

D = 1024
N_DEV = 8
N_IN = 6 * D
GROUPS = 64
G_H = 16
G_P = 64
N_Q = 4
Q_W = 2 * 16 * G_P
N_STATE = N_Q * Q_W
POOL_WINDOWS = (2, 4, 8, 16)
HALO = 16
RMS_EPS = 1e-6
SUBLANES = 8
LANE_CHUNK = 512
SCAN_UNROLL = 2
VMEM_LIMIT = 56 * 1024 * 1024

ADAM_LR = 0.001
ADAM_B1 = 0.9
ADAM_B2 = 0.999
ADAM_EPS = 1e-08
ADAM_WD = 0.01
ADAM_STEP = 10

WEIGHTS = ['w_ada', 'b_ada', 'norm_pre', 'norm_post', 'w_in', 'pool_w', 'pool_scale', 'ssm_a_re',
           'ssm_a_im', 'ssm_log_dt', 'ssm_b_re', 'ssm_b_im', 'ssm_c_re', 'ssm_c_im', 'ssm_d', 'glu_w',
           'glu_b', 'w_branch_pool', 'w_branch_ssm', 'w_out']


def _pcall(body, **kw):
    return pl.pallas_call(body, **kw)


def _params(sem=None, vmem=VMEM_LIMIT):
    return pltpu.CompilerParams(dimension_semantics=sem, vmem_limit_bytes=vmem)


def _tb(rows, pref):
    return pref if rows % pref == 0 and rows // pref >= 2 else rows // 2


def _full(shape, single=False):
    nd = len(shape)
    if single:
        return pl.BlockSpec(shape, lambda i: (0,) * nd, pipeline_mode=pl.Buffered(1))
    return pl.BlockSpec(shape, lambda i: (0,) * nd)


ANY = pl.BlockSpec(memory_space=pl.ANY)


def _me():
    return lax.axis_index("x"), lax.axis_index("y"), lax.axis_index("c")


def _flat(p):
    return 4 * p[0] + 2 * p[1] + p[2]


def _peer(k):
    x, y, c = _me()
    return (1 - x if k & 4 else x, 1 - y if k & 2 else y, 1 - c if k & 1 else c)


def _silu_parts(z):
    s = jax.nn.sigmoid(z)
    return z * s, s * (1.0 + z * (1.0 - s))


_GELU_C = math.sqrt(2.0 / math.pi)


def _gelu_parts(x):
    x2 = x * x
    t = jnp.tanh(_GELU_C * (x + 0.044715 * x * x2))
    g = 0.5 * x * (1.0 + t)
    dg = 0.5 * (1.0 + t) + 0.5 * x * (1.0 - t * t) * (_GELU_C * (1.0 + 3.0 * 0.044715 * x2))
    return g, dg


def _dot(a, b):
    return jnp.dot(a, b, preferred_element_type=F32)


def _dot_nt(a, b):
    return lax.dot_general(a, b, (((1,), (1,)), ((), ())), preferred_element_type=F32)


def _dot_tn(a, b):
    return lax.dot_general(a, b, (((0,), (0,)), ((), ())), preferred_element_type=F32)


def _rms_parts(x):
    r = lax.rsqrt(jnp.mean(x * x, axis=-1, keepdims=True) + RMS_EPS)
    return x * r, r


def _rms_bwd(dxn, xn, r):
    return r * (dxn - xn * jnp.mean(dxn * xn, axis=-1, keepdims=True))


def _ada_exchange(c, w_ada_s, b_ada_s):
    cols = w_ada_s.shape[1]

    def body(c_ref, w_ref, b_ref, call_ref, mod_ref, part_ref, ssem, rsem, lsem):
        me3 = _me()
        me = _flat(me3)
        mine = pltpu.make_async_copy(c_ref, call_ref.at[pl.ds(me, 1), :], lsem.at[0])
        mine.start()
        sends = []
        for k in range(1, N_DEV):
            cp = pltpu.make_async_remote_copy(src_ref=c_ref, dst_ref=call_ref.at[pl.ds(me, 1), :],
                                              send_sem=ssem.at[k - 1], recv_sem=rsem.at[k - 1],
                                              device_id=_peer(k), device_id_type=MESH)
            cp.start()
            sends.append(cp)
        mine.wait()
        for k in range(1, N_DEV):
            p = _flat(_peer(k))
            pltpu.make_async_remote_copy(src_ref=c_ref, dst_ref=call_ref.at[pl.ds(p, 1), :],
                                         send_sem=ssem.at[k - 1], recv_sem=rsem.at[k - 1],
                                         device_id=_peer(k), device_id_type=MESH).wait_recv()
        for cp in sends:
            cp.wait_send()
        ca = call_ref[...]
        act = ca * jax.nn.sigmoid(ca)
        part_ref[...] = jnp.dot(act, w_ref[...], preferred_element_type=F32,
                                precision=lax.Precision.HIGHEST) + b_ref[...]
        own = pltpu.make_async_copy(part_ref.at[pl.ds(me, 1), :], mod_ref.at[pl.ds(me, 1), :], lsem.at[1])
        own.start()
        sends = []
        for k in range(1, N_DEV):
            p = _flat(_peer(k))
            s = N_DEV - 1 + k - 1
            cp = pltpu.make_async_remote_copy(src_ref=part_ref.at[pl.ds(p, 1), :],
                                              dst_ref=mod_ref.at[pl.ds(me, 1), :],
                                              send_sem=ssem.at[s], recv_sem=rsem.at[s],
                                              device_id=_peer(k), device_id_type=MESH)
            cp.start()
            sends.append(cp)
        own.wait()
        for k in range(1, N_DEV):
            p = _flat(_peer(k))
            s = N_DEV - 1 + k - 1
            pltpu.make_async_remote_copy(src_ref=part_ref.at[pl.ds(p, 1), :],
                                         dst_ref=mod_ref.at[pl.ds(p, 1), :],
                                         send_sem=ssem.at[s], recv_sem=rsem.at[s],
                                         device_id=_peer(k), device_id_type=MESH).wait_recv()
        for cp in sends:
            cp.wait_send()

    vm = pl.BlockSpec(memory_space=pltpu.VMEM)
    return _pcall(
        body, name="ada_exchange",
        out_shape=(jax.ShapeDtypeStruct((N_DEV, D), F32), jax.ShapeDtypeStruct((N_DEV, cols), F32)),
        in_specs=[vm, vm, vm], out_specs=(vm, vm),
        scratch_shapes=[pltpu.VMEM((N_DEV, cols), F32),
                        pltpu.SemaphoreType.DMA((2 * (N_DEV - 1),)),
                        pltpu.SemaphoreType.DMA((2 * (N_DEV - 1),)),
                        pltpu.SemaphoreType.DMA((2,))],
    )(c, w_ada_s, b_ada_s)


class _Item(NamedTuple):
    src: int
    out: int
    src_view: Callable
    dst_view: Callable
    pred: Optional[Callable] = None


def _when(pred, dest, fn):
    if pred is None:
        fn()
    else:
        pl.when(pred(dest))(fn)


def _n_sems(items):
    return len(items) * (N_DEV - 1)


def _hosted_copies(items, srcs, outs, ssem, rsem, lsem, act):
    me = _flat(_me())
    for t, it in enumerate(items):
        local = lambda t=t, it=it: pltpu.make_async_copy(
            it.src_view(srcs[it.src], me), it.dst_view(outs[it.out], me), lsem.at[t])
        if act == "start":
            _when(it.pred, me, lambda local=local: local().start())
        else:
            _when(it.pred, me, lambda local=local: local().wait())
    for k in range(1, N_DEV):
        p3 = _peer(k)
        p = _flat(p3)
        for t, it in enumerate(items):
            s = t * (N_DEV - 1) + k - 1
            send = lambda it=it, s=s, p=p, p3=p3: pltpu.make_async_remote_copy(
                src_ref=it.src_view(srcs[it.src], p), dst_ref=it.dst_view(outs[it.out], me),
                send_sem=ssem.at[s], recv_sem=rsem.at[s], device_id=p3, device_id_type=MESH)
            recv = lambda it=it, s=s, p=p, p3=p3: pltpu.make_async_remote_copy(
                src_ref=it.src_view(srcs[it.src], p), dst_ref=it.dst_view(outs[it.out], p),
                send_sem=ssem.at[s], recv_sem=rsem.at[s], device_id=p3, device_id_type=MESH)
            if act == "start":
                _when(it.pred, p, lambda send=send: send().start())
            else:
                _when(it.pred, me, lambda recv=recv: recv().wait_recv())
                _when(it.pred, p, lambda send=send: send().wait_send())


def _sem_scratch(items):
    return [pltpu.SemaphoreType.DMA((_n_sems(items),)), pltpu.SemaphoreType.DMA((_n_sems(items),)),
            pltpu.SemaphoreType.DMA((len(items),))]


def _exchange(name, srcs, out_structs, items):
    n_src, n_out = len(srcs), len(out_structs)

    def body(*refs):
        src_refs, out_refs = refs[:n_src], refs[n_src:n_src + n_out]
        sems = refs[n_src + n_out:]
        _hosted_copies(items, src_refs, out_refs, *sems, act="start")
        _hosted_copies(items, src_refs, out_refs, *sems, act="wait")

    return _pcall(body, name=name, out_shape=tuple(out_structs),
                  in_specs=[ANY] * n_src, out_specs=tuple([ANY] * n_out),
                  scratch_shapes=_sem_scratch(items))(*srcs)


def _whole(ref, dest):
    return ref


def _slot(ref, sender):
    return ref.at[sender]


def _rows_of(rows):
    return lambda ref, dev: ref.at[pl.ds(dev * rows, rows), :]


def _cols_of(cols):
    return lambda ref, dev: ref.at[:, pl.ds(dev * cols, cols)]


def _pool_rows_of(rows):
    return lambda ref, dev: ref.at[:, pl.ds(dev * rows, rows), :]


def _gather_item(src, out, dst_view):
    return _Item(src, out, _whole, dst_view)


def _scatter_item(src, out, src_view):
    return _Item(src, out, src_view, _slot)


W_IN_BLOCK = 256
W_IN_SHARD = N_IN // N_DEV
SSM_BLOCKS = (2 * D // W_IN_BLOCK, 4 * D // W_IN_BLOCK)


def _w_in_block_item(src, out, j, ssm_part):
    def block(dest):
        return (W_IN_SHARD // W_IN_BLOCK) * dest + j

    def in_ssm(dest):
        b = block(dest)
        return (b >= SSM_BLOCKS[0]) & (b < SSM_BLOCKS[1])

    def src_view(ref, dest):
        b = block(dest)
        local = b - SSM_BLOCKS[0] if ssm_part else jnp.where(b < SSM_BLOCKS[0], b, b - (SSM_BLOCKS[1] - SSM_BLOCKS[0]))
        local = jnp.clip(local, 0, ref.shape[1] // W_IN_BLOCK - 1)
        return ref.at[:, pl.ds(local * W_IN_BLOCK, W_IN_BLOCK)]

    def dst_view(ref, sender):
        return ref.at[sender, :, pl.ds(j * W_IN_BLOCK, W_IN_BLOCK)]

    pred = in_ssm if ssm_part else (lambda dest: jnp.logical_not(in_ssm(dest)))
    return _Item(src, out, src_view, dst_view, pred)


def _cast_shards(arrs):
    def body(*refs):
        n = len(refs) // 2
        for i in range(n):
            refs[n + i][...] = refs[i][...].astype(BF16)

    vm = pl.BlockSpec(memory_space=pltpu.VMEM)
    return _pcall(body, name="cast_shards",
                  out_shape=tuple(jax.ShapeDtypeStruct(a.shape, BF16) for a in arrs),
                  in_specs=[vm] * len(arrs), out_specs=tuple([vm] * len(arrs)),
                  compiler_params=_params())(*arrs)


def _s5_discretise(a_re, a_im, log_dt, b_re_t, b_im_t):
    dt = jnp.exp(log_dt)
    lam_re = jnp.minimum(a_re, -1e-4)
    lam_im = a_im
    mag = jnp.exp(lam_re * dt)
    abar_re = mag * jnp.cos(lam_im * dt)
    abar_im = mag * jnp.sin(lam_im * dt)
    den = lam_re * lam_re + lam_im * lam_im
    num_re = abar_re - 1.0
    f_re = (num_re * lam_re + abar_im * lam_im) / den
    f_im = (abar_im * lam_re - num_re * lam_im) / den
    f_re, f_im = f_re[:, None, :], f_im[:, None, :]
    bb_re = f_re * b_re_t - f_im * b_im_t
    bb_im = f_re * b_im_t + f_im * b_re_t
    return abar_re, abar_im, bb_re, bb_im


def _group_masks():
    spread = lax.broadcasted_iota(jnp.int32, (G_P, 16 * G_P), 1) % G_P == lax.broadcasted_iota(
        jnp.int32, (G_P, 16 * G_P), 0)
    own = lax.broadcasted_iota(jnp.int32, (16 * G_H, 16 * G_P), 0) // G_H == lax.broadcasted_iota(
        jnp.int32, (16 * G_H, 16 * G_P), 1) // G_P
    return spread, own


def _s5_prep(a_re, a_im, log_dt, b_re_t, b_im_t, c_re, c_im, n_pow):
    def body(ar_ref, ai_ref, ld_ref, br_ref, bi_ref, cr_ref, ci_ref, wb_ref, wct_ref, pr_ref, pi_ref):
        abar_re, abar_im, bb_re, bb_im = _s5_discretise(ar_ref[...], ai_ref[...], ld_ref[...], br_ref[...], bi_ref[...])
        spread, own = _group_masks()
        spread = spread.astype(BF16)
        for ref, parts in ((wb_ref, (bb_re, bb_im)), (wct_ref, (cr_ref[...], -ci_ref[...]))):
            for half, t in enumerate(parts):
                for q in range(N_Q):
                    blocks = t[q * 16:(q + 1) * 16].reshape(16 * G_H, G_P).astype(BF16)
                    dense = jnp.where(own, _dot(blocks, spread), 0.0)
                    ref[q, :, half * (Q_W // 2):(half + 1) * (Q_W // 2)] = dense.astype(BF16)
        p_re, p_im = abar_re, abar_im
        pr_ref[0] = p_re
        pi_ref[0] = p_im
        for k in range(1, n_pow):
            p_re, p_im = p_re * abar_re - p_im * abar_im, p_re * abar_im + p_im * abar_re
            pr_ref[k] = p_re
            pi_ref[k] = p_im

    vm = pl.BlockSpec(memory_space=pltpu.VMEM)
    return _pcall(body, name="s5_prep",
                  out_shape=(jax.ShapeDtypeStruct((N_Q, 16 * G_H, Q_W), BF16),
                             jax.ShapeDtypeStruct((N_Q, 16 * G_H, Q_W), BF16),
                             jax.ShapeDtypeStruct((n_pow, GROUPS, G_P), F32),
                             jax.ShapeDtypeStruct((n_pow, GROUPS, G_P), F32)),
                  in_specs=[vm] * 7, out_specs=(vm, vm, vm, vm), compiler_params=_params(),
                  )(a_re, a_im, log_dt, b_re_t, b_im_t, c_re, c_im)


def _s5_prep_bwd(a_re, a_im, log_dt, b_re_t, b_im_t, d_abar_re, d_abar_im, d_bb_re, d_bb_im):
    def body(ar_ref, ai_ref, ld_ref, br_ref, bi_ref, dar_ref, dai_ref, dbr_ref, dbi_ref,
             gar_ref, gai_ref, gld_ref, gbr_ref, gbi_ref):
        _, vjp = jax.vjp(_s5_discretise, ar_ref[...], ai_ref[...], ld_ref[...], br_ref[...], bi_ref[...])
        g = vjp((dar_ref[...], dai_ref[...], dbr_ref[...], dbi_ref[...]))
        gar_ref[...] = g[0]
        gai_ref[...] = g[1]
        gld_ref[...] = g[2]
        gbr_ref[...] = g[3]
        gbi_ref[...] = g[4]

    vm = pl.BlockSpec(memory_space=pltpu.VMEM)
    ins = (a_re, a_im, log_dt, b_re_t, b_im_t)
    return _pcall(body, name="s5_prep_bwd",
                  out_shape=tuple(jax.ShapeDtypeStruct(a.shape, F32) for a in ins),
                  in_specs=[vm] * 9, out_specs=tuple([vm] * 5), compiler_params=_params(),
                  )(*ins, d_abar_re, d_abar_im, d_bb_re, d_bb_im)


def _state_layout(re, im):
    lead = re.shape[:-2]
    r = re.reshape(lead + (N_Q, 1, 16 * G_P))
    i = im.reshape(lead + (N_Q, 1, 16 * G_P))
    return jnp.concatenate([r, i], axis=-2).reshape(lead + (N_STATE,))


def _state_unlayout(v):
    v4 = v.reshape(N_Q, 2, 16, G_P)
    return v4[:, 0].reshape(GROUPS, G_P), v4[:, 1].reshape(GROUPS, G_P)


def _perm_matrix(tb):
    k_steps = tb // SUBLANES
    r = jnp.arange(tb)
    src = (r % SUBLANES) * k_steps + r // SUBLANES
    return (src[:, None] == jnp.arange(tb)[None, :]).astype(BF16)


def _lane_chunks():
    for q in range(N_Q):
        for lc in range(Q_W // 2 // LANE_CHUNK):
            re = q * Q_W + lc * LANE_CHUNK
            yield re, re + Q_W // 2


def _steps(lo, hi, body, init):
    trips = (hi - lo) // SCAN_UNROLL

    def trip(j, carry):
        for u in range(SCAN_UNROLL):
            carry = body(lo + j * SCAN_UNROLL + u, carry)
        return carry

    carry = lax.fori_loop(0, trips, trip, init)
    for k in range(lo + trips * SCAN_UNROLL, hi):
        carry = body(k, carry)
    return carry


def _tile(k):
    if isinstance(k, int):
        return pl.ds(k * SUBLANES, SUBLANES)
    return pl.ds(pl.multiple_of(k * SUBLANES, SUBLANES), SUBLANES)


def _scan_forward(s_ref, p_ref, carry_ref, enter_ref, fin_ref, k_steps):
    for re, im in _lane_chunks():
        lr, li = pl.ds(re, LANE_CHUNK), pl.ds(im, LANE_CHUNK)
        a_re = jnp.broadcast_to(p_ref[0:1, lr], (SUBLANES, LANE_CHUNK))
        a_im = jnp.broadcast_to(p_ref[0:1, li], (SUBLANES, LANE_CHUNK))

        def local(k, st):
            sr, si = st
            rows = _tile(k)
            nr = a_re * sr - a_im * si + s_ref[rows, lr]
            ni = a_re * si + a_im * sr + s_ref[rows, li]
            s_ref[rows, lr] = nr
            s_ref[rows, li] = ni
            return nr, ni

        zero = jnp.zeros((SUBLANES, LANE_CHUNK), F32)
        fr, fi = _steps(0, k_steps, local, (zero, zero))
        fin_ref[:, lr] = fr
        fin_ref[:, li] = fi
        ak_re, ak_im = p_ref[k_steps - 1:k_steps, lr], p_ref[k_steps - 1:k_steps, li]
        c_re, c_im = carry_ref[:, lr], carry_ref[:, li]
        for seg in range(SUBLANES):
            enter_ref[seg:seg + 1, lr] = c_re
            enter_ref[seg:seg + 1, li] = c_im
            f_re, f_im = fin_ref[seg:seg + 1, lr], fin_ref[seg:seg + 1, li]
            c_re, c_im = f_re + ak_re * c_re - ak_im * c_im, f_im + ak_re * c_im + ak_im * c_re
        carry_ref[:, lr] = c_re
        carry_ref[:, li] = c_im
        e_re, e_im = enter_ref[:, lr], enter_ref[:, li]

        def fix(k, _):
            rows = _tile(k)
            p_re = p_ref[pl.ds(k, 1), lr]
            p_im = p_ref[pl.ds(k, 1), li]
            s_ref[rows, lr] = s_ref[rows, lr] + (p_re * e_re - p_im * e_im)
            s_ref[rows, li] = s_ref[rows, li] + (p_re * e_im + p_im * e_re)
            return 0

        _steps(0, k_steps, fix, 0)


def _scan_backward(g_ref, s_ref, p_ref, carry_ref, s_in_ref, fin_ref, da_ref, k_steps):
    seg_id = lax.broadcasted_iota(jnp.int32, (SUBLANES, LANE_CHUNK), 0)
    for re, im in _lane_chunks():
        lr, li = pl.ds(re, LANE_CHUNK), pl.ds(im, LANE_CHUNK)
        a_re = jnp.broadcast_to(p_ref[0:1, lr], (SUBLANES, LANE_CHUNK))
        a_im = jnp.broadcast_to(p_ref[0:1, li], (SUBLANES, LANE_CHUNK))

        def local(j, st):
            sr, si = st
            rows = _tile(k_steps - 1 - j)
            nr = a_re * sr + a_im * si + g_ref[rows, lr]
            ni = a_re * si - a_im * sr + g_ref[rows, li]
            g_ref[rows, lr] = nr
            g_ref[rows, li] = ni
            return nr, ni

        zero = jnp.zeros((SUBLANES, LANE_CHUNK), F32)
        fr, fi = _steps(0, k_steps, local, (zero, zero))
        fin_ref[:, lr] = fr
        fin_ref[:, li] = fi
        ak_re, ak_im = p_ref[k_steps - 1:k_steps, lr], p_ref[k_steps - 1:k_steps, li]
        c_re, c_im = carry_ref[:, lr], carry_ref[:, li]
        lam_in = [None] * SUBLANES
        for seg in reversed(range(SUBLANES)):
            lam_in[seg] = (c_re, c_im)
            f_re, f_im = fin_ref[seg:seg + 1, lr], fin_ref[seg:seg + 1, li]
            c_re, c_im = f_re + ak_re * c_re + ak_im * c_im, f_im + ak_re * c_im - ak_im * c_re
        carry_ref[:, lr] = c_re
        carry_ref[:, li] = c_im
        for seg in range(SUBLANES):
            fin_ref[seg:seg + 1, lr] = lam_in[seg][0]
            fin_ref[seg:seg + 1, li] = lam_in[seg][1]
        e_re, e_im = fin_ref[:, lr], fin_ref[:, li]

        def fix_with(k, acc, sp_re, sp_im):
            acc_re, acc_im = acc
            rows = _tile(k)
            p_re = p_ref[pl.ds(k_steps - 1 - k, 1), lr]
            p_im = p_ref[pl.ds(k_steps - 1 - k, 1), li]
            l_re = g_ref[rows, lr] + (p_re * e_re + p_im * e_im)
            l_im = g_ref[rows, li] + (p_re * e_im - p_im * e_re)
            g_ref[rows, lr] = l_re
            g_ref[rows, li] = l_im
            return acc_re + (l_re * sp_re + l_im * sp_im), acc_im + (l_im * sp_re - l_re * sp_im)

        def fix(k, acc):
            prev = _tile(k - 1)
            return fix_with(k, acc, s_ref[prev, lr], s_ref[prev, li])

        last = _tile(k_steps - 1)
        before_re = jnp.where(seg_id == 0, s_in_ref[:, lr], pltpu.roll(s_ref[last, lr], 1, axis=0))
        before_im = jnp.where(seg_id == 0, s_in_ref[:, li], pltpu.roll(s_ref[last, li], 1, axis=0))
        acc = fix_with(0, (zero, zero), before_re, before_im)
        acc_re, acc_im = _steps(1, k_steps, fix, acc)
        da_ref[:, lr] = da_ref[:, lr] + jnp.sum(acc_re, axis=0, keepdims=True)
        da_ref[:, li] = da_ref[:, li] + jnp.sum(acc_im, axis=0, keepdims=True)


def _prenorm(x, mod3, norm_pre):
    xn, r = _rms_parts(x)
    return xn, r, xn * norm_pre * (1.0 + mod3[1:2, :]) + mod3[0:1, :]


CHIP_FLIPS = (4, 2, 6)


def _shard_order(me):
    flips = [0, 1] + [f + c for f in CHIP_FLIPS for c in (0, 1)]
    return jnp.stack([me ^ f for f in flips]).astype(jnp.int32)


def _in_proj(x, mod3, norm_pre, w_in_s, shards):
    rows = x.shape[0]
    tb = _tb(rows, 512)
    nblk = rows // tb
    n_sh = len(shards)
    last_step = N_DEV - 1
    items = [_gather_item(0, 0, _pool_rows_of(shards[0].shape[1]))] + \
            [_gather_item(t, t, _rows_of(shards[t].shape[0])) for t in range(1, n_sh)]

    def body(order_ref, x_ref, mod_ref, np_ref, w_src, *rest):
        src_refs, proj_ref, w_full, out_refs = rest[:n_sh], rest[n_sh], rest[n_sh + 1], rest[n_sh + 2:2 * n_sh + 2]
        h_scr, wg, ssem, rsem, lsem, *sems = rest[2 * n_sh + 2:]
        s, i = pl.program_id(0), pl.program_id(1)
        me3 = _me()
        me = _flat(me3)
        sibling = _peer(1)

        def own_copy(slot, k):
            return pltpu.make_async_remote_copy(src_ref=w_src, dst_ref=wg.at[me], send_sem=ssem.at[slot],
                                                recv_sem=rsem.at[slot], device_id=_peer(k), device_id_type=MESH)

        def passed_copy(j):
            p = _flat(_peer(CHIP_FLIPS[j]))
            return pltpu.make_async_remote_copy(src_ref=wg.at[p], dst_ref=wg.at[p], send_sem=ssem.at[4 + j],
                                                recv_sem=rsem.at[4 + j], device_id=sibling, device_id_type=MESH)

        def arrival(slot, flip):
            p = _flat(_peer(flip))
            pltpu.make_async_remote_copy(src_ref=w_src, dst_ref=wg.at[p], send_sem=ssem.at[slot],
                                         recv_sem=rsem.at[slot], device_id=sibling, device_id_type=MESH).wait_recv()

        first = i == 0

        @pl.when(first & (s == 0))
        def _():
            mine = pltpu.make_async_copy(w_src, wg.at[me], lsem.at[0])
            mine.start()
            own_copy(0, 1).start()
            for j, f in enumerate(CHIP_FLIPS):
                own_copy(1 + j, f).start()
            mine.wait()

        @pl.when(first & (s == 1))
        def _():
            arrival(0, 1)

        for j, f in enumerate(CHIP_FLIPS):
            @pl.when(first & (s == 2 + 2 * j))
            def _(j=j, f=f):
                arrival(1 + j, f)
                passed_copy(j).start()

            @pl.when(first & (s == 3 + 2 * j))
            def _(j=j, f=f):
                arrival(4 + j, f + 1)

        @pl.when(first & (s == last_step - 1))
        def _():
            _hosted_copies(items, src_refs, out_refs, *sems, act="start")

        rows_i = pl.ds(pl.multiple_of(i * tb, tb), tb)

        @pl.when(s == 0)
        def _():
            _, _, h = _prenorm(x_ref[...], mod_ref[...], np_ref[...])
            h_scr[rows_i, :] = h.astype(BF16)

        proj_ref[...] = _dot(h_scr[rows_i, :], wg[order_ref[s]]).astype(BF16)

        @pl.when((s == last_step) & (i == nblk - 1))
        def _():
            own_copy(0, 1).wait_send()
            for j, f in enumerate(CHIP_FLIPS):
                own_copy(1 + j, f).wait_send()
                passed_copy(j).wait_send()
            outs = [pltpu.make_async_copy(wg.at[p], w_full.at[:, pl.ds(p * W_IN_SHARD, W_IN_SHARD)], lsem.at[1 + p])
                    for p in range(N_DEV)]
            for cp in outs:
                cp.start()
            for cp in outs:
                cp.wait()
            _hosted_copies(items, src_refs, out_refs, *sems, act="wait")

    full = [jax.ShapeDtypeStruct((4, 256, 256), BF16)] + [jax.ShapeDtypeStruct((D, D), BF16)] * (n_sh - 1)
    grid_spec = pltpu.PrefetchScalarGridSpec(
        num_scalar_prefetch=1, grid=(N_DEV, nblk),
        in_specs=[pl.BlockSpec((tb, D), lambda s, i, order: (jnp.where(s == 0, i, nblk - 1), 0)),
                  pl.BlockSpec((3, D), lambda s, i, order: (0, 0)), pl.BlockSpec((1, D), lambda s, i, order: (0, 0)),
                  ANY] + [ANY] * n_sh,
        out_specs=(pl.BlockSpec((tb, W_IN_SHARD), lambda s, i, order: (i, order[s])), ANY, *([ANY] * n_sh)),
        scratch_shapes=[pltpu.VMEM((rows, D), BF16), pltpu.VMEM((N_DEV, D, W_IN_SHARD), BF16),
                        pltpu.SemaphoreType.DMA((N_DEV - 1,)), pltpu.SemaphoreType.DMA((N_DEV - 1,)),
                        pltpu.SemaphoreType.DMA((1 + N_DEV,))] + _sem_scratch(items))
    return _pcall(body, name="in_proj", grid_spec=grid_spec,
                  out_shape=(jax.ShapeDtypeStruct((rows, N_IN), BF16), jax.ShapeDtypeStruct((D, N_IN), BF16), *full),
                  compiler_params=_params(("arbitrary", "arbitrary")),
                  )(_shard_order(_flat(_me())), x, mod3, norm_pre, w_in_s, *shards)


def _pool_windows(ext, tb, first_row):
    pos = (first_row + lax.broadcasted_iota(jnp.int32, (tb, 1), 0) + 1).astype(F32)
    pooled, counts = [], []
    for g, w in enumerate(POOL_WINDOWS):
        acc = ext[:, g * 256:(g + 1) * 256]
        tok = acc[HALO:, :]
        s = 1
        while s < w:
            acc = acc + pltpu.roll(acc, s, axis=0)
            s *= 2
        cnt = jnp.minimum(pos, float(w))
        pooled.append(acc[HALO:, :] / cnt - tok)
        counts.append(cnt)
    return pooled, counts


def _pool_fwd(proj, pool_w, pool_scale):
    rows = proj.shape[0]
    tb = _tb(rows, 512)
    hb = tb // HALO

    def body(u_ref, halo_ref, z_ref, pw_ref, ps_ref, y_ref):
        i = pl.program_id(0)
        u = u_ref[...].astype(F32)
        halo = jnp.where(i > 0, halo_ref[...].astype(F32), 0.0)
        pooled, _ = _pool_windows(jnp.concatenate([halo, u], axis=0), tb, i * tb)
        silu_z, _ = _silu_parts(z_ref[...].astype(F32))
        for g in range(4):
            cols = slice(g * 256, (g + 1) * 256)
            mixed = _dot(pooled[g].astype(BF16), pw_ref[g])
            y_ref[:, cols] = (mixed * ps_ref[:, cols] * silu_z[:, cols]).astype(BF16)

    return _pcall(body, name="pool_fwd", grid=(rows // tb,),
                  out_shape=jax.ShapeDtypeStruct((rows, D), BF16),
                  in_specs=[pl.BlockSpec((tb, D), lambda i: (i, 0)),
                            pl.BlockSpec((HALO, D), lambda i: (jnp.maximum(i * hb - 1, 0), 0)),
                            pl.BlockSpec((tb, D), lambda i: (i, 1)),
                            _full((4, 256, 256)), _full((1, D))],
                  out_specs=pl.BlockSpec((tb, D), lambda i: (i, 0)),
                  compiler_params=_params(("arbitrary",)))(proj, proj, proj, pool_w, pool_scale)


def _ssm_fwd(proj, pm, pmt, wb, wct, ptab, dvec, glu_w, glu_b, shards):
    rows = proj.shape[0]
    tb = pm.shape[0]
    k_steps = tb // SUBLANES
    nblk = rows // tb
    n_sh = len(shards)
    items = [_gather_item(t, t, _rows_of(shards[t].shape[0])) for t in range(n_sh)]

    def body(u_ref, z_ref, pm_ref, pmt_ref, wb_ref, wct_ref, p_ref, d_ref, gw_ref, gb_ref, *rest):
        src_refs = rest[:n_sh]
        y_ref, ys_ref, carry_out_ref, s_ref = rest[n_sh:n_sh + 4]
        out_refs = rest[n_sh + 4:2 * n_sh + 4]
        carry_ref, enter_ref, fin_ref, *sems = rest[2 * n_sh + 4:]

        @pl.when(pl.program_id(0) == 0)
        def _():
            _hosted_copies(items, src_refs, out_refs, *sems, act="start")
            carry_ref[...] = jnp.zeros_like(carry_ref)

        carry_out_ref[...] = carry_ref[...]
        up = _dot(pm_ref[...], u_ref[...]).astype(BF16)
        for q in range(N_Q):
            s_ref[:, q * Q_W:(q + 1) * Q_W] = _dot(up[:, q * 256:(q + 1) * 256], wb_ref[q])
        _scan_forward(s_ref, p_ref, carry_ref, enter_ref, fin_ref, k_steps)
        for q in range(N_Q):
            cols = slice(q * 256, (q + 1) * 256)
            y = _dot_nt(s_ref[:, q * Q_W:(q + 1) * Q_W].astype(BF16), wct_ref[q])
            ys_ref[:, cols] = y + d_ref[:, cols] * up[:, cols].astype(F32)
        yg, _ = _gelu_parts(ys_ref[...])
        gate = jax.nn.sigmoid(_dot(yg.astype(BF16), gw_ref[...]) + gb_ref[...])
        zp = _dot(pm_ref[...], z_ref[...])
        silu_z, _ = _silu_parts(zp)
        y_ref[...] = _dot(pmt_ref[...], (yg * gate * silu_z).astype(BF16)).astype(BF16)

        @pl.when(pl.program_id(0) == nblk - 1)
        def _():
            _hosted_copies(items, src_refs, out_refs, *sems, act="wait")

    return _pcall(body, name="ssm_fwd", grid=(nblk,),
                  out_shape=(jax.ShapeDtypeStruct((rows, D), BF16), jax.ShapeDtypeStruct((rows, D), F32),
                             jax.ShapeDtypeStruct((nblk, 1, N_STATE), F32),
                             jax.ShapeDtypeStruct((rows, N_STATE), F32),
                             *[jax.ShapeDtypeStruct((D, D), BF16)] * n_sh),
                  in_specs=[pl.BlockSpec((tb, D), lambda i: (i, 2)), pl.BlockSpec((tb, D), lambda i: (i, 3)),
                            _full((tb, tb)), _full((tb, tb)),
                            _full((N_Q, 256, Q_W), single=True), _full((N_Q, 256, Q_W), single=True),
                            _full((k_steps, N_STATE)), _full((1, D)), _full((D, D), single=True), _full((1, D))] +
                           [ANY] * n_sh,
                  out_specs=(pl.BlockSpec((tb, D), lambda i: (i, 0)), pl.BlockSpec((tb, D), lambda i: (i, 0)),
                             pl.BlockSpec((None, 1, N_STATE), lambda i: (i, 0, 0)),
                             pl.BlockSpec((tb, N_STATE), lambda i: (i, 0)), *([ANY] * n_sh)),
                  scratch_shapes=[pltpu.VMEM((1, N_STATE), F32),
                                  pltpu.VMEM((SUBLANES, N_STATE), F32), pltpu.VMEM((SUBLANES, N_STATE), F32)] +
                                 _sem_scratch(items),
                  compiler_params=_params(("arbitrary",)))(proj, proj, pm, pmt, wb, wct, ptab, dvec, glu_w, glu_b,
                                                           *shards)


def _head(x, target, proj, y_pool, y_ssm, mod3, norm_post, wbp, wbs, wout):
    rows = x.shape[0]
    tb = _tb(rows, 256)
    nblk = rows // tb
    n_feat = float(D)

    def body(x_ref, t_ref, gp_ref, gs_ref, yp_ref, ys_ref, mod_ref, npost_ref, wbp_ref, wbs_ref, wout_ref,
             loss_ref, dy_ref, dyp_ref, dys_ref, dg_ref, dwbp_hbm, dwbs_hbm, dwout_hbm, vec_ref,
             acc_bp, acc_bs, acc_out, acc_loss, acc_vec):
        i = pl.program_id(0)

        @pl.when(i == 0)
        def _():
            acc_bp[...] = jnp.zeros_like(acc_bp)
            acc_bs[...] = jnp.zeros_like(acc_bs)
            acc_out[...] = jnp.zeros_like(acc_out)
            acc_loss[...] = jnp.zeros_like(acc_loss)
            acc_vec[...] = jnp.zeros_like(acc_vec)

        yp, ys = yp_ref[...], ys_ref[...]
        sgp = jax.nn.sigmoid(gp_ref[...].astype(F32))
        sgs = jax.nn.sigmoid(gs_ref[...].astype(F32))
        pb = _dot(yp, wbp_ref[...])
        psm = _dot(ys, wbs_ref[...])
        mb = (sgp * pb + sgs * psm).astype(BF16)
        out = _dot(mb, wout_ref[...])
        on, r = _rms_parts(out)
        gate = mod_ref[2:3, :]
        npost = npost_ref[...]
        normed = on * npost
        diff = x_ref[...] + gate * normed - t_ref[...]
        acc_loss[...] += jnp.sum(diff * diff, axis=0, keepdims=True)
        dy = diff * (1.0 / n_feat)
        dy_ref[...] = dy
        acc_vec[0:1, :] += jnp.sum(dy * normed, axis=0, keepdims=True)
        dn = dy * gate
        acc_vec[1:2, :] += jnp.sum(dn * on, axis=0, keepdims=True)
        dout = _rms_bwd(dn * npost, on, r).astype(BF16)
        acc_out[...] += _dot_tn(mb, dout)
        dm = _dot_nt(dout, wout_ref[...])
        dpb = (dm * sgp).astype(BF16)
        dps = (dm * sgs).astype(BF16)
        dg_ref[:, :D] = (dm * pb * sgp * (1.0 - sgp)).astype(BF16)
        dg_ref[:, D:] = (dm * psm * sgs * (1.0 - sgs)).astype(BF16)
        acc_bp[...] += _dot_tn(yp, dpb)
        acc_bs[...] += _dot_tn(ys, dps)
        dyp_ref[...] = _dot_nt(dpb, wbp_ref[...]).astype(BF16)
        dys_ref[...] = _dot_nt(dps, wbs_ref[...]).astype(BF16)

        @pl.when(i == nblk - 1)
        def _():
            loss_ref[...] = 0.5 / n_feat * jnp.sum(acc_loss[...], axis=1, keepdims=True)
            vec_ref[...] = acc_vec[...]
            pltpu.sync_copy(acc_bp, dwbp_hbm)
            pltpu.sync_copy(acc_bs, dwbs_hbm)
            pltpu.sync_copy(acc_out, dwout_hbm)

    row = lambda c: pl.BlockSpec((tb, D), lambda i: (i, c))
    w = _full((D, D), single=True)
    return _pcall(body, name="head", grid=(nblk,),
                  out_shape=(jax.ShapeDtypeStruct((1, 1), F32), jax.ShapeDtypeStruct((rows, D), F32),
                             jax.ShapeDtypeStruct((rows, D), BF16), jax.ShapeDtypeStruct((rows, D), BF16),
                             jax.ShapeDtypeStruct((rows, 2 * D), BF16),
                             jax.ShapeDtypeStruct((D, D), F32), jax.ShapeDtypeStruct((D, D), F32),
                             jax.ShapeDtypeStruct((D, D), F32), jax.ShapeDtypeStruct((2, D), F32)),
                  in_specs=[row(0), row(0), row(4), row(5), row(0), row(0), _full((3, D)), _full((1, D)), w, w, w],
                  out_specs=(_full((1, 1)), row(0), row(0), row(0), pl.BlockSpec((tb, 2 * D), lambda i: (i, 0)),
                             ANY, ANY, ANY, _full((2, D))),
                  scratch_shapes=[pltpu.VMEM((D, D), F32), pltpu.VMEM((D, D), F32), pltpu.VMEM((D, D), F32),
                                  pltpu.VMEM((1, D), F32), pltpu.VMEM((2, D), F32)],
                  compiler_params=_params(("arbitrary",)))(x, target, proj, proj, y_pool, y_ssm, mod3, norm_post,
                                                           wbp, wbs, wout)


def _glu_bwd(dys, proj, ys_pre, pm, pmt, glu_w, glu_b):
    rows = dys.shape[0]
    tb = pm.shape[0]
    nblk = rows // tb

    def body(dys_ref, z_ref, ysp_ref, pm_ref, pmt_ref, gw_ref, gb_ref, dyp_ref, dz_ref, dgw_hbm, dgb_ref,
             acc_w, acc_b):
        i = pl.program_id(0)

        @pl.when(i == 0)
        def _():
            acc_w[...] = jnp.zeros_like(acc_w)
            acc_b[...] = jnp.zeros_like(acc_b)

        d_out = _dot(pm_ref[...], dys_ref[...])
        z = _dot(pm_ref[...], z_ref[...])
        yg, dgelu = _gelu_parts(ysp_ref[...])
        ygb = yg.astype(BF16)
        sg = jax.nn.sigmoid(_dot(ygb, gw_ref[...]) + gb_ref[...])
        silu_z, dsilu_z = _silu_parts(z)
        dz = d_out * (yg * sg) * dsilu_z
        dz_ref[...] = _dot(pmt_ref[...], dz.astype(BF16)).astype(BF16)
        dglu = d_out * silu_z
        dq = dglu * yg * sg * (1.0 - sg)
        dqb = dq.astype(BF16)
        acc_b[...] += jnp.sum(dq, axis=0, keepdims=True)
        acc_w[...] += _dot_tn(ygb, dqb)
        dyg = dglu * sg + _dot_nt(dqb, gw_ref[...])
        dyp_ref[...] = (dyg * dgelu).astype(BF16)

        @pl.when(i == nblk - 1)
        def _():
            dgb_ref[...] = acc_b[...]
            pltpu.sync_copy(acc_w, dgw_hbm)

    row = lambda c: pl.BlockSpec((tb, D), lambda i: (i, c))
    return _pcall(body, name="glu_bwd", grid=(nblk,),
                  out_shape=(jax.ShapeDtypeStruct((rows, D), BF16), jax.ShapeDtypeStruct((rows, D), BF16),
                             jax.ShapeDtypeStruct((D, D), F32), jax.ShapeDtypeStruct((1, D), F32)),
                  in_specs=[row(0), row(3), row(0), _full((tb, tb)), _full((tb, tb)),
                            _full((D, D), single=True), _full((1, D))],
                  out_specs=(row(0), row(0), ANY, _full((1, D))),
                  scratch_shapes=[pltpu.VMEM((D, D), F32), pltpu.VMEM((1, D), F32)],
                  compiler_params=_params(("arbitrary",)))(dys, proj, ys_pre, pm, pmt, glu_w, glu_b)


def _ssm_bwd(dyp, proj, states, carries, pm, pmt, wb, wct, ptab, dvec, mat_grads, dpool_w, dw_in_rest):
    rows = dyp.shape[0]
    tb = pm.shape[0]
    k_steps = tb // SUBLANES
    nblk = rows // tb
    n_mat = len(mat_grads)
    hosted = [*mat_grads, dpool_w, dw_in_rest]
    n_h = len(hosted)
    shard_rows = D // N_DEV
    pool_rows = dpool_w.shape[1] // N_DEV
    items = [_scatter_item(t, t, _rows_of(shard_rows)) for t in range(n_mat)] + \
            [_scatter_item(n_mat, n_mat, _pool_rows_of(pool_rows))] + \
            [_w_in_block_item(n_mat + 1, n_mat + 1, j, ssm_part=False) for j in range(W_IN_SHARD // W_IN_BLOCK)]
    n_in, n_out = 10, 5

    def body(*refs):
        dyp_ref, u_ref, s_ref, cin_ref, pm_ref, pmt_ref, wb_ref, wct_ref, p_ref, d_ref = refs[:n_in]
        src_refs = refs[n_in:n_in + n_h]
        du_ref, dbb_ref, dcc_ref, da_ref, dd_ref = refs[n_in + n_h:n_in + n_h + n_out]
        recv_refs = refs[n_in + n_h + n_out:n_in + 2 * n_h + n_out]
        (g_ref, carry_b, fin_ref, acc_wb, acc_wct, acc_da, acc_dd, dup_ref,
         *sems) = refs[n_in + 2 * n_h + n_out:]
        i = pl.program_id(0)

        @pl.when(i == 0)
        def _():
            _hosted_copies(items, src_refs, recv_refs, *sems, act="start")
            carry_b[...] = jnp.zeros_like(carry_b)
            acc_wb[...] = jnp.zeros_like(acc_wb)
            acc_wct[...] = jnp.zeros_like(acc_wct)
            acc_da[...] = jnp.zeros_like(acc_da)
            acc_dd[...] = jnp.zeros_like(acc_dd)

        dy = dyp_ref[...]
        up = _dot(pm_ref[...], u_ref[...]).astype(BF16)
        acc_dd[...] += jnp.sum(dy.astype(F32) * up.astype(F32), axis=0, keepdims=True)
        for q in range(N_Q):
            cols = slice(q * 256, (q + 1) * 256)
            g_ref[:, q * Q_W:(q + 1) * Q_W] = _dot(dy[:, cols], wct_ref[q])
            acc_wct[q] += _dot_tn(dy[:, cols], s_ref[:, q * Q_W:(q + 1) * Q_W].astype(BF16))
        _scan_backward(g_ref, s_ref, p_ref, carry_b, cin_ref, fin_ref, acc_da, k_steps)
        for q in range(N_Q):
            cols = slice(q * 256, (q + 1) * 256)
            lam = g_ref[:, q * Q_W:(q + 1) * Q_W].astype(BF16)
            acc_wb[q] += _dot_tn(up[:, cols], lam)
            dup_ref[:, cols] = (_dot_nt(lam, wb_ref[q]) + d_ref[:, cols] * dy[:, cols].astype(F32)).astype(BF16)
        du_ref[...] = _dot(pmt_ref[...], dup_ref[...]).astype(BF16)

        @pl.when(i == nblk - 1)
        def _():
            da_ref[...] = acc_da[...]
            dd_ref[...] = acc_dd[...]
            spread, own = _group_masks()
            spread = spread.astype(F32)
            for acc, out in ((acc_wb, dbb_ref), (acc_wct, dcc_ref)):
                for half in range(2):
                    for q in range(N_Q):
                        kept = jnp.where(own, acc[q, :, half * (Q_W // 2):(half + 1) * (Q_W // 2)], 0.0)
                        out[half, q] = lax.dot_general(kept, spread, (((1,), (1,)), ((), ())),
                                                       preferred_element_type=F32, precision=lax.Precision.HIGHEST)
            _hosted_copies(items, src_refs, recv_refs, *sems, act="wait")

    rev = lambda c: pl.BlockSpec((tb, D), lambda i: (nblk - 1 - i, c))
    recv = [jax.ShapeDtypeStruct((N_DEV, shard_rows, D), F32)] * n_mat + \
           [jax.ShapeDtypeStruct((N_DEV, dpool_w.shape[0], pool_rows, dpool_w.shape[2]), F32),
            jax.ShapeDtypeStruct((N_DEV, D, W_IN_SHARD), BF16)]
    return _pcall(body, name="ssm_bwd", grid=(nblk,),
                  out_shape=(jax.ShapeDtypeStruct((rows, D), BF16),
                             jax.ShapeDtypeStruct((2, N_Q, 16 * G_H, G_P), F32),
                             jax.ShapeDtypeStruct((2, N_Q, 16 * G_H, G_P), F32),
                             jax.ShapeDtypeStruct((1, N_STATE), F32), jax.ShapeDtypeStruct((1, D), F32), *recv),
                  in_specs=[rev(0), rev(2), pl.BlockSpec((tb, N_STATE), lambda i: (nblk - 1 - i, 0)),
                            pl.BlockSpec((None, 1, N_STATE), lambda i: (nblk - 1 - i, 0, 0)),
                            _full((tb, tb)), _full((tb, tb)),
                            _full((N_Q, 256, Q_W), single=True), _full((N_Q, 256, Q_W), single=True),
                            _full((k_steps, N_STATE)), _full((1, D))] + [ANY] * n_h,
                  out_specs=(rev(0), _full((2, N_Q, 16 * G_H, G_P)), _full((2, N_Q, 16 * G_H, G_P)),
                             _full((1, N_STATE)), _full((1, D)), *([ANY] * n_h)),
                  scratch_shapes=[pltpu.VMEM((tb, N_STATE), F32), pltpu.VMEM((1, N_STATE), F32),
                                  pltpu.VMEM((SUBLANES, N_STATE), F32),
                                  pltpu.VMEM((N_Q, 256, Q_W), F32), pltpu.VMEM((N_Q, 256, Q_W), F32),
                                  pltpu.VMEM((1, N_STATE), F32), pltpu.VMEM((1, D), F32),
                                  pltpu.VMEM((tb, D), BF16)] + _sem_scratch(items),
                  compiler_params=_params(("arbitrary",), vmem=60 * 1024 * 1024),
                  )(dyp, proj, states, carries, pm, pmt, wb, wct, ptab, dvec, *hosted)


def _pool_bwd(dyp, proj, pool_w, pool_scale):
    rows = dyp.shape[0]
    tb = _tb(rows, 512)
    nblk = rows // tb
    hb = tb // HALO

    def body(dy_ref, u_ref, halo_ref, z_ref, pw_ref, ps_ref, dp_ref, dpw_ref, dps_ref, ahead_ref):
        i = pl.program_id(0)
        blk = nblk - 1 - i

        @pl.when(i == 0)
        def _():
            ahead_ref[...] = jnp.zeros_like(ahead_ref)
            dpw_ref[...] = jnp.zeros_like(dpw_ref)
            dps_ref[...] = jnp.zeros_like(dps_ref)

        u = u_ref[...].astype(F32)
        halo = jnp.where(blk > 0, halo_ref[...].astype(F32), 0.0)
        pooled, counts = _pool_windows(jnp.concatenate([halo, u], axis=0), tb, blk * tb)
        silu_z, dsilu_z = _silu_parts(z_ref[...].astype(F32))
        dy = dy_ref[...].astype(F32)
        for g, w in enumerate(POOL_WINDOWS):
            cols = slice(g * 256, (g + 1) * 256)
            pooled_b = pooled[g].astype(BF16)
            mixed = _dot(pooled_b, pw_ref[g])
            scale = ps_ref[:, cols]
            dp_ref[:, D + g * 256:D + (g + 1) * 256] = (dy[:, cols] * (mixed * scale) * dsilu_z[:, cols]).astype(BF16)
            dms = dy[:, cols] * silu_z[:, cols]
            dps_ref[:, cols] += jnp.sum(dms * mixed, axis=0, keepdims=True)
            dmixed = (dms * scale).astype(BF16)
            dpw_ref[g] += _dot_tn(pooled_b, dmixed)
            dpooled = _dot_nt(dmixed, pw_ref[g])
            ratio = dpooled / counts[g]
            acc = jnp.concatenate([ratio, ahead_ref[:, cols]], axis=0)
            ahead_ref[:, cols] = ratio[:HALO, :]
            s = 1
            while s < w:
                acc = acc + pltpu.roll(acc, tb + HALO - s, axis=0)
                s *= 2
            dp_ref[:, cols] = (acc[:tb, :] - dpooled).astype(BF16)

    rev = lambda c: pl.BlockSpec((tb, D), lambda i: (nblk - 1 - i, c))
    return _pcall(body, name="pool_bwd", grid=(nblk,),
                  out_shape=(jax.ShapeDtypeStruct((rows, 2 * D), BF16), jax.ShapeDtypeStruct((4, 256, 256), F32),
                             jax.ShapeDtypeStruct((1, D), F32)),
                  in_specs=[rev(0), rev(0),
                            pl.BlockSpec((HALO, D), lambda i: (jnp.maximum((nblk - 1 - i) * hb - 1, 0), 0)),
                            rev(1), _full((4, 256, 256)), _full((1, D))],
                  out_specs=(pl.BlockSpec((tb, 2 * D), lambda i: (nblk - 1 - i, 0)), _full((4, 256, 256)),
                             _full((1, D))),
                  scratch_shapes=[pltpu.VMEM((HALO, D), F32)],
                  compiler_params=_params(("arbitrary",)))(dyp, proj, proj, proj, pool_w, pool_scale)


def _dproj_specs(tb):
    return [pl.BlockSpec((tb, 2 * D), lambda i: (i, 0)), pl.BlockSpec((tb, D), lambda i: (i, 0)),
            pl.BlockSpec((tb, D), lambda i: (i, 0)), pl.BlockSpec((tb, 2 * D), lambda i: (i, 0))]


def _in_proj_bwd_x(x, dy, dpp, dus, dzs, dpg, mod3, norm_pre, w_in, dw_in_ssm, small32, small16, recv_w_in):
    rows = x.shape[0]
    tb = _tb(rows, 256)
    nblk = rows // tb
    items = [_w_in_block_item(0, 0, j, ssm_part=True) for j in range(W_IN_SHARD // W_IN_BLOCK)] + \
            [_Item(1, 1, _whole, _slot), _Item(2, 2, _whole, _slot)]

    def body(x_ref, dy_ref, dpp_ref, dus_ref, dzs_ref, dpg_ref, mod_ref, np_ref, w_ref,
             dw_src, s32_src, s16_src, _, gx_ref, vec_ref, recv_w, recv32, recv16, *sems):
        src_refs, recv_refs = (dw_src, s32_src, s16_src), (recv_w, recv32, recv16)

        @pl.when(pl.program_id(0) == 0)
        def _():
            _hosted_copies(items, src_refs, recv_refs, *sems, act="start")
            vec_ref[...] = jnp.zeros_like(vec_ref)

        dh = _dot_nt(dpp_ref[...], w_ref[:, 0:2 * D])
        dh += _dot_nt(dus_ref[...], w_ref[:, 2 * D:3 * D])
        dh += _dot_nt(dzs_ref[...], w_ref[:, 3 * D:4 * D])
        dh += _dot_nt(dpg_ref[...], w_ref[:, 4 * D:6 * D])
        xn, r, _ = _prenorm(x_ref[...], mod_ref[...], np_ref[...])
        one_scale = 1.0 + mod_ref[1:2, :]
        vec_ref[0:1, :] += jnp.sum(dh, axis=0, keepdims=True)
        vec_ref[1:2, :] += jnp.sum(dh * xn, axis=0, keepdims=True) * np_ref[...]
        vec_ref[2:3, :] += jnp.sum(dh * xn, axis=0, keepdims=True) * one_scale
        gx_ref[...] = dy_ref[...] + _rms_bwd(dh * (np_ref[...] * one_scale), xn, r)

        @pl.when(pl.program_id(0) == nblk - 1)
        def _():
            _hosted_copies(items, src_refs, recv_refs, *sems, act="wait")

    row = pl.BlockSpec((tb, D), lambda i: (i, 0))
    recv = (jax.ShapeDtypeStruct(recv_w_in.shape, recv_w_in.dtype),
            jax.ShapeDtypeStruct((N_DEV,) + small32.shape, small32.dtype),
            jax.ShapeDtypeStruct((N_DEV,) + small16.shape, small16.dtype))
    return _pcall(body, name="in_proj_bwd_x", grid=(nblk,),
                  out_shape=(jax.ShapeDtypeStruct((rows, D), F32), jax.ShapeDtypeStruct((3, D), F32), *recv),
                  in_specs=[row, row] + _dproj_specs(tb) + [_full((3, D)), _full((1, D)),
                                                            _full((D, N_IN), single=True)] + [ANY] * 4,
                  out_specs=(row, _full((3, D)), ANY, ANY, ANY),
                  input_output_aliases={12: 2},
                  scratch_shapes=_sem_scratch(items),
                  compiler_params=_params(("arbitrary",)))(x, dy, dpp, dus, dzs, dpg, mod3, norm_pre, w_in,
                                                           dw_in_ssm, small32, small16, recv_w_in)


def _in_proj_bwd_w(name, x, dparts, mod3, norm_pre):
    rows = x.shape[0]
    tb = _tb(rows, 256)
    nblk = rows // tb
    widths = [p.shape[1] for p in dparts]
    n_p = len(dparts)

    def body(x_ref, *rest):
        part_refs, (mod_ref, np_ref, dw_ref, acc) = rest[:n_p], rest[n_p:]
        i = pl.program_id(0)

        @pl.when(i == 0)
        def _():
            acc[...] = jnp.zeros_like(acc)

        _, _, h = _prenorm(x_ref[...], mod_ref[...], np_ref[...])
        ht = h.astype(BF16)
        lo = 0
        for ref, w in zip(part_refs, widths):
            acc[:, lo:lo + w] += _dot_tn(ht, ref[...])
            lo += w

        @pl.when(i == nblk - 1)
        def _():
            dw_ref[...] = acc[...].astype(BF16)

    row = pl.BlockSpec((tb, D), lambda i: (i, 0))
    return _pcall(body, name=name, grid=(nblk,),
                  out_shape=jax.ShapeDtypeStruct((D, sum(widths)), BF16),
                  in_specs=[row] + [pl.BlockSpec((tb, w), lambda i: (i, 0)) for w in widths] +
                           [_full((3, D)), _full((1, D))],
                  out_specs=_full((D, sum(widths))),
                  scratch_shapes=[pltpu.VMEM((D, sum(widths)), F32)],
                  compiler_params=_params(("arbitrary",)))(x, *dparts, mod3, norm_pre)


def _adamw_math(w, g, m, v):
    m = ADAM_B1 * m + (1.0 - ADAM_B1) * g
    v = ADAM_B2 * v + (1.0 - ADAM_B2) * (g * g)
    m_hat = m / (1.0 - ADAM_B1 ** ADAM_STEP)
    v_hat = v / (1.0 - ADAM_B2 ** ADAM_STEP)
    delta = -ADAM_LR * (m_hat / (jnp.sqrt(v_hat) + ADAM_EPS) + ADAM_WD * w)
    return delta, m, v


def _sum_sources(ref):
    g = ref[0].astype(F32)
    for s in range(1, N_DEV):
        g = g + ref[s].astype(F32)
    return g


def _adamw_reduce(name, parts, w, m, v):
    r, c = w.shape
    tr = r if r * c <= 256 * 1024 else max(8, (256 * 1024 // c) // 8 * 8)
    while r % tr:
        tr -= 8

    def body(p_ref, w_ref, m_ref, v_ref, g_ref, d_ref, nm_ref, nv_ref):
        g = _sum_sources(p_ref)
        g_ref[...] = g
        d_ref[...], nm_ref[...], nv_ref[...] = _adamw_math(w_ref[...], g, m_ref[...], v_ref[...])

    blk = pl.BlockSpec((tr, c), lambda i: (i, 0))
    return _pcall(body, name=name, grid=(r // tr,),
                  out_shape=tuple([jax.ShapeDtypeStruct((r, c), F32)] * 4),
                  in_specs=[pl.BlockSpec((N_DEV, tr, c), lambda i: (0, i, 0)), blk, blk, blk],
                  out_specs=(blk, blk, blk, blk),
                  compiler_params=_params(("arbitrary",)))(parts, w, m, v)


def _adamw_plain(name, g, w, m, v):
    def body(g_ref, w_ref, m_ref, v_ref, d_ref, nm_ref, nv_ref):
        d_ref[...], nm_ref[...], nv_ref[...] = _adamw_math(w_ref[...], g_ref[...], m_ref[...], v_ref[...])

    vm = pl.BlockSpec(memory_space=pltpu.VMEM)
    return _pcall(body, name=name, out_shape=tuple([jax.ShapeDtypeStruct(w.shape, F32)] * 3),
                  in_specs=[vm] * 4, out_specs=(vm, vm, vm), compiler_params=_params())(g, w, m, v)


def _sum_small(parts):
    n = len(parts)

    def body(*refs):
        for t in range(n):
            refs[n + t][...] = _sum_sources(refs[t])

    vm = pl.BlockSpec(memory_space=pltpu.VMEM)
    return _pcall(body, name="sum_small",
                  out_shape=tuple(jax.ShapeDtypeStruct(p.shape[1:], F32) for p in parts),
                  in_specs=[vm] * n, out_specs=tuple([vm] * n), compiler_params=_params())(*parts)


def _ada_update(c_all, dmod_cols, w, m, v):
    def body(c_ref, dm_ref, w_ref, m_ref, v_ref, g_ref, d_ref, nm_ref, nv_ref):
        ca = c_ref[...]
        g = lax.dot_general(ca * jax.nn.sigmoid(ca), dm_ref[...], (((0,), (0,)), ((), ())),
                            preferred_element_type=F32, precision=lax.Precision.HIGHEST)
        g_ref[...] = g
        d_ref[...], nm_ref[...], nv_ref[...] = _adamw_math(w_ref[...], g, m_ref[...], v_ref[...])

    vm = pl.BlockSpec(memory_space=pltpu.VMEM)
    return _pcall(body, name="ada_update", out_shape=tuple([jax.ShapeDtypeStruct(w.shape, F32)] * 4),
                  in_specs=[vm] * 5, out_specs=(vm, vm, vm, vm), compiler_params=_params())(c_all, dmod_cols, w, m, v)


def kernel(x, c, w_ada, b_ada, norm_pre, norm_post, w_in, pool_w, pool_scale, ssm_a_re, ssm_a_im, ssm_log_dt, ssm_b_re, ssm_b_im, ssm_c_re, ssm_c_im, ssm_d, glu_w, glu_b, w_branch_pool, w_branch_ssm, w_out, loss_target, m_w_ada, m_b_ada, m_norm_pre, m_norm_post, m_w_in, m_pool_w, m_pool_scale, m_ssm_a_re, m_ssm_a_im, m_ssm_log_dt, m_ssm_b_re, m_ssm_b_im, m_ssm_c_re, m_ssm_c_im, m_ssm_d, m_glu_w, m_glu_b, m_w_branch_pool, m_w_branch_ssm, m_w_out, v_w_ada, v_b_ada, v_norm_pre, v_norm_post, v_w_in, v_pool_w, v_pool_scale, v_ssm_a_re, v_ssm_a_im, v_ssm_log_dt, v_ssm_b_re, v_ssm_b_im, v_ssm_c_re, v_ssm_c_im, v_ssm_d, v_glu_w, v_glu_b, v_w_branch_pool, v_w_branch_ssm, v_w_out):
    given = dict(locals())
    me = _flat(_me())
    rows = x.shape[1]
    x2 = x[0]
    target = loss_target[0]
    ada_cols = w_ada.shape[2]

    b_ada_s = lax.dynamic_slice(b_ada, (0, me * ada_cols), (1, ada_cols))
    c_all, mod_rows = _ada_exchange(c, w_ada[0], b_ada_s)
    mod3 = mod_rows.reshape(3, D)

    shards = _cast_shards([w_in[0], pool_w[0], glu_w[0], w_branch_pool[0], w_branch_ssm[0], w_out[0]])

    tb_ssm = _tb(rows, 256)
    k_steps = tb_ssm // SUBLANES
    a_re, a_im = ssm_a_re[0], ssm_a_im[0]
    log_dt = ssm_log_dt[0].reshape(GROUPS, 1)
    b_re_t, b_im_t = ssm_b_re[0].transpose(0, 2, 1), ssm_b_im[0].transpose(0, 2, 1)
    wb, wct, pow_re, pow_im = _s5_prep(a_re, a_im, log_dt, b_re_t, b_im_t, ssm_c_re[0], ssm_c_im[0], k_steps)
    ptab = _state_layout(pow_re, pow_im)
    dvec = ssm_d[0].reshape(1, D)
    pm = _perm_matrix(tb_ssm)
    pmt = pm.T

    proj, w_in_g, pool_w_g, glu_g = _in_proj(x2, mod3, norm_pre, shards[0], shards[1:3])
    y_pool = _pool_fwd(proj, pool_w_g, pool_scale)
    y_ssm, ys_pre, carries, states, wbp_g, wbs_g, wout_g = _ssm_fwd(
        proj, pm, pmt, wb, wct, ptab, dvec, glu_g, glu_b, shards[3:])
    loss_part, dy, dyp, dys, dpg, dwbp, dwbs, dwout, head_vec = _head(
        x2, target, proj, y_pool, y_ssm, mod3, norm_post, wbp_g, wbs_g, wout_g)

    dpp, dpool_w, dpool_scale = _pool_bwd(dyp, proj, pool_w_g, pool_scale)
    dw_in_rest = _in_proj_bwd_w("in_proj_bwd_w_rest", x2, [dpp, dpg], mod3, norm_pre)
    dy_pre, dzs, dglu_w, dglu_b = _glu_bwd(dys, proj, ys_pre, pm, pmt, glu_g, glu_b)
    dus, dbb, dcc, dabar, dd, p_glu, p_wbp, p_wbs, p_wout, p_pool_w, p_w_in = _ssm_bwd(
        dy_pre, proj, states, carries, pm, pmt, wb, wct, ptab, dvec, [dglu_w, dwbp, dwbs, dwout], dpool_w, dw_in_rest)
    dw_in_ssm = _in_proj_bwd_w("in_proj_bwd_w_ssm", x2, [dus, dzs], mod3, norm_pre)

    small32 = jnp.concatenate([head_vec, dpool_scale, dglu_b, dd, jnp.broadcast_to(loss_part, (1, D)),
                               jnp.zeros((2, D), F32), dabar.reshape(8, D)], axis=0)
    small16 = jnp.concatenate([dbb.reshape(2 * GROUPS, D), dcc.reshape(2 * GROUPS, D)], axis=0).astype(BF16)
    grad_x, pre_vec, p_w_in, p_small32, p_small16 = _in_proj_bwd_x(
        x2, dy, dpp, dus, dzs, dpg, mod3, norm_pre, w_in_g, dw_in_ssm, small32, small16, p_w_in)
    small_pre = jnp.concatenate([pre_vec, jnp.zeros((5, D), F32)], axis=0)
    (p_pre,) = _exchange("gather_prenorm_sums", [small_pre], [jax.ShapeDtypeStruct((N_DEV, 8, D), F32)],
                         [_Item(0, 0, _whole, _slot)])

    tot32, tot16, tot_pre = _sum_small([p_small32, p_small16, p_pre])
    d_abar_re, d_abar_im = _state_unlayout(tot32[8:16].reshape(N_STATE))
    d_bb_re, d_bb_im = tot16[0:64].reshape(GROUPS, G_H, G_P), tot16[64:128].reshape(GROUPS, G_H, G_P)
    g_a_re, g_a_im, g_log_dt, g_b_re_t, g_b_im_t = _s5_prep_bwd(
        a_re, a_im, log_dt, b_re_t, b_im_t, d_abar_re, d_abar_im, d_bb_re, d_bb_im)

    grads, deltas, new_m, new_v = {}, {}, {}, {}

    def small_update(name, g2):
        shape = given[name].shape
        w2, m2, v2 = (given[p + name].reshape(g2.shape) for p in ("", "m_", "v_"))
        d2, nm2, nv2 = _adamw_plain("adamw_" + name, g2, w2, m2, v2)
        grads[name], deltas[name], new_m[name], new_v[name] = (a.reshape(shape) for a in (g2, d2, nm2, nv2))

    def shard_update(name, parts):
        shape = given[name].shape
        r2 = parts.shape[1:] if parts.ndim == 3 else (parts.shape[1] * parts.shape[2], parts.shape[3])
        w2, m2, v2 = (given[p + name].reshape(r2) for p in ("", "m_", "v_"))
        out = _adamw_reduce("adamw_" + name, parts.reshape((N_DEV,) + tuple(r2)), w2, m2, v2)
        grads[name], deltas[name], new_m[name], new_v[name] = (a.reshape(shape) for a in out)

    dmod_all = jnp.concatenate([p_pre[:, 0:2, :], p_small32[:, 0:1, :]], axis=1).reshape(N_DEV, 3 * D)
    dmod_cols = lax.dynamic_slice(dmod_all, (0, me * ada_cols), (N_DEV, ada_cols))
    out = _ada_update(c_all, dmod_cols, w_ada[0], m_w_ada[0], v_w_ada[0])
    grads['w_ada'], deltas['w_ada'], new_m['w_ada'], new_v['w_ada'] = (a.reshape(w_ada.shape) for a in out)

    small_update('b_ada', jnp.concatenate([tot_pre[0:2], tot32[0:1]], axis=0).reshape(1, 3 * D))
    small_update('norm_pre', tot_pre[2:3])
    small_update('norm_post', tot32[1:2])
    small_update('pool_scale', tot32[2:3])
    small_update('glu_b', tot32[3:4])
    small_update('ssm_d', tot32[4:5])
    small_update('ssm_a_re', g_a_re)
    small_update('ssm_a_im', g_a_im)
    small_update('ssm_log_dt', g_log_dt.reshape(1, GROUPS))
    small_update('ssm_b_re', g_b_re_t.transpose(0, 2, 1).reshape(GROUPS, G_P * G_H))
    small_update('ssm_b_im', g_b_im_t.transpose(0, 2, 1).reshape(GROUPS, G_P * G_H))
    small_update('ssm_c_re', tot16[128:192])
    small_update('ssm_c_im', -tot16[192:256])
    shard_update('w_in', p_w_in)
    shard_update('pool_w', p_pool_w)
    shard_update('glu_w', p_glu)
    shard_update('w_branch_pool', p_wbp)
    shard_update('w_branch_ssm', p_wbs)
    shard_update('w_out', p_wout)

    return (tot32[5, 0], grad_x[None], *[grads[n] for n in WEIGHTS], *[deltas[n] for n in WEIGHTS],
            *[new_m[n] for n in WEIGHTS], *[new_v[n] for n in WEIGHTS])
```

```python
import functools
import math
from typing import Callable, NamedTuple, Optional

import jax
import jax.numpy as jnp
from jax import lax
from jax.experimental import pallas as pl
from jax.experimental.pallas import tpu as pltpu

F32 = jnp.float32
BF16 = jnp.bfloat16
MESH = pl.DeviceIdType.MESH

D = 1024
N_DEV = 8
N_IN = 6 * D
GROUPS = 64
G_H = 16
G_P = 64
N_Q = 4
Q_W = 2 * 16 * G_P
N_STATE = N_Q * Q_W
POOL_WINDOWS = (2, 4, 8, 16)
HALO = 16
RMS_EPS = 1e-6
SUBLANES = 8
LANE_CHUNK = 512
SCAN_UNROLL = 2
VMEM_LIMIT = 56 * 1024 * 1024

ADAM_LR = 0.001
ADAM_B1 = 0.9
ADAM_B2 = 0.999
ADAM_EPS = 1e-08
ADAM_WD = 0.01
ADAM_STEP = 10

WEIGHTS = ['w_ada', 'b_ada', 'norm_pre', 'norm_post', 'w_in', 'pool_w', 'pool_scale', 'ssm_a_re',
           'ssm_a_im', 'ssm_log_dt', 'ssm_b_re', 'ssm_b_im', 'ssm_c_re', 'ssm_c_im', 'ssm_d', 'glu_w',
           'glu_b', 'w_branch_pool', 'w_branch_ssm', 'w_out']


def _pcall(body, **kw):
    return pl.pallas_call(body, **kw)


def _params(sem=None, vmem=VMEM_LIMIT):
    return pltpu.CompilerParams(dimension_semantics=sem, vmem_limit_bytes=vmem)


def _tb(rows, pref):
    return pref if rows % pref == 0 and rows // pref >= 2 else rows // 2


def _full(shape, single=False):
    nd = len(shape)
    if single:
        return pl.BlockSpec(shape, lambda i: (0,) * nd, pipeline_mode=pl.Buffered(1))
    return pl.BlockSpec(shape, lambda i: (0,) * nd)


ANY = pl.BlockSpec(memory_space=pl.ANY)


def _me():
    return lax.axis_index("x"), lax.axis_index("y"), lax.axis_index("c")


def _flat(p):
    return 4 * p[0] + 2 * p[1] + p[2]


def _peer(k):
    x, y, c = _me()
    return (1 - x if k & 4 else x, 1 - y if k & 2 else y, 1 - c if k & 1 else c)


def _silu_parts(z):
    s = jax.nn.sigmoid(z)
    return z * s, s * (1.0 + z * (1.0 - s))


_GELU_C = math.sqrt(2.0 / math.pi)


def _gelu_parts(x):
    x2 = x * x
    t = jnp.tanh(_GELU_C * (x + 0.044715 * x * x2))
    g = 0.5 * x * (1.0 + t)
    dg = 0.5 * (1.0 + t) + 0.5 * x * (1.0 - t * t) * (_GELU_C * (1.0 + 3.0 * 0.044715 * x2))
    return g, dg


def _dot(a, b):
    return jnp.dot(a, b, preferred_element_type=F32)


def _dot_nt(a, b):
    return lax.dot_general(a, b, (((1,), (1,)), ((), ())), preferred_element_type=F32)


def _dot_tn(a, b):
    return lax.dot_general(a, b, (((0,), (0,)), ((), ())), preferred_element_type=F32)


def _rms_parts(x):
    r = lax.rsqrt(jnp.mean(x * x, axis=-1, keepdims=True) + RMS_EPS)
    return x * r, r


def _rms_bwd(dxn, xn, r):
    return r * (dxn - xn * jnp.mean(dxn * xn, axis=-1, keepdims=True))


def _ada_exchange(c, w_ada_s, b_ada_s):
    cols = w_ada_s.shape[1]

    def body(c_ref, w_ref, b_ref, call_ref, mod_ref, part_ref, ssem, rsem, lsem):
        me3 = _me()
        me = _flat(me3)
        mine = pltpu.make_async_copy(c_ref, call_ref.at[pl.ds(me, 1), :], lsem.at[0])
        mine.start()
        sends = []
        for k in range(1, N_DEV):
            cp = pltpu.make_async_remote_copy(src_ref=c_ref, dst_ref=call_ref.at[pl.ds(me, 1), :],
                                              send_sem=ssem.at[k - 1], recv_sem=rsem.at[k - 1],
                                              device_id=_peer(k), device_id_type=MESH)
            cp.start()
            sends.append(cp)
        mine.wait()
        for k in range(1, N_DEV):
            p = _flat(_peer(k))
            pltpu.make_async_remote_copy(src_ref=c_ref, dst_ref=call_ref.at[pl.ds(p, 1), :],
                                         send_sem=ssem.at[k - 1], recv_sem=rsem.at[k - 1],
                                         device_id=_peer(k), device_id_type=MESH).wait_recv()
        for cp in sends:
            cp.wait_send()
        ca = call_ref[...]
        act = ca * jax.nn.sigmoid(ca)
        part_ref[...] = jnp.dot(act, w_ref[...], preferred_element_type=F32,
                                precision=lax.Precision.HIGHEST) + b_ref[...]
        own = pltpu.make_async_copy(part_ref.at[pl.ds(me, 1), :], mod_ref.at[pl.ds(me, 1), :], lsem.at[1])
        own.start()
        sends = []
        for k in range(1, N_DEV):
            p = _flat(_peer(k))
            s = N_DEV - 1 + k - 1
            cp = pltpu.make_async_remote_copy(src_ref=part_ref.at[pl.ds(p, 1), :],
                                              dst_ref=mod_ref.at[pl.ds(me, 1), :],
                                              send_sem=ssem.at[s], recv_sem=rsem.at[s],
                                              device_id=_peer(k), device_id_type=MESH)
            cp.start()
            sends.append(cp)
        own.wait()
        for k in range(1, N_DEV):
            p = _flat(_peer(k))
            s = N_DEV - 1 + k - 1
            pltpu.make_async_remote_copy(src_ref=part_ref.at[pl.ds(p, 1), :],
                                         dst_ref=mod_ref.at[pl.ds(p, 1), :],
                                         send_sem=ssem.at[s], recv_sem=rsem.at[s],
                                         device_id=_peer(k), device_id_type=MESH).wait_recv()
        for cp in sends:
            cp.wait_send()

    vm = pl.BlockSpec(memory_space=pltpu.VMEM)
    return _pcall(
        body, name="ada_exchange",
        out_shape=(jax.ShapeDtypeStruct((N_DEV, D), F32), jax.ShapeDtypeStruct((N_DEV, cols), F32)),
        in_specs=[vm, vm, vm], out_specs=(vm, vm),
        scratch_shapes=[pltpu.VMEM((N_DEV, cols), F32),
                        pltpu.SemaphoreType.DMA((2 * (N_DEV - 1),)),
                        pltpu.SemaphoreType.DMA((2 * (N_DEV - 1),)),
                        pltpu.SemaphoreType.DMA((2,))],
    )(c, w_ada_s, b_ada_s)


class _Item(NamedTuple):
    src: int
    out: int
    src_view: Callable
    dst_view: Callable
    pred: Optional[Callable] = None


def _when(pred, dest, fn):
    if pred is None:
        fn()
    else:
        pl.when(pred(dest))(fn)


def _n_sems(items):
    return len(items) * (N_DEV - 1)


def _hosted_copies(items, srcs, outs, ssem, rsem, lsem, act):
    me = _flat(_me())
    for t, it in enumerate(items):
        local = lambda t=t, it=it: pltpu.make_async_copy(
            it.src_view(srcs[it.src], me), it.dst_view(outs[it.out], me), lsem.at[t])
        if act == "start":
            _when(it.pred, me, lambda local=local: local().start())
        else:
            _when(it.pred, me, lambda local=local: local().wait())
    for k in range(1, N_DEV):
        p3 = _peer(k)
        p = _flat(p3)
        for t, it in enumerate(items):
            s = t * (N_DEV - 1) + k - 1
            send = lambda it=it, s=s, p=p, p3=p3: pltpu.make_async_remote_copy(
                src_ref=it.src_view(srcs[it.src], p), dst_ref=it.dst_view(outs[it.out], me),
                send_sem=ssem.at[s], recv_sem=rsem.at[s], device_id=p3, device_id_type=MESH)
            recv = lambda it=it, s=s, p=p, p3=p3: pltpu.make_async_remote_copy(
                src_ref=it.src_view(srcs[it.src], p), dst_ref=it.dst_view(outs[it.out], p),
                send_sem=ssem.at[s], recv_sem=rsem.at[s], device_id=p3, device_id_type=MESH)
            if act == "start":
                _when(it.pred, p, lambda send=send: send().start())
            else:
                _when(it.pred, me, lambda recv=recv: recv().wait_recv())
                _when(it.pred, p, lambda send=send: send().wait_send())


def _sem_scratch(items):
    return [pltpu.SemaphoreType.DMA((_n_sems(items),)), pltpu.SemaphoreType.DMA((_n_sems(items),)),
            pltpu.SemaphoreType.DMA((len(items),))]


def _exchange(name, srcs, out_structs, items):
    n_src, n_out = len(srcs), len(out_structs)

    def body(*refs):
        src_refs, out_refs = refs[:n_src], refs[n_src:n_src + n_out]
        sems = refs[n_src + n_out:]
        _hosted_copies(items, src_refs, out_refs, *sems, act="start")
        _hosted_copies(items, src_refs, out_refs, *sems, act="wait")

    return _pcall(body, name=name, out_shape=tuple(out_structs),
                  in_specs=[ANY] * n_src, out_specs=tuple([ANY] * n_out),
                  scratch_shapes=_sem_scratch(items))(*srcs)


def _whole(ref, dest):
    return ref


def _slot(ref, sender):
    return ref.at[sender]


def _rows_of(rows):
    return lambda ref, dev: ref.at[pl.ds(dev * rows, rows), :]


def _cols_of(cols):
    return lambda ref, dev: ref.at[:, pl.ds(dev * cols, cols)]


def _pool_rows_of(rows):
    return lambda ref, dev: ref.at[:, pl.ds(dev * rows, rows), :]


def _gather_item(src, out, dst_view):
    return _Item(src, out, _whole, dst_view)


def _scatter_item(src, out, src_view):
    return _Item(src, out, src_view, _slot)


W_IN_BLOCK = 256
W_IN_SHARD = N_IN // N_DEV
SSM_BLOCKS = (2 * D // W_IN_BLOCK, 4 * D // W_IN_BLOCK)


def _w_in_block_item(src, out, j, ssm_part):
    def block(dest):
        return (W_IN_SHARD // W_IN_BLOCK) * dest + j

    def in_ssm(dest):
        b = block(dest)
        return (b >= SSM_BLOCKS[0]) & (b < SSM_BLOCKS[1])

    def src_view(ref, dest):
        b = block(dest)
        local = b - SSM_BLOCKS[0] if ssm_part else jnp.where(b < SSM_BLOCKS[0], b, b - (SSM_BLOCKS[1] - SSM_BLOCKS[0]))
        local = jnp.clip(local, 0, ref.shape[1] // W_IN_BLOCK - 1)
        return ref.at[:, pl.ds(local * W_IN_BLOCK, W_IN_BLOCK)]

    def dst_view(ref, sender):
        return ref.at[sender, :, pl.ds(j * W_IN_BLOCK, W_IN_BLOCK)]

    pred = in_ssm if ssm_part else (lambda dest: jnp.logical_not(in_ssm(dest)))
    return _Item(src, out, src_view, dst_view, pred)


def _cast_shards(arrs):
    def body(*refs):
        n = len(refs) // 2
        for i in range(n):
            refs[n + i][...] = refs[i][...].astype(BF16)

    vm = pl.BlockSpec(memory_space=pltpu.VMEM)
    return _pcall(body, name="cast_shards",
                  out_shape=tuple(jax.ShapeDtypeStruct(a.shape, BF16) for a in arrs),
                  in_specs=[vm] * len(arrs), out_specs=tuple([vm] * len(arrs)),
                  compiler_params=_params())(*arrs)


def _s5_discretise(a_re, a_im, log_dt, b_re_t, b_im_t):
    dt = jnp.exp(log_dt)
    lam_re = jnp.minimum(a_re, -1e-4)
    lam_im = a_im
    mag = jnp.exp(lam_re * dt)
    abar_re = mag * jnp.cos(lam_im * dt)
    abar_im = mag * jnp.sin(lam_im * dt)
    den = lam_re * lam_re + lam_im * lam_im
    num_re = abar_re - 1.0
    f_re = (num_re * lam_re + abar_im * lam_im) / den
    f_im = (abar_im * lam_re - num_re * lam_im) / den
    f_re, f_im = f_re[:, None, :], f_im[:, None, :]
    bb_re = f_re * b_re_t - f_im * b_im_t
    bb_im = f_re * b_im_t + f_im * b_re_t
    return abar_re, abar_im, bb_re, bb_im


def _group_masks():
    spread = lax.broadcasted_iota(jnp.int32, (G_P, 16 * G_P), 1) % G_P == lax.broadcasted_iota(
        jnp.int32, (G_P, 16 * G_P), 0)
    own = lax.broadcasted_iota(jnp.int32, (16 * G_H, 16 * G_P), 0) // G_H == lax.broadcasted_iota(
        jnp.int32, (16 * G_H, 16 * G_P), 1) // G_P
    return spread, own


def _s5_prep(a_re, a_im, log_dt, b_re_t, b_im_t, c_re, c_im, n_pow):
    def body(ar_ref, ai_ref, ld_ref, br_ref, bi_ref, cr_ref, ci_ref, wb_ref, wct_ref, pr_ref, pi_ref):
        abar_re, abar_im, bb_re, bb_im = _s5_discretise(ar_ref[...], ai_ref[...], ld_ref[...], br_ref[...], bi_ref[...])
        spread, own = _group_masks()
        spread = spread.astype(BF16)
        for ref, parts in ((wb_ref, (bb_re, bb_im)), (wct_ref, (cr_ref[...], -ci_ref[...]))):
            for half, t in enumerate(parts):
                for q in range(N_Q):
                    blocks = t[q * 16:(q + 1) * 16].reshape(16 * G_H, G_P).astype(BF16)
                    dense = jnp.where(own, _dot(blocks, spread), 0.0)
                    ref[q, :, half * (Q_W // 2):(half + 1) * (Q_W // 2)] = dense.astype(BF16)
        p_re, p_im = abar_re, abar_im
        pr_ref[0] = p_re
        pi_ref[0] = p_im
        for k in range(1, n_pow):
            p_re, p_im = p_re * abar_re - p_im * abar_im, p_re * abar_im + p_im * abar_re
            pr_ref[k] = p_re
            pi_ref[k] = p_im

    vm = pl.BlockSpec(memory_space=pltpu.VMEM)
    return _pcall(body, name="s5_prep",
                  out_shape=(jax.ShapeDtypeStruct((N_Q, 16 * G_H, Q_W), BF16),
                             jax.ShapeDtypeStruct((N_Q, 16 * G_H, Q_W), BF16),
                             jax.ShapeDtypeStruct((n_pow, GROUPS, G_P), F32),
                             jax.ShapeDtypeStruct((n_pow, GROUPS, G_P), F32)),
                  in_specs=[vm] * 7, out_specs=(vm, vm, vm, vm), compiler_params=_params(),
                  )(a_re, a_im, log_dt, b_re_t, b_im_t, c_re, c_im)


def _s5_prep_bwd(a_re, a_im, log_dt, b_re_t, b_im_t, d_abar_re, d_abar_im, d_bb_re, d_bb_im):
    def body(ar_ref, ai_ref, ld_ref, br_ref, bi_ref, dar_ref, dai_ref, dbr_ref, dbi_ref,
             gar_ref, gai_ref, gld_ref, gbr_ref, gbi_ref):
        _, vjp = jax.vjp(_s5_discretise, ar_ref[...], ai_ref[...], ld_ref[...], br_ref[...], bi_ref[...])
        g = vjp((dar_ref[...], dai_ref[...], dbr_ref[...], dbi_ref[...]))
        gar_ref[...] = g[0]
        gai_ref[...] = g[1]
        gld_ref[...] = g[2]
        gbr_ref[...] = g[3]
        gbi_ref[...] = g[4]

    vm = pl.BlockSpec(memory_space=pltpu.VMEM)
    ins = (a_re, a_im, log_dt, b_re_t, b_im_t)
    return _pcall(body, name="s5_prep_bwd",
                  out_shape=tuple(jax.ShapeDtypeStruct(a.shape, F32) for a in ins),
                  in_specs=[vm] * 9, out_specs=tuple([vm] * 5), compiler_params=_params(),
                  )(*ins, d_abar_re, d_abar_im, d_bb_re, d_bb_im)


def _state_layout(re, im):
    lead = re.shape[:-2]
    r = re.reshape(lead + (N_Q, 1, 16 * G_P))
    i = im.reshape(lead + (N_Q, 1, 16 * G_P))
    return jnp.concatenate([r, i], axis=-2).reshape(lead + (N_STATE,))


def _state_unlayout(v):
    v4 = v.reshape(N_Q, 2, 16, G_P)
    return v4[:, 0].reshape(GROUPS, G_P), v4[:, 1].reshape(GROUPS, G_P)


def _perm_matrix(tb):
    k_steps = tb // SUBLANES
    r = jnp.arange(tb)
    src = (r % SUBLANES) * k_steps + r // SUBLANES
    return (src[:, None] == jnp.arange(tb)[None, :]).astype(BF16)


def _lane_chunks(q):
    for lc in range(Q_W // 2 // LANE_CHUNK):
        re = q * Q_W + lc * LANE_CHUNK
        yield re, re + Q_W // 2


def _steps(lo, hi, body, init):
    if hi - lo <= SCAN_UNROLL:
        for k in range(lo, hi):
            init = body(k, init)
        return init
    trips = (hi - lo) // SCAN_UNROLL

    def trip(j, carry):
        for u in range(SCAN_UNROLL):
            carry = body(lo + j * SCAN_UNROLL + u, carry)
        return carry

    carry = lax.fori_loop(0, trips, trip, init)
    for k in range(lo + trips * SCAN_UNROLL, hi):
        carry = body(k, carry)
    return carry


def _tile(k):
    if isinstance(k, int):
        return pl.ds(k * SUBLANES, SUBLANES)
    return pl.ds(pl.multiple_of(k * SUBLANES, SUBLANES), SUBLANES)


def _scan_forward(q, s_ref, p_ref, carry_ref, enter_ref, fin_ref, k_steps):
    for re, im in _lane_chunks(q):
        lr, li = pl.ds(re, LANE_CHUNK), pl.ds(im, LANE_CHUNK)
        a_re = jnp.broadcast_to(p_ref[0:1, lr], (SUBLANES, LANE_CHUNK))
        a_im = jnp.broadcast_to(p_ref[0:1, li], (SUBLANES, LANE_CHUNK))

        def local(k, st):
            sr, si = st
            rows = _tile(k)
            nr = a_re * sr - a_im * si + s_ref[rows, lr]
            ni = a_re * si + a_im * sr + s_ref[rows, li]
            s_ref[rows, lr] = nr
            s_ref[rows, li] = ni
            return nr, ni

        zero = jnp.zeros((SUBLANES, LANE_CHUNK), F32)
        fr, fi = _steps(0, k_steps, local, (zero, zero))
        fin_ref[:, lr] = fr
        fin_ref[:, li] = fi
        ak_re, ak_im = p_ref[k_steps - 1:k_steps, lr], p_ref[k_steps - 1:k_steps, li]
        c_re, c_im = carry_ref[:, lr], carry_ref[:, li]
        for seg in range(SUBLANES):
            enter_ref[seg:seg + 1, lr] = c_re
            enter_ref[seg:seg + 1, li] = c_im
            f_re, f_im = fin_ref[seg:seg + 1, lr], fin_ref[seg:seg + 1, li]
            c_re, c_im = f_re + ak_re * c_re - ak_im * c_im, f_im + ak_re * c_im + ak_im * c_re
        carry_ref[:, lr] = c_re
        carry_ref[:, li] = c_im
        e_re, e_im = enter_ref[:, lr], enter_ref[:, li]

        def fix(k, _):
            rows = _tile(k)
            p_re = p_ref[pl.ds(k, 1), lr]
            p_im = p_ref[pl.ds(k, 1), li]
            s_ref[rows, lr] = s_ref[rows, lr] + (p_re * e_re - p_im * e_im)
            s_ref[rows, li] = s_ref[rows, li] + (p_re * e_im + p_im * e_re)
            return 0

        _steps(0, k_steps, fix, 0)


def _scan_backward(q, g_ref, s_ref, p_ref, carry_ref, s_in_ref, fin_ref, da_ref, k_steps):
    seg_id = lax.broadcasted_iota(jnp.int32, (SUBLANES, LANE_CHUNK), 0)
    for re, im in _lane_chunks(q):
        lr, li = pl.ds(re, LANE_CHUNK), pl.ds(im, LANE_CHUNK)
        a_re = jnp.broadcast_to(p_ref[0:1, lr], (SUBLANES, LANE_CHUNK))
        a_im = jnp.broadcast_to(p_ref[0:1, li], (SUBLANES, LANE_CHUNK))

        def local(j, st):
            sr, si = st
            rows = _tile(k_steps - 1 - j)
            nr = a_re * sr + a_im * si + g_ref[rows, lr]
            ni = a_re * si - a_im * sr + g_ref[rows, li]
            g_ref[rows, lr] = nr
            g_ref[rows, li] = ni
            return nr, ni

        zero = jnp.zeros((SUBLANES, LANE_CHUNK), F32)
        fr, fi = _steps(0, k_steps, local, (zero, zero))
        fin_ref[:, lr] = fr
        fin_ref[:, li] = fi
        ak_re, ak_im = p_ref[k_steps - 1:k_steps, lr], p_ref[k_steps - 1:k_steps, li]
        c_re, c_im = carry_ref[:, lr], carry_ref[:, li]
        lam_in = [None] * SUBLANES
        for seg in reversed(range(SUBLANES)):
            lam_in[seg] = (c_re, c_im)
            f_re, f_im = fin_ref[seg:seg + 1, lr], fin_ref[seg:seg + 1, li]
            c_re, c_im = f_re + ak_re * c_re + ak_im * c_im, f_im + ak_re * c_im - ak_im * c_re
        carry_ref[:, lr] = c_re
        carry_ref[:, li] = c_im
        for seg in range(SUBLANES):
            fin_ref[seg:seg + 1, lr] = lam_in[seg][0]
            fin_ref[seg:seg + 1, li] = lam_in[seg][1]
        e_re, e_im = fin_ref[:, lr], fin_ref[:, li]

        def fix_with(k, acc, sp_re, sp_im):
            acc_re, acc_im = acc
            rows = _tile(k)
            p_re = p_ref[pl.ds(k_steps - 1 - k, 1), lr]
            p_im = p_ref[pl.ds(k_steps - 1 - k, 1), li]
            l_re = g_ref[rows, lr] + (p_re * e_re + p_im * e_im)
            l_im = g_ref[rows, li] + (p_re * e_im - p_im * e_re)
            g_ref[rows, lr] = l_re
            g_ref[rows, li] = l_im
            return acc_re + (l_re * sp_re + l_im * sp_im), acc_im + (l_im * sp_re - l_re * sp_im)

        def fix(k, acc):
            prev = _tile(k - 1)
            return fix_with(k, acc, s_ref[prev, lr], s_ref[prev, li])

        last = _tile(k_steps - 1)
        before_re = jnp.where(seg_id == 0, s_in_ref[:, lr], pltpu.roll(s_ref[last, lr], 1, axis=0))
        before_im = jnp.where(seg_id == 0, s_in_ref[:, li], pltpu.roll(s_ref[last, li], 1, axis=0))
        acc = fix_with(0, (zero, zero), before_re, before_im)
        acc_re, acc_im = _steps(1, k_steps, fix, acc)
        da_ref[:, lr] = da_ref[:, lr] + jnp.sum(acc_re, axis=0, keepdims=True)
        da_ref[:, li] = da_ref[:, li] + jnp.sum(acc_im, axis=0, keepdims=True)


def _prenorm(x, mod3, norm_pre):
    xn, r = _rms_parts(x)
    return xn, r, xn * norm_pre * (1.0 + mod3[1:2, :]) + mod3[0:1, :]


CHIP_FLIPS = (4, 2, 6)


def _shard_order(me):
    flips = [0, 1] + [f + c for f in CHIP_FLIPS for c in (0, 1)]
    return jnp.stack([me ^ f for f in flips]).astype(jnp.int32)


def _in_proj(x, mod3, norm_pre, w_in_s, shards):
    rows = x.shape[0]
    tb = _tb(rows, 512)
    nblk = rows // tb
    n_sh = len(shards)
    last_step = N_DEV - 1
    items = [_gather_item(0, 0, _pool_rows_of(shards[0].shape[1]))] + \
            [_gather_item(t, t, _rows_of(shards[t].shape[0])) for t in range(1, n_sh)]

    def body(order_ref, x_ref, mod_ref, np_ref, w_src, *rest):
        src_refs, proj_ref, w_full, out_refs = rest[:n_sh], rest[n_sh], rest[n_sh + 1], rest[n_sh + 2:2 * n_sh + 2]
        h_scr, wg, ssem, rsem, lsem, *sems = rest[2 * n_sh + 2:]
        s, i = pl.program_id(0), pl.program_id(1)
        me3 = _me()
        me = _flat(me3)
        sibling = _peer(1)

        def own_copy(slot, k):
            return pltpu.make_async_remote_copy(src_ref=w_src, dst_ref=wg.at[me], send_sem=ssem.at[slot],
                                                recv_sem=rsem.at[slot], device_id=_peer(k), device_id_type=MESH)

        def passed_copy(j):
            p = _flat(_peer(CHIP_FLIPS[j]))
            return pltpu.make_async_remote_copy(src_ref=wg.at[p], dst_ref=wg.at[p], send_sem=ssem.at[4 + j],
                                                recv_sem=rsem.at[4 + j], device_id=sibling, device_id_type=MESH)

        def arrival(slot, flip):
            p = _flat(_peer(flip))
            pltpu.make_async_remote_copy(src_ref=w_src, dst_ref=wg.at[p], send_sem=ssem.at[slot],
                                         recv_sem=rsem.at[slot], device_id=sibling, device_id_type=MESH).wait_recv()

        def keep(t):
            p = order_ref[t]
            return pltpu.make_async_copy(wg.at[p], w_full.at[:, pl.ds(p * W_IN_SHARD, W_IN_SHARD)], lsem.at[1 + t])

        first = i == 0
        for t in range(last_step):
            pl.when(first & (s == t + 1))(lambda t=t: keep(t).start())

        @pl.when(first & (s == 0))
        def _():
            mine = pltpu.make_async_copy(w_src, wg.at[me], lsem.at[0])
            mine.start()
            own_copy(0, 1).start()
            for j, f in enumerate(CHIP_FLIPS[:2]):
                own_copy(1 + j, f).start()
            mine.wait()

        @pl.when(first & (s == 1))
        def _():
            arrival(0, 1)

        for j, f in enumerate(CHIP_FLIPS):
            @pl.when(first & (s == 2 + 2 * j))
            def _(j=j, f=f):
                arrival(1 + j, f)
                passed_copy(j).start()
                if j == 0:
                    own_copy(3, CHIP_FLIPS[2]).start()

            @pl.when(first & (s == 3 + 2 * j))
            def _(j=j, f=f):
                arrival(4 + j, f + 1)

        @pl.when(first & (s == last_step - 1))
        def _():
            _hosted_copies(items, src_refs, out_refs, *sems, act="start")

        rows_i = pl.ds(pl.multiple_of(i * tb, tb), tb)

        @pl.when(s == 0)
        def _():
            _, _, h = _prenorm(x_ref[...], mod_ref[...], np_ref[...])
            h_scr[rows_i, :] = h.astype(BF16)

        proj_ref[...] = _dot(h_scr[rows_i, :], wg[order_ref[s]]).astype(BF16)

        @pl.when((s == last_step) & (i == nblk - 1))
        def _():
            own_copy(0, 1).wait_send()
            for j, f in enumerate(CHIP_FLIPS):
                own_copy(1 + j, f).wait_send()
                passed_copy(j).wait_send()
            keep(last_step).start()
            for t in range(N_DEV):
                keep(t).wait()
            _hosted_copies(items, src_refs, out_refs, *sems, act="wait")

    full = [jax.ShapeDtypeStruct((4, 256, 256), BF16)] + [jax.ShapeDtypeStruct((D, D), BF16)] * (n_sh - 1)
    grid_spec = pltpu.PrefetchScalarGridSpec(
        num_scalar_prefetch=1, grid=(N_DEV, nblk),
        in_specs=[pl.BlockSpec((tb, D), lambda s, i, order: (jnp.where(s == 0, i, nblk - 1), 0)),
                  pl.BlockSpec((3, D), lambda s, i, order: (0, 0)), pl.BlockSpec((1, D), lambda s, i, order: (0, 0)),
                  ANY] + [ANY] * n_sh,
        out_specs=(pl.BlockSpec((tb, W_IN_SHARD), lambda s, i, order: (i, order[s])), ANY, *([ANY] * n_sh)),
        scratch_shapes=[pltpu.VMEM((rows, D), BF16), pltpu.VMEM((N_DEV, D, W_IN_SHARD), BF16),
                        pltpu.SemaphoreType.DMA((N_DEV - 1,)), pltpu.SemaphoreType.DMA((N_DEV - 1,)),
                        pltpu.SemaphoreType.DMA((1 + N_DEV,))] + _sem_scratch(items))
    return _pcall(body, name="in_proj", grid_spec=grid_spec,
                  out_shape=(jax.ShapeDtypeStruct((rows, N_IN), BF16), jax.ShapeDtypeStruct((D, N_IN), BF16), *full),
                  compiler_params=_params(("arbitrary", "arbitrary")),
                  )(_shard_order(_flat(_me())), x, mod3, norm_pre, w_in_s, *shards)


def _pool_windows(ext, tb, first_row):
    pos = (first_row + lax.broadcasted_iota(jnp.int32, (tb, 1), 0) + 1).astype(F32)
    pooled, counts = [], []
    for g, w in enumerate(POOL_WINDOWS):
        acc = ext[:, g * 256:(g + 1) * 256]
        tok = acc[HALO:, :]
        s = 1
        while s < w:
            acc = acc + pltpu.roll(acc, s, axis=0)
            s *= 2
        cnt = jnp.minimum(pos, float(w))
        pooled.append(acc[HALO:, :] / cnt - tok)
        counts.append(cnt)
    return pooled, counts


def _pool_fwd(proj, pool_w, pool_scale):
    rows = proj.shape[0]
    tb = _tb(rows, 512)
    hb = tb // HALO

    def body(u_ref, halo_ref, z_ref, pw_ref, ps_ref, y_ref):
        i = pl.program_id(0)
        u = u_ref[...].astype(F32)
        halo = jnp.where(i > 0, halo_ref[...].astype(F32), 0.0)
        pooled, _ = _pool_windows(jnp.concatenate([halo, u], axis=0), tb, i * tb)
        silu_z, _ = _silu_parts(z_ref[...].astype(F32))
        for g in range(4):
            cols = slice(g * 256, (g + 1) * 256)
            mixed = _dot(pooled[g].astype(BF16), pw_ref[g])
            y_ref[:, cols] = (mixed * ps_ref[:, cols] * silu_z[:, cols]).astype(BF16)

    return _pcall(body, name="pool_fwd", grid=(rows // tb,),
                  out_shape=jax.ShapeDtypeStruct((rows, D), BF16),
                  in_specs=[pl.BlockSpec((tb, D), lambda i: (i, 0)),
                            pl.BlockSpec((HALO, D), lambda i: (jnp.maximum(i * hb - 1, 0), 0)),
                            pl.BlockSpec((tb, D), lambda i: (i, 1)),
                            _full((4, 256, 256)), _full((1, D))],
                  out_specs=pl.BlockSpec((tb, D), lambda i: (i, 0)),
                  compiler_params=_params(("arbitrary",)))(proj, proj, proj, pool_w, pool_scale)


def _ssm_fwd(proj, pm, pmt, wb, wct, ptab, dvec, glu_w, glu_b, shards):
    rows = proj.shape[0]
    tb = pm.shape[0]
    k_steps = tb // SUBLANES
    nblk = rows // tb
    n_sh = len(shards)
    items = [_gather_item(t, t, _rows_of(shards[t].shape[0])) for t in range(n_sh)]

    def body(u_ref, z_ref, pm_ref, pmt_ref, wb_ref, wct_ref, p_ref, d_ref, gw_ref, gb_ref, *rest):
        src_refs = rest[:n_sh]
        y_ref, ys_ref, carry_out_ref, s_ref = rest[n_sh:n_sh + 4]
        out_refs = rest[n_sh + 4:2 * n_sh + 4]
        carry_ref, enter_ref, fin_ref, *sems = rest[2 * n_sh + 4:]

        @pl.when(pl.program_id(0) == 0)
        def _():
            _hosted_copies(items, src_refs, out_refs, *sems, act="start")
            carry_ref[...] = jnp.zeros_like(carry_ref)

        carry_out_ref[...] = carry_ref[...]
        up = _dot(pm_ref[...], u_ref[...]).astype(BF16)

        for q in range(N_Q):
            s_ref[:, q * Q_W:(q + 1) * Q_W] = _dot(up[:, q * 256:(q + 1) * 256], wb_ref[q])
        for q in range(N_Q):
            _scan_forward(q, s_ref, p_ref, carry_ref, enter_ref, fin_ref, k_steps)
        for q in range(N_Q):
            cols = slice(q * 256, (q + 1) * 256)
            y = _dot_nt(s_ref[:, q * Q_W:(q + 1) * Q_W].astype(BF16), wct_ref[q])
            ys_ref[:, cols] = y + d_ref[:, cols] * up[:, cols].astype(F32)
        yg, _ = _gelu_parts(ys_ref[...])
        gate = jax.nn.sigmoid(_dot(yg.astype(BF16), gw_ref[...]) + gb_ref[...])
        zp = _dot(pm_ref[...], z_ref[...])
        silu_z, _ = _silu_parts(zp)
        y_ref[...] = _dot(pmt_ref[...], (yg * gate * silu_z).astype(BF16)).astype(BF16)

        @pl.when(pl.program_id(0) == nblk - 1)
        def _():
            _hosted_copies(items, src_refs, out_refs, *sems, act="wait")

    return _pcall(body, name="ssm_fwd", grid=(nblk,),
                  out_shape=(jax.ShapeDtypeStruct((rows, D), BF16), jax.ShapeDtypeStruct((rows, D), F32),
                             jax.ShapeDtypeStruct((nblk, 1, N_STATE), F32),
                             jax.ShapeDtypeStruct((rows, N_STATE), F32),
                             *[jax.ShapeDtypeStruct((D, D), BF16)] * n_sh),
                  in_specs=[pl.BlockSpec((tb, D), lambda i: (i, 2)), pl.BlockSpec((tb, D), lambda i: (i, 3)),
                            _full((tb, tb)), _full((tb, tb)),
                            _full((N_Q, 256, Q_W), single=True), _full((N_Q, 256, Q_W), single=True),
                            _full((k_steps, N_STATE)), _full((1, D)), _full((D, D), single=True), _full((1, D))] +
                           [ANY] * n_sh,
                  out_specs=(pl.BlockSpec((tb, D), lambda i: (i, 0)), pl.BlockSpec((tb, D), lambda i: (i, 0)),
                             pl.BlockSpec((None, 1, N_STATE), lambda i: (i, 0, 0)),
                             pl.BlockSpec((tb, N_STATE), lambda i: (i, 0)), *([ANY] * n_sh)),
                  scratch_shapes=[pltpu.VMEM((1, N_STATE), F32),
                                  pltpu.VMEM((SUBLANES, N_STATE), F32), pltpu.VMEM((SUBLANES, N_STATE), F32)] +
                                 _sem_scratch(items),
                  compiler_params=_params(("arbitrary",)))(proj, proj, pm, pmt, wb, wct, ptab, dvec, glu_w, glu_b,
                                                           *shards)


def _head(x, target, proj, y_pool, y_ssm, mod3, norm_post, wbp, wbs, wout):
    rows = x.shape[0]
    tb = _tb(rows, 256)
    nblk = rows // tb
    n_feat = float(D)

    def body(x_ref, t_ref, gp_ref, gs_ref, yp_ref, ys_ref, mod_ref, npost_ref, wbp_ref, wbs_ref, wout_ref,
             loss_ref, dy_ref, dyp_ref, dys_ref, dg_ref, dwbp_hbm, dwbs_hbm, dwout_hbm, vec_ref,
             acc_bp, acc_bs, acc_out, acc_loss, acc_vec):
        i = pl.program_id(0)

        @pl.when(i == 0)
        def _():
            acc_bp[...] = jnp.zeros_like(acc_bp)
            acc_bs[...] = jnp.zeros_like(acc_bs)
            acc_out[...] = jnp.zeros_like(acc_out)
            acc_loss[...] = jnp.zeros_like(acc_loss)
            acc_vec[...] = jnp.zeros_like(acc_vec)

        yp, ys = yp_ref[...], ys_ref[...]
        sgp = jax.nn.sigmoid(gp_ref[...].astype(F32))
        sgs = jax.nn.sigmoid(gs_ref[...].astype(F32))
        pb = _dot(yp, wbp_ref[...])
        psm = _dot(ys, wbs_ref[...])
        mb = (sgp * pb + sgs * psm).astype(BF16)
        out = _dot(mb, wout_ref[...])
        on, r = _rms_parts(out)
        gate = mod_ref[2:3, :]
        npost = npost_ref[...]
        normed = on * npost
        diff = x_ref[...] + gate * normed - t_ref[...]
        acc_loss[...] += jnp.sum(diff * diff, axis=0, keepdims=True)
        dy = diff * (1.0 / n_feat)
        dy_ref[...] = dy
        acc_vec[0:1, :] += jnp.sum(dy * normed, axis=0, keepdims=True)
        dn = dy * gate
        acc_vec[1:2, :] += jnp.sum(dn * on, axis=0, keepdims=True)
        dout = _rms_bwd(dn * npost, on, r).astype(BF16)
        acc_out[...] += _dot_tn(mb, dout)
        dm = _dot_nt(dout, wout_ref[...])
        dpb = (dm * sgp).astype(BF16)
        dps = (dm * sgs).astype(BF16)
        dg_ref[:, :D] = (dm * pb * sgp * (1.0 - sgp)).astype(BF16)
        dg_ref[:, D:] = (dm * psm * sgs * (1.0 - sgs)).astype(BF16)
        acc_bp[...] += _dot_tn(yp, dpb)
        acc_bs[...] += _dot_tn(ys, dps)
        dyp_ref[...] = _dot_nt(dpb, wbp_ref[...]).astype(BF16)
        dys_ref[...] = _dot_nt(dps, wbs_ref[...]).astype(BF16)

        @pl.when(i == nblk - 1)
        def _():
            loss_ref[...] = 0.5 / n_feat * jnp.sum(acc_loss[...], axis=1, keepdims=True)
            vec_ref[...] = acc_vec[...]
            pltpu.sync_copy(acc_bp, dwbp_hbm)
            pltpu.sync_copy(acc_bs, dwbs_hbm)
            pltpu.sync_copy(acc_out, dwout_hbm)

    row = lambda c: pl.BlockSpec((tb, D), lambda i: (i, c))
    w = _full((D, D), single=True)
    return _pcall(body, name="head", grid=(nblk,),
                  out_shape=(jax.ShapeDtypeStruct((1, 1), F32), jax.ShapeDtypeStruct((rows, D), F32),
                             jax.ShapeDtypeStruct((rows, D), BF16), jax.ShapeDtypeStruct((rows, D), BF16),
                             jax.ShapeDtypeStruct((rows, 2 * D), BF16),
                             jax.ShapeDtypeStruct((D, D), F32), jax.ShapeDtypeStruct((D, D), F32),
                             jax.ShapeDtypeStruct((D, D), F32), jax.ShapeDtypeStruct((2, D), F32)),
                  in_specs=[row(0), row(0), row(4), row(5), row(0), row(0), _full((3, D)), _full((1, D)), w, w, w],
                  out_specs=(_full((1, 1)), row(0), row(0), row(0), pl.BlockSpec((tb, 2 * D), lambda i: (i, 0)),
                             ANY, ANY, ANY, _full((2, D))),
                  scratch_shapes=[pltpu.VMEM((D, D), F32), pltpu.VMEM((D, D), F32), pltpu.VMEM((D, D), F32),
                                  pltpu.VMEM((1, D), F32), pltpu.VMEM((2, D), F32)],
                  compiler_params=_params(("arbitrary",)))(x, target, proj, proj, y_pool, y_ssm, mod3, norm_post,
                                                           wbp, wbs, wout)


def _glu_bwd(dys, proj, ys_pre, pm, pmt, glu_w, glu_b):
    rows = dys.shape[0]
    tb = pm.shape[0]
    nblk = rows // tb

    def body(dys_ref, z_ref, ysp_ref, pm_ref, pmt_ref, gw_ref, gb_ref, dyp_ref, dz_ref, dgw_hbm, dgb_ref,
             acc_w, acc_b):
        i = pl.program_id(0)

        @pl.when(i == 0)
        def _():
            acc_w[...] = jnp.zeros_like(acc_w)
            acc_b[...] = jnp.zeros_like(acc_b)

        d_out = _dot(pm_ref[...], dys_ref[...])
        z = _dot(pm_ref[...], z_ref[...])
        yg, dgelu = _gelu_parts(ysp_ref[...])
        ygb = yg.astype(BF16)
        sg = jax.nn.sigmoid(_dot(ygb, gw_ref[...]) + gb_ref[...])
        silu_z, dsilu_z = _silu_parts(z)
        dz = d_out * (yg * sg) * dsilu_z
        dz_ref[...] = _dot(pmt_ref[...], dz.astype(BF16)).astype(BF16)
        dglu = d_out * silu_z
        dq = dglu * yg * sg * (1.0 - sg)
        dqb = dq.astype(BF16)
        acc_b[...] += jnp.sum(dq, axis=0, keepdims=True)
        acc_w[...] += _dot_tn(ygb, dqb)
        dyg = dglu * sg + _dot_nt(dqb, gw_ref[...])
        dyp_ref[...] = (dyg * dgelu).astype(BF16)

        @pl.when(i == nblk - 1)
        def _():
            dgb_ref[...] = acc_b[...]
            pltpu.sync_copy(acc_w, dgw_hbm)

    row = lambda c: pl.BlockSpec((tb, D), lambda i: (i, c))
    return _pcall(body, name="glu_bwd", grid=(nblk,),
                  out_shape=(jax.ShapeDtypeStruct((rows, D), BF16), jax.ShapeDtypeStruct((rows, D), BF16),
                             jax.ShapeDtypeStruct((D, D), F32), jax.ShapeDtypeStruct((1, D), F32)),
                  in_specs=[row(0), row(3), row(0), _full((tb, tb)), _full((tb, tb)),
                            _full((D, D), single=True), _full((1, D))],
                  out_specs=(row(0), row(0), ANY, _full((1, D))),
                  scratch_shapes=[pltpu.VMEM((D, D), F32), pltpu.VMEM((1, D), F32)],
                  compiler_params=_params(("arbitrary",)))(dys, proj, ys_pre, pm, pmt, glu_w, glu_b)


def _ssm_bwd(dyp, proj, states, carries, pm, pmt, wb, wct, ptab, dvec, mat_grads, dpool_w, dw_in_rest):
    rows = dyp.shape[0]
    tb = pm.shape[0]
    k_steps = tb // SUBLANES
    nblk = rows // tb
    n_mat = len(mat_grads)
    hosted = [*mat_grads, dpool_w, dw_in_rest]
    n_h = len(hosted)
    shard_rows = D // N_DEV
    pool_rows = dpool_w.shape[1] // N_DEV
    items = [_scatter_item(t, t, _rows_of(shard_rows)) for t in range(n_mat)] + \
            [_scatter_item(n_mat, n_mat, _pool_rows_of(pool_rows))] + \
            [_w_in_block_item(n_mat + 1, n_mat + 1, j, ssm_part=False) for j in range(W_IN_SHARD // W_IN_BLOCK)]
    n_in, n_out = 10, 5

    def body(*refs):
        dyp_ref, u_ref, s_ref, cin_ref, pm_ref, pmt_ref, wb_ref, wct_ref, p_ref, d_ref = refs[:n_in]
        src_refs = refs[n_in:n_in + n_h]
        du_ref, dbb_ref, dcc_ref, da_ref, dd_ref = refs[n_in + n_h:n_in + n_h + n_out]
        recv_refs = refs[n_in + n_h + n_out:n_in + 2 * n_h + n_out]
        (g_ref, carry_b, fin_ref, acc_wb, acc_wct, acc_da, acc_dd, dup_ref,
         *sems) = refs[n_in + 2 * n_h + n_out:]
        i = pl.program_id(0)

        @pl.when(i == 0)
        def _():
            _hosted_copies(items, src_refs, recv_refs, *sems, act="start")
            carry_b[...] = jnp.zeros_like(carry_b)
            acc_wb[...] = jnp.zeros_like(acc_wb)
            acc_wct[...] = jnp.zeros_like(acc_wct)
            acc_da[...] = jnp.zeros_like(acc_da)
            acc_dd[...] = jnp.zeros_like(acc_dd)

        dy = dyp_ref[...]
        up = _dot(pm_ref[...], u_ref[...]).astype(BF16)
        acc_dd[...] += jnp.sum(dy.astype(F32) * up.astype(F32), axis=0, keepdims=True)
        for q in range(N_Q):
            cols = slice(q * 256, (q + 1) * 256)
            g_ref[:, q * Q_W:(q + 1) * Q_W] = _dot(dy[:, cols], wct_ref[q])
            acc_wct[q] += _dot_tn(dy[:, cols], s_ref[:, q * Q_W:(q + 1) * Q_W].astype(BF16))
        for q in range(N_Q):
            _scan_backward(q, g_ref, s_ref, p_ref, carry_b, cin_ref, fin_ref, acc_da, k_steps)
        for q in range(N_Q):
            cols = slice(q * 256, (q + 1) * 256)
            lam = g_ref[:, q * Q_W:(q + 1) * Q_W].astype(BF16)
            acc_wb[q] += _dot_tn(up[:, cols], lam)
            dup_ref[:, cols] = (_dot_nt(lam, wb_ref[q]) + d_ref[:, cols] * dy[:, cols].astype(F32)).astype(BF16)
        du_ref[...] = _dot(pmt_ref[...], dup_ref[...]).astype(BF16)

        @pl.when(i == nblk - 1)
        def _():
            da_ref[...] = acc_da[...]
            dd_ref[...] = acc_dd[...]
            spread, own = _group_masks()
            spread = spread.astype(F32)
            for acc, out in ((acc_wb, dbb_ref), (acc_wct, dcc_ref)):
                for half in range(2):
                    for q in range(N_Q):
                        kept = jnp.where(own, acc[q, :, half * (Q_W // 2):(half + 1) * (Q_W // 2)], 0.0)
                        out[half, q] = lax.dot_general(kept, spread, (((1,), (1,)), ((), ())),
                                                       preferred_element_type=F32, precision=lax.Precision.HIGHEST)
            _hosted_copies(items, src_refs, recv_refs, *sems, act="wait")

    rev = lambda c: pl.BlockSpec((tb, D), lambda i: (nblk - 1 - i, c))
    recv = [jax.ShapeDtypeStruct((N_DEV, shard_rows, D), F32)] * n_mat + \
           [jax.ShapeDtypeStruct((N_DEV, dpool_w.shape[0], pool_rows, dpool_w.shape[2]), F32),
            jax.ShapeDtypeStruct((N_DEV, D, W_IN_SHARD), BF16)]
    return _pcall(body, name="ssm_bwd", grid=(nblk,),
                  out_shape=(jax.ShapeDtypeStruct((rows, D), BF16),
                             jax.ShapeDtypeStruct((2, N_Q, 16 * G_H, G_P), F32),
                             jax.ShapeDtypeStruct((2, N_Q, 16 * G_H, G_P), F32),
                             jax.ShapeDtypeStruct((1, N_STATE), F32), jax.ShapeDtypeStruct((1, D), F32), *recv),
                  in_specs=[rev(0), rev(2), pl.BlockSpec((tb, N_STATE), lambda i: (nblk - 1 - i, 0)),
                            pl.BlockSpec((None, 1, N_STATE), lambda i: (nblk - 1 - i, 0, 0)),
                            _full((tb, tb)), _full((tb, tb)),
                            _full((N_Q, 256, Q_W), single=True), _full((N_Q, 256, Q_W), single=True),
                            _full((k_steps, N_STATE)), _full((1, D))] + [ANY] * n_h,
                  out_specs=(rev(0), _full((2, N_Q, 16 * G_H, G_P)), _full((2, N_Q, 16 * G_H, G_P)),
                             _full((1, N_STATE)), _full((1, D)), *([ANY] * n_h)),
                  scratch_shapes=[pltpu.VMEM((tb, N_STATE), F32), pltpu.VMEM((1, N_STATE), F32),
                                  pltpu.VMEM((SUBLANES, N_STATE), F32),
                                  pltpu.VMEM((N_Q, 256, Q_W), F32), pltpu.VMEM((N_Q, 256, Q_W), F32),
                                  pltpu.VMEM((1, N_STATE), F32), pltpu.VMEM((1, D), F32),
                                  pltpu.VMEM((tb, D), BF16)] + _sem_scratch(items),
                  compiler_params=_params(("arbitrary",), vmem=60 * 1024 * 1024),
                  )(dyp, proj, states, carries, pm, pmt, wb, wct, ptab, dvec, *hosted)


def _pool_bwd(dyp, proj, pool_w, pool_scale):
    rows = dyp.shape[0]
    tb = _tb(rows, 512)
    nblk = rows // tb
    hb = tb // HALO

    def body(dy_ref, u_ref, halo_ref, z_ref, pw_ref, ps_ref, dp_ref, dpw_ref, dps_ref, ahead_ref):
        i = pl.program_id(0)
        blk = nblk - 1 - i

        @pl.when(i == 0)
        def _():
            ahead_ref[...] = jnp.zeros_like(ahead_ref)
            dpw_ref[...] = jnp.zeros_like(dpw_ref)
            dps_ref[...] = jnp.zeros_like(dps_ref)

        u = u_ref[...].astype(F32)
        halo = jnp.where(blk > 0, halo_ref[...].astype(F32), 0.0)
        pooled, counts = _pool_windows(jnp.concatenate([halo, u], axis=0), tb, blk * tb)
        silu_z, dsilu_z = _silu_parts(z_ref[...].astype(F32))
        dy = dy_ref[...].astype(F32)
        for g, w in enumerate(POOL_WINDOWS):
            cols = slice(g * 256, (g + 1) * 256)
            pooled_b = pooled[g].astype(BF16)
            mixed = _dot(pooled_b, pw_ref[g])
            scale = ps_ref[:, cols]
            dp_ref[:, D + g * 256:D + (g + 1) * 256] = (dy[:, cols] * (mixed * scale) * dsilu_z[:, cols]).astype(BF16)
            dms = dy[:, cols] * silu_z[:, cols]
            dps_ref[:, cols] += jnp.sum(dms * mixed, axis=0, keepdims=True)
            dmixed = (dms * scale).astype(BF16)
            dpw_ref[g] += _dot_tn(pooled_b, dmixed)
            dpooled = _dot_nt(dmixed, pw_ref[g])
            ratio = dpooled / counts[g]
            acc = jnp.concatenate([ratio, ahead_ref[:, cols]], axis=0)
            ahead_ref[:, cols] = ratio[:HALO, :]
            s = 1
            while s < w:
                acc = acc + pltpu.roll(acc, tb + HALO - s, axis=0)
                s *= 2
            dp_ref[:, cols] = (acc[:tb, :] - dpooled).astype(BF16)

    rev = lambda c: pl.BlockSpec((tb, D), lambda i: (nblk - 1 - i, c))
    return _pcall(body, name="pool_bwd", grid=(nblk,),
                  out_shape=(jax.ShapeDtypeStruct((rows, 2 * D), BF16), jax.ShapeDtypeStruct((4, 256, 256), F32),
                             jax.ShapeDtypeStruct((1, D), F32)),
                  in_specs=[rev(0), rev(0),
                            pl.BlockSpec((HALO, D), lambda i: (jnp.maximum((nblk - 1 - i) * hb - 1, 0), 0)),
                            rev(1), _full((4, 256, 256)), _full((1, D))],
                  out_specs=(pl.BlockSpec((tb, 2 * D), lambda i: (nblk - 1 - i, 0)), _full((4, 256, 256)),
                             _full((1, D))),
                  scratch_shapes=[pltpu.VMEM((HALO, D), F32)],
                  compiler_params=_params(("arbitrary",)))(dyp, proj, proj, proj, pool_w, pool_scale)


def _dproj_specs(tb):
    return [pl.BlockSpec((tb, 2 * D), lambda i: (i, 0)), pl.BlockSpec((tb, D), lambda i: (i, 0)),
            pl.BlockSpec((tb, D), lambda i: (i, 0)), pl.BlockSpec((tb, 2 * D), lambda i: (i, 0))]


def _in_proj_bwd_x(x, dy, dpp, dus, dzs, dpg, mod3, norm_pre, w_in, dw_in_ssm, small32, small16, recv_w_in):
    rows = x.shape[0]
    tb = _tb(rows, 256)
    nblk = rows // tb
    items = [_w_in_block_item(0, 0, j, ssm_part=True) for j in range(W_IN_SHARD // W_IN_BLOCK)] + \
            [_Item(1, 1, _whole, _slot), _Item(2, 2, _whole, _slot)]

    def body(x_ref, dy_ref, dpp_ref, dus_ref, dzs_ref, dpg_ref, mod_ref, np_ref, w_ref,
             dw_src, s32_src, s16_src, _, gx_ref, vec_ref, recv_w, recv32, recv16, *sems):
        src_refs, recv_refs = (dw_src, s32_src, s16_src), (recv_w, recv32, recv16)

        @pl.when(pl.program_id(0) == 0)
        def _():
            _hosted_copies(items, src_refs, recv_refs, *sems, act="start")
            vec_ref[...] = jnp.zeros_like(vec_ref)

        dh = _dot_nt(dpp_ref[...], w_ref[:, 0:2 * D])
        dh += _dot_nt(dus_ref[...], w_ref[:, 2 * D:3 * D])
        dh += _dot_nt(dzs_ref[...], w_ref[:, 3 * D:4 * D])
        dh += _dot_nt(dpg_ref[...], w_ref[:, 4 * D:6 * D])
        xn, r, _ = _prenorm(x_ref[...], mod_ref[...], np_ref[...])
        one_scale = 1.0 + mod_ref[1:2, :]
        vec_ref[0:1, :] += jnp.sum(dh, axis=0, keepdims=True)
        vec_ref[1:2, :] += jnp.sum(dh * xn, axis=0, keepdims=True) * np_ref[...]
        vec_ref[2:3, :] += jnp.sum(dh * xn, axis=0, keepdims=True) * one_scale
        gx_ref[...] = dy_ref[...] + _rms_bwd(dh * (np_ref[...] * one_scale), xn, r)

        @pl.when(pl.program_id(0) == nblk - 1)
        def _():
            _hosted_copies(items, src_refs, recv_refs, *sems, act="wait")

    row = pl.BlockSpec((tb, D), lambda i: (i, 0))
    recv = (jax.ShapeDtypeStruct(recv_w_in.shape, recv_w_in.dtype),
            jax.ShapeDtypeStruct((N_DEV,) + small32.shape, small32.dtype),
            jax.ShapeDtypeStruct((N_DEV,) + small16.shape, small16.dtype))
    return _pcall(body, name="in_proj_bwd_x", grid=(nblk,),
                  out_shape=(jax.ShapeDtypeStruct((rows, D), F32), jax.ShapeDtypeStruct((3, D), F32), *recv),
                  in_specs=[row, row] + _dproj_specs(tb) + [_full((3, D)), _full((1, D)),
                                                            _full((D, N_IN), single=True)] + [ANY] * 4,
                  out_specs=(row, _full((3, D)), ANY, ANY, ANY),
                  input_output_aliases={12: 2},
                  scratch_shapes=_sem_scratch(items),
                  compiler_params=_params(("arbitrary",)))(x, dy, dpp, dus, dzs, dpg, mod3, norm_pre, w_in,
                                                           dw_in_ssm, small32, small16, recv_w_in)


def _in_proj_bwd_w(name, x, dparts, mod3, norm_pre):
    rows = x.shape[0]
    tb = _tb(rows, 256)
    nblk = rows // tb
    widths = [p.shape[1] for p in dparts]
    n_p = len(dparts)

    def body(x_ref, *rest):
        part_refs, (mod_ref, np_ref, dw_ref, acc) = rest[:n_p], rest[n_p:]
        i = pl.program_id(0)

        @pl.when(i == 0)
        def _():
            acc[...] = jnp.zeros_like(acc)

        _, _, h = _prenorm(x_ref[...], mod_ref[...], np_ref[...])
        ht = h.astype(BF16)
        lo = 0
        for ref, w in zip(part_refs, widths):
            acc[:, lo:lo + w] += _dot_tn(ht, ref[...])
            lo += w

        @pl.when(i == nblk - 1)
        def _():
            dw_ref[...] = acc[...].astype(BF16)

    row = pl.BlockSpec((tb, D), lambda i: (i, 0))
    return _pcall(body, name=name, grid=(nblk,),
                  out_shape=jax.ShapeDtypeStruct((D, sum(widths)), BF16),
                  in_specs=[row] + [pl.BlockSpec((tb, w), lambda i: (i, 0)) for w in widths] +
                           [_full((3, D)), _full((1, D))],
                  out_specs=_full((D, sum(widths))),
                  scratch_shapes=[pltpu.VMEM((D, sum(widths)), F32)],
                  compiler_params=_params(("arbitrary",)))(x, *dparts, mod3, norm_pre)


def _adamw_math(w, g, m, v):
    m = ADAM_B1 * m + (1.0 - ADAM_B1) * g
    v = ADAM_B2 * v + (1.0 - ADAM_B2) * (g * g)
    m_hat = m / (1.0 - ADAM_B1 ** ADAM_STEP)
    v_hat = v / (1.0 - ADAM_B2 ** ADAM_STEP)
    delta = -ADAM_LR * (m_hat / (jnp.sqrt(v_hat) + ADAM_EPS) + ADAM_WD * w)
    return delta, m, v


def _sum_sources(ref):
    g = ref[0].astype(F32)
    for s in range(1, N_DEV):
        g = g + ref[s].astype(F32)
    return g


def _adamw_reduce(name, parts, w, m, v):
    r, c = w.shape
    tr = r if r * c <= 256 * 1024 else max(8, (256 * 1024 // c) // 8 * 8)
    while r % tr:
        tr -= 8

    def body(p_ref, w_ref, m_ref, v_ref, g_ref, d_ref, nm_ref, nv_ref):
        g = _sum_sources(p_ref)
        g_ref[...] = g
        d_ref[...], nm_ref[...], nv_ref[...] = _adamw_math(w_ref[...], g, m_ref[...], v_ref[...])

    blk = pl.BlockSpec((tr, c), lambda i: (i, 0))
    return _pcall(body, name=name, grid=(r // tr,),
                  out_shape=tuple([jax.ShapeDtypeStruct((r, c), F32)] * 4),
                  in_specs=[pl.BlockSpec((N_DEV, tr, c), lambda i: (0, i, 0)), blk, blk, blk],
                  out_specs=(blk, blk, blk, blk),
                  compiler_params=_params(("arbitrary",)))(parts, w, m, v)


def _adamw_small(gs, ws, ms, vs):
    n = len(gs)

    def body(*refs):
        ins, outs = refs[:4 * n], refs[4 * n:]
        for t in range(n):
            g_ref, w_ref, m_ref, v_ref = ins[4 * t:4 * t + 4]
            outs[3 * t][...], outs[3 * t + 1][...], outs[3 * t + 2][...] = _adamw_math(
                w_ref[...], g_ref[...], m_ref[...], v_ref[...])

    vm = pl.BlockSpec(memory_space=pltpu.VMEM)
    flat = [a for t in range(n) for a in (gs[t], ws[t], ms[t], vs[t])]
    return _pcall(body, name="adamw_small",
                  out_shape=tuple(jax.ShapeDtypeStruct(w.shape, F32) for w in ws for _ in range(3)),
                  in_specs=[vm] * (4 * n), out_specs=tuple([vm] * (3 * n)), compiler_params=_params())(*flat)


def _sum_small(parts):
    n = len(parts)

    def body(*refs):
        for t in range(n):
            refs[n + t][...] = _sum_sources(refs[t])

    vm = pl.BlockSpec(memory_space=pltpu.VMEM)
    return _pcall(body, name="sum_small",
                  out_shape=tuple(jax.ShapeDtypeStruct(p.shape[1:], F32) for p in parts),
                  in_specs=[vm] * n, out_specs=tuple([vm] * n), compiler_params=_params())(*parts)


def _ada_update(c_all, dmod_cols, w, m, v):
    def body(c_ref, dm_ref, w_ref, m_ref, v_ref, g_ref, d_ref, nm_ref, nv_ref):
        ca = c_ref[...]
        g = lax.dot_general(ca * jax.nn.sigmoid(ca), dm_ref[...], (((0,), (0,)), ((), ())),
                            preferred_element_type=F32, precision=lax.Precision.HIGHEST)
        g_ref[...] = g
        d_ref[...], nm_ref[...], nv_ref[...] = _adamw_math(w_ref[...], g, m_ref[...], v_ref[...])

    vm = pl.BlockSpec(memory_space=pltpu.VMEM)
    return _pcall(body, name="ada_update", out_shape=tuple([jax.ShapeDtypeStruct(w.shape, F32)] * 4),
                  in_specs=[vm] * 5, out_specs=(vm, vm, vm, vm), compiler_params=_params())(c_all, dmod_cols, w, m, v)


def kernel(x, c, w_ada, b_ada, norm_pre, norm_post, w_in, pool_w, pool_scale, ssm_a_re, ssm_a_im, ssm_log_dt, ssm_b_re, ssm_b_im, ssm_c_re, ssm_c_im, ssm_d, glu_w, glu_b, w_branch_pool, w_branch_ssm, w_out, loss_target, m_w_ada, m_b_ada, m_norm_pre, m_norm_post, m_w_in, m_pool_w, m_pool_scale, m_ssm_a_re, m_ssm_a_im, m_ssm_log_dt, m_ssm_b_re, m_ssm_b_im, m_ssm_c_re, m_ssm_c_im, m_ssm_d, m_glu_w, m_glu_b, m_w_branch_pool, m_w_branch_ssm, m_w_out, v_w_ada, v_b_ada, v_norm_pre, v_norm_post, v_w_in, v_pool_w, v_pool_scale, v_ssm_a_re, v_ssm_a_im, v_ssm_log_dt, v_ssm_b_re, v_ssm_b_im, v_ssm_c_re, v_ssm_c_im, v_ssm_d, v_glu_w, v_glu_b, v_w_branch_pool, v_w_branch_ssm, v_w_out):
    given = dict(locals())
    me = _flat(_me())
    rows = x.shape[1]
    x2 = x[0]
    target = loss_target[0]
    ada_cols = w_ada.shape[2]

    b_ada_s = lax.dynamic_slice(b_ada, (0, me * ada_cols), (1, ada_cols))
    c_all, mod_rows = _ada_exchange(c, w_ada[0], b_ada_s)
    mod3 = mod_rows.reshape(3, D)

    shards = _cast_shards([w_in[0], pool_w[0], glu_w[0], w_branch_pool[0], w_branch_ssm[0], w_out[0]])

    tb_ssm = _tb(rows, 256)
    k_steps = tb_ssm // SUBLANES
    a_re, a_im = ssm_a_re[0], ssm_a_im[0]
    log_dt = ssm_log_dt[0].reshape(GROUPS, 1)
    b_re_t, b_im_t = ssm_b_re[0].transpose(0, 2, 1), ssm_b_im[0].transpose(0, 2, 1)
    wb, wct, pow_re, pow_im = _s5_prep(a_re, a_im, log_dt, b_re_t, b_im_t, ssm_c_re[0], ssm_c_im[0], k_steps)
    ptab = _state_layout(pow_re, pow_im)
    dvec = ssm_d[0].reshape(1, D)
    pm = _perm_matrix(tb_ssm)
    pmt = pm.T

    proj, w_in_g, pool_w_g, glu_g = _in_proj(x2, mod3, norm_pre, shards[0], shards[1:3])
    y_pool = _pool_fwd(proj, pool_w_g, pool_scale)
    y_ssm, ys_pre, carries, states, wbp_g, wbs_g, wout_g = _ssm_fwd(
        proj, pm, pmt, wb, wct, ptab, dvec, glu_g, glu_b, shards[3:])
    loss_part, dy, dyp, dys, dpg, dwbp, dwbs, dwout, head_vec = _head(
        x2, target, proj, y_pool, y_ssm, mod3, norm_post, wbp_g, wbs_g, wout_g)

    dpp, dpool_w, dpool_scale = _pool_bwd(dyp, proj, pool_w_g, pool_scale)
    dw_in_rest = _in_proj_bwd_w("in_proj_bwd_w_rest", x2, [dpp, dpg], mod3, norm_pre)
    dy_pre, dzs, dglu_w, dglu_b = _glu_bwd(dys, proj, ys_pre, pm, pmt, glu_g, glu_b)
    dus, dbb, dcc, dabar, dd, p_glu, p_wbp, p_wbs, p_wout, p_pool_w, p_w_in = _ssm_bwd(
        dy_pre, proj, states, carries, pm, pmt, wb, wct, ptab, dvec, [dglu_w, dwbp, dwbs, dwout], dpool_w, dw_in_rest)
    dw_in_ssm = _in_proj_bwd_w("in_proj_bwd_w_ssm", x2, [dus, dzs], mod3, norm_pre)

    small32 = jnp.concatenate([head_vec, dpool_scale, dglu_b, dd, jnp.broadcast_to(loss_part, (1, D)),
                               jnp.zeros((2, D), F32), dabar.reshape(8, D)], axis=0)
    small16 = jnp.concatenate([dbb.reshape(2 * GROUPS, D), dcc.reshape(2 * GROUPS, D)], axis=0).astype(BF16)
    grad_x, pre_vec, p_w_in, p_small32, p_small16 = _in_proj_bwd_x(
        x2, dy, dpp, dus, dzs, dpg, mod3, norm_pre, w_in_g, dw_in_ssm, small32, small16, p_w_in)
    small_pre = jnp.concatenate([pre_vec, jnp.zeros((5, D), F32)], axis=0)
    (p_pre,) = _exchange("gather_prenorm_sums", [small_pre], [jax.ShapeDtypeStruct((N_DEV, 8, D), F32)],
                         [_Item(0, 0, _whole, _slot)])

    tot32, tot16, tot_pre = _sum_small([p_small32, p_small16, p_pre])
    d_abar_re, d_abar_im = _state_unlayout(tot32[8:16].reshape(N_STATE))
    d_bb_re, d_bb_im = tot16[0:64].reshape(GROUPS, G_H, G_P), tot16[64:128].reshape(GROUPS, G_H, G_P)
    g_a_re, g_a_im, g_log_dt, g_b_re_t, g_b_im_t = _s5_prep_bwd(
        a_re, a_im, log_dt, b_re_t, b_im_t, d_abar_re, d_abar_im, d_bb_re, d_bb_im)

    grads, deltas, new_m, new_v = {}, {}, {}, {}

    small = []

    def small_update(name, g2):
        small.append((name, g2))

    def shard_update(name, parts):
        shape = given[name].shape
        r2 = parts.shape[1:] if parts.ndim == 3 else (parts.shape[1] * parts.shape[2], parts.shape[3])
        w2, m2, v2 = (given[p + name].reshape(r2) for p in ("", "m_", "v_"))
        out = _adamw_reduce("adamw_" + name, parts.reshape((N_DEV,) + tuple(r2)), w2, m2, v2)
        grads[name], deltas[name], new_m[name], new_v[name] = (a.reshape(shape) for a in out)

    dmod_all = jnp.concatenate([p_pre[:, 0:2, :], p_small32[:, 0:1, :]], axis=1).reshape(N_DEV, 3 * D)
    dmod_cols = lax.dynamic_slice(dmod_all, (0, me * ada_cols), (N_DEV, ada_cols))
    out = _ada_update(c_all, dmod_cols, w_ada[0], m_w_ada[0], v_w_ada[0])
    grads['w_ada'], deltas['w_ada'], new_m['w_ada'], new_v['w_ada'] = (a.reshape(w_ada.shape) for a in out)

    small_update('b_ada', jnp.concatenate([tot_pre[0:2], tot32[0:1]], axis=0).reshape(1, 3 * D))
    small_update('norm_pre', tot_pre[2:3])
    small_update('norm_post', tot32[1:2])
    small_update('pool_scale', tot32[2:3])
    small_update('glu_b', tot32[3:4])
    small_update('ssm_d', tot32[4:5])
    small_update('ssm_a_re', g_a_re)
    small_update('ssm_a_im', g_a_im)
    small_update('ssm_log_dt', g_log_dt.reshape(1, GROUPS))
    small_update('ssm_b_re', g_b_re_t.transpose(0, 2, 1).reshape(GROUPS, G_P * G_H))
    small_update('ssm_b_im', g_b_im_t.transpose(0, 2, 1).reshape(GROUPS, G_P * G_H))
    small_update('ssm_c_re', tot16[128:192])
    small_update('ssm_c_im', -tot16[192:256])
    flat = _adamw_small([g2 for _, g2 in small],
                        *[[given[p + name].reshape(g2.shape) for name, g2 in small] for p in ("", "m_", "v_")])
    for t, (name, g2) in enumerate(small):
        shape = given[name].shape
        grads[name], deltas[name], new_m[name], new_v[name] = (
            a.reshape(shape) for a in (g2, *flat[3 * t:3 * t + 3]))
    shard_update('w_in', p_w_in)
    shard_update('pool_w', p_pool_w)
    shard_update('glu_w', p_glu)
    shard_update('w_branch_pool', p_wbp)
    shard_update('w_branch_ssm', p_wbs)
    shard_update('w_out', p_wout)

    return (tot32[5, 0], grad_x[None], *[grads[n] for n in WEIGHTS], *[deltas[n] for n in WEIGHTS],
            *[new_m[n] for n in WEIGHTS], *[new_v[n] for n in WEIGHTS])
```

```python
import functools
import math
from typing import Callable, NamedTuple, Optional

import jax
import jax.numpy as jnp
from jax import lax
from jax.experimental import pallas as pl
from jax.experimental.pallas import tpu as pltpu

F32 = jnp.float32
BF16 = jnp.bfloat16
MESH = pl.DeviceIdType.MESH

D = 1024
N_DEV = 8
N_IN = 6 * D
GROUPS = 64
G_H = 16
G_P = 64
N_Q = 4
Q_W = 2 * 16 * G_P
N_STATE = N_Q * Q_W
POOL_WINDOWS = (2, 4, 8, 16)
HALO = 16
RMS_EPS = 1e-6
SUBLANES = 8
LANE_CHUNK = 512
SCAN_UNROLL = 2
VMEM_LIMIT = 56 * 1024 * 1024

ADAM_LR = 0.001
ADAM_B1 = 0.9
ADAM_B2 = 0.999
ADAM_EPS = 1e-08
ADAM_WD = 0.01
ADAM_STEP = 10

WEIGHTS = ['w_ada', 'b_ada', 'norm_pre', 'norm_post', 'w_in', 'pool_w', 'pool_scale', 'ssm_a_re',
           'ssm_a_im', 'ssm_log_dt', 'ssm_b_re', 'ssm_b_im', 'ssm_c_re', 'ssm_c_im', 'ssm_d', 'glu_w',
           'glu_b', 'w_branch_pool', 'w_branch_ssm', 'w_out']


def _pcall(body, **kw):
    return pl.pallas_call(body, **kw)


def _params(sem=None, vmem=VMEM_LIMIT):
    return pltpu.CompilerParams(dimension_semantics=sem, vmem_limit_bytes=vmem)


def _tb(rows, pref):
    return pref if rows % pref == 0 and rows // pref >= 2 else rows // 2


def _full(shape, single=False):
    nd = len(shape)
    if single:
        return pl.BlockSpec(shape, lambda i: (0,) * nd, pipeline_mode=pl.Buffered(1))
    return pl.BlockSpec(shape, lambda i: (0,) * nd)


ANY = pl.BlockSpec(memory_space=pl.ANY)


def _me():
    return lax.axis_index("x"), lax.axis_index("y"), lax.axis_index("c")


def _flat(p):
    return 4 * p[0] + 2 * p[1] + p[2]


def _peer(k):
    x, y, c = _me()
    return (1 - x if k & 4 else x, 1 - y if k & 2 else y, 1 - c if k & 1 else c)


def _silu_parts(z):
    s = jax.nn.sigmoid(z)
    return z * s, s * (1.0 + z * (1.0 - s))


_GELU_C = math.sqrt(2.0 / math.pi)


def _gelu_parts(x):
    x2 = x * x
    t = jnp.tanh(_GELU_C * (x + 0.044715 * x * x2))
    g = 0.5 * x * (1.0 + t)
    dg = 0.5 * (1.0 + t) + 0.5 * x * (1.0 - t * t) * (_GELU_C * (1.0 + 3.0 * 0.044715 * x2))
    return g, dg


def _dot(a, b):
    return jnp.dot(a, b, preferred_element_type=F32)


def _dot_nt(a, b):
    return lax.dot_general(a, b, (((1,), (1,)), ((), ())), preferred_element_type=F32)


def _dot_tn(a, b):
    return lax.dot_general(a, b, (((0,), (0,)), ((), ())), preferred_element_type=F32)


def _rms_parts(x):
    r = lax.rsqrt(jnp.mean(x * x, axis=-1, keepdims=True) + RMS_EPS)
    return x * r, r


def _rms_bwd(dxn, xn, r):
    return r * (dxn - xn * jnp.mean(dxn * xn, axis=-1, keepdims=True))


def _ada_exchange(c, w_ada_s, b_ada_s):
    cols = w_ada_s.shape[1]

    def body(c_ref, w_ref, b_ref, call_ref, mod_ref, part_ref, ssem, rsem, lsem):
        me3 = _me()
        me = _flat(me3)
        mine = pltpu.make_async_copy(c_ref, call_ref.at[pl.ds(me, 1), :], lsem.at[0])
        mine.start()
        sends = []
        for k in range(1, N_DEV):
            cp = pltpu.make_async_remote_copy(src_ref=c_ref, dst_ref=call_ref.at[pl.ds(me, 1), :],
                                              send_sem=ssem.at[k - 1], recv_sem=rsem.at[k - 1],
                                              device_id=_peer(k), device_id_type=MESH)
            cp.start()
            sends.append(cp)
        mine.wait()
        for k in range(1, N_DEV):
            p = _flat(_peer(k))
            pltpu.make_async_remote_copy(src_ref=c_ref, dst_ref=call_ref.at[pl.ds(p, 1), :],
                                         send_sem=ssem.at[k - 1], recv_sem=rsem.at[k - 1],
                                         device_id=_peer(k), device_id_type=MESH).wait_recv()
        for cp in sends:
            cp.wait_send()
        ca = call_ref[...]
        act = ca * jax.nn.sigmoid(ca)
        part_ref[...] = jnp.dot(act, w_ref[...], preferred_element_type=F32,
                                precision=lax.Precision.HIGHEST) + b_ref[...]
        own = pltpu.make_async_copy(part_ref.at[pl.ds(me, 1), :], mod_ref.at[pl.ds(me, 1), :], lsem.at[1])
        own.start()
        sends = []
        for k in range(1, N_DEV):
            p = _flat(_peer(k))
            s = N_DEV - 1 + k - 1
            cp = pltpu.make_async_remote_copy(src_ref=part_ref.at[pl.ds(p, 1), :],
                                              dst_ref=mod_ref.at[pl.ds(me, 1), :],
                                              send_sem=ssem.at[s], recv_sem=rsem.at[s],
                                              device_id=_peer(k), device_id_type=MESH)
            cp.start()
            sends.append(cp)
        own.wait()
        for k in range(1, N_DEV):
            p = _flat(_peer(k))
            s = N_DEV - 1 + k - 1
            pltpu.make_async_remote_copy(src_ref=part_ref.at[pl.ds(p, 1), :],
                                         dst_ref=mod_ref.at[pl.ds(p, 1), :],
                                         send_sem=ssem.at[s], recv_sem=rsem.at[s],
                                         device_id=_peer(k), device_id_type=MESH).wait_recv()
        for cp in sends:
            cp.wait_send()

    vm = pl.BlockSpec(memory_space=pltpu.VMEM)
    return _pcall(
        body, name="ada_exchange",
        out_shape=(jax.ShapeDtypeStruct((N_DEV, D), F32), jax.ShapeDtypeStruct((N_DEV, cols), F32)),
        in_specs=[vm, vm, vm], out_specs=(vm, vm),
        scratch_shapes=[pltpu.VMEM((N_DEV, cols), F32),
                        pltpu.SemaphoreType.DMA((2 * (N_DEV - 1),)),
                        pltpu.SemaphoreType.DMA((2 * (N_DEV - 1),)),
                        pltpu.SemaphoreType.DMA((2,))],
    )(c, w_ada_s, b_ada_s)


class _Item(NamedTuple):
    src: int
    out: int
    src_view: Callable
    dst_view: Callable
    pred: Optional[Callable] = None


def _when(pred, dest, fn):
    if pred is None:
        fn()
    else:
        pl.when(pred(dest))(fn)


def _n_sems(items):
    return len(items) * (N_DEV - 1)


def _hosted_copies(items, srcs, outs, ssem, rsem, lsem, act):
    me = _flat(_me())
    for t, it in enumerate(items):
        local = lambda t=t, it=it: pltpu.make_async_copy(
            it.src_view(srcs[it.src], me), it.dst_view(outs[it.out], me), lsem.at[t])
        if act == "start":
            _when(it.pred, me, lambda local=local: local().start())
        else:
            _when(it.pred, me, lambda local=local: local().wait())
    for k in range(1, N_DEV):
        p3 = _peer(k)
        p = _flat(p3)
        for t, it in enumerate(items):
            s = t * (N_DEV - 1) + k - 1
            send = lambda it=it, s=s, p=p, p3=p3: pltpu.make_async_remote_copy(
                src_ref=it.src_view(srcs[it.src], p), dst_ref=it.dst_view(outs[it.out], me),
                send_sem=ssem.at[s], recv_sem=rsem.at[s], device_id=p3, device_id_type=MESH)
            recv = lambda it=it, s=s, p=p, p3=p3: pltpu.make_async_remote_copy(
                src_ref=it.src_view(srcs[it.src], p), dst_ref=it.dst_view(outs[it.out], p),
                send_sem=ssem.at[s], recv_sem=rsem.at[s], device_id=p3, device_id_type=MESH)
            if act == "start":
                _when(it.pred, p, lambda send=send: send().start())
            else:
                _when(it.pred, me, lambda recv=recv: recv().wait_recv())
                _when(it.pred, p, lambda send=send: send().wait_send())


def _sem_scratch(items):
    return [pltpu.SemaphoreType.DMA((_n_sems(items),)), pltpu.SemaphoreType.DMA((_n_sems(items),)),
            pltpu.SemaphoreType.DMA((len(items),))]


def _exchange(name, srcs, out_structs, items):
    n_src, n_out = len(srcs), len(out_structs)

    def body(*refs):
        src_refs, out_refs = refs[:n_src], refs[n_src:n_src + n_out]
        sems = refs[n_src + n_out:]
        _hosted_copies(items, src_refs, out_refs, *sems, act="start")
        _hosted_copies(items, src_refs, out_refs, *sems, act="wait")

    return _pcall(body, name=name, out_shape=tuple(out_structs),
                  in_specs=[ANY] * n_src, out_specs=tuple([ANY] * n_out),
                  scratch_shapes=_sem_scratch(items))(*srcs)


def _whole(ref, dest):
    return ref


def _slot(ref, sender):
    return ref.at[sender]


def _rows_of(rows):
    return lambda ref, dev: ref.at[pl.ds(dev * rows, rows), :]


def _cols_of(cols):
    return lambda ref, dev: ref.at[:, pl.ds(dev * cols, cols)]


def _pool_rows_of(rows):
    return lambda ref, dev: ref.at[:, pl.ds(dev * rows, rows), :]


def _gather_item(src, out, dst_view):
    return _Item(src, out, _whole, dst_view)


def _scatter_item(src, out, src_view):
    return _Item(src, out, src_view, _slot)


W_IN_BLOCK = 256
W_IN_SHARD = N_IN // N_DEV
SSM_BLOCKS = (2 * D // W_IN_BLOCK, 4 * D // W_IN_BLOCK)


def _w_in_block_item(src, out, j, ssm_part):
    def block(dest):
        return (W_IN_SHARD // W_IN_BLOCK) * dest + j

    def in_ssm(dest):
        b = block(dest)
        return (b >= SSM_BLOCKS[0]) & (b < SSM_BLOCKS[1])

    def src_view(ref, dest):
        b = block(dest)
        local = b - SSM_BLOCKS[0] if ssm_part else jnp.where(b < SSM_BLOCKS[0], b, b - (SSM_BLOCKS[1] - SSM_BLOCKS[0]))
        local = jnp.clip(local, 0, ref.shape[1] // W_IN_BLOCK - 1)
        return ref.at[:, pl.ds(local * W_IN_BLOCK, W_IN_BLOCK)]

    def dst_view(ref, sender):
        return ref.at[sender, :, pl.ds(j * W_IN_BLOCK, W_IN_BLOCK)]

    pred = in_ssm if ssm_part else (lambda dest: jnp.logical_not(in_ssm(dest)))
    return _Item(src, out, src_view, dst_view, pred)


def _cast_shards(arrs):
    def body(*refs):
        n = len(refs) // 2
        for i in range(n):
            refs[n + i][...] = refs[i][...].astype(BF16)

    vm = pl.BlockSpec(memory_space=pltpu.VMEM)
    return _pcall(body, name="cast_shards",
                  out_shape=tuple(jax.ShapeDtypeStruct(a.shape, BF16) for a in arrs),
                  in_specs=[vm] * len(arrs), out_specs=tuple([vm] * len(arrs)),
                  compiler_params=_params())(*arrs)


def _s5_discretise(a_re, a_im, log_dt, b_re_t, b_im_t):
    dt = jnp.exp(log_dt)
    lam_re = jnp.minimum(a_re, -1e-4)
    lam_im = a_im
    mag = jnp.exp(lam_re * dt)
    abar_re = mag * jnp.cos(lam_im * dt)
    abar_im = mag * jnp.sin(lam_im * dt)
    den = lam_re * lam_re + lam_im * lam_im
    num_re = abar_re - 1.0
    f_re = (num_re * lam_re + abar_im * lam_im) / den
    f_im = (abar_im * lam_re - num_re * lam_im) / den
    f_re, f_im = f_re[:, None, :], f_im[:, None, :]
    bb_re = f_re * b_re_t - f_im * b_im_t
    bb_im = f_re * b_im_t + f_im * b_re_t
    return abar_re, abar_im, bb_re, bb_im


def _group_masks():
    spread = lax.broadcasted_iota(jnp.int32, (G_P, 16 * G_P), 1) % G_P == lax.broadcasted_iota(
        jnp.int32, (G_P, 16 * G_P), 0)
    own = lax.broadcasted_iota(jnp.int32, (16 * G_H, 16 * G_P), 0) // G_H == lax.broadcasted_iota(
        jnp.int32, (16 * G_H, 16 * G_P), 1) // G_P
    return spread, own


def _s5_prep(a_re, a_im, log_dt, b_re_t, b_im_t, c_re, c_im, n_pow):
    def body(ar_ref, ai_ref, ld_ref, br_ref, bi_ref, cr_ref, ci_ref, wb_ref, wct_ref, pr_ref, pi_ref):
        abar_re, abar_im, bb_re, bb_im = _s5_discretise(ar_ref[...], ai_ref[...], ld_ref[...], br_ref[...], bi_ref[...])
        spread, own = _group_masks()
        spread = spread.astype(BF16)
        for ref, parts in ((wb_ref, (bb_re, bb_im)), (wct_ref, (cr_ref[...], -ci_ref[...]))):
            for half, t in enumerate(parts):
                for q in range(N_Q):
                    blocks = t[q * 16:(q + 1) * 16].reshape(16 * G_H, G_P).astype(BF16)
                    dense = jnp.where(own, _dot(blocks, spread), 0.0)
                    ref[q, :, half * (Q_W // 2):(half + 1) * (Q_W // 2)] = dense.astype(BF16)
        p_re, p_im = abar_re, abar_im
        pr_ref[0] = p_re
        pi_ref[0] = p_im
        for k in range(1, n_pow):
            p_re, p_im = p_re * abar_re - p_im * abar_im, p_re * abar_im + p_im * abar_re
            pr_ref[k] = p_re
            pi_ref[k] = p_im

    vm = pl.BlockSpec(memory_space=pltpu.VMEM)
    return _pcall(body, name="s5_prep",
                  out_shape=(jax.ShapeDtypeStruct((N_Q, 16 * G_H, Q_W), BF16),
                             jax.ShapeDtypeStruct((N_Q, 16 * G_H, Q_W), BF16),
                             jax.ShapeDtypeStruct((n_pow, GROUPS, G_P), F32),
                             jax.ShapeDtypeStruct((n_pow, GROUPS, G_P), F32)),
                  in_specs=[vm] * 7, out_specs=(vm, vm, vm, vm), compiler_params=_params(),
                  )(a_re, a_im, log_dt, b_re_t, b_im_t, c_re, c_im)


def _s5_prep_bwd(a_re, a_im, log_dt, b_re_t, b_im_t, d_abar_re, d_abar_im, d_bb_re, d_bb_im):
    def body(ar_ref, ai_ref, ld_ref, br_ref, bi_ref, dar_ref, dai_ref, dbr_ref, dbi_ref,
             gar_ref, gai_ref, gld_ref, gbr_ref, gbi_ref):
        _, vjp = jax.vjp(_s5_discretise, ar_ref[...], ai_ref[...], ld_ref[...], br_ref[...], bi_ref[...])
        g = vjp((dar_ref[...], dai_ref[...], dbr_ref[...], dbi_ref[...]))
        gar_ref[...] = g[0]
        gai_ref[...] = g[1]
        gld_ref[...] = g[2]
        gbr_ref[...] = g[3]
        gbi_ref[...] = g[4]

    vm = pl.BlockSpec(memory_space=pltpu.VMEM)
    ins = (a_re, a_im, log_dt, b_re_t, b_im_t)
    return _pcall(body, name="s5_prep_bwd",
                  out_shape=tuple(jax.ShapeDtypeStruct(a.shape, F32) for a in ins),
                  in_specs=[vm] * 9, out_specs=tuple([vm] * 5), compiler_params=_params(),
                  )(*ins, d_abar_re, d_abar_im, d_bb_re, d_bb_im)


def _state_layout(re, im):
    lead = re.shape[:-2]
    r = re.reshape(lead + (N_Q, 1, 16 * G_P))
    i = im.reshape(lead + (N_Q, 1, 16 * G_P))
    return jnp.concatenate([r, i], axis=-2).reshape(lead + (N_STATE,))


def _state_unlayout(v):
    v4 = v.reshape(N_Q, 2, 16, G_P)
    return v4[:, 0].reshape(GROUPS, G_P), v4[:, 1].reshape(GROUPS, G_P)


def _perm_matrix(tb):
    k_steps = tb // SUBLANES
    r = jnp.arange(tb)
    src = (r % SUBLANES) * k_steps + r // SUBLANES
    return (src[:, None] == jnp.arange(tb)[None, :]).astype(BF16)


def _lane_chunks(q):
    for lc in range(Q_W // 2 // LANE_CHUNK):
        re = q * Q_W + lc * LANE_CHUNK
        yield re, re + Q_W // 2


def _steps(lo, hi, body, init):
    if hi - lo <= SCAN_UNROLL:
        for k in range(lo, hi):
            init = body(k, init)
        return init
    trips = (hi - lo) // SCAN_UNROLL

    def trip(j, carry):
        for u in range(SCAN_UNROLL):
            carry = body(lo + j * SCAN_UNROLL + u, carry)
        return carry

    carry = lax.fori_loop(0, trips, trip, init)
    for k in range(lo + trips * SCAN_UNROLL, hi):
        carry = body(k, carry)
    return carry


def _tile(k):
    if isinstance(k, int):
        return pl.ds(k * SUBLANES, SUBLANES)
    return pl.ds(pl.multiple_of(k * SUBLANES, SUBLANES), SUBLANES)


def _scan_forward(q, s_ref, p_ref, carry_ref, enter_ref, fin_ref, k_steps):
    for re, im in _lane_chunks(q):
        lr, li = pl.ds(re, LANE_CHUNK), pl.ds(im, LANE_CHUNK)
        a_re = jnp.broadcast_to(p_ref[0:1, lr], (SUBLANES, LANE_CHUNK))
        a_im = jnp.broadcast_to(p_ref[0:1, li], (SUBLANES, LANE_CHUNK))

        def local(k, st):
            sr, si = st
            rows = _tile(k)
            nr = a_re * sr - a_im * si + s_ref[rows, lr]
            ni = a_re * si + a_im * sr + s_ref[rows, li]
            s_ref[rows, lr] = nr
            s_ref[rows, li] = ni
            return nr, ni

        zero = jnp.zeros((SUBLANES, LANE_CHUNK), F32)
        fr, fi = _steps(0, k_steps, local, (zero, zero))
        fin_ref[:, lr] = fr
        fin_ref[:, li] = fi
        ak_re, ak_im = p_ref[k_steps - 1:k_steps, lr], p_ref[k_steps - 1:k_steps, li]
        c_re, c_im = carry_ref[:, lr], carry_ref[:, li]
        for seg in range(SUBLANES):
            enter_ref[seg:seg + 1, lr] = c_re
            enter_ref[seg:seg + 1, li] = c_im
            f_re, f_im = fin_ref[seg:seg + 1, lr], fin_ref[seg:seg + 1, li]
            c_re, c_im = f_re + ak_re * c_re - ak_im * c_im, f_im + ak_re * c_im + ak_im * c_re
        carry_ref[:, lr] = c_re
        carry_ref[:, li] = c_im
        e_re, e_im = enter_ref[:, lr], enter_ref[:, li]

        def fix(k, _):
            rows = _tile(k)
            p_re = p_ref[pl.ds(k, 1), lr]
            p_im = p_ref[pl.ds(k, 1), li]
            s_ref[rows, lr] = s_ref[rows, lr] + (p_re * e_re - p_im * e_im)
            s_ref[rows, li] = s_ref[rows, li] + (p_re * e_im + p_im * e_re)
            return 0

        _steps(0, k_steps, fix, 0)


def _scan_backward(q, g_ref, s_ref, p_ref, carry_ref, s_in_ref, fin_ref, da_ref, k_steps):
    seg_id = lax.broadcasted_iota(jnp.int32, (SUBLANES, LANE_CHUNK), 0)
    for re, im in _lane_chunks(q):
        lr, li = pl.ds(re, LANE_CHUNK), pl.ds(im, LANE_CHUNK)
        a_re = jnp.broadcast_to(p_ref[0:1, lr], (SUBLANES, LANE_CHUNK))
        a_im = jnp.broadcast_to(p_ref[0:1, li], (SUBLANES, LANE_CHUNK))

        def local(j, st):
            sr, si = st
            rows = _tile(k_steps - 1 - j)
            nr = a_re * sr + a_im * si + g_ref[rows, lr]
            ni = a_re * si - a_im * sr + g_ref[rows, li]
            g_ref[rows, lr] = nr
            g_ref[rows, li] = ni
            return nr, ni

        zero = jnp.zeros((SUBLANES, LANE_CHUNK), F32)
        fr, fi = _steps(0, k_steps, local, (zero, zero))
        fin_ref[:, lr] = fr
        fin_ref[:, li] = fi
        ak_re, ak_im = p_ref[k_steps - 1:k_steps, lr], p_ref[k_steps - 1:k_steps, li]
        c_re, c_im = carry_ref[:, lr], carry_ref[:, li]
        lam_in = [None] * SUBLANES
        for seg in reversed(range(SUBLANES)):
            lam_in[seg] = (c_re, c_im)
            f_re, f_im = fin_ref[seg:seg + 1, lr], fin_ref[seg:seg + 1, li]
            c_re, c_im = f_re + ak_re * c_re + ak_im * c_im, f_im + ak_re * c_im - ak_im * c_re
        carry_ref[:, lr] = c_re
        carry_ref[:, li] = c_im
        for seg in range(SUBLANES):
            fin_ref[seg:seg + 1, lr] = lam_in[seg][0]
            fin_ref[seg:seg + 1, li] = lam_in[seg][1]
        e_re, e_im = fin_ref[:, lr], fin_ref[:, li]

        def fix_with(k, acc, sp_re, sp_im):
            acc_re, acc_im = acc
            rows = _tile(k)
            p_re = p_ref[pl.ds(k_steps - 1 - k, 1), lr]
            p_im = p_ref[pl.ds(k_steps - 1 - k, 1), li]
            l_re = g_ref[rows, lr] + (p_re * e_re + p_im * e_im)
            l_im = g_ref[rows, li] + (p_re * e_im - p_im * e_re)
            g_ref[rows, lr] = l_re
            g_ref[rows, li] = l_im
            return acc_re + (l_re * sp_re + l_im * sp_im), acc_im + (l_im * sp_re - l_re * sp_im)

        def fix(k, acc):
            prev = _tile(k - 1)
            return fix_with(k, acc, s_ref[prev, lr], s_ref[prev, li])

        last = _tile(k_steps - 1)
        before_re = jnp.where(seg_id == 0, s_in_ref[:, lr], pltpu.roll(s_ref[last, lr], 1, axis=0))
        before_im = jnp.where(seg_id == 0, s_in_ref[:, li], pltpu.roll(s_ref[last, li], 1, axis=0))
        acc = fix_with(0, (zero, zero), before_re, before_im)
        acc_re, acc_im = _steps(1, k_steps, fix, acc)
        da_ref[:, lr] = da_ref[:, lr] + jnp.sum(acc_re, axis=0, keepdims=True)
        da_ref[:, li] = da_ref[:, li] + jnp.sum(acc_im, axis=0, keepdims=True)


def _prenorm(x, mod3, norm_pre):
    xn, r = _rms_parts(x)
    return xn, r, xn * norm_pre * (1.0 + mod3[1:2, :]) + mod3[0:1, :]


CHIP_FLIPS = (4, 2, 6)


def _shard_order(me):
    flips = [0, 1] + [f + c for f in CHIP_FLIPS for c in (0, 1)]
    return jnp.stack([me ^ f for f in flips]).astype(jnp.int32)


def _in_proj(x, mod3, norm_pre, w_in_s, shards):
    rows = x.shape[0]
    tb = _tb(rows, 2048)
    nblk = rows // tb
    n_sh = len(shards)
    last_step = N_DEV - 1
    items = [_gather_item(0, 0, _pool_rows_of(shards[0].shape[1]))] + \
            [_gather_item(t, t, _rows_of(shards[t].shape[0])) for t in range(1, n_sh)]

    def body(order_ref, x_ref, mod_ref, np_ref, w_src, *rest):
        src_refs, proj_ref, w_full, out_refs = rest[:n_sh], rest[n_sh], rest[n_sh + 1], rest[n_sh + 2:2 * n_sh + 2]
        h_scr, wg, ssem, rsem, lsem, *sems = rest[2 * n_sh + 2:]
        s, i = pl.program_id(0), pl.program_id(1)
        me3 = _me()
        me = _flat(me3)
        sibling = _peer(1)

        def own_copy(slot, k):
            return pltpu.make_async_remote_copy(src_ref=w_src, dst_ref=wg.at[me], send_sem=ssem.at[slot],
                                                recv_sem=rsem.at[slot], device_id=_peer(k), device_id_type=MESH)

        def passed_copy(j):
            p = _flat(_peer(CHIP_FLIPS[j]))
            return pltpu.make_async_remote_copy(src_ref=wg.at[p], dst_ref=wg.at[p], send_sem=ssem.at[4 + j],
                                                recv_sem=rsem.at[4 + j], device_id=sibling, device_id_type=MESH)

        def arrival(slot, flip):
            p = _flat(_peer(flip))
            pltpu.make_async_remote_copy(src_ref=w_src, dst_ref=wg.at[p], send_sem=ssem.at[slot],
                                         recv_sem=rsem.at[slot], device_id=sibling, device_id_type=MESH).wait_recv()

        def keep(t):
            p = order_ref[t]
            return pltpu.make_async_copy(wg.at[p], w_full.at[:, pl.ds(p * W_IN_SHARD, W_IN_SHARD)], lsem.at[1 + t])

        first = i == 0
        for t in range(last_step):
            pl.when(first & (s == t + 1))(lambda t=t: keep(t).start())

        @pl.when(first & (s == 0))
        def _():
            mine = pltpu.make_async_copy(w_src, wg.at[me], lsem.at[0])
            mine.start()
            own_copy(0, 1).start()
            for j, f in enumerate(CHIP_FLIPS[:2]):
                own_copy(1 + j, f).start()
            mine.wait()

        @pl.when(first & (s == 1))
        def _():
            arrival(0, 1)

        for j, f in enumerate(CHIP_FLIPS):
            @pl.when(first & (s == 2 + 2 * j))
            def _(j=j, f=f):
                arrival(1 + j, f)
                passed_copy(j).start()
                if j == 0:
                    own_copy(3, CHIP_FLIPS[2]).start()

            @pl.when(first & (s == 3 + 2 * j))
            def _(j=j, f=f):
                arrival(4 + j, f + 1)

        @pl.when(first & (s == last_step - 1))
        def _():
            _hosted_copies(items, src_refs, out_refs, *sems, act="start")

        rows_i = pl.ds(pl.multiple_of(i * tb, tb), tb)

        @pl.when(s == 0)
        def _():
            _, _, h = _prenorm(x_ref[...], mod_ref[...], np_ref[...])
            h_scr[rows_i, :] = h.astype(BF16)

        proj_ref[...] = _dot(h_scr[rows_i, :], wg[order_ref[s]]).astype(BF16)

        @pl.when((s == last_step) & (i == nblk - 1))
        def _():
            own_copy(0, 1).wait_send()
            for j, f in enumerate(CHIP_FLIPS):
                own_copy(1 + j, f).wait_send()
                passed_copy(j).wait_send()
            keep(last_step).start()
            for t in range(N_DEV):
                keep(t).wait()
            _hosted_copies(items, src_refs, out_refs, *sems, act="wait")

    full = [jax.ShapeDtypeStruct((4, 256, 256), BF16)] + [jax.ShapeDtypeStruct((D, D), BF16)] * (n_sh - 1)
    grid_spec = pltpu.PrefetchScalarGridSpec(
        num_scalar_prefetch=1, grid=(N_DEV, nblk),
        in_specs=[pl.BlockSpec((tb, D), lambda s, i, order: (jnp.where(s == 0, i, nblk - 1), 0)),
                  pl.BlockSpec((3, D), lambda s, i, order: (0, 0)), pl.BlockSpec((1, D), lambda s, i, order: (0, 0)),
                  ANY] + [ANY] * n_sh,
        out_specs=(pl.BlockSpec((tb, W_IN_SHARD), lambda s, i, order: (i, order[s])), ANY, *([ANY] * n_sh)),
        scratch_shapes=[pltpu.VMEM((rows, D), BF16), pltpu.VMEM((N_DEV, D, W_IN_SHARD), BF16),
                        pltpu.SemaphoreType.DMA((N_DEV - 1,)), pltpu.SemaphoreType.DMA((N_DEV - 1,)),
                        pltpu.SemaphoreType.DMA((1 + N_DEV,))] + _sem_scratch(items))
    return _pcall(body, name="in_proj", grid_spec=grid_spec,
                  out_shape=(jax.ShapeDtypeStruct((rows, N_IN), BF16), jax.ShapeDtypeStruct((D, N_IN), BF16), *full),
                  compiler_params=_params(("arbitrary", "arbitrary")),
                  )(_shard_order(_flat(_me())), x, mod3, norm_pre, w_in_s, *shards)


def _pool_windows(ext, tb, first_row):
    pos = (first_row + lax.broadcasted_iota(jnp.int32, (tb, 1), 0) + 1).astype(F32)
    pooled, counts = [], []
    for g, w in enumerate(POOL_WINDOWS):
        acc = ext[:, g * 256:(g + 1) * 256]
        tok = acc[HALO:, :]
        s = 1
        while s < w:
            acc = acc + pltpu.roll(acc, s, axis=0)
            s *= 2
        cnt = jnp.minimum(pos, float(w))
        pooled.append(acc[HALO:, :] / cnt - tok)
        counts.append(cnt)
    return pooled, counts


def _pool_fwd(proj, pool_w, pool_scale):
    rows = proj.shape[0]
    tb = _tb(rows, 512)
    hb = tb // HALO

    def body(u_ref, halo_ref, z_ref, pw_ref, ps_ref, y_ref):
        i = pl.program_id(0)
        u = u_ref[...].astype(F32)
        halo = jnp.where(i > 0, halo_ref[...].astype(F32), 0.0)
        pooled, _ = _pool_windows(jnp.concatenate([halo, u], axis=0), tb, i * tb)
        silu_z, _ = _silu_parts(z_ref[...].astype(F32))
        for g in range(4):
            cols = slice(g * 256, (g + 1) * 256)
            mixed = _dot(pooled[g].astype(BF16), pw_ref[g])
            y_ref[:, cols] = (mixed * ps_ref[:, cols] * silu_z[:, cols]).astype(BF16)

    return _pcall(body, name="pool_fwd", grid=(rows // tb,),
                  out_shape=jax.ShapeDtypeStruct((rows, D), BF16),
                  in_specs=[pl.BlockSpec((tb, D), lambda i: (i, 0)),
                            pl.BlockSpec((HALO, D), lambda i: (jnp.maximum(i * hb - 1, 0), 0)),
                            pl.BlockSpec((tb, D), lambda i: (i, 1)),
                            _full((4, 256, 256)), _full((1, D))],
                  out_specs=pl.BlockSpec((tb, D), lambda i: (i, 0)),
                  compiler_params=_params(("arbitrary",)))(proj, proj, proj, pool_w, pool_scale)


def _ssm_fwd(proj, pm, pmt, wb, wct, ptab, dvec, glu_w, glu_b, shards):
    rows = proj.shape[0]
    tb = pm.shape[0]
    k_steps = tb // SUBLANES
    nblk = rows // tb
    n_sh = len(shards)
    items = [_gather_item(t, t, _rows_of(shards[t].shape[0])) for t in range(n_sh)]

    def body(u_ref, z_ref, pm_ref, pmt_ref, wb_ref, wct_ref, p_ref, d_ref, gw_ref, gb_ref, *rest):
        src_refs = rest[:n_sh]
        y_ref, ys_ref, carry_out_ref, s_ref = rest[n_sh:n_sh + 4]
        out_refs = rest[n_sh + 4:2 * n_sh + 4]
        carry_ref, enter_ref, fin_ref, *sems = rest[2 * n_sh + 4:]

        @pl.when(pl.program_id(0) == 0)
        def _():
            _hosted_copies(items, src_refs, out_refs, *sems, act="start")
            carry_ref[...] = jnp.zeros_like(carry_ref)

        carry_out_ref[...] = carry_ref[...]
        up = _dot(pm_ref[...], u_ref[...]).astype(BF16)

        for q in range(N_Q):
            s_ref[:, q * Q_W:(q + 1) * Q_W] = _dot(up[:, q * 256:(q + 1) * 256], wb_ref[q])
        for q in range(N_Q):
            _scan_forward(q, s_ref, p_ref, carry_ref, enter_ref, fin_ref, k_steps)
        for q in range(N_Q):
            cols = slice(q * 256, (q + 1) * 256)
            y = _dot_nt(s_ref[:, q * Q_W:(q + 1) * Q_W].astype(BF16), wct_ref[q])
            ys_ref[:, cols] = y + d_ref[:, cols] * up[:, cols].astype(F32)
        yg, _ = _gelu_parts(ys_ref[...])
        gate = jax.nn.sigmoid(_dot(yg.astype(BF16), gw_ref[...]) + gb_ref[...])
        zp = _dot(pm_ref[...], z_ref[...])
        silu_z, _ = _silu_parts(zp)
        y_ref[...] = _dot(pmt_ref[...], (yg * gate * silu_z).astype(BF16)).astype(BF16)

        @pl.when(pl.program_id(0) == nblk - 1)
        def _():
            _hosted_copies(items, src_refs, out_refs, *sems, act="wait")

    return _pcall(body, name="ssm_fwd", grid=(nblk,),
                  out_shape=(jax.ShapeDtypeStruct((rows, D), BF16), jax.ShapeDtypeStruct((rows, D), F32),
                             jax.ShapeDtypeStruct((nblk, 1, N_STATE), F32),
                             jax.ShapeDtypeStruct((rows, N_STATE), F32),
                             *[jax.ShapeDtypeStruct((D, D), BF16)] * n_sh),
                  in_specs=[pl.BlockSpec((tb, D), lambda i: (i, 2)), pl.BlockSpec((tb, D), lambda i: (i, 3)),
                            _full((tb, tb)), _full((tb, tb)),
                            _full((N_Q, 256, Q_W), single=True), _full((N_Q, 256, Q_W), single=True),
                            _full((k_steps, N_STATE)), _full((1, D)), _full((D, D), single=True), _full((1, D))] +
                           [ANY] * n_sh,
                  out_specs=(pl.BlockSpec((tb, D), lambda i: (i, 0)), pl.BlockSpec((tb, D), lambda i: (i, 0)),
                             pl.BlockSpec((None, 1, N_STATE), lambda i: (i, 0, 0)),
                             pl.BlockSpec((tb, N_STATE), lambda i: (i, 0)), *([ANY] * n_sh)),
                  scratch_shapes=[pltpu.VMEM((1, N_STATE), F32),
                                  pltpu.VMEM((SUBLANES, N_STATE), F32), pltpu.VMEM((SUBLANES, N_STATE), F32)] +
                                 _sem_scratch(items),
                  compiler_params=_params(("arbitrary",)))(proj, proj, pm, pmt, wb, wct, ptab, dvec, glu_w, glu_b,
                                                           *shards)


def _head(x, target, proj, y_pool, y_ssm, mod3, norm_post, wbp, wbs, wout):
    rows = x.shape[0]
    tb = _tb(rows, 256)
    nblk = rows // tb
    n_feat = float(D)

    def body(x_ref, t_ref, gp_ref, gs_ref, yp_ref, ys_ref, mod_ref, npost_ref, wbp_ref, wbs_ref, wout_ref,
             loss_ref, dy_ref, dyp_ref, dys_ref, dg_ref, dwbp_hbm, dwbs_hbm, dwout_hbm, vec_ref,
             acc_bp, acc_bs, acc_out, acc_loss, acc_vec):
        i = pl.program_id(0)

        @pl.when(i == 0)
        def _():
            acc_bp[...] = jnp.zeros_like(acc_bp)
            acc_bs[...] = jnp.zeros_like(acc_bs)
            acc_out[...] = jnp.zeros_like(acc_out)
            acc_loss[...] = jnp.zeros_like(acc_loss)
            acc_vec[...] = jnp.zeros_like(acc_vec)

        yp, ys = yp_ref[...], ys_ref[...]
        sgp = jax.nn.sigmoid(gp_ref[...].astype(F32))
        sgs = jax.nn.sigmoid(gs_ref[...].astype(F32))
        pb = _dot(yp, wbp_ref[...])
        psm = _dot(ys, wbs_ref[...])
        mb = (sgp * pb + sgs * psm).astype(BF16)
        out = _dot(mb, wout_ref[...])
        on, r = _rms_parts(out)
        gate = mod_ref[2:3, :]
        npost = npost_ref[...]
        normed = on * npost
        diff = x_ref[...] + gate * normed - t_ref[...]
        acc_loss[...] += jnp.sum(diff * diff, axis=0, keepdims=True)
        dy = diff * (1.0 / n_feat)
        dy_ref[...] = dy
        acc_vec[0:1, :] += jnp.sum(dy * normed, axis=0, keepdims=True)
        dn = dy * gate
        acc_vec[1:2, :] += jnp.sum(dn * on, axis=0, keepdims=True)
        dout = _rms_bwd(dn * npost, on, r).astype(BF16)
        acc_out[...] += _dot_tn(mb, dout)
        dm = _dot_nt(dout, wout_ref[...])
        dpb = (dm * sgp).astype(BF16)
        dps = (dm * sgs).astype(BF16)
        dg_ref[:, :D] = (dm * pb * sgp * (1.0 - sgp)).astype(BF16)
        dg_ref[:, D:] = (dm * psm * sgs * (1.0 - sgs)).astype(BF16)
        acc_bp[...] += _dot_tn(yp, dpb)
        acc_bs[...] += _dot_tn(ys, dps)
        dyp_ref[...] = _dot_nt(dpb, wbp_ref[...]).astype(BF16)
        dys_ref[...] = _dot_nt(dps, wbs_ref[...]).astype(BF16)

        @pl.when(i == nblk - 1)
        def _():
            loss_ref[...] = 0.5 / n_feat * jnp.sum(acc_loss[...], axis=1, keepdims=True)
            vec_ref[...] = acc_vec[...]
            pltpu.sync_copy(acc_bp, dwbp_hbm)
            pltpu.sync_copy(acc_bs, dwbs_hbm)
            pltpu.sync_copy(acc_out, dwout_hbm)

    row = lambda c: pl.BlockSpec((tb, D), lambda i: (i, c))
    w = _full((D, D), single=True)
    return _pcall(body, name="head", grid=(nblk,),
                  out_shape=(jax.ShapeDtypeStruct((1, 1), F32), jax.ShapeDtypeStruct((rows, D), F32),
                             jax.ShapeDtypeStruct((rows, D), BF16), jax.ShapeDtypeStruct((rows, D), BF16),
                             jax.ShapeDtypeStruct((rows, 2 * D), BF16),
                             jax.ShapeDtypeStruct((D, D), F32), jax.ShapeDtypeStruct((D, D), F32),
                             jax.ShapeDtypeStruct((D, D), F32), jax.ShapeDtypeStruct((2, D), F32)),
                  in_specs=[row(0), row(0), row(4), row(5), row(0), row(0), _full((3, D)), _full((1, D)), w, w, w],
                  out_specs=(_full((1, 1)), row(0), row(0), row(0), pl.BlockSpec((tb, 2 * D), lambda i: (i, 0)),
                             ANY, ANY, ANY, _full((2, D))),
                  scratch_shapes=[pltpu.VMEM((D, D), F32), pltpu.VMEM((D, D), F32), pltpu.VMEM((D, D), F32),
                                  pltpu.VMEM((1, D), F32), pltpu.VMEM((2, D), F32)],
                  compiler_params=_params(("arbitrary",)))(x, target, proj, proj, y_pool, y_ssm, mod3, norm_post,
                                                           wbp, wbs, wout)


def _glu_bwd(dys, proj, ys_pre, pm, pmt, glu_w, glu_b):
    rows = dys.shape[0]
    tb = pm.shape[0]
    nblk = rows // tb

    def body(dys_ref, z_ref, ysp_ref, pm_ref, pmt_ref, gw_ref, gb_ref, dyp_ref, dz_ref, dgw_hbm, dgb_ref,
             acc_w, acc_b):
        i = pl.program_id(0)

        @pl.when(i == 0)
        def _():
            acc_w[...] = jnp.zeros_like(acc_w)
            acc_b[...] = jnp.zeros_like(acc_b)

        d_out = _dot(pm_ref[...], dys_ref[...])
        z = _dot(pm_ref[...], z_ref[...])
        yg, dgelu = _gelu_parts(ysp_ref[...])
        ygb = yg.astype(BF16)
        sg = jax.nn.sigmoid(_dot(ygb, gw_ref[...]) + gb_ref[...])
        silu_z, dsilu_z = _silu_parts(z)
        dz = d_out * (yg * sg) * dsilu_z
        dz_ref[...] = _dot(pmt_ref[...], dz.astype(BF16)).astype(BF16)
        dglu = d_out * silu_z
        dq = dglu * yg * sg * (1.0 - sg)
        dqb = dq.astype(BF16)
        acc_b[...] += jnp.sum(dq, axis=0, keepdims=True)
        acc_w[...] += _dot_tn(ygb, dqb)
        dyg = dglu * sg + _dot_nt(dqb, gw_ref[...])
        dyp_ref[...] = (dyg * dgelu).astype(BF16)

        @pl.when(i == nblk - 1)
        def _():
            dgb_ref[...] = acc_b[...]
            pltpu.sync_copy(acc_w, dgw_hbm)

    row = lambda c: pl.BlockSpec((tb, D), lambda i: (i, c))
    return _pcall(body, name="glu_bwd", grid=(nblk,),
                  out_shape=(jax.ShapeDtypeStruct((rows, D), BF16), jax.ShapeDtypeStruct((rows, D), BF16),
                             jax.ShapeDtypeStruct((D, D), F32), jax.ShapeDtypeStruct((1, D), F32)),
                  in_specs=[row(0), row(3), row(0), _full((tb, tb)), _full((tb, tb)),
                            _full((D, D), single=True), _full((1, D))],
                  out_specs=(row(0), row(0), ANY, _full((1, D))),
                  scratch_shapes=[pltpu.VMEM((D, D), F32), pltpu.VMEM((1, D), F32)],
                  compiler_params=_params(("arbitrary",)))(dys, proj, ys_pre, pm, pmt, glu_w, glu_b)


def _ssm_bwd(dyp, proj, states, carries, pm, pmt, wb, wct, ptab, dvec, mat_grads, dpool_w, dw_in_rest):
    rows = dyp.shape[0]
    tb = pm.shape[0]
    k_steps = tb // SUBLANES
    nblk = rows // tb
    n_mat = len(mat_grads)
    hosted = [*mat_grads, dpool_w, dw_in_rest]
    n_h = len(hosted)
    shard_rows = D // N_DEV
    pool_rows = dpool_w.shape[1] // N_DEV
    items = [_scatter_item(t, t, _rows_of(shard_rows)) for t in range(n_mat)] + \
            [_scatter_item(n_mat, n_mat, _pool_rows_of(pool_rows))] + \
            [_w_in_block_item(n_mat + 1, n_mat + 1, j, ssm_part=False) for j in range(W_IN_SHARD // W_IN_BLOCK)]
    n_in, n_out = 10, 5

    def body(*refs):
        dyp_ref, u_ref, s_ref, cin_ref, pm_ref, pmt_ref, wb_ref, wct_ref, p_ref, d_ref = refs[:n_in]
        src_refs = refs[n_in:n_in + n_h]
        du_ref, dbb_ref, dcc_ref, da_ref, dd_ref = refs[n_in + n_h:n_in + n_h + n_out]
        recv_refs = refs[n_in + n_h + n_out:n_in + 2 * n_h + n_out]
        (g_ref, carry_b, fin_ref, acc_wb, acc_wct, acc_da, acc_dd, dup_ref,
         *sems) = refs[n_in + 2 * n_h + n_out:]
        i = pl.program_id(0)

        @pl.when(i == 0)
        def _():
            _hosted_copies(items, src_refs, recv_refs, *sems, act="start")
            carry_b[...] = jnp.zeros_like(carry_b)
            acc_wb[...] = jnp.zeros_like(acc_wb)
            acc_wct[...] = jnp.zeros_like(acc_wct)
            acc_da[...] = jnp.zeros_like(acc_da)
            acc_dd[...] = jnp.zeros_like(acc_dd)

        dy = dyp_ref[...]
        up = _dot(pm_ref[...], u_ref[...]).astype(BF16)
        acc_dd[...] += jnp.sum(dy.astype(F32) * up.astype(F32), axis=0, keepdims=True)
        for q in range(N_Q):
            cols = slice(q * 256, (q + 1) * 256)
            g_ref[:, q * Q_W:(q + 1) * Q_W] = _dot(dy[:, cols], wct_ref[q])
            acc_wct[q] += _dot_tn(dy[:, cols], s_ref[:, q * Q_W:(q + 1) * Q_W].astype(BF16))
        for q in range(N_Q):
            _scan_backward(q, g_ref, s_ref, p_ref, carry_b, cin_ref, fin_ref, acc_da, k_steps)
        for q in range(N_Q):
            cols = slice(q * 256, (q + 1) * 256)
            lam = g_ref[:, q * Q_W:(q + 1) * Q_W].astype(BF16)
            acc_wb[q] += _dot_tn(up[:, cols], lam)
            dup_ref[:, cols] = (_dot_nt(lam, wb_ref[q]) + d_ref[:, cols] * dy[:, cols].astype(F32)).astype(BF16)
        du_ref[...] = _dot(pmt_ref[...], dup_ref[...]).astype(BF16)

        @pl.when(i == nblk - 1)
        def _():
            da_ref[...] = acc_da[...]
            dd_ref[...] = acc_dd[...]
            spread, own = _group_masks()
            spread = spread.astype(F32)
            for acc, out in ((acc_wb, dbb_ref), (acc_wct, dcc_ref)):
                for half in range(2):
                    for q in range(N_Q):
                        kept = jnp.where(own, acc[q, :, half * (Q_W // 2):(half + 1) * (Q_W // 2)], 0.0)
                        out[half, q] = lax.dot_general(kept, spread, (((1,), (1,)), ((), ())),
                                                       preferred_element_type=F32, precision=lax.Precision.HIGHEST)
            _hosted_copies(items, src_refs, recv_refs, *sems, act="wait")

    rev = lambda c: pl.BlockSpec((tb, D), lambda i: (nblk - 1 - i, c))
    recv = [jax.ShapeDtypeStruct((N_DEV, shard_rows, D), F32)] * n_mat + \
           [jax.ShapeDtypeStruct((N_DEV, dpool_w.shape[0], pool_rows, dpool_w.shape[2]), F32),
            jax.ShapeDtypeStruct((N_DEV, D, W_IN_SHARD), BF16)]
    return _pcall(body, name="ssm_bwd", grid=(nblk,),
                  out_shape=(jax.ShapeDtypeStruct((rows, D), BF16),
                             jax.ShapeDtypeStruct((2, N_Q, 16 * G_H, G_P), F32),
                             jax.ShapeDtypeStruct((2, N_Q, 16 * G_H, G_P), F32),
                             jax.ShapeDtypeStruct((1, N_STATE), F32), jax.ShapeDtypeStruct((1, D), F32), *recv),
                  in_specs=[rev(0), rev(2), pl.BlockSpec((tb, N_STATE), lambda i: (nblk - 1 - i, 0)),
                            pl.BlockSpec((None, 1, N_STATE), lambda i: (nblk - 1 - i, 0, 0)),
                            _full((tb, tb)), _full((tb, tb)),
                            _full((N_Q, 256, Q_W), single=True), _full((N_Q, 256, Q_W), single=True),
                            _full((k_steps, N_STATE)), _full((1, D))] + [ANY] * n_h,
                  out_specs=(rev(0), _full((2, N_Q, 16 * G_H, G_P)), _full((2, N_Q, 16 * G_H, G_P)),
                             _full((1, N_STATE)), _full((1, D)), *([ANY] * n_h)),
                  scratch_shapes=[pltpu.VMEM((tb, N_STATE), F32), pltpu.VMEM((1, N_STATE), F32),
                                  pltpu.VMEM((SUBLANES, N_STATE), F32),
                                  pltpu.VMEM((N_Q, 256, Q_W), F32), pltpu.VMEM((N_Q, 256, Q_W), F32),
                                  pltpu.VMEM((1, N_STATE), F32), pltpu.VMEM((1, D), F32),
                                  pltpu.VMEM((tb, D), BF16)] + _sem_scratch(items),
                  compiler_params=_params(("arbitrary",), vmem=60 * 1024 * 1024),
                  )(dyp, proj, states, carries, pm, pmt, wb, wct, ptab, dvec, *hosted)


def _pool_bwd(dyp, proj, pool_w, pool_scale):
    rows = dyp.shape[0]
    tb = _tb(rows, 512)
    nblk = rows // tb
    hb = tb // HALO

    def body(dy_ref, u_ref, halo_ref, z_ref, pw_ref, ps_ref, dp_ref, dpw_ref, dps_ref, ahead_ref):
        i = pl.program_id(0)
        blk = nblk - 1 - i

        @pl.when(i == 0)
        def _():
            ahead_ref[...] = jnp.zeros_like(ahead_ref)
            dpw_ref[...] = jnp.zeros_like(dpw_ref)
            dps_ref[...] = jnp.zeros_like(dps_ref)

        u = u_ref[...].astype(F32)
        halo = jnp.where(blk > 0, halo_ref[...].astype(F32), 0.0)
        pooled, counts = _pool_windows(jnp.concatenate([halo, u], axis=0), tb, blk * tb)
        silu_z, dsilu_z = _silu_parts(z_ref[...].astype(F32))
        dy = dy_ref[...].astype(F32)
        for g, w in enumerate(POOL_WINDOWS):
            cols = slice(g * 256, (g + 1) * 256)
            pooled_b = pooled[g].astype(BF16)
            mixed = _dot(pooled_b, pw_ref[g])
            scale = ps_ref[:, cols]
            dp_ref[:, D + g * 256:D + (g + 1) * 256] = (dy[:, cols] * (mixed * scale) * dsilu_z[:, cols]).astype(BF16)
            dms = dy[:, cols] * silu_z[:, cols]
            dps_ref[:, cols] += jnp.sum(dms * mixed, axis=0, keepdims=True)
            dmixed = (dms * scale).astype(BF16)
            dpw_ref[g] += _dot_tn(pooled_b, dmixed)
            dpooled = _dot_nt(dmixed, pw_ref[g])
            ratio = dpooled / counts[g]
            acc = jnp.concatenate([ratio, ahead_ref[:, cols]], axis=0)
            ahead_ref[:, cols] = ratio[:HALO, :]
            s = 1
            while s < w:
                acc = acc + pltpu.roll(acc, tb + HALO - s, axis=0)
                s *= 2
            dp_ref[:, cols] = (acc[:tb, :] - dpooled).astype(BF16)

    rev = lambda c: pl.BlockSpec((tb, D), lambda i: (nblk - 1 - i, c))
    return _pcall(body, name="pool_bwd", grid=(nblk,),
                  out_shape=(jax.ShapeDtypeStruct((rows, 2 * D), BF16), jax.ShapeDtypeStruct((4, 256, 256), F32),
                             jax.ShapeDtypeStruct((1, D), F32)),
                  in_specs=[rev(0), rev(0),
                            pl.BlockSpec((HALO, D), lambda i: (jnp.maximum((nblk - 1 - i) * hb - 1, 0), 0)),
                            rev(1), _full((4, 256, 256)), _full((1, D))],
                  out_specs=(pl.BlockSpec((tb, 2 * D), lambda i: (nblk - 1 - i, 0)), _full((4, 256, 256)),
                             _full((1, D))),
                  scratch_shapes=[pltpu.VMEM((HALO, D), F32)],
                  compiler_params=_params(("arbitrary",)))(dyp, proj, proj, proj, pool_w, pool_scale)


def _dproj_specs(tb):
    return [pl.BlockSpec((tb, 2 * D), lambda i: (i, 0)), pl.BlockSpec((tb, D), lambda i: (i, 0)),
            pl.BlockSpec((tb, D), lambda i: (i, 0)), pl.BlockSpec((tb, 2 * D), lambda i: (i, 0))]


def _in_proj_bwd_x(x, dy, dpp, dus, dzs, dpg, mod3, norm_pre, w_in, dw_in_ssm, small32, small16, recv_w_in):
    rows = x.shape[0]
    tb = _tb(rows, 256)
    nblk = rows // tb
    items = [_w_in_block_item(0, 0, j, ssm_part=True) for j in range(W_IN_SHARD // W_IN_BLOCK)] + \
            [_Item(1, 1, _whole, _slot), _Item(2, 2, _whole, _slot)]
    sums_item = [_Item(0, 0, _whole, _slot)]

    def body(x_ref, dy_ref, dpp_ref, dus_ref, dzs_ref, dpg_ref, mod_ref, np_ref, w_ref,
             dw_src, s32_src, s16_src, _, gx_ref, recv_w, recv32, recv16, recv_sums,
             vec_ref, ssem, rsem, lsem, *sums_sems):
        src_refs, recv_refs, sems = (dw_src, s32_src, s16_src), (recv_w, recv32, recv16), (ssem, rsem, lsem)

        @pl.when(pl.program_id(0) == 0)
        def _():
            _hosted_copies(items, src_refs, recv_refs, *sems, act="start")
            vec_ref[...] = jnp.zeros_like(vec_ref)

        dh = _dot_nt(dpp_ref[...], w_ref[:, 0:2 * D])
        dh += _dot_nt(dus_ref[...], w_ref[:, 2 * D:3 * D])
        dh += _dot_nt(dzs_ref[...], w_ref[:, 3 * D:4 * D])
        dh += _dot_nt(dpg_ref[...], w_ref[:, 4 * D:6 * D])
        xn, r, _ = _prenorm(x_ref[...], mod_ref[...], np_ref[...])
        one_scale = 1.0 + mod_ref[1:2, :]
        vec_ref[0:1, :] += jnp.sum(dh, axis=0, keepdims=True)
        vec_ref[1:2, :] += jnp.sum(dh * xn, axis=0, keepdims=True) * np_ref[...]
        vec_ref[2:3, :] += jnp.sum(dh * xn, axis=0, keepdims=True) * one_scale
        gx_ref[...] = dy_ref[...] + _rms_bwd(dh * (np_ref[...] * one_scale), xn, r)

        @pl.when(pl.program_id(0) == nblk - 1)
        def _():
            _hosted_copies(sums_item, (vec_ref,), (recv_sums,), *sums_sems, act="start")
            _hosted_copies(items, src_refs, recv_refs, *sems, act="wait")
            _hosted_copies(sums_item, (vec_ref,), (recv_sums,), *sums_sems, act="wait")

    row = pl.BlockSpec((tb, D), lambda i: (i, 0))
    recv = (jax.ShapeDtypeStruct(recv_w_in.shape, recv_w_in.dtype),
            jax.ShapeDtypeStruct((N_DEV,) + small32.shape, small32.dtype),
            jax.ShapeDtypeStruct((N_DEV,) + small16.shape, small16.dtype),
            jax.ShapeDtypeStruct((N_DEV, 3, D), F32))
    return _pcall(body, name="in_proj_bwd_x", grid=(nblk,),
                  out_shape=(jax.ShapeDtypeStruct((rows, D), F32), *recv),
                  in_specs=[row, row] + _dproj_specs(tb) + [_full((3, D)), _full((1, D)),
                                                            _full((D, N_IN), single=True)] + [ANY] * 4,
                  out_specs=(row, ANY, ANY, ANY, ANY),
                  input_output_aliases={12: 1},
                  scratch_shapes=[pltpu.VMEM((3, D), F32)] + _sem_scratch(items) + _sem_scratch(sums_item),
                  compiler_params=_params(("arbitrary",)))(x, dy, dpp, dus, dzs, dpg, mod3, norm_pre, w_in,
                                                           dw_in_ssm, small32, small16, recv_w_in)


def _in_proj_bwd_w(name, x, dparts, mod3, norm_pre):
    rows = x.shape[0]
    tb = _tb(rows, 256)
    nblk = rows // tb
    widths = [p.shape[1] for p in dparts]
    n_p = len(dparts)

    def body(x_ref, *rest):
        part_refs, (mod_ref, np_ref, dw_ref, acc) = rest[:n_p], rest[n_p:]
        i = pl.program_id(0)

        @pl.when(i == 0)
        def _():
            acc[...] = jnp.zeros_like(acc)

        _, _, h = _prenorm(x_ref[...], mod_ref[...], np_ref[...])
        ht = h.astype(BF16)
        lo = 0
        for ref, w in zip(part_refs, widths):
            acc[:, lo:lo + w] += _dot_tn(ht, ref[...])
            lo += w

        @pl.when(i == nblk - 1)
        def _():
            dw_ref[...] = acc[...].astype(BF16)

    row = pl.BlockSpec((tb, D), lambda i: (i, 0))
    return _pcall(body, name=name, grid=(nblk,),
                  out_shape=jax.ShapeDtypeStruct((D, sum(widths)), BF16),
                  in_specs=[row] + [pl.BlockSpec((tb, w), lambda i: (i, 0)) for w in widths] +
                           [_full((3, D)), _full((1, D))],
                  out_specs=_full((D, sum(widths))),
                  scratch_shapes=[pltpu.VMEM((D, sum(widths)), F32)],
                  compiler_params=_params(("arbitrary",)))(x, *dparts, mod3, norm_pre)


def _adamw_math(w, g, m, v):
    m = ADAM_B1 * m + (1.0 - ADAM_B1) * g
    v = ADAM_B2 * v + (1.0 - ADAM_B2) * (g * g)
    m_hat = m / (1.0 - ADAM_B1 ** ADAM_STEP)
    v_hat = v / (1.0 - ADAM_B2 ** ADAM_STEP)
    delta = -ADAM_LR * (m_hat / (jnp.sqrt(v_hat) + ADAM_EPS) + ADAM_WD * w)
    return delta, m, v


def _sum_sources(ref):
    g = ref[0].astype(F32)
    for s in range(1, N_DEV):
        g = g + ref[s].astype(F32)
    return g


def _adamw_reduce(name, parts, w, m, v):
    r, c = w.shape
    tr = r if r * c <= 256 * 1024 else max(8, (256 * 1024 // c) // 8 * 8)
    while r % tr:
        tr -= 8

    def body(p_ref, w_ref, m_ref, v_ref, g_ref, d_ref, nm_ref, nv_ref):
        g = _sum_sources(p_ref)
        g_ref[...] = g
        d_ref[...], nm_ref[...], nv_ref[...] = _adamw_math(w_ref[...], g, m_ref[...], v_ref[...])

    blk = pl.BlockSpec((tr, c), lambda i: (i, 0))
    return _pcall(body, name=name, grid=(r // tr,),
                  out_shape=tuple([jax.ShapeDtypeStruct((r, c), F32)] * 4),
                  in_specs=[pl.BlockSpec((N_DEV, tr, c), lambda i: (0, i, 0)), blk, blk, blk],
                  out_specs=(blk, blk, blk, blk),
                  compiler_params=_params(("arbitrary",)))(parts, w, m, v)


def _adamw_small(gs, ws, ms, vs):
    n = len(gs)

    def body(*refs):
        ins, outs = refs[:4 * n], refs[4 * n:]
        for t in range(n):
            g_ref, w_ref, m_ref, v_ref = ins[4 * t:4 * t + 4]
            outs[3 * t][...], outs[3 * t + 1][...], outs[3 * t + 2][...] = _adamw_math(
                w_ref[...], g_ref[...], m_ref[...], v_ref[...])

    vm = pl.BlockSpec(memory_space=pltpu.VMEM)
    flat = [a for t in range(n) for a in (gs[t], ws[t], ms[t], vs[t])]
    return _pcall(body, name="adamw_small",
                  out_shape=tuple(jax.ShapeDtypeStruct(w.shape, F32) for w in ws for _ in range(3)),
                  in_specs=[vm] * (4 * n), out_specs=tuple([vm] * (3 * n)), compiler_params=_params())(*flat)


def _sum_small(parts):
    n = len(parts)

    def body(*refs):
        for t in range(n):
            refs[n + t][...] = _sum_sources(refs[t])

    vm = pl.BlockSpec(memory_space=pltpu.VMEM)
    return _pcall(body, name="sum_small",
                  out_shape=tuple(jax.ShapeDtypeStruct(p.shape[1:], F32) for p in parts),
                  in_specs=[vm] * n, out_specs=tuple([vm] * n), compiler_params=_params())(*parts)


def _ada_update(c_all, dmod_cols, w, m, v):
    def body(c_ref, dm_ref, w_ref, m_ref, v_ref, g_ref, d_ref, nm_ref, nv_ref):
        ca = c_ref[...]
        g = lax.dot_general(ca * jax.nn.sigmoid(ca), dm_ref[...], (((0,), (0,)), ((), ())),
                            preferred_element_type=F32, precision=lax.Precision.HIGHEST)
        g_ref[...] = g
        d_ref[...], nm_ref[...], nv_ref[...] = _adamw_math(w_ref[...], g, m_ref[...], v_ref[...])

    vm = pl.BlockSpec(memory_space=pltpu.VMEM)
    return _pcall(body, name="ada_update", out_shape=tuple([jax.ShapeDtypeStruct(w.shape, F32)] * 4),
                  in_specs=[vm] * 5, out_specs=(vm, vm, vm, vm), compiler_params=_params())(c_all, dmod_cols, w, m, v)


def kernel(x, c, w_ada, b_ada, norm_pre, norm_post, w_in, pool_w, pool_scale, ssm_a_re, ssm_a_im, ssm_log_dt, ssm_b_re, ssm_b_im, ssm_c_re, ssm_c_im, ssm_d, glu_w, glu_b, w_branch_pool, w_branch_ssm, w_out, loss_target, m_w_ada, m_b_ada, m_norm_pre, m_norm_post, m_w_in, m_pool_w, m_pool_scale, m_ssm_a_re, m_ssm_a_im, m_ssm_log_dt, m_ssm_b_re, m_ssm_b_im, m_ssm_c_re, m_ssm_c_im, m_ssm_d, m_glu_w, m_glu_b, m_w_branch_pool, m_w_branch_ssm, m_w_out, v_w_ada, v_b_ada, v_norm_pre, v_norm_post, v_w_in, v_pool_w, v_pool_scale, v_ssm_a_re, v_ssm_a_im, v_ssm_log_dt, v_ssm_b_re, v_ssm_b_im, v_ssm_c_re, v_ssm_c_im, v_ssm_d, v_glu_w, v_glu_b, v_w_branch_pool, v_w_branch_ssm, v_w_out):
    given = dict(locals())
    me = _flat(_me())
    rows = x.shape[1]
    x2 = x[0]
    target = loss_target[0]
    ada_cols = w_ada.shape[2]

    b_ada_s = lax.dynamic_slice(b_ada, (0, me * ada_cols), (1, ada_cols))
    c_all, mod_rows = _ada_exchange(c, w_ada[0], b_ada_s)
    mod3 = mod_rows.reshape(3, D)

    shards = _cast_shards([w_in[0], pool_w[0], glu_w[0], w_branch_pool[0], w_branch_ssm[0], w_out[0]])

    tb_ssm = _tb(rows, 256)
    k_steps = tb_ssm // SUBLANES
    a_re, a_im = ssm_a_re[0], ssm_a_im[0]
    log_dt = ssm_log_dt[0].reshape(GROUPS, 1)
    b_re_t, b_im_t = ssm_b_re[0].transpose(0, 2, 1), ssm_b_im[0].transpose(0, 2, 1)
    wb, wct, pow_re, pow_im = _s5_prep(a_re, a_im, log_dt, b_re_t, b_im_t, ssm_c_re[0], ssm_c_im[0], k_steps)
    ptab = _state_layout(pow_re, pow_im)
    dvec = ssm_d[0].reshape(1, D)
    pm = _perm_matrix(tb_ssm)
    pmt = pm.T

    proj, w_in_g, pool_w_g, glu_g = _in_proj(x2, mod3, norm_pre, shards[0], shards[1:3])
    y_pool = _pool_fwd(proj, pool_w_g, pool_scale)
    y_ssm, ys_pre, carries, states, wbp_g, wbs_g, wout_g = _ssm_fwd(
        proj, pm, pmt, wb, wct, ptab, dvec, glu_g, glu_b, shards[3:])
    loss_part, dy, dyp, dys, dpg, dwbp, dwbs, dwout, head_vec = _head(
        x2, target, proj, y_pool, y_ssm, mod3, norm_post, wbp_g, wbs_g, wout_g)

    dpp, dpool_w, dpool_scale = _pool_bwd(dyp, proj, pool_w_g, pool_scale)
    dw_in_rest = _in_proj_bwd_w("in_proj_bwd_w_rest", x2, [dpp, dpg], mod3, norm_pre)
    dy_pre, dzs, dglu_w, dglu_b = _glu_bwd(dys, proj, ys_pre, pm, pmt, glu_g, glu_b)
    dus, dbb, dcc, dabar, dd, p_glu, p_wbp, p_wbs, p_wout, p_pool_w, p_w_in = _ssm_bwd(
        dy_pre, proj, states, carries, pm, pmt, wb, wct, ptab, dvec, [dglu_w, dwbp, dwbs, dwout], dpool_w, dw_in_rest)
    dw_in_ssm = _in_proj_bwd_w("in_proj_bwd_w_ssm", x2, [dus, dzs], mod3, norm_pre)

    small32 = jnp.concatenate([head_vec, dpool_scale, dglu_b, dd, jnp.broadcast_to(loss_part, (1, D)),
                               jnp.zeros((2, D), F32), dabar.reshape(8, D)], axis=0)
    small16 = jnp.concatenate([dbb.reshape(2 * GROUPS, D), dcc.reshape(2 * GROUPS, D)], axis=0).astype(BF16)
    grad_x, p_w_in, p_small32, p_small16, p_pre = _in_proj_bwd_x(
        x2, dy, dpp, dus, dzs, dpg, mod3, norm_pre, w_in_g, dw_in_ssm, small32, small16, p_w_in)

    tot32, tot16, tot_pre = _sum_small([p_small32, p_small16, p_pre])
    d_abar_re, d_abar_im = _state_unlayout(tot32[8:16].reshape(N_STATE))
    d_bb_re, d_bb_im = tot16[0:64].reshape(GROUPS, G_H, G_P), tot16[64:128].reshape(GROUPS, G_H, G_P)
    g_a_re, g_a_im, g_log_dt, g_b_re_t, g_b_im_t = _s5_prep_bwd(
        a_re, a_im, log_dt, b_re_t, b_im_t, d_abar_re, d_abar_im, d_bb_re, d_bb_im)

    grads, deltas, new_m, new_v = {}, {}, {}, {}

    small = []

    def small_update(name, g2):
        small.append((name, g2))

    def shard_update(name, parts):
        shape = given[name].shape
        r2 = parts.shape[1:] if parts.ndim == 3 else (parts.shape[1] * parts.shape[2], parts.shape[3])
        w2, m2, v2 = (given[p + name].reshape(r2) for p in ("", "m_", "v_"))
        out = _adamw_reduce("adamw_" + name, parts.reshape((N_DEV,) + tuple(r2)), w2, m2, v2)
        grads[name], deltas[name], new_m[name], new_v[name] = (a.reshape(shape) for a in out)

    dmod_all = jnp.concatenate([p_pre[:, 0:2, :], p_small32[:, 0:1, :]], axis=1).reshape(N_DEV, 3 * D)
    dmod_cols = lax.dynamic_slice(dmod_all, (0, me * ada_cols), (N_DEV, ada_cols))
    out = _ada_update(c_all, dmod_cols, w_ada[0], m_w_ada[0], v_w_ada[0])
    grads['w_ada'], deltas['w_ada'], new_m['w_ada'], new_v['w_ada'] = (a.reshape(w_ada.shape) for a in out)

    small_update('b_ada', jnp.concatenate([tot_pre[0:2], tot32[0:1]], axis=0).reshape(1, 3 * D))
    small_update('norm_pre', tot_pre[2:3])
    small_update('norm_post', tot32[1:2])
    small_update('pool_scale', tot32[2:3])
    small_update('glu_b', tot32[3:4])
    small_update('ssm_d', tot32[4:5])
    small_update('ssm_a_re', g_a_re)
    small_update('ssm_a_im', g_a_im)
    small_update('ssm_log_dt', g_log_dt.reshape(1, GROUPS))
    small_update('ssm_b_re', g_b_re_t.transpose(0, 2, 1).reshape(GROUPS, G_P * G_H))
    small_update('ssm_b_im', g_b_im_t.transpose(0, 2, 1).reshape(GROUPS, G_P * G_H))
    small_update('ssm_c_re', tot16[128:192])
    small_update('ssm_c_im', -tot16[192:256])
    flat = _adamw_small([g2 for _, g2 in small],
                        *[[given[p + name].reshape(g2.shape) for name, g2 in small] for p in ("", "m_", "v_")])
    for t, (name, g2) in enumerate(small):
        shape = given[name].shape
        grads[name], deltas[name], new_m[name], new_v[name] = (
            a.reshape(shape) for a in (g2, *flat[3 * t:3 * t + 3]))
    shard_update('w_in', p_w_in)
    shard_update('pool_w', p_pool_w)
    shard_update('glu_w', p_glu)
    shard_update('w_branch_pool', p_wbp)
    shard_update('w_branch_ssm', p_wbs)
    shard_update('w_out', p_wout)

    return (tot32[5, 0], grad_x[None], *[grads[n] for n in WEIGHTS], *[deltas[n] for n in WEIGHTS],
            *[new_m[n] for n in WEIGHTS], *[new_v[n] for n in WEIGHTS])
```

```python
import functools
import math
from typing import Callable, NamedTuple, Optional

import jax
import jax.numpy as jnp
from jax import lax
from jax.experimental import pallas as pl
from jax.experimental.pallas import tpu as pltpu

F32 = jnp.float32
BF16 = jnp.bfloat16
MESH = pl.DeviceIdType.MESH

D = 1024
N_DEV = 8
N_IN = 6 * D
GROUPS = 64
G_H = 16
G_P = 64
N_Q = 4
Q_W = 2 * 16 * G_P
N_STATE = N_Q * Q_W
POOL_WINDOWS = (2, 4, 8, 16)
HALO = 16
RMS_EPS = 1e-6
SUBLANES = 8
LANE_CHUNK = 512
SCAN_UNROLL = 2
VMEM_LIMIT = 56 * 1024 * 1024

ADAM_LR = 0.001
ADAM_B1 = 0.9
ADAM_B2 = 0.999
ADAM_EPS = 1e-08
ADAM_WD = 0.01
ADAM_STEP = 10

WEIGHTS = ['w_ada', 'b_ada', 'norm_pre', 'norm_post', 'w_in', 'pool_w', 'pool_scale', 'ssm_a_re',
           'ssm_a_im', 'ssm_log_dt', 'ssm_b_re', 'ssm_b_im', 'ssm_c_re', 'ssm_c_im', 'ssm_d', 'glu_w',
           'glu_b', 'w_branch_pool', 'w_branch_ssm', 'w_out']


def _pcall(body, **kw):
    return pl.pallas_call(body, **kw)


def _params(sem=None, vmem=VMEM_LIMIT):
    return pltpu.CompilerParams(dimension_semantics=sem, vmem_limit_bytes=vmem)


def _tb(rows, pref):
    return pref if rows % pref == 0 and rows // pref >= 2 else rows // 2


def _full(shape, single=False):
    nd = len(shape)
    if single:
        return pl.BlockSpec(shape, lambda i: (0,) * nd, pipeline_mode=pl.Buffered(1))
    return pl.BlockSpec(shape, lambda i: (0,) * nd)


ANY = pl.BlockSpec(memory_space=pl.ANY)


def _me():
    return lax.axis_index("x"), lax.axis_index("y"), lax.axis_index("c")


def _flat(p):
    return 4 * p[0] + 2 * p[1] + p[2]


def _peer(k):
    x, y, c = _me()
    return (1 - x if k & 4 else x, 1 - y if k & 2 else y, 1 - c if k & 1 else c)


def _silu_parts(z):
    s = jax.nn.sigmoid(z)
    return z * s, s * (1.0 + z * (1.0 - s))


_GELU_C = math.sqrt(2.0 / math.pi)


def _gelu_parts(x):
    x2 = x * x
    t = jnp.tanh(_GELU_C * (x + 0.044715 * x * x2))
    g = 0.5 * x * (1.0 + t)
    dg = 0.5 * (1.0 + t) + 0.5 * x * (1.0 - t * t) * (_GELU_C * (1.0 + 3.0 * 0.044715 * x2))
    return g, dg


def _dot(a, b):
    return jnp.dot(a, b, preferred_element_type=F32)


def _dot_nt(a, b):
    return lax.dot_general(a, b, (((1,), (1,)), ((), ())), preferred_element_type=F32)


def _dot_tn(a, b):
    return lax.dot_general(a, b, (((0,), (0,)), ((), ())), preferred_element_type=F32)


def _rms_parts(x):
    r = lax.rsqrt(jnp.mean(x * x, axis=-1, keepdims=True) + RMS_EPS)
    return x * r, r


def _rms_bwd(dxn, xn, r):
    return r * (dxn - xn * jnp.mean(dxn * xn, axis=-1, keepdims=True))


def _ada_exchange(c, w_ada_s, b_ada_s):
    cols = w_ada_s.shape[1]

    def body(c_ref, w_ref, b_ref, call_ref, mod_ref, part_ref, ssem, rsem, lsem):
        me3 = _me()
        me = _flat(me3)
        mine = pltpu.make_async_copy(c_ref, call_ref.at[pl.ds(me, 1), :], lsem.at[0])
        mine.start()
        sends = []
        for k in range(1, N_DEV):
            cp = pltpu.make_async_remote_copy(src_ref=c_ref, dst_ref=call_ref.at[pl.ds(me, 1), :],
                                              send_sem=ssem.at[k - 1], recv_sem=rsem.at[k - 1],
                                              device_id=_peer(k), device_id_type=MESH)
            cp.start()
            sends.append(cp)
        mine.wait()
        for k in range(1, N_DEV):
            p = _flat(_peer(k))
            pltpu.make_async_remote_copy(src_ref=c_ref, dst_ref=call_ref.at[pl.ds(p, 1), :],
                                         send_sem=ssem.at[k - 1], recv_sem=rsem.at[k - 1],
                                         device_id=_peer(k), device_id_type=MESH).wait_recv()
        for cp in sends:
            cp.wait_send()
        ca = call_ref[...]
        act = ca * jax.nn.sigmoid(ca)
        part_ref[...] = jnp.dot(act, w_ref[...], preferred_element_type=F32,
                                precision=lax.Precision.HIGHEST) + b_ref[...]
        own = pltpu.make_async_copy(part_ref.at[pl.ds(me, 1), :], mod_ref.at[pl.ds(me, 1), :], lsem.at[1])
        own.start()
        sends = []
        for k in range(1, N_DEV):
            p = _flat(_peer(k))
            s = N_DEV - 1 + k - 1
            cp = pltpu.make_async_remote_copy(src_ref=part_ref.at[pl.ds(p, 1), :],
                                              dst_ref=mod_ref.at[pl.ds(me, 1), :],
                                              send_sem=ssem.at[s], recv_sem=rsem.at[s],
                                              device_id=_peer(k), device_id_type=MESH)
            cp.start()
            sends.append(cp)
        own.wait()
        for k in range(1, N_DEV):
            p = _flat(_peer(k))
            s = N_DEV - 1 + k - 1
            pltpu.make_async_remote_copy(src_ref=part_ref.at[pl.ds(p, 1), :],
                                         dst_ref=mod_ref.at[pl.ds(p, 1), :],
                                         send_sem=ssem.at[s], recv_sem=rsem.at[s],
                                         device_id=_peer(k), device_id_type=MESH).wait_recv()
        for cp in sends:
            cp.wait_send()

    vm = pl.BlockSpec(memory_space=pltpu.VMEM)
    return _pcall(
        body, name="ada_exchange",
        out_shape=(jax.ShapeDtypeStruct((N_DEV, D), F32), jax.ShapeDtypeStruct((N_DEV, cols), F32)),
        in_specs=[vm, vm, vm], out_specs=(vm, vm),
        scratch_shapes=[pltpu.VMEM((N_DEV, cols), F32),
                        pltpu.SemaphoreType.DMA((2 * (N_DEV - 1),)),
                        pltpu.SemaphoreType.DMA((2 * (N_DEV - 1),)),
                        pltpu.SemaphoreType.DMA((2,))],
    )(c, w_ada_s, b_ada_s)


class _Item(NamedTuple):
    src: int
    out: int
    src_view: Callable
    dst_view: Callable
    pred: Optional[Callable] = None


def _when(pred, dest, fn):
    if pred is None:
        fn()
    else:
        pl.when(pred(dest))(fn)


def _n_sems(items):
    return len(items) * (N_DEV - 1)


def _hosted_copies(items, srcs, outs, ssem, rsem, lsem, act):
    me = _flat(_me())
    for t, it in enumerate(items):
        local = lambda t=t, it=it: pltpu.make_async_copy(
            it.src_view(srcs[it.src], me), it.dst_view(outs[it.out], me), lsem.at[t])
        if act == "start":
            _when(it.pred, me, lambda local=local: local().start())
        else:
            _when(it.pred, me, lambda local=local: local().wait())
    for k in range(1, N_DEV):
        p3 = _peer(k)
        p = _flat(p3)
        for t, it in enumerate(items):
            s = t * (N_DEV - 1) + k - 1
            send = lambda it=it, s=s, p=p, p3=p3: pltpu.make_async_remote_copy(
                src_ref=it.src_view(srcs[it.src], p), dst_ref=it.dst_view(outs[it.out], me),
                send_sem=ssem.at[s], recv_sem=rsem.at[s], device_id=p3, device_id_type=MESH)
            recv = lambda it=it, s=s, p=p, p3=p3: pltpu.make_async_remote_copy(
                src_ref=it.src_view(srcs[it.src], p), dst_ref=it.dst_view(outs[it.out], p),
                send_sem=ssem.at[s], recv_sem=rsem.at[s], device_id=p3, device_id_type=MESH)
            if act == "start":
                _when(it.pred, p, lambda send=send: send().start())
            else:
                _when(it.pred, me, lambda recv=recv: recv().wait_recv())
                _when(it.pred, p, lambda send=send: send().wait_send())


def _sem_scratch(items):
    return [pltpu.SemaphoreType.DMA((_n_sems(items),)), pltpu.SemaphoreType.DMA((_n_sems(items),)),
            pltpu.SemaphoreType.DMA((len(items),))]


def _exchange(name, srcs, out_structs, items):
    n_src, n_out = len(srcs), len(out_structs)

    def body(*refs):
        src_refs, out_refs = refs[:n_src], refs[n_src:n_src + n_out]
        sems = refs[n_src + n_out:]
        _hosted_copies(items, src_refs, out_refs, *sems, act="start")
        _hosted_copies(items, src_refs, out_refs, *sems, act="wait")

    return _pcall(body, name=name, out_shape=tuple(out_structs),
                  in_specs=[ANY] * n_src, out_specs=tuple([ANY] * n_out),
                  scratch_shapes=_sem_scratch(items))(*srcs)


def _whole(ref, dest):
    return ref


def _slot(ref, sender):
    return ref.at[sender]


def _rows_of(rows):
    return lambda ref, dev: ref.at[pl.ds(dev * rows, rows), :]


def _cols_of(cols):
    return lambda ref, dev: ref.at[:, pl.ds(dev * cols, cols)]


def _pool_rows_of(rows):
    return lambda ref, dev: ref.at[:, pl.ds(dev * rows, rows), :]


def _gather_item(src, out, dst_view):
    return _Item(src, out, _whole, dst_view)


def _scatter_item(src, out, src_view):
    return _Item(src, out, src_view, _slot)


W_IN_BLOCK = 256
W_IN_SHARD = N_IN // N_DEV
SSM_BLOCKS = (2 * D // W_IN_BLOCK, 4 * D // W_IN_BLOCK)


def _w_in_block_item(src, out, j, ssm_part):
    def block(dest):
        return (W_IN_SHARD // W_IN_BLOCK) * dest + j

    def in_ssm(dest):
        b = block(dest)
        return (b >= SSM_BLOCKS[0]) & (b < SSM_BLOCKS[1])

    def src_view(ref, dest):
        b = block(dest)
        local = b - SSM_BLOCKS[0] if ssm_part else jnp.where(b < SSM_BLOCKS[0], b, b - (SSM_BLOCKS[1] - SSM_BLOCKS[0]))
        local = jnp.clip(local, 0, ref.shape[1] // W_IN_BLOCK - 1)
        return ref.at[:, pl.ds(local * W_IN_BLOCK, W_IN_BLOCK)]

    def dst_view(ref, sender):
        return ref.at[sender, :, pl.ds(j * W_IN_BLOCK, W_IN_BLOCK)]

    pred = in_ssm if ssm_part else (lambda dest: jnp.logical_not(in_ssm(dest)))
    return _Item(src, out, src_view, dst_view, pred)


def _cast_shards(arrs):
    def body(*refs):
        n = len(refs) // 2
        for i in range(n):
            refs[n + i][...] = refs[i][...].astype(BF16)

    vm = pl.BlockSpec(memory_space=pltpu.VMEM)
    return _pcall(body, name="cast_shards",
                  out_shape=tuple(jax.ShapeDtypeStruct(a.shape, BF16) for a in arrs),
                  in_specs=[vm] * len(arrs), out_specs=tuple([vm] * len(arrs)),
                  compiler_params=_params())(*arrs)


def _s5_discretise(a_re, a_im, log_dt, b_re_t, b_im_t):
    dt = jnp.exp(log_dt)
    lam_re = jnp.minimum(a_re, -1e-4)
    lam_im = a_im
    mag = jnp.exp(lam_re * dt)
    abar_re = mag * jnp.cos(lam_im * dt)
    abar_im = mag * jnp.sin(lam_im * dt)
    den = lam_re * lam_re + lam_im * lam_im
    num_re = abar_re - 1.0
    f_re = (num_re * lam_re + abar_im * lam_im) / den
    f_im = (abar_im * lam_re - num_re * lam_im) / den
    f_re, f_im = f_re[:, None, :], f_im[:, None, :]
    bb_re = f_re * b_re_t - f_im * b_im_t
    bb_im = f_re * b_im_t + f_im * b_re_t
    return abar_re, abar_im, bb_re, bb_im


def _group_masks():
    spread = lax.broadcasted_iota(jnp.int32, (G_P, 16 * G_P), 1) % G_P == lax.broadcasted_iota(
        jnp.int32, (G_P, 16 * G_P), 0)
    own = lax.broadcasted_iota(jnp.int32, (16 * G_H, 16 * G_P), 0) // G_H == lax.broadcasted_iota(
        jnp.int32, (16 * G_H, 16 * G_P), 1) // G_P
    return spread, own


def _s5_prep(a_re, a_im, log_dt, b_re_t, b_im_t, c_re, c_im, n_pow):
    def body(ar_ref, ai_ref, ld_ref, br_ref, bi_ref, cr_ref, ci_ref, wb_ref, wct_ref, pr_ref, pi_ref):
        abar_re, abar_im, bb_re, bb_im = _s5_discretise(ar_ref[...], ai_ref[...], ld_ref[...], br_ref[...], bi_ref[...])
        spread, own = _group_masks()
        spread = spread.astype(BF16)
        for ref, parts in ((wb_ref, (bb_re, bb_im)), (wct_ref, (cr_ref[...], -ci_ref[...]))):
            for half, t in enumerate(parts):
                for q in range(N_Q):
                    blocks = t[q * 16:(q + 1) * 16].reshape(16 * G_H, G_P).astype(BF16)
                    dense = jnp.where(own, _dot(blocks, spread), 0.0)
                    ref[q, :, half * (Q_W // 2):(half + 1) * (Q_W // 2)] = dense.astype(BF16)
        p_re, p_im = abar_re, abar_im
        pr_ref[0] = p_re
        pi_ref[0] = p_im
        for k in range(1, n_pow):
            p_re, p_im = p_re * abar_re - p_im * abar_im, p_re * abar_im + p_im * abar_re
            pr_ref[k] = p_re
            pi_ref[k] = p_im

    vm = pl.BlockSpec(memory_space=pltpu.VMEM)
    return _pcall(body, name="s5_prep",
                  out_shape=(jax.ShapeDtypeStruct((N_Q, 16 * G_H, Q_W), BF16),
                             jax.ShapeDtypeStruct((N_Q, 16 * G_H, Q_W), BF16),
                             jax.ShapeDtypeStruct((n_pow, GROUPS, G_P), F32),
                             jax.ShapeDtypeStruct((n_pow, GROUPS, G_P), F32)),
                  in_specs=[vm] * 7, out_specs=(vm, vm, vm, vm), compiler_params=_params(),
                  )(a_re, a_im, log_dt, b_re_t, b_im_t, c_re, c_im)


def _s5_prep_bwd(a_re, a_im, log_dt, b_re_t, b_im_t, d_abar_re, d_abar_im, d_bb_re, d_bb_im):
    def body(ar_ref, ai_ref, ld_ref, br_ref, bi_ref, dar_ref, dai_ref, dbr_ref, dbi_ref,
             gar_ref, gai_ref, gld_ref, gbr_ref, gbi_ref):
        _, vjp = jax.vjp(_s5_discretise, ar_ref[...], ai_ref[...], ld_ref[...], br_ref[...], bi_ref[...])
        g = vjp((dar_ref[...], dai_ref[...], dbr_ref[...], dbi_ref[...]))
        gar_ref[...] = g[0]
        gai_ref[...] = g[1]
        gld_ref[...] = g[2]
        gbr_ref[...] = g[3]
        gbi_ref[...] = g[4]

    vm = pl.BlockSpec(memory_space=pltpu.VMEM)
    ins = (a_re, a_im, log_dt, b_re_t, b_im_t)
    return _pcall(body, name="s5_prep_bwd",
                  out_shape=tuple(jax.ShapeDtypeStruct(a.shape, F32) for a in ins),
                  in_specs=[vm] * 9, out_specs=tuple([vm] * 5), compiler_params=_params(),
                  )(*ins, d_abar_re, d_abar_im, d_bb_re, d_bb_im)


def _state_layout(re, im):
    lead = re.shape[:-2]
    r = re.reshape(lead + (N_Q, 1, 16 * G_P))
    i = im.reshape(lead + (N_Q, 1, 16 * G_P))
    return jnp.concatenate([r, i], axis=-2).reshape(lead + (N_STATE,))


def _state_unlayout(v):
    v4 = v.reshape(N_Q, 2, 16, G_P)
    return v4[:, 0].reshape(GROUPS, G_P), v4[:, 1].reshape(GROUPS, G_P)


def _perm_matrix(tb):
    k_steps = tb // SUBLANES
    r = jnp.arange(tb)
    src = (r % SUBLANES) * k_steps + r // SUBLANES
    return (src[:, None] == jnp.arange(tb)[None, :]).astype(BF16)


def _lane_chunks(q):
    for lc in range(Q_W // 2 // LANE_CHUNK):
        re = q * Q_W + lc * LANE_CHUNK
        yield re, re + Q_W // 2


def _steps(lo, hi, body, init):
    if hi - lo <= SCAN_UNROLL:
        for k in range(lo, hi):
            init = body(k, init)
        return init
    trips = (hi - lo) // SCAN_UNROLL

    def trip(j, carry):
        for u in range(SCAN_UNROLL):
            carry = body(lo + j * SCAN_UNROLL + u, carry)
        return carry

    carry = lax.fori_loop(0, trips, trip, init)
    for k in range(lo + trips * SCAN_UNROLL, hi):
        carry = body(k, carry)
    return carry


def _tile(k):
    if isinstance(k, int):
        return pl.ds(k * SUBLANES, SUBLANES)
    return pl.ds(pl.multiple_of(k * SUBLANES, SUBLANES), SUBLANES)


def _scan_forward(q, s_ref, p_ref, carry_ref, enter_ref, fin_ref, k_steps):
    for re, im in _lane_chunks(q):
        lr, li = pl.ds(re, LANE_CHUNK), pl.ds(im, LANE_CHUNK)
        a_re = jnp.broadcast_to(p_ref[0:1, lr], (SUBLANES, LANE_CHUNK))
        a_im = jnp.broadcast_to(p_ref[0:1, li], (SUBLANES, LANE_CHUNK))

        def local(k, st):
            sr, si = st
            rows = _tile(k)
            nr = a_re * sr - a_im * si + s_ref[rows, lr]
            ni = a_re * si + a_im * sr + s_ref[rows, li]
            s_ref[rows, lr] = nr
            s_ref[rows, li] = ni
            return nr, ni

        zero = jnp.zeros((SUBLANES, LANE_CHUNK), F32)
        fr, fi = _steps(0, k_steps, local, (zero, zero))
        fin_ref[:, lr] = fr
        fin_ref[:, li] = fi
        ak_re, ak_im = p_ref[k_steps - 1:k_steps, lr], p_ref[k_steps - 1:k_steps, li]
        c_re, c_im = carry_ref[:, lr], carry_ref[:, li]
        for seg in range(SUBLANES):
            enter_ref[seg:seg + 1, lr] = c_re
            enter_ref[seg:seg + 1, li] = c_im
            f_re, f_im = fin_ref[seg:seg + 1, lr], fin_ref[seg:seg + 1, li]
            c_re, c_im = f_re + ak_re * c_re - ak_im * c_im, f_im + ak_re * c_im + ak_im * c_re
        carry_ref[:, lr] = c_re
        carry_ref[:, li] = c_im
        e_re, e_im = enter_ref[:, lr], enter_ref[:, li]

        def fix(k, _):
            rows = _tile(k)
            p_re = p_ref[pl.ds(k, 1), lr]
            p_im = p_ref[pl.ds(k, 1), li]
            s_ref[rows, lr] = s_ref[rows, lr] + (p_re * e_re - p_im * e_im)
            s_ref[rows, li] = s_ref[rows, li] + (p_re * e_im + p_im * e_re)
            return 0

        _steps(0, k_steps, fix, 0)


def _scan_backward(q, g_ref, s_ref, p_ref, carry_ref, s_in_ref, fin_ref, da_ref, k_steps):
    seg_id = lax.broadcasted_iota(jnp.int32, (SUBLANES, LANE_CHUNK), 0)
    for re, im in _lane_chunks(q):
        lr, li = pl.ds(re, LANE_CHUNK), pl.ds(im, LANE_CHUNK)
        a_re = jnp.broadcast_to(p_ref[0:1, lr], (SUBLANES, LANE_CHUNK))
        a_im = jnp.broadcast_to(p_ref[0:1, li], (SUBLANES, LANE_CHUNK))

        def local(j, st):
            sr, si = st
            rows = _tile(k_steps - 1 - j)
            nr = a_re * sr + a_im * si + g_ref[rows, lr]
            ni = a_re * si - a_im * sr + g_ref[rows, li]
            g_ref[rows, lr] = nr
            g_ref[rows, li] = ni
            return nr, ni

        zero = jnp.zeros((SUBLANES, LANE_CHUNK), F32)
        fr, fi = _steps(0, k_steps, local, (zero, zero))
        fin_ref[:, lr] = fr
        fin_ref[:, li] = fi
        ak_re, ak_im = p_ref[k_steps - 1:k_steps, lr], p_ref[k_steps - 1:k_steps, li]
        c_re, c_im = carry_ref[:, lr], carry_ref[:, li]
        lam_in = [None] * SUBLANES
        for seg in reversed(range(SUBLANES)):
            lam_in[seg] = (c_re, c_im)
            f_re, f_im = fin_ref[seg:seg + 1, lr], fin_ref[seg:seg + 1, li]
            c_re, c_im = f_re + ak_re * c_re + ak_im * c_im, f_im + ak_re * c_im - ak_im * c_re
        carry_ref[:, lr] = c_re
        carry_ref[:, li] = c_im
        for seg in range(SUBLANES):
            fin_ref[seg:seg + 1, lr] = lam_in[seg][0]
            fin_ref[seg:seg + 1, li] = lam_in[seg][1]
        e_re, e_im = fin_ref[:, lr], fin_ref[:, li]

        def fix_with(k, acc, sp_re, sp_im):
            acc_re, acc_im = acc
            rows = _tile(k)
            p_re = p_ref[pl.ds(k_steps - 1 - k, 1), lr]
            p_im = p_ref[pl.ds(k_steps - 1 - k, 1), li]
            l_re = g_ref[rows, lr] + (p_re * e_re + p_im * e_im)
            l_im = g_ref[rows, li] + (p_re * e_im - p_im * e_re)
            g_ref[rows, lr] = l_re
            g_ref[rows, li] = l_im
            return acc_re + (l_re * sp_re + l_im * sp_im), acc_im + (l_im * sp_re - l_re * sp_im)

        def fix(k, acc):
            prev = _tile(k - 1)
            return fix_with(k, acc, s_ref[prev, lr], s_ref[prev, li])

        last = _tile(k_steps - 1)
        before_re = jnp.where(seg_id == 0, s_in_ref[:, lr], pltpu.roll(s_ref[last, lr], 1, axis=0))
        before_im = jnp.where(seg_id == 0, s_in_ref[:, li], pltpu.roll(s_ref[last, li], 1, axis=0))
        acc = fix_with(0, (zero, zero), before_re, before_im)
        acc_re, acc_im = _steps(1, k_steps, fix, acc)
        da_ref[:, lr] = da_ref[:, lr] + jnp.sum(acc_re, axis=0, keepdims=True)
        da_ref[:, li] = da_ref[:, li] + jnp.sum(acc_im, axis=0, keepdims=True)


def _prenorm(x, mod3, norm_pre):
    xn, r = _rms_parts(x)
    return xn, r, xn * norm_pre * (1.0 + mod3[1:2, :]) + mod3[0:1, :]


CHIP_FLIPS = (4, 2, 6)


def _shard_order(me):
    flips = [0, 1] + [f + c for f in CHIP_FLIPS for c in (0, 1)]
    return jnp.stack([me ^ f for f in flips]).astype(jnp.int32)


def _in_proj(x, mod3, norm_pre, w_in_s, shards):
    rows = x.shape[0]
    tb = _tb(rows, 2048)
    nblk = rows // tb
    n_sh = len(shards)
    last_step = N_DEV - 1
    items = [_gather_item(0, 0, _pool_rows_of(shards[0].shape[1]))] + \
            [_gather_item(t, t, _rows_of(shards[t].shape[0])) for t in range(1, n_sh)]

    def body(order_ref, x_ref, mod_ref, np_ref, w_src, *rest):
        src_refs, proj_ref, w_full, out_refs = rest[:n_sh], rest[n_sh], rest[n_sh + 1], rest[n_sh + 2:2 * n_sh + 2]
        h_scr, wg, ssem, rsem, lsem, *sems = rest[2 * n_sh + 2:]
        s, i = pl.program_id(0), pl.program_id(1)
        me3 = _me()
        me = _flat(me3)
        sibling = _peer(1)

        def own_copy(slot, k):
            return pltpu.make_async_remote_copy(src_ref=w_src, dst_ref=wg.at[me], send_sem=ssem.at[slot],
                                                recv_sem=rsem.at[slot], device_id=_peer(k), device_id_type=MESH)

        def passed_copy(j):
            p = _flat(_peer(CHIP_FLIPS[j]))
            return pltpu.make_async_remote_copy(src_ref=wg.at[p], dst_ref=wg.at[p], send_sem=ssem.at[4 + j],
                                                recv_sem=rsem.at[4 + j], device_id=sibling, device_id_type=MESH)

        def arrival(slot, flip):
            p = _flat(_peer(flip))
            pltpu.make_async_remote_copy(src_ref=w_src, dst_ref=wg.at[p], send_sem=ssem.at[slot],
                                         recv_sem=rsem.at[slot], device_id=sibling, device_id_type=MESH).wait_recv()

        def keep(t):
            p = order_ref[t]
            return pltpu.make_async_copy(wg.at[p], w_full.at[:, pl.ds(p * W_IN_SHARD, W_IN_SHARD)], lsem.at[1 + t])

        first = i == 0
        for t in range(last_step):
            pl.when(first & (s == t + 1))(lambda t=t: keep(t).start())

        @pl.when(first & (s == 0))
        def _():
            mine = pltpu.make_async_copy(w_src, wg.at[me], lsem.at[0])
            mine.start()
            own_copy(0, 1).start()
            for j, f in enumerate(CHIP_FLIPS[:2]):
                own_copy(1 + j, f).start()
            mine.wait()

        @pl.when(first & (s == 1))
        def _():
            arrival(0, 1)

        for j, f in enumerate(CHIP_FLIPS):
            @pl.when(first & (s == 2 + 2 * j))
            def _(j=j, f=f):
                arrival(1 + j, f)
                passed_copy(j).start()
                if j == 0:
                    own_copy(3, CHIP_FLIPS[2]).start()

            @pl.when(first & (s == 3 + 2 * j))
            def _(j=j, f=f):
                arrival(4 + j, f + 1)

        @pl.when(first & (s == last_step - 1))
        def _():
            _hosted_copies(items, src_refs, out_refs, *sems, act="start")

        rows_i = pl.ds(pl.multiple_of(i * tb, tb), tb)

        @pl.when(s == 0)
        def _():
            _, _, h = _prenorm(x_ref[...], mod_ref[...], np_ref[...])
            h_scr[rows_i, :] = h.astype(BF16)

        proj_ref[...] = _dot(h_scr[rows_i, :], wg[order_ref[s]]).astype(BF16)

        @pl.when((s == last_step) & (i == nblk - 1))
        def _():
            own_copy(0, 1).wait_send()
            for j, f in enumerate(CHIP_FLIPS):
                own_copy(1 + j, f).wait_send()
                passed_copy(j).wait_send()
            keep(last_step).start()
            for t in range(N_DEV):
                keep(t).wait()
            _hosted_copies(items, src_refs, out_refs, *sems, act="wait")

    full = [jax.ShapeDtypeStruct((4, 256, 256), BF16)] + [jax.ShapeDtypeStruct((D, D), BF16)] * (n_sh - 1)
    grid_spec = pltpu.PrefetchScalarGridSpec(
        num_scalar_prefetch=1, grid=(N_DEV, nblk),
        in_specs=[pl.BlockSpec((tb, D), lambda s, i, order: (jnp.where(s == 0, i, nblk - 1), 0)),
                  pl.BlockSpec((3, D), lambda s, i, order: (0, 0)), pl.BlockSpec((1, D), lambda s, i, order: (0, 0)),
                  ANY] + [ANY] * n_sh,
        out_specs=(pl.BlockSpec((tb, W_IN_SHARD), lambda s, i, order: (i, order[s])), ANY, *([ANY] * n_sh)),
        scratch_shapes=[pltpu.VMEM((rows, D), BF16), pltpu.VMEM((N_DEV, D, W_IN_SHARD), BF16),
                        pltpu.SemaphoreType.DMA((N_DEV - 1,)), pltpu.SemaphoreType.DMA((N_DEV - 1,)),
                        pltpu.SemaphoreType.DMA((1 + N_DEV,))] + _sem_scratch(items))
    return _pcall(body, name="in_proj", grid_spec=grid_spec,
                  out_shape=(jax.ShapeDtypeStruct((rows, N_IN), BF16), jax.ShapeDtypeStruct((D, N_IN), BF16), *full),
                  compiler_params=_params(("arbitrary", "arbitrary")),
                  )(_shard_order(_flat(_me())), x, mod3, norm_pre, w_in_s, *shards)


def _pool_windows(ext, tb, first_row):
    pos = (first_row + lax.broadcasted_iota(jnp.int32, (tb, 1), 0) + 1).astype(F32)
    pooled, counts = [], []
    for g, w in enumerate(POOL_WINDOWS):
        acc = ext[:, g * 256:(g + 1) * 256]
        tok = acc[HALO:, :]
        s = 1
        while s < w:
            acc = acc + pltpu.roll(acc, s, axis=0)
            s *= 2
        cnt = jnp.minimum(pos, float(w))
        pooled.append(acc[HALO:, :] / cnt - tok)
        counts.append(cnt)
    return pooled, counts


def _pool_fwd(proj, pool_w, pool_scale):
    rows = proj.shape[0]
    tb = _tb(rows, 512)
    hb = tb // HALO

    def body(u_ref, halo_ref, z_ref, pw_ref, ps_ref, y_ref):
        i = pl.program_id(0)
        u = u_ref[...].astype(F32)
        halo = jnp.where(i > 0, halo_ref[...].astype(F32), 0.0)
        pooled, _ = _pool_windows(jnp.concatenate([halo, u], axis=0), tb, i * tb)
        silu_z, _ = _silu_parts(z_ref[...].astype(F32))
        for g in range(4):
            cols = slice(g * 256, (g + 1) * 256)
            mixed = _dot(pooled[g].astype(BF16), pw_ref[g])
            y_ref[:, cols] = (mixed * ps_ref[:, cols] * silu_z[:, cols]).astype(BF16)

    return _pcall(body, name="pool_fwd", grid=(rows // tb,),
                  out_shape=jax.ShapeDtypeStruct((rows, D), BF16),
                  in_specs=[pl.BlockSpec((tb, D), lambda i: (i, 0)),
                            pl.BlockSpec((HALO, D), lambda i: (jnp.maximum(i * hb - 1, 0), 0)),
                            pl.BlockSpec((tb, D), lambda i: (i, 1)),
                            _full((4, 256, 256)), _full((1, D))],
                  out_specs=pl.BlockSpec((tb, D), lambda i: (i, 0)),
                  compiler_params=_params(("arbitrary",)))(proj, proj, proj, pool_w, pool_scale)


def _ssm_fwd(proj, pm, pmt, wb, wct, ptab, dvec, glu_w, glu_b, shards):
    rows = proj.shape[0]
    tb = pm.shape[0]
    k_steps = tb // SUBLANES
    nblk = rows // tb
    n_sh = len(shards)
    items = [_gather_item(t, t, _rows_of(shards[t].shape[0])) for t in range(n_sh)]

    def body(u_ref, z_ref, pm_ref, pmt_ref, wb_ref, wct_ref, p_ref, d_ref, gw_ref, gb_ref, *rest):
        src_refs = rest[:n_sh]
        y_ref, ys_ref, carry_out_ref, s_ref = rest[n_sh:n_sh + 4]
        out_refs = rest[n_sh + 4:2 * n_sh + 4]
        carry_ref, enter_ref, fin_ref, *sems = rest[2 * n_sh + 4:]

        @pl.when(pl.program_id(0) == 0)
        def _():
            _hosted_copies(items, src_refs, out_refs, *sems, act="start")
            carry_ref[...] = jnp.zeros_like(carry_ref)

        carry_out_ref[...] = carry_ref[...]
        up = _dot(pm_ref[...], u_ref[...]).astype(BF16)

        for q in range(N_Q):
            s_ref[:, q * Q_W:(q + 1) * Q_W] = _dot(up[:, q * 256:(q + 1) * 256], wb_ref[q])
        for q in range(N_Q):
            _scan_forward(q, s_ref, p_ref, carry_ref, enter_ref, fin_ref, k_steps)
        for q in range(N_Q):
            cols = slice(q * 256, (q + 1) * 256)
            y = _dot_nt(s_ref[:, q * Q_W:(q + 1) * Q_W].astype(BF16), wct_ref[q])
            ys_ref[:, cols] = y + d_ref[:, cols] * up[:, cols].astype(F32)
        yg, _ = _gelu_parts(ys_ref[...])
        gate = jax.nn.sigmoid(_dot(yg.astype(BF16), gw_ref[...]) + gb_ref[...])
        zp = _dot(pm_ref[...], z_ref[...])
        silu_z, _ = _silu_parts(zp)
        y_ref[...] = _dot(pmt_ref[...], (yg * gate * silu_z).astype(BF16)).astype(BF16)

        @pl.when(pl.program_id(0) == nblk - 1)
        def _():
            _hosted_copies(items, src_refs, out_refs, *sems, act="wait")

    return _pcall(body, name="ssm_fwd", grid=(nblk,),
                  out_shape=(jax.ShapeDtypeStruct((rows, D), BF16), jax.ShapeDtypeStruct((rows, D), F32),
                             jax.ShapeDtypeStruct((nblk, 1, N_STATE), F32),
                             jax.ShapeDtypeStruct((rows, N_STATE), F32),
                             *[jax.ShapeDtypeStruct((D, D), BF16)] * n_sh),
                  in_specs=[pl.BlockSpec((tb, D), lambda i: (i, 2)), pl.BlockSpec((tb, D), lambda i: (i, 3)),
                            _full((tb, tb)), _full((tb, tb)),
                            _full((N_Q, 256, Q_W), single=True), _full((N_Q, 256, Q_W), single=True),
                            _full((k_steps, N_STATE)), _full((1, D)), _full((D, D), single=True), _full((1, D))] +
                           [ANY] * n_sh,
                  out_specs=(pl.BlockSpec((tb, D), lambda i: (i, 0)), pl.BlockSpec((tb, D), lambda i: (i, 0)),
                             pl.BlockSpec((None, 1, N_STATE), lambda i: (i, 0, 0)),
                             pl.BlockSpec((tb, N_STATE), lambda i: (i, 0)), *([ANY] * n_sh)),
                  scratch_shapes=[pltpu.VMEM((1, N_STATE), F32),
                                  pltpu.VMEM((SUBLANES, N_STATE), F32), pltpu.VMEM((SUBLANES, N_STATE), F32)] +
                                 _sem_scratch(items),
                  compiler_params=_params(("arbitrary",)))(proj, proj, pm, pmt, wb, wct, ptab, dvec, glu_w, glu_b,
                                                           *shards)


def _head(x, target, proj, y_pool, y_ssm, mod3, norm_post, wbp, wbs, wout):
    rows = x.shape[0]
    tb = _tb(rows, 256)
    nblk = rows // tb
    n_feat = float(D)

    def body(x_ref, t_ref, gp_ref, gs_ref, yp_ref, ys_ref, mod_ref, npost_ref, wbp_ref, wbs_ref, wout_ref,
             loss_ref, dy_ref, dyp_ref, dys_ref, dg_ref, dwbp_hbm, dwbs_hbm, dwout_hbm, vec_ref,
             acc_bp, acc_bs, acc_out, acc_loss, acc_vec):
        i = pl.program_id(0)

        @pl.when(i == 0)
        def _():
            acc_bp[...] = jnp.zeros_like(acc_bp)
            acc_bs[...] = jnp.zeros_like(acc_bs)
            acc_out[...] = jnp.zeros_like(acc_out)
            acc_loss[...] = jnp.zeros_like(acc_loss)
            acc_vec[...] = jnp.zeros_like(acc_vec)

        yp, ys = yp_ref[...], ys_ref[...]
        sgp = jax.nn.sigmoid(gp_ref[...].astype(F32))
        sgs = jax.nn.sigmoid(gs_ref[...].astype(F32))
        pb = _dot(yp, wbp_ref[...])
        psm = _dot(ys, wbs_ref[...])
        mb = (sgp * pb + sgs * psm).astype(BF16)
        out = _dot(mb, wout_ref[...])
        on, r = _rms_parts(out)
        gate = mod_ref[2:3, :]
        npost = npost_ref[...]
        normed = on * npost
        diff = x_ref[...] + gate * normed - t_ref[...]
        acc_loss[...] += jnp.sum(diff * diff, axis=0, keepdims=True)
        dy = diff * (1.0 / n_feat)
        dy_ref[...] = dy
        acc_vec[0:1, :] += jnp.sum(dy * normed, axis=0, keepdims=True)
        dn = dy * gate
        acc_vec[1:2, :] += jnp.sum(dn * on, axis=0, keepdims=True)
        dout = _rms_bwd(dn * npost, on, r).astype(BF16)
        acc_out[...] += _dot_tn(mb, dout)
        dm = _dot_nt(dout, wout_ref[...])
        dpb = (dm * sgp).astype(BF16)
        dps = (dm * sgs).astype(BF16)
        dg_ref[:, :D] = (dm * pb * sgp * (1.0 - sgp)).astype(BF16)
        dg_ref[:, D:] = (dm * psm * sgs * (1.0 - sgs)).astype(BF16)
        acc_bp[...] += _dot_tn(yp, dpb)
        acc_bs[...] += _dot_tn(ys, dps)
        dyp_ref[...] = _dot_nt(dpb, wbp_ref[...]).astype(BF16)
        dys_ref[...] = _dot_nt(dps, wbs_ref[...]).astype(BF16)

        @pl.when(i == nblk - 1)
        def _():
            loss_ref[...] = 0.5 / n_feat * jnp.sum(acc_loss[...], axis=1, keepdims=True)
            vec_ref[...] = acc_vec[...]
            pltpu.sync_copy(acc_bp, dwbp_hbm)
            pltpu.sync_copy(acc_bs, dwbs_hbm)
            pltpu.sync_copy(acc_out, dwout_hbm)

    row = lambda c: pl.BlockSpec((tb, D), lambda i: (i, c))
    w = _full((D, D), single=True)
    return _pcall(body, name="head", grid=(nblk,),
                  out_shape=(jax.ShapeDtypeStruct((1, 1), F32), jax.ShapeDtypeStruct((rows, D), F32),
                             jax.ShapeDtypeStruct((rows, D), BF16), jax.ShapeDtypeStruct((rows, D), BF16),
                             jax.ShapeDtypeStruct((rows, 2 * D), BF16),
                             jax.ShapeDtypeStruct((D, D), F32), jax.ShapeDtypeStruct((D, D), F32),
                             jax.ShapeDtypeStruct((D, D), F32), jax.ShapeDtypeStruct((2, D), F32)),
                  in_specs=[row(0), row(0), row(4), row(5), row(0), row(0), _full((3, D)), _full((1, D)), w, w, w],
                  out_specs=(_full((1, 1)), row(0), row(0), row(0), pl.BlockSpec((tb, 2 * D), lambda i: (i, 0)),
                             ANY, ANY, ANY, _full((2, D))),
                  scratch_shapes=[pltpu.VMEM((D, D), F32), pltpu.VMEM((D, D), F32), pltpu.VMEM((D, D), F32),
                                  pltpu.VMEM((1, D), F32), pltpu.VMEM((2, D), F32)],
                  compiler_params=_params(("arbitrary",)))(x, target, proj, proj, y_pool, y_ssm, mod3, norm_post,
                                                           wbp, wbs, wout)


def _glu_bwd(dys, proj, ys_pre, pm, pmt, glu_w, glu_b):
    rows = dys.shape[0]
    tb = pm.shape[0]
    nblk = rows // tb

    def body(dys_ref, z_ref, ysp_ref, pm_ref, pmt_ref, gw_ref, gb_ref, dyp_ref, dz_ref, dgw_hbm, dgb_ref,
             acc_w, acc_b):
        i = pl.program_id(0)

        @pl.when(i == 0)
        def _():
            acc_w[...] = jnp.zeros_like(acc_w)
            acc_b[...] = jnp.zeros_like(acc_b)

        d_out = _dot(pm_ref[...], dys_ref[...])
        z = _dot(pm_ref[...], z_ref[...])
        yg, dgelu = _gelu_parts(ysp_ref[...])
        ygb = yg.astype(BF16)
        sg = jax.nn.sigmoid(_dot(ygb, gw_ref[...]) + gb_ref[...])
        silu_z, dsilu_z = _silu_parts(z)
        dz = d_out * (yg * sg) * dsilu_z
        dz_ref[...] = _dot(pmt_ref[...], dz.astype(BF16)).astype(BF16)
        dglu = d_out * silu_z
        dq = dglu * yg * sg * (1.0 - sg)
        dqb = dq.astype(BF16)
        acc_b[...] += jnp.sum(dq, axis=0, keepdims=True)
        acc_w[...] += _dot_tn(ygb, dqb)
        dyg = dglu * sg + _dot_nt(dqb, gw_ref[...])
        dyp_ref[...] = (dyg * dgelu).astype(BF16)

        @pl.when(i == nblk - 1)
        def _():
            dgb_ref[...] = acc_b[...]
            pltpu.sync_copy(acc_w, dgw_hbm)

    row = lambda c: pl.BlockSpec((tb, D), lambda i: (i, c))
    return _pcall(body, name="glu_bwd", grid=(nblk,),
                  out_shape=(jax.ShapeDtypeStruct((rows, D), BF16), jax.ShapeDtypeStruct((rows, D), BF16),
                             jax.ShapeDtypeStruct((D, D), F32), jax.ShapeDtypeStruct((1, D), F32)),
                  in_specs=[row(0), row(3), row(0), _full((tb, tb)), _full((tb, tb)),
                            _full((D, D), single=True), _full((1, D))],
                  out_specs=(row(0), row(0), ANY, _full((1, D))),
                  scratch_shapes=[pltpu.VMEM((D, D), F32), pltpu.VMEM((1, D), F32)],
                  compiler_params=_params(("arbitrary",)))(dys, proj, ys_pre, pm, pmt, glu_w, glu_b)


def _ssm_bwd(dyp, proj, states, carries, pm, pmt, wb, wct, ptab, dvec, mat_grads, dpool_w, dw_in_rest):
    rows = dyp.shape[0]
    tb = pm.shape[0]
    k_steps = tb // SUBLANES
    nblk = rows // tb
    n_mat = len(mat_grads)
    hosted = [*mat_grads, dpool_w, dw_in_rest]
    n_h = len(hosted)
    shard_rows = D // N_DEV
    pool_rows = dpool_w.shape[1] // N_DEV
    items = [_scatter_item(t, t, _rows_of(shard_rows)) for t in range(n_mat)] + \
            [_scatter_item(n_mat, n_mat, _pool_rows_of(pool_rows))] + \
            [_w_in_block_item(n_mat + 1, n_mat + 1, j, ssm_part=False) for j in range(W_IN_SHARD // W_IN_BLOCK)]
    n_in, n_out = 10, 5

    def body(*refs):
        dyp_ref, u_ref, s_ref, cin_ref, pm_ref, pmt_ref, wb_ref, wct_ref, p_ref, d_ref = refs[:n_in]
        src_refs = refs[n_in:n_in + n_h]
        du_ref, dbb_ref, dcc_ref, da_ref, dd_ref = refs[n_in + n_h:n_in + n_h + n_out]
        recv_refs = refs[n_in + n_h + n_out:n_in + 2 * n_h + n_out]
        (g_ref, carry_b, fin_ref, acc_wb, acc_wct, acc_da, acc_dd, dup_ref,
         *sems) = refs[n_in + 2 * n_h + n_out:]
        i = pl.program_id(0)

        @pl.when(i == 0)
        def _():
            _hosted_copies(items, src_refs, recv_refs, *sems, act="start")
            carry_b[...] = jnp.zeros_like(carry_b)
            acc_wb[...] = jnp.zeros_like(acc_wb)
            acc_wct[...] = jnp.zeros_like(acc_wct)
            acc_da[...] = jnp.zeros_like(acc_da)
            acc_dd[...] = jnp.zeros_like(acc_dd)

        def keep_own(acc, q, prod):
            for gl in range(16):
                r, c = slice(gl * G_H, (gl + 1) * G_H), (gl // 2) * 128
                acc[q, r, 0:128] += prod[r, c:c + 128]
                acc[q, r, 128:256] += prod[r, Q_W // 2 + c:Q_W // 2 + c + 128]

        dy = dyp_ref[...]
        up = _dot(pm_ref[...], u_ref[...]).astype(BF16)
        acc_dd[...] += jnp.sum(dy.astype(F32) * up.astype(F32), axis=0, keepdims=True)
        for q in range(N_Q):
            cols = slice(q * 256, (q + 1) * 256)
            g_ref[:, q * Q_W:(q + 1) * Q_W] = _dot(dy[:, cols], wct_ref[q])
            keep_own(acc_wct, q, _dot_tn(dy[:, cols], s_ref[:, q * Q_W:(q + 1) * Q_W].astype(BF16)))
        for q in range(N_Q):
            _scan_backward(q, g_ref, s_ref, p_ref, carry_b, cin_ref, fin_ref, acc_da, k_steps)
        for q in range(N_Q):
            cols = slice(q * 256, (q + 1) * 256)
            lam = g_ref[:, q * Q_W:(q + 1) * Q_W].astype(BF16)
            keep_own(acc_wb, q, _dot_tn(up[:, cols], lam))
            dup_ref[:, cols] = (_dot_nt(lam, wb_ref[q]) + d_ref[:, cols] * dy[:, cols].astype(F32)).astype(BF16)
        du_ref[...] = _dot(pmt_ref[...], dup_ref[...]).astype(BF16)

        @pl.when(i == nblk - 1)
        def _():
            da_ref[...] = acc_da[...]
            dd_ref[...] = acc_dd[...]
            lane = lax.broadcasted_iota(jnp.int32, (16 * G_H, 128), 1)
            row = lax.broadcasted_iota(jnp.int32, (16 * G_H, 128), 0)
            own = lane // G_P == (row // G_H) % 2
            spread = (lax.broadcasted_iota(jnp.int32, (G_P, 128), 1) % G_P ==
                      lax.broadcasted_iota(jnp.int32, (G_P, 128), 0)).astype(F32)
            for acc, out in ((acc_wb, dbb_ref), (acc_wct, dcc_ref)):
                for half in range(2):
                    for q in range(N_Q):
                        kept = jnp.where(own, acc[q, :, half * 128:(half + 1) * 128], 0.0)
                        out[half, q] = lax.dot_general(kept, spread, (((1,), (1,)), ((), ())),
                                                       preferred_element_type=F32, precision=lax.Precision.HIGHEST)
            _hosted_copies(items, src_refs, recv_refs, *sems, act="wait")

    rev = lambda c: pl.BlockSpec((tb, D), lambda i: (nblk - 1 - i, c))
    recv = [jax.ShapeDtypeStruct((N_DEV, shard_rows, D), F32)] * n_mat + \
           [jax.ShapeDtypeStruct((N_DEV, dpool_w.shape[0], pool_rows, dpool_w.shape[2]), F32),
            jax.ShapeDtypeStruct((N_DEV, D, W_IN_SHARD), BF16)]
    return _pcall(body, name="ssm_bwd", grid=(nblk,),
                  out_shape=(jax.ShapeDtypeStruct((rows, D), BF16),
                             jax.ShapeDtypeStruct((2, N_Q, 16 * G_H, G_P), F32),
                             jax.ShapeDtypeStruct((2, N_Q, 16 * G_H, G_P), F32),
                             jax.ShapeDtypeStruct((1, N_STATE), F32), jax.ShapeDtypeStruct((1, D), F32), *recv),
                  in_specs=[rev(0), rev(2), pl.BlockSpec((tb, N_STATE), lambda i: (nblk - 1 - i, 0)),
                            pl.BlockSpec((None, 1, N_STATE), lambda i: (nblk - 1 - i, 0, 0)),
                            _full((tb, tb)), _full((tb, tb)),
                            _full((N_Q, 256, Q_W), single=True), _full((N_Q, 256, Q_W), single=True),
                            _full((k_steps, N_STATE)), _full((1, D))] + [ANY] * n_h,
                  out_specs=(rev(0), _full((2, N_Q, 16 * G_H, G_P)), _full((2, N_Q, 16 * G_H, G_P)),
                             _full((1, N_STATE)), _full((1, D)), *([ANY] * n_h)),
                  scratch_shapes=[pltpu.VMEM((tb, N_STATE), F32), pltpu.VMEM((1, N_STATE), F32),
                                  pltpu.VMEM((SUBLANES, N_STATE), F32),
                                  pltpu.VMEM((N_Q, 16 * G_H, 256), F32), pltpu.VMEM((N_Q, 16 * G_H, 256), F32),
                                  pltpu.VMEM((1, N_STATE), F32), pltpu.VMEM((1, D), F32),
                                  pltpu.VMEM((tb, D), BF16)] + _sem_scratch(items),
                  compiler_params=_params(("arbitrary",), vmem=60 * 1024 * 1024),
                  )(dyp, proj, states, carries, pm, pmt, wb, wct, ptab, dvec, *hosted)


def _pool_bwd(dyp, proj, pool_w, pool_scale):
    rows = dyp.shape[0]
    tb = _tb(rows, 512)
    nblk = rows // tb
    hb = tb // HALO

    def body(dy_ref, u_ref, halo_ref, z_ref, pw_ref, ps_ref, dp_ref, dpw_ref, dps_ref, ahead_ref):
        i = pl.program_id(0)
        blk = nblk - 1 - i

        @pl.when(i == 0)
        def _():
            ahead_ref[...] = jnp.zeros_like(ahead_ref)
            dpw_ref[...] = jnp.zeros_like(dpw_ref)
            dps_ref[...] = jnp.zeros_like(dps_ref)

        u = u_ref[...].astype(F32)
        halo = jnp.where(blk > 0, halo_ref[...].astype(F32), 0.0)
        pooled, counts = _pool_windows(jnp.concatenate([halo, u], axis=0), tb, blk * tb)
        silu_z, dsilu_z = _silu_parts(z_ref[...].astype(F32))
        dy = dy_ref[...].astype(F32)
        for g, w in enumerate(POOL_WINDOWS):
            cols = slice(g * 256, (g + 1) * 256)
            pooled_b = pooled[g].astype(BF16)
            mixed = _dot(pooled_b, pw_ref[g])
            scale = ps_ref[:, cols]
            dp_ref[:, D + g * 256:D + (g + 1) * 256] = (dy[:, cols] * (mixed * scale) * dsilu_z[:, cols]).astype(BF16)
            dms = dy[:, cols] * silu_z[:, cols]
            dps_ref[:, cols] += jnp.sum(dms * mixed, axis=0, keepdims=True)
            dmixed = (dms * scale).astype(BF16)
            dpw_ref[g] += _dot_tn(pooled_b, dmixed)
            dpooled = _dot_nt(dmixed, pw_ref[g])
            ratio = dpooled / counts[g]
            acc = jnp.concatenate([ratio, ahead_ref[:, cols]], axis=0)
            ahead_ref[:, cols] = ratio[:HALO, :]
            s = 1
            while s < w:
                acc = acc + pltpu.roll(acc, tb + HALO - s, axis=0)
                s *= 2
            dp_ref[:, cols] = (acc[:tb, :] - dpooled).astype(BF16)

    rev = lambda c: pl.BlockSpec((tb, D), lambda i: (nblk - 1 - i, c))
    return _pcall(body, name="pool_bwd", grid=(nblk,),
                  out_shape=(jax.ShapeDtypeStruct((rows, 2 * D), BF16), jax.ShapeDtypeStruct((4, 256, 256), F32),
                             jax.ShapeDtypeStruct((1, D), F32)),
                  in_specs=[rev(0), rev(0),
                            pl.BlockSpec((HALO, D), lambda i: (jnp.maximum((nblk - 1 - i) * hb - 1, 0), 0)),
                            rev(1), _full((4, 256, 256)), _full((1, D))],
                  out_specs=(pl.BlockSpec((tb, 2 * D), lambda i: (nblk - 1 - i, 0)), _full((4, 256, 256)),
                             _full((1, D))),
                  scratch_shapes=[pltpu.VMEM((HALO, D), F32)],
                  compiler_params=_params(("arbitrary",)))(dyp, proj, proj, proj, pool_w, pool_scale)


def _dproj_specs(tb):
    return [pl.BlockSpec((tb, 2 * D), lambda i: (i, 0)), pl.BlockSpec((tb, D), lambda i: (i, 0)),
            pl.BlockSpec((tb, D), lambda i: (i, 0)), pl.BlockSpec((tb, 2 * D), lambda i: (i, 0))]


def _in_proj_bwd_x(x, dy, dpp, dus, dzs, dpg, mod3, norm_pre, w_in, dw_in_ssm, small32, small16, recv_w_in):
    rows = x.shape[0]
    tb = _tb(rows, 256)
    nblk = rows // tb
    items = [_w_in_block_item(0, 0, j, ssm_part=True) for j in range(W_IN_SHARD // W_IN_BLOCK)] + \
            [_Item(1, 1, _whole, _slot), _Item(2, 2, _whole, _slot)]
    sums_item = [_Item(0, 0, _whole, _slot)]

    def body(x_ref, dy_ref, dpp_ref, dus_ref, dzs_ref, dpg_ref, mod_ref, np_ref, w_ref,
             dw_src, s32_src, s16_src, _, gx_ref, recv_w, recv32, recv16, recv_sums,
             vec_ref, ssem, rsem, lsem, *sums_sems):
        src_refs, recv_refs, sems = (dw_src, s32_src, s16_src), (recv_w, recv32, recv16), (ssem, rsem, lsem)

        @pl.when(pl.program_id(0) == 0)
        def _():
            _hosted_copies(items, src_refs, recv_refs, *sems, act="start")
            vec_ref[...] = jnp.zeros_like(vec_ref)

        dh = _dot_nt(dpp_ref[...], w_ref[:, 0:2 * D])
        dh += _dot_nt(dus_ref[...], w_ref[:, 2 * D:3 * D])
        dh += _dot_nt(dzs_ref[...], w_ref[:, 3 * D:4 * D])
        dh += _dot_nt(dpg_ref[...], w_ref[:, 4 * D:6 * D])
        xn, r, _ = _prenorm(x_ref[...], mod_ref[...], np_ref[...])
        one_scale = 1.0 + mod_ref[1:2, :]
        vec_ref[0:1, :] += jnp.sum(dh, axis=0, keepdims=True)
        vec_ref[1:2, :] += jnp.sum(dh * xn, axis=0, keepdims=True) * np_ref[...]
        vec_ref[2:3, :] += jnp.sum(dh * xn, axis=0, keepdims=True) * one_scale
        gx_ref[...] = dy_ref[...] + _rms_bwd(dh * (np_ref[...] * one_scale), xn, r)

        @pl.when(pl.program_id(0) == nblk - 1)
        def _():
            _hosted_copies(sums_item, (vec_ref,), (recv_sums,), *sums_sems, act="start")
            _hosted_copies(items, src_refs, recv_refs, *sems, act="wait")
            _hosted_copies(sums_item, (vec_ref,), (recv_sums,), *sums_sems, act="wait")

    row = pl.BlockSpec((tb, D), lambda i: (i, 0))
    recv = (jax.ShapeDtypeStruct(recv_w_in.shape, recv_w_in.dtype),
            jax.ShapeDtypeStruct((N_DEV,) + small32.shape, small32.dtype),
            jax.ShapeDtypeStruct((N_DEV,) + small16.shape, small16.dtype),
            jax.ShapeDtypeStruct((N_DEV, 3, D), F32))
    return _pcall(body, name="in_proj_bwd_x", grid=(nblk,),
                  out_shape=(jax.ShapeDtypeStruct((rows, D), F32), *recv),
                  in_specs=[row, row] + _dproj_specs(tb) + [_full((3, D)), _full((1, D)),
                                                            _full((D, N_IN), single=True)] + [ANY] * 4,
                  out_specs=(row, ANY, ANY, ANY, ANY),
                  input_output_aliases={12: 1},
                  scratch_shapes=[pltpu.VMEM((3, D), F32)] + _sem_scratch(items) + _sem_scratch(sums_item),
                  compiler_params=_params(("arbitrary",)))(x, dy, dpp, dus, dzs, dpg, mod3, norm_pre, w_in,
                                                           dw_in_ssm, small32, small16, recv_w_in)


def _in_proj_bwd_w(name, x, dparts, mod3, norm_pre):
    rows = x.shape[0]
    tb = _tb(rows, 256)
    nblk = rows // tb
    widths = [p.shape[1] for p in dparts]
    n_p = len(dparts)

    def body(x_ref, *rest):
        part_refs, (mod_ref, np_ref, dw_ref, acc) = rest[:n_p], rest[n_p:]
        i = pl.program_id(0)

        @pl.when(i == 0)
        def _():
            acc[...] = jnp.zeros_like(acc)

        _, _, h = _prenorm(x_ref[...], mod_ref[...], np_ref[...])
        ht = h.astype(BF16)
        lo = 0
        for ref, w in zip(part_refs, widths):
            acc[:, lo:lo + w] += _dot_tn(ht, ref[...])
            lo += w

        @pl.when(i == nblk - 1)
        def _():
            dw_ref[...] = acc[...].astype(BF16)

    row = pl.BlockSpec((tb, D), lambda i: (i, 0))
    return _pcall(body, name=name, grid=(nblk,),
                  out_shape=jax.ShapeDtypeStruct((D, sum(widths)), BF16),
                  in_specs=[row] + [pl.BlockSpec((tb, w), lambda i: (i, 0)) for w in widths] +
                           [_full((3, D)), _full((1, D))],
                  out_specs=_full((D, sum(widths))),
                  scratch_shapes=[pltpu.VMEM((D, sum(widths)), F32)],
                  compiler_params=_params(("arbitrary",)))(x, *dparts, mod3, norm_pre)


def _adamw_math(w, g, m, v):
    m = ADAM_B1 * m + (1.0 - ADAM_B1) * g
    v = ADAM_B2 * v + (1.0 - ADAM_B2) * (g * g)
    m_hat = m / (1.0 - ADAM_B1 ** ADAM_STEP)
    v_hat = v / (1.0 - ADAM_B2 ** ADAM_STEP)
    delta = -ADAM_LR * (m_hat / (jnp.sqrt(v_hat) + ADAM_EPS) + ADAM_WD * w)
    return delta, m, v


def _sum_sources(ref):
    g = ref[0].astype(F32)
    for s in range(1, N_DEV):
        g = g + ref[s].astype(F32)
    return g


def _adamw_reduce(name, parts, w, m, v):
    r, c = w.shape
    tr = r if r * c <= 256 * 1024 else max(8, (256 * 1024 // c) // 8 * 8)
    while r % tr:
        tr -= 8

    def body(p_ref, w_ref, m_ref, v_ref, g_ref, d_ref, nm_ref, nv_ref):
        g = _sum_sources(p_ref)
        g_ref[...] = g
        d_ref[...], nm_ref[...], nv_ref[...] = _adamw_math(w_ref[...], g, m_ref[...], v_ref[...])

    blk = pl.BlockSpec((tr, c), lambda i: (i, 0))
    return _pcall(body, name=name, grid=(r // tr,),
                  out_shape=tuple([jax.ShapeDtypeStruct((r, c), F32)] * 4),
                  in_specs=[pl.BlockSpec((N_DEV, tr, c), lambda i: (0, i, 0)), blk, blk, blk],
                  out_specs=(blk, blk, blk, blk),
                  compiler_params=_params(("arbitrary",)))(parts, w, m, v)


def _adamw_small(gs, ws, ms, vs):
    n = len(gs)

    def body(*refs):
        ins, outs = refs[:4 * n], refs[4 * n:]
        for t in range(n):
            g_ref, w_ref, m_ref, v_ref = ins[4 * t:4 * t + 4]
            outs[3 * t][...], outs[3 * t + 1][...], outs[3 * t + 2][...] = _adamw_math(
                w_ref[...], g_ref[...], m_ref[...], v_ref[...])

    vm = pl.BlockSpec(memory_space=pltpu.VMEM)
    flat = [a for t in range(n) for a in (gs[t], ws[t], ms[t], vs[t])]
    return _pcall(body, name="adamw_small",
                  out_shape=tuple(jax.ShapeDtypeStruct(w.shape, F32) for w in ws for _ in range(3)),
                  in_specs=[vm] * (4 * n), out_specs=tuple([vm] * (3 * n)), compiler_params=_params())(*flat)


def _sum_small(parts):
    n = len(parts)

    def body(*refs):
        for t in range(n):
            refs[n + t][...] = _sum_sources(refs[t])

    vm = pl.BlockSpec(memory_space=pltpu.VMEM)
    return _pcall(body, name="sum_small",
                  out_shape=tuple(jax.ShapeDtypeStruct(p.shape[1:], F32) for p in parts),
                  in_specs=[vm] * n, out_specs=tuple([vm] * n), compiler_params=_params())(*parts)


def _ada_update(c_all, dmod_cols, w, m, v):
    def body(c_ref, dm_ref, w_ref, m_ref, v_ref, g_ref, d_ref, nm_ref, nv_ref):
        ca = c_ref[...]
        g = lax.dot_general(ca * jax.nn.sigmoid(ca), dm_ref[...], (((0,), (0,)), ((), ())),
                            preferred_element_type=F32, precision=lax.Precision.HIGHEST)
        g_ref[...] = g
        d_ref[...], nm_ref[...], nv_ref[...] = _adamw_math(w_ref[...], g, m_ref[...], v_ref[...])

    vm = pl.BlockSpec(memory_space=pltpu.VMEM)
    return _pcall(body, name="ada_update", out_shape=tuple([jax.ShapeDtypeStruct(w.shape, F32)] * 4),
                  in_specs=[vm] * 5, out_specs=(vm, vm, vm, vm), compiler_params=_params())(c_all, dmod_cols, w, m, v)


def kernel(x, c, w_ada, b_ada, norm_pre, norm_post, w_in, pool_w, pool_scale, ssm_a_re, ssm_a_im, ssm_log_dt, ssm_b_re, ssm_b_im, ssm_c_re, ssm_c_im, ssm_d, glu_w, glu_b, w_branch_pool, w_branch_ssm, w_out, loss_target, m_w_ada, m_b_ada, m_norm_pre, m_norm_post, m_w_in, m_pool_w, m_pool_scale, m_ssm_a_re, m_ssm_a_im, m_ssm_log_dt, m_ssm_b_re, m_ssm_b_im, m_ssm_c_re, m_ssm_c_im, m_ssm_d, m_glu_w, m_glu_b, m_w_branch_pool, m_w_branch_ssm, m_w_out, v_w_ada, v_b_ada, v_norm_pre, v_norm_post, v_w_in, v_pool_w, v_pool_scale, v_ssm_a_re, v_ssm_a_im, v_ssm_log_dt, v_ssm_b_re, v_ssm_b_im, v_ssm_c_re, v_ssm_c_im, v_ssm_d, v_glu_w, v_glu_b, v_w_branch_pool, v_w_branch_ssm, v_w_out):
    given = dict(locals())
    me = _flat(_me())
    rows = x.shape[1]
    x2 = x[0]
    target = loss_target[0]
    ada_cols = w_ada.shape[2]

    b_ada_s = lax.dynamic_slice(b_ada, (0, me * ada_cols), (1, ada_cols))
    c_all, mod_rows = _ada_exchange(c, w_ada[0], b_ada_s)
    mod3 = mod_rows.reshape(3, D)

    shards = _cast_shards([w_in[0], pool_w[0], glu_w[0], w_branch_pool[0], w_branch_ssm[0], w_out[0]])

    tb_ssm = _tb(rows, 256)
    k_steps = tb_ssm // SUBLANES
    a_re, a_im = ssm_a_re[0], ssm_a_im[0]
    log_dt = ssm_log_dt[0].reshape(GROUPS, 1)
    b_re_t, b_im_t = ssm_b_re[0].transpose(0, 2, 1), ssm_b_im[0].transpose(0, 2, 1)
    wb, wct, pow_re, pow_im = _s5_prep(a_re, a_im, log_dt, b_re_t, b_im_t, ssm_c_re[0], ssm_c_im[0], k_steps)
    ptab = _state_layout(pow_re, pow_im)
    dvec = ssm_d[0].reshape(1, D)
    pm = _perm_matrix(tb_ssm)
    pmt = pm.T

    proj, w_in_g, pool_w_g, glu_g = _in_proj(x2, mod3, norm_pre, shards[0], shards[1:3])
    y_pool = _pool_fwd(proj, pool_w_g, pool_scale)
    y_ssm, ys_pre, carries, states, wbp_g, wbs_g, wout_g = _ssm_fwd(
        proj, pm, pmt, wb, wct, ptab, dvec, glu_g, glu_b, shards[3:])
    loss_part, dy, dyp, dys, dpg, dwbp, dwbs, dwout, head_vec = _head(
        x2, target, proj, y_pool, y_ssm, mod3, norm_post, wbp_g, wbs_g, wout_g)

    dpp, dpool_w, dpool_scale = _pool_bwd(dyp, proj, pool_w_g, pool_scale)
    dw_in_rest = _in_proj_bwd_w("in_proj_bwd_w_rest", x2, [dpp, dpg], mod3, norm_pre)
    dy_pre, dzs, dglu_w, dglu_b = _glu_bwd(dys, proj, ys_pre, pm, pmt, glu_g, glu_b)
    dus, dbb, dcc, dabar, dd, p_glu, p_wbp, p_wbs, p_wout, p_pool_w, p_w_in = _ssm_bwd(
        dy_pre, proj, states, carries, pm, pmt, wb, wct, ptab, dvec, [dglu_w, dwbp, dwbs, dwout], dpool_w, dw_in_rest)
    dw_in_ssm = _in_proj_bwd_w("in_proj_bwd_w_ssm", x2, [dus, dzs], mod3, norm_pre)

    small32 = jnp.concatenate([head_vec, dpool_scale, dglu_b, dd, jnp.broadcast_to(loss_part, (1, D)),
                               jnp.zeros((2, D), F32), dabar.reshape(8, D)], axis=0)
    small16 = jnp.concatenate([dbb.reshape(2 * GROUPS, D), dcc.reshape(2 * GROUPS, D)], axis=0).astype(BF16)
    grad_x, p_w_in, p_small32, p_small16, p_pre = _in_proj_bwd_x(
        x2, dy, dpp, dus, dzs, dpg, mod3, norm_pre, w_in_g, dw_in_ssm, small32, small16, p_w_in)

    tot32, tot16, tot_pre = _sum_small([p_small32, p_small16, p_pre])
    d_abar_re, d_abar_im = _state_unlayout(tot32[8:16].reshape(N_STATE))
    d_bb_re, d_bb_im = tot16[0:64].reshape(GROUPS, G_H, G_P), tot16[64:128].reshape(GROUPS, G_H, G_P)
    g_a_re, g_a_im, g_log_dt, g_b_re_t, g_b_im_t = _s5_prep_bwd(
        a_re, a_im, log_dt, b_re_t, b_im_t, d_abar_re, d_abar_im, d_bb_re, d_bb_im)

    grads, deltas, new_m, new_v = {}, {}, {}, {}

    small = []

    def small_update(name, g2):
        small.append((name, g2))

    def shard_update(name, parts):
        shape = given[name].shape
        r2 = parts.shape[1:] if parts.ndim == 3 else (parts.shape[1] * parts.shape[2], parts.shape[3])
        w2, m2, v2 = (given[p + name].reshape(r2) for p in ("", "m_", "v_"))
        out = _adamw_reduce("adamw_" + name, parts.reshape((N_DEV,) + tuple(r2)), w2, m2, v2)
        grads[name], deltas[name], new_m[name], new_v[name] = (a.reshape(shape) for a in out)

    dmod_all = jnp.concatenate([p_pre[:, 0:2, :], p_small32[:, 0:1, :]], axis=1).reshape(N_DEV, 3 * D)
    dmod_cols = lax.dynamic_slice(dmod_all, (0, me * ada_cols), (N_DEV, ada_cols))
    out = _ada_update(c_all, dmod_cols, w_ada[0], m_w_ada[0], v_w_ada[0])
    grads['w_ada'], deltas['w_ada'], new_m['w_ada'], new_v['w_ada'] = (a.reshape(w_ada.shape) for a in out)

    small_update('b_ada', jnp.concatenate([tot_pre[0:2], tot32[0:1]], axis=0).reshape(1, 3 * D))
    small_update('norm_pre', tot_pre[2:3])
    small_update('norm_post', tot32[1:2])
    small_update('pool_scale', tot32[2:3])
    small_update('glu_b', tot32[3:4])
    small_update('ssm_d', tot32[4:5])
    small_update('ssm_a_re', g_a_re)
    small_update('ssm_a_im', g_a_im)
    small_update('ssm_log_dt', g_log_dt.reshape(1, GROUPS))
    small_update('ssm_b_re', g_b_re_t.transpose(0, 2, 1).reshape(GROUPS, G_P * G_H))
    small_update('ssm_b_im', g_b_im_t.transpose(0, 2, 1).reshape(GROUPS, G_P * G_H))
    small_update('ssm_c_re', tot16[128:192])
    small_update('ssm_c_im', -tot16[192:256])
    flat = _adamw_small([g2 for _, g2 in small],
                        *[[given[p + name].reshape(g2.shape) for name, g2 in small] for p in ("", "m_", "v_")])
    for t, (name, g2) in enumerate(small):
        shape = given[name].shape
        grads[name], deltas[name], new_m[name], new_v[name] = (
            a.reshape(shape) for a in (g2, *flat[3 * t:3 * t + 3]))
    shard_update('w_in', p_w_in)
    shard_update('pool_w', p_pool_w)
    shard_update('glu_w', p_glu)
    shard_update('w_branch_pool', p_wbp)
    shard_update('w_branch_ssm', p_wbs)
    shard_update('w_out', p_wout)

    return (tot32[5, 0], grad_x[None], *[grads[n] for n in WEIGHTS], *[deltas[n] for n in WEIGHTS],
            *[new_m[n] for n in WEIGHTS], *[new_v[n] for n in WEIGHTS])
```

```python
import functools
import math
from typing import Callable, NamedTuple, Optional

import jax
import jax.numpy as jnp
from jax import lax
from jax.experimental import pallas as pl
from jax.experimental.pallas import tpu as pltpu

F32 = jnp.float32
BF16 = jnp.bfloat16
MESH = pl.DeviceIdType.MESH

D = 1024
N_DEV = 8
N_IN = 6 * D
GROUPS = 64
G_H = 16
G_P = 64
N_Q = 4
Q_W = 2 * 16 * G_P
N_STATE = N_Q * Q_W
POOL_WINDOWS = (2, 4, 8, 16)
HALO = 16
RMS_EPS = 1e-6
SUBLANES = 8
LANE_CHUNK = 512
SCAN_UNROLL = 2
VMEM_LIMIT = 56 * 1024 * 1024

ADAM_LR = 0.001
ADAM_B1 = 0.9
ADAM_B2 = 0.999
ADAM_EPS = 1e-08
ADAM_WD = 0.01
ADAM_STEP = 10

WEIGHTS = ['w_ada', 'b_ada', 'norm_pre', 'norm_post', 'w_in', 'pool_w', 'pool_scale', 'ssm_a_re',
           'ssm_a_im', 'ssm_log_dt', 'ssm_b_re', 'ssm_b_im', 'ssm_c_re', 'ssm_c_im', 'ssm_d', 'glu_w',
           'glu_b', 'w_branch_pool', 'w_branch_ssm', 'w_out']


def _pcall(body, **kw):
    return pl.pallas_call(body, **kw)


def _params(sem=None, vmem=VMEM_LIMIT):
    return pltpu.CompilerParams(dimension_semantics=sem, vmem_limit_bytes=vmem)


def _tb(rows, pref):
    return pref if rows % pref == 0 and rows // pref >= 2 else rows // 2


def _full(shape, single=False):
    nd = len(shape)
    if single:
        return pl.BlockSpec(shape, lambda i: (0,) * nd, pipeline_mode=pl.Buffered(1))
    return pl.BlockSpec(shape, lambda i: (0,) * nd)


ANY = pl.BlockSpec(memory_space=pl.ANY)


def _me():
    return lax.axis_index("x"), lax.axis_index("y"), lax.axis_index("c")


def _flat(p):
    return 4 * p[0] + 2 * p[1] + p[2]


def _peer(k):
    x, y, c = _me()
    return (1 - x if k & 4 else x, 1 - y if k & 2 else y, 1 - c if k & 1 else c)


def _silu_parts(z):
    s = jax.nn.sigmoid(z)
    return z * s, s * (1.0 + z * (1.0 - s))


_GELU_C = math.sqrt(2.0 / math.pi)


def _gelu_parts(x):
    x2 = x * x
    t = jnp.tanh(_GELU_C * (x + 0.044715 * x * x2))
    g = 0.5 * x * (1.0 + t)
    dg = 0.5 * (1.0 + t) + 0.5 * x * (1.0 - t * t) * (_GELU_C * (1.0 + 3.0 * 0.044715 * x2))
    return g, dg


def _dot(a, b):
    return jnp.dot(a, b, preferred_element_type=F32)


def _dot_nt(a, b):
    return lax.dot_general(a, b, (((1,), (1,)), ((), ())), preferred_element_type=F32)


def _dot_tn(a, b):
    return lax.dot_general(a, b, (((0,), (0,)), ((), ())), preferred_element_type=F32)


def _rms_parts(x):
    r = lax.rsqrt(jnp.mean(x * x, axis=-1, keepdims=True) + RMS_EPS)
    return x * r, r


def _rms_bwd(dxn, xn, r):
    return r * (dxn - xn * jnp.mean(dxn * xn, axis=-1, keepdims=True))


def _ada_exchange(c, w_ada_s, b_ada_s):
    cols = w_ada_s.shape[1]

    def body(c_ref, w_ref, b_ref, call_ref, mod_ref, part_ref, ssem, rsem, lsem):
        me3 = _me()
        me = _flat(me3)
        mine = pltpu.make_async_copy(c_ref, call_ref.at[pl.ds(me, 1), :], lsem.at[0])
        mine.start()
        sends = []
        for k in range(1, N_DEV):
            cp = pltpu.make_async_remote_copy(src_ref=c_ref, dst_ref=call_ref.at[pl.ds(me, 1), :],
                                              send_sem=ssem.at[k - 1], recv_sem=rsem.at[k - 1],
                                              device_id=_peer(k), device_id_type=MESH)
            cp.start()
            sends.append(cp)
        mine.wait()
        for k in range(1, N_DEV):
            p = _flat(_peer(k))
            pltpu.make_async_remote_copy(src_ref=c_ref, dst_ref=call_ref.at[pl.ds(p, 1), :],
                                         send_sem=ssem.at[k - 1], recv_sem=rsem.at[k - 1],
                                         device_id=_peer(k), device_id_type=MESH).wait_recv()
        for cp in sends:
            cp.wait_send()
        ca = call_ref[...]
        act = ca * jax.nn.sigmoid(ca)
        part_ref[...] = jnp.dot(act, w_ref[...], preferred_element_type=F32,
                                precision=lax.Precision.HIGHEST) + b_ref[...]
        own = pltpu.make_async_copy(part_ref.at[pl.ds(me, 1), :], mod_ref.at[pl.ds(me, 1), :], lsem.at[1])
        own.start()
        sends = []
        for k in range(1, N_DEV):
            p = _flat(_peer(k))
            s = N_DEV - 1 + k - 1
            cp = pltpu.make_async_remote_copy(src_ref=part_ref.at[pl.ds(p, 1), :],
                                              dst_ref=mod_ref.at[pl.ds(me, 1), :],
                                              send_sem=ssem.at[s], recv_sem=rsem.at[s],
                                              device_id=_peer(k), device_id_type=MESH)
            cp.start()
            sends.append(cp)
        own.wait()
        for k in range(1, N_DEV):
            p = _flat(_peer(k))
            s = N_DEV - 1 + k - 1
            pltpu.make_async_remote_copy(src_ref=part_ref.at[pl.ds(p, 1), :],
                                         dst_ref=mod_ref.at[pl.ds(p, 1), :],
                                         send_sem=ssem.at[s], recv_sem=rsem.at[s],
                                         device_id=_peer(k), device_id_type=MESH).wait_recv()
        for cp in sends:
            cp.wait_send()

    vm = pl.BlockSpec(memory_space=pltpu.VMEM)
    return _pcall(
        body, name="ada_exchange",
        out_shape=(jax.ShapeDtypeStruct((N_DEV, D), F32), jax.ShapeDtypeStruct((N_DEV, cols), F32)),
        in_specs=[vm, vm, vm], out_specs=(vm, vm),
        scratch_shapes=[pltpu.VMEM((N_DEV, cols), F32),
                        pltpu.SemaphoreType.DMA((2 * (N_DEV - 1),)),
                        pltpu.SemaphoreType.DMA((2 * (N_DEV - 1),)),
                        pltpu.SemaphoreType.DMA((2,))],
    )(c, w_ada_s, b_ada_s)


class _Item(NamedTuple):
    src: int
    out: int
    src_view: Callable
    dst_view: Callable
    pred: Optional[Callable] = None


def _when(pred, dest, fn):
    if pred is None:
        fn()
    else:
        pl.when(pred(dest))(fn)


def _n_sems(items):
    return len(items) * (N_DEV - 1)


def _hosted_copies(items, srcs, outs, ssem, rsem, lsem, act):
    me = _flat(_me())
    for t, it in enumerate(items):
        local = lambda t=t, it=it: pltpu.make_async_copy(
            it.src_view(srcs[it.src], me), it.dst_view(outs[it.out], me), lsem.at[t])
        if act == "start":
            _when(it.pred, me, lambda local=local: local().start())
        else:
            _when(it.pred, me, lambda local=local: local().wait())
    for k in range(1, N_DEV):
        p3 = _peer(k)
        p = _flat(p3)
        for t, it in enumerate(items):
            s = t * (N_DEV - 1) + k - 1
            send = lambda it=it, s=s, p=p, p3=p3: pltpu.make_async_remote_copy(
                src_ref=it.src_view(srcs[it.src], p), dst_ref=it.dst_view(outs[it.out], me),
                send_sem=ssem.at[s], recv_sem=rsem.at[s], device_id=p3, device_id_type=MESH)
            recv = lambda it=it, s=s, p=p, p3=p3: pltpu.make_async_remote_copy(
                src_ref=it.src_view(srcs[it.src], p), dst_ref=it.dst_view(outs[it.out], p),
                send_sem=ssem.at[s], recv_sem=rsem.at[s], device_id=p3, device_id_type=MESH)
            if act == "start":
                _when(it.pred, p, lambda send=send: send().start())
            else:
                _when(it.pred, me, lambda recv=recv: recv().wait_recv())
                _when(it.pred, p, lambda send=send: send().wait_send())


def _sem_scratch(items):
    return [pltpu.SemaphoreType.DMA((_n_sems(items),)), pltpu.SemaphoreType.DMA((_n_sems(items),)),
            pltpu.SemaphoreType.DMA((len(items),))]


def _exchange(name, srcs, out_structs, items):
    n_src, n_out = len(srcs), len(out_structs)

    def body(*refs):
        src_refs, out_refs = refs[:n_src], refs[n_src:n_src + n_out]
        sems = refs[n_src + n_out:]
        _hosted_copies(items, src_refs, out_refs, *sems, act="start")
        _hosted_copies(items, src_refs, out_refs, *sems, act="wait")

    return _pcall(body, name=name, out_shape=tuple(out_structs),
                  in_specs=[ANY] * n_src, out_specs=tuple([ANY] * n_out),
                  scratch_shapes=_sem_scratch(items))(*srcs)


def _whole(ref, dest):
    return ref


def _slot(ref, sender):
    return ref.at[sender]


def _rows_of(rows):
    return lambda ref, dev: ref.at[pl.ds(dev * rows, rows), :]


def _cols_of(cols):
    return lambda ref, dev: ref.at[:, pl.ds(dev * cols, cols)]


def _pool_rows_of(rows):
    return lambda ref, dev: ref.at[:, pl.ds(dev * rows, rows), :]


def _gather_item(src, out, dst_view):
    return _Item(src, out, _whole, dst_view)


def _scatter_item(src, out, src_view):
    return _Item(src, out, src_view, _slot)


W_IN_BLOCK = 256
W_IN_SHARD = N_IN // N_DEV
SSM_BLOCKS = (2 * D // W_IN_BLOCK, 4 * D // W_IN_BLOCK)


def _w_in_block_item(src, out, j, ssm_part):
    def block(dest):
        return (W_IN_SHARD // W_IN_BLOCK) * dest + j

    def in_ssm(dest):
        b = block(dest)
        return (b >= SSM_BLOCKS[0]) & (b < SSM_BLOCKS[1])

    def src_view(ref, dest):
        b = block(dest)
        local = b - SSM_BLOCKS[0] if ssm_part else jnp.where(b < SSM_BLOCKS[0], b, b - (SSM_BLOCKS[1] - SSM_BLOCKS[0]))
        local = jnp.clip(local, 0, ref.shape[1] // W_IN_BLOCK - 1)
        return ref.at[:, pl.ds(local * W_IN_BLOCK, W_IN_BLOCK)]

    def dst_view(ref, sender):
        return ref.at[sender, :, pl.ds(j * W_IN_BLOCK, W_IN_BLOCK)]

    pred = in_ssm if ssm_part else (lambda dest: jnp.logical_not(in_ssm(dest)))
    return _Item(src, out, src_view, dst_view, pred)


def _cast_shards(arrs):
    def body(*refs):
        n = len(refs) // 2
        for i in range(n):
            refs[n + i][...] = refs[i][...].astype(BF16)

    vm = pl.BlockSpec(memory_space=pltpu.VMEM)
    return _pcall(body, name="cast_shards",
                  out_shape=tuple(jax.ShapeDtypeStruct(a.shape, BF16) for a in arrs),
                  in_specs=[vm] * len(arrs), out_specs=tuple([vm] * len(arrs)),
                  compiler_params=_params())(*arrs)


def _s5_discretise(a_re, a_im, log_dt, b_re_t, b_im_t):
    dt = jnp.exp(log_dt)
    lam_re = jnp.minimum(a_re, -1e-4)
    lam_im = a_im
    mag = jnp.exp(lam_re * dt)
    abar_re = mag * jnp.cos(lam_im * dt)
    abar_im = mag * jnp.sin(lam_im * dt)
    den = lam_re * lam_re + lam_im * lam_im
    num_re = abar_re - 1.0
    f_re = (num_re * lam_re + abar_im * lam_im) / den
    f_im = (abar_im * lam_re - num_re * lam_im) / den
    f_re, f_im = f_re[:, None, :], f_im[:, None, :]
    bb_re = f_re * b_re_t - f_im * b_im_t
    bb_im = f_re * b_im_t + f_im * b_re_t
    return abar_re, abar_im, bb_re, bb_im


def _group_masks():
    spread = lax.broadcasted_iota(jnp.int32, (G_P, 16 * G_P), 1) % G_P == lax.broadcasted_iota(
        jnp.int32, (G_P, 16 * G_P), 0)
    own = lax.broadcasted_iota(jnp.int32, (16 * G_H, 16 * G_P), 0) // G_H == lax.broadcasted_iota(
        jnp.int32, (16 * G_H, 16 * G_P), 1) // G_P
    return spread, own


def _s5_prep(a_re, a_im, log_dt, b_re_t, b_im_t, c_re, c_im, n_pow):
    def body(ar_ref, ai_ref, ld_ref, br_ref, bi_ref, cr_ref, ci_ref, wb_ref, wct_ref, pr_ref, pi_ref):
        abar_re, abar_im, bb_re, bb_im = _s5_discretise(ar_ref[...], ai_ref[...], ld_ref[...], br_ref[...], bi_ref[...])
        spread, own = _group_masks()
        spread = spread.astype(BF16)
        for ref, parts in ((wb_ref, (bb_re, bb_im)), (wct_ref, (cr_ref[...], -ci_ref[...]))):
            for half, t in enumerate(parts):
                for q in range(N_Q):
                    blocks = t[q * 16:(q + 1) * 16].reshape(16 * G_H, G_P).astype(BF16)
                    dense = jnp.where(own, _dot(blocks, spread), 0.0)
                    ref[q, :, half * (Q_W // 2):(half + 1) * (Q_W // 2)] = dense.astype(BF16)
        p_re, p_im = abar_re, abar_im
        pr_ref[0] = p_re
        pi_ref[0] = p_im
        for k in range(1, n_pow):
            p_re, p_im = p_re * abar_re - p_im * abar_im, p_re * abar_im + p_im * abar_re
            pr_ref[k] = p_re
            pi_ref[k] = p_im

    vm = pl.BlockSpec(memory_space=pltpu.VMEM)
    return _pcall(body, name="s5_prep",
                  out_shape=(jax.ShapeDtypeStruct((N_Q, 16 * G_H, Q_W), BF16),
                             jax.ShapeDtypeStruct((N_Q, 16 * G_H, Q_W), BF16),
                             jax.ShapeDtypeStruct((n_pow, GROUPS, G_P), F32),
                             jax.ShapeDtypeStruct((n_pow, GROUPS, G_P), F32)),
                  in_specs=[vm] * 7, out_specs=(vm, vm, vm, vm), compiler_params=_params(),
                  )(a_re, a_im, log_dt, b_re_t, b_im_t, c_re, c_im)


def _s5_prep_bwd(a_re, a_im, log_dt, b_re_t, b_im_t, d_abar_re, d_abar_im, d_bb_re, d_bb_im):
    def body(ar_ref, ai_ref, ld_ref, br_ref, bi_ref, dar_ref, dai_ref, dbr_ref, dbi_ref,
             gar_ref, gai_ref, gld_ref, gbr_ref, gbi_ref):
        _, vjp = jax.vjp(_s5_discretise, ar_ref[...], ai_ref[...], ld_ref[...], br_ref[...], bi_ref[...])
        g = vjp((dar_ref[...], dai_ref[...], dbr_ref[...], dbi_ref[...]))
        gar_ref[...] = g[0]
        gai_ref[...] = g[1]
        gld_ref[...] = g[2]
        gbr_ref[...] = g[3]
        gbi_ref[...] = g[4]

    vm = pl.BlockSpec(memory_space=pltpu.VMEM)
    ins = (a_re, a_im, log_dt, b_re_t, b_im_t)
    return _pcall(body, name="s5_prep_bwd",
                  out_shape=tuple(jax.ShapeDtypeStruct(a.shape, F32) for a in ins),
                  in_specs=[vm] * 9, out_specs=tuple([vm] * 5), compiler_params=_params(),
                  )(*ins, d_abar_re, d_abar_im, d_bb_re, d_bb_im)


def _state_layout(re, im):
    lead = re.shape[:-2]
    r = re.reshape(lead + (N_Q, 1, 16 * G_P))
    i = im.reshape(lead + (N_Q, 1, 16 * G_P))
    return jnp.concatenate([r, i], axis=-2).reshape(lead + (N_STATE,))


def _state_unlayout(v):
    v4 = v.reshape(N_Q, 2, 16, G_P)
    return v4[:, 0].reshape(GROUPS, G_P), v4[:, 1].reshape(GROUPS, G_P)


def _perm_matrix(tb):
    k_steps = tb // SUBLANES
    r = jnp.arange(tb)
    src = (r % SUBLANES) * k_steps + r // SUBLANES
    return (src[:, None] == jnp.arange(tb)[None, :]).astype(BF16)


def _lane_chunks(q):
    for lc in range(Q_W // 2 // LANE_CHUNK):
        re = q * Q_W + lc * LANE_CHUNK
        yield re, re + Q_W // 2


def _steps(lo, hi, body, init):
    if hi - lo <= SCAN_UNROLL:
        for k in range(lo, hi):
            init = body(k, init)
        return init
    trips = (hi - lo) // SCAN_UNROLL

    def trip(j, carry):
        for u in range(SCAN_UNROLL):
            carry = body(lo + j * SCAN_UNROLL + u, carry)
        return carry

    carry = lax.fori_loop(0, trips, trip, init)
    for k in range(lo + trips * SCAN_UNROLL, hi):
        carry = body(k, carry)
    return carry


def _tile(k):
    if isinstance(k, int):
        return pl.ds(k * SUBLANES, SUBLANES)
    return pl.ds(pl.multiple_of(k * SUBLANES, SUBLANES), SUBLANES)


def _scan_forward(q, s_ref, p_ref, carry_ref, enter_ref, fin_ref, k_steps):
    for re, im in _lane_chunks(q):
        lr, li = pl.ds(re, LANE_CHUNK), pl.ds(im, LANE_CHUNK)
        a_re = jnp.broadcast_to(p_ref[0:1, lr], (SUBLANES, LANE_CHUNK))
        a_im = jnp.broadcast_to(p_ref[0:1, li], (SUBLANES, LANE_CHUNK))

        def local(k, st):
            sr, si = st
            rows = _tile(k)
            nr = a_re * sr - a_im * si + s_ref[rows, lr]
            ni = a_re * si + a_im * sr + s_ref[rows, li]
            s_ref[rows, lr] = nr
            s_ref[rows, li] = ni
            return nr, ni

        zero = jnp.zeros((SUBLANES, LANE_CHUNK), F32)
        fr, fi = _steps(0, k_steps, local, (zero, zero))
        fin_ref[:, lr] = fr
        fin_ref[:, li] = fi
        ak_re, ak_im = p_ref[k_steps - 1:k_steps, lr], p_ref[k_steps - 1:k_steps, li]
        c_re, c_im = carry_ref[:, lr], carry_ref[:, li]
        for seg in range(SUBLANES):
            enter_ref[seg:seg + 1, lr] = c_re
            enter_ref[seg:seg + 1, li] = c_im
            f_re, f_im = fin_ref[seg:seg + 1, lr], fin_ref[seg:seg + 1, li]
            c_re, c_im = f_re + ak_re * c_re - ak_im * c_im, f_im + ak_re * c_im + ak_im * c_re
        carry_ref[:, lr] = c_re
        carry_ref[:, li] = c_im
        e_re, e_im = enter_ref[:, lr], enter_ref[:, li]

        def fix(k, _):
            rows = _tile(k)
            p_re = p_ref[pl.ds(k, 1), lr]
            p_im = p_ref[pl.ds(k, 1), li]
            s_ref[rows, lr] = s_ref[rows, lr] + (p_re * e_re - p_im * e_im)
            s_ref[rows, li] = s_ref[rows, li] + (p_re * e_im + p_im * e_re)
            return 0

        _steps(0, k_steps, fix, 0)


def _scan_backward(q, g_ref, s_ref, p_ref, carry_ref, s_in_ref, fin_ref, da_ref, k_steps):
    seg_id = lax.broadcasted_iota(jnp.int32, (SUBLANES, LANE_CHUNK), 0)
    for re, im in _lane_chunks(q):
        lr, li = pl.ds(re, LANE_CHUNK), pl.ds(im, LANE_CHUNK)
        a_re = jnp.broadcast_to(p_ref[0:1, lr], (SUBLANES, LANE_CHUNK))
        a_im = jnp.broadcast_to(p_ref[0:1, li], (SUBLANES, LANE_CHUNK))

        def local(j, st):
            sr, si = st
            rows = _tile(k_steps - 1 - j)
            nr = a_re * sr + a_im * si + g_ref[rows, lr]
            ni = a_re * si - a_im * sr + g_ref[rows, li]
            g_ref[rows, lr] = nr
            g_ref[rows, li] = ni
            return nr, ni

        zero = jnp.zeros((SUBLANES, LANE_CHUNK), F32)
        fr, fi = _steps(0, k_steps, local, (zero, zero))
        fin_ref[:, lr] = fr
        fin_ref[:, li] = fi
        ak_re, ak_im = p_ref[k_steps - 1:k_steps, lr], p_ref[k_steps - 1:k_steps, li]
        c_re, c_im = carry_ref[:, lr], carry_ref[:, li]
        lam_in = [None] * SUBLANES
        for seg in reversed(range(SUBLANES)):
            lam_in[seg] = (c_re, c_im)
            f_re, f_im = fin_ref[seg:seg + 1, lr], fin_ref[seg:seg + 1, li]
            c_re, c_im = f_re + ak_re * c_re + ak_im * c_im, f_im + ak_re * c_im - ak_im * c_re
        carry_ref[:, lr] = c_re
        carry_ref[:, li] = c_im
        for seg in range(SUBLANES):
            fin_ref[seg:seg + 1, lr] = lam_in[seg][0]
            fin_ref[seg:seg + 1, li] = lam_in[seg][1]
        e_re, e_im = fin_ref[:, lr], fin_ref[:, li]

        def fix_with(k, acc, sp_re, sp_im):
            acc_re, acc_im = acc
            rows = _tile(k)
            p_re = p_ref[pl.ds(k_steps - 1 - k, 1), lr]
            p_im = p_ref[pl.ds(k_steps - 1 - k, 1), li]
            l_re = g_ref[rows, lr] + (p_re * e_re + p_im * e_im)
            l_im = g_ref[rows, li] + (p_re * e_im - p_im * e_re)
            g_ref[rows, lr] = l_re
            g_ref[rows, li] = l_im
            return acc_re + (l_re * sp_re + l_im * sp_im), acc_im + (l_im * sp_re - l_re * sp_im)

        def fix(k, acc):
            prev = _tile(k - 1)
            return fix_with(k, acc, s_ref[prev, lr], s_ref[prev, li])

        last = _tile(k_steps - 1)
        before_re = jnp.where(seg_id == 0, s_in_ref[:, lr], pltpu.roll(s_ref[last, lr], 1, axis=0))
        before_im = jnp.where(seg_id == 0, s_in_ref[:, li], pltpu.roll(s_ref[last, li], 1, axis=0))
        acc = fix_with(0, (zero, zero), before_re, before_im)
        acc_re, acc_im = _steps(1, k_steps, fix, acc)
        da_ref[:, lr] = da_ref[:, lr] + jnp.sum(acc_re, axis=0, keepdims=True)
        da_ref[:, li] = da_ref[:, li] + jnp.sum(acc_im, axis=0, keepdims=True)


def _prenorm(x, mod3, norm_pre):
    xn, r = _rms_parts(x)
    return xn, r, xn * norm_pre * (1.0 + mod3[1:2, :]) + mod3[0:1, :]


CHIP_FLIPS = (4, 2, 6)


def _shard_order(me):
    flips = [0, 1] + [f + c for f in CHIP_FLIPS for c in (0, 1)]
    return jnp.stack([me ^ f for f in flips]).astype(jnp.int32)


def _in_proj(x, mod3, norm_pre, w_in_s, shards):
    rows = x.shape[0]
    tb = _tb(rows, 2048)
    nblk = rows // tb
    n_sh = len(shards)
    last_step = N_DEV - 1
    items = [_gather_item(0, 0, _pool_rows_of(shards[0].shape[1]))] + \
            [_gather_item(t, t, _rows_of(shards[t].shape[0])) for t in range(1, n_sh)]

    def body(order_ref, x_ref, mod_ref, np_ref, w_src, *rest):
        src_refs, proj_ref, w_full, out_refs = rest[:n_sh], rest[n_sh], rest[n_sh + 1], rest[n_sh + 2:2 * n_sh + 2]
        h_scr, wg, ssem, rsem, lsem, *sems = rest[2 * n_sh + 2:]
        s, i = pl.program_id(0), pl.program_id(1)
        me3 = _me()
        me = _flat(me3)
        sibling = _peer(1)

        def own_copy(slot, k):
            return pltpu.make_async_remote_copy(src_ref=w_src, dst_ref=wg.at[me], send_sem=ssem.at[slot],
                                                recv_sem=rsem.at[slot], device_id=_peer(k), device_id_type=MESH)

        def passed_copy(j):
            p = _flat(_peer(CHIP_FLIPS[j]))
            return pltpu.make_async_remote_copy(src_ref=wg.at[p], dst_ref=wg.at[p], send_sem=ssem.at[4 + j],
                                                recv_sem=rsem.at[4 + j], device_id=sibling, device_id_type=MESH)

        def arrival(slot, flip):
            p = _flat(_peer(flip))
            pltpu.make_async_remote_copy(src_ref=w_src, dst_ref=wg.at[p], send_sem=ssem.at[slot],
                                         recv_sem=rsem.at[slot], device_id=sibling, device_id_type=MESH).wait_recv()

        def keep(t):
            p = order_ref[t]
            return pltpu.make_async_copy(wg.at[p], w_full.at[:, pl.ds(p * W_IN_SHARD, W_IN_SHARD)], lsem.at[1 + t])

        first = i == 0
        for t in range(last_step):
            pl.when(first & (s == t + 1))(lambda t=t: keep(t).start())

        @pl.when(first & (s == 0))
        def _():
            mine = pltpu.make_async_copy(w_src, wg.at[me], lsem.at[0])
            mine.start()
            own_copy(0, 1).start()
            for j, f in enumerate(CHIP_FLIPS[:2]):
                own_copy(1 + j, f).start()
            mine.wait()

        @pl.when(first & (s == 1))
        def _():
            arrival(0, 1)

        for j, f in enumerate(CHIP_FLIPS):
            @pl.when(first & (s == 2 + 2 * j))
            def _(j=j, f=f):
                arrival(1 + j, f)
                passed_copy(j).start()
                if j == 0:
                    own_copy(3, CHIP_FLIPS[2]).start()

            @pl.when(first & (s == 3 + 2 * j))
            def _(j=j, f=f):
                arrival(4 + j, f + 1)

        @pl.when(first & (s == last_step - 1))
        def _():
            _hosted_copies(items, src_refs, out_refs, *sems, act="start")

        rows_i = pl.ds(pl.multiple_of(i * tb, tb), tb)

        @pl.when(s == 0)
        def _():
            _, _, h = _prenorm(x_ref[...], mod_ref[...], np_ref[...])
            h_scr[rows_i, :] = h.astype(BF16)

        proj_ref[...] = _dot(h_scr[rows_i, :], wg[order_ref[s]]).astype(BF16)

        @pl.when((s == last_step) & (i == nblk - 1))
        def _():
            own_copy(0, 1).wait_send()
            for j, f in enumerate(CHIP_FLIPS):
                own_copy(1 + j, f).wait_send()
                passed_copy(j).wait_send()
            keep(last_step).start()
            for t in range(N_DEV):
                keep(t).wait()
            _hosted_copies(items, src_refs, out_refs, *sems, act="wait")

    full = [jax.ShapeDtypeStruct((4, 256, 256), BF16)] + [jax.ShapeDtypeStruct((D, D), BF16)] * (n_sh - 1)
    grid_spec = pltpu.PrefetchScalarGridSpec(
        num_scalar_prefetch=1, grid=(N_DEV, nblk),
        in_specs=[pl.BlockSpec((tb, D), lambda s, i, order: (jnp.where(s == 0, i, nblk - 1), 0)),
                  pl.BlockSpec((3, D), lambda s, i, order: (0, 0)), pl.BlockSpec((1, D), lambda s, i, order: (0, 0)),
                  ANY] + [ANY] * n_sh,
        out_specs=(pl.BlockSpec((tb, W_IN_SHARD), lambda s, i, order: (i, order[s])), ANY, *([ANY] * n_sh)),
        scratch_shapes=[pltpu.VMEM((rows, D), BF16), pltpu.VMEM((N_DEV, D, W_IN_SHARD), BF16),
                        pltpu.SemaphoreType.DMA((N_DEV - 1,)), pltpu.SemaphoreType.DMA((N_DEV - 1,)),
                        pltpu.SemaphoreType.DMA((1 + N_DEV,))] + _sem_scratch(items))
    return _pcall(body, name="in_proj", grid_spec=grid_spec,
                  out_shape=(jax.ShapeDtypeStruct((rows, N_IN), BF16), jax.ShapeDtypeStruct((D, N_IN), BF16), *full),
                  compiler_params=_params(("arbitrary", "arbitrary")),
                  )(_shard_order(_flat(_me())), x, mod3, norm_pre, w_in_s, *shards)


def _pool_windows(ext, tb, first_row):
    pos = (first_row + lax.broadcasted_iota(jnp.int32, (tb, 1), 0) + 1).astype(F32)
    pooled, inv_counts = [], []
    for g, w in enumerate(POOL_WINDOWS):
        acc = ext[:, g * 256:(g + 1) * 256]
        tok = acc[HALO:, :]
        s = 1
        while s < w:
            acc = acc + pltpu.roll(acc, s, axis=0)
            s *= 2
        inv = 1.0 / jnp.minimum(pos, float(w))
        pooled.append(acc[HALO:, :] * inv - tok)
        inv_counts.append(inv)
    return pooled, inv_counts


def _pool_fwd(proj, pool_w, pool_scale):
    rows = proj.shape[0]
    tb = _tb(rows, 512)
    hb = tb // HALO

    def body(u_ref, halo_ref, z_ref, pw_ref, ps_ref, y_ref):
        i = pl.program_id(0)
        u = u_ref[...].astype(F32)
        halo = jnp.where(i > 0, halo_ref[...].astype(F32), 0.0)
        pooled, _ = _pool_windows(jnp.concatenate([halo, u], axis=0), tb, i * tb)
        silu_z, _ = _silu_parts(z_ref[...].astype(F32))
        for g in range(4):
            cols = slice(g * 256, (g + 1) * 256)
            mixed = _dot(pooled[g].astype(BF16), pw_ref[g])
            y_ref[:, cols] = (mixed * ps_ref[:, cols] * silu_z[:, cols]).astype(BF16)

    return _pcall(body, name="pool_fwd", grid=(rows // tb,),
                  out_shape=jax.ShapeDtypeStruct((rows, D), BF16),
                  in_specs=[pl.BlockSpec((tb, D), lambda i: (i, 0)),
                            pl.BlockSpec((HALO, D), lambda i: (jnp.maximum(i * hb - 1, 0), 0)),
                            pl.BlockSpec((tb, D), lambda i: (i, 1)),
                            _full((4, 256, 256)), _full((1, D))],
                  out_specs=pl.BlockSpec((tb, D), lambda i: (i, 0)),
                  compiler_params=_params(("arbitrary",)))(proj, proj, proj, pool_w, pool_scale)


def _ssm_fwd(proj, pm, pmt, wb, wct, ptab, dvec, glu_w, glu_b, shards):
    rows = proj.shape[0]
    tb = pm.shape[0]
    k_steps = tb // SUBLANES
    nblk = rows // tb
    n_sh = len(shards)
    items = [_gather_item(t, t, _rows_of(shards[t].shape[0])) for t in range(n_sh)]

    def body(u_ref, z_ref, pm_ref, pmt_ref, wb_ref, wct_ref, p_ref, d_ref, gw_ref, gb_ref, *rest):
        src_refs = rest[:n_sh]
        y_ref, ys_ref, carry_out_ref, s_ref = rest[n_sh:n_sh + 4]
        out_refs = rest[n_sh + 4:2 * n_sh + 4]
        carry_ref, enter_ref, fin_ref, *sems = rest[2 * n_sh + 4:]

        @pl.when(pl.program_id(0) == 0)
        def _():
            _hosted_copies(items, src_refs, out_refs, *sems, act="start")
            carry_ref[...] = jnp.zeros_like(carry_ref)

        carry_out_ref[...] = carry_ref[...]
        up = _dot(pm_ref[...], u_ref[...]).astype(BF16)

        for q in range(N_Q):
            s_ref[:, q * Q_W:(q + 1) * Q_W] = _dot(up[:, q * 256:(q + 1) * 256], wb_ref[q])
        for q in range(N_Q):
            _scan_forward(q, s_ref, p_ref, carry_ref, enter_ref, fin_ref, k_steps)
        for q in range(N_Q):
            cols = slice(q * 256, (q + 1) * 256)
            y = _dot_nt(s_ref[:, q * Q_W:(q + 1) * Q_W].astype(BF16), wct_ref[q])
            ys_ref[:, cols] = y + d_ref[:, cols] * up[:, cols].astype(F32)
        yg, _ = _gelu_parts(ys_ref[...])
        gate = jax.nn.sigmoid(_dot(yg.astype(BF16), gw_ref[...]) + gb_ref[...])
        zp = _dot(pm_ref[...], z_ref[...])
        silu_z, _ = _silu_parts(zp)
        y_ref[...] = _dot(pmt_ref[...], (yg * gate * silu_z).astype(BF16)).astype(BF16)

        @pl.when(pl.program_id(0) == nblk - 1)
        def _():
            _hosted_copies(items, src_refs, out_refs, *sems, act="wait")

    return _pcall(body, name="ssm_fwd", grid=(nblk,),
                  out_shape=(jax.ShapeDtypeStruct((rows, D), BF16), jax.ShapeDtypeStruct((rows, D), F32),
                             jax.ShapeDtypeStruct((nblk, 1, N_STATE), F32),
                             jax.ShapeDtypeStruct((rows, N_STATE), F32),
                             *[jax.ShapeDtypeStruct((D, D), BF16)] * n_sh),
                  in_specs=[pl.BlockSpec((tb, D), lambda i: (i, 2)), pl.BlockSpec((tb, D), lambda i: (i, 3)),
                            _full((tb, tb)), _full((tb, tb)),
                            _full((N_Q, 256, Q_W), single=True), _full((N_Q, 256, Q_W), single=True),
                            _full((k_steps, N_STATE)), _full((1, D)), _full((D, D), single=True), _full((1, D))] +
                           [ANY] * n_sh,
                  out_specs=(pl.BlockSpec((tb, D), lambda i: (i, 0)), pl.BlockSpec((tb, D), lambda i: (i, 0)),
                             pl.BlockSpec((None, 1, N_STATE), lambda i: (i, 0, 0)),
                             pl.BlockSpec((tb, N_STATE), lambda i: (i, 0)), *([ANY] * n_sh)),
                  scratch_shapes=[pltpu.VMEM((1, N_STATE), F32),
                                  pltpu.VMEM((SUBLANES, N_STATE), F32), pltpu.VMEM((SUBLANES, N_STATE), F32)] +
                                 _sem_scratch(items),
                  compiler_params=_params(("arbitrary",)))(proj, proj, pm, pmt, wb, wct, ptab, dvec, glu_w, glu_b,
                                                           *shards)


def _head(x, target, proj, y_pool, y_ssm, mod3, norm_post, wbp, wbs, wout):
    rows = x.shape[0]
    tb = _tb(rows, 256)
    nblk = rows // tb
    n_feat = float(D)

    def body(x_ref, t_ref, gp_ref, gs_ref, yp_ref, ys_ref, mod_ref, npost_ref, wbp_ref, wbs_ref, wout_ref,
             loss_ref, dy_ref, dyp_ref, dys_ref, dg_ref, dwbp_hbm, dwbs_hbm, dwout_hbm, vec_ref,
             acc_bp, acc_bs, acc_out, acc_loss, acc_vec):
        i = pl.program_id(0)

        @pl.when(i == 0)
        def _():
            acc_bp[...] = jnp.zeros_like(acc_bp)
            acc_bs[...] = jnp.zeros_like(acc_bs)
            acc_out[...] = jnp.zeros_like(acc_out)
            acc_loss[...] = jnp.zeros_like(acc_loss)
            acc_vec[...] = jnp.zeros_like(acc_vec)

        yp, ys = yp_ref[...], ys_ref[...]
        sgp = jax.nn.sigmoid(gp_ref[...].astype(F32))
        sgs = jax.nn.sigmoid(gs_ref[...].astype(F32))
        pb = _dot(yp, wbp_ref[...])
        psm = _dot(ys, wbs_ref[...])
        mb = (sgp * pb + sgs * psm).astype(BF16)
        out = _dot(mb, wout_ref[...])
        on, r = _rms_parts(out)
        gate = mod_ref[2:3, :]
        npost = npost_ref[...]
        normed = on * npost
        diff = x_ref[...] + gate * normed - t_ref[...]
        acc_loss[...] += jnp.sum(diff * diff, axis=0, keepdims=True)
        dy = diff * (1.0 / n_feat)
        dy_ref[...] = dy
        acc_vec[0:1, :] += jnp.sum(dy * normed, axis=0, keepdims=True)
        dn = dy * gate
        acc_vec[1:2, :] += jnp.sum(dn * on, axis=0, keepdims=True)
        dout = _rms_bwd(dn * npost, on, r).astype(BF16)
        acc_out[...] += _dot_tn(mb, dout)
        dm = _dot_nt(dout, wout_ref[...])
        dpb = (dm * sgp).astype(BF16)
        dps = (dm * sgs).astype(BF16)
        dg_ref[:, :D] = (dm * pb * sgp * (1.0 - sgp)).astype(BF16)
        dg_ref[:, D:] = (dm * psm * sgs * (1.0 - sgs)).astype(BF16)
        acc_bp[...] += _dot_tn(yp, dpb)
        acc_bs[...] += _dot_tn(ys, dps)
        dyp_ref[...] = _dot_nt(dpb, wbp_ref[...]).astype(BF16)
        dys_ref[...] = _dot_nt(dps, wbs_ref[...]).astype(BF16)

        @pl.when(i == nblk - 1)
        def _():
            loss_ref[...] = 0.5 / n_feat * jnp.sum(acc_loss[...], axis=1, keepdims=True)
            vec_ref[...] = acc_vec[...]
            pltpu.sync_copy(acc_bp, dwbp_hbm)
            pltpu.sync_copy(acc_bs, dwbs_hbm)
            pltpu.sync_copy(acc_out, dwout_hbm)

    row = lambda c: pl.BlockSpec((tb, D), lambda i: (i, c))
    w = _full((D, D), single=True)
    return _pcall(body, name="head", grid=(nblk,),
                  out_shape=(jax.ShapeDtypeStruct((1, 1), F32), jax.ShapeDtypeStruct((rows, D), F32),
                             jax.ShapeDtypeStruct((rows, D), BF16), jax.ShapeDtypeStruct((rows, D), BF16),
                             jax.ShapeDtypeStruct((rows, 2 * D), BF16),
                             jax.ShapeDtypeStruct((D, D), F32), jax.ShapeDtypeStruct((D, D), F32),
                             jax.ShapeDtypeStruct((D, D), F32), jax.ShapeDtypeStruct((2, D), F32)),
                  in_specs=[row(0), row(0), row(4), row(5), row(0), row(0), _full((3, D)), _full((1, D)), w, w, w],
                  out_specs=(_full((1, 1)), row(0), row(0), row(0), pl.BlockSpec((tb, 2 * D), lambda i: (i, 0)),
                             ANY, ANY, ANY, _full((2, D))),
                  scratch_shapes=[pltpu.VMEM((D, D), F32), pltpu.VMEM((D, D), F32), pltpu.VMEM((D, D), F32),
                                  pltpu.VMEM((1, D), F32), pltpu.VMEM((2, D), F32)],
                  compiler_params=_params(("arbitrary",)))(x, target, proj, proj, y_pool, y_ssm, mod3, norm_post,
                                                           wbp, wbs, wout)


def _glu_bwd(dys, proj, ys_pre, pm, pmt, glu_w, glu_b):
    rows = dys.shape[0]
    tb = pm.shape[0]
    nblk = rows // tb

    def body(dys_ref, z_ref, ysp_ref, pm_ref, pmt_ref, gw_ref, gb_ref, dyp_ref, dz_ref, dgw_hbm, dgb_ref,
             acc_w, acc_b):
        i = pl.program_id(0)

        @pl.when(i == 0)
        def _():
            acc_w[...] = jnp.zeros_like(acc_w)
            acc_b[...] = jnp.zeros_like(acc_b)

        d_out = _dot(pm_ref[...], dys_ref[...])
        z = _dot(pm_ref[...], z_ref[...])
        yg, dgelu = _gelu_parts(ysp_ref[...])
        ygb = yg.astype(BF16)
        sg = jax.nn.sigmoid(_dot(ygb, gw_ref[...]) + gb_ref[...])
        silu_z, dsilu_z = _silu_parts(z)
        dz = d_out * (yg * sg) * dsilu_z
        dz_ref[...] = _dot(pmt_ref[...], dz.astype(BF16)).astype(BF16)
        dglu = d_out * silu_z
        dq = dglu * yg * sg * (1.0 - sg)
        dqb = dq.astype(BF16)
        acc_b[...] += jnp.sum(dq, axis=0, keepdims=True)
        acc_w[...] += _dot_tn(ygb, dqb)
        dyg = dglu * sg + _dot_nt(dqb, gw_ref[...])
        dyp_ref[...] = (dyg * dgelu).astype(BF16)

        @pl.when(i == nblk - 1)
        def _():
            dgb_ref[...] = acc_b[...]
            pltpu.sync_copy(acc_w, dgw_hbm)

    row = lambda c: pl.BlockSpec((tb, D), lambda i: (i, c))
    return _pcall(body, name="glu_bwd", grid=(nblk,),
                  out_shape=(jax.ShapeDtypeStruct((rows, D), BF16), jax.ShapeDtypeStruct((rows, D), BF16),
                             jax.ShapeDtypeStruct((D, D), F32), jax.ShapeDtypeStruct((1, D), F32)),
                  in_specs=[row(0), row(3), row(0), _full((tb, tb)), _full((tb, tb)),
                            _full((D, D), single=True), _full((1, D))],
                  out_specs=(row(0), row(0), ANY, _full((1, D))),
                  scratch_shapes=[pltpu.VMEM((D, D), F32), pltpu.VMEM((1, D), F32)],
                  compiler_params=_params(("arbitrary",)))(dys, proj, ys_pre, pm, pmt, glu_w, glu_b)


def _ssm_bwd(dyp, proj, states, carries, pm, pmt, wb, wct, ptab, dvec, mat_grads, dpool_w, dw_in_rest):
    rows = dyp.shape[0]
    tb = pm.shape[0]
    k_steps = tb // SUBLANES
    nblk = rows // tb
    n_mat = len(mat_grads)
    hosted = [*mat_grads, dpool_w, dw_in_rest]
    n_h = len(hosted)
    shard_rows = D // N_DEV
    pool_rows = dpool_w.shape[1] // N_DEV
    items = [_scatter_item(t, t, _rows_of(shard_rows)) for t in range(n_mat)] + \
            [_scatter_item(n_mat, n_mat, _pool_rows_of(pool_rows))] + \
            [_w_in_block_item(n_mat + 1, n_mat + 1, j, ssm_part=False) for j in range(W_IN_SHARD // W_IN_BLOCK)]
    n_in, n_out = 10, 5

    def body(*refs):
        dyp_ref, u_ref, s_ref, cin_ref, pm_ref, pmt_ref, wb_ref, wct_ref, p_ref, d_ref = refs[:n_in]
        src_refs = refs[n_in:n_in + n_h]
        du_ref, dbb_ref, dcc_ref, da_ref, dd_ref = refs[n_in + n_h:n_in + n_h + n_out]
        recv_refs = refs[n_in + n_h + n_out:n_in + 2 * n_h + n_out]
        (g_ref, carry_b, fin_ref, acc_wb, acc_wct, acc_da, acc_dd, dup_ref,
         *sems) = refs[n_in + 2 * n_h + n_out:]
        i = pl.program_id(0)

        @pl.when(i == 0)
        def _():
            _hosted_copies(items, src_refs, recv_refs, *sems, act="start")
            carry_b[...] = jnp.zeros_like(carry_b)
            acc_wb[...] = jnp.zeros_like(acc_wb)
            acc_wct[...] = jnp.zeros_like(acc_wct)
            acc_da[...] = jnp.zeros_like(acc_da)
            acc_dd[...] = jnp.zeros_like(acc_dd)

        def keep_own(acc, q, prod):
            for gl in range(16):
                r, c = slice(gl * G_H, (gl + 1) * G_H), (gl // 2) * 128
                acc[q, r, 0:128] += prod[r, c:c + 128]
                acc[q, r, 128:256] += prod[r, Q_W // 2 + c:Q_W // 2 + c + 128]

        dy = dyp_ref[...]
        up = _dot(pm_ref[...], u_ref[...]).astype(BF16)
        acc_dd[...] += jnp.sum(dy.astype(F32) * up.astype(F32), axis=0, keepdims=True)
        for q in range(N_Q):
            cols = slice(q * 256, (q + 1) * 256)
            g_ref[:, q * Q_W:(q + 1) * Q_W] = _dot(dy[:, cols], wct_ref[q])
            keep_own(acc_wct, q, _dot_tn(dy[:, cols], s_ref[:, q * Q_W:(q + 1) * Q_W].astype(BF16)))
        for q in range(N_Q):
            _scan_backward(q, g_ref, s_ref, p_ref, carry_b, cin_ref, fin_ref, acc_da, k_steps)
        for q in range(N_Q):
            cols = slice(q * 256, (q + 1) * 256)
            lam = g_ref[:, q * Q_W:(q + 1) * Q_W].astype(BF16)
            keep_own(acc_wb, q, _dot_tn(up[:, cols], lam))
            dup_ref[:, cols] = (_dot_nt(lam, wb_ref[q]) + d_ref[:, cols] * dy[:, cols].astype(F32)).astype(BF16)
        du_ref[...] = _dot(pmt_ref[...], dup_ref[...]).astype(BF16)

        @pl.when(i == nblk - 1)
        def _():
            da_ref[...] = acc_da[...]
            dd_ref[...] = acc_dd[...]
            lane = lax.broadcasted_iota(jnp.int32, (16 * G_H, 128), 1)
            row = lax.broadcasted_iota(jnp.int32, (16 * G_H, 128), 0)
            own = lane // G_P == (row // G_H) % 2
            spread = (lax.broadcasted_iota(jnp.int32, (G_P, 128), 1) % G_P ==
                      lax.broadcasted_iota(jnp.int32, (G_P, 128), 0)).astype(F32)
            for acc, out in ((acc_wb, dbb_ref), (acc_wct, dcc_ref)):
                for half in range(2):
                    for q in range(N_Q):
                        kept = jnp.where(own, acc[q, :, half * 128:(half + 1) * 128], 0.0)
                        out[half, q] = lax.dot_general(kept, spread, (((1,), (1,)), ((), ())),
                                                       preferred_element_type=F32, precision=lax.Precision.HIGHEST)
            _hosted_copies(items, src_refs, recv_refs, *sems, act="wait")

    rev = lambda c: pl.BlockSpec((tb, D), lambda i: (nblk - 1 - i, c))
    recv = [jax.ShapeDtypeStruct((N_DEV, shard_rows, D), F32)] * n_mat + \
           [jax.ShapeDtypeStruct((N_DEV, dpool_w.shape[0], pool_rows, dpool_w.shape[2]), F32),
            jax.ShapeDtypeStruct((N_DEV, D, W_IN_SHARD), BF16)]
    return _pcall(body, name="ssm_bwd", grid=(nblk,),
                  out_shape=(jax.ShapeDtypeStruct((rows, D), BF16),
                             jax.ShapeDtypeStruct((2, N_Q, 16 * G_H, G_P), F32),
                             jax.ShapeDtypeStruct((2, N_Q, 16 * G_H, G_P), F32),
                             jax.ShapeDtypeStruct((1, N_STATE), F32), jax.ShapeDtypeStruct((1, D), F32), *recv),
                  in_specs=[rev(0), rev(2), pl.BlockSpec((tb, N_STATE), lambda i: (nblk - 1 - i, 0)),
                            pl.BlockSpec((None, 1, N_STATE), lambda i: (nblk - 1 - i, 0, 0)),
                            _full((tb, tb)), _full((tb, tb)),
                            _full((N_Q, 256, Q_W), single=True), _full((N_Q, 256, Q_W), single=True),
                            _full((k_steps, N_STATE)), _full((1, D))] + [ANY] * n_h,
                  out_specs=(rev(0), _full((2, N_Q, 16 * G_H, G_P)), _full((2, N_Q, 16 * G_H, G_P)),
                             _full((1, N_STATE)), _full((1, D)), *([ANY] * n_h)),
                  scratch_shapes=[pltpu.VMEM((tb, N_STATE), F32), pltpu.VMEM((1, N_STATE), F32),
                                  pltpu.VMEM((SUBLANES, N_STATE), F32),
                                  pltpu.VMEM((N_Q, 16 * G_H, 256), F32), pltpu.VMEM((N_Q, 16 * G_H, 256), F32),
                                  pltpu.VMEM((1, N_STATE), F32), pltpu.VMEM((1, D), F32),
                                  pltpu.VMEM((tb, D), BF16)] + _sem_scratch(items),
                  compiler_params=_params(("arbitrary",), vmem=60 * 1024 * 1024),
                  )(dyp, proj, states, carries, pm, pmt, wb, wct, ptab, dvec, *hosted)


def _pool_bwd(dyp, proj, pool_w, pool_scale):
    rows = dyp.shape[0]
    tb = _tb(rows, 512)
    nblk = rows // tb
    hb = tb // HALO

    def body(dy_ref, u_ref, halo_ref, z_ref, pw_ref, ps_ref, dp_ref, dpw_ref, dps_ref, ahead_ref):
        i = pl.program_id(0)
        blk = nblk - 1 - i

        @pl.when(i == 0)
        def _():
            ahead_ref[...] = jnp.zeros_like(ahead_ref)
            dpw_ref[...] = jnp.zeros_like(dpw_ref)
            dps_ref[...] = jnp.zeros_like(dps_ref)

        u = u_ref[...].astype(F32)
        halo = jnp.where(blk > 0, halo_ref[...].astype(F32), 0.0)
        pooled, inv_counts = _pool_windows(jnp.concatenate([halo, u], axis=0), tb, blk * tb)
        silu_z, dsilu_z = _silu_parts(z_ref[...].astype(F32))
        dy = dy_ref[...].astype(F32)
        for g, w in enumerate(POOL_WINDOWS):
            cols = slice(g * 256, (g + 1) * 256)
            pooled_b = pooled[g].astype(BF16)
            mixed = _dot(pooled_b, pw_ref[g])
            scale = ps_ref[:, cols]
            dp_ref[:, D + g * 256:D + (g + 1) * 256] = (dy[:, cols] * (mixed * scale) * dsilu_z[:, cols]).astype(BF16)
            dms = dy[:, cols] * silu_z[:, cols]
            dps_ref[:, cols] += jnp.sum(dms * mixed, axis=0, keepdims=True)
            dmixed = (dms * scale).astype(BF16)
            dpw_ref[g] += _dot_tn(pooled_b, dmixed)
            dpooled = _dot_nt(dmixed, pw_ref[g])
            ratio = dpooled * inv_counts[g]
            acc = jnp.concatenate([ratio, ahead_ref[:, cols]], axis=0)
            ahead_ref[:, cols] = ratio[:HALO, :]
            s = 1
            while s < w:
                acc = acc + pltpu.roll(acc, tb + HALO - s, axis=0)
                s *= 2
            dp_ref[:, cols] = (acc[:tb, :] - dpooled).astype(BF16)

    rev = lambda c: pl.BlockSpec((tb, D), lambda i: (nblk - 1 - i, c))
    return _pcall(body, name="pool_bwd", grid=(nblk,),
                  out_shape=(jax.ShapeDtypeStruct((rows, 2 * D), BF16), jax.ShapeDtypeStruct((4, 256, 256), F32),
                             jax.ShapeDtypeStruct((1, D), F32)),
                  in_specs=[rev(0), rev(0),
                            pl.BlockSpec((HALO, D), lambda i: (jnp.maximum((nblk - 1 - i) * hb - 1, 0), 0)),
                            rev(1), _full((4, 256, 256)), _full((1, D))],
                  out_specs=(pl.BlockSpec((tb, 2 * D), lambda i: (nblk - 1 - i, 0)), _full((4, 256, 256)),
                             _full((1, D))),
                  scratch_shapes=[pltpu.VMEM((HALO, D), F32)],
                  compiler_params=_params(("arbitrary",)))(dyp, proj, proj, proj, pool_w, pool_scale)


def _dproj_specs(tb):
    return [pl.BlockSpec((tb, 2 * D), lambda i: (i, 0)), pl.BlockSpec((tb, D), lambda i: (i, 0)),
            pl.BlockSpec((tb, D), lambda i: (i, 0)), pl.BlockSpec((tb, 2 * D), lambda i: (i, 0))]


def _in_proj_bwd_x(x, dy, dpp, dus, dzs, dpg, mod3, norm_pre, w_in, dw_in_ssm, small32, small16, recv_w_in):
    rows = x.shape[0]
    tb = _tb(rows, 512)
    nblk = rows // tb
    items = [_w_in_block_item(0, 0, j, ssm_part=True) for j in range(W_IN_SHARD // W_IN_BLOCK)] + \
            [_Item(1, 1, _whole, _slot), _Item(2, 2, _whole, _slot)]
    sums_item = [_Item(0, 0, _whole, _slot)]

    def body(x_ref, dy_ref, dpp_ref, dus_ref, dzs_ref, dpg_ref, mod_ref, np_ref, w_ref,
             dw_src, s32_src, s16_src, _, gx_ref, recv_w, recv32, recv16, recv_sums,
             vec_ref, ssem, rsem, lsem, *sums_sems):
        src_refs, recv_refs, sems = (dw_src, s32_src, s16_src), (recv_w, recv32, recv16), (ssem, rsem, lsem)

        @pl.when(pl.program_id(0) == 0)
        def _():
            _hosted_copies(items, src_refs, recv_refs, *sems, act="start")
            vec_ref[...] = jnp.zeros_like(vec_ref)

        dh = _dot_nt(dpp_ref[...], w_ref[:, 0:2 * D])
        dh += _dot_nt(dus_ref[...], w_ref[:, 2 * D:3 * D])
        dh += _dot_nt(dzs_ref[...], w_ref[:, 3 * D:4 * D])
        dh += _dot_nt(dpg_ref[...], w_ref[:, 4 * D:6 * D])
        xn, r, _ = _prenorm(x_ref[...], mod_ref[...], np_ref[...])
        one_scale = 1.0 + mod_ref[1:2, :]
        vec_ref[0:1, :] += jnp.sum(dh, axis=0, keepdims=True)
        vec_ref[1:2, :] += jnp.sum(dh * xn, axis=0, keepdims=True) * np_ref[...]
        vec_ref[2:3, :] += jnp.sum(dh * xn, axis=0, keepdims=True) * one_scale
        gx_ref[...] = dy_ref[...] + _rms_bwd(dh * (np_ref[...] * one_scale), xn, r)

        @pl.when(pl.program_id(0) == nblk - 1)
        def _():
            _hosted_copies(sums_item, (vec_ref,), (recv_sums,), *sums_sems, act="start")
            _hosted_copies(items, src_refs, recv_refs, *sems, act="wait")
            _hosted_copies(sums_item, (vec_ref,), (recv_sums,), *sums_sems, act="wait")

    row = pl.BlockSpec((tb, D), lambda i: (i, 0))
    recv = (jax.ShapeDtypeStruct(recv_w_in.shape, recv_w_in.dtype),
            jax.ShapeDtypeStruct((N_DEV,) + small32.shape, small32.dtype),
            jax.ShapeDtypeStruct((N_DEV,) + small16.shape, small16.dtype),
            jax.ShapeDtypeStruct((N_DEV, 3, D), F32))
    return _pcall(body, name="in_proj_bwd_x", grid=(nblk,),
                  out_shape=(jax.ShapeDtypeStruct((rows, D), F32), *recv),
                  in_specs=[row, row] + _dproj_specs(tb) + [_full((3, D)), _full((1, D)),
                                                            _full((D, N_IN), single=True)] + [ANY] * 4,
                  out_specs=(row, ANY, ANY, ANY, ANY),
                  input_output_aliases={12: 1},
                  scratch_shapes=[pltpu.VMEM((3, D), F32)] + _sem_scratch(items) + _sem_scratch(sums_item),
                  compiler_params=_params(("arbitrary",)))(x, dy, dpp, dus, dzs, dpg, mod3, norm_pre, w_in,
                                                           dw_in_ssm, small32, small16, recv_w_in)


def _in_proj_bwd_w(name, x, dparts, mod3, norm_pre):
    rows = x.shape[0]
    tb = _tb(rows, 512)
    nblk = rows // tb
    widths = [p.shape[1] for p in dparts]
    n_p = len(dparts)

    def body(x_ref, *rest):
        part_refs, (mod_ref, np_ref, dw_ref, acc) = rest[:n_p], rest[n_p:]
        i = pl.program_id(0)

        @pl.when(i == 0)
        def _():
            acc[...] = jnp.zeros_like(acc)

        _, _, h = _prenorm(x_ref[...], mod_ref[...], np_ref[...])
        ht = h.astype(BF16)
        lo = 0
        for ref, w in zip(part_refs, widths):
            acc[:, lo:lo + w] += _dot_tn(ht, ref[...])
            lo += w

        @pl.when(i == nblk - 1)
        def _():
            dw_ref[...] = acc[...].astype(BF16)

    row = pl.BlockSpec((tb, D), lambda i: (i, 0))
    return _pcall(body, name=name, grid=(nblk,),
                  out_shape=jax.ShapeDtypeStruct((D, sum(widths)), BF16),
                  in_specs=[row] + [pl.BlockSpec((tb, w), lambda i: (i, 0)) for w in widths] +
                           [_full((3, D)), _full((1, D))],
                  out_specs=_full((D, sum(widths))),
                  scratch_shapes=[pltpu.VMEM((D, sum(widths)), F32)],
                  compiler_params=_params(("arbitrary",)))(x, *dparts, mod3, norm_pre)


def _adamw_math(w, g, m, v):
    m = ADAM_B1 * m + (1.0 - ADAM_B1) * g
    v = ADAM_B2 * v + (1.0 - ADAM_B2) * (g * g)
    m_hat = m / (1.0 - ADAM_B1 ** ADAM_STEP)
    v_hat = v / (1.0 - ADAM_B2 ** ADAM_STEP)
    delta = -ADAM_LR * (m_hat / (jnp.sqrt(v_hat) + ADAM_EPS) + ADAM_WD * w)
    return delta, m, v


def _sum_sources(ref):
    g = ref[0].astype(F32)
    for s in range(1, N_DEV):
        g = g + ref[s].astype(F32)
    return g


def _adamw_reduce(name, parts, w, m, v):
    r, c = w.shape
    tr = r if r * c <= 256 * 1024 else max(8, (256 * 1024 // c) // 8 * 8)
    while r % tr:
        tr -= 8

    def body(p_ref, w_ref, m_ref, v_ref, g_ref, d_ref, nm_ref, nv_ref):
        g = _sum_sources(p_ref)
        g_ref[...] = g
        d_ref[...], nm_ref[...], nv_ref[...] = _adamw_math(w_ref[...], g, m_ref[...], v_ref[...])

    blk = pl.BlockSpec((tr, c), lambda i: (i, 0))
    return _pcall(body, name=name, grid=(r // tr,),
                  out_shape=tuple([jax.ShapeDtypeStruct((r, c), F32)] * 4),
                  in_specs=[pl.BlockSpec((N_DEV, tr, c), lambda i: (0, i, 0)), blk, blk, blk],
                  out_specs=(blk, blk, blk, blk),
                  compiler_params=_params(("arbitrary",)))(parts, w, m, v)


def _adamw_small(gs, ws, ms, vs):
    n = len(gs)

    def body(*refs):
        ins, outs = refs[:4 * n], refs[4 * n:]
        for t in range(n):
            g_ref, w_ref, m_ref, v_ref = ins[4 * t:4 * t + 4]
            outs[3 * t][...], outs[3 * t + 1][...], outs[3 * t + 2][...] = _adamw_math(
                w_ref[...], g_ref[...], m_ref[...], v_ref[...])

    vm = pl.BlockSpec(memory_space=pltpu.VMEM)
    flat = [a for t in range(n) for a in (gs[t], ws[t], ms[t], vs[t])]
    return _pcall(body, name="adamw_small",
                  out_shape=tuple(jax.ShapeDtypeStruct(w.shape, F32) for w in ws for _ in range(3)),
                  in_specs=[vm] * (4 * n), out_specs=tuple([vm] * (3 * n)), compiler_params=_params())(*flat)


def _sum_small(parts):
    n = len(parts)

    def body(*refs):
        for t in range(n):
            refs[n + t][...] = _sum_sources(refs[t])

    vm = pl.BlockSpec(memory_space=pltpu.VMEM)
    return _pcall(body, name="sum_small",
                  out_shape=tuple(jax.ShapeDtypeStruct(p.shape[1:], F32) for p in parts),
                  in_specs=[vm] * n, out_specs=tuple([vm] * n), compiler_params=_params())(*parts)


def _ada_update(c_all, dmod_cols, w, m, v):
    def body(c_ref, dm_ref, w_ref, m_ref, v_ref, g_ref, d_ref, nm_ref, nv_ref):
        ca = c_ref[...]
        g = lax.dot_general(ca * jax.nn.sigmoid(ca), dm_ref[...], (((0,), (0,)), ((), ())),
                            preferred_element_type=F32, precision=lax.Precision.HIGHEST)
        g_ref[...] = g
        d_ref[...], nm_ref[...], nv_ref[...] = _adamw_math(w_ref[...], g, m_ref[...], v_ref[...])

    vm = pl.BlockSpec(memory_space=pltpu.VMEM)
    return _pcall(body, name="ada_update", out_shape=tuple([jax.ShapeDtypeStruct(w.shape, F32)] * 4),
                  in_specs=[vm] * 5, out_specs=(vm, vm, vm, vm), compiler_params=_params())(c_all, dmod_cols, w, m, v)


def kernel(x, c, w_ada, b_ada, norm_pre, norm_post, w_in, pool_w, pool_scale, ssm_a_re, ssm_a_im, ssm_log_dt, ssm_b_re, ssm_b_im, ssm_c_re, ssm_c_im, ssm_d, glu_w, glu_b, w_branch_pool, w_branch_ssm, w_out, loss_target, m_w_ada, m_b_ada, m_norm_pre, m_norm_post, m_w_in, m_pool_w, m_pool_scale, m_ssm_a_re, m_ssm_a_im, m_ssm_log_dt, m_ssm_b_re, m_ssm_b_im, m_ssm_c_re, m_ssm_c_im, m_ssm_d, m_glu_w, m_glu_b, m_w_branch_pool, m_w_branch_ssm, m_w_out, v_w_ada, v_b_ada, v_norm_pre, v_norm_post, v_w_in, v_pool_w, v_pool_scale, v_ssm_a_re, v_ssm_a_im, v_ssm_log_dt, v_ssm_b_re, v_ssm_b_im, v_ssm_c_re, v_ssm_c_im, v_ssm_d, v_glu_w, v_glu_b, v_w_branch_pool, v_w_branch_ssm, v_w_out):
    given = dict(locals())
    me = _flat(_me())
    rows = x.shape[1]
    x2 = x[0]
    target = loss_target[0]
    ada_cols = w_ada.shape[2]

    b_ada_s = lax.dynamic_slice(b_ada, (0, me * ada_cols), (1, ada_cols))
    c_all, mod_rows = _ada_exchange(c, w_ada[0], b_ada_s)
    mod3 = mod_rows.reshape(3, D)

    shards = _cast_shards([w_in[0], pool_w[0], glu_w[0], w_branch_pool[0], w_branch_ssm[0], w_out[0]])

    tb_ssm = _tb(rows, 256)
    k_steps = tb_ssm // SUBLANES
    a_re, a_im = ssm_a_re[0], ssm_a_im[0]
    log_dt = ssm_log_dt[0].reshape(GROUPS, 1)
    b_re_t, b_im_t = ssm_b_re[0].transpose(0, 2, 1), ssm_b_im[0].transpose(0, 2, 1)
    wb, wct, pow_re, pow_im = _s5_prep(a_re, a_im, log_dt, b_re_t, b_im_t, ssm_c_re[0], ssm_c_im[0], k_steps)
    ptab = _state_layout(pow_re, pow_im)
    dvec = ssm_d[0].reshape(1, D)
    pm = _perm_matrix(tb_ssm)
    pmt = pm.T

    proj, w_in_g, pool_w_g, glu_g = _in_proj(x2, mod3, norm_pre, shards[0], shards[1:3])
    y_pool = _pool_fwd(proj, pool_w_g, pool_scale)
    y_ssm, ys_pre, carries, states, wbp_g, wbs_g, wout_g = _ssm_fwd(
        proj, pm, pmt, wb, wct, ptab, dvec, glu_g, glu_b, shards[3:])
    loss_part, dy, dyp, dys, dpg, dwbp, dwbs, dwout, head_vec = _head(
        x2, target, proj, y_pool, y_ssm, mod3, norm_post, wbp_g, wbs_g, wout_g)

    dpp, dpool_w, dpool_scale = _pool_bwd(dyp, proj, pool_w_g, pool_scale)
    dw_in_rest = _in_proj_bwd_w("in_proj_bwd_w_rest", x2, [dpp, dpg], mod3, norm_pre)
    dy_pre, dzs, dglu_w, dglu_b = _glu_bwd(dys, proj, ys_pre, pm, pmt, glu_g, glu_b)
    dus, dbb, dcc, dabar, dd, p_glu, p_wbp, p_wbs, p_wout, p_pool_w, p_w_in = _ssm_bwd(
        dy_pre, proj, states, carries, pm, pmt, wb, wct, ptab, dvec, [dglu_w, dwbp, dwbs, dwout], dpool_w, dw_in_rest)
    dw_in_ssm = _in_proj_bwd_w("in_proj_bwd_w_ssm", x2, [dus, dzs], mod3, norm_pre)

    small32 = jnp.concatenate([head_vec, dpool_scale, dglu_b, dd, jnp.broadcast_to(loss_part, (1, D)),
                               jnp.zeros((2, D), F32), dabar.reshape(8, D)], axis=0)
    small16 = jnp.concatenate([dbb.reshape(2 * GROUPS, D), dcc.reshape(2 * GROUPS, D)], axis=0).astype(BF16)
    grad_x, p_w_in, p_small32, p_small16, p_pre = _in_proj_bwd_x(
        x2, dy, dpp, dus, dzs, dpg, mod3, norm_pre, w_in_g, dw_in_ssm, small32, small16, p_w_in)

    tot32, tot16, tot_pre = _sum_small([p_small32, p_small16, p_pre])
    d_abar_re, d_abar_im = _state_unlayout(tot32[8:16].reshape(N_STATE))
    d_bb_re, d_bb_im = tot16[0:64].reshape(GROUPS, G_H, G_P), tot16[64:128].reshape(GROUPS, G_H, G_P)
    g_a_re, g_a_im, g_log_dt, g_b_re_t, g_b_im_t = _s5_prep_bwd(
        a_re, a_im, log_dt, b_re_t, b_im_t, d_abar_re, d_abar_im, d_bb_re, d_bb_im)

    grads, deltas, new_m, new_v = {}, {}, {}, {}

    small = []

    def small_update(name, g2):
        small.append((name, g2))

    def shard_update(name, parts):
        shape = given[name].shape
        r2 = parts.shape[1:] if parts.ndim == 3 else (parts.shape[1] * parts.shape[2], parts.shape[3])
        w2, m2, v2 = (given[p + name].reshape(r2) for p in ("", "m_", "v_"))
        out = _adamw_reduce("adamw_" + name, parts.reshape((N_DEV,) + tuple(r2)), w2, m2, v2)
        grads[name], deltas[name], new_m[name], new_v[name] = (a.reshape(shape) for a in out)

    dmod_all = jnp.concatenate([p_pre[:, 0:2, :], p_small32[:, 0:1, :]], axis=1).reshape(N_DEV, 3 * D)
    dmod_cols = lax.dynamic_slice(dmod_all, (0, me * ada_cols), (N_DEV, ada_cols))
    out = _ada_update(c_all, dmod_cols, w_ada[0], m_w_ada[0], v_w_ada[0])
    grads['w_ada'], deltas['w_ada'], new_m['w_ada'], new_v['w_ada'] = (a.reshape(w_ada.shape) for a in out)

    small_update('b_ada', jnp.concatenate([tot_pre[0:2], tot32[0:1]], axis=0).reshape(1, 3 * D))
    small_update('norm_pre', tot_pre[2:3])
    small_update('norm_post', tot32[1:2])
    small_update('pool_scale', tot32[2:3])
    small_update('glu_b', tot32[3:4])
    small_update('ssm_d', tot32[4:5])
    small_update('ssm_a_re', g_a_re)
    small_update('ssm_a_im', g_a_im)
    small_update('ssm_log_dt', g_log_dt.reshape(1, GROUPS))
    small_update('ssm_b_re', g_b_re_t.transpose(0, 2, 1).reshape(GROUPS, G_P * G_H))
    small_update('ssm_b_im', g_b_im_t.transpose(0, 2, 1).reshape(GROUPS, G_P * G_H))
    small_update('ssm_c_re', tot16[128:192])
    small_update('ssm_c_im', -tot16[192:256])
    flat = _adamw_small([g2 for _, g2 in small],
                        *[[given[p + name].reshape(g2.shape) for name, g2 in small] for p in ("", "m_", "v_")])
    for t, (name, g2) in enumerate(small):
        shape = given[name].shape
        grads[name], deltas[name], new_m[name], new_v[name] = (
            a.reshape(shape) for a in (g2, *flat[3 * t:3 * t + 3]))
    shard_update('w_in', p_w_in)
    shard_update('pool_w', p_pool_w)
    shard_update('glu_w', p_glu)
    shard_update('w_branch_pool', p_wbp)
    shard_update('w_branch_ssm', p_wbs)
    shard_update('w_out', p_wout)

    return (tot32[5, 0], grad_x[None], *[grads[n] for n in WEIGHTS], *[deltas[n] for n in WEIGHTS],
            *[new_m[n] for n in WEIGHTS], *[new_v[n] for n in WEIGHTS])
```

```python
import functools
import math
from typing import Callable, NamedTuple, Optional

import jax
import jax.numpy as jnp
from jax import lax
from jax.experimental import pallas as pl
from jax.experimental.pallas import tpu as pltpu

F32 = jnp.float32
BF16 = jnp.bfloat16
MESH = pl.DeviceIdType.MESH

D = 1024
N_DEV = 8
N_IN = 6 * D
GROUPS = 64
G_H = 16
G_P = 64
N_Q = 4
Q_W = 2 * 16 * G_P
N_STATE = N_Q * Q_W
POOL_WINDOWS = (2, 4, 8, 16)
HALO = 16
RMS_EPS = 1e-6
SUBLANES = 8
LANE_CHUNK = 512
SCAN_UNROLL = 2
VMEM_LIMIT = 56 * 1024 * 1024

ADAM_LR = 0.001
ADAM_B1 = 0.9
ADAM_B2 = 0.999
ADAM_EPS = 1e-08
ADAM_WD = 0.01
ADAM_STEP = 10

WEIGHTS = ['w_ada', 'b_ada', 'norm_pre', 'norm_post', 'w_in', 'pool_w', 'pool_scale', 'ssm_a_re',
           'ssm_a_im', 'ssm_log_dt', 'ssm_b_re', 'ssm_b_im', 'ssm_c_re', 'ssm_c_im', 'ssm_d', 'glu_w',
           'glu_b', 'w_branch_pool', 'w_branch_ssm', 'w_out']


def _pcall(body, **kw):
    return pl.pallas_call(body, **kw)


def _params(sem=None, vmem=VMEM_LIMIT):
    return pltpu.CompilerParams(dimension_semantics=sem, vmem_limit_bytes=vmem)


def _tb(rows, pref):
    return pref if rows % pref == 0 and rows // pref >= 2 else rows // 2


def _full(shape, single=False):
    nd = len(shape)
    if single:
        return pl.BlockSpec(shape, lambda i: (0,) * nd, pipeline_mode=pl.Buffered(1))
    return pl.BlockSpec(shape, lambda i: (0,) * nd)


ANY = pl.BlockSpec(memory_space=pl.ANY)


def _me():
    return lax.axis_index("x"), lax.axis_index("y"), lax.axis_index("c")


def _flat(p):
    return 4 * p[0] + 2 * p[1] + p[2]


def _peer(k):
    x, y, c = _me()
    return (1 - x if k & 4 else x, 1 - y if k & 2 else y, 1 - c if k & 1 else c)


def _silu_parts(z):
    s = jax.nn.sigmoid(z)
    return z * s, s * (1.0 + z * (1.0 - s))


_GELU_C = math.sqrt(2.0 / math.pi)


def _gelu_parts(x):
    x2 = x * x
    t = jnp.tanh(_GELU_C * (x + 0.044715 * x * x2))
    g = 0.5 * x * (1.0 + t)
    dg = 0.5 * (1.0 + t) + 0.5 * x * (1.0 - t * t) * (_GELU_C * (1.0 + 3.0 * 0.044715 * x2))
    return g, dg


def _dot(a, b):
    return jnp.dot(a, b, preferred_element_type=F32)


def _dot_nt(a, b):
    return lax.dot_general(a, b, (((1,), (1,)), ((), ())), preferred_element_type=F32)


def _dot_tn(a, b):
    return lax.dot_general(a, b, (((0,), (0,)), ((), ())), preferred_element_type=F32)


def _rms_parts(x):
    r = lax.rsqrt(jnp.mean(x * x, axis=-1, keepdims=True) + RMS_EPS)
    return x * r, r


def _rms_bwd(dxn, xn, r):
    return r * (dxn - xn * jnp.mean(dxn * xn, axis=-1, keepdims=True))


def _ada_exchange(c, w_ada_s, b_ada_s):
    cols = w_ada_s.shape[1]

    def body(c_ref, w_ref, b_ref, call_ref, mod_ref, part_ref, ssem, rsem, lsem):
        me3 = _me()
        me = _flat(me3)
        mine = pltpu.make_async_copy(c_ref, call_ref.at[pl.ds(me, 1), :], lsem.at[0])
        mine.start()
        sends = []
        for k in range(1, N_DEV):
            cp = pltpu.make_async_remote_copy(src_ref=c_ref, dst_ref=call_ref.at[pl.ds(me, 1), :],
                                              send_sem=ssem.at[k - 1], recv_sem=rsem.at[k - 1],
                                              device_id=_peer(k), device_id_type=MESH)
            cp.start()
            sends.append(cp)
        mine.wait()
        for k in range(1, N_DEV):
            p = _flat(_peer(k))
            pltpu.make_async_remote_copy(src_ref=c_ref, dst_ref=call_ref.at[pl.ds(p, 1), :],
                                         send_sem=ssem.at[k - 1], recv_sem=rsem.at[k - 1],
                                         device_id=_peer(k), device_id_type=MESH).wait_recv()
        for cp in sends:
            cp.wait_send()
        ca = call_ref[...]
        act = ca * jax.nn.sigmoid(ca)
        part_ref[...] = jnp.dot(act, w_ref[...], preferred_element_type=F32,
                                precision=lax.Precision.HIGHEST) + b_ref[...]
        own = pltpu.make_async_copy(part_ref.at[pl.ds(me, 1), :], mod_ref.at[pl.ds(me, 1), :], lsem.at[1])
        own.start()
        sends = []
        for k in range(1, N_DEV):
            p = _flat(_peer(k))
            s = N_DEV - 1 + k - 1
            cp = pltpu.make_async_remote_copy(src_ref=part_ref.at[pl.ds(p, 1), :],
                                              dst_ref=mod_ref.at[pl.ds(me, 1), :],
                                              send_sem=ssem.at[s], recv_sem=rsem.at[s],
                                              device_id=_peer(k), device_id_type=MESH)
            cp.start()
            sends.append(cp)
        own.wait()
        for k in range(1, N_DEV):
            p = _flat(_peer(k))
            s = N_DEV - 1 + k - 1
            pltpu.make_async_remote_copy(src_ref=part_ref.at[pl.ds(p, 1), :],
                                         dst_ref=mod_ref.at[pl.ds(p, 1), :],
                                         send_sem=ssem.at[s], recv_sem=rsem.at[s],
                                         device_id=_peer(k), device_id_type=MESH).wait_recv()
        for cp in sends:
            cp.wait_send()

    vm = pl.BlockSpec(memory_space=pltpu.VMEM)
    return _pcall(
        body, name="ada_exchange",
        out_shape=(jax.ShapeDtypeStruct((N_DEV, D), F32), jax.ShapeDtypeStruct((N_DEV, cols), F32)),
        in_specs=[vm, vm, vm], out_specs=(vm, vm),
        scratch_shapes=[pltpu.VMEM((N_DEV, cols), F32),
                        pltpu.SemaphoreType.DMA((2 * (N_DEV - 1),)),
                        pltpu.SemaphoreType.DMA((2 * (N_DEV - 1),)),
                        pltpu.SemaphoreType.DMA((2,))],
    )(c, w_ada_s, b_ada_s)


class _Item(NamedTuple):
    src: int
    out: int
    src_view: Callable
    dst_view: Callable
    pred: Optional[Callable] = None


def _when(pred, dest, fn):
    if pred is None:
        fn()
    else:
        pl.when(pred(dest))(fn)


def _n_sems(items):
    return len(items) * (N_DEV - 1)


def _hosted_copies(items, srcs, outs, ssem, rsem, lsem, act):
    me = _flat(_me())
    for t, it in enumerate(items):
        local = lambda t=t, it=it: pltpu.make_async_copy(
            it.src_view(srcs[it.src], me), it.dst_view(outs[it.out], me), lsem.at[t])
        if act == "start":
            _when(it.pred, me, lambda local=local: local().start())
        else:
            _when(it.pred, me, lambda local=local: local().wait())
    for k in range(1, N_DEV):
        p3 = _peer(k)
        p = _flat(p3)
        for t, it in enumerate(items):
            s = t * (N_DEV - 1) + k - 1
            send = lambda it=it, s=s, p=p, p3=p3: pltpu.make_async_remote_copy(
                src_ref=it.src_view(srcs[it.src], p), dst_ref=it.dst_view(outs[it.out], me),
                send_sem=ssem.at[s], recv_sem=rsem.at[s], device_id=p3, device_id_type=MESH)
            recv = lambda it=it, s=s, p=p, p3=p3: pltpu.make_async_remote_copy(
                src_ref=it.src_view(srcs[it.src], p), dst_ref=it.dst_view(outs[it.out], p),
                send_sem=ssem.at[s], recv_sem=rsem.at[s], device_id=p3, device_id_type=MESH)
            if act == "start":
                _when(it.pred, p, lambda send=send: send().start())
            else:
                _when(it.pred, me, lambda recv=recv: recv().wait_recv())
                _when(it.pred, p, lambda send=send: send().wait_send())


def _sem_scratch(items):
    return [pltpu.SemaphoreType.DMA((_n_sems(items),)), pltpu.SemaphoreType.DMA((_n_sems(items),)),
            pltpu.SemaphoreType.DMA((len(items),))]


def _exchange(name, srcs, out_structs, items):
    n_src, n_out = len(srcs), len(out_structs)

    def body(*refs):
        src_refs, out_refs = refs[:n_src], refs[n_src:n_src + n_out]
        sems = refs[n_src + n_out:]
        _hosted_copies(items, src_refs, out_refs, *sems, act="start")
        _hosted_copies(items, src_refs, out_refs, *sems, act="wait")

    return _pcall(body, name=name, out_shape=tuple(out_structs),
                  in_specs=[ANY] * n_src, out_specs=tuple([ANY] * n_out),
                  scratch_shapes=_sem_scratch(items))(*srcs)


def _whole(ref, dest):
    return ref


def _slot(ref, sender):
    return ref.at[sender]


def _rows_of(rows):
    return lambda ref, dev: ref.at[pl.ds(dev * rows, rows), :]


def _cols_of(cols):
    return lambda ref, dev: ref.at[:, pl.ds(dev * cols, cols)]


def _pool_rows_of(rows):
    return lambda ref, dev: ref.at[:, pl.ds(dev * rows, rows), :]


def _gather_item(src, out, dst_view):
    return _Item(src, out, _whole, dst_view)


def _scatter_item(src, out, src_view):
    return _Item(src, out, src_view, _slot)


W_IN_BLOCK = 256
W_IN_SHARD = N_IN // N_DEV
SSM_BLOCKS = (2 * D // W_IN_BLOCK, 4 * D // W_IN_BLOCK)


def _w_in_block_item(src, out, j, ssm_part):
    def block(dest):
        return (W_IN_SHARD // W_IN_BLOCK) * dest + j

    def in_ssm(dest):
        b = block(dest)
        return (b >= SSM_BLOCKS[0]) & (b < SSM_BLOCKS[1])

    def src_view(ref, dest):
        b = block(dest)
        local = b - SSM_BLOCKS[0] if ssm_part else jnp.where(b < SSM_BLOCKS[0], b, b - (SSM_BLOCKS[1] - SSM_BLOCKS[0]))
        local = jnp.clip(local, 0, ref.shape[1] // W_IN_BLOCK - 1)
        return ref.at[:, pl.ds(local * W_IN_BLOCK, W_IN_BLOCK)]

    def dst_view(ref, sender):
        return ref.at[sender, :, pl.ds(j * W_IN_BLOCK, W_IN_BLOCK)]

    pred = in_ssm if ssm_part else (lambda dest: jnp.logical_not(in_ssm(dest)))
    return _Item(src, out, src_view, dst_view, pred)


def _cast_shards(arrs):
    def body(*refs):
        n = len(refs) // 2
        for i in range(n):
            refs[n + i][...] = refs[i][...].astype(BF16)

    vm = pl.BlockSpec(memory_space=pltpu.VMEM)
    return _pcall(body, name="cast_shards",
                  out_shape=tuple(jax.ShapeDtypeStruct(a.shape, BF16) for a in arrs),
                  in_specs=[vm] * len(arrs), out_specs=tuple([vm] * len(arrs)),
                  compiler_params=_params())(*arrs)


def _s5_discretise(a_re, a_im, log_dt, b_re_t, b_im_t):
    dt = jnp.exp(log_dt)
    lam_re = jnp.minimum(a_re, -1e-4)
    lam_im = a_im
    mag = jnp.exp(lam_re * dt)
    abar_re = mag * jnp.cos(lam_im * dt)
    abar_im = mag * jnp.sin(lam_im * dt)
    den = lam_re * lam_re + lam_im * lam_im
    num_re = abar_re - 1.0
    f_re = (num_re * lam_re + abar_im * lam_im) / den
    f_im = (abar_im * lam_re - num_re * lam_im) / den
    f_re, f_im = f_re[:, None, :], f_im[:, None, :]
    bb_re = f_re * b_re_t - f_im * b_im_t
    bb_im = f_re * b_im_t + f_im * b_re_t
    return abar_re, abar_im, bb_re, bb_im


def _group_masks():
    spread = lax.broadcasted_iota(jnp.int32, (G_P, 16 * G_P), 1) % G_P == lax.broadcasted_iota(
        jnp.int32, (G_P, 16 * G_P), 0)
    own = lax.broadcasted_iota(jnp.int32, (16 * G_H, 16 * G_P), 0) // G_H == lax.broadcasted_iota(
        jnp.int32, (16 * G_H, 16 * G_P), 1) // G_P
    return spread, own


def _s5_prep(a_re, a_im, log_dt, b_re_t, b_im_t, c_re, c_im, n_pow):
    def body(ar_ref, ai_ref, ld_ref, br_ref, bi_ref, cr_ref, ci_ref, wb_ref, wct_ref, pr_ref, pi_ref):
        abar_re, abar_im, bb_re, bb_im = _s5_discretise(ar_ref[...], ai_ref[...], ld_ref[...], br_ref[...], bi_ref[...])
        spread, own = _group_masks()
        spread = spread.astype(BF16)
        for ref, parts in ((wb_ref, (bb_re, bb_im)), (wct_ref, (cr_ref[...], -ci_ref[...]))):
            for half, t in enumerate(parts):
                for q in range(N_Q):
                    blocks = t[q * 16:(q + 1) * 16].reshape(16 * G_H, G_P).astype(BF16)
                    dense = jnp.where(own, _dot(blocks, spread), 0.0)
                    ref[q, :, half * (Q_W // 2):(half + 1) * (Q_W // 2)] = dense.astype(BF16)
        p_re, p_im = abar_re, abar_im
        pr_ref[0] = p_re
        pi_ref[0] = p_im
        for k in range(1, n_pow):
            p_re, p_im = p_re * abar_re - p_im * abar_im, p_re * abar_im + p_im * abar_re
            pr_ref[k] = p_re
            pi_ref[k] = p_im

    vm = pl.BlockSpec(memory_space=pltpu.VMEM)
    return _pcall(body, name="s5_prep",
                  out_shape=(jax.ShapeDtypeStruct((N_Q, 16 * G_H, Q_W), BF16),
                             jax.ShapeDtypeStruct((N_Q, 16 * G_H, Q_W), BF16),
                             jax.ShapeDtypeStruct((n_pow, GROUPS, G_P), F32),
                             jax.ShapeDtypeStruct((n_pow, GROUPS, G_P), F32)),
                  in_specs=[vm] * 7, out_specs=(vm, vm, vm, vm), compiler_params=_params(),
                  )(a_re, a_im, log_dt, b_re_t, b_im_t, c_re, c_im)


def _s5_prep_bwd(a_re, a_im, log_dt, b_re_t, b_im_t, d_abar_re, d_abar_im, d_bb_re, d_bb_im):
    def body(ar_ref, ai_ref, ld_ref, br_ref, bi_ref, dar_ref, dai_ref, dbr_ref, dbi_ref,
             gar_ref, gai_ref, gld_ref, gbr_ref, gbi_ref):
        _, vjp = jax.vjp(_s5_discretise, ar_ref[...], ai_ref[...], ld_ref[...], br_ref[...], bi_ref[...])
        g = vjp((dar_ref[...], dai_ref[...], dbr_ref[...], dbi_ref[...]))
        gar_ref[...] = g[0]
        gai_ref[...] = g[1]
        gld_ref[...] = g[2]
        gbr_ref[...] = g[3]
        gbi_ref[...] = g[4]

    vm = pl.BlockSpec(memory_space=pltpu.VMEM)
    ins = (a_re, a_im, log_dt, b_re_t, b_im_t)
    return _pcall(body, name="s5_prep_bwd",
                  out_shape=tuple(jax.ShapeDtypeStruct(a.shape, F32) for a in ins),
                  in_specs=[vm] * 9, out_specs=tuple([vm] * 5), compiler_params=_params(),
                  )(*ins, d_abar_re, d_abar_im, d_bb_re, d_bb_im)


def _state_layout(re, im):
    lead = re.shape[:-2]
    r = re.reshape(lead + (N_Q, 1, 16 * G_P))
    i = im.reshape(lead + (N_Q, 1, 16 * G_P))
    return jnp.concatenate([r, i], axis=-2).reshape(lead + (N_STATE,))


def _state_unlayout(v):
    v4 = v.reshape(N_Q, 2, 16, G_P)
    return v4[:, 0].reshape(GROUPS, G_P), v4[:, 1].reshape(GROUPS, G_P)


def _perm_matrix(tb):
    k_steps = tb // SUBLANES
    r = jnp.arange(tb)
    src = (r % SUBLANES) * k_steps + r // SUBLANES
    return (src[:, None] == jnp.arange(tb)[None, :]).astype(BF16)


def _lane_chunks(q):
    for lc in range(Q_W // 2 // LANE_CHUNK):
        re = q * Q_W + lc * LANE_CHUNK
        yield re, re + Q_W // 2


def _steps(lo, hi, body, init):
    if hi - lo <= SCAN_UNROLL:
        for k in range(lo, hi):
            init = body(k, init)
        return init
    trips = (hi - lo) // SCAN_UNROLL

    def trip(j, carry):
        for u in range(SCAN_UNROLL):
            carry = body(lo + j * SCAN_UNROLL + u, carry)
        return carry

    carry = lax.fori_loop(0, trips, trip, init)
    for k in range(lo + trips * SCAN_UNROLL, hi):
        carry = body(k, carry)
    return carry


def _tile(k):
    if isinstance(k, int):
        return pl.ds(k * SUBLANES, SUBLANES)
    return pl.ds(pl.multiple_of(k * SUBLANES, SUBLANES), SUBLANES)


def _scan_forward(q, s_ref, p_ref, carry_ref, enter_ref, fin_ref, k_steps):
    for re, im in _lane_chunks(q):
        lr, li = pl.ds(re, LANE_CHUNK), pl.ds(im, LANE_CHUNK)
        a_re = jnp.broadcast_to(p_ref[0:1, lr], (SUBLANES, LANE_CHUNK))
        a_im = jnp.broadcast_to(p_ref[0:1, li], (SUBLANES, LANE_CHUNK))

        def local(k, st):
            sr, si = st
            rows = _tile(k)
            nr = a_re * sr - a_im * si + s_ref[rows, lr]
            ni = a_re * si + a_im * sr + s_ref[rows, li]
            s_ref[rows, lr] = nr
            s_ref[rows, li] = ni
            return nr, ni

        zero = jnp.zeros((SUBLANES, LANE_CHUNK), F32)
        fr, fi = _steps(0, k_steps, local, (zero, zero))
        fin_ref[:, lr] = fr
        fin_ref[:, li] = fi
        ak_re, ak_im = p_ref[k_steps - 1:k_steps, lr], p_ref[k_steps - 1:k_steps, li]
        c_re, c_im = carry_ref[:, lr], carry_ref[:, li]
        for seg in range(SUBLANES):
            enter_ref[seg:seg + 1, lr] = c_re
            enter_ref[seg:seg + 1, li] = c_im
            f_re, f_im = fin_ref[seg:seg + 1, lr], fin_ref[seg:seg + 1, li]
            c_re, c_im = f_re + ak_re * c_re - ak_im * c_im, f_im + ak_re * c_im + ak_im * c_re
        carry_ref[:, lr] = c_re
        carry_ref[:, li] = c_im
        e_re, e_im = enter_ref[:, lr], enter_ref[:, li]

        def fix(k, _):
            rows = _tile(k)
            p_re = p_ref[pl.ds(k, 1), lr]
            p_im = p_ref[pl.ds(k, 1), li]
            s_ref[rows, lr] = s_ref[rows, lr] + (p_re * e_re - p_im * e_im)
            s_ref[rows, li] = s_ref[rows, li] + (p_re * e_im + p_im * e_re)
            return 0

        _steps(0, k_steps, fix, 0)


def _scan_backward(q, g_ref, s_ref, p_ref, carry_ref, s_in_ref, fin_ref, da_ref, k_steps):
    seg_id = lax.broadcasted_iota(jnp.int32, (SUBLANES, LANE_CHUNK), 0)
    for re, im in _lane_chunks(q):
        lr, li = pl.ds(re, LANE_CHUNK), pl.ds(im, LANE_CHUNK)
        a_re = jnp.broadcast_to(p_ref[0:1, lr], (SUBLANES, LANE_CHUNK))
        a_im = jnp.broadcast_to(p_ref[0:1, li], (SUBLANES, LANE_CHUNK))

        def local(j, st):
            sr, si = st
            rows = _tile(k_steps - 1 - j)
            nr = a_re * sr + a_im * si + g_ref[rows, lr]
            ni = a_re * si - a_im * sr + g_ref[rows, li]
            g_ref[rows, lr] = nr
            g_ref[rows, li] = ni
            return nr, ni

        zero = jnp.zeros((SUBLANES, LANE_CHUNK), F32)
        fr, fi = _steps(0, k_steps, local, (zero, zero))
        fin_ref[:, lr] = fr
        fin_ref[:, li] = fi
        ak_re, ak_im = p_ref[k_steps - 1:k_steps, lr], p_ref[k_steps - 1:k_steps, li]
        c_re, c_im = carry_ref[:, lr], carry_ref[:, li]
        lam_in = [None] * SUBLANES
        for seg in reversed(range(SUBLANES)):
            lam_in[seg] = (c_re, c_im)
            f_re, f_im = fin_ref[seg:seg + 1, lr], fin_ref[seg:seg + 1, li]
            c_re, c_im = f_re + ak_re * c_re + ak_im * c_im, f_im + ak_re * c_im - ak_im * c_re
        carry_ref[:, lr] = c_re
        carry_ref[:, li] = c_im
        for seg in range(SUBLANES):
            fin_ref[seg:seg + 1, lr] = lam_in[seg][0]
            fin_ref[seg:seg + 1, li] = lam_in[seg][1]
        e_re, e_im = fin_ref[:, lr], fin_ref[:, li]

        def fix_with(k, acc, sp_re, sp_im):
            acc_re, acc_im = acc
            rows = _tile(k)
            p_re = p_ref[pl.ds(k_steps - 1 - k, 1), lr]
            p_im = p_ref[pl.ds(k_steps - 1 - k, 1), li]
            l_re = g_ref[rows, lr] + (p_re * e_re + p_im * e_im)
            l_im = g_ref[rows, li] + (p_re * e_im - p_im * e_re)
            g_ref[rows, lr] = l_re
            g_ref[rows, li] = l_im
            return acc_re + (l_re * sp_re + l_im * sp_im), acc_im + (l_im * sp_re - l_re * sp_im)

        def fix(k, acc):
            prev = _tile(k - 1)
            return fix_with(k, acc, s_ref[prev, lr], s_ref[prev, li])

        last = _tile(k_steps - 1)
        before_re = jnp.where(seg_id == 0, s_in_ref[:, lr], pltpu.roll(s_ref[last, lr], 1, axis=0))
        before_im = jnp.where(seg_id == 0, s_in_ref[:, li], pltpu.roll(s_ref[last, li], 1, axis=0))
        acc = fix_with(0, (zero, zero), before_re, before_im)
        acc_re, acc_im = _steps(1, k_steps, fix, acc)
        da_ref[:, lr] = da_ref[:, lr] + jnp.sum(acc_re, axis=0, keepdims=True)
        da_ref[:, li] = da_ref[:, li] + jnp.sum(acc_im, axis=0, keepdims=True)


def _prenorm(x, mod3, norm_pre):
    xn, r = _rms_parts(x)
    return xn, r, xn * norm_pre * (1.0 + mod3[1:2, :]) + mod3[0:1, :]


CHIP_FLIPS = (4, 2, 6)


def _shard_order(me):
    flips = [0, 1] + [f + c for f in CHIP_FLIPS for c in (0, 1)]
    return jnp.stack([me ^ f for f in flips]).astype(jnp.int32)


def _in_proj(x, mod3, norm_pre, w_in_s, shards):
    rows = x.shape[0]
    tb = _tb(rows, 2048)
    nblk = rows // tb
    n_sh = len(shards)
    last_step = N_DEV - 1
    items = [_gather_item(0, 0, _pool_rows_of(shards[0].shape[1]))] + \
            [_gather_item(t, t, _rows_of(shards[t].shape[0])) for t in range(1, n_sh)]

    def body(order_ref, x_ref, mod_ref, np_ref, w_src, *rest):
        src_refs, proj_ref, w_full, out_refs = rest[:n_sh], rest[n_sh], rest[n_sh + 1], rest[n_sh + 2:2 * n_sh + 2]
        h_scr, wg, ssem, rsem, lsem, *sems = rest[2 * n_sh + 2:]
        s, i = pl.program_id(0), pl.program_id(1)
        me3 = _me()
        me = _flat(me3)
        sibling = _peer(1)

        def own_copy(slot, k):
            return pltpu.make_async_remote_copy(src_ref=w_src, dst_ref=wg.at[me], send_sem=ssem.at[slot],
                                                recv_sem=rsem.at[slot], device_id=_peer(k), device_id_type=MESH)

        def passed_copy(j):
            p = _flat(_peer(CHIP_FLIPS[j]))
            return pltpu.make_async_remote_copy(src_ref=wg.at[p], dst_ref=wg.at[p], send_sem=ssem.at[4 + j],
                                                recv_sem=rsem.at[4 + j], device_id=sibling, device_id_type=MESH)

        def arrival(slot, flip):
            p = _flat(_peer(flip))
            pltpu.make_async_remote_copy(src_ref=w_src, dst_ref=wg.at[p], send_sem=ssem.at[slot],
                                         recv_sem=rsem.at[slot], device_id=sibling, device_id_type=MESH).wait_recv()

        def keep(t):
            p = order_ref[t]
            return pltpu.make_async_copy(wg.at[p], w_full.at[:, pl.ds(p * W_IN_SHARD, W_IN_SHARD)], lsem.at[1 + t])

        first = i == 0
        for t in range(last_step):
            pl.when(first & (s == t + 1))(lambda t=t: keep(t).start())

        @pl.when(first & (s == 0))
        def _():
            mine = pltpu.make_async_copy(w_src, wg.at[me], lsem.at[0])
            mine.start()
            own_copy(0, 1).start()
            for j, f in enumerate(CHIP_FLIPS[:2]):
                own_copy(1 + j, f).start()
            mine.wait()

        @pl.when(first & (s == 1))
        def _():
            arrival(0, 1)

        for j, f in enumerate(CHIP_FLIPS):
            @pl.when(first & (s == 2 + 2 * j))
            def _(j=j, f=f):
                arrival(1 + j, f)
                passed_copy(j).start()
                if j == 0:
                    own_copy(3, CHIP_FLIPS[2]).start()

            @pl.when(first & (s == 3 + 2 * j))
            def _(j=j, f=f):
                arrival(4 + j, f + 1)

        @pl.when(first & (s == last_step - 1))
        def _():
            _hosted_copies(items, src_refs, out_refs, *sems, act="start")

        rows_i = pl.ds(pl.multiple_of(i * tb, tb), tb)

        @pl.when(s == 0)
        def _():
            _, _, h = _prenorm(x_ref[...], mod_ref[...], np_ref[...])
            h_scr[rows_i, :] = h.astype(BF16)

        proj_ref[...] = _dot(h_scr[rows_i, :], wg[order_ref[s]]).astype(BF16)

        @pl.when((s == last_step) & (i == nblk - 1))
        def _():
            own_copy(0, 1).wait_send()
            for j, f in enumerate(CHIP_FLIPS):
                own_copy(1 + j, f).wait_send()
                passed_copy(j).wait_send()
            keep(last_step).start()
            for t in range(N_DEV):
                keep(t).wait()
            _hosted_copies(items, src_refs, out_refs, *sems, act="wait")

    full = [jax.ShapeDtypeStruct((4, 256, 256), BF16)] + [jax.ShapeDtypeStruct((D, D), BF16)] * (n_sh - 1)
    grid_spec = pltpu.PrefetchScalarGridSpec(
        num_scalar_prefetch=1, grid=(N_DEV, nblk),
        in_specs=[pl.BlockSpec((tb, D), lambda s, i, order: (jnp.where(s == 0, i, nblk - 1), 0)),
                  pl.BlockSpec((3, D), lambda s, i, order: (0, 0)), pl.BlockSpec((1, D), lambda s, i, order: (0, 0)),
                  ANY] + [ANY] * n_sh,
        out_specs=(pl.BlockSpec((tb, W_IN_SHARD), lambda s, i, order: (i, order[s])), ANY, *([ANY] * n_sh)),
        scratch_shapes=[pltpu.VMEM((rows, D), BF16), pltpu.VMEM((N_DEV, D, W_IN_SHARD), BF16),
                        pltpu.SemaphoreType.DMA((N_DEV - 1,)), pltpu.SemaphoreType.DMA((N_DEV - 1,)),
                        pltpu.SemaphoreType.DMA((1 + N_DEV,))] + _sem_scratch(items))
    return _pcall(body, name="in_proj", grid_spec=grid_spec,
                  out_shape=(jax.ShapeDtypeStruct((rows, N_IN), BF16), jax.ShapeDtypeStruct((D, N_IN), BF16), *full),
                  compiler_params=_params(("arbitrary", "arbitrary")),
                  )(_shard_order(_flat(_me())), x, mod3, norm_pre, w_in_s, *shards)


def _pool_windows(ext, tb, first_row):
    pos = (first_row + lax.broadcasted_iota(jnp.int32, (tb, 1), 0) + 1).astype(F32)
    pooled, inv_counts = [], []
    for g, w in enumerate(POOL_WINDOWS):
        acc = ext[:, g * 256:(g + 1) * 256]
        tok = acc[HALO:, :]
        s = 1
        while s < w:
            acc = acc + pltpu.roll(acc, s, axis=0)
            s *= 2
        inv = 1.0 / jnp.minimum(pos, float(w))
        pooled.append(acc[HALO:, :] * inv - tok)
        inv_counts.append(inv)
    return pooled, inv_counts


def _pool_fwd(proj, pool_w, pool_scale):
    rows = proj.shape[0]
    tb = _tb(rows, 512)
    hb = tb // HALO

    def body(u_ref, halo_ref, z_ref, pw_ref, ps_ref, y_ref):
        i = pl.program_id(0)
        u = u_ref[...].astype(F32)
        halo = jnp.where(i > 0, halo_ref[...].astype(F32), 0.0)
        pooled, _ = _pool_windows(jnp.concatenate([halo, u], axis=0), tb, i * tb)
        silu_z, _ = _silu_parts(z_ref[...].astype(F32))
        for g in range(4):
            cols = slice(g * 256, (g + 1) * 256)
            mixed = _dot(pooled[g].astype(BF16), pw_ref[g])
            y_ref[:, cols] = (mixed * ps_ref[:, cols] * silu_z[:, cols]).astype(BF16)

    return _pcall(body, name="pool_fwd", grid=(rows // tb,),
                  out_shape=jax.ShapeDtypeStruct((rows, D), BF16),
                  in_specs=[pl.BlockSpec((tb, D), lambda i: (i, 0)),
                            pl.BlockSpec((HALO, D), lambda i: (jnp.maximum(i * hb - 1, 0), 0)),
                            pl.BlockSpec((tb, D), lambda i: (i, 1)),
                            _full((4, 256, 256)), _full((1, D))],
                  out_specs=pl.BlockSpec((tb, D), lambda i: (i, 0)),
                  compiler_params=_params(("arbitrary",)))(proj, proj, proj, pool_w, pool_scale)


def _ssm_fwd(proj, pm, pmt, wb, wct, ptab, dvec, glu_w, glu_b, shards):
    rows = proj.shape[0]
    tb = pm.shape[0]
    k_steps = tb // SUBLANES
    nblk = rows // tb
    n_sh = len(shards)
    items = [_gather_item(t, t, _rows_of(shards[t].shape[0])) for t in range(n_sh)]

    def body(u_ref, z_ref, pm_ref, pmt_ref, wb_ref, wct_ref, p_ref, d_ref, gw_ref, gb_ref, *rest):
        src_refs = rest[:n_sh]
        y_ref, ys_ref, carry_out_ref, s_ref = rest[n_sh:n_sh + 4]
        out_refs = rest[n_sh + 4:2 * n_sh + 4]
        carry_ref, enter_ref, fin_ref, *sems = rest[2 * n_sh + 4:]

        @pl.when(pl.program_id(0) == 0)
        def _():
            _hosted_copies(items, src_refs, out_refs, *sems, act="start")
            carry_ref[...] = jnp.zeros_like(carry_ref)

        carry_out_ref[...] = carry_ref[...]
        up = _dot(pm_ref[...], u_ref[...]).astype(BF16)

        for q in range(N_Q):
            s_ref[:, q * Q_W:(q + 1) * Q_W] = _dot(up[:, q * 256:(q + 1) * 256], wb_ref[q])
        for q in range(N_Q):
            _scan_forward(q, s_ref, p_ref, carry_ref, enter_ref, fin_ref, k_steps)
        for q in range(N_Q):
            cols = slice(q * 256, (q + 1) * 256)
            y = _dot_nt(s_ref[:, q * Q_W:(q + 1) * Q_W].astype(BF16), wct_ref[q])
            ys_ref[:, cols] = y + d_ref[:, cols] * up[:, cols].astype(F32)
        yg, _ = _gelu_parts(ys_ref[...])
        gate = jax.nn.sigmoid(_dot(yg.astype(BF16), gw_ref[...]) + gb_ref[...])
        zp = _dot(pm_ref[...], z_ref[...])
        silu_z, _ = _silu_parts(zp)
        y_ref[...] = _dot(pmt_ref[...], (yg * gate * silu_z).astype(BF16)).astype(BF16)

        @pl.when(pl.program_id(0) == nblk - 1)
        def _():
            _hosted_copies(items, src_refs, out_refs, *sems, act="wait")

    return _pcall(body, name="ssm_fwd", grid=(nblk,),
                  out_shape=(jax.ShapeDtypeStruct((rows, D), BF16), jax.ShapeDtypeStruct((rows, D), F32),
                             jax.ShapeDtypeStruct((nblk, 1, N_STATE), F32),
                             jax.ShapeDtypeStruct((rows, N_STATE), F32),
                             *[jax.ShapeDtypeStruct((D, D), BF16)] * n_sh),
                  in_specs=[pl.BlockSpec((tb, D), lambda i: (i, 2)), pl.BlockSpec((tb, D), lambda i: (i, 3)),
                            _full((tb, tb)), _full((tb, tb)),
                            _full((N_Q, 256, Q_W), single=True), _full((N_Q, 256, Q_W), single=True),
                            _full((k_steps, N_STATE)), _full((1, D)), _full((D, D), single=True), _full((1, D))] +
                           [ANY] * n_sh,
                  out_specs=(pl.BlockSpec((tb, D), lambda i: (i, 0)), pl.BlockSpec((tb, D), lambda i: (i, 0)),
                             pl.BlockSpec((None, 1, N_STATE), lambda i: (i, 0, 0)),
                             pl.BlockSpec((tb, N_STATE), lambda i: (i, 0)), *([ANY] * n_sh)),
                  scratch_shapes=[pltpu.VMEM((1, N_STATE), F32),
                                  pltpu.VMEM((SUBLANES, N_STATE), F32), pltpu.VMEM((SUBLANES, N_STATE), F32)] +
                                 _sem_scratch(items),
                  compiler_params=_params(("arbitrary",)))(proj, proj, pm, pmt, wb, wct, ptab, dvec, glu_w, glu_b,
                                                           *shards)


def _head(x, target, proj, y_pool, y_ssm, mod3, norm_post, wbp, wbs, wout):
    rows = x.shape[0]
    tb = _tb(rows, 256)
    nblk = rows // tb
    n_feat = float(D)

    def body(x_ref, t_ref, gp_ref, gs_ref, yp_ref, ys_ref, mod_ref, npost_ref, wbp_ref, wbs_ref, wout_ref,
             loss_ref, dy_ref, dyp_ref, dys_ref, dg_ref, dwbp_hbm, dwbs_hbm, dwout_hbm, vec_ref,
             acc_bp, acc_bs, acc_out, acc_loss, acc_vec):
        i = pl.program_id(0)

        @pl.when(i == 0)
        def _():
            acc_bp[...] = jnp.zeros_like(acc_bp)
            acc_bs[...] = jnp.zeros_like(acc_bs)
            acc_out[...] = jnp.zeros_like(acc_out)
            acc_loss[...] = jnp.zeros_like(acc_loss)
            acc_vec[...] = jnp.zeros_like(acc_vec)

        yp, ys = yp_ref[...], ys_ref[...]
        sgp = jax.nn.sigmoid(gp_ref[...].astype(F32))
        sgs = jax.nn.sigmoid(gs_ref[...].astype(F32))
        pb = _dot(yp, wbp_ref[...])
        psm = _dot(ys, wbs_ref[...])
        mb = (sgp * pb + sgs * psm).astype(BF16)
        out = _dot(mb, wout_ref[...])
        on, r = _rms_parts(out)
        gate = mod_ref[2:3, :]
        npost = npost_ref[...]
        normed = on * npost
        diff = x_ref[...] + gate * normed - t_ref[...]
        acc_loss[...] += jnp.sum(diff * diff, axis=0, keepdims=True)
        dy = diff * (1.0 / n_feat)
        dy_ref[...] = dy
        acc_vec[0:1, :] += jnp.sum(dy * normed, axis=0, keepdims=True)
        dn = dy * gate
        acc_vec[1:2, :] += jnp.sum(dn * on, axis=0, keepdims=True)
        dout = _rms_bwd(dn * npost, on, r).astype(BF16)
        acc_out[...] += _dot_tn(mb, dout)
        dm = _dot_nt(dout, wout_ref[...])
        dpb = (dm * sgp).astype(BF16)
        dps = (dm * sgs).astype(BF16)
        dg_ref[:, :D] = (dm * pb * sgp * (1.0 - sgp)).astype(BF16)
        dg_ref[:, D:] = (dm * psm * sgs * (1.0 - sgs)).astype(BF16)
        acc_bp[...] += _dot_tn(yp, dpb)
        acc_bs[...] += _dot_tn(ys, dps)
        dyp_ref[...] = _dot_nt(dpb, wbp_ref[...]).astype(BF16)
        dys_ref[...] = _dot_nt(dps, wbs_ref[...]).astype(BF16)

        @pl.when(i == nblk - 1)
        def _():
            loss_ref[...] = 0.5 / n_feat * jnp.sum(acc_loss[...], axis=1, keepdims=True)
            vec_ref[...] = acc_vec[...]
            pltpu.sync_copy(acc_bp, dwbp_hbm)
            pltpu.sync_copy(acc_bs, dwbs_hbm)
            pltpu.sync_copy(acc_out, dwout_hbm)

    row = lambda c: pl.BlockSpec((tb, D), lambda i: (i, c))
    w = _full((D, D), single=True)
    return _pcall(body, name="head", grid=(nblk,),
                  out_shape=(jax.ShapeDtypeStruct((1, 1), F32), jax.ShapeDtypeStruct((rows, D), F32),
                             jax.ShapeDtypeStruct((rows, D), BF16), jax.ShapeDtypeStruct((rows, D), BF16),
                             jax.ShapeDtypeStruct((rows, 2 * D), BF16),
                             jax.ShapeDtypeStruct((D, D), F32), jax.ShapeDtypeStruct((D, D), F32),
                             jax.ShapeDtypeStruct((D, D), F32), jax.ShapeDtypeStruct((2, D), F32)),
                  in_specs=[row(0), row(0), row(4), row(5), row(0), row(0), _full((3, D)), _full((1, D)), w, w, w],
                  out_specs=(_full((1, 1)), row(0), row(0), row(0), pl.BlockSpec((tb, 2 * D), lambda i: (i, 0)),
                             ANY, ANY, ANY, _full((2, D))),
                  scratch_shapes=[pltpu.VMEM((D, D), F32), pltpu.VMEM((D, D), F32), pltpu.VMEM((D, D), F32),
                                  pltpu.VMEM((1, D), F32), pltpu.VMEM((2, D), F32)],
                  compiler_params=_params(("arbitrary",)))(x, target, proj, proj, y_pool, y_ssm, mod3, norm_post,
                                                           wbp, wbs, wout)


def _glu_bwd(dys, proj, ys_pre, pm, pmt, glu_w, glu_b):
    rows = dys.shape[0]
    tb = pm.shape[0]
    nblk = rows // tb

    def body(dys_ref, z_ref, ysp_ref, pm_ref, pmt_ref, gw_ref, gb_ref, dyp_ref, dz_ref, dgw_hbm, dgb_ref,
             acc_w, acc_b):
        i = pl.program_id(0)

        @pl.when(i == 0)
        def _():
            acc_w[...] = jnp.zeros_like(acc_w)
            acc_b[...] = jnp.zeros_like(acc_b)

        d_out = _dot(pm_ref[...], dys_ref[...])
        z = _dot(pm_ref[...], z_ref[...])
        yg, dgelu = _gelu_parts(ysp_ref[...])
        ygb = yg.astype(BF16)
        sg = jax.nn.sigmoid(_dot(ygb, gw_ref[...]) + gb_ref[...])
        silu_z, dsilu_z = _silu_parts(z)
        dz = d_out * (yg * sg) * dsilu_z
        dz_ref[...] = _dot(pmt_ref[...], dz.astype(BF16)).astype(BF16)
        dglu = d_out * silu_z
        dq = dglu * yg * sg * (1.0 - sg)
        dqb = dq.astype(BF16)
        acc_b[...] += jnp.sum(dq, axis=0, keepdims=True)
        acc_w[...] += _dot_tn(ygb, dqb)
        dyg = dglu * sg + _dot_nt(dqb, gw_ref[...])
        dyp_ref[...] = (dyg * dgelu).astype(BF16)

        @pl.when(i == nblk - 1)
        def _():
            dgb_ref[...] = acc_b[...]
            pltpu.sync_copy(acc_w, dgw_hbm)

    row = lambda c: pl.BlockSpec((tb, D), lambda i: (i, c))
    return _pcall(body, name="glu_bwd", grid=(nblk,),
                  out_shape=(jax.ShapeDtypeStruct((rows, D), BF16), jax.ShapeDtypeStruct((rows, D), BF16),
                             jax.ShapeDtypeStruct((D, D), F32), jax.ShapeDtypeStruct((1, D), F32)),
                  in_specs=[row(0), row(3), row(0), _full((tb, tb)), _full((tb, tb)),
                            _full((D, D), single=True), _full((1, D))],
                  out_specs=(row(0), row(0), ANY, _full((1, D))),
                  scratch_shapes=[pltpu.VMEM((D, D), F32), pltpu.VMEM((1, D), F32)],
                  compiler_params=_params(("arbitrary",)))(dys, proj, ys_pre, pm, pmt, glu_w, glu_b)


def _ssm_bwd(dyp, proj, states, carries, pm, pmt, wb, wct, ptab, dvec, mat_grads, dpool_w, dw_in_rest):
    rows = dyp.shape[0]
    tb = pm.shape[0]
    k_steps = tb // SUBLANES
    nblk = rows // tb
    n_mat = len(mat_grads)
    hosted = [*mat_grads, dpool_w, dw_in_rest]
    n_h = len(hosted)
    shard_rows = D // N_DEV
    pool_rows = dpool_w.shape[1] // N_DEV
    items = [_scatter_item(t, t, _rows_of(shard_rows)) for t in range(n_mat)] + \
            [_scatter_item(n_mat, n_mat, _pool_rows_of(pool_rows))] + \
            [_w_in_block_item(n_mat + 1, n_mat + 1, j, ssm_part=False) for j in range(W_IN_SHARD // W_IN_BLOCK)]
    n_in, n_out = 10, 5

    def body(*refs):
        dyp_ref, u_ref, s_ref, cin_ref, pm_ref, pmt_ref, wb_ref, wct_ref, p_ref, d_ref = refs[:n_in]
        src_refs = refs[n_in:n_in + n_h]
        du_ref, dbb_ref, dcc_ref, da_ref, dd_ref = refs[n_in + n_h:n_in + n_h + n_out]
        recv_refs = refs[n_in + n_h + n_out:n_in + 2 * n_h + n_out]
        (g_ref, carry_b, fin_ref, acc_wb, acc_wct, acc_da, acc_dd, dup_ref,
         *sems) = refs[n_in + 2 * n_h + n_out:]
        i = pl.program_id(0)

        @pl.when(i == 0)
        def _():
            _hosted_copies(items, src_refs, recv_refs, *sems, act="start")
            carry_b[...] = jnp.zeros_like(carry_b)
            acc_wb[...] = jnp.zeros_like(acc_wb)
            acc_wct[...] = jnp.zeros_like(acc_wct)
            acc_da[...] = jnp.zeros_like(acc_da)
            acc_dd[...] = jnp.zeros_like(acc_dd)

        def keep_own(acc, q, prod):
            for gl in range(16):
                r, c = slice(gl * G_H, (gl + 1) * G_H), (gl // 2) * 128
                acc[q, r, 0:128] += prod[r, c:c + 128]
                acc[q, r, 128:256] += prod[r, Q_W // 2 + c:Q_W // 2 + c + 128]

        dy = dyp_ref[...]
        up = _dot(pm_ref[...], u_ref[...]).astype(BF16)
        acc_dd[...] += jnp.sum(dy.astype(F32) * up.astype(F32), axis=0, keepdims=True)
        for q in range(N_Q):
            cols = slice(q * 256, (q + 1) * 256)
            g_ref[:, q * Q_W:(q + 1) * Q_W] = _dot(dy[:, cols], wct_ref[q])
            keep_own(acc_wct, q, _dot_tn(dy[:, cols], s_ref[:, q * Q_W:(q + 1) * Q_W].astype(BF16)))
        for q in range(N_Q):
            _scan_backward(q, g_ref, s_ref, p_ref, carry_b, cin_ref, fin_ref, acc_da, k_steps)
        for q in range(N_Q):
            cols = slice(q * 256, (q + 1) * 256)
            lam = g_ref[:, q * Q_W:(q + 1) * Q_W].astype(BF16)
            keep_own(acc_wb, q, _dot_tn(up[:, cols], lam))
            dup_ref[:, cols] = (_dot_nt(lam, wb_ref[q]) + d_ref[:, cols] * dy[:, cols].astype(F32)).astype(BF16)
        du_ref[...] = _dot(pmt_ref[...], dup_ref[...]).astype(BF16)

        @pl.when(i == nblk - 1)
        def _():
            da_ref[...] = acc_da[...]
            dd_ref[...] = acc_dd[...]
            lane = lax.broadcasted_iota(jnp.int32, (16 * G_H, 128), 1)
            row = lax.broadcasted_iota(jnp.int32, (16 * G_H, 128), 0)
            own = lane // G_P == (row // G_H) % 2
            spread = (lax.broadcasted_iota(jnp.int32, (G_P, 128), 1) % G_P ==
                      lax.broadcasted_iota(jnp.int32, (G_P, 128), 0)).astype(F32)
            for acc, out in ((acc_wb, dbb_ref), (acc_wct, dcc_ref)):
                for half in range(2):
                    for q in range(N_Q):
                        kept = jnp.where(own, acc[q, :, half * 128:(half + 1) * 128], 0.0)
                        out[half, q] = lax.dot_general(kept, spread, (((1,), (1,)), ((), ())),
                                                       preferred_element_type=F32, precision=lax.Precision.HIGHEST)
            _hosted_copies(items, src_refs, recv_refs, *sems, act="wait")

    rev = lambda c: pl.BlockSpec((tb, D), lambda i: (nblk - 1 - i, c))
    recv = [jax.ShapeDtypeStruct((N_DEV, shard_rows, D), F32)] * n_mat + \
           [jax.ShapeDtypeStruct((N_DEV, dpool_w.shape[0], pool_rows, dpool_w.shape[2]), F32),
            jax.ShapeDtypeStruct((N_DEV, D, W_IN_SHARD), BF16)]
    return _pcall(body, name="ssm_bwd", grid=(nblk,),
                  out_shape=(jax.ShapeDtypeStruct((rows, D), BF16),
                             jax.ShapeDtypeStruct((2, N_Q, 16 * G_H, G_P), F32),
                             jax.ShapeDtypeStruct((2, N_Q, 16 * G_H, G_P), F32),
                             jax.ShapeDtypeStruct((1, N_STATE), F32), jax.ShapeDtypeStruct((1, D), F32), *recv),
                  in_specs=[rev(0), rev(2), pl.BlockSpec((tb, N_STATE), lambda i: (nblk - 1 - i, 0)),
                            pl.BlockSpec((None, 1, N_STATE), lambda i: (nblk - 1 - i, 0, 0)),
                            _full((tb, tb)), _full((tb, tb)),
                            _full((N_Q, 256, Q_W), single=True), _full((N_Q, 256, Q_W), single=True),
                            _full((k_steps, N_STATE)), _full((1, D))] + [ANY] * n_h,
                  out_specs=(rev(0), _full((2, N_Q, 16 * G_H, G_P)), _full((2, N_Q, 16 * G_H, G_P)),
                             _full((1, N_STATE)), _full((1, D)), *([ANY] * n_h)),
                  scratch_shapes=[pltpu.VMEM((tb, N_STATE), F32), pltpu.VMEM((1, N_STATE), F32),
                                  pltpu.VMEM((SUBLANES, N_STATE), F32),
                                  pltpu.VMEM((N_Q, 16 * G_H, 256), F32), pltpu.VMEM((N_Q, 16 * G_H, 256), F32),
                                  pltpu.VMEM((1, N_STATE), F32), pltpu.VMEM((1, D), F32),
                                  pltpu.VMEM((tb, D), BF16)] + _sem_scratch(items),
                  compiler_params=_params(("arbitrary",), vmem=60 * 1024 * 1024),
                  )(dyp, proj, states, carries, pm, pmt, wb, wct, ptab, dvec, *hosted)


def _pool_bwd(dyp, proj, pool_w, pool_scale):
    rows = dyp.shape[0]
    tb = _tb(rows, 512)
    nblk = rows // tb
    hb = tb // HALO

    def body(dy_ref, u_ref, halo_ref, z_ref, pw_ref, ps_ref, dp_ref, dpw_ref, dps_ref, ahead_ref):
        i = pl.program_id(0)
        blk = nblk - 1 - i

        @pl.when(i == 0)
        def _():
            ahead_ref[...] = jnp.zeros_like(ahead_ref)
            dpw_ref[...] = jnp.zeros_like(dpw_ref)
            dps_ref[...] = jnp.zeros_like(dps_ref)

        u = u_ref[...].astype(F32)
        halo = jnp.where(blk > 0, halo_ref[...].astype(F32), 0.0)
        pooled, inv_counts = _pool_windows(jnp.concatenate([halo, u], axis=0), tb, blk * tb)
        silu_z, dsilu_z = _silu_parts(z_ref[...].astype(F32))
        dy = dy_ref[...].astype(F32)
        for g, w in enumerate(POOL_WINDOWS):
            cols = slice(g * 256, (g + 1) * 256)
            pooled_b = pooled[g].astype(BF16)
            mixed = _dot(pooled_b, pw_ref[g])
            scale = ps_ref[:, cols]
            dp_ref[:, D + g * 256:D + (g + 1) * 256] = (dy[:, cols] * (mixed * scale) * dsilu_z[:, cols]).astype(BF16)
            dms = dy[:, cols] * silu_z[:, cols]
            dps_ref[:, cols] += jnp.sum(dms * mixed, axis=0, keepdims=True)
            dmixed = (dms * scale).astype(BF16)
            dpw_ref[g] += _dot_tn(pooled_b, dmixed)
            dpooled = _dot_nt(dmixed, pw_ref[g])
            ratio = dpooled * inv_counts[g]
            acc = jnp.concatenate([ratio, ahead_ref[:, cols]], axis=0)
            ahead_ref[:, cols] = ratio[:HALO, :]
            s = 1
            while s < w:
                acc = acc + pltpu.roll(acc, tb + HALO - s, axis=0)
                s *= 2
            dp_ref[:, cols] = (acc[:tb, :] - dpooled).astype(BF16)

    rev = lambda c: pl.BlockSpec((tb, D), lambda i: (nblk - 1 - i, c))
    return _pcall(body, name="pool_bwd", grid=(nblk,),
                  out_shape=(jax.ShapeDtypeStruct((rows, 2 * D), BF16), jax.ShapeDtypeStruct((4, 256, 256), F32),
                             jax.ShapeDtypeStruct((1, D), F32)),
                  in_specs=[rev(0), rev(0),
                            pl.BlockSpec((HALO, D), lambda i: (jnp.maximum((nblk - 1 - i) * hb - 1, 0), 0)),
                            rev(1), _full((4, 256, 256)), _full((1, D))],
                  out_specs=(pl.BlockSpec((tb, 2 * D), lambda i: (nblk - 1 - i, 0)), _full((4, 256, 256)),
                             _full((1, D))),
                  scratch_shapes=[pltpu.VMEM((HALO, D), F32)],
                  compiler_params=_params(("arbitrary",)))(dyp, proj, proj, proj, pool_w, pool_scale)


def _dproj_specs(tb):
    return [pl.BlockSpec((tb, 2 * D), lambda i: (i, 0)), pl.BlockSpec((tb, D), lambda i: (i, 0)),
            pl.BlockSpec((tb, D), lambda i: (i, 0)), pl.BlockSpec((tb, 2 * D), lambda i: (i, 0))]


def _in_proj_bwd_x(x, dy, dpp, dus, dzs, dpg, mod3, norm_pre, w_in, dw_in_ssm, recv_w_in):
    rows = x.shape[0]
    tb = _tb(rows, 512)
    nblk = rows // tb
    items = [_w_in_block_item(0, 0, j, ssm_part=True) for j in range(W_IN_SHARD // W_IN_BLOCK)]
    sums_item = [_Item(0, 0, _whole, _slot)]

    def body(x_ref, dy_ref, dpp_ref, dus_ref, dzs_ref, dpg_ref, mod_ref, np_ref, w_ref,
             dw_src, _, gx_ref, recv_w, recv_sums, vec_ref, ssem, rsem, lsem, *sums_sems):
        src_refs, recv_refs, sems = (dw_src,), (recv_w,), (ssem, rsem, lsem)

        @pl.when(pl.program_id(0) == 0)
        def _():
            _hosted_copies(items, src_refs, recv_refs, *sems, act="start")
            vec_ref[...] = jnp.zeros_like(vec_ref)

        dh = _dot_nt(dpp_ref[...], w_ref[:, 0:2 * D])
        dh += _dot_nt(dus_ref[...], w_ref[:, 2 * D:3 * D])
        dh += _dot_nt(dzs_ref[...], w_ref[:, 3 * D:4 * D])
        dh += _dot_nt(dpg_ref[...], w_ref[:, 4 * D:6 * D])
        xn, r, _ = _prenorm(x_ref[...], mod_ref[...], np_ref[...])
        one_scale = 1.0 + mod_ref[1:2, :]
        vec_ref[0:1, :] += jnp.sum(dh, axis=0, keepdims=True)
        vec_ref[1:2, :] += jnp.sum(dh * xn, axis=0, keepdims=True) * np_ref[...]
        vec_ref[2:3, :] += jnp.sum(dh * xn, axis=0, keepdims=True) * one_scale
        gx_ref[...] = dy_ref[...] + _rms_bwd(dh * (np_ref[...] * one_scale), xn, r)

        @pl.when(pl.program_id(0) == nblk - 1)
        def _():
            _hosted_copies(sums_item, (vec_ref,), (recv_sums,), *sums_sems, act="start")
            _hosted_copies(items, src_refs, recv_refs, *sems, act="wait")
            _hosted_copies(sums_item, (vec_ref,), (recv_sums,), *sums_sems, act="wait")

    row = pl.BlockSpec((tb, D), lambda i: (i, 0))
    recv = (jax.ShapeDtypeStruct(recv_w_in.shape, recv_w_in.dtype), jax.ShapeDtypeStruct((N_DEV, 3, D), F32))
    return _pcall(body, name="in_proj_bwd_x", grid=(nblk,),
                  out_shape=(jax.ShapeDtypeStruct((rows, D), F32), *recv),
                  in_specs=[row, row] + _dproj_specs(tb) + [_full((3, D)), _full((1, D)),
                                                            _full((D, N_IN), single=True)] + [ANY] * 2,
                  out_specs=(row, ANY, ANY),
                  input_output_aliases={10: 1},
                  scratch_shapes=[pltpu.VMEM((3, D), F32)] + _sem_scratch(items) + _sem_scratch(sums_item),
                  compiler_params=_params(("arbitrary",)))(x, dy, dpp, dus, dzs, dpg, mod3, norm_pre, w_in,
                                                           dw_in_ssm, recv_w_in)


def _in_proj_bwd_w(name, x, dparts, mod3, norm_pre, gathered=()):
    rows = x.shape[0]
    tb = _tb(rows, 512)
    nblk = rows // tb
    widths = [p.shape[1] for p in dparts]
    n_p, n_g = len(dparts), len(gathered)
    items = [_Item(t, t, _whole, _slot) for t in range(n_g)]

    def body(x_ref, *rest):
        part_refs, (mod_ref, np_ref) = rest[:n_p], rest[n_p:n_p + 2]
        src_refs, dw_ref = rest[n_p + 2:n_p + 2 + n_g], rest[n_p + 2 + n_g]
        recv_refs, (acc, *sems) = rest[n_p + 3 + n_g:n_p + 3 + 2 * n_g], rest[n_p + 3 + 2 * n_g:]
        i = pl.program_id(0)

        @pl.when(i == 0)
        def _():
            if n_g:
                _hosted_copies(items, src_refs, recv_refs, *sems, act="start")
            acc[...] = jnp.zeros_like(acc)

        _, _, h = _prenorm(x_ref[...], mod_ref[...], np_ref[...])
        ht = h.astype(BF16)
        lo = 0
        for ref, w in zip(part_refs, widths):
            acc[:, lo:lo + w] += _dot_tn(ht, ref[...])
            lo += w

        @pl.when(i == nblk - 1)
        def _():
            dw_ref[...] = acc[...].astype(BF16)
            if n_g:
                _hosted_copies(items, src_refs, recv_refs, *sems, act="wait")

    row = pl.BlockSpec((tb, D), lambda i: (i, 0))
    out = _pcall(body, name=name, grid=(nblk,),
                 out_shape=(jax.ShapeDtypeStruct((D, sum(widths)), BF16),
                            *[jax.ShapeDtypeStruct((N_DEV,) + g.shape, g.dtype) for g in gathered]),
                 in_specs=[row] + [pl.BlockSpec((tb, w), lambda i: (i, 0)) for w in widths] +
                          [_full((3, D)), _full((1, D))] + [ANY] * n_g,
                 out_specs=(_full((D, sum(widths))), *([ANY] * n_g)),
                 scratch_shapes=[pltpu.VMEM((D, sum(widths)), F32)] + (_sem_scratch(items) if n_g else []),
                 compiler_params=_params(("arbitrary",)))(x, *dparts, mod3, norm_pre, *gathered)
    return out if n_g else out[0]


def _adamw_math(w, g, m, v):
    m = ADAM_B1 * m + (1.0 - ADAM_B1) * g
    v = ADAM_B2 * v + (1.0 - ADAM_B2) * (g * g)
    m_hat = m / (1.0 - ADAM_B1 ** ADAM_STEP)
    v_hat = v / (1.0 - ADAM_B2 ** ADAM_STEP)
    delta = -ADAM_LR * (m_hat / (jnp.sqrt(v_hat) + ADAM_EPS) + ADAM_WD * w)
    return delta, m, v


def _sum_sources(ref):
    g = ref[0].astype(F32)
    for s in range(1, N_DEV):
        g = g + ref[s].astype(F32)
    return g


def _adamw_reduce(name, parts, w, m, v):
    r, c = w.shape
    tr = r if r * c <= 256 * 1024 else max(8, (256 * 1024 // c) // 8 * 8)
    while r % tr:
        tr -= 8

    def body(p_ref, w_ref, m_ref, v_ref, g_ref, d_ref, nm_ref, nv_ref):
        g = _sum_sources(p_ref)
        g_ref[...] = g
        d_ref[...], nm_ref[...], nv_ref[...] = _adamw_math(w_ref[...], g, m_ref[...], v_ref[...])

    blk = pl.BlockSpec((tr, c), lambda i: (i, 0))
    return _pcall(body, name=name, grid=(r // tr,),
                  out_shape=tuple([jax.ShapeDtypeStruct((r, c), F32)] * 4),
                  in_specs=[pl.BlockSpec((N_DEV, tr, c), lambda i: (0, i, 0)), blk, blk, blk],
                  out_specs=(blk, blk, blk, blk),
                  compiler_params=_params(("arbitrary",)))(parts, w, m, v)


def _adamw_small(gs, ws, ms, vs):
    n = len(gs)

    def body(*refs):
        ins, outs = refs[:4 * n], refs[4 * n:]
        for t in range(n):
            g_ref, w_ref, m_ref, v_ref = ins[4 * t:4 * t + 4]
            outs[3 * t][...], outs[3 * t + 1][...], outs[3 * t + 2][...] = _adamw_math(
                w_ref[...], g_ref[...], m_ref[...], v_ref[...])

    vm = pl.BlockSpec(memory_space=pltpu.VMEM)
    flat = [a for t in range(n) for a in (gs[t], ws[t], ms[t], vs[t])]
    return _pcall(body, name="adamw_small",
                  out_shape=tuple(jax.ShapeDtypeStruct(w.shape, F32) for w in ws for _ in range(3)),
                  in_specs=[vm] * (4 * n), out_specs=tuple([vm] * (3 * n)), compiler_params=_params())(*flat)


def _sum_small(parts):
    n = len(parts)

    def body(*refs):
        for t in range(n):
            refs[n + t][...] = _sum_sources(refs[t])

    vm = pl.BlockSpec(memory_space=pltpu.VMEM)
    return _pcall(body, name="sum_small",
                  out_shape=tuple(jax.ShapeDtypeStruct(p.shape[1:], F32) for p in parts),
                  in_specs=[vm] * n, out_specs=tuple([vm] * n), compiler_params=_params())(*parts)


def _ada_update(c_all, dmod_cols, w, m, v):
    def body(c_ref, dm_ref, w_ref, m_ref, v_ref, g_ref, d_ref, nm_ref, nv_ref):
        ca = c_ref[...]
        g = lax.dot_general(ca * jax.nn.sigmoid(ca), dm_ref[...], (((0,), (0,)), ((), ())),
                            preferred_element_type=F32, precision=lax.Precision.HIGHEST)
        g_ref[...] = g
        d_ref[...], nm_ref[...], nv_ref[...] = _adamw_math(w_ref[...], g, m_ref[...], v_ref[...])

    vm = pl.BlockSpec(memory_space=pltpu.VMEM)
    return _pcall(body, name="ada_update", out_shape=tuple([jax.ShapeDtypeStruct(w.shape, F32)] * 4),
                  in_specs=[vm] * 5, out_specs=(vm, vm, vm, vm), compiler_params=_params())(c_all, dmod_cols, w, m, v)


def kernel(x, c, w_ada, b_ada, norm_pre, norm_post, w_in, pool_w, pool_scale, ssm_a_re, ssm_a_im, ssm_log_dt, ssm_b_re, ssm_b_im, ssm_c_re, ssm_c_im, ssm_d, glu_w, glu_b, w_branch_pool, w_branch_ssm, w_out, loss_target, m_w_ada, m_b_ada, m_norm_pre, m_norm_post, m_w_in, m_pool_w, m_pool_scale, m_ssm_a_re, m_ssm_a_im, m_ssm_log_dt, m_ssm_b_re, m_ssm_b_im, m_ssm_c_re, m_ssm_c_im, m_ssm_d, m_glu_w, m_glu_b, m_w_branch_pool, m_w_branch_ssm, m_w_out, v_w_ada, v_b_ada, v_norm_pre, v_norm_post, v_w_in, v_pool_w, v_pool_scale, v_ssm_a_re, v_ssm_a_im, v_ssm_log_dt, v_ssm_b_re, v_ssm_b_im, v_ssm_c_re, v_ssm_c_im, v_ssm_d, v_glu_w, v_glu_b, v_w_branch_pool, v_w_branch_ssm, v_w_out):
    given = dict(locals())
    me = _flat(_me())
    rows = x.shape[1]
    x2 = x[0]
    target = loss_target[0]
    ada_cols = w_ada.shape[2]

    b_ada_s = lax.dynamic_slice(b_ada, (0, me * ada_cols), (1, ada_cols))
    c_all, mod_rows = _ada_exchange(c, w_ada[0], b_ada_s)
    mod3 = mod_rows.reshape(3, D)

    shards = _cast_shards([w_in[0], pool_w[0], glu_w[0], w_branch_pool[0], w_branch_ssm[0], w_out[0]])

    tb_ssm = _tb(rows, 256)
    k_steps = tb_ssm // SUBLANES
    a_re, a_im = ssm_a_re[0], ssm_a_im[0]
    log_dt = ssm_log_dt[0].reshape(GROUPS, 1)
    b_re_t, b_im_t = ssm_b_re[0].transpose(0, 2, 1), ssm_b_im[0].transpose(0, 2, 1)
    wb, wct, pow_re, pow_im = _s5_prep(a_re, a_im, log_dt, b_re_t, b_im_t, ssm_c_re[0], ssm_c_im[0], k_steps)
    ptab = _state_layout(pow_re, pow_im)
    dvec = ssm_d[0].reshape(1, D)
    pm = _perm_matrix(tb_ssm)
    pmt = pm.T

    proj, w_in_g, pool_w_g, glu_g = _in_proj(x2, mod3, norm_pre, shards[0], shards[1:3])
    y_pool = _pool_fwd(proj, pool_w_g, pool_scale)
    y_ssm, ys_pre, carries, states, wbp_g, wbs_g, wout_g = _ssm_fwd(
        proj, pm, pmt, wb, wct, ptab, dvec, glu_g, glu_b, shards[3:])
    loss_part, dy, dyp, dys, dpg, dwbp, dwbs, dwout, head_vec = _head(
        x2, target, proj, y_pool, y_ssm, mod3, norm_post, wbp_g, wbs_g, wout_g)

    dpp, dpool_w, dpool_scale = _pool_bwd(dyp, proj, pool_w_g, pool_scale)
    dw_in_rest = _in_proj_bwd_w("in_proj_bwd_w_rest", x2, [dpp, dpg], mod3, norm_pre)
    dy_pre, dzs, dglu_w, dglu_b = _glu_bwd(dys, proj, ys_pre, pm, pmt, glu_g, glu_b)
    dus, dbb, dcc, dabar, dd, p_glu, p_wbp, p_wbs, p_wout, p_pool_w, p_w_in = _ssm_bwd(
        dy_pre, proj, states, carries, pm, pmt, wb, wct, ptab, dvec, [dglu_w, dwbp, dwbs, dwout], dpool_w, dw_in_rest)

    small32 = jnp.concatenate([head_vec, dpool_scale, dglu_b, dd, jnp.broadcast_to(loss_part, (1, D)),
                               jnp.zeros((2, D), F32), dabar.reshape(8, D)], axis=0)
    small16 = jnp.concatenate([dbb.reshape(2 * GROUPS, D), dcc.reshape(2 * GROUPS, D)], axis=0).astype(BF16)
    dw_in_ssm, p_small32, p_small16 = _in_proj_bwd_w("in_proj_bwd_w_ssm", x2, [dus, dzs], mod3, norm_pre,
                                                     gathered=(small32, small16))
    grad_x, p_w_in, p_pre = _in_proj_bwd_x(x2, dy, dpp, dus, dzs, dpg, mod3, norm_pre, w_in_g, dw_in_ssm, p_w_in)

    tot32, tot16, tot_pre = _sum_small([p_small32, p_small16, p_pre])
    d_abar_re, d_abar_im = _state_unlayout(tot32[8:16].reshape(N_STATE))
    d_bb_re, d_bb_im = tot16[0:64].reshape(GROUPS, G_H, G_P), tot16[64:128].reshape(GROUPS, G_H, G_P)
    g_a_re, g_a_im, g_log_dt, g_b_re_t, g_b_im_t = _s5_prep_bwd(
        a_re, a_im, log_dt, b_re_t, b_im_t, d_abar_re, d_abar_im, d_bb_re, d_bb_im)

    grads, deltas, new_m, new_v = {}, {}, {}, {}

    small = []

    def small_update(name, g2):
        small.append((name, g2))

    def shard_update(name, parts):
        shape = given[name].shape
        r2 = parts.shape[1:] if parts.ndim == 3 else (parts.shape[1] * parts.shape[2], parts.shape[3])
        w2, m2, v2 = (given[p + name].reshape(r2) for p in ("", "m_", "v_"))
        out = _adamw_reduce("adamw_" + name, parts.reshape((N_DEV,) + tuple(r2)), w2, m2, v2)
        grads[name], deltas[name], new_m[name], new_v[name] = (a.reshape(shape) for a in out)

    dmod_all = jnp.concatenate([p_pre[:, 0:2, :], p_small32[:, 0:1, :]], axis=1).reshape(N_DEV, 3 * D)
    dmod_cols = lax.dynamic_slice(dmod_all, (0, me * ada_cols), (N_DEV, ada_cols))
    out = _ada_update(c_all, dmod_cols, w_ada[0], m_w_ada[0], v_w_ada[0])
    grads['w_ada'], deltas['w_ada'], new_m['w_ada'], new_v['w_ada'] = (a.reshape(w_ada.shape) for a in out)

    small_update('b_ada', jnp.concatenate([tot_pre[0:2], tot32[0:1]], axis=0).reshape(1, 3 * D))
    small_update('norm_pre', tot_pre[2:3])
    small_update('norm_post', tot32[1:2])
    small_update('pool_scale', tot32[2:3])
    small_update('glu_b', tot32[3:4])
    small_update('ssm_d', tot32[4:5])
    small_update('ssm_a_re', g_a_re)
    small_update('ssm_a_im', g_a_im)
    small_update('ssm_log_dt', g_log_dt.reshape(1, GROUPS))
    small_update('ssm_b_re', g_b_re_t.transpose(0, 2, 1).reshape(GROUPS, G_P * G_H))
    small_update('ssm_b_im', g_b_im_t.transpose(0, 2, 1).reshape(GROUPS, G_P * G_H))
    small_update('ssm_c_re', tot16[128:192])
    small_update('ssm_c_im', -tot16[192:256])
    flat = _adamw_small([g2 for _, g2 in small],
                        *[[given[p + name].reshape(g2.shape) for name, g2 in small] for p in ("", "m_", "v_")])
    for t, (name, g2) in enumerate(small):
        shape = given[name].shape
        grads[name], deltas[name], new_m[name], new_v[name] = (
            a.reshape(shape) for a in (g2, *flat[3 * t:3 * t + 3]))
    shard_update('w_in', p_w_in)
    shard_update('pool_w', p_pool_w)
    shard_update('glu_w', p_glu)
    shard_update('w_branch_pool', p_wbp)
    shard_update('w_branch_ssm', p_wbs)
    shard_update('w_out', p_wout)

    return (tot32[5, 0], grad_x[None], *[grads[n] for n in WEIGHTS], *[deltas[n] for n in WEIGHTS],
            *[new_m[n] for n in WEIGHTS], *[new_v[n] for n in WEIGHTS])
```

```python
import functools
import math
from typing import Callable, NamedTuple, Optional

import jax
import jax.numpy as jnp
from jax import lax
from jax.experimental import pallas as pl
from jax.experimental.pallas import tpu as pltpu

F32 = jnp.float32
BF16 = jnp.bfloat16
MESH = pl.DeviceIdType.MESH

D = 1024
N_DEV = 8
N_IN = 6 * D
GROUPS = 64
G_H = 16
G_P = 64
N_Q = 4
Q_W = 2 * 16 * G_P
N_STATE = N_Q * Q_W
POOL_WINDOWS = (2, 4, 8, 16)
HALO = 16
RMS_EPS = 1e-6
SUBLANES = 8
LANE_CHUNK = 512
SCAN_UNROLL = 2
HEAD_PARTS = 1
VMEM_LIMIT = 56 * 1024 * 1024

ADAM_LR = 0.001
ADAM_B1 = 0.9
ADAM_B2 = 0.999
ADAM_EPS = 1e-08
ADAM_WD = 0.01
ADAM_STEP = 10

WEIGHTS = ['w_ada', 'b_ada', 'norm_pre', 'norm_post', 'w_in', 'pool_w', 'pool_scale', 'ssm_a_re',
           'ssm_a_im', 'ssm_log_dt', 'ssm_b_re', 'ssm_b_im', 'ssm_c_re', 'ssm_c_im', 'ssm_d', 'glu_w',
           'glu_b', 'w_branch_pool', 'w_branch_ssm', 'w_out']


def _pcall(body, **kw):
    return pl.pallas_call(body, **kw)


def _params(sem=None, vmem=VMEM_LIMIT):
    return pltpu.CompilerParams(dimension_semantics=sem, vmem_limit_bytes=vmem)


def _tb(rows, pref):
    return pref if rows % pref == 0 and rows // pref >= 2 else rows // 2


def _full(shape, single=False):
    nd = len(shape)
    if single:
        return pl.BlockSpec(shape, lambda i: (0,) * nd, pipeline_mode=pl.Buffered(1))
    return pl.BlockSpec(shape, lambda i: (0,) * nd)


ANY = pl.BlockSpec(memory_space=pl.ANY)


def _me():
    return lax.axis_index("x"), lax.axis_index("y"), lax.axis_index("c")


def _flat(p):
    return 4 * p[0] + 2 * p[1] + p[2]


def _peer(k):
    x, y, c = _me()
    return (1 - x if k & 4 else x, 1 - y if k & 2 else y, 1 - c if k & 1 else c)


def _silu_parts(z):
    s = jax.nn.sigmoid(z)
    return z * s, s * (1.0 + z * (1.0 - s))


_GELU_C = math.sqrt(2.0 / math.pi)


def _gelu_parts(x):
    x2 = x * x
    t = jnp.tanh(_GELU_C * (x + 0.044715 * x * x2))
    g = 0.5 * x * (1.0 + t)
    dg = 0.5 * (1.0 + t) + 0.5 * x * (1.0 - t * t) * (_GELU_C * (1.0 + 3.0 * 0.044715 * x2))
    return g, dg


def _dot(a, b):
    return jnp.dot(a, b, preferred_element_type=F32)


def _dot_nt(a, b):
    return lax.dot_general(a, b, (((1,), (1,)), ((), ())), preferred_element_type=F32)


def _dot_tn(a, b):
    return lax.dot_general(a, b, (((0,), (0,)), ((), ())), preferred_element_type=F32)


def _rms_parts(x):
    r = lax.rsqrt(jnp.mean(x * x, axis=-1, keepdims=True) + RMS_EPS)
    return x * r, r


def _rms_bwd(dxn, xn, r):
    return r * (dxn - xn * jnp.mean(dxn * xn, axis=-1, keepdims=True))


def _ada_exchange(c, w_ada_s, b_ada_s):
    cols = w_ada_s.shape[1]

    def body(c_ref, w_ref, b_ref, call_ref, mod_ref, part_ref, ssem, rsem, lsem):
        me3 = _me()
        me = _flat(me3)
        mine = pltpu.make_async_copy(c_ref, call_ref.at[pl.ds(me, 1), :], lsem.at[0])
        mine.start()
        sends = []
        for k in range(1, N_DEV):
            cp = pltpu.make_async_remote_copy(src_ref=c_ref, dst_ref=call_ref.at[pl.ds(me, 1), :],
                                              send_sem=ssem.at[k - 1], recv_sem=rsem.at[k - 1],
                                              device_id=_peer(k), device_id_type=MESH)
            cp.start()
            sends.append(cp)
        mine.wait()
        for k in range(1, N_DEV):
            p = _flat(_peer(k))
            pltpu.make_async_remote_copy(src_ref=c_ref, dst_ref=call_ref.at[pl.ds(p, 1), :],
                                         send_sem=ssem.at[k - 1], recv_sem=rsem.at[k - 1],
                                         device_id=_peer(k), device_id_type=MESH).wait_recv()
        for cp in sends:
            cp.wait_send()
        ca = call_ref[...]
        act = ca * jax.nn.sigmoid(ca)
        part_ref[...] = jnp.dot(act, w_ref[...], preferred_element_type=F32,
                                precision=lax.Precision.HIGHEST) + b_ref[...]
        own = pltpu.make_async_copy(part_ref.at[pl.ds(me, 1), :], mod_ref.at[pl.ds(me, 1), :], lsem.at[1])
        own.start()
        sends = []
        for k in range(1, N_DEV):
            p = _flat(_peer(k))
            s = N_DEV - 1 + k - 1
            cp = pltpu.make_async_remote_copy(src_ref=part_ref.at[pl.ds(p, 1), :],
                                              dst_ref=mod_ref.at[pl.ds(me, 1), :],
                                              send_sem=ssem.at[s], recv_sem=rsem.at[s],
                                              device_id=_peer(k), device_id_type=MESH)
            cp.start()
            sends.append(cp)
        own.wait()
        for k in range(1, N_DEV):
            p = _flat(_peer(k))
            s = N_DEV - 1 + k - 1
            pltpu.make_async_remote_copy(src_ref=part_ref.at[pl.ds(p, 1), :],
                                         dst_ref=mod_ref.at[pl.ds(p, 1), :],
                                         send_sem=ssem.at[s], recv_sem=rsem.at[s],
                                         device_id=_peer(k), device_id_type=MESH).wait_recv()
        for cp in sends:
            cp.wait_send()

    vm = pl.BlockSpec(memory_space=pltpu.VMEM)
    return _pcall(
        body, name="ada_exchange",
        out_shape=(jax.ShapeDtypeStruct((N_DEV, D), F32), jax.ShapeDtypeStruct((N_DEV, cols), F32)),
        in_specs=[vm, vm, vm], out_specs=(vm, vm),
        scratch_shapes=[pltpu.VMEM((N_DEV, cols), F32),
                        pltpu.SemaphoreType.DMA((2 * (N_DEV - 1),)),
                        pltpu.SemaphoreType.DMA((2 * (N_DEV - 1),)),
                        pltpu.SemaphoreType.DMA((2,))],
    )(c, w_ada_s, b_ada_s)


class _Item(NamedTuple):
    src: int
    out: int
    src_view: Callable
    dst_view: Callable
    pred: Optional[Callable] = None


def _when(pred, dest, fn):
    if pred is None:
        fn()
    else:
        pl.when(pred(dest))(fn)


def _n_sems(items):
    return len(items) * (N_DEV - 1)


def _hosted_copies(items, srcs, outs, ssem, rsem, lsem, act):
    me = _flat(_me())
    for t, it in enumerate(items):
        local = lambda t=t, it=it: pltpu.make_async_copy(
            it.src_view(srcs[it.src], me), it.dst_view(outs[it.out], me), lsem.at[t])
        if act == "start":
            _when(it.pred, me, lambda local=local: local().start())
        else:
            _when(it.pred, me, lambda local=local: local().wait())
    for k in range(1, N_DEV):
        p3 = _peer(k)
        p = _flat(p3)
        for t, it in enumerate(items):
            s = t * (N_DEV - 1) + k - 1
            send = lambda it=it, s=s, p=p, p3=p3: pltpu.make_async_remote_copy(
                src_ref=it.src_view(srcs[it.src], p), dst_ref=it.dst_view(outs[it.out], me),
                send_sem=ssem.at[s], recv_sem=rsem.at[s], device_id=p3, device_id_type=MESH)
            recv = lambda it=it, s=s, p=p, p3=p3: pltpu.make_async_remote_copy(
                src_ref=it.src_view(srcs[it.src], p), dst_ref=it.dst_view(outs[it.out], p),
                send_sem=ssem.at[s], recv_sem=rsem.at[s], device_id=p3, device_id_type=MESH)
            if act == "start":
                _when(it.pred, p, lambda send=send: send().start())
            else:
                _when(it.pred, me, lambda recv=recv: recv().wait_recv())
                _when(it.pred, p, lambda send=send: send().wait_send())


def _sem_scratch(items):
    return [pltpu.SemaphoreType.DMA((_n_sems(items),)), pltpu.SemaphoreType.DMA((_n_sems(items),)),
            pltpu.SemaphoreType.DMA((len(items),))]


def _exchange(name, srcs, out_structs, items):
    n_src, n_out = len(srcs), len(out_structs)

    def body(*refs):
        src_refs, out_refs = refs[:n_src], refs[n_src:n_src + n_out]
        sems = refs[n_src + n_out:]
        _hosted_copies(items, src_refs, out_refs, *sems, act="start")
        _hosted_copies(items, src_refs, out_refs, *sems, act="wait")

    return _pcall(body, name=name, out_shape=tuple(out_structs),
                  in_specs=[ANY] * n_src, out_specs=tuple([ANY] * n_out),
                  scratch_shapes=_sem_scratch(items))(*srcs)


def _whole(ref, dest):
    return ref


def _slot(ref, sender):
    return ref.at[sender]


def _rows_of(rows):
    return lambda ref, dev: ref.at[pl.ds(dev * rows, rows), :]


def _cols_of(cols):
    return lambda ref, dev: ref.at[:, pl.ds(dev * cols, cols)]


def _pool_rows_of(rows):
    return lambda ref, dev: ref.at[:, pl.ds(dev * rows, rows), :]


def _gather_item(src, out, dst_view):
    return _Item(src, out, _whole, dst_view)


def _scatter_item(src, out, src_view):
    return _Item(src, out, src_view, _slot)


W_IN_BLOCK = 256
W_IN_SHARD = N_IN // N_DEV
SSM_BLOCKS = (2 * D // W_IN_BLOCK, 4 * D // W_IN_BLOCK)


def _w_in_block_item(src, out, j, ssm_part):
    def block(dest):
        return (W_IN_SHARD // W_IN_BLOCK) * dest + j

    def in_ssm(dest):
        b = block(dest)
        return (b >= SSM_BLOCKS[0]) & (b < SSM_BLOCKS[1])

    def src_view(ref, dest):
        b = block(dest)
        local = b - SSM_BLOCKS[0] if ssm_part else jnp.where(b < SSM_BLOCKS[0], b, b - (SSM_BLOCKS[1] - SSM_BLOCKS[0]))
        local = jnp.clip(local, 0, ref.shape[1] // W_IN_BLOCK - 1)
        return ref.at[:, pl.ds(local * W_IN_BLOCK, W_IN_BLOCK)]

    def dst_view(ref, sender):
        return ref.at[sender, :, pl.ds(j * W_IN_BLOCK, W_IN_BLOCK)]

    pred = in_ssm if ssm_part else (lambda dest: jnp.logical_not(in_ssm(dest)))
    return _Item(src, out, src_view, dst_view, pred)


def _cast_shards(arrs):
    def body(*refs):
        n = len(refs) // 2
        for i in range(n):
            refs[n + i][...] = refs[i][...].astype(BF16)

    vm = pl.BlockSpec(memory_space=pltpu.VMEM)
    return _pcall(body, name="cast_shards",
                  out_shape=tuple(jax.ShapeDtypeStruct(a.shape, BF16) for a in arrs),
                  in_specs=[vm] * len(arrs), out_specs=tuple([vm] * len(arrs)),
                  compiler_params=_params())(*arrs)


def _s5_discretise(a_re, a_im, log_dt, b_re_t, b_im_t):
    dt = jnp.exp(log_dt)
    lam_re = jnp.minimum(a_re, -1e-4)
    lam_im = a_im
    mag = jnp.exp(lam_re * dt)
    abar_re = mag * jnp.cos(lam_im * dt)
    abar_im = mag * jnp.sin(lam_im * dt)
    den = lam_re * lam_re + lam_im * lam_im
    num_re = abar_re - 1.0
    f_re = (num_re * lam_re + abar_im * lam_im) / den
    f_im = (abar_im * lam_re - num_re * lam_im) / den
    f_re, f_im = f_re[:, None, :], f_im[:, None, :]
    bb_re = f_re * b_re_t - f_im * b_im_t
    bb_im = f_re * b_im_t + f_im * b_re_t
    return abar_re, abar_im, bb_re, bb_im


def _group_masks():
    spread = lax.broadcasted_iota(jnp.int32, (G_P, 16 * G_P), 1) % G_P == lax.broadcasted_iota(
        jnp.int32, (G_P, 16 * G_P), 0)
    own = lax.broadcasted_iota(jnp.int32, (16 * G_H, 16 * G_P), 0) // G_H == lax.broadcasted_iota(
        jnp.int32, (16 * G_H, 16 * G_P), 1) // G_P
    return spread, own


def _s5_prep(a_re, a_im, log_dt, b_re_t, b_im_t, c_re, c_im, n_pow):
    def body(ar_ref, ai_ref, ld_ref, br_ref, bi_ref, cr_ref, ci_ref, wb_ref, wct_ref, pr_ref, pi_ref):
        abar_re, abar_im, bb_re, bb_im = _s5_discretise(ar_ref[...], ai_ref[...], ld_ref[...], br_ref[...], bi_ref[...])
        spread, own = _group_masks()
        spread = spread.astype(BF16)
        for ref, parts in ((wb_ref, (bb_re, bb_im)), (wct_ref, (cr_ref[...], -ci_ref[...]))):
            for half, t in enumerate(parts):
                for q in range(N_Q):
                    blocks = t[q * 16:(q + 1) * 16].reshape(16 * G_H, G_P).astype(BF16)
                    dense = jnp.where(own, _dot(blocks, spread), 0.0)
                    ref[q, :, half * (Q_W // 2):(half + 1) * (Q_W // 2)] = dense.astype(BF16)
        p_re, p_im = abar_re, abar_im
        pr_ref[0] = p_re
        pi_ref[0] = p_im
        for k in range(1, n_pow):
            p_re, p_im = p_re * abar_re - p_im * abar_im, p_re * abar_im + p_im * abar_re
            pr_ref[k] = p_re
            pi_ref[k] = p_im

    vm = pl.BlockSpec(memory_space=pltpu.VMEM)
    return _pcall(body, name="s5_prep",
                  out_shape=(jax.ShapeDtypeStruct((N_Q, 16 * G_H, Q_W), BF16),
                             jax.ShapeDtypeStruct((N_Q, 16 * G_H, Q_W), BF16),
                             jax.ShapeDtypeStruct((n_pow, GROUPS, G_P), F32),
                             jax.ShapeDtypeStruct((n_pow, GROUPS, G_P), F32)),
                  in_specs=[vm] * 7, out_specs=(vm, vm, vm, vm), compiler_params=_params(),
                  )(a_re, a_im, log_dt, b_re_t, b_im_t, c_re, c_im)


def _s5_prep_bwd(a_re, a_im, log_dt, b_re_t, b_im_t, d_abar_re, d_abar_im, d_bb_re, d_bb_im):
    def body(ar_ref, ai_ref, ld_ref, br_ref, bi_ref, dar_ref, dai_ref, dbr_ref, dbi_ref,
             gar_ref, gai_ref, gld_ref, gbr_ref, gbi_ref):
        _, vjp = jax.vjp(_s5_discretise, ar_ref[...], ai_ref[...], ld_ref[...], br_ref[...], bi_ref[...])
        g = vjp((dar_ref[...], dai_ref[...], dbr_ref[...], dbi_ref[...]))
        gar_ref[...] = g[0]
        gai_ref[...] = g[1]
        gld_ref[...] = g[2]
        gbr_ref[...] = g[3]
        gbi_ref[...] = g[4]

    vm = pl.BlockSpec(memory_space=pltpu.VMEM)
    ins = (a_re, a_im, log_dt, b_re_t, b_im_t)
    return _pcall(body, name="s5_prep_bwd",
                  out_shape=tuple(jax.ShapeDtypeStruct(a.shape, F32) for a in ins),
                  in_specs=[vm] * 9, out_specs=tuple([vm] * 5), compiler_params=_params(),
                  )(*ins, d_abar_re, d_abar_im, d_bb_re, d_bb_im)


def _state_layout(re, im):
    lead = re.shape[:-2]
    r = re.reshape(lead + (N_Q, 1, 16 * G_P))
    i = im.reshape(lead + (N_Q, 1, 16 * G_P))
    return jnp.concatenate([r, i], axis=-2).reshape(lead + (N_STATE,))


def _state_unlayout(v):
    v4 = v.reshape(N_Q, 2, 16, G_P)
    return v4[:, 0].reshape(GROUPS, G_P), v4[:, 1].reshape(GROUPS, G_P)


def _perm_matrix(tb):
    k_steps = tb // SUBLANES
    r = jnp.arange(tb)
    src = (r % SUBLANES) * k_steps + r // SUBLANES
    return (src[:, None] == jnp.arange(tb)[None, :]).astype(BF16)


def _lane_chunks(q):
    for lc in range(Q_W // 2 // LANE_CHUNK):
        re = q * Q_W + lc * LANE_CHUNK
        yield re, re + Q_W // 2


def _steps(lo, hi, body, init):
    if hi - lo <= SCAN_UNROLL:
        for k in range(lo, hi):
            init = body(k, init)
        return init
    trips = (hi - lo) // SCAN_UNROLL

    def trip(j, carry):
        for u in range(SCAN_UNROLL):
            carry = body(lo + j * SCAN_UNROLL + u, carry)
        return carry

    carry = lax.fori_loop(0, trips, trip, init)
    for k in range(lo + trips * SCAN_UNROLL, hi):
        carry = body(k, carry)
    return carry


def _tile(k):
    if isinstance(k, int):
        return pl.ds(k * SUBLANES, SUBLANES)
    return pl.ds(pl.multiple_of(k * SUBLANES, SUBLANES), SUBLANES)


def _scan_forward(q, s_ref, p_ref, carry_ref, enter_ref, fin_ref, k_steps):
    for re, im in _lane_chunks(q):
        lr, li = pl.ds(re, LANE_CHUNK), pl.ds(im, LANE_CHUNK)
        a_re = jnp.broadcast_to(p_ref[0:1, lr], (SUBLANES, LANE_CHUNK))
        a_im = jnp.broadcast_to(p_ref[0:1, li], (SUBLANES, LANE_CHUNK))

        def local(k, st):
            sr, si = st
            rows = _tile(k)
            nr = a_re * sr - a_im * si + s_ref[rows, lr]
            ni = a_re * si + a_im * sr + s_ref[rows, li]
            s_ref[rows, lr] = nr
            s_ref[rows, li] = ni
            return nr, ni

        zero = jnp.zeros((SUBLANES, LANE_CHUNK), F32)
        fr, fi = _steps(0, k_steps, local, (zero, zero))
        fin_ref[:, lr] = fr
        fin_ref[:, li] = fi
        ak_re, ak_im = p_ref[k_steps - 1:k_steps, lr], p_ref[k_steps - 1:k_steps, li]
        c_re, c_im = carry_ref[:, lr], carry_ref[:, li]
        for seg in range(SUBLANES):
            enter_ref[seg:seg + 1, lr] = c_re
            enter_ref[seg:seg + 1, li] = c_im
            f_re, f_im = fin_ref[seg:seg + 1, lr], fin_ref[seg:seg + 1, li]
            c_re, c_im = f_re + ak_re * c_re - ak_im * c_im, f_im + ak_re * c_im + ak_im * c_re
        carry_ref[:, lr] = c_re
        carry_ref[:, li] = c_im
        e_re, e_im = enter_ref[:, lr], enter_ref[:, li]

        def fix(k, _):
            rows = _tile(k)
            p_re = p_ref[pl.ds(k, 1), lr]
            p_im = p_ref[pl.ds(k, 1), li]
            s_ref[rows, lr] = s_ref[rows, lr] + (p_re * e_re - p_im * e_im)
            s_ref[rows, li] = s_ref[rows, li] + (p_re * e_im + p_im * e_re)
            return 0

        _steps(0, k_steps, fix, 0)


def _scan_backward(q, g_ref, s_ref, p_ref, carry_ref, s_in_ref, fin_ref, da_ref, k_steps):
    seg_id = lax.broadcasted_iota(jnp.int32, (SUBLANES, LANE_CHUNK), 0)
    for re, im in _lane_chunks(q):
        lr, li = pl.ds(re, LANE_CHUNK), pl.ds(im, LANE_CHUNK)
        a_re = jnp.broadcast_to(p_ref[0:1, lr], (SUBLANES, LANE_CHUNK))
        a_im = jnp.broadcast_to(p_ref[0:1, li], (SUBLANES, LANE_CHUNK))

        def local(j, st):
            sr, si = st
            rows = _tile(k_steps - 1 - j)
            nr = a_re * sr + a_im * si + g_ref[rows, lr]
            ni = a_re * si - a_im * sr + g_ref[rows, li]
            g_ref[rows, lr] = nr
            g_ref[rows, li] = ni
            return nr, ni

        zero = jnp.zeros((SUBLANES, LANE_CHUNK), F32)
        fr, fi = _steps(0, k_steps, local, (zero, zero))
        fin_ref[:, lr] = fr
        fin_ref[:, li] = fi
        ak_re, ak_im = p_ref[k_steps - 1:k_steps, lr], p_ref[k_steps - 1:k_steps, li]
        c_re, c_im = carry_ref[:, lr], carry_ref[:, li]
        lam_in = [None] * SUBLANES
        for seg in reversed(range(SUBLANES)):
            lam_in[seg] = (c_re, c_im)
            f_re, f_im = fin_ref[seg:seg + 1, lr], fin_ref[seg:seg + 1, li]
            c_re, c_im = f_re + ak_re * c_re + ak_im * c_im, f_im + ak_re * c_im - ak_im * c_re
        carry_ref[:, lr] = c_re
        carry_ref[:, li] = c_im
        for seg in range(SUBLANES):
            fin_ref[seg:seg + 1, lr] = lam_in[seg][0]
            fin_ref[seg:seg + 1, li] = lam_in[seg][1]
        e_re, e_im = fin_ref[:, lr], fin_ref[:, li]

        def fix_with(k, acc, sp_re, sp_im):
            acc_re, acc_im = acc
            rows = _tile(k)
            p_re = p_ref[pl.ds(k_steps - 1 - k, 1), lr]
            p_im = p_ref[pl.ds(k_steps - 1 - k, 1), li]
            l_re = g_ref[rows, lr] + (p_re * e_re + p_im * e_im)
            l_im = g_ref[rows, li] + (p_re * e_im - p_im * e_re)
            g_ref[rows, lr] = l_re
            g_ref[rows, li] = l_im
            return acc_re + (l_re * sp_re + l_im * sp_im), acc_im + (l_im * sp_re - l_re * sp_im)

        def fix(k, acc):
            prev = _tile(k - 1)
            return fix_with(k, acc, s_ref[prev, lr], s_ref[prev, li])

        last = _tile(k_steps - 1)
        before_re = jnp.where(seg_id == 0, s_in_ref[:, lr], pltpu.roll(s_ref[last, lr], 1, axis=0))
        before_im = jnp.where(seg_id == 0, s_in_ref[:, li], pltpu.roll(s_ref[last, li], 1, axis=0))
        acc = fix_with(0, (zero, zero), before_re, before_im)
        acc_re, acc_im = _steps(1, k_steps, fix, acc)
        da_ref[:, lr] = da_ref[:, lr] + jnp.sum(acc_re, axis=0, keepdims=True)
        da_ref[:, li] = da_ref[:, li] + jnp.sum(acc_im, axis=0, keepdims=True)


def _prenorm(x, mod3, norm_pre):
    xn, r = _rms_parts(x)
    return xn, r, xn * norm_pre * (1.0 + mod3[1:2, :]) + mod3[0:1, :]


CHIP_FLIPS = (4, 2, 6)


def _shard_order(me):
    flips = [0, 1] + [f + c for f in CHIP_FLIPS for c in (0, 1)]
    return jnp.stack([me ^ f for f in flips]).astype(jnp.int32)


def _in_proj(x, mod3, norm_pre, w_in_s, shards):
    rows = x.shape[0]
    tb = _tb(rows, 2048)
    nblk = rows // tb
    n_sh = len(shards)
    last_step = N_DEV - 1
    items = [_gather_item(0, 0, _pool_rows_of(shards[0].shape[1]))] + \
            [_gather_item(t, t, _rows_of(shards[t].shape[0])) for t in range(1, n_sh)]

    def body(order_ref, x_ref, mod_ref, np_ref, w_src, *rest):
        src_refs, proj_ref, w_full, out_refs = rest[:n_sh], rest[n_sh], rest[n_sh + 1], rest[n_sh + 2:2 * n_sh + 2]
        h_scr, wg, ssem, rsem, lsem, *sems = rest[2 * n_sh + 2:]
        s, i = pl.program_id(0), pl.program_id(1)
        me3 = _me()
        me = _flat(me3)
        sibling = _peer(1)

        def own_copy(slot, k):
            return pltpu.make_async_remote_copy(src_ref=w_src, dst_ref=wg.at[me], send_sem=ssem.at[slot],
                                                recv_sem=rsem.at[slot], device_id=_peer(k), device_id_type=MESH)

        def passed_copy(j):
            p = _flat(_peer(CHIP_FLIPS[j]))
            return pltpu.make_async_remote_copy(src_ref=wg.at[p], dst_ref=wg.at[p], send_sem=ssem.at[4 + j],
                                                recv_sem=rsem.at[4 + j], device_id=sibling, device_id_type=MESH)

        def arrival(slot, flip):
            p = _flat(_peer(flip))
            pltpu.make_async_remote_copy(src_ref=w_src, dst_ref=wg.at[p], send_sem=ssem.at[slot],
                                         recv_sem=rsem.at[slot], device_id=sibling, device_id_type=MESH).wait_recv()

        def keep(t):
            p = order_ref[t]
            return pltpu.make_async_copy(wg.at[p], w_full.at[:, pl.ds(p * W_IN_SHARD, W_IN_SHARD)], lsem.at[1 + t])

        first = i == 0
        for t in range(last_step):
            pl.when(first & (s == t + 1))(lambda t=t: keep(t).start())

        @pl.when(first & (s == 0))
        def _():
            mine = pltpu.make_async_copy(w_src, wg.at[me], lsem.at[0])
            mine.start()
            own_copy(0, 1).start()
            for j, f in enumerate(CHIP_FLIPS[:2]):
                own_copy(1 + j, f).start()
            mine.wait()

        @pl.when(first & (s == 1))
        def _():
            arrival(0, 1)

        for j, f in enumerate(CHIP_FLIPS):
            @pl.when(first & (s == 2 + 2 * j))
            def _(j=j, f=f):
                arrival(1 + j, f)
                passed_copy(j).start()
                if j == 0:
                    own_copy(3, CHIP_FLIPS[2]).start()

            @pl.when(first & (s == 3 + 2 * j))
            def _(j=j, f=f):
                arrival(4 + j, f + 1)

        @pl.when(first & (s == last_step - 1))
        def _():
            _hosted_copies(items, src_refs, out_refs, *sems, act="start")

        rows_i = pl.ds(pl.multiple_of(i * tb, tb), tb)

        @pl.when(s == 0)
        def _():
            _, _, h = _prenorm(x_ref[...], mod_ref[...], np_ref[...])
            h_scr[rows_i, :] = h.astype(BF16)

        proj_ref[...] = _dot(h_scr[rows_i, :], wg[order_ref[s]]).astype(BF16)

        @pl.when((s == last_step) & (i == nblk - 1))
        def _():
            own_copy(0, 1).wait_send()
            for j, f in enumerate(CHIP_FLIPS):
                own_copy(1 + j, f).wait_send()
                passed_copy(j).wait_send()
            keep(last_step).start()
            for t in range(N_DEV):
                keep(t).wait()
            _hosted_copies(items, src_refs, out_refs, *sems, act="wait")

    full = [jax.ShapeDtypeStruct((4, 256, 256), BF16)] + [jax.ShapeDtypeStruct((D, D), BF16)] * (n_sh - 1)
    grid_spec = pltpu.PrefetchScalarGridSpec(
        num_scalar_prefetch=1, grid=(N_DEV, nblk),
        in_specs=[pl.BlockSpec((tb, D), lambda s, i, order: (jnp.where(s == 0, i, nblk - 1), 0)),
                  pl.BlockSpec((3, D), lambda s, i, order: (0, 0)), pl.BlockSpec((1, D), lambda s, i, order: (0, 0)),
                  ANY] + [ANY] * n_sh,
        out_specs=(pl.BlockSpec((tb, W_IN_SHARD), lambda s, i, order: (i, order[s])), ANY, *([ANY] * n_sh)),
        scratch_shapes=[pltpu.VMEM((rows, D), BF16), pltpu.VMEM((N_DEV, D, W_IN_SHARD), BF16),
                        pltpu.SemaphoreType.DMA((N_DEV - 1,)), pltpu.SemaphoreType.DMA((N_DEV - 1,)),
                        pltpu.SemaphoreType.DMA((1 + N_DEV,))] + _sem_scratch(items))
    return _pcall(body, name="in_proj", grid_spec=grid_spec,
                  out_shape=(jax.ShapeDtypeStruct((rows, N_IN), BF16), jax.ShapeDtypeStruct((D, N_IN), BF16), *full),
                  compiler_params=_params(("arbitrary", "arbitrary")),
                  )(_shard_order(_flat(_me())), x, mod3, norm_pre, w_in_s, *shards)


def _pool_windows(ext, tb, first_row):
    inv_counts = _inv_counts(tb, first_row)
    pooled = []
    for g, w in enumerate(POOL_WINDOWS):
        acc = ext[:, g * 256:(g + 1) * 256]
        tok = acc[HALO:, :]
        s = 1
        while s < w:
            acc = acc + pltpu.roll(acc, s, axis=0)
            s *= 2
        pooled.append(acc[HALO:, :] * inv_counts[g] - tok)
    return pooled, inv_counts


def _inv_counts(tb, first_row):
    pos = (first_row + lax.broadcasted_iota(jnp.int32, (tb, 1), 0) + 1).astype(F32)
    return [1.0 / jnp.minimum(pos, float(w)) for w in POOL_WINDOWS]


def _pool_fwd(proj, pool_w, pool_scale):
    rows = proj.shape[0]
    tb = _tb(rows, 512)
    hb = tb // HALO

    def body(u_ref, halo_ref, z_ref, pw_ref, ps_ref, y_ref, pooled_ref):
        i = pl.program_id(0)
        u = u_ref[...].astype(F32)
        halo = jnp.where(i > 0, halo_ref[...].astype(F32), 0.0)
        pooled, _ = _pool_windows(jnp.concatenate([halo, u], axis=0), tb, i * tb)
        silu_z, _ = _silu_parts(z_ref[...].astype(F32))
        for g in range(4):
            cols = slice(g * 256, (g + 1) * 256)
            pooled_b = pooled[g].astype(BF16)
            pooled_ref[:, cols] = pooled_b
            mixed = _dot(pooled_b, pw_ref[g])
            y_ref[:, cols] = (mixed * ps_ref[:, cols] * silu_z[:, cols]).astype(BF16)

    blk = pl.BlockSpec((tb, D), lambda i: (i, 0))
    return _pcall(body, name="pool_fwd", grid=(rows // tb,),
                  out_shape=(jax.ShapeDtypeStruct((rows, D), BF16), jax.ShapeDtypeStruct((rows, D), BF16)),
                  in_specs=[blk, pl.BlockSpec((HALO, D), lambda i: (jnp.maximum(i * hb - 1, 0), 0)),
                            pl.BlockSpec((tb, D), lambda i: (i, 1)),
                            _full((4, 256, 256)), _full((1, D))],
                  out_specs=(blk, blk),
                  compiler_params=_params(("arbitrary",)))(proj, proj, proj, pool_w, pool_scale)


def _ssm_fwd(proj, pm, pmt, wb, wct, ptab, dvec, glu_w, glu_b, shards):
    rows = proj.shape[0]
    tb = pm.shape[0]
    k_steps = tb // SUBLANES
    nblk = rows // tb
    n_sh = len(shards)
    items = [_gather_item(t, t, _rows_of(shards[t].shape[0])) for t in range(n_sh)]

    def body(u_ref, z_ref, pm_ref, pmt_ref, wb_ref, wct_ref, p_ref, d_ref, gw_ref, gb_ref, *rest):
        src_refs = rest[:n_sh]
        y_ref, ys_ref, carry_out_ref, s_ref = rest[n_sh:n_sh + 4]
        out_refs = rest[n_sh + 4:2 * n_sh + 4]
        carry_ref, enter_ref, fin_ref, *sems = rest[2 * n_sh + 4:]

        @pl.when(pl.program_id(0) == 0)
        def _():
            _hosted_copies(items, src_refs, out_refs, *sems, act="start")
            carry_ref[...] = jnp.zeros_like(carry_ref)

        carry_out_ref[...] = carry_ref[...]
        up = _dot(pm_ref[...], u_ref[...]).astype(BF16)

        for q in range(N_Q):
            s_ref[:, q * Q_W:(q + 1) * Q_W] = _dot(up[:, q * 256:(q + 1) * 256], wb_ref[q])
        for q in range(N_Q):
            _scan_forward(q, s_ref, p_ref, carry_ref, enter_ref, fin_ref, k_steps)
        for q in range(N_Q):
            cols = slice(q * 256, (q + 1) * 256)
            y = _dot_nt(s_ref[:, q * Q_W:(q + 1) * Q_W].astype(BF16), wct_ref[q])
            ys_ref[:, cols] = y + d_ref[:, cols] * up[:, cols].astype(F32)
        yg, _ = _gelu_parts(ys_ref[...])
        gate = jax.nn.sigmoid(_dot(yg.astype(BF16), gw_ref[...]) + gb_ref[...])
        zp = _dot(pm_ref[...], z_ref[...])
        silu_z, _ = _silu_parts(zp)
        y_ref[...] = _dot(pmt_ref[...], (yg * gate * silu_z).astype(BF16)).astype(BF16)

        @pl.when(pl.program_id(0) == nblk - 1)
        def _():
            _hosted_copies(items, src_refs, out_refs, *sems, act="wait")

    return _pcall(body, name="ssm_fwd", grid=(nblk,),
                  out_shape=(jax.ShapeDtypeStruct((rows, D), BF16), jax.ShapeDtypeStruct((rows, D), F32),
                             jax.ShapeDtypeStruct((nblk, 1, N_STATE), F32),
                             jax.ShapeDtypeStruct((rows, N_STATE), F32),
                             *[jax.ShapeDtypeStruct((D, D), BF16)] * n_sh),
                  in_specs=[pl.BlockSpec((tb, D), lambda i: (i, 2)), pl.BlockSpec((tb, D), lambda i: (i, 3)),
                            _full((tb, tb)), _full((tb, tb)),
                            _full((N_Q, 256, Q_W), single=True), _full((N_Q, 256, Q_W), single=True),
                            _full((k_steps, N_STATE)), _full((1, D)), _full((D, D), single=True), _full((1, D))] +
                           [ANY] * n_sh,
                  out_specs=(pl.BlockSpec((tb, D), lambda i: (i, 0)), pl.BlockSpec((tb, D), lambda i: (i, 0)),
                             pl.BlockSpec((None, 1, N_STATE), lambda i: (i, 0, 0)),
                             pl.BlockSpec((tb, N_STATE), lambda i: (i, 0)), *([ANY] * n_sh)),
                  scratch_shapes=[pltpu.VMEM((1, N_STATE), F32),
                                  pltpu.VMEM((SUBLANES, N_STATE), F32), pltpu.VMEM((SUBLANES, N_STATE), F32)] +
                                 _sem_scratch(items),
                  compiler_params=_params(("arbitrary",)))(proj, proj, pm, pmt, wb, wct, ptab, dvec, glu_w, glu_b,
                                                           *shards)


def _head(x, target, proj, y_pool, y_ssm, mod3, norm_post, wbp, wbs, wout):
    rows = x.shape[0]
    tb = _tb(rows, 256)
    nblk = rows // tb
    n_feat = float(D)

    def body(x_ref, t_ref, gp_ref, gs_ref, yp_ref, ys_ref, mod_ref, npost_ref, wbp_ref, wbs_ref, wout_ref,
             loss_ref, dy_ref, dyp_ref, dys_ref, dg_ref, dwbp_hbm, dwbs_hbm, dwout_hbm, vec_ref,
             acc_bp, acc_bs, acc_out, acc_loss, acc_vec):
        i = pl.program_id(0)

        @pl.when(i == 0)
        def _():
            acc_bp[...] = jnp.zeros_like(acc_bp)
            acc_bs[...] = jnp.zeros_like(acc_bs)
            acc_out[...] = jnp.zeros_like(acc_out)
            acc_loss[...] = jnp.zeros_like(acc_loss)
            acc_vec[...] = jnp.zeros_like(acc_vec)

        gate = mod_ref[2:3, :]
        npost = npost_ref[...]
        keep = []
        for part in range(HEAD_PARTS):
            rs = slice(part * tb // HEAD_PARTS, (part + 1) * tb // HEAD_PARTS)
            yp, ys = yp_ref[rs, :], ys_ref[rs, :]
            sgp = jax.nn.sigmoid(gp_ref[rs, :].astype(F32))
            sgs = jax.nn.sigmoid(gs_ref[rs, :].astype(F32))
            pb = _dot(yp, wbp_ref[...])
            psm = _dot(ys, wbs_ref[...])
            mb = (sgp * pb + sgs * psm).astype(BF16)
            out = _dot(mb, wout_ref[...])
            on, r = _rms_parts(out)
            normed = on * npost
            diff = x_ref[rs, :] + gate * normed - t_ref[rs, :]
            acc_loss[...] += jnp.sum(diff * diff, axis=0, keepdims=True)
            dy = diff * (1.0 / n_feat)
            dy_ref[rs, :] = dy
            acc_vec[0:1, :] += jnp.sum(dy * normed, axis=0, keepdims=True)
            dn = dy * gate
            acc_vec[1:2, :] += jnp.sum(dn * on, axis=0, keepdims=True)
            dout = _rms_bwd(dn * npost, on, r).astype(BF16)
            dm = _dot_nt(dout, wout_ref[...])
            dpb = (dm * sgp).astype(BF16)
            dps = (dm * sgs).astype(BF16)
            dg_ref[rs, :D] = (dm * pb * sgp * (1.0 - sgp)).astype(BF16)
            dg_ref[rs, D:] = (dm * psm * sgs * (1.0 - sgs)).astype(BF16)
            dyp_ref[rs, :] = _dot_nt(dpb, wbp_ref[...]).astype(BF16)
            dys_ref[rs, :] = _dot_nt(dps, wbs_ref[...]).astype(BF16)
            keep.append((mb, dout, dpb, dps))
        cat = lambda k: jnp.concatenate([p[k] for p in keep], axis=0) if HEAD_PARTS > 1 else keep[0][k]
        acc_out[...] += _dot_tn(cat(0), cat(1))
        acc_bp[...] += _dot_tn(yp_ref[...], cat(2))
        acc_bs[...] += _dot_tn(ys_ref[...], cat(3))

        @pl.when(i == nblk - 1)
        def _():
            loss_ref[...] = 0.5 / n_feat * jnp.sum(acc_loss[...], axis=1, keepdims=True)
            vec_ref[...] = acc_vec[...]
            pltpu.sync_copy(acc_bp, dwbp_hbm)
            pltpu.sync_copy(acc_bs, dwbs_hbm)
            pltpu.sync_copy(acc_out, dwout_hbm)

    row = lambda c: pl.BlockSpec((tb, D), lambda i: (i, c))
    w = _full((D, D), single=True)
    return _pcall(body, name="head", grid=(nblk,),
                  out_shape=(jax.ShapeDtypeStruct((1, 1), F32), jax.ShapeDtypeStruct((rows, D), F32),
                             jax.ShapeDtypeStruct((rows, D), BF16), jax.ShapeDtypeStruct((rows, D), BF16),
                             jax.ShapeDtypeStruct((rows, 2 * D), BF16),
                             jax.ShapeDtypeStruct((D, D), F32), jax.ShapeDtypeStruct((D, D), F32),
                             jax.ShapeDtypeStruct((D, D), F32), jax.ShapeDtypeStruct((2, D), F32)),
                  in_specs=[row(0), row(0), row(4), row(5), row(0), row(0), _full((3, D)), _full((1, D)), w, w, w],
                  out_specs=(_full((1, 1)), row(0), row(0), row(0), pl.BlockSpec((tb, 2 * D), lambda i: (i, 0)),
                             ANY, ANY, ANY, _full((2, D))),
                  scratch_shapes=[pltpu.VMEM((D, D), F32), pltpu.VMEM((D, D), F32), pltpu.VMEM((D, D), F32),
                                  pltpu.VMEM((1, D), F32), pltpu.VMEM((2, D), F32)],
                  compiler_params=_params(("arbitrary",)))(x, target, proj, proj, y_pool, y_ssm, mod3, norm_post,
                                                           wbp, wbs, wout)


def _glu_bwd(dys, proj, ys_pre, pm, pmt, glu_w, glu_b):
    rows = dys.shape[0]
    tb = pm.shape[0]
    nblk = rows // tb

    def body(dys_ref, z_ref, ysp_ref, pm_ref, pmt_ref, gw_ref, gb_ref, dyp_ref, dz_ref, dgw_hbm, dgb_ref,
             acc_w, acc_b):
        i = pl.program_id(0)

        @pl.when(i == 0)
        def _():
            acc_w[...] = jnp.zeros_like(acc_w)
            acc_b[...] = jnp.zeros_like(acc_b)

        d_out = _dot(pm_ref[...], dys_ref[...])
        z = _dot(pm_ref[...], z_ref[...])
        yg, dgelu = _gelu_parts(ysp_ref[...])
        ygb = yg.astype(BF16)
        sg = jax.nn.sigmoid(_dot(ygb, gw_ref[...]) + gb_ref[...])
        silu_z, dsilu_z = _silu_parts(z)
        dz = d_out * (yg * sg) * dsilu_z
        dz_ref[...] = _dot(pmt_ref[...], dz.astype(BF16)).astype(BF16)
        dglu = d_out * silu_z
        dq = dglu * yg * sg * (1.0 - sg)
        dqb = dq.astype(BF16)
        acc_b[...] += jnp.sum(dq, axis=0, keepdims=True)
        acc_w[...] += _dot_tn(ygb, dqb)
        dyg = dglu * sg + _dot_nt(dqb, gw_ref[...])
        dyp_ref[...] = (dyg * dgelu).astype(BF16)

        @pl.when(i == nblk - 1)
        def _():
            dgb_ref[...] = acc_b[...]
            pltpu.sync_copy(acc_w, dgw_hbm)

    row = lambda c: pl.BlockSpec((tb, D), lambda i: (i, c))
    return _pcall(body, name="glu_bwd", grid=(nblk,),
                  out_shape=(jax.ShapeDtypeStruct((rows, D), BF16), jax.ShapeDtypeStruct((rows, D), BF16),
                             jax.ShapeDtypeStruct((D, D), F32), jax.ShapeDtypeStruct((1, D), F32)),
                  in_specs=[row(0), row(3), row(0), _full((tb, tb)), _full((tb, tb)),
                            _full((D, D), single=True), _full((1, D))],
                  out_specs=(row(0), row(0), ANY, _full((1, D))),
                  scratch_shapes=[pltpu.VMEM((D, D), F32), pltpu.VMEM((1, D), F32)],
                  compiler_params=_params(("arbitrary",)))(dys, proj, ys_pre, pm, pmt, glu_w, glu_b)


def _ssm_bwd(dyp, proj, states, carries, pm, pmt, wb, wct, ptab, dvec, mat_grads, dpool_w, dw_in_rest):
    rows = dyp.shape[0]
    tb = pm.shape[0]
    k_steps = tb // SUBLANES
    nblk = rows // tb
    n_mat = len(mat_grads)
    hosted = [*mat_grads, dpool_w, dw_in_rest]
    n_h = len(hosted)
    shard_rows = D // N_DEV
    pool_rows = dpool_w.shape[1] // N_DEV
    items = [_scatter_item(t, t, _rows_of(shard_rows)) for t in range(n_mat)] + \
            [_scatter_item(n_mat, n_mat, _pool_rows_of(pool_rows))] + \
            [_w_in_block_item(n_mat + 1, n_mat + 1, j, ssm_part=False) for j in range(W_IN_SHARD // W_IN_BLOCK)]
    n_in, n_out = 10, 5

    def body(*refs):
        dyp_ref, u_ref, s_ref, cin_ref, pm_ref, pmt_ref, wb_ref, wct_ref, p_ref, d_ref = refs[:n_in]
        src_refs = refs[n_in:n_in + n_h]
        du_ref, dbb_ref, dcc_ref, da_ref, dd_ref = refs[n_in + n_h:n_in + n_h + n_out]
        recv_refs = refs[n_in + n_h + n_out:n_in + 2 * n_h + n_out]
        (g_ref, carry_b, fin_ref, acc_wb, acc_wct, acc_da, acc_dd, dup_ref,
         *sems) = refs[n_in + 2 * n_h + n_out:]
        i = pl.program_id(0)

        @pl.when(i == 0)
        def _():
            _hosted_copies(items, src_refs, recv_refs, *sems, act="start")
            carry_b[...] = jnp.zeros_like(carry_b)
            acc_wb[...] = jnp.zeros_like(acc_wb)
            acc_wct[...] = jnp.zeros_like(acc_wct)
            acc_da[...] = jnp.zeros_like(acc_da)
            acc_dd[...] = jnp.zeros_like(acc_dd)

        def keep_own(acc, q, prod):
            for gl in range(16):
                r, c = slice(gl * G_H, (gl + 1) * G_H), (gl // 2) * 128
                acc[q, r, 0:128] += prod[r, c:c + 128]
                acc[q, r, 128:256] += prod[r, Q_W // 2 + c:Q_W // 2 + c + 128]

        dy = dyp_ref[...]
        up = _dot(pm_ref[...], u_ref[...]).astype(BF16)
        acc_dd[...] += jnp.sum(dy.astype(F32) * up.astype(F32), axis=0, keepdims=True)
        for q in range(N_Q):
            cols = slice(q * 256, (q + 1) * 256)
            g_ref[:, q * Q_W:(q + 1) * Q_W] = _dot(dy[:, cols], wct_ref[q])
            keep_own(acc_wct, q, _dot_tn(dy[:, cols], s_ref[:, q * Q_W:(q + 1) * Q_W].astype(BF16)))
        for q in range(N_Q):
            _scan_backward(q, g_ref, s_ref, p_ref, carry_b, cin_ref, fin_ref, acc_da, k_steps)
        for q in range(N_Q):
            cols = slice(q * 256, (q + 1) * 256)
            lam = g_ref[:, q * Q_W:(q + 1) * Q_W].astype(BF16)
            keep_own(acc_wb, q, _dot_tn(up[:, cols], lam))
            dup_ref[:, cols] = (_dot_nt(lam, wb_ref[q]) + d_ref[:, cols] * dy[:, cols].astype(F32)).astype(BF16)
        du_ref[...] = _dot(pmt_ref[...], dup_ref[...]).astype(BF16)

        @pl.when(i == nblk - 1)
        def _():
            da_ref[...] = acc_da[...]
            dd_ref[...] = acc_dd[...]
            lane = lax.broadcasted_iota(jnp.int32, (16 * G_H, 128), 1)
            row = lax.broadcasted_iota(jnp.int32, (16 * G_H, 128), 0)
            own = lane // G_P == (row // G_H) % 2
            spread = (lax.broadcasted_iota(jnp.int32, (G_P, 128), 1) % G_P ==
                      lax.broadcasted_iota(jnp.int32, (G_P, 128), 0)).astype(F32)
            for acc, out in ((acc_wb, dbb_ref), (acc_wct, dcc_ref)):
                for half in range(2):
                    for q in range(N_Q):
                        kept = jnp.where(own, acc[q, :, half * 128:(half + 1) * 128], 0.0)
                        out[half, q] = lax.dot_general(kept, spread, (((1,), (1,)), ((), ())),
                                                       preferred_element_type=F32, precision=lax.Precision.HIGHEST)
            _hosted_copies(items, src_refs, recv_refs, *sems, act="wait")

    rev = lambda c: pl.BlockSpec((tb, D), lambda i: (nblk - 1 - i, c))
    recv = [jax.ShapeDtypeStruct((N_DEV, shard_rows, D), F32)] * n_mat + \
           [jax.ShapeDtypeStruct((N_DEV, dpool_w.shape[0], pool_rows, dpool_w.shape[2]), F32),
            jax.ShapeDtypeStruct((N_DEV, D, W_IN_SHARD), BF16)]
    return _pcall(body, name="ssm_bwd", grid=(nblk,),
                  out_shape=(jax.ShapeDtypeStruct((rows, D), BF16),
                             jax.ShapeDtypeStruct((2, N_Q, 16 * G_H, G_P), F32),
                             jax.ShapeDtypeStruct((2, N_Q, 16 * G_H, G_P), F32),
                             jax.ShapeDtypeStruct((1, N_STATE), F32), jax.ShapeDtypeStruct((1, D), F32), *recv),
                  in_specs=[rev(0), rev(2), pl.BlockSpec((tb, N_STATE), lambda i: (nblk - 1 - i, 0)),
                            pl.BlockSpec((None, 1, N_STATE), lambda i: (nblk - 1 - i, 0, 0)),
                            _full((tb, tb)), _full((tb, tb)),
                            _full((N_Q, 256, Q_W), single=True), _full((N_Q, 256, Q_W), single=True),
                            _full((k_steps, N_STATE)), _full((1, D))] + [ANY] * n_h,
                  out_specs=(rev(0), _full((2, N_Q, 16 * G_H, G_P)), _full((2, N_Q, 16 * G_H, G_P)),
                             _full((1, N_STATE)), _full((1, D)), *([ANY] * n_h)),
                  scratch_shapes=[pltpu.VMEM((tb, N_STATE), F32), pltpu.VMEM((1, N_STATE), F32),
                                  pltpu.VMEM((SUBLANES, N_STATE), F32),
                                  pltpu.VMEM((N_Q, 16 * G_H, 256), F32), pltpu.VMEM((N_Q, 16 * G_H, 256), F32),
                                  pltpu.VMEM((1, N_STATE), F32), pltpu.VMEM((1, D), F32),
                                  pltpu.VMEM((tb, D), BF16)] + _sem_scratch(items),
                  compiler_params=_params(("arbitrary",), vmem=60 * 1024 * 1024),
                  )(dyp, proj, states, carries, pm, pmt, wb, wct, ptab, dvec, *hosted)


def _pool_bwd(dyp, pooled, proj, pool_w, pool_scale):
    rows = dyp.shape[0]
    tb = _tb(rows, 512)
    nblk = rows // tb

    def body(dy_ref, pooled_ref, z_ref, pw_ref, ps_ref, dp_ref, dpw_ref, dps_ref, ahead_ref):
        i = pl.program_id(0)
        blk = nblk - 1 - i

        @pl.when(i == 0)
        def _():
            ahead_ref[...] = jnp.zeros_like(ahead_ref)
            dpw_ref[...] = jnp.zeros_like(dpw_ref)
            dps_ref[...] = jnp.zeros_like(dps_ref)

        inv_counts = _inv_counts(tb, blk * tb)
        silu_z, dsilu_z = _silu_parts(z_ref[...].astype(F32))
        dy = dy_ref[...].astype(F32)
        for g, w in enumerate(POOL_WINDOWS):
            cols = slice(g * 256, (g + 1) * 256)
            pooled_b = pooled_ref[:, cols]
            mixed = _dot(pooled_b, pw_ref[g])
            scale = ps_ref[:, cols]
            dp_ref[:, D + g * 256:D + (g + 1) * 256] = (dy[:, cols] * (mixed * scale) * dsilu_z[:, cols]).astype(BF16)
            dms = dy[:, cols] * silu_z[:, cols]
            dps_ref[:, cols] += jnp.sum(dms * mixed, axis=0, keepdims=True)
            dmixed = (dms * scale).astype(BF16)
            dpw_ref[g] += _dot_tn(pooled_b, dmixed)
            dpooled = _dot_nt(dmixed, pw_ref[g])
            ratio = dpooled * inv_counts[g]
            acc = jnp.concatenate([ratio, ahead_ref[:, cols]], axis=0)
            ahead_ref[:, cols] = ratio[:HALO, :]
            s = 1
            while s < w:
                acc = acc + pltpu.roll(acc, tb + HALO - s, axis=0)
                s *= 2
            dp_ref[:, cols] = (acc[:tb, :] - dpooled).astype(BF16)

    rev = lambda c: pl.BlockSpec((tb, D), lambda i: (nblk - 1 - i, c))
    return _pcall(body, name="pool_bwd", grid=(nblk,),
                  out_shape=(jax.ShapeDtypeStruct((rows, 2 * D), BF16), jax.ShapeDtypeStruct((4, 256, 256), F32),
                             jax.ShapeDtypeStruct((1, D), F32)),
                  in_specs=[rev(0), rev(0), rev(1), _full((4, 256, 256)), _full((1, D))],
                  out_specs=(pl.BlockSpec((tb, 2 * D), lambda i: (nblk - 1 - i, 0)), _full((4, 256, 256)),
                             _full((1, D))),
                  scratch_shapes=[pltpu.VMEM((HALO, D), F32)],
                  compiler_params=_params(("arbitrary",)))(dyp, pooled, proj, pool_w, pool_scale)


def _dproj_specs(tb):
    return [pl.BlockSpec((tb, 2 * D), lambda i: (i, 0)), pl.BlockSpec((tb, D), lambda i: (i, 0)),
            pl.BlockSpec((tb, D), lambda i: (i, 0)), pl.BlockSpec((tb, 2 * D), lambda i: (i, 0))]


def _in_proj_bwd_x(x, dy, dpp, dus, dzs, dpg, mod3, norm_pre, w_in, dw_in_ssm, recv_w_in):
    rows = x.shape[0]
    tb = _tb(rows, 512)
    nblk = rows // tb
    items = [_w_in_block_item(0, 0, j, ssm_part=True) for j in range(W_IN_SHARD // W_IN_BLOCK)]
    sums_item = [_Item(0, 0, _whole, _slot)]

    def body(x_ref, dy_ref, dpp_ref, dus_ref, dzs_ref, dpg_ref, mod_ref, np_ref, w_ref,
             dw_src, _, gx_ref, recv_w, recv_sums, vec_ref, ssem, rsem, lsem, *sums_sems):
        src_refs, recv_refs, sems = (dw_src,), (recv_w,), (ssem, rsem, lsem)

        @pl.when(pl.program_id(0) == 0)
        def _():
            _hosted_copies(items, src_refs, recv_refs, *sems, act="start")
            vec_ref[...] = jnp.zeros_like(vec_ref)

        dh = _dot_nt(dpp_ref[...], w_ref[:, 0:2 * D])
        dh += _dot_nt(dus_ref[...], w_ref[:, 2 * D:3 * D])
        dh += _dot_nt(dzs_ref[...], w_ref[:, 3 * D:4 * D])
        dh += _dot_nt(dpg_ref[...], w_ref[:, 4 * D:6 * D])
        xn, r, _ = _prenorm(x_ref[...], mod_ref[...], np_ref[...])
        one_scale = 1.0 + mod_ref[1:2, :]
        vec_ref[0:1, :] += jnp.sum(dh, axis=0, keepdims=True)
        vec_ref[1:2, :] += jnp.sum(dh * xn, axis=0, keepdims=True) * np_ref[...]
        vec_ref[2:3, :] += jnp.sum(dh * xn, axis=0, keepdims=True) * one_scale
        gx_ref[...] = dy_ref[...] + _rms_bwd(dh * (np_ref[...] * one_scale), xn, r)

        @pl.when(pl.program_id(0) == nblk - 1)
        def _():
            _hosted_copies(sums_item, (vec_ref,), (recv_sums,), *sums_sems, act="start")
            _hosted_copies(items, src_refs, recv_refs, *sems, act="wait")
            _hosted_copies(sums_item, (vec_ref,), (recv_sums,), *sums_sems, act="wait")

    row = pl.BlockSpec((tb, D), lambda i: (i, 0))
    recv = (jax.ShapeDtypeStruct(recv_w_in.shape, recv_w_in.dtype), jax.ShapeDtypeStruct((N_DEV, 3, D), F32))
    return _pcall(body, name="in_proj_bwd_x", grid=(nblk,),
                  out_shape=(jax.ShapeDtypeStruct((rows, D), F32), *recv),
                  in_specs=[row, row] + _dproj_specs(tb) + [_full((3, D)), _full((1, D)),
                                                            _full((D, N_IN), single=True)] + [ANY] * 2,
                  out_specs=(row, ANY, ANY),
                  input_output_aliases={10: 1},
                  scratch_shapes=[pltpu.VMEM((3, D), F32)] + _sem_scratch(items) + _sem_scratch(sums_item),
                  compiler_params=_params(("arbitrary",)))(x, dy, dpp, dus, dzs, dpg, mod3, norm_pre, w_in,
                                                           dw_in_ssm, recv_w_in)


def _in_proj_bwd_w(name, x, dparts, mod3, norm_pre, gathered=()):
    rows = x.shape[0]
    tb = _tb(rows, 512)
    nblk = rows // tb
    widths = [p.shape[1] for p in dparts]
    n_p, n_g = len(dparts), len(gathered)
    items = [_Item(t, t, _whole, _slot) for t in range(n_g)]

    def body(x_ref, *rest):
        part_refs, (mod_ref, np_ref) = rest[:n_p], rest[n_p:n_p + 2]
        src_refs, dw_ref = rest[n_p + 2:n_p + 2 + n_g], rest[n_p + 2 + n_g]
        recv_refs, (acc, *sems) = rest[n_p + 3 + n_g:n_p + 3 + 2 * n_g], rest[n_p + 3 + 2 * n_g:]
        i = pl.program_id(0)

        @pl.when(i == 0)
        def _():
            if n_g:
                _hosted_copies(items, src_refs, recv_refs, *sems, act="start")
            acc[...] = jnp.zeros_like(acc)

        _, _, h = _prenorm(x_ref[...], mod_ref[...], np_ref[...])
        ht = h.astype(BF16)
        lo = 0
        for ref, w in zip(part_refs, widths):
            acc[:, lo:lo + w] += _dot_tn(ht, ref[...])
            lo += w

        @pl.when(i == nblk - 1)
        def _():
            dw_ref[...] = acc[...].astype(BF16)
            if n_g:
                _hosted_copies(items, src_refs, recv_refs, *sems, act="wait")

    row = pl.BlockSpec((tb, D), lambda i: (i, 0))
    out = _pcall(body, name=name, grid=(nblk,),
                 out_shape=(jax.ShapeDtypeStruct((D, sum(widths)), BF16),
                            *[jax.ShapeDtypeStruct((N_DEV,) + g.shape, g.dtype) for g in gathered]),
                 in_specs=[row] + [pl.BlockSpec((tb, w), lambda i: (i, 0)) for w in widths] +
                          [_full((3, D)), _full((1, D))] + [ANY] * n_g,
                 out_specs=(_full((D, sum(widths))), *([ANY] * n_g)),
                 scratch_shapes=[pltpu.VMEM((D, sum(widths)), F32)] + (_sem_scratch(items) if n_g else []),
                 compiler_params=_params(("arbitrary",)))(x, *dparts, mod3, norm_pre, *gathered)
    return out if n_g else out[0]


def _adamw_math(w, g, m, v):
    m = ADAM_B1 * m + (1.0 - ADAM_B1) * g
    v = ADAM_B2 * v + (1.0 - ADAM_B2) * (g * g)
    m_hat = m / (1.0 - ADAM_B1 ** ADAM_STEP)
    v_hat = v / (1.0 - ADAM_B2 ** ADAM_STEP)
    delta = -ADAM_LR * (m_hat / (jnp.sqrt(v_hat) + ADAM_EPS) + ADAM_WD * w)
    return delta, m, v


def _sum_sources(ref):
    g = ref[0].astype(F32)
    for s in range(1, N_DEV):
        g = g + ref[s].astype(F32)
    return g


def _adamw_reduce(name, parts, w, m, v):
    r, c = w.shape
    tr = r if r * c <= 256 * 1024 else max(8, (256 * 1024 // c) // 8 * 8)
    while r % tr:
        tr -= 8

    def body(p_ref, w_ref, m_ref, v_ref, g_ref, d_ref, nm_ref, nv_ref):
        g = _sum_sources(p_ref)
        g_ref[...] = g
        d_ref[...], nm_ref[...], nv_ref[...] = _adamw_math(w_ref[...], g, m_ref[...], v_ref[...])

    blk = pl.BlockSpec((tr, c), lambda i: (i, 0))
    return _pcall(body, name=name, grid=(r // tr,),
                  out_shape=tuple([jax.ShapeDtypeStruct((r, c), F32)] * 4),
                  in_specs=[pl.BlockSpec((N_DEV, tr, c), lambda i: (0, i, 0)), blk, blk, blk],
                  out_specs=(blk, blk, blk, blk),
                  compiler_params=_params(("arbitrary",)))(parts, w, m, v)


def _adamw_small(gs, ws, ms, vs):
    n = len(gs)

    def body(*refs):
        ins, outs = refs[:4 * n], refs[4 * n:]
        for t in range(n):
            g_ref, w_ref, m_ref, v_ref = ins[4 * t:4 * t + 4]
            outs[3 * t][...], outs[3 * t + 1][...], outs[3 * t + 2][...] = _adamw_math(
                w_ref[...], g_ref[...], m_ref[...], v_ref[...])

    vm = pl.BlockSpec(memory_space=pltpu.VMEM)
    flat = [a for t in range(n) for a in (gs[t], ws[t], ms[t], vs[t])]
    return _pcall(body, name="adamw_small",
                  out_shape=tuple(jax.ShapeDtypeStruct(w.shape, F32) for w in ws for _ in range(3)),
                  in_specs=[vm] * (4 * n), out_specs=tuple([vm] * (3 * n)), compiler_params=_params())(*flat)


def _sum_small(parts):
    n = len(parts)

    def body(*refs):
        for t in range(n):
            refs[n + t][...] = _sum_sources(refs[t])

    vm = pl.BlockSpec(memory_space=pltpu.VMEM)
    return _pcall(body, name="sum_small",
                  out_shape=tuple(jax.ShapeDtypeStruct(p.shape[1:], F32) for p in parts),
                  in_specs=[vm] * n, out_specs=tuple([vm] * n), compiler_params=_params())(*parts)


def _ada_update(c_all, dmod_cols, w, m, v):
    def body(c_ref, dm_ref, w_ref, m_ref, v_ref, g_ref, d_ref, nm_ref, nv_ref):
        ca = c_ref[...]
        g = lax.dot_general(ca * jax.nn.sigmoid(ca), dm_ref[...], (((0,), (0,)), ((), ())),
                            preferred_element_type=F32, precision=lax.Precision.HIGHEST)
        g_ref[...] = g
        d_ref[...], nm_ref[...], nv_ref[...] = _adamw_math(w_ref[...], g, m_ref[...], v_ref[...])

    vm = pl.BlockSpec(memory_space=pltpu.VMEM)
    return _pcall(body, name="ada_update", out_shape=tuple([jax.ShapeDtypeStruct(w.shape, F32)] * 4),
                  in_specs=[vm] * 5, out_specs=(vm, vm, vm, vm), compiler_params=_params())(c_all, dmod_cols, w, m, v)


def kernel(x, c, w_ada, b_ada, norm_pre, norm_post, w_in, pool_w, pool_scale, ssm_a_re, ssm_a_im, ssm_log_dt, ssm_b_re, ssm_b_im, ssm_c_re, ssm_c_im, ssm_d, glu_w, glu_b, w_branch_pool, w_branch_ssm, w_out, loss_target, m_w_ada, m_b_ada, m_norm_pre, m_norm_post, m_w_in, m_pool_w, m_pool_scale, m_ssm_a_re, m_ssm_a_im, m_ssm_log_dt, m_ssm_b_re, m_ssm_b_im, m_ssm_c_re, m_ssm_c_im, m_ssm_d, m_glu_w, m_glu_b, m_w_branch_pool, m_w_branch_ssm, m_w_out, v_w_ada, v_b_ada, v_norm_pre, v_norm_post, v_w_in, v_pool_w, v_pool_scale, v_ssm_a_re, v_ssm_a_im, v_ssm_log_dt, v_ssm_b_re, v_ssm_b_im, v_ssm_c_re, v_ssm_c_im, v_ssm_d, v_glu_w, v_glu_b, v_w_branch_pool, v_w_branch_ssm, v_w_out):
    given = dict(locals())
    me = _flat(_me())
    rows = x.shape[1]
    x2 = x[0]
    target = loss_target[0]
    ada_cols = w_ada.shape[2]

    b_ada_s = lax.dynamic_slice(b_ada, (0, me * ada_cols), (1, ada_cols))
    c_all, mod_rows = _ada_exchange(c, w_ada[0], b_ada_s)
    mod3 = mod_rows.reshape(3, D)

    shards = _cast_shards([w_in[0], pool_w[0], glu_w[0], w_branch_pool[0], w_branch_ssm[0], w_out[0]])

    tb_ssm = _tb(rows, 256)
    k_steps = tb_ssm // SUBLANES
    a_re, a_im = ssm_a_re[0], ssm_a_im[0]
    log_dt = ssm_log_dt[0].reshape(GROUPS, 1)
    b_re_t, b_im_t = ssm_b_re[0].transpose(0, 2, 1), ssm_b_im[0].transpose(0, 2, 1)
    wb, wct, pow_re, pow_im = _s5_prep(a_re, a_im, log_dt, b_re_t, b_im_t, ssm_c_re[0], ssm_c_im[0], k_steps)
    ptab = _state_layout(pow_re, pow_im)
    dvec = ssm_d[0].reshape(1, D)
    pm = _perm_matrix(tb_ssm)
    pmt = pm.T

    proj, w_in_g, pool_w_g, glu_g = _in_proj(x2, mod3, norm_pre, shards[0], shards[1:3])
    y_pool, pooled = _pool_fwd(proj, pool_w_g, pool_scale)
    y_ssm, ys_pre, carries, states, wbp_g, wbs_g, wout_g = _ssm_fwd(
        proj, pm, pmt, wb, wct, ptab, dvec, glu_g, glu_b, shards[3:])
    loss_part, dy, dyp, dys, dpg, dwbp, dwbs, dwout, head_vec = _head(
        x2, target, proj, y_pool, y_ssm, mod3, norm_post, wbp_g, wbs_g, wout_g)

    dpp, dpool_w, dpool_scale = _pool_bwd(dyp, pooled, proj, pool_w_g, pool_scale)
    dw_in_rest = _in_proj_bwd_w("in_proj_bwd_w_rest", x2, [dpp, dpg], mod3, norm_pre)
    dy_pre, dzs, dglu_w, dglu_b = _glu_bwd(dys, proj, ys_pre, pm, pmt, glu_g, glu_b)
    dus, dbb, dcc, dabar, dd, p_glu, p_wbp, p_wbs, p_wout, p_pool_w, p_w_in = _ssm_bwd(
        dy_pre, proj, states, carries, pm, pmt, wb, wct, ptab, dvec, [dglu_w, dwbp, dwbs, dwout], dpool_w, dw_in_rest)

    small32 = jnp.concatenate([head_vec, dpool_scale, dglu_b, dd, jnp.broadcast_to(loss_part, (1, D)),
                               jnp.zeros((2, D), F32), dabar.reshape(8, D)], axis=0)
    small16 = jnp.concatenate([dbb.reshape(2 * GROUPS, D), dcc.reshape(2 * GROUPS, D)], axis=0).astype(BF16)
    dw_in_ssm, p_small32, p_small16 = _in_proj_bwd_w("in_proj_bwd_w_ssm", x2, [dus, dzs], mod3, norm_pre,
                                                     gathered=(small32, small16))
    grad_x, p_w_in, p_pre = _in_proj_bwd_x(x2, dy, dpp, dus, dzs, dpg, mod3, norm_pre, w_in_g, dw_in_ssm, p_w_in)

    tot32, tot16, tot_pre = _sum_small([p_small32, p_small16, p_pre])
    d_abar_re, d_abar_im = _state_unlayout(tot32[8:16].reshape(N_STATE))
    d_bb_re, d_bb_im = tot16[0:64].reshape(GROUPS, G_H, G_P), tot16[64:128].reshape(GROUPS, G_H, G_P)
    g_a_re, g_a_im, g_log_dt, g_b_re_t, g_b_im_t = _s5_prep_bwd(
        a_re, a_im, log_dt, b_re_t, b_im_t, d_abar_re, d_abar_im, d_bb_re, d_bb_im)

    grads, deltas, new_m, new_v = {}, {}, {}, {}

    small = []

    def small_update(name, g2):
        small.append((name, g2))

    def shard_update(name, parts):
        shape = given[name].shape
        r2 = parts.shape[1:] if parts.ndim == 3 else (parts.shape[1] * parts.shape[2], parts.shape[3])
        w2, m2, v2 = (given[p + name].reshape(r2) for p in ("", "m_", "v_"))
        out = _adamw_reduce("adamw_" + name, parts.reshape((N_DEV,) + tuple(r2)), w2, m2, v2)
        grads[name], deltas[name], new_m[name], new_v[name] = (a.reshape(shape) for a in out)

    dmod_all = jnp.concatenate([p_pre[:, 0:2, :], p_small32[:, 0:1, :]], axis=1).reshape(N_DEV, 3 * D)
    dmod_cols = lax.dynamic_slice(dmod_all, (0, me * ada_cols), (N_DEV, ada_cols))
    out = _ada_update(c_all, dmod_cols, w_ada[0], m_w_ada[0], v_w_ada[0])
    grads['w_ada'], deltas['w_ada'], new_m['w_ada'], new_v['w_ada'] = (a.reshape(w_ada.shape) for a in out)

    small_update('b_ada', jnp.concatenate([tot_pre[0:2], tot32[0:1]], axis=0).reshape(1, 3 * D))
    small_update('norm_pre', tot_pre[2:3])
    small_update('norm_post', tot32[1:2])
    small_update('pool_scale', tot32[2:3])
    small_update('glu_b', tot32[3:4])
    small_update('ssm_d', tot32[4:5])
    small_update('ssm_a_re', g_a_re)
    small_update('ssm_a_im', g_a_im)
    small_update('ssm_log_dt', g_log_dt.reshape(1, GROUPS))
    small_update('ssm_b_re', g_b_re_t.transpose(0, 2, 1).reshape(GROUPS, G_P * G_H))
    small_update('ssm_b_im', g_b_im_t.transpose(0, 2, 1).reshape(GROUPS, G_P * G_H))
    small_update('ssm_c_re', tot16[128:192])
    small_update('ssm_c_im', -tot16[192:256])
    flat = _adamw_small([g2 for _, g2 in small],
                        *[[given[p + name].reshape(g2.shape) for name, g2 in small] for p in ("", "m_", "v_")])
    for t, (name, g2) in enumerate(small):
        shape = given[name].shape
        grads[name], deltas[name], new_m[name], new_v[name] = (
            a.reshape(shape) for a in (g2, *flat[3 * t:3 * t + 3]))
    shard_update('w_in', p_w_in)
    shard_update('pool_w', p_pool_w)
    shard_update('glu_w', p_glu)
    shard_update('w_branch_pool', p_wbp)
    shard_update('w_branch_ssm', p_wbs)
    shard_update('w_out', p_wout)

    return (tot32[5, 0], grad_x[None], *[grads[n] for n in WEIGHTS], *[deltas[n] for n in WEIGHTS],
            *[new_m[n] for n in WEIGHTS], *[new_v[n] for n in WEIGHTS])
```

```python
import functools
import math
from typing import Callable, NamedTuple, Optional

import jax
import jax.numpy as jnp
from jax import lax
from jax.experimental import pallas as pl
from jax.experimental.pallas import tpu as pltpu

F32 = jnp.float32
BF16 = jnp.bfloat16
MESH = pl.DeviceIdType.MESH

D = 1024
N_DEV = 8
N_IN = 6 * D
GROUPS = 64
G_H = 16
G_P = 64
N_Q = 4
Q_W = 2 * 16 * G_P
N_STATE = N_Q * Q_W
POOL_WINDOWS = (2, 4, 8, 16)
HALO = 16
RMS_EPS = 1e-6
SUBLANES = 8
LANE_CHUNK = 512
SCAN_UNROLL = 2
HEAD_PARTS = 1
VMEM_LIMIT = 56 * 1024 * 1024

ADAM_LR = 0.001
ADAM_B1 = 0.9
ADAM_B2 = 0.999
ADAM_EPS = 1e-08
ADAM_WD = 0.01
ADAM_STEP = 10

WEIGHTS = ['w_ada', 'b_ada', 'norm_pre', 'norm_post', 'w_in', 'pool_w', 'pool_scale', 'ssm_a_re',
           'ssm_a_im', 'ssm_log_dt', 'ssm_b_re', 'ssm_b_im', 'ssm_c_re', 'ssm_c_im', 'ssm_d', 'glu_w',
           'glu_b', 'w_branch_pool', 'w_branch_ssm', 'w_out']


def _pcall(body, **kw):
    return pl.pallas_call(body, **kw)


def _params(sem=None, vmem=VMEM_LIMIT):
    return pltpu.CompilerParams(dimension_semantics=sem, vmem_limit_bytes=vmem)


def _tb(rows, pref):
    return pref if rows % pref == 0 and rows // pref >= 2 else rows // 2


def _full(shape, single=False):
    nd = len(shape)
    if single:
        return pl.BlockSpec(shape, lambda i: (0,) * nd, pipeline_mode=pl.Buffered(1))
    return pl.BlockSpec(shape, lambda i: (0,) * nd)


ANY = pl.BlockSpec(memory_space=pl.ANY)


def _me():
    return lax.axis_index("x"), lax.axis_index("y"), lax.axis_index("c")


def _flat(p):
    return 4 * p[0] + 2 * p[1] + p[2]


def _peer(k):
    x, y, c = _me()
    return (1 - x if k & 4 else x, 1 - y if k & 2 else y, 1 - c if k & 1 else c)


def _silu_parts(z):
    s = jax.nn.sigmoid(z)
    return z * s, s * (1.0 + z * (1.0 - s))


_GELU_C = math.sqrt(2.0 / math.pi)


def _gelu_parts(x):
    x2 = x * x
    t = jnp.tanh(_GELU_C * (x + 0.044715 * x * x2))
    g = 0.5 * x * (1.0 + t)
    dg = 0.5 * (1.0 + t) + 0.5 * x * (1.0 - t * t) * (_GELU_C * (1.0 + 3.0 * 0.044715 * x2))
    return g, dg


def _dot(a, b):
    return jnp.dot(a, b, preferred_element_type=F32)


def _dot_nt(a, b):
    return lax.dot_general(a, b, (((1,), (1,)), ((), ())), preferred_element_type=F32)


def _dot_tn(a, b):
    return lax.dot_general(a, b, (((0,), (0,)), ((), ())), preferred_element_type=F32)


def _rms_parts(x):
    r = lax.rsqrt(jnp.mean(x * x, axis=-1, keepdims=True) + RMS_EPS)
    return x * r, r


def _rms_bwd(dxn, xn, r):
    return r * (dxn - xn * jnp.mean(dxn * xn, axis=-1, keepdims=True))


def _ada_exchange(c, w_ada_s, b_ada_s):
    cols = w_ada_s.shape[1]

    def body(c_ref, w_ref, b_ref, call_ref, mod_ref, part_ref, ssem, rsem, lsem):
        me3 = _me()
        me = _flat(me3)
        mine = pltpu.make_async_copy(c_ref, call_ref.at[pl.ds(me, 1), :], lsem.at[0])
        mine.start()
        sends = []
        for k in range(1, N_DEV):
            cp = pltpu.make_async_remote_copy(src_ref=c_ref, dst_ref=call_ref.at[pl.ds(me, 1), :],
                                              send_sem=ssem.at[k - 1], recv_sem=rsem.at[k - 1],
                                              device_id=_peer(k), device_id_type=MESH)
            cp.start()
            sends.append(cp)
        mine.wait()
        for k in range(1, N_DEV):
            p = _flat(_peer(k))
            pltpu.make_async_remote_copy(src_ref=c_ref, dst_ref=call_ref.at[pl.ds(p, 1), :],
                                         send_sem=ssem.at[k - 1], recv_sem=rsem.at[k - 1],
                                         device_id=_peer(k), device_id_type=MESH).wait_recv()
        for cp in sends:
            cp.wait_send()
        ca = call_ref[...]
        act = ca * jax.nn.sigmoid(ca)
        part_ref[...] = jnp.dot(act, w_ref[...], preferred_element_type=F32,
                                precision=lax.Precision.HIGHEST) + b_ref[...]
        own = pltpu.make_async_copy(part_ref.at[pl.ds(me, 1), :], mod_ref.at[pl.ds(me, 1), :], lsem.at[1])
        own.start()
        sends = []
        for k in range(1, N_DEV):
            p = _flat(_peer(k))
            s = N_DEV - 1 + k - 1
            cp = pltpu.make_async_remote_copy(src_ref=part_ref.at[pl.ds(p, 1), :],
                                              dst_ref=mod_ref.at[pl.ds(me, 1), :],
                                              send_sem=ssem.at[s], recv_sem=rsem.at[s],
                                              device_id=_peer(k), device_id_type=MESH)
            cp.start()
            sends.append(cp)
        own.wait()
        for k in range(1, N_DEV):
            p = _flat(_peer(k))
            s = N_DEV - 1 + k - 1
            pltpu.make_async_remote_copy(src_ref=part_ref.at[pl.ds(p, 1), :],
                                         dst_ref=mod_ref.at[pl.ds(p, 1), :],
                                         send_sem=ssem.at[s], recv_sem=rsem.at[s],
                                         device_id=_peer(k), device_id_type=MESH).wait_recv()
        for cp in sends:
            cp.wait_send()

    vm = pl.BlockSpec(memory_space=pltpu.VMEM)
    return _pcall(
        body, name="ada_exchange",
        out_shape=(jax.ShapeDtypeStruct((N_DEV, D), F32), jax.ShapeDtypeStruct((N_DEV, cols), F32)),
        in_specs=[vm, vm, vm], out_specs=(vm, vm),
        scratch_shapes=[pltpu.VMEM((N_DEV, cols), F32),
                        pltpu.SemaphoreType.DMA((2 * (N_DEV - 1),)),
                        pltpu.SemaphoreType.DMA((2 * (N_DEV - 1),)),
                        pltpu.SemaphoreType.DMA((2,))],
    )(c, w_ada_s, b_ada_s)


class _Item(NamedTuple):
    src: int
    out: int
    src_view: Callable
    dst_view: Callable
    pred: Optional[Callable] = None


def _when(pred, dest, fn):
    if pred is None:
        fn()
    else:
        pl.when(pred(dest))(fn)


def _n_sems(items):
    return len(items) * (N_DEV - 1)


def _hosted_copies(items, srcs, outs, ssem, rsem, lsem, act):
    me = _flat(_me())
    for t, it in enumerate(items):
        local = lambda t=t, it=it: pltpu.make_async_copy(
            it.src_view(srcs[it.src], me), it.dst_view(outs[it.out], me), lsem.at[t])
        if act == "start":
            _when(it.pred, me, lambda local=local: local().start())
        else:
            _when(it.pred, me, lambda local=local: local().wait())
    for k in range(1, N_DEV):
        p3 = _peer(k)
        p = _flat(p3)
        for t, it in enumerate(items):
            s = t * (N_DEV - 1) + k - 1
            send = lambda it=it, s=s, p=p, p3=p3: pltpu.make_async_remote_copy(
                src_ref=it.src_view(srcs[it.src], p), dst_ref=it.dst_view(outs[it.out], me),
                send_sem=ssem.at[s], recv_sem=rsem.at[s], device_id=p3, device_id_type=MESH)
            recv = lambda it=it, s=s, p=p, p3=p3: pltpu.make_async_remote_copy(
                src_ref=it.src_view(srcs[it.src], p), dst_ref=it.dst_view(outs[it.out], p),
                send_sem=ssem.at[s], recv_sem=rsem.at[s], device_id=p3, device_id_type=MESH)
            if act == "start":
                _when(it.pred, p, lambda send=send: send().start())
            else:
                _when(it.pred, me, lambda recv=recv: recv().wait_recv())
                _when(it.pred, p, lambda send=send: send().wait_send())


def _sem_scratch(items):
    return [pltpu.SemaphoreType.DMA((_n_sems(items),)), pltpu.SemaphoreType.DMA((_n_sems(items),)),
            pltpu.SemaphoreType.DMA((len(items),))]


def _exchange(name, srcs, out_structs, items):
    n_src, n_out = len(srcs), len(out_structs)

    def body(*refs):
        src_refs, out_refs = refs[:n_src], refs[n_src:n_src + n_out]
        sems = refs[n_src + n_out:]
        _hosted_copies(items, src_refs, out_refs, *sems, act="start")
        _hosted_copies(items, src_refs, out_refs, *sems, act="wait")

    return _pcall(body, name=name, out_shape=tuple(out_structs),
                  in_specs=[ANY] * n_src, out_specs=tuple([ANY] * n_out),
                  scratch_shapes=_sem_scratch(items))(*srcs)


def _whole(ref, dest):
    return ref


def _slot(ref, sender):
    return ref.at[sender]


def _rows_of(rows):
    return lambda ref, dev: ref.at[pl.ds(dev * rows, rows), :]


def _cols_of(cols):
    return lambda ref, dev: ref.at[:, pl.ds(dev * cols, cols)]


def _pool_rows_of(rows):
    return lambda ref, dev: ref.at[:, pl.ds(dev * rows, rows), :]


def _gather_item(src, out, dst_view):
    return _Item(src, out, _whole, dst_view)


def _scatter_item(src, out, src_view):
    return _Item(src, out, src_view, _slot)


W_IN_BLOCK = 256
W_IN_SHARD = N_IN // N_DEV
SSM_BLOCKS = (2 * D // W_IN_BLOCK, 4 * D // W_IN_BLOCK)


def _w_in_block_item(src, out, j, ssm_part):
    def block(dest):
        return (W_IN_SHARD // W_IN_BLOCK) * dest + j

    def in_ssm(dest):
        b = block(dest)
        return (b >= SSM_BLOCKS[0]) & (b < SSM_BLOCKS[1])

    def src_view(ref, dest):
        b = block(dest)
        local = b - SSM_BLOCKS[0] if ssm_part else jnp.where(b < SSM_BLOCKS[0], b, b - (SSM_BLOCKS[1] - SSM_BLOCKS[0]))
        local = jnp.clip(local, 0, ref.shape[1] // W_IN_BLOCK - 1)
        return ref.at[:, pl.ds(local * W_IN_BLOCK, W_IN_BLOCK)]

    def dst_view(ref, sender):
        return ref.at[sender, :, pl.ds(j * W_IN_BLOCK, W_IN_BLOCK)]

    pred = in_ssm if ssm_part else (lambda dest: jnp.logical_not(in_ssm(dest)))
    return _Item(src, out, src_view, dst_view, pred)


def _cast_shards(arrs):
    def body(*refs):
        n = len(refs) // 2
        for i in range(n):
            refs[n + i][...] = refs[i][...].astype(BF16)

    vm = pl.BlockSpec(memory_space=pltpu.VMEM)
    return _pcall(body, name="cast_shards",
                  out_shape=tuple(jax.ShapeDtypeStruct(a.shape, BF16) for a in arrs),
                  in_specs=[vm] * len(arrs), out_specs=tuple([vm] * len(arrs)),
                  compiler_params=_params())(*arrs)


def _s5_discretise(a_re, a_im, log_dt, b_re_t, b_im_t):
    dt = jnp.exp(log_dt)
    lam_re = jnp.minimum(a_re, -1e-4)
    lam_im = a_im
    mag = jnp.exp(lam_re * dt)
    abar_re = mag * jnp.cos(lam_im * dt)
    abar_im = mag * jnp.sin(lam_im * dt)
    den = lam_re * lam_re + lam_im * lam_im
    num_re = abar_re - 1.0
    f_re = (num_re * lam_re + abar_im * lam_im) / den
    f_im = (abar_im * lam_re - num_re * lam_im) / den
    f_re, f_im = f_re[:, None, :], f_im[:, None, :]
    bb_re = f_re * b_re_t - f_im * b_im_t
    bb_im = f_re * b_im_t + f_im * b_re_t
    return abar_re, abar_im, bb_re, bb_im


def _group_masks():
    spread = lax.broadcasted_iota(jnp.int32, (G_P, 16 * G_P), 1) % G_P == lax.broadcasted_iota(
        jnp.int32, (G_P, 16 * G_P), 0)
    own = lax.broadcasted_iota(jnp.int32, (16 * G_H, 16 * G_P), 0) // G_H == lax.broadcasted_iota(
        jnp.int32, (16 * G_H, 16 * G_P), 1) // G_P
    return spread, own


def _s5_prep(a_re, a_im, log_dt, b_re_t, b_im_t, c_re, c_im, n_pow):
    def body(ar_ref, ai_ref, ld_ref, br_ref, bi_ref, cr_ref, ci_ref, wb_ref, wct_ref, pr_ref, pi_ref):
        abar_re, abar_im, bb_re, bb_im = _s5_discretise(ar_ref[...], ai_ref[...], ld_ref[...], br_ref[...], bi_ref[...])
        spread, own = _group_masks()
        spread = spread.astype(BF16)
        for ref, parts in ((wb_ref, (bb_re, bb_im)), (wct_ref, (cr_ref[...], -ci_ref[...]))):
            for half, t in enumerate(parts):
                for q in range(N_Q):
                    blocks = t[q * 16:(q + 1) * 16].reshape(16 * G_H, G_P).astype(BF16)
                    dense = jnp.where(own, _dot(blocks, spread), 0.0)
                    ref[q, :, half * (Q_W // 2):(half + 1) * (Q_W // 2)] = dense.astype(BF16)
        p_re, p_im = abar_re, abar_im
        pr_ref[0] = p_re
        pi_ref[0] = p_im
        for k in range(1, n_pow):
            p_re, p_im = p_re * abar_re - p_im * abar_im, p_re * abar_im + p_im * abar_re
            pr_ref[k] = p_re
            pi_ref[k] = p_im

    vm = pl.BlockSpec(memory_space=pltpu.VMEM)
    return _pcall(body, name="s5_prep",
                  out_shape=(jax.ShapeDtypeStruct((N_Q, 16 * G_H, Q_W), BF16),
                             jax.ShapeDtypeStruct((N_Q, 16 * G_H, Q_W), BF16),
                             jax.ShapeDtypeStruct((n_pow, GROUPS, G_P), F32),
                             jax.ShapeDtypeStruct((n_pow, GROUPS, G_P), F32)),
                  in_specs=[vm] * 7, out_specs=(vm, vm, vm, vm), compiler_params=_params(),
                  )(a_re, a_im, log_dt, b_re_t, b_im_t, c_re, c_im)


def _s5_prep_bwd(a_re, a_im, log_dt, b_re_t, b_im_t, d_abar_re, d_abar_im, d_bb_re, d_bb_im):
    def body(ar_ref, ai_ref, ld_ref, br_ref, bi_ref, dar_ref, dai_ref, dbr_ref, dbi_ref,
             gar_ref, gai_ref, gld_ref, gbr_ref, gbi_ref):
        _, vjp = jax.vjp(_s5_discretise, ar_ref[...], ai_ref[...], ld_ref[...], br_ref[...], bi_ref[...])
        g = vjp((dar_ref[...], dai_ref[...], dbr_ref[...], dbi_ref[...]))
        gar_ref[...] = g[0]
        gai_ref[...] = g[1]
        gld_ref[...] = g[2]
        gbr_ref[...] = g[3]
        gbi_ref[...] = g[4]

    vm = pl.BlockSpec(memory_space=pltpu.VMEM)
    ins = (a_re, a_im, log_dt, b_re_t, b_im_t)
    return _pcall(body, name="s5_prep_bwd",
                  out_shape=tuple(jax.ShapeDtypeStruct(a.shape, F32) for a in ins),
                  in_specs=[vm] * 9, out_specs=tuple([vm] * 5), compiler_params=_params(),
                  )(*ins, d_abar_re, d_abar_im, d_bb_re, d_bb_im)


def _state_layout(re, im):
    lead = re.shape[:-2]
    r = re.reshape(lead + (N_Q, 1, 16 * G_P))
    i = im.reshape(lead + (N_Q, 1, 16 * G_P))
    return jnp.concatenate([r, i], axis=-2).reshape(lead + (N_STATE,))


def _state_unlayout(v):
    v4 = v.reshape(N_Q, 2, 16, G_P)
    return v4[:, 0].reshape(GROUPS, G_P), v4[:, 1].reshape(GROUPS, G_P)


def _perm_matrix(tb):
    k_steps = tb // SUBLANES
    r = jnp.arange(tb)
    src = (r % SUBLANES) * k_steps + r // SUBLANES
    return (src[:, None] == jnp.arange(tb)[None, :]).astype(BF16)


def _lane_chunks(q):
    for lc in range(Q_W // 2 // LANE_CHUNK):
        re = q * Q_W + lc * LANE_CHUNK
        yield re, re + Q_W // 2


def _steps(lo, hi, body, init):
    if hi - lo <= SCAN_UNROLL:
        for k in range(lo, hi):
            init = body(k, init)
        return init
    trips = (hi - lo) // SCAN_UNROLL

    def trip(j, carry):
        for u in range(SCAN_UNROLL):
            carry = body(lo + j * SCAN_UNROLL + u, carry)
        return carry

    carry = lax.fori_loop(0, trips, trip, init)
    for k in range(lo + trips * SCAN_UNROLL, hi):
        carry = body(k, carry)
    return carry


def _tile(k):
    if isinstance(k, int):
        return pl.ds(k * SUBLANES, SUBLANES)
    return pl.ds(pl.multiple_of(k * SUBLANES, SUBLANES), SUBLANES)


def _scan_forward(q, s_ref, p_ref, carry_ref, enter_ref, fin_ref, k_steps):
    for re, im in _lane_chunks(q):
        lr, li = pl.ds(re, LANE_CHUNK), pl.ds(im, LANE_CHUNK)
        a_re = jnp.broadcast_to(p_ref[0:1, lr], (SUBLANES, LANE_CHUNK))
        a_im = jnp.broadcast_to(p_ref[0:1, li], (SUBLANES, LANE_CHUNK))

        def local(k, st):
            sr, si = st
            rows = _tile(k)
            nr = a_re * sr - a_im * si + s_ref[rows, lr]
            ni = a_re * si + a_im * sr + s_ref[rows, li]
            s_ref[rows, lr] = nr
            s_ref[rows, li] = ni
            return nr, ni

        zero = jnp.zeros((SUBLANES, LANE_CHUNK), F32)
        fr, fi = _steps(0, k_steps, local, (zero, zero))
        fin_ref[:, lr] = fr
        fin_ref[:, li] = fi
        ak_re, ak_im = p_ref[k_steps - 1:k_steps, lr], p_ref[k_steps - 1:k_steps, li]
        c_re, c_im = carry_ref[:, lr], carry_ref[:, li]
        for seg in range(SUBLANES):
            enter_ref[seg:seg + 1, lr] = c_re
            enter_ref[seg:seg + 1, li] = c_im
            f_re, f_im = fin_ref[seg:seg + 1, lr], fin_ref[seg:seg + 1, li]
            c_re, c_im = f_re + ak_re * c_re - ak_im * c_im, f_im + ak_re * c_im + ak_im * c_re
        carry_ref[:, lr] = c_re
        carry_ref[:, li] = c_im
        e_re, e_im = enter_ref[:, lr], enter_ref[:, li]

        def fix(k, _):
            rows = _tile(k)
            p_re = p_ref[pl.ds(k, 1), lr]
            p_im = p_ref[pl.ds(k, 1), li]
            s_ref[rows, lr] = s_ref[rows, lr] + (p_re * e_re - p_im * e_im)
            s_ref[rows, li] = s_ref[rows, li] + (p_re * e_im + p_im * e_re)
            return 0

        _steps(0, k_steps, fix, 0)


def _scan_backward(q, g_ref, s_ref, p_ref, carry_ref, s_in_ref, fin_ref, da_ref, k_steps):
    seg_id = lax.broadcasted_iota(jnp.int32, (SUBLANES, LANE_CHUNK), 0)
    for re, im in _lane_chunks(q):
        lr, li = pl.ds(re, LANE_CHUNK), pl.ds(im, LANE_CHUNK)
        a_re = jnp.broadcast_to(p_ref[0:1, lr], (SUBLANES, LANE_CHUNK))
        a_im = jnp.broadcast_to(p_ref[0:1, li], (SUBLANES, LANE_CHUNK))

        def local(j, st):
            sr, si = st
            rows = _tile(k_steps - 1 - j)
            nr = a_re * sr + a_im * si + g_ref[rows, lr]
            ni = a_re * si - a_im * sr + g_ref[rows, li]
            g_ref[rows, lr] = nr
            g_ref[rows, li] = ni
            return nr, ni

        zero = jnp.zeros((SUBLANES, LANE_CHUNK), F32)
        fr, fi = _steps(0, k_steps, local, (zero, zero))
        fin_ref[:, lr] = fr
        fin_ref[:, li] = fi
        ak_re, ak_im = p_ref[k_steps - 1:k_steps, lr], p_ref[k_steps - 1:k_steps, li]
        c_re, c_im = carry_ref[:, lr], carry_ref[:, li]
        lam_in = [None] * SUBLANES
        for seg in reversed(range(SUBLANES)):
            lam_in[seg] = (c_re, c_im)
            f_re, f_im = fin_ref[seg:seg + 1, lr], fin_ref[seg:seg + 1, li]
            c_re, c_im = f_re + ak_re * c_re + ak_im * c_im, f_im + ak_re * c_im - ak_im * c_re
        carry_ref[:, lr] = c_re
        carry_ref[:, li] = c_im
        for seg in range(SUBLANES):
            fin_ref[seg:seg + 1, lr] = lam_in[seg][0]
            fin_ref[seg:seg + 1, li] = lam_in[seg][1]
        e_re, e_im = fin_ref[:, lr], fin_ref[:, li]

        def fix_with(k, acc, sp_re, sp_im):
            acc_re, acc_im = acc
            rows = _tile(k)
            p_re = p_ref[pl.ds(k_steps - 1 - k, 1), lr]
            p_im = p_ref[pl.ds(k_steps - 1 - k, 1), li]
            l_re = g_ref[rows, lr] + (p_re * e_re + p_im * e_im)
            l_im = g_ref[rows, li] + (p_re * e_im - p_im * e_re)
            g_ref[rows, lr] = l_re
            g_ref[rows, li] = l_im
            return acc_re + (l_re * sp_re + l_im * sp_im), acc_im + (l_im * sp_re - l_re * sp_im)

        def fix(k, acc):
            prev = _tile(k - 1)
            return fix_with(k, acc, s_ref[prev, lr], s_ref[prev, li])

        last = _tile(k_steps - 1)
        before_re = jnp.where(seg_id == 0, s_in_ref[:, lr], pltpu.roll(s_ref[last, lr], 1, axis=0))
        before_im = jnp.where(seg_id == 0, s_in_ref[:, li], pltpu.roll(s_ref[last, li], 1, axis=0))
        acc = fix_with(0, (zero, zero), before_re, before_im)
        acc_re, acc_im = _steps(1, k_steps, fix, acc)
        da_ref[:, lr] = da_ref[:, lr] + jnp.sum(acc_re, axis=0, keepdims=True)
        da_ref[:, li] = da_ref[:, li] + jnp.sum(acc_im, axis=0, keepdims=True)


def _prenorm(x, mod3, norm_pre):
    xn, r = _rms_parts(x)
    return xn, r, xn * norm_pre * (1.0 + mod3[1:2, :]) + mod3[0:1, :]


CHIP_FLIPS = (4, 2, 6)


def _shard_order(me):
    flips = [0, 1] + [f + c for f in CHIP_FLIPS for c in (0, 1)]
    return jnp.stack([me ^ f for f in flips]).astype(jnp.int32)


def _in_proj(x, mod3, norm_pre, w_in_s, shards):
    rows = x.shape[0]
    tb = _tb(rows, 2048)
    nblk = rows // tb
    n_sh = len(shards)
    last_step = N_DEV - 1
    items = [_gather_item(0, 0, _pool_rows_of(shards[0].shape[1]))] + \
            [_gather_item(t, t, _rows_of(shards[t].shape[0])) for t in range(1, n_sh)]

    def body(order_ref, x_ref, mod_ref, np_ref, w_src, *rest):
        src_refs, proj_ref, w_full, out_refs = rest[:n_sh], rest[n_sh], rest[n_sh + 1], rest[n_sh + 2:2 * n_sh + 2]
        h_scr, wg, ssem, rsem, lsem, *sems = rest[2 * n_sh + 2:]
        s, i = pl.program_id(0), pl.program_id(1)
        me3 = _me()
        me = _flat(me3)
        sibling = _peer(1)

        def own_copy(slot, k):
            return pltpu.make_async_remote_copy(src_ref=w_src, dst_ref=wg.at[me], send_sem=ssem.at[slot],
                                                recv_sem=rsem.at[slot], device_id=_peer(k), device_id_type=MESH)

        def passed_copy(j):
            p = _flat(_peer(CHIP_FLIPS[j]))
            return pltpu.make_async_remote_copy(src_ref=wg.at[p], dst_ref=wg.at[p], send_sem=ssem.at[4 + j],
                                                recv_sem=rsem.at[4 + j], device_id=sibling, device_id_type=MESH)

        def arrival(slot, flip):
            p = _flat(_peer(flip))
            pltpu.make_async_remote_copy(src_ref=w_src, dst_ref=wg.at[p], send_sem=ssem.at[slot],
                                         recv_sem=rsem.at[slot], device_id=sibling, device_id_type=MESH).wait_recv()

        def keep(t):
            p = order_ref[t]
            return pltpu.make_async_copy(wg.at[p], w_full.at[:, pl.ds(p * W_IN_SHARD, W_IN_SHARD)], lsem.at[1 + t])

        first = i == 0
        for t in range(last_step):
            pl.when(first & (s == t + 1))(lambda t=t: keep(t).start())

        @pl.when(first & (s == 0))
        def _():
            mine = pltpu.make_async_copy(w_src, wg.at[me], lsem.at[0])
            mine.start()
            own_copy(0, 1).start()
            for j, f in enumerate(CHIP_FLIPS[:2]):
                own_copy(1 + j, f).start()
            mine.wait()

        @pl.when(first & (s == 1))
        def _():
            arrival(0, 1)

        for j, f in enumerate(CHIP_FLIPS):
            @pl.when(first & (s == 2 + 2 * j))
            def _(j=j, f=f):
                arrival(1 + j, f)
                passed_copy(j).start()
                if j == 0:
                    own_copy(3, CHIP_FLIPS[2]).start()

            @pl.when(first & (s == 3 + 2 * j))
            def _(j=j, f=f):
                arrival(4 + j, f + 1)

        @pl.when(first & (s == last_step - 1))
        def _():
            _hosted_copies(items, src_refs, out_refs, *sems, act="start")

        rows_i = pl.ds(pl.multiple_of(i * tb, tb), tb)

        @pl.when(s == 0)
        def _():
            _, _, h = _prenorm(x_ref[...], mod_ref[...], np_ref[...])
            h_scr[rows_i, :] = h.astype(BF16)

        proj_ref[...] = _dot(h_scr[rows_i, :], wg[order_ref[s]]).astype(BF16)

        @pl.when((s == last_step) & (i == nblk - 1))
        def _():
            own_copy(0, 1).wait_send()
            for j, f in enumerate(CHIP_FLIPS):
                own_copy(1 + j, f).wait_send()
                passed_copy(j).wait_send()
            keep(last_step).start()
            for t in range(N_DEV):
                keep(t).wait()
            _hosted_copies(items, src_refs, out_refs, *sems, act="wait")

    full = [jax.ShapeDtypeStruct((4, 256, 256), BF16)] + [jax.ShapeDtypeStruct((D, D), BF16)] * (n_sh - 1)
    grid_spec = pltpu.PrefetchScalarGridSpec(
        num_scalar_prefetch=1, grid=(N_DEV, nblk),
        in_specs=[pl.BlockSpec((tb, D), lambda s, i, order: (jnp.where(s == 0, i, nblk - 1), 0)),
                  pl.BlockSpec((3, D), lambda s, i, order: (0, 0)), pl.BlockSpec((1, D), lambda s, i, order: (0, 0)),
                  ANY] + [ANY] * n_sh,
        out_specs=(pl.BlockSpec((tb, W_IN_SHARD), lambda s, i, order: (i, order[s])), ANY, *([ANY] * n_sh)),
        scratch_shapes=[pltpu.VMEM((rows, D), BF16), pltpu.VMEM((N_DEV, D, W_IN_SHARD), BF16),
                        pltpu.SemaphoreType.DMA((N_DEV - 1,)), pltpu.SemaphoreType.DMA((N_DEV - 1,)),
                        pltpu.SemaphoreType.DMA((1 + N_DEV,))] + _sem_scratch(items))
    return _pcall(body, name="in_proj", grid_spec=grid_spec,
                  out_shape=(jax.ShapeDtypeStruct((rows, N_IN), BF16), jax.ShapeDtypeStruct((D, N_IN), BF16), *full),
                  compiler_params=_params(("arbitrary", "arbitrary")),
                  )(_shard_order(_flat(_me())), x, mod3, norm_pre, w_in_s, *shards)


def _pool_windows(ext, tb, first_row):
    inv_counts = _inv_counts(tb, first_row)
    pooled = []
    for g, w in enumerate(POOL_WINDOWS):
        acc = ext[:, g * 256:(g + 1) * 256]
        tok = acc[HALO:, :]
        s = 1
        while s < w:
            acc = acc + pltpu.roll(acc, s, axis=0)
            s *= 2
        pooled.append(acc[HALO:, :] * inv_counts[g] - tok)
    return pooled, inv_counts


def _inv_counts(tb, first_row):
    pos = (first_row + lax.broadcasted_iota(jnp.int32, (tb, 1), 0) + 1).astype(F32)
    return [1.0 / jnp.minimum(pos, float(w)) for w in POOL_WINDOWS]


def _pool_fwd(proj, pool_w, pool_scale):
    rows = proj.shape[0]
    tb = _tb(rows, 512)
    hb = tb // HALO

    def body(u_ref, halo_ref, z_ref, pw_ref, ps_ref, y_ref, pooled_ref):
        i = pl.program_id(0)
        u = u_ref[...].astype(F32)
        halo = jnp.where(i > 0, halo_ref[...].astype(F32), 0.0)
        pooled, _ = _pool_windows(jnp.concatenate([halo, u], axis=0), tb, i * tb)
        silu_z, _ = _silu_parts(z_ref[...].astype(F32))
        for g in range(4):
            cols = slice(g * 256, (g + 1) * 256)
            pooled_b = pooled[g].astype(BF16)
            pooled_ref[:, cols] = pooled_b
            mixed = _dot(pooled_b, pw_ref[g])
            y_ref[:, cols] = (mixed * ps_ref[:, cols] * silu_z[:, cols]).astype(BF16)

    blk = pl.BlockSpec((tb, D), lambda i: (i, 0))
    return _pcall(body, name="pool_fwd", grid=(rows // tb,),
                  out_shape=(jax.ShapeDtypeStruct((rows, D), BF16), jax.ShapeDtypeStruct((rows, D), BF16)),
                  in_specs=[blk, pl.BlockSpec((HALO, D), lambda i: (jnp.maximum(i * hb - 1, 0), 0)),
                            pl.BlockSpec((tb, D), lambda i: (i, 1)),
                            _full((4, 256, 256)), _full((1, D))],
                  out_specs=(blk, blk),
                  compiler_params=_params(("arbitrary",)))(proj, proj, proj, pool_w, pool_scale)


def _ssm_fwd(proj, pm, pmt, wb, wct, ptab, dvec, glu_w, glu_b, shards):
    rows = proj.shape[0]
    tb = pm.shape[0]
    k_steps = tb // SUBLANES
    nblk = rows // tb
    n_sh = len(shards)
    items = [_gather_item(t, t, _rows_of(shards[t].shape[0])) for t in range(n_sh)]

    def body(u_ref, z_ref, pm_ref, pmt_ref, wb_ref, wct_ref, p_ref, d_ref, gw_ref, gb_ref, *rest):
        src_refs = rest[:n_sh]
        y_ref, ys_ref, carry_out_ref, s_ref, gate_ref, zp_ref, up_ref = rest[n_sh:n_sh + 7]
        out_refs = rest[n_sh + 7:2 * n_sh + 7]
        carry_ref, enter_ref, fin_ref, *sems = rest[2 * n_sh + 7:]

        @pl.when(pl.program_id(0) == 0)
        def _():
            _hosted_copies(items, src_refs, out_refs, *sems, act="start")
            carry_ref[...] = jnp.zeros_like(carry_ref)

        carry_out_ref[...] = carry_ref[...]
        up = _dot(pm_ref[...], u_ref[...]).astype(BF16)
        up_ref[...] = up

        for q in range(N_Q):
            s_ref[:, q * Q_W:(q + 1) * Q_W] = _dot(up[:, q * 256:(q + 1) * 256], wb_ref[q])
        for q in range(N_Q):
            _scan_forward(q, s_ref, p_ref, carry_ref, enter_ref, fin_ref, k_steps)
        for q in range(N_Q):
            cols = slice(q * 256, (q + 1) * 256)
            y = _dot_nt(s_ref[:, q * Q_W:(q + 1) * Q_W].astype(BF16), wct_ref[q])
            ys_ref[:, cols] = y + d_ref[:, cols] * up[:, cols].astype(F32)
        yg, _ = _gelu_parts(ys_ref[...])
        gate = jax.nn.sigmoid(_dot(yg.astype(BF16), gw_ref[...]) + gb_ref[...])
        gate_ref[...] = gate
        zp = _dot(pm_ref[...], z_ref[...])
        zp_ref[...] = zp.astype(BF16)
        silu_z, _ = _silu_parts(zp)
        y_ref[...] = _dot(pmt_ref[...], (yg * gate * silu_z).astype(BF16)).astype(BF16)

        @pl.when(pl.program_id(0) == nblk - 1)
        def _():
            _hosted_copies(items, src_refs, out_refs, *sems, act="wait")

    return _pcall(body, name="ssm_fwd", grid=(nblk,),
                  out_shape=(jax.ShapeDtypeStruct((rows, D), BF16), jax.ShapeDtypeStruct((rows, D), F32),
                             jax.ShapeDtypeStruct((nblk, 1, N_STATE), F32),
                             jax.ShapeDtypeStruct((rows, N_STATE), F32),
                             jax.ShapeDtypeStruct((rows, D), F32), jax.ShapeDtypeStruct((rows, D), BF16),
                             jax.ShapeDtypeStruct((rows, D), BF16),
                             *[jax.ShapeDtypeStruct((D, D), BF16)] * n_sh),
                  in_specs=[pl.BlockSpec((tb, D), lambda i: (i, 2)), pl.BlockSpec((tb, D), lambda i: (i, 3)),
                            _full((tb, tb)), _full((tb, tb)),
                            _full((N_Q, 256, Q_W), single=True), _full((N_Q, 256, Q_W), single=True),
                            _full((k_steps, N_STATE)), _full((1, D)), _full((D, D), single=True), _full((1, D))] +
                           [ANY] * n_sh,
                  out_specs=(pl.BlockSpec((tb, D), lambda i: (i, 0)), pl.BlockSpec((tb, D), lambda i: (i, 0)),
                             pl.BlockSpec((None, 1, N_STATE), lambda i: (i, 0, 0)),
                             pl.BlockSpec((tb, N_STATE), lambda i: (i, 0)),
                             *[pl.BlockSpec((tb, D), lambda i: (i, 0))] * 3, *([ANY] * n_sh)),
                  scratch_shapes=[pltpu.VMEM((1, N_STATE), F32),
                                  pltpu.VMEM((SUBLANES, N_STATE), F32), pltpu.VMEM((SUBLANES, N_STATE), F32)] +
                                 _sem_scratch(items),
                  compiler_params=_params(("arbitrary",)))(proj, proj, pm, pmt, wb, wct, ptab, dvec, glu_w, glu_b,
                                                           *shards)


def _head(x, target, proj, y_pool, y_ssm, mod3, norm_post, wbp, wbs, wout):
    rows = x.shape[0]
    tb = _tb(rows, 256)
    nblk = rows // tb
    n_feat = float(D)

    def body(x_ref, t_ref, gp_ref, gs_ref, yp_ref, ys_ref, mod_ref, npost_ref, wbp_ref, wbs_ref, wout_ref,
             loss_ref, dy_ref, dyp_ref, dys_ref, dg_ref, dwbp_hbm, dwbs_hbm, dwout_hbm, vec_ref,
             acc_bp, acc_bs, acc_out, acc_loss, acc_vec):
        i = pl.program_id(0)

        @pl.when(i == 0)
        def _():
            acc_bp[...] = jnp.zeros_like(acc_bp)
            acc_bs[...] = jnp.zeros_like(acc_bs)
            acc_out[...] = jnp.zeros_like(acc_out)
            acc_loss[...] = jnp.zeros_like(acc_loss)
            acc_vec[...] = jnp.zeros_like(acc_vec)

        gate = mod_ref[2:3, :]
        npost = npost_ref[...]
        keep = []
        for part in range(HEAD_PARTS):
            rs = slice(part * tb // HEAD_PARTS, (part + 1) * tb // HEAD_PARTS)
            yp, ys = yp_ref[rs, :], ys_ref[rs, :]
            sgp = jax.nn.sigmoid(gp_ref[rs, :].astype(F32))
            sgs = jax.nn.sigmoid(gs_ref[rs, :].astype(F32))
            pb = _dot(yp, wbp_ref[...])
            psm = _dot(ys, wbs_ref[...])
            mb = (sgp * pb + sgs * psm).astype(BF16)
            out = _dot(mb, wout_ref[...])
            on, r = _rms_parts(out)
            normed = on * npost
            diff = x_ref[rs, :] + gate * normed - t_ref[rs, :]
            acc_loss[...] += jnp.sum(diff * diff, axis=0, keepdims=True)
            dy = diff * (1.0 / n_feat)
            dy_ref[rs, :] = dy
            acc_vec[0:1, :] += jnp.sum(dy * normed, axis=0, keepdims=True)
            dn = dy * gate
            acc_vec[1:2, :] += jnp.sum(dn * on, axis=0, keepdims=True)
            dout = _rms_bwd(dn * npost, on, r).astype(BF16)
            dm = _dot_nt(dout, wout_ref[...])
            dpb = (dm * sgp).astype(BF16)
            dps = (dm * sgs).astype(BF16)
            dg_ref[rs, :D] = (dm * pb * sgp * (1.0 - sgp)).astype(BF16)
            dg_ref[rs, D:] = (dm * psm * sgs * (1.0 - sgs)).astype(BF16)
            dyp_ref[rs, :] = _dot_nt(dpb, wbp_ref[...]).astype(BF16)
            dys_ref[rs, :] = _dot_nt(dps, wbs_ref[...]).astype(BF16)
            keep.append((mb, dout, dpb, dps))
        cat = lambda k: jnp.concatenate([p[k] for p in keep], axis=0) if HEAD_PARTS > 1 else keep[0][k]
        acc_out[...] += _dot_tn(cat(0), cat(1))
        acc_bp[...] += _dot_tn(yp_ref[...], cat(2))
        acc_bs[...] += _dot_tn(ys_ref[...], cat(3))

        @pl.when(i == nblk - 1)
        def _():
            loss_ref[...] = 0.5 / n_feat * jnp.sum(acc_loss[...], axis=1, keepdims=True)
            vec_ref[...] = acc_vec[...]
            pltpu.sync_copy(acc_bp, dwbp_hbm)
            pltpu.sync_copy(acc_bs, dwbs_hbm)
            pltpu.sync_copy(acc_out, dwout_hbm)

    row = lambda c: pl.BlockSpec((tb, D), lambda i: (i, c))
    w = _full((D, D), single=True)
    return _pcall(body, name="head", grid=(nblk,),
                  out_shape=(jax.ShapeDtypeStruct((1, 1), F32), jax.ShapeDtypeStruct((rows, D), F32),
                             jax.ShapeDtypeStruct((rows, D), BF16), jax.ShapeDtypeStruct((rows, D), BF16),
                             jax.ShapeDtypeStruct((rows, 2 * D), BF16),
                             jax.ShapeDtypeStruct((D, D), F32), jax.ShapeDtypeStruct((D, D), F32),
                             jax.ShapeDtypeStruct((D, D), F32), jax.ShapeDtypeStruct((2, D), F32)),
                  in_specs=[row(0), row(0), row(4), row(5), row(0), row(0), _full((3, D)), _full((1, D)), w, w, w],
                  out_specs=(_full((1, 1)), row(0), row(0), row(0), pl.BlockSpec((tb, 2 * D), lambda i: (i, 0)),
                             ANY, ANY, ANY, _full((2, D))),
                  scratch_shapes=[pltpu.VMEM((D, D), F32), pltpu.VMEM((D, D), F32), pltpu.VMEM((D, D), F32),
                                  pltpu.VMEM((1, D), F32), pltpu.VMEM((2, D), F32)],
                  compiler_params=_params(("arbitrary",)))(x, target, proj, proj, y_pool, y_ssm, mod3, norm_post,
                                                           wbp, wbs, wout)


def _glu_bwd(dys, zp, ys_pre, gate, pm, pmt, glu_w):
    rows = dys.shape[0]
    tb = pm.shape[0]
    nblk = rows // tb

    def body(dys_ref, z_ref, ysp_ref, sg_ref, pm_ref, pmt_ref, gw_ref, dyp_ref, dz_ref, dgw_hbm, dgb_ref,
             acc_w, acc_b):
        i = pl.program_id(0)

        @pl.when(i == 0)
        def _():
            acc_w[...] = jnp.zeros_like(acc_w)
            acc_b[...] = jnp.zeros_like(acc_b)

        d_out = _dot(pm_ref[...], dys_ref[...])
        yg, dgelu = _gelu_parts(ysp_ref[...])
        ygb = yg.astype(BF16)
        sg = sg_ref[...]
        silu_z, dsilu_z = _silu_parts(z_ref[...].astype(F32))
        dz = d_out * (yg * sg) * dsilu_z
        dz_ref[...] = _dot(pmt_ref[...], dz.astype(BF16)).astype(BF16)
        dglu = d_out * silu_z
        dq = dglu * yg * sg * (1.0 - sg)
        dqb = dq.astype(BF16)
        acc_b[...] += jnp.sum(dq, axis=0, keepdims=True)
        acc_w[...] += _dot_tn(ygb, dqb)
        dyg = dglu * sg + _dot_nt(dqb, gw_ref[...])
        dyp_ref[...] = (dyg * dgelu).astype(BF16)

        @pl.when(i == nblk - 1)
        def _():
            dgb_ref[...] = acc_b[...]
            pltpu.sync_copy(acc_w, dgw_hbm)

    row = lambda c: pl.BlockSpec((tb, D), lambda i: (i, c))
    return _pcall(body, name="glu_bwd", grid=(nblk,),
                  out_shape=(jax.ShapeDtypeStruct((rows, D), BF16), jax.ShapeDtypeStruct((rows, D), BF16),
                             jax.ShapeDtypeStruct((D, D), F32), jax.ShapeDtypeStruct((1, D), F32)),
                  in_specs=[row(0), row(0), row(0), row(0), _full((tb, tb)), _full((tb, tb)),
                            _full((D, D), single=True)],
                  out_specs=(row(0), row(0), ANY, _full((1, D))),
                  scratch_shapes=[pltpu.VMEM((D, D), F32), pltpu.VMEM((1, D), F32)],
                  compiler_params=_params(("arbitrary",)))(dys, zp, ys_pre, gate, pm, pmt, glu_w)


def _ssm_bwd(dyp, up, states, carries, pm, pmt, wb, wct, ptab, dvec, mat_grads, dpool_w, dw_in_rest):
    rows = dyp.shape[0]
    tb = pm.shape[0]
    k_steps = tb // SUBLANES
    nblk = rows // tb
    n_mat = len(mat_grads)
    hosted = [*mat_grads, dpool_w, dw_in_rest]
    n_h = len(hosted)
    shard_rows = D // N_DEV
    pool_rows = dpool_w.shape[1] // N_DEV
    items = [_scatter_item(t, t, _rows_of(shard_rows)) for t in range(n_mat)] + \
            [_scatter_item(n_mat, n_mat, _pool_rows_of(pool_rows))] + \
            [_w_in_block_item(n_mat + 1, n_mat + 1, j, ssm_part=False) for j in range(W_IN_SHARD // W_IN_BLOCK)]
    n_in, n_out = 10, 5

    def body(*refs):
        dyp_ref, u_ref, s_ref, cin_ref, pm_ref, pmt_ref, wb_ref, wct_ref, p_ref, d_ref = refs[:n_in]
        src_refs = refs[n_in:n_in + n_h]
        du_ref, dbb_ref, dcc_ref, da_ref, dd_ref = refs[n_in + n_h:n_in + n_h + n_out]
        recv_refs = refs[n_in + n_h + n_out:n_in + 2 * n_h + n_out]
        (g_ref, carry_b, fin_ref, acc_wb, acc_wct, acc_da, acc_dd, dup_ref,
         *sems) = refs[n_in + 2 * n_h + n_out:]
        i = pl.program_id(0)

        @pl.when(i == 0)
        def _():
            _hosted_copies(items, src_refs, recv_refs, *sems, act="start")
            carry_b[...] = jnp.zeros_like(carry_b)
            acc_wb[...] = jnp.zeros_like(acc_wb)
            acc_wct[...] = jnp.zeros_like(acc_wct)
            acc_da[...] = jnp.zeros_like(acc_da)
            acc_dd[...] = jnp.zeros_like(acc_dd)

        def keep_own(acc, q, prod):
            for gl in range(16):
                r, c = slice(gl * G_H, (gl + 1) * G_H), (gl // 2) * 128
                acc[q, r, 0:128] += prod[r, c:c + 128]
                acc[q, r, 128:256] += prod[r, Q_W // 2 + c:Q_W // 2 + c + 128]

        dy = dyp_ref[...]
        up = u_ref[...]
        acc_dd[...] += jnp.sum(dy.astype(F32) * up.astype(F32), axis=0, keepdims=True)
        for q in range(N_Q):
            cols = slice(q * 256, (q + 1) * 256)
            g_ref[:, q * Q_W:(q + 1) * Q_W] = _dot(dy[:, cols], wct_ref[q])
            keep_own(acc_wct, q, _dot_tn(dy[:, cols], s_ref[:, q * Q_W:(q + 1) * Q_W].astype(BF16)))
        for q in range(N_Q):
            _scan_backward(q, g_ref, s_ref, p_ref, carry_b, cin_ref, fin_ref, acc_da, k_steps)
        for q in range(N_Q):
            cols = slice(q * 256, (q + 1) * 256)
            lam = g_ref[:, q * Q_W:(q + 1) * Q_W].astype(BF16)
            keep_own(acc_wb, q, _dot_tn(up[:, cols], lam))
            dup_ref[:, cols] = (_dot_nt(lam, wb_ref[q]) + d_ref[:, cols] * dy[:, cols].astype(F32)).astype(BF16)
        du_ref[...] = _dot(pmt_ref[...], dup_ref[...]).astype(BF16)

        @pl.when(i == nblk - 1)
        def _():
            da_ref[...] = acc_da[...]
            dd_ref[...] = acc_dd[...]
            lane = lax.broadcasted_iota(jnp.int32, (16 * G_H, 128), 1)
            row = lax.broadcasted_iota(jnp.int32, (16 * G_H, 128), 0)
            own = lane // G_P == (row // G_H) % 2
            spread = (lax.broadcasted_iota(jnp.int32, (G_P, 128), 1) % G_P ==
                      lax.broadcasted_iota(jnp.int32, (G_P, 128), 0)).astype(F32)
            for acc, out in ((acc_wb, dbb_ref), (acc_wct, dcc_ref)):
                for half in range(2):
                    for q in range(N_Q):
                        kept = jnp.where(own, acc[q, :, half * 128:(half + 1) * 128], 0.0)
                        out[half, q] = lax.dot_general(kept, spread, (((1,), (1,)), ((), ())),
                                                       preferred_element_type=F32, precision=lax.Precision.HIGHEST)
            _hosted_copies(items, src_refs, recv_refs, *sems, act="wait")

    rev = lambda c: pl.BlockSpec((tb, D), lambda i: (nblk - 1 - i, c))
    recv = [jax.ShapeDtypeStruct((N_DEV, shard_rows, D), F32)] * n_mat + \
           [jax.ShapeDtypeStruct((N_DEV, dpool_w.shape[0], pool_rows, dpool_w.shape[2]), F32),
            jax.ShapeDtypeStruct((N_DEV, D, W_IN_SHARD), BF16)]
    return _pcall(body, name="ssm_bwd", grid=(nblk,),
                  out_shape=(jax.ShapeDtypeStruct((rows, D), BF16),
                             jax.ShapeDtypeStruct((2, N_Q, 16 * G_H, G_P), F32),
                             jax.ShapeDtypeStruct((2, N_Q, 16 * G_H, G_P), F32),
                             jax.ShapeDtypeStruct((1, N_STATE), F32), jax.ShapeDtypeStruct((1, D), F32), *recv),
                  in_specs=[rev(0), rev(0), pl.BlockSpec((tb, N_STATE), lambda i: (nblk - 1 - i, 0)),
                            pl.BlockSpec((None, 1, N_STATE), lambda i: (nblk - 1 - i, 0, 0)),
                            _full((tb, tb)), _full((tb, tb)),
                            _full((N_Q, 256, Q_W), single=True), _full((N_Q, 256, Q_W), single=True),
                            _full((k_steps, N_STATE)), _full((1, D))] + [ANY] * n_h,
                  out_specs=(rev(0), _full((2, N_Q, 16 * G_H, G_P)), _full((2, N_Q, 16 * G_H, G_P)),
                             _full((1, N_STATE)), _full((1, D)), *([ANY] * n_h)),
                  scratch_shapes=[pltpu.VMEM((tb, N_STATE), F32), pltpu.VMEM((1, N_STATE), F32),
                                  pltpu.VMEM((SUBLANES, N_STATE), F32),
                                  pltpu.VMEM((N_Q, 16 * G_H, 256), F32), pltpu.VMEM((N_Q, 16 * G_H, 256), F32),
                                  pltpu.VMEM((1, N_STATE), F32), pltpu.VMEM((1, D), F32),
                                  pltpu.VMEM((tb, D), BF16)] + _sem_scratch(items),
                  compiler_params=_params(("arbitrary",), vmem=60 * 1024 * 1024),
                  )(dyp, up, states, carries, pm, pmt, wb, wct, ptab, dvec, *hosted)


def _pool_bwd(dyp, pooled, proj, pool_w, pool_scale):
    rows = dyp.shape[0]
    tb = _tb(rows, 512)
    nblk = rows // tb

    def body(dy_ref, pooled_ref, z_ref, pw_ref, ps_ref, dp_ref, dpw_ref, dps_ref, ahead_ref):
        i = pl.program_id(0)
        blk = nblk - 1 - i

        @pl.when(i == 0)
        def _():
            ahead_ref[...] = jnp.zeros_like(ahead_ref)
            dpw_ref[...] = jnp.zeros_like(dpw_ref)
            dps_ref[...] = jnp.zeros_like(dps_ref)

        inv_counts = _inv_counts(tb, blk * tb)
        silu_z, dsilu_z = _silu_parts(z_ref[...].astype(F32))
        dy = dy_ref[...].astype(F32)
        for g, w in enumerate(POOL_WINDOWS):
            cols = slice(g * 256, (g + 1) * 256)
            pooled_b = pooled_ref[:, cols]
            mixed = _dot(pooled_b, pw_ref[g])
            scale = ps_ref[:, cols]
            dp_ref[:, D + g * 256:D + (g + 1) * 256] = (dy[:, cols] * (mixed * scale) * dsilu_z[:, cols]).astype(BF16)
            dms = dy[:, cols] * silu_z[:, cols]
            dps_ref[:, cols] += jnp.sum(dms * mixed, axis=0, keepdims=True)
            dmixed = (dms * scale).astype(BF16)
            dpw_ref[g] += _dot_tn(pooled_b, dmixed)
            dpooled = _dot_nt(dmixed, pw_ref[g])
            ratio = dpooled * inv_counts[g]
            acc = jnp.concatenate([ratio, ahead_ref[:, cols]], axis=0)
            ahead_ref[:, cols] = ratio[:HALO, :]
            s = 1
            while s < w:
                acc = acc + pltpu.roll(acc, tb + HALO - s, axis=0)
                s *= 2
            dp_ref[:, cols] = (acc[:tb, :] - dpooled).astype(BF16)

    rev = lambda c: pl.BlockSpec((tb, D), lambda i: (nblk - 1 - i, c))
    return _pcall(body, name="pool_bwd", grid=(nblk,),
                  out_shape=(jax.ShapeDtypeStruct((rows, 2 * D), BF16), jax.ShapeDtypeStruct((4, 256, 256), F32),
                             jax.ShapeDtypeStruct((1, D), F32)),
                  in_specs=[rev(0), rev(0), rev(1), _full((4, 256, 256)), _full((1, D))],
                  out_specs=(pl.BlockSpec((tb, 2 * D), lambda i: (nblk - 1 - i, 0)), _full((4, 256, 256)),
                             _full((1, D))),
                  scratch_shapes=[pltpu.VMEM((HALO, D), F32)],
                  compiler_params=_params(("arbitrary",)))(dyp, pooled, proj, pool_w, pool_scale)


def _dproj_specs(tb):
    return [pl.BlockSpec((tb, 2 * D), lambda i: (i, 0)), pl.BlockSpec((tb, D), lambda i: (i, 0)),
            pl.BlockSpec((tb, D), lambda i: (i, 0)), pl.BlockSpec((tb, 2 * D), lambda i: (i, 0))]


def _in_proj_bwd_x(x, dy, dpp, dus, dzs, dpg, mod3, norm_pre, w_in, dw_in_ssm, recv_w_in):
    rows = x.shape[0]
    tb = _tb(rows, 512)
    nblk = rows // tb
    items = [_w_in_block_item(0, 0, j, ssm_part=True) for j in range(W_IN_SHARD // W_IN_BLOCK)]
    sums_item = [_Item(0, 0, _whole, _slot)]

    def body(x_ref, dy_ref, dpp_ref, dus_ref, dzs_ref, dpg_ref, mod_ref, np_ref, w_ref,
             dw_src, _, gx_ref, recv_w, recv_sums, vec_ref, ssem, rsem, lsem, *sums_sems):
        src_refs, recv_refs, sems = (dw_src,), (recv_w,), (ssem, rsem, lsem)

        @pl.when(pl.program_id(0) == 0)
        def _():
            _hosted_copies(items, src_refs, recv_refs, *sems, act="start")
            vec_ref[...] = jnp.zeros_like(vec_ref)

        dh = _dot_nt(dpp_ref[...], w_ref[:, 0:2 * D])
        dh += _dot_nt(dus_ref[...], w_ref[:, 2 * D:3 * D])
        dh += _dot_nt(dzs_ref[...], w_ref[:, 3 * D:4 * D])
        dh += _dot_nt(dpg_ref[...], w_ref[:, 4 * D:6 * D])
        xn, r, _ = _prenorm(x_ref[...], mod_ref[...], np_ref[...])
        one_scale = 1.0 + mod_ref[1:2, :]
        vec_ref[0:1, :] += jnp.sum(dh, axis=0, keepdims=True)
        vec_ref[1:2, :] += jnp.sum(dh * xn, axis=0, keepdims=True) * np_ref[...]
        vec_ref[2:3, :] += jnp.sum(dh * xn, axis=0, keepdims=True) * one_scale
        gx_ref[...] = dy_ref[...] + _rms_bwd(dh * (np_ref[...] * one_scale), xn, r)

        @pl.when(pl.program_id(0) == nblk - 1)
        def _():
            _hosted_copies(sums_item, (vec_ref,), (recv_sums,), *sums_sems, act="start")
            _hosted_copies(items, src_refs, recv_refs, *sems, act="wait")
            _hosted_copies(sums_item, (vec_ref,), (recv_sums,), *sums_sems, act="wait")

    row = pl.BlockSpec((tb, D), lambda i: (i, 0))
    recv = (jax.ShapeDtypeStruct(recv_w_in.shape, recv_w_in.dtype), jax.ShapeDtypeStruct((N_DEV, 3, D), F32))
    return _pcall(body, name="in_proj_bwd_x", grid=(nblk,),
                  out_shape=(jax.ShapeDtypeStruct((rows, D), F32), *recv),
                  in_specs=[row, row] + _dproj_specs(tb) + [_full((3, D)), _full((1, D)),
                                                            _full((D, N_IN), single=True)] + [ANY] * 2,
                  out_specs=(row, ANY, ANY),
                  input_output_aliases={10: 1},
                  scratch_shapes=[pltpu.VMEM((3, D), F32)] + _sem_scratch(items) + _sem_scratch(sums_item),
                  compiler_params=_params(("arbitrary",)))(x, dy, dpp, dus, dzs, dpg, mod3, norm_pre, w_in,
                                                           dw_in_ssm, recv_w_in)


def _in_proj_bwd_w(name, x, dparts, mod3, norm_pre, gathered=()):
    rows = x.shape[0]
    tb = _tb(rows, 512)
    nblk = rows // tb
    widths = [p.shape[1] for p in dparts]
    n_p, n_g = len(dparts), len(gathered)
    items = [_Item(t, t, _whole, _slot) for t in range(n_g)]

    def body(x_ref, *rest):
        part_refs, (mod_ref, np_ref) = rest[:n_p], rest[n_p:n_p + 2]
        src_refs, dw_ref = rest[n_p + 2:n_p + 2 + n_g], rest[n_p + 2 + n_g]
        recv_refs, (acc, *sems) = rest[n_p + 3 + n_g:n_p + 3 + 2 * n_g], rest[n_p + 3 + 2 * n_g:]
        i = pl.program_id(0)

        @pl.when(i == 0)
        def _():
            if n_g:
                _hosted_copies(items, src_refs, recv_refs, *sems, act="start")
            acc[...] = jnp.zeros_like(acc)

        _, _, h = _prenorm(x_ref[...], mod_ref[...], np_ref[...])
        ht = h.astype(BF16)
        lo = 0
        for ref, w in zip(part_refs, widths):
            acc[:, lo:lo + w] += _dot_tn(ht, ref[...])
            lo += w

        @pl.when(i == nblk - 1)
        def _():
            dw_ref[...] = acc[...].astype(BF16)
            if n_g:
                _hosted_copies(items, src_refs, recv_refs, *sems, act="wait")

    row = pl.BlockSpec((tb, D), lambda i: (i, 0))
    out = _pcall(body, name=name, grid=(nblk,),
                 out_shape=(jax.ShapeDtypeStruct((D, sum(widths)), BF16),
                            *[jax.ShapeDtypeStruct((N_DEV,) + g.shape, g.dtype) for g in gathered]),
                 in_specs=[row] + [pl.BlockSpec((tb, w), lambda i: (i, 0)) for w in widths] +
                          [_full((3, D)), _full((1, D))] + [ANY] * n_g,
                 out_specs=(_full((D, sum(widths))), *([ANY] * n_g)),
                 scratch_shapes=[pltpu.VMEM((D, sum(widths)), F32)] + (_sem_scratch(items) if n_g else []),
                 compiler_params=_params(("arbitrary",)))(x, *dparts, mod3, norm_pre, *gathered)
    return out if n_g else out[0]


def _adamw_math(w, g, m, v):
    m = ADAM_B1 * m + (1.0 - ADAM_B1) * g
    v = ADAM_B2 * v + (1.0 - ADAM_B2) * (g * g)
    m_hat = m / (1.0 - ADAM_B1 ** ADAM_STEP)
    v_hat = v / (1.0 - ADAM_B2 ** ADAM_STEP)
    delta = -ADAM_LR * (m_hat / (jnp.sqrt(v_hat) + ADAM_EPS) + ADAM_WD * w)
    return delta, m, v


def _sum_sources(ref):
    g = ref[0].astype(F32)
    for s in range(1, N_DEV):
        g = g + ref[s].astype(F32)
    return g


def _adamw_reduce(name, parts, w, m, v):
    r, c = w.shape
    tr = r if r * c <= 256 * 1024 else max(8, (256 * 1024 // c) // 8 * 8)
    while r % tr:
        tr -= 8

    def body(p_ref, w_ref, m_ref, v_ref, g_ref, d_ref, nm_ref, nv_ref):
        g = _sum_sources(p_ref)
        g_ref[...] = g
        d_ref[...], nm_ref[...], nv_ref[...] = _adamw_math(w_ref[...], g, m_ref[...], v_ref[...])

    blk = pl.BlockSpec((tr, c), lambda i: (i, 0))
    return _pcall(body, name=name, grid=(r // tr,),
                  out_shape=tuple([jax.ShapeDtypeStruct((r, c), F32)] * 4),
                  in_specs=[pl.BlockSpec((N_DEV, tr, c), lambda i: (0, i, 0)), blk, blk, blk],
                  out_specs=(blk, blk, blk, blk),
                  compiler_params=_params(("arbitrary",)))(parts, w, m, v)


def _adamw_small(gs, ws, ms, vs):
    n = len(gs)

    def body(*refs):
        ins, outs = refs[:4 * n], refs[4 * n:]
        for t in range(n):
            g_ref, w_ref, m_ref, v_ref = ins[4 * t:4 * t + 4]
            outs[3 * t][...], outs[3 * t + 1][...], outs[3 * t + 2][...] = _adamw_math(
                w_ref[...], g_ref[...], m_ref[...], v_ref[...])

    vm = pl.BlockSpec(memory_space=pltpu.VMEM)
    flat = [a for t in range(n) for a in (gs[t], ws[t], ms[t], vs[t])]
    return _pcall(body, name="adamw_small",
                  out_shape=tuple(jax.ShapeDtypeStruct(w.shape, F32) for w in ws for _ in range(3)),
                  in_specs=[vm] * (4 * n), out_specs=tuple([vm] * (3 * n)), compiler_params=_params())(*flat)


def _sum_small(parts):
    n = len(parts)

    def body(*refs):
        for t in range(n):
            refs[n + t][...] = _sum_sources(refs[t])

    vm = pl.BlockSpec(memory_space=pltpu.VMEM)
    return _pcall(body, name="sum_small",
                  out_shape=tuple(jax.ShapeDtypeStruct(p.shape[1:], F32) for p in parts),
                  in_specs=[vm] * n, out_specs=tuple([vm] * n), compiler_params=_params())(*parts)


def _ada_update(c_all, dmod_cols, w, m, v):
    def body(c_ref, dm_ref, w_ref, m_ref, v_ref, g_ref, d_ref, nm_ref, nv_ref):
        ca = c_ref[...]
        g = lax.dot_general(ca * jax.nn.sigmoid(ca), dm_ref[...], (((0,), (0,)), ((), ())),
                            preferred_element_type=F32, precision=lax.Precision.HIGHEST)
        g_ref[...] = g
        d_ref[...], nm_ref[...], nv_ref[...] = _adamw_math(w_ref[...], g, m_ref[...], v_ref[...])

    vm = pl.BlockSpec(memory_space=pltpu.VMEM)
    return _pcall(body, name="ada_update", out_shape=tuple([jax.ShapeDtypeStruct(w.shape, F32)] * 4),
                  in_specs=[vm] * 5, out_specs=(vm, vm, vm, vm), compiler_params=_params())(c_all, dmod_cols, w, m, v)


def kernel(x, c, w_ada, b_ada, norm_pre, norm_post, w_in, pool_w, pool_scale, ssm_a_re, ssm_a_im, ssm_log_dt, ssm_b_re, ssm_b_im, ssm_c_re, ssm_c_im, ssm_d, glu_w, glu_b, w_branch_pool, w_branch_ssm, w_out, loss_target, m_w_ada, m_b_ada, m_norm_pre, m_norm_post, m_w_in, m_pool_w, m_pool_scale, m_ssm_a_re, m_ssm_a_im, m_ssm_log_dt, m_ssm_b_re, m_ssm_b_im, m_ssm_c_re, m_ssm_c_im, m_ssm_d, m_glu_w, m_glu_b, m_w_branch_pool, m_w_branch_ssm, m_w_out, v_w_ada, v_b_ada, v_norm_pre, v_norm_post, v_w_in, v_pool_w, v_pool_scale, v_ssm_a_re, v_ssm_a_im, v_ssm_log_dt, v_ssm_b_re, v_ssm_b_im, v_ssm_c_re, v_ssm_c_im, v_ssm_d, v_glu_w, v_glu_b, v_w_branch_pool, v_w_branch_ssm, v_w_out):
    given = dict(locals())
    me = _flat(_me())
    rows = x.shape[1]
    x2 = x[0]
    target = loss_target[0]
    ada_cols = w_ada.shape[2]

    b_ada_s = lax.dynamic_slice(b_ada, (0, me * ada_cols), (1, ada_cols))
    c_all, mod_rows = _ada_exchange(c, w_ada[0], b_ada_s)
    mod3 = mod_rows.reshape(3, D)

    shards = _cast_shards([w_in[0], pool_w[0], glu_w[0], w_branch_pool[0], w_branch_ssm[0], w_out[0]])

    tb_ssm = _tb(rows, 256)
    k_steps = tb_ssm // SUBLANES
    a_re, a_im = ssm_a_re[0], ssm_a_im[0]
    log_dt = ssm_log_dt[0].reshape(GROUPS, 1)
    b_re_t, b_im_t = ssm_b_re[0].transpose(0, 2, 1), ssm_b_im[0].transpose(0, 2, 1)
    wb, wct, pow_re, pow_im = _s5_prep(a_re, a_im, log_dt, b_re_t, b_im_t, ssm_c_re[0], ssm_c_im[0], k_steps)
    ptab = _state_layout(pow_re, pow_im)
    dvec = ssm_d[0].reshape(1, D)
    pm = _perm_matrix(tb_ssm)
    pmt = pm.T

    proj, w_in_g, pool_w_g, glu_g = _in_proj(x2, mod3, norm_pre, shards[0], shards[1:3])
    y_pool, pooled = _pool_fwd(proj, pool_w_g, pool_scale)
    y_ssm, ys_pre, carries, states, glu_gate, z_perm, u_perm, wbp_g, wbs_g, wout_g = _ssm_fwd(
        proj, pm, pmt, wb, wct, ptab, dvec, glu_g, glu_b, shards[3:])
    loss_part, dy, dyp, dys, dpg, dwbp, dwbs, dwout, head_vec = _head(
        x2, target, proj, y_pool, y_ssm, mod3, norm_post, wbp_g, wbs_g, wout_g)

    dpp, dpool_w, dpool_scale = _pool_bwd(dyp, pooled, proj, pool_w_g, pool_scale)
    dw_in_rest = _in_proj_bwd_w("in_proj_bwd_w_rest", x2, [dpp, dpg], mod3, norm_pre)
    dy_pre, dzs, dglu_w, dglu_b = _glu_bwd(dys, z_perm, ys_pre, glu_gate, pm, pmt, glu_g)
    dus, dbb, dcc, dabar, dd, p_glu, p_wbp, p_wbs, p_wout, p_pool_w, p_w_in = _ssm_bwd(
        dy_pre, u_perm, states, carries, pm, pmt, wb, wct, ptab, dvec, [dglu_w, dwbp, dwbs, dwout], dpool_w,
        dw_in_rest)

    small32 = jnp.concatenate([head_vec, dpool_scale, dglu_b, dd, jnp.broadcast_to(loss_part, (1, D)),
                               jnp.zeros((2, D), F32), dabar.reshape(8, D)], axis=0)
    small16 = jnp.concatenate([dbb.reshape(2 * GROUPS, D), dcc.reshape(2 * GROUPS, D)], axis=0).astype(BF16)
    dw_in_ssm, p_small32, p_small16 = _in_proj_bwd_w("in_proj_bwd_w_ssm", x2, [dus, dzs], mod3, norm_pre,
                                                     gathered=(small32, small16))
    grad_x, p_w_in, p_pre = _in_proj_bwd_x(x2, dy, dpp, dus, dzs, dpg, mod3, norm_pre, w_in_g, dw_in_ssm, p_w_in)

    tot32, tot16, tot_pre = _sum_small([p_small32, p_small16, p_pre])
    d_abar_re, d_abar_im = _state_unlayout(tot32[8:16].reshape(N_STATE))
    d_bb_re, d_bb_im = tot16[0:64].reshape(GROUPS, G_H, G_P), tot16[64:128].reshape(GROUPS, G_H, G_P)
    g_a_re, g_a_im, g_log_dt, g_b_re_t, g_b_im_t = _s5_prep_bwd(
        a_re, a_im, log_dt, b_re_t, b_im_t, d_abar_re, d_abar_im, d_bb_re, d_bb_im)

    grads, deltas, new_m, new_v = {}, {}, {}, {}

    small = []

    def small_update(name, g2):
        small.append((name, g2))

    def shard_update(name, parts):
        shape = given[name].shape
        r2 = parts.shape[1:] if parts.ndim == 3 else (parts.shape[1] * parts.shape[2], parts.shape[3])
        w2, m2, v2 = (given[p + name].reshape(r2) for p in ("", "m_", "v_"))
        out = _adamw_reduce("adamw_" + name, parts.reshape((N_DEV,) + tuple(r2)), w2, m2, v2)
        grads[name], deltas[name], new_m[name], new_v[name] = (a.reshape(shape) for a in out)

    dmod_all = jnp.concatenate([p_pre[:, 0:2, :], p_small32[:, 0:1, :]], axis=1).reshape(N_DEV, 3 * D)
    dmod_cols = lax.dynamic_slice(dmod_all, (0, me * ada_cols), (N_DEV, ada_cols))
    out = _ada_update(c_all, dmod_cols, w_ada[0], m_w_ada[0], v_w_ada[0])
    grads['w_ada'], deltas['w_ada'], new_m['w_ada'], new_v['w_ada'] = (a.reshape(w_ada.shape) for a in out)

    small_update('b_ada', jnp.concatenate([tot_pre[0:2], tot32[0:1]], axis=0).reshape(1, 3 * D))
    small_update('norm_pre', tot_pre[2:3])
    small_update('norm_post', tot32[1:2])
    small_update('pool_scale', tot32[2:3])
    small_update('glu_b', tot32[3:4])
    small_update('ssm_d', tot32[4:5])
    small_update('ssm_a_re', g_a_re)
    small_update('ssm_a_im', g_a_im)
    small_update('ssm_log_dt', g_log_dt.reshape(1, GROUPS))
    small_update('ssm_b_re', g_b_re_t.transpose(0, 2, 1).reshape(GROUPS, G_P * G_H))
    small_update('ssm_b_im', g_b_im_t.transpose(0, 2, 1).reshape(GROUPS, G_P * G_H))
    small_update('ssm_c_re', tot16[128:192])
    small_update('ssm_c_im', -tot16[192:256])
    flat = _adamw_small([g2 for _, g2 in small],
                        *[[given[p + name].reshape(g2.shape) for name, g2 in small] for p in ("", "m_", "v_")])
    for t, (name, g2) in enumerate(small):
        shape = given[name].shape
        grads[name], deltas[name], new_m[name], new_v[name] = (
            a.reshape(shape) for a in (g2, *flat[3 * t:3 * t + 3]))
    shard_update('w_in', p_w_in)
    shard_update('pool_w', p_pool_w)
    shard_update('glu_w', p_glu)
    shard_update('w_branch_pool', p_wbp)
    shard_update('w_branch_ssm', p_wbs)
    shard_update('w_out', p_wout)

    return (tot32[5, 0], grad_x[None], *[grads[n] for n in WEIGHTS], *[deltas[n] for n in WEIGHTS],
            *[new_m[n] for n in WEIGHTS], *[new_v[n] for n in WEIGHTS])
```

```python
import math
from typing import Callable, NamedTuple, Optional

import jax
import jax.numpy as jnp
from jax import lax
from jax.experimental import pallas as pl
from jax.experimental.pallas import tpu as pltpu

F32 = jnp.float32
BF16 = jnp.bfloat16
MESH = pl.DeviceIdType.MESH

D = 1024
N_DEV = 8
N_IN = 6 * D
GROUPS = 64
G_H = 16
G_P = 64
N_Q = 4
Q_W = 2 * 16 * G_P
N_STATE = N_Q * Q_W
POOL_WINDOWS = (2, 4, 8, 16)
HALO = 16
RMS_EPS = 1e-6
SUBLANES = 8
LANE_CHUNK = 512
SCAN_UNROLL = 2
HEAD_PARTS = 1
VMEM_LIMIT = 56 * 1024 * 1024

ADAM_LR = 0.001
ADAM_B1 = 0.9
ADAM_B2 = 0.999
ADAM_EPS = 1e-08
ADAM_WD = 0.01
ADAM_STEP = 10

WEIGHTS = ['w_ada', 'b_ada', 'norm_pre', 'norm_post', 'w_in', 'pool_w', 'pool_scale', 'ssm_a_re',
           'ssm_a_im', 'ssm_log_dt', 'ssm_b_re', 'ssm_b_im', 'ssm_c_re', 'ssm_c_im', 'ssm_d', 'glu_w',
           'glu_b', 'w_branch_pool', 'w_branch_ssm', 'w_out']


def _pcall(body, **kw):
    return pl.pallas_call(body, **kw)


def _params(sem=None, vmem=VMEM_LIMIT):
    return pltpu.CompilerParams(dimension_semantics=sem, vmem_limit_bytes=vmem)


def _tb(rows, pref):
    return pref if rows % pref == 0 and rows // pref >= 2 else rows // 2


def _full(shape, single=False):
    nd = len(shape)
    if single:
        return pl.BlockSpec(shape, lambda i: (0,) * nd, pipeline_mode=pl.Buffered(1))
    return pl.BlockSpec(shape, lambda i: (0,) * nd)


ANY = pl.BlockSpec(memory_space=pl.ANY)


def _me():
    return lax.axis_index("x"), lax.axis_index("y"), lax.axis_index("c")


def _flat(p):
    return 4 * p[0] + 2 * p[1] + p[2]


def _peer(k):
    x, y, c = _me()
    return (1 - x if k & 4 else x, 1 - y if k & 2 else y, 1 - c if k & 1 else c)


def _silu_parts(z):
    s = jax.nn.sigmoid(z)
    return z * s, s * (1.0 + z * (1.0 - s))


_GELU_C = math.sqrt(2.0 / math.pi)


def _gelu_parts(x):
    x2 = x * x
    t = jnp.tanh(_GELU_C * (x + 0.044715 * x * x2))
    g = 0.5 * x * (1.0 + t)
    dg = 0.5 * (1.0 + t) + 0.5 * x * (1.0 - t * t) * (_GELU_C * (1.0 + 3.0 * 0.044715 * x2))
    return g, dg


def _dot(a, b):
    return jnp.dot(a, b, preferred_element_type=F32)


def _dot_nt(a, b):
    return lax.dot_general(a, b, (((1,), (1,)), ((), ())), preferred_element_type=F32)


def _dot_tn(a, b):
    return lax.dot_general(a, b, (((0,), (0,)), ((), ())), preferred_element_type=F32)


def _rms_parts(x):
    r = lax.rsqrt(jnp.mean(x * x, axis=-1, keepdims=True) + RMS_EPS)
    return x * r, r


def _rms_bwd(dxn, xn, r):
    return r * (dxn - xn * jnp.mean(dxn * xn, axis=-1, keepdims=True))


def _ada_exchange(c, w_ada_s, b_ada_s):
    cols = w_ada_s.shape[1]

    def body(c_ref, w_ref, b_ref, call_ref, mod_ref, part_ref, ssem, rsem, lsem):
        me3 = _me()
        me = _flat(me3)
        mine = pltpu.make_async_copy(c_ref, call_ref.at[pl.ds(me, 1), :], lsem.at[0])
        mine.start()
        sends = []
        for k in range(1, N_DEV):
            cp = pltpu.make_async_remote_copy(src_ref=c_ref, dst_ref=call_ref.at[pl.ds(me, 1), :],
                                              send_sem=ssem.at[k - 1], recv_sem=rsem.at[k - 1],
                                              device_id=_peer(k), device_id_type=MESH)
            cp.start()
            sends.append(cp)
        mine.wait()
        for k in range(1, N_DEV):
            p = _flat(_peer(k))
            pltpu.make_async_remote_copy(src_ref=c_ref, dst_ref=call_ref.at[pl.ds(p, 1), :],
                                         send_sem=ssem.at[k - 1], recv_sem=rsem.at[k - 1],
                                         device_id=_peer(k), device_id_type=MESH).wait_recv()
        for cp in sends:
            cp.wait_send()
        ca = call_ref[...]
        act = ca * jax.nn.sigmoid(ca)
        part_ref[...] = jnp.dot(act, w_ref[...], preferred_element_type=F32,
                                precision=lax.Precision.HIGHEST) + b_ref[...]
        own = pltpu.make_async_copy(part_ref.at[pl.ds(me, 1), :], mod_ref.at[pl.ds(me, 1), :], lsem.at[1])
        own.start()
        sends = []
        for k in range(1, N_DEV):
            p = _flat(_peer(k))
            s = N_DEV - 1 + k - 1
            cp = pltpu.make_async_remote_copy(src_ref=part_ref.at[pl.ds(p, 1), :],
                                              dst_ref=mod_ref.at[pl.ds(me, 1), :],
                                              send_sem=ssem.at[s], recv_sem=rsem.at[s],
                                              device_id=_peer(k), device_id_type=MESH)
            cp.start()
            sends.append(cp)
        own.wait()
        for k in range(1, N_DEV):
            p = _flat(_peer(k))
            s = N_DEV - 1 + k - 1
            pltpu.make_async_remote_copy(src_ref=part_ref.at[pl.ds(p, 1), :],
                                         dst_ref=mod_ref.at[pl.ds(p, 1), :],
                                         send_sem=ssem.at[s], recv_sem=rsem.at[s],
                                         device_id=_peer(k), device_id_type=MESH).wait_recv()
        for cp in sends:
            cp.wait_send()

    vm = pl.BlockSpec(memory_space=pltpu.VMEM)
    return _pcall(
        body, name="ada_exchange",
        out_shape=(jax.ShapeDtypeStruct((N_DEV, D), F32), jax.ShapeDtypeStruct((N_DEV, cols), F32)),
        in_specs=[vm, vm, vm], out_specs=(vm, vm),
        scratch_shapes=[pltpu.VMEM((N_DEV, cols), F32),
                        pltpu.SemaphoreType.DMA((2 * (N_DEV - 1),)),
                        pltpu.SemaphoreType.DMA((2 * (N_DEV - 1),)),
                        pltpu.SemaphoreType.DMA((2,))],
    )(c, w_ada_s, b_ada_s)


class _Item(NamedTuple):
    src: int
    out: int
    src_view: Callable
    dst_view: Callable
    pred: Optional[Callable] = None


def _when(pred, dest, fn):
    if pred is None:
        fn()
    else:
        pl.when(pred(dest))(fn)


def _n_sems(items):
    return len(items) * (N_DEV - 1)


def _hosted_copies(items, srcs, outs, ssem, rsem, lsem, act):
    me = _flat(_me())
    for t, it in enumerate(items):
        local = lambda t=t, it=it: pltpu.make_async_copy(
            it.src_view(srcs[it.src], me), it.dst_view(outs[it.out], me), lsem.at[t])
        if act == "start":
            _when(it.pred, me, lambda local=local: local().start())
        else:
            _when(it.pred, me, lambda local=local: local().wait())
    for k in range(1, N_DEV):
        p3 = _peer(k)
        p = _flat(p3)
        for t, it in enumerate(items):
            s = t * (N_DEV - 1) + k - 1
            send = lambda it=it, s=s, p=p, p3=p3: pltpu.make_async_remote_copy(
                src_ref=it.src_view(srcs[it.src], p), dst_ref=it.dst_view(outs[it.out], me),
                send_sem=ssem.at[s], recv_sem=rsem.at[s], device_id=p3, device_id_type=MESH)
            recv = lambda it=it, s=s, p=p, p3=p3: pltpu.make_async_remote_copy(
                src_ref=it.src_view(srcs[it.src], p), dst_ref=it.dst_view(outs[it.out], p),
                send_sem=ssem.at[s], recv_sem=rsem.at[s], device_id=p3, device_id_type=MESH)
            if act == "start":
                _when(it.pred, p, lambda send=send: send().start())
            else:
                _when(it.pred, me, lambda recv=recv: recv().wait_recv())
                _when(it.pred, p, lambda send=send: send().wait_send())


def _sem_scratch(items):
    return [pltpu.SemaphoreType.DMA((_n_sems(items),)), pltpu.SemaphoreType.DMA((_n_sems(items),)),
            pltpu.SemaphoreType.DMA((len(items),))]


def _whole(ref, dest):
    return ref


def _slot(ref, sender):
    return ref.at[sender]


def _rows_of(rows):
    return lambda ref, dev: ref.at[pl.ds(dev * rows, rows), :]


def _pool_rows_of(rows):
    return lambda ref, dev: ref.at[:, pl.ds(dev * rows, rows), :]


def _gather_item(src, out, dst_view):
    return _Item(src, out, _whole, dst_view)


def _scatter_item(src, out, src_view):
    return _Item(src, out, src_view, _slot)


W_IN_BLOCK = 256
W_IN_SHARD = N_IN // N_DEV
SSM_BLOCKS = (2 * D // W_IN_BLOCK, 4 * D // W_IN_BLOCK)


def _w_in_block_item(src, out, j, ssm_part):
    def block(dest):
        return (W_IN_SHARD // W_IN_BLOCK) * dest + j

    def in_ssm(dest):
        b = block(dest)
        return (b >= SSM_BLOCKS[0]) & (b < SSM_BLOCKS[1])

    def src_view(ref, dest):
        b = block(dest)
        local = b - SSM_BLOCKS[0] if ssm_part else jnp.where(b < SSM_BLOCKS[0], b, b - (SSM_BLOCKS[1] - SSM_BLOCKS[0]))
        local = jnp.clip(local, 0, ref.shape[1] // W_IN_BLOCK - 1)
        return ref.at[:, pl.ds(local * W_IN_BLOCK, W_IN_BLOCK)]

    def dst_view(ref, sender):
        return ref.at[sender, :, pl.ds(j * W_IN_BLOCK, W_IN_BLOCK)]

    pred = in_ssm if ssm_part else (lambda dest: jnp.logical_not(in_ssm(dest)))
    return _Item(src, out, src_view, dst_view, pred)


def _cast_shards(arrs):
    def body(*refs):
        n = len(refs) // 2
        for i in range(n):
            refs[n + i][...] = refs[i][...].astype(BF16)

    vm = pl.BlockSpec(memory_space=pltpu.VMEM)
    return _pcall(body, name="cast_shards",
                  out_shape=tuple(jax.ShapeDtypeStruct(a.shape, BF16) for a in arrs),
                  in_specs=[vm] * len(arrs), out_specs=tuple([vm] * len(arrs)),
                  compiler_params=_params())(*arrs)


def _s5_discretise(a_re, a_im, log_dt, b_re_t, b_im_t):
    dt = jnp.exp(log_dt)
    lam_re = jnp.minimum(a_re, -1e-4)
    lam_im = a_im
    mag = jnp.exp(lam_re * dt)
    abar_re = mag * jnp.cos(lam_im * dt)
    abar_im = mag * jnp.sin(lam_im * dt)
    den = lam_re * lam_re + lam_im * lam_im
    num_re = abar_re - 1.0
    f_re = (num_re * lam_re + abar_im * lam_im) / den
    f_im = (abar_im * lam_re - num_re * lam_im) / den
    f_re, f_im = f_re[:, None, :], f_im[:, None, :]
    bb_re = f_re * b_re_t - f_im * b_im_t
    bb_im = f_re * b_im_t + f_im * b_re_t
    return abar_re, abar_im, bb_re, bb_im


def _group_masks():
    spread = lax.broadcasted_iota(jnp.int32, (G_P, 16 * G_P), 1) % G_P == lax.broadcasted_iota(
        jnp.int32, (G_P, 16 * G_P), 0)
    own = lax.broadcasted_iota(jnp.int32, (16 * G_H, 16 * G_P), 0) // G_H == lax.broadcasted_iota(
        jnp.int32, (16 * G_H, 16 * G_P), 1) // G_P
    return spread, own


def _s5_prep(a_re, a_im, log_dt, b_re_t, b_im_t, c_re, c_im, n_pow):
    def body(ar_ref, ai_ref, ld_ref, br_ref, bi_ref, cr_ref, ci_ref, wb_ref, wct_ref, pr_ref, pi_ref):
        abar_re, abar_im, bb_re, bb_im = _s5_discretise(ar_ref[...], ai_ref[...], ld_ref[...], br_ref[...], bi_ref[...])
        spread, own = _group_masks()
        spread = spread.astype(BF16)
        for ref, parts in ((wb_ref, (bb_re, bb_im)), (wct_ref, (cr_ref[...], -ci_ref[...]))):
            for half, t in enumerate(parts):
                for q in range(N_Q):
                    blocks = t[q * 16:(q + 1) * 16].reshape(16 * G_H, G_P).astype(BF16)
                    dense = jnp.where(own, _dot(blocks, spread), 0.0)
                    ref[q, :, half * (Q_W // 2):(half + 1) * (Q_W // 2)] = dense.astype(BF16)
        p_re, p_im = abar_re, abar_im
        pr_ref[0] = p_re
        pi_ref[0] = p_im
        for k in range(1, n_pow):
            p_re, p_im = p_re * abar_re - p_im * abar_im, p_re * abar_im + p_im * abar_re
            pr_ref[k] = p_re
            pi_ref[k] = p_im

    vm = pl.BlockSpec(memory_space=pltpu.VMEM)
    return _pcall(body, name="s5_prep",
                  out_shape=(jax.ShapeDtypeStruct((N_Q, 16 * G_H, Q_W), BF16),
                             jax.ShapeDtypeStruct((N_Q, 16 * G_H, Q_W), BF16),
                             jax.ShapeDtypeStruct((n_pow, GROUPS, G_P), F32),
                             jax.ShapeDtypeStruct((n_pow, GROUPS, G_P), F32)),
                  in_specs=[vm] * 7, out_specs=(vm, vm, vm, vm), compiler_params=_params(),
                  )(a_re, a_im, log_dt, b_re_t, b_im_t, c_re, c_im)


def _s5_prep_bwd(a_re, a_im, log_dt, b_re_t, b_im_t, d_abar_re, d_abar_im, d_bb_re, d_bb_im):
    def body(ar_ref, ai_ref, ld_ref, br_ref, bi_ref, dar_ref, dai_ref, dbr_ref, dbi_ref,
             gar_ref, gai_ref, gld_ref, gbr_ref, gbi_ref):
        _, vjp = jax.vjp(_s5_discretise, ar_ref[...], ai_ref[...], ld_ref[...], br_ref[...], bi_ref[...])
        g = vjp((dar_ref[...], dai_ref[...], dbr_ref[...], dbi_ref[...]))
        gar_ref[...] = g[0]
        gai_ref[...] = g[1]
        gld_ref[...] = g[2]
        gbr_ref[...] = g[3]
        gbi_ref[...] = g[4]

    vm = pl.BlockSpec(memory_space=pltpu.VMEM)
    ins = (a_re, a_im, log_dt, b_re_t, b_im_t)
    return _pcall(body, name="s5_prep_bwd",
                  out_shape=tuple(jax.ShapeDtypeStruct(a.shape, F32) for a in ins),
                  in_specs=[vm] * 9, out_specs=tuple([vm] * 5), compiler_params=_params(),
                  )(*ins, d_abar_re, d_abar_im, d_bb_re, d_bb_im)


def _state_layout(re, im):
    lead = re.shape[:-2]
    r = re.reshape(lead + (N_Q, 1, 16 * G_P))
    i = im.reshape(lead + (N_Q, 1, 16 * G_P))
    return jnp.concatenate([r, i], axis=-2).reshape(lead + (N_STATE,))


def _state_unlayout(v):
    v4 = v.reshape(N_Q, 2, 16, G_P)
    return v4[:, 0].reshape(GROUPS, G_P), v4[:, 1].reshape(GROUPS, G_P)


def _perm_matrix(tb):
    k_steps = tb // SUBLANES
    r = jnp.arange(tb)
    src = (r % SUBLANES) * k_steps + r // SUBLANES
    return (src[:, None] == jnp.arange(tb)[None, :]).astype(BF16)


def _lane_chunks(q):
    for lc in range(Q_W // 2 // LANE_CHUNK):
        re = q * Q_W + lc * LANE_CHUNK
        yield re, re + Q_W // 2


def _steps(lo, hi, body, init):
    if hi - lo <= SCAN_UNROLL:
        for k in range(lo, hi):
            init = body(k, init)
        return init
    trips = (hi - lo) // SCAN_UNROLL

    def trip(j, carry):
        for u in range(SCAN_UNROLL):
            carry = body(lo + j * SCAN_UNROLL + u, carry)
        return carry

    carry = lax.fori_loop(0, trips, trip, init)
    for k in range(lo + trips * SCAN_UNROLL, hi):
        carry = body(k, carry)
    return carry


def _tile(k):
    if isinstance(k, int):
        return pl.ds(k * SUBLANES, SUBLANES)
    return pl.ds(pl.multiple_of(k * SUBLANES, SUBLANES), SUBLANES)


def _scan_forward(q, s_ref, p_ref, carry_ref, enter_ref, fin_ref, k_steps):
    for re, im in _lane_chunks(q):
        lr, li = pl.ds(re, LANE_CHUNK), pl.ds(im, LANE_CHUNK)
        a_re = jnp.broadcast_to(p_ref[0:1, lr], (SUBLANES, LANE_CHUNK))
        a_im = jnp.broadcast_to(p_ref[0:1, li], (SUBLANES, LANE_CHUNK))

        def local(k, st):
            sr, si = st
            rows = _tile(k)
            nr = a_re * sr - a_im * si + s_ref[rows, lr]
            ni = a_re * si + a_im * sr + s_ref[rows, li]
            s_ref[rows, lr] = nr
            s_ref[rows, li] = ni
            return nr, ni

        zero = jnp.zeros((SUBLANES, LANE_CHUNK), F32)
        fr, fi = _steps(0, k_steps, local, (zero, zero))
        fin_ref[:, lr] = fr
        fin_ref[:, li] = fi
        ak_re, ak_im = p_ref[k_steps - 1:k_steps, lr], p_ref[k_steps - 1:k_steps, li]
        c_re, c_im = carry_ref[:, lr], carry_ref[:, li]
        for seg in range(SUBLANES):
            enter_ref[seg:seg + 1, lr] = c_re
            enter_ref[seg:seg + 1, li] = c_im
            f_re, f_im = fin_ref[seg:seg + 1, lr], fin_ref[seg:seg + 1, li]
            c_re, c_im = f_re + ak_re * c_re - ak_im * c_im, f_im + ak_re * c_im + ak_im * c_re
        carry_ref[:, lr] = c_re
        carry_ref[:, li] = c_im
        e_re, e_im = enter_ref[:, lr], enter_ref[:, li]

        def fix(k, _):
            rows = _tile(k)
            p_re = p_ref[pl.ds(k, 1), lr]
            p_im = p_ref[pl.ds(k, 1), li]
            s_ref[rows, lr] = s_ref[rows, lr] + (p_re * e_re - p_im * e_im)
            s_ref[rows, li] = s_ref[rows, li] + (p_re * e_im + p_im * e_re)
            return 0

        _steps(0, k_steps, fix, 0)


def _scan_backward(q, g_ref, s_ref, p_ref, carry_ref, s_in_ref, fin_ref, da_ref, k_steps):
    seg_id = lax.broadcasted_iota(jnp.int32, (SUBLANES, LANE_CHUNK), 0)
    for re, im in _lane_chunks(q):
        lr, li = pl.ds(re, LANE_CHUNK), pl.ds(im, LANE_CHUNK)
        a_re = jnp.broadcast_to(p_ref[0:1, lr], (SUBLANES, LANE_CHUNK))
        a_im = jnp.broadcast_to(p_ref[0:1, li], (SUBLANES, LANE_CHUNK))

        def local(j, st):
            sr, si = st
            rows = _tile(k_steps - 1 - j)
            nr = a_re * sr + a_im * si + g_ref[rows, lr]
            ni = a_re * si - a_im * sr + g_ref[rows, li]
            g_ref[rows, lr] = nr
            g_ref[rows, li] = ni
            return nr, ni

        zero = jnp.zeros((SUBLANES, LANE_CHUNK), F32)
        fr, fi = _steps(0, k_steps, local, (zero, zero))
        fin_ref[:, lr] = fr
        fin_ref[:, li] = fi
        ak_re, ak_im = p_ref[k_steps - 1:k_steps, lr], p_ref[k_steps - 1:k_steps, li]
        c_re, c_im = carry_ref[:, lr], carry_ref[:, li]
        lam_in = [None] * SUBLANES
        for seg in reversed(range(SUBLANES)):
            lam_in[seg] = (c_re, c_im)
            f_re, f_im = fin_ref[seg:seg + 1, lr], fin_ref[seg:seg + 1, li]
            c_re, c_im = f_re + ak_re * c_re + ak_im * c_im, f_im + ak_re * c_im - ak_im * c_re
        carry_ref[:, lr] = c_re
        carry_ref[:, li] = c_im
        for seg in range(SUBLANES):
            fin_ref[seg:seg + 1, lr] = lam_in[seg][0]
            fin_ref[seg:seg + 1, li] = lam_in[seg][1]
        e_re, e_im = fin_ref[:, lr], fin_ref[:, li]

        def fix_with(k, acc, sp_re, sp_im):
            acc_re, acc_im = acc
            rows = _tile(k)
            p_re = p_ref[pl.ds(k_steps - 1 - k, 1), lr]
            p_im = p_ref[pl.ds(k_steps - 1 - k, 1), li]
            l_re = g_ref[rows, lr] + (p_re * e_re + p_im * e_im)
            l_im = g_ref[rows, li] + (p_re * e_im - p_im * e_re)
            g_ref[rows, lr] = l_re
            g_ref[rows, li] = l_im
            return acc_re + (l_re * sp_re + l_im * sp_im), acc_im + (l_im * sp_re - l_re * sp_im)

        def fix(k, acc):
            prev = _tile(k - 1)
            return fix_with(k, acc, s_ref[prev, lr], s_ref[prev, li])

        last = _tile(k_steps - 1)
        before_re = jnp.where(seg_id == 0, s_in_ref[:, lr], pltpu.roll(s_ref[last, lr], 1, axis=0))
        before_im = jnp.where(seg_id == 0, s_in_ref[:, li], pltpu.roll(s_ref[last, li], 1, axis=0))
        acc = fix_with(0, (zero, zero), before_re, before_im)
        acc_re, acc_im = _steps(1, k_steps, fix, acc)
        da_ref[:, lr] = da_ref[:, lr] + jnp.sum(acc_re, axis=0, keepdims=True)
        da_ref[:, li] = da_ref[:, li] + jnp.sum(acc_im, axis=0, keepdims=True)


def _prenorm(x, mod3, norm_pre):
    xn, r = _rms_parts(x)
    return xn, r, xn * norm_pre * (1.0 + mod3[1:2, :]) + mod3[0:1, :]


CHIP_FLIPS = (4, 2, 6)


def _shard_order(me):
    flips = [0, 1] + [f + c for f in CHIP_FLIPS for c in (0, 1)]
    return jnp.stack([me ^ f for f in flips]).astype(jnp.int32)


def _in_proj(x, mod3, norm_pre, w_in_s, shards):
    rows = x.shape[0]
    tb = _tb(rows, 2048)
    nblk = rows // tb
    n_sh = len(shards)
    last_step = N_DEV - 1
    items = [_gather_item(0, 0, _pool_rows_of(shards[0].shape[1]))] + \
            [_gather_item(t, t, _rows_of(shards[t].shape[0])) for t in range(1, n_sh)]

    def body(order_ref, x_ref, mod_ref, np_ref, w_src, *rest):
        src_refs, proj_ref, w_full, out_refs = rest[:n_sh], rest[n_sh], rest[n_sh + 1], rest[n_sh + 2:2 * n_sh + 2]
        h_scr, wg, ssem, rsem, lsem, *sems = rest[2 * n_sh + 2:]
        s, i = pl.program_id(0), pl.program_id(1)
        me3 = _me()
        me = _flat(me3)
        sibling = _peer(1)

        def own_copy(slot, k):
            return pltpu.make_async_remote_copy(src_ref=w_src, dst_ref=wg.at[me], send_sem=ssem.at[slot],
                                                recv_sem=rsem.at[slot], device_id=_peer(k), device_id_type=MESH)

        def passed_copy(j):
            p = _flat(_peer(CHIP_FLIPS[j]))
            return pltpu.make_async_remote_copy(src_ref=wg.at[p], dst_ref=wg.at[p], send_sem=ssem.at[4 + j],
                                                recv_sem=rsem.at[4 + j], device_id=sibling, device_id_type=MESH)

        def arrival(slot, flip):
            p = _flat(_peer(flip))
            pltpu.make_async_remote_copy(src_ref=w_src, dst_ref=wg.at[p], send_sem=ssem.at[slot],
                                         recv_sem=rsem.at[slot], device_id=sibling, device_id_type=MESH).wait_recv()

        def keep(t):
            p = order_ref[t]
            return pltpu.make_async_copy(wg.at[p], w_full.at[:, pl.ds(p * W_IN_SHARD, W_IN_SHARD)], lsem.at[1 + t])

        first = i == 0
        for t in range(last_step):
            pl.when(first & (s == t + 1))(lambda t=t: keep(t).start())

        @pl.when(first & (s == 0))
        def _():
            mine = pltpu.make_async_copy(w_src, wg.at[me], lsem.at[0])
            mine.start()
            own_copy(0, 1).start()
            for j, f in enumerate(CHIP_FLIPS[:2]):
                own_copy(1 + j, f).start()
            mine.wait()

        @pl.when(first & (s == 1))
        def _():
            arrival(0, 1)

        for j, f in enumerate(CHIP_FLIPS):
            @pl.when(first & (s == 2 + 2 * j))
            def _(j=j, f=f):
                arrival(1 + j, f)
                passed_copy(j).start()
                if j == 0:
                    own_copy(3, CHIP_FLIPS[2]).start()

            @pl.when(first & (s == 3 + 2 * j))
            def _(j=j, f=f):
                arrival(4 + j, f + 1)

        @pl.when(first & (s == last_step - 1))
        def _():
            _hosted_copies(items, src_refs, out_refs, *sems, act="start")

        rows_i = pl.ds(pl.multiple_of(i * tb, tb), tb)

        @pl.when(s == 0)
        def _():
            _, _, h = _prenorm(x_ref[...], mod_ref[...], np_ref[...])
            h_scr[rows_i, :] = h.astype(BF16)

        proj_ref[...] = _dot(h_scr[rows_i, :], wg[order_ref[s]]).astype(BF16)

        @pl.when((s == last_step) & (i == nblk - 1))
        def _():
            own_copy(0, 1).wait_send()
            for j, f in enumerate(CHIP_FLIPS):
                own_copy(1 + j, f).wait_send()
                passed_copy(j).wait_send()
            keep(last_step).start()
            for t in range(N_DEV):
                keep(t).wait()
            _hosted_copies(items, src_refs, out_refs, *sems, act="wait")

    full = [jax.ShapeDtypeStruct((4, 256, 256), BF16)] + [jax.ShapeDtypeStruct((D, D), BF16)] * (n_sh - 1)
    grid_spec = pltpu.PrefetchScalarGridSpec(
        num_scalar_prefetch=1, grid=(N_DEV, nblk),
        in_specs=[pl.BlockSpec((tb, D), lambda s, i, order: (jnp.where(s == 0, i, nblk - 1), 0)),
                  pl.BlockSpec((3, D), lambda s, i, order: (0, 0)), pl.BlockSpec((1, D), lambda s, i, order: (0, 0)),
                  ANY] + [ANY] * n_sh,
        out_specs=(pl.BlockSpec((tb, W_IN_SHARD), lambda s, i, order: (i, order[s])), ANY, *([ANY] * n_sh)),
        scratch_shapes=[pltpu.VMEM((rows, D), BF16), pltpu.VMEM((N_DEV, D, W_IN_SHARD), BF16),
                        pltpu.SemaphoreType.DMA((N_DEV - 1,)), pltpu.SemaphoreType.DMA((N_DEV - 1,)),
                        pltpu.SemaphoreType.DMA((1 + N_DEV,))] + _sem_scratch(items))
    return _pcall(body, name="in_proj", grid_spec=grid_spec,
                  out_shape=(jax.ShapeDtypeStruct((rows, N_IN), BF16), jax.ShapeDtypeStruct((D, N_IN), BF16), *full),
                  compiler_params=_params(("arbitrary", "arbitrary")),
                  )(_shard_order(_flat(_me())), x, mod3, norm_pre, w_in_s, *shards)


def _pool_windows(ext, tb, first_row):
    inv_counts = _inv_counts(tb, first_row)
    pooled = []
    for g, w in enumerate(POOL_WINDOWS):
        acc = ext[:, g * 256:(g + 1) * 256]
        tok = acc[HALO:, :]
        s = 1
        while s < w:
            acc = acc + pltpu.roll(acc, s, axis=0)
            s *= 2
        pooled.append(acc[HALO:, :] * inv_counts[g] - tok)
    return pooled, inv_counts


def _inv_counts(tb, first_row):
    pos = (first_row + lax.broadcasted_iota(jnp.int32, (tb, 1), 0) + 1).astype(F32)
    return [1.0 / jnp.minimum(pos, float(w)) for w in POOL_WINDOWS]


def _pool_fwd(proj, pool_w, pool_scale):
    rows = proj.shape[0]
    tb = _tb(rows, 512)
    hb = tb // HALO

    def body(u_ref, halo_ref, z_ref, pw_ref, ps_ref, y_ref, pooled_ref):
        i = pl.program_id(0)
        u = u_ref[...].astype(F32)
        halo = jnp.where(i > 0, halo_ref[...].astype(F32), 0.0)
        pooled, _ = _pool_windows(jnp.concatenate([halo, u], axis=0), tb, i * tb)
        silu_z, _ = _silu_parts(z_ref[...].astype(F32))
        for g in range(4):
            cols = slice(g * 256, (g + 1) * 256)
            pooled_b = pooled[g].astype(BF16)
            pooled_ref[:, cols] = pooled_b
            mixed = _dot(pooled_b, pw_ref[g])
            y_ref[:, cols] = (mixed * ps_ref[:, cols] * silu_z[:, cols]).astype(BF16)

    blk = pl.BlockSpec((tb, D), lambda i: (i, 0))
    return _pcall(body, name="pool_fwd", grid=(rows // tb,),
                  out_shape=(jax.ShapeDtypeStruct((rows, D), BF16), jax.ShapeDtypeStruct((rows, D), BF16)),
                  in_specs=[blk, pl.BlockSpec((HALO, D), lambda i: (jnp.maximum(i * hb - 1, 0), 0)),
                            pl.BlockSpec((tb, D), lambda i: (i, 1)),
                            _full((4, 256, 256)), _full((1, D))],
                  out_specs=(blk, blk),
                  compiler_params=_params(("arbitrary",)))(proj, proj, proj, pool_w, pool_scale)


def _ssm_fwd(proj, pm, pmt, wb, wct, ptab, dvec, glu_w, glu_b, shards):
    rows = proj.shape[0]
    tb = pm.shape[0]
    k_steps = tb // SUBLANES
    nblk = rows // tb
    n_sh = len(shards)
    items = [_gather_item(t, t, _rows_of(shards[t].shape[0])) for t in range(n_sh)]

    def body(u_ref, z_ref, pm_ref, pmt_ref, wb_ref, wct_ref, p_ref, d_ref, gw_ref, gb_ref, *rest):
        src_refs = rest[:n_sh]
        y_ref, ys_ref, carry_out_ref, s_ref, gate_ref, zp_ref, up_ref = rest[n_sh:n_sh + 7]
        out_refs = rest[n_sh + 7:2 * n_sh + 7]
        carry_ref, enter_ref, fin_ref, *sems = rest[2 * n_sh + 7:]

        @pl.when(pl.program_id(0) == 0)
        def _():
            _hosted_copies(items, src_refs, out_refs, *sems, act="start")
            carry_ref[...] = jnp.zeros_like(carry_ref)

        carry_out_ref[...] = carry_ref[...]
        up = _dot(pm_ref[...], u_ref[...]).astype(BF16)
        up_ref[...] = up

        for q in range(N_Q):
            s_ref[:, q * Q_W:(q + 1) * Q_W] = _dot(up[:, q * 256:(q + 1) * 256], wb_ref[q])
        for q in range(N_Q):
            _scan_forward(q, s_ref, p_ref, carry_ref, enter_ref, fin_ref, k_steps)
        for q in range(N_Q):
            cols = slice(q * 256, (q + 1) * 256)
            y = _dot_nt(s_ref[:, q * Q_W:(q + 1) * Q_W].astype(BF16), wct_ref[q])
            ys_ref[:, cols] = y + d_ref[:, cols] * up[:, cols].astype(F32)
        yg, _ = _gelu_parts(ys_ref[...])
        gate = jax.nn.sigmoid(_dot(yg.astype(BF16), gw_ref[...]) + gb_ref[...])
        gate_ref[...] = gate
        zp = _dot(pm_ref[...], z_ref[...])
        zp_ref[...] = zp.astype(BF16)
        silu_z, _ = _silu_parts(zp)
        y_ref[...] = _dot(pmt_ref[...], (yg * gate * silu_z).astype(BF16)).astype(BF16)

        @pl.when(pl.program_id(0) == nblk - 1)
        def _():
            _hosted_copies(items, src_refs, out_refs, *sems, act="wait")

    return _pcall(body, name="ssm_fwd", grid=(nblk,),
                  out_shape=(jax.ShapeDtypeStruct((rows, D), BF16), jax.ShapeDtypeStruct((rows, D), F32),
                             jax.ShapeDtypeStruct((nblk, 1, N_STATE), F32),
                             jax.ShapeDtypeStruct((rows, N_STATE), F32),
                             jax.ShapeDtypeStruct((rows, D), F32), jax.ShapeDtypeStruct((rows, D), BF16),
                             jax.ShapeDtypeStruct((rows, D), BF16),
                             *[jax.ShapeDtypeStruct((D, D), BF16)] * n_sh),
                  in_specs=[pl.BlockSpec((tb, D), lambda i: (i, 2)), pl.BlockSpec((tb, D), lambda i: (i, 3)),
                            _full((tb, tb)), _full((tb, tb)),
                            _full((N_Q, 256, Q_W), single=True), _full((N_Q, 256, Q_W), single=True),
                            _full((k_steps, N_STATE)), _full((1, D)), _full((D, D), single=True), _full((1, D))] +
                           [ANY] * n_sh,
                  out_specs=(pl.BlockSpec((tb, D), lambda i: (i, 0)), pl.BlockSpec((tb, D), lambda i: (i, 0)),
                             pl.BlockSpec((None, 1, N_STATE), lambda i: (i, 0, 0)),
                             pl.BlockSpec((tb, N_STATE), lambda i: (i, 0)),
                             *[pl.BlockSpec((tb, D), lambda i: (i, 0))] * 3, *([ANY] * n_sh)),
                  scratch_shapes=[pltpu.VMEM((1, N_STATE), F32),
                                  pltpu.VMEM((SUBLANES, N_STATE), F32), pltpu.VMEM((SUBLANES, N_STATE), F32)] +
                                 _sem_scratch(items),
                  compiler_params=_params(("arbitrary",)))(proj, proj, pm, pmt, wb, wct, ptab, dvec, glu_w, glu_b,
                                                           *shards)


def _head(x, target, proj, y_pool, y_ssm, mod3, norm_post, wbp, wbs, wout):
    rows = x.shape[0]
    tb = _tb(rows, 256)
    nblk = rows // tb
    n_feat = float(D)

    def body(x_ref, t_ref, gp_ref, gs_ref, yp_ref, ys_ref, mod_ref, npost_ref, wbp_ref, wbs_ref, wout_ref,
             loss_ref, dy_ref, dyp_ref, dys_ref, dg_ref, dwbp_hbm, dwbs_hbm, dwout_hbm, vec_ref,
             acc_bp, acc_bs, acc_out, acc_loss, acc_vec):
        i = pl.program_id(0)

        @pl.when(i == 0)
        def _():
            acc_bp[...] = jnp.zeros_like(acc_bp)
            acc_bs[...] = jnp.zeros_like(acc_bs)
            acc_out[...] = jnp.zeros_like(acc_out)
            acc_loss[...] = jnp.zeros_like(acc_loss)
            acc_vec[...] = jnp.zeros_like(acc_vec)

        gate = mod_ref[2:3, :]
        npost = npost_ref[...]
        keep = []
        for part in range(HEAD_PARTS):
            rs = slice(part * tb // HEAD_PARTS, (part + 1) * tb // HEAD_PARTS)
            yp, ys = yp_ref[rs, :], ys_ref[rs, :]
            sgp = jax.nn.sigmoid(gp_ref[rs, :].astype(F32))
            sgs = jax.nn.sigmoid(gs_ref[rs, :].astype(F32))
            pb = _dot(yp, wbp_ref[...])
            psm = _dot(ys, wbs_ref[...])
            mb = (sgp * pb + sgs * psm).astype(BF16)
            out = _dot(mb, wout_ref[...])
            on, r = _rms_parts(out)
            normed = on * npost
            diff = x_ref[rs, :] + gate * normed - t_ref[rs, :]
            acc_loss[...] += jnp.sum(diff * diff, axis=0, keepdims=True)
            dy = diff * (1.0 / n_feat)
            dy_ref[rs, :] = dy
            acc_vec[0:1, :] += jnp.sum(dy * normed, axis=0, keepdims=True)
            dn = dy * gate
            acc_vec[1:2, :] += jnp.sum(dn * on, axis=0, keepdims=True)
            dout = _rms_bwd(dn * npost, on, r).astype(BF16)
            dm = _dot_nt(dout, wout_ref[...])
            dpb = (dm * sgp).astype(BF16)
            dps = (dm * sgs).astype(BF16)
            dg_ref[rs, :D] = (dm * pb * sgp * (1.0 - sgp)).astype(BF16)
            dg_ref[rs, D:] = (dm * psm * sgs * (1.0 - sgs)).astype(BF16)
            dyp_ref[rs, :] = _dot_nt(dpb, wbp_ref[...]).astype(BF16)
            dys_ref[rs, :] = _dot_nt(dps, wbs_ref[...]).astype(BF16)
            keep.append((mb, dout, dpb, dps))
        cat = lambda k: jnp.concatenate([p[k] for p in keep], axis=0) if HEAD_PARTS > 1 else keep[0][k]
        acc_out[...] += _dot_tn(cat(0), cat(1))
        acc_bp[...] += _dot_tn(yp_ref[...], cat(2))
        acc_bs[...] += _dot_tn(ys_ref[...], cat(3))

        @pl.when(i == nblk - 1)
        def _():
            loss_ref[...] = 0.5 / n_feat * jnp.sum(acc_loss[...], axis=1, keepdims=True)
            vec_ref[...] = acc_vec[...]
            pltpu.sync_copy(acc_bp, dwbp_hbm)
            pltpu.sync_copy(acc_bs, dwbs_hbm)
            pltpu.sync_copy(acc_out, dwout_hbm)

    row = lambda c: pl.BlockSpec((tb, D), lambda i: (i, c))
    w = _full((D, D), single=True)
    return _pcall(body, name="head", grid=(nblk,),
                  out_shape=(jax.ShapeDtypeStruct((1, 1), F32), jax.ShapeDtypeStruct((rows, D), F32),
                             jax.ShapeDtypeStruct((rows, D), BF16), jax.ShapeDtypeStruct((rows, D), BF16),
                             jax.ShapeDtypeStruct((rows, 2 * D), BF16),
                             jax.ShapeDtypeStruct((D, D), F32), jax.ShapeDtypeStruct((D, D), F32),
                             jax.ShapeDtypeStruct((D, D), F32), jax.ShapeDtypeStruct((2, D), F32)),
                  in_specs=[row(0), row(0), row(4), row(5), row(0), row(0), _full((3, D)), _full((1, D)), w, w, w],
                  out_specs=(_full((1, 1)), row(0), row(0), row(0), pl.BlockSpec((tb, 2 * D), lambda i: (i, 0)),
                             ANY, ANY, ANY, _full((2, D))),
                  scratch_shapes=[pltpu.VMEM((D, D), F32), pltpu.VMEM((D, D), F32), pltpu.VMEM((D, D), F32),
                                  pltpu.VMEM((1, D), F32), pltpu.VMEM((2, D), F32)],
                  compiler_params=_params(("arbitrary",)))(x, target, proj, proj, y_pool, y_ssm, mod3, norm_post,
                                                           wbp, wbs, wout)


def _glu_bwd(dys, zp, ys_pre, gate, pm, pmt, glu_w):
    rows = dys.shape[0]
    pb = pm.shape[0]
    per_step = 2 if rows % (2 * pb) == 0 and rows // pb >= 4 else 1
    tb = per_step * pb
    nblk = rows // tb
    parts = [slice(j * pb, (j + 1) * pb) for j in range(per_step)]

    def body(dys_ref, z_ref, ysp_ref, sg_ref, pm_ref, pmt_ref, gw_ref, dyp_ref, dz_ref, dgw_hbm, dgb_ref,
             acc_w, acc_b):
        i = pl.program_id(0)

        @pl.when(i == 0)
        def _():
            acc_w[...] = jnp.zeros_like(acc_w)
            acc_b[...] = jnp.zeros_like(acc_b)

        d_out = jnp.concatenate([_dot(pm_ref[...], dys_ref[rs, :]) for rs in parts], axis=0)
        yg, dgelu = _gelu_parts(ysp_ref[...])
        ygb = yg.astype(BF16)
        sg = sg_ref[...]
        silu_z, dsilu_z = _silu_parts(z_ref[...].astype(F32))
        dz = (d_out * (yg * sg) * dsilu_z).astype(BF16)
        for rs in parts:
            dz_ref[rs, :] = _dot(pmt_ref[...], dz[rs, :]).astype(BF16)
        dglu = d_out * silu_z
        dq = dglu * yg * sg * (1.0 - sg)
        dqb = dq.astype(BF16)
        acc_b[...] += jnp.sum(dq, axis=0, keepdims=True)
        acc_w[...] += _dot_tn(ygb, dqb)
        dyg = dglu * sg + _dot_nt(dqb, gw_ref[...])
        dyp_ref[...] = (dyg * dgelu).astype(BF16)

        @pl.when(i == nblk - 1)
        def _():
            dgb_ref[...] = acc_b[...]
            pltpu.sync_copy(acc_w, dgw_hbm)

    row = lambda c: pl.BlockSpec((tb, D), lambda i: (i, c))
    return _pcall(body, name="glu_bwd", grid=(nblk,),
                  out_shape=(jax.ShapeDtypeStruct((rows, D), BF16), jax.ShapeDtypeStruct((rows, D), BF16),
                             jax.ShapeDtypeStruct((D, D), F32), jax.ShapeDtypeStruct((1, D), F32)),
                  in_specs=[row(0), row(0), row(0), row(0), _full((pb, pb)), _full((pb, pb)),
                            _full((D, D), single=True)],
                  out_specs=(row(0), row(0), ANY, _full((1, D))),
                  scratch_shapes=[pltpu.VMEM((D, D), F32), pltpu.VMEM((1, D), F32)],
                  compiler_params=_params(("arbitrary",)))(dys, zp, ys_pre, gate, pm, pmt, glu_w)


def _ssm_bwd(dyp, up, states, carries, pmt, wb, wct, ptab, dvec, mat_grads, dpool_w, dw_in_rest):
    rows = dyp.shape[0]
    tb = pmt.shape[0]
    k_steps = tb // SUBLANES
    nblk = rows // tb
    n_mat = len(mat_grads)
    hosted = [*mat_grads, dpool_w, dw_in_rest]
    n_h = len(hosted)
    shard_rows = D // N_DEV
    pool_rows = dpool_w.shape[1] // N_DEV
    items = [_scatter_item(t, t, _rows_of(shard_rows)) for t in range(n_mat)] + \
            [_scatter_item(n_mat, n_mat, _pool_rows_of(pool_rows))] + \
            [_w_in_block_item(n_mat + 1, n_mat + 1, j, ssm_part=False) for j in range(W_IN_SHARD // W_IN_BLOCK)]
    n_in, n_out = 9, 5

    def body(*refs):
        dyp_ref, u_ref, s_ref, cin_ref, pmt_ref, wb_ref, wct_ref, p_ref, d_ref = refs[:n_in]
        src_refs = refs[n_in:n_in + n_h]
        du_ref, dbb_ref, dcc_ref, da_ref, dd_ref = refs[n_in + n_h:n_in + n_h + n_out]
        recv_refs = refs[n_in + n_h + n_out:n_in + 2 * n_h + n_out]
        (g_ref, carry_b, fin_ref, acc_wb, acc_wct, acc_da, acc_dd, dup_ref,
         *sems) = refs[n_in + 2 * n_h + n_out:]
        i = pl.program_id(0)

        @pl.when(i == 0)
        def _():
            _hosted_copies(items, src_refs, recv_refs, *sems, act="start")
            carry_b[...] = jnp.zeros_like(carry_b)
            acc_wb[...] = jnp.zeros_like(acc_wb)
            acc_wct[...] = jnp.zeros_like(acc_wct)
            acc_da[...] = jnp.zeros_like(acc_da)
            acc_dd[...] = jnp.zeros_like(acc_dd)

        def keep_own(acc, q, prod):
            for gl in range(16):
                r, c = slice(gl * G_H, (gl + 1) * G_H), (gl // 2) * 128
                acc[q, r, 0:128] += prod[r, c:c + 128]
                acc[q, r, 128:256] += prod[r, Q_W // 2 + c:Q_W // 2 + c + 128]

        dy = dyp_ref[...]
        up = u_ref[...]
        acc_dd[...] += jnp.sum(dy.astype(F32) * up.astype(F32), axis=0, keepdims=True)
        for q in range(N_Q):
            cols = slice(q * 256, (q + 1) * 256)
            g_ref[:, q * Q_W:(q + 1) * Q_W] = _dot(dy[:, cols], wct_ref[q])
            keep_own(acc_wct, q, _dot_tn(dy[:, cols], s_ref[:, q * Q_W:(q + 1) * Q_W].astype(BF16)))
        for q in range(N_Q):
            _scan_backward(q, g_ref, s_ref, p_ref, carry_b, cin_ref, fin_ref, acc_da, k_steps)
        for q in range(N_Q):
            cols = slice(q * 256, (q + 1) * 256)
            lam = g_ref[:, q * Q_W:(q + 1) * Q_W].astype(BF16)
            keep_own(acc_wb, q, _dot_tn(up[:, cols], lam))
            dup_ref[:, cols] = (_dot_nt(lam, wb_ref[q]) + d_ref[:, cols] * dy[:, cols].astype(F32)).astype(BF16)
        du_ref[...] = _dot(pmt_ref[...], dup_ref[...]).astype(BF16)

        @pl.when(i == nblk - 1)
        def _():
            da_ref[...] = acc_da[...]
            dd_ref[...] = acc_dd[...]
            lane = lax.broadcasted_iota(jnp.int32, (16 * G_H, 128), 1)
            row = lax.broadcasted_iota(jnp.int32, (16 * G_H, 128), 0)
            own = lane // G_P == (row // G_H) % 2
            spread = (lax.broadcasted_iota(jnp.int32, (G_P, 128), 1) % G_P ==
                      lax.broadcasted_iota(jnp.int32, (G_P, 128), 0)).astype(F32)
            for acc, out in ((acc_wb, dbb_ref), (acc_wct, dcc_ref)):
                for half in range(2):
                    for q in range(N_Q):
                        kept = jnp.where(own, acc[q, :, half * 128:(half + 1) * 128], 0.0)
                        out[half, q] = lax.dot_general(kept, spread, (((1,), (1,)), ((), ())),
                                                       preferred_element_type=F32, precision=lax.Precision.HIGHEST)
            _hosted_copies(items, src_refs, recv_refs, *sems, act="wait")

    rev = lambda c: pl.BlockSpec((tb, D), lambda i: (nblk - 1 - i, c))
    recv = [jax.ShapeDtypeStruct((N_DEV, shard_rows, D), F32)] * n_mat + \
           [jax.ShapeDtypeStruct((N_DEV, dpool_w.shape[0], pool_rows, dpool_w.shape[2]), F32),
            jax.ShapeDtypeStruct((N_DEV, D, W_IN_SHARD), BF16)]
    return _pcall(body, name="ssm_bwd", grid=(nblk,),
                  out_shape=(jax.ShapeDtypeStruct((rows, D), BF16),
                             jax.ShapeDtypeStruct((2, N_Q, 16 * G_H, G_P), F32),
                             jax.ShapeDtypeStruct((2, N_Q, 16 * G_H, G_P), F32),
                             jax.ShapeDtypeStruct((1, N_STATE), F32), jax.ShapeDtypeStruct((1, D), F32), *recv),
                  in_specs=[rev(0), rev(0), pl.BlockSpec((tb, N_STATE), lambda i: (nblk - 1 - i, 0)),
                            pl.BlockSpec((None, 1, N_STATE), lambda i: (nblk - 1 - i, 0, 0)),
                            _full((tb, tb)),
                            _full((N_Q, 256, Q_W), single=True), _full((N_Q, 256, Q_W), single=True),
                            _full((k_steps, N_STATE)), _full((1, D))] + [ANY] * n_h,
                  out_specs=(rev(0), _full((2, N_Q, 16 * G_H, G_P)), _full((2, N_Q, 16 * G_H, G_P)),
                             _full((1, N_STATE)), _full((1, D)), *([ANY] * n_h)),
                  scratch_shapes=[pltpu.VMEM((tb, N_STATE), F32), pltpu.VMEM((1, N_STATE), F32),
                                  pltpu.VMEM((SUBLANES, N_STATE), F32),
                                  pltpu.VMEM((N_Q, 16 * G_H, 256), F32), pltpu.VMEM((N_Q, 16 * G_H, 256), F32),
                                  pltpu.VMEM((1, N_STATE), F32), pltpu.VMEM((1, D), F32),
                                  pltpu.VMEM((tb, D), BF16)] + _sem_scratch(items),
                  compiler_params=_params(("arbitrary",), vmem=60 * 1024 * 1024),
                  )(dyp, up, states, carries, pmt, wb, wct, ptab, dvec, *hosted)


def _pool_bwd(dyp, pooled, proj, pool_w, pool_scale):
    rows = dyp.shape[0]
    tb = _tb(rows, 512)
    nblk = rows // tb

    def body(dy_ref, pooled_ref, z_ref, pw_ref, ps_ref, dp_ref, dpw_ref, dps_ref, ahead_ref):
        i = pl.program_id(0)
        blk = nblk - 1 - i

        @pl.when(i == 0)
        def _():
            ahead_ref[...] = jnp.zeros_like(ahead_ref)
            dpw_ref[...] = jnp.zeros_like(dpw_ref)
            dps_ref[...] = jnp.zeros_like(dps_ref)

        inv_counts = _inv_counts(tb, blk * tb)
        silu_z, dsilu_z = _silu_parts(z_ref[...].astype(F32))
        dy = dy_ref[...].astype(F32)
        for g, w in enumerate(POOL_WINDOWS):
            cols = slice(g * 256, (g + 1) * 256)
            pooled_b = pooled_ref[:, cols]
            mixed = _dot(pooled_b, pw_ref[g])
            scale = ps_ref[:, cols]
            dp_ref[:, D + g * 256:D + (g + 1) * 256] = (dy[:, cols] * (mixed * scale) * dsilu_z[:, cols]).astype(BF16)
            dms = dy[:, cols] * silu_z[:, cols]
            dps_ref[:, cols] += jnp.sum(dms * mixed, axis=0, keepdims=True)
            dmixed = (dms * scale).astype(BF16)
            dpw_ref[g] += _dot_tn(pooled_b, dmixed)
            dpooled = _dot_nt(dmixed, pw_ref[g])
            ratio = dpooled * inv_counts[g]
            acc = jnp.concatenate([ratio, ahead_ref[:, cols]], axis=0)
            ahead_ref[:, cols] = ratio[:HALO, :]
            s = 1
            while s < w:
                acc = acc + pltpu.roll(acc, tb + HALO - s, axis=0)
                s *= 2
            dp_ref[:, cols] = (acc[:tb, :] - dpooled).astype(BF16)

    rev = lambda c: pl.BlockSpec((tb, D), lambda i: (nblk - 1 - i, c))
    return _pcall(body, name="pool_bwd", grid=(nblk,),
                  out_shape=(jax.ShapeDtypeStruct((rows, 2 * D), BF16), jax.ShapeDtypeStruct((4, 256, 256), F32),
                             jax.ShapeDtypeStruct((1, D), F32)),
                  in_specs=[rev(0), rev(0), rev(1), _full((4, 256, 256)), _full((1, D))],
                  out_specs=(pl.BlockSpec((tb, 2 * D), lambda i: (nblk - 1 - i, 0)), _full((4, 256, 256)),
                             _full((1, D))),
                  scratch_shapes=[pltpu.VMEM((HALO, D), F32)],
                  compiler_params=_params(("arbitrary",)))(dyp, pooled, proj, pool_w, pool_scale)


def _dproj_specs(tb):
    return [pl.BlockSpec((tb, 2 * D), lambda i: (i, 0)), pl.BlockSpec((tb, D), lambda i: (i, 0)),
            pl.BlockSpec((tb, D), lambda i: (i, 0)), pl.BlockSpec((tb, 2 * D), lambda i: (i, 0))]


def _in_proj_bwd_x(x, dy, dpp, dus, dzs, dpg, mod3, norm_pre, w_in, dw_in_ssm, recv_w_in):
    rows = x.shape[0]
    tb = _tb(rows, 512)
    nblk = rows // tb
    items = [_w_in_block_item(0, 0, j, ssm_part=True) for j in range(W_IN_SHARD // W_IN_BLOCK)]
    sums_item = [_Item(0, 0, _whole, _slot)]

    def body(x_ref, dy_ref, dpp_ref, dus_ref, dzs_ref, dpg_ref, mod_ref, np_ref, w_ref,
             dw_src, _, gx_ref, recv_w, recv_sums, vec_ref, ssem, rsem, lsem, *sums_sems):
        src_refs, recv_refs, sems = (dw_src,), (recv_w,), (ssem, rsem, lsem)

        @pl.when(pl.program_id(0) == 0)
        def _():
            _hosted_copies(items, src_refs, recv_refs, *sems, act="start")
            vec_ref[...] = jnp.zeros_like(vec_ref)

        dh = _dot_nt(dpp_ref[...], w_ref[:, 0:2 * D])
        dh += _dot_nt(dus_ref[...], w_ref[:, 2 * D:3 * D])
        dh += _dot_nt(dzs_ref[...], w_ref[:, 3 * D:4 * D])
        dh += _dot_nt(dpg_ref[...], w_ref[:, 4 * D:6 * D])
        xn, r, _ = _prenorm(x_ref[...], mod_ref[...], np_ref[...])
        one_scale = 1.0 + mod_ref[1:2, :]
        vec_ref[0:1, :] += jnp.sum(dh, axis=0, keepdims=True)
        vec_ref[1:2, :] += jnp.sum(dh * xn, axis=0, keepdims=True) * np_ref[...]
        vec_ref[2:3, :] += jnp.sum(dh * xn, axis=0, keepdims=True) * one_scale
        gx_ref[...] = dy_ref[...] + _rms_bwd(dh * (np_ref[...] * one_scale), xn, r)

        @pl.when(pl.program_id(0) == nblk - 1)
        def _():
            _hosted_copies(sums_item, (vec_ref,), (recv_sums,), *sums_sems, act="start")
            _hosted_copies(items, src_refs, recv_refs, *sems, act="wait")
            _hosted_copies(sums_item, (vec_ref,), (recv_sums,), *sums_sems, act="wait")

    row = pl.BlockSpec((tb, D), lambda i: (i, 0))
    recv = (jax.ShapeDtypeStruct(recv_w_in.shape, recv_w_in.dtype), jax.ShapeDtypeStruct((N_DEV, 3, D), F32))
    return _pcall(body, name="in_proj_bwd_x", grid=(nblk,),
                  out_shape=(jax.ShapeDtypeStruct((rows, D), F32), *recv),
                  in_specs=[row, row] + _dproj_specs(tb) + [_full((3, D)), _full((1, D)),
                                                            _full((D, N_IN), single=True)] + [ANY] * 2,
                  out_specs=(row, ANY, ANY),
                  input_output_aliases={10: 1},
                  scratch_shapes=[pltpu.VMEM((3, D), F32)] + _sem_scratch(items) + _sem_scratch(sums_item),
                  compiler_params=_params(("arbitrary",)))(x, dy, dpp, dus, dzs, dpg, mod3, norm_pre, w_in,
                                                           dw_in_ssm, recv_w_in)


def _in_proj_bwd_w(name, x, dparts, mod3, norm_pre, gathered=()):
    rows = x.shape[0]
    tb = _tb(rows, 512)
    nblk = rows // tb
    widths = [p.shape[1] for p in dparts]
    n_p, n_g = len(dparts), len(gathered)
    items = [_Item(t, t, _whole, _slot) for t in range(n_g)]

    def body(x_ref, *rest):
        part_refs, (mod_ref, np_ref) = rest[:n_p], rest[n_p:n_p + 2]
        src_refs, dw_ref = rest[n_p + 2:n_p + 2 + n_g], rest[n_p + 2 + n_g]
        recv_refs, (acc, *sems) = rest[n_p + 3 + n_g:n_p + 3 + 2 * n_g], rest[n_p + 3 + 2 * n_g:]
        i = pl.program_id(0)

        @pl.when(i == 0)
        def _():
            if n_g:
                _hosted_copies(items, src_refs, recv_refs, *sems, act="start")
            acc[...] = jnp.zeros_like(acc)

        _, _, h = _prenorm(x_ref[...], mod_ref[...], np_ref[...])
        ht = h.astype(BF16)
        lo = 0
        for ref, w in zip(part_refs, widths):
            acc[:, lo:lo + w] += _dot_tn(ht, ref[...])
            lo += w

        @pl.when(i == nblk - 1)
        def _():
            dw_ref[...] = acc[...].astype(BF16)
            if n_g:
                _hosted_copies(items, src_refs, recv_refs, *sems, act="wait")

    row = pl.BlockSpec((tb, D), lambda i: (i, 0))
    out = _pcall(body, name=name, grid=(nblk,),
                 out_shape=(jax.ShapeDtypeStruct((D, sum(widths)), BF16),
                            *[jax.ShapeDtypeStruct((N_DEV,) + g.shape, g.dtype) for g in gathered]),
                 in_specs=[row] + [pl.BlockSpec((tb, w), lambda i: (i, 0)) for w in widths] +
                          [_full((3, D)), _full((1, D))] + [ANY] * n_g,
                 out_specs=(_full((D, sum(widths))), *([ANY] * n_g)),
                 scratch_shapes=[pltpu.VMEM((D, sum(widths)), F32)] + (_sem_scratch(items) if n_g else []),
                 compiler_params=_params(("arbitrary",)))(x, *dparts, mod3, norm_pre, *gathered)
    return out if n_g else out[0]


def _adamw_math(w, g, m, v):
    m = ADAM_B1 * m + (1.0 - ADAM_B1) * g
    v = ADAM_B2 * v + (1.0 - ADAM_B2) * (g * g)
    m_hat = m / (1.0 - ADAM_B1 ** ADAM_STEP)
    v_hat = v / (1.0 - ADAM_B2 ** ADAM_STEP)
    delta = -ADAM_LR * (m_hat / (jnp.sqrt(v_hat) + ADAM_EPS) + ADAM_WD * w)
    return delta, m, v


def _sum_sources(ref):
    g = ref[0].astype(F32)
    for s in range(1, N_DEV):
        g = g + ref[s].astype(F32)
    return g


def _adamw_reduce(name, parts, w, m, v):
    r, c = w.shape
    tr = r if r * c <= 256 * 1024 else max(8, (256 * 1024 // c) // 8 * 8)
    while r % tr:
        tr -= 8

    def body(p_ref, w_ref, m_ref, v_ref, g_ref, d_ref, nm_ref, nv_ref):
        g = _sum_sources(p_ref)
        g_ref[...] = g
        d_ref[...], nm_ref[...], nv_ref[...] = _adamw_math(w_ref[...], g, m_ref[...], v_ref[...])

    blk = pl.BlockSpec((tr, c), lambda i: (i, 0))
    return _pcall(body, name=name, grid=(r // tr,),
                  out_shape=tuple([jax.ShapeDtypeStruct((r, c), F32)] * 4),
                  in_specs=[pl.BlockSpec((N_DEV, tr, c), lambda i: (0, i, 0)), blk, blk, blk],
                  out_specs=(blk, blk, blk, blk),
                  compiler_params=_params(("arbitrary",)))(parts, w, m, v)


def _adamw_small(gs, ws, ms, vs):
    n = len(gs)

    def body(*refs):
        ins, outs = refs[:4 * n], refs[4 * n:]
        for t in range(n):
            g_ref, w_ref, m_ref, v_ref = ins[4 * t:4 * t + 4]
            outs[3 * t][...], outs[3 * t + 1][...], outs[3 * t + 2][...] = _adamw_math(
                w_ref[...], g_ref[...], m_ref[...], v_ref[...])

    vm = pl.BlockSpec(memory_space=pltpu.VMEM)
    flat = [a for t in range(n) for a in (gs[t], ws[t], ms[t], vs[t])]
    return _pcall(body, name="adamw_small",
                  out_shape=tuple(jax.ShapeDtypeStruct(w.shape, F32) for w in ws for _ in range(3)),
                  in_specs=[vm] * (4 * n), out_specs=tuple([vm] * (3 * n)), compiler_params=_params())(*flat)


def _sum_small(parts):
    n = len(parts)

    def body(*refs):
        for t in range(n):
            refs[n + t][...] = _sum_sources(refs[t])

    vm = pl.BlockSpec(memory_space=pltpu.VMEM)
    return _pcall(body, name="sum_small",
                  out_shape=tuple(jax.ShapeDtypeStruct(p.shape[1:], F32) for p in parts),
                  in_specs=[vm] * n, out_specs=tuple([vm] * n), compiler_params=_params())(*parts)


def _ada_update(c_all, dmod_cols, w, m, v):
    def body(c_ref, dm_ref, w_ref, m_ref, v_ref, g_ref, d_ref, nm_ref, nv_ref):
        ca = c_ref[...]
        g = lax.dot_general(ca * jax.nn.sigmoid(ca), dm_ref[...], (((0,), (0,)), ((), ())),
                            preferred_element_type=F32, precision=lax.Precision.HIGHEST)
        g_ref[...] = g
        d_ref[...], nm_ref[...], nv_ref[...] = _adamw_math(w_ref[...], g, m_ref[...], v_ref[...])

    vm = pl.BlockSpec(memory_space=pltpu.VMEM)
    return _pcall(body, name="ada_update", out_shape=tuple([jax.ShapeDtypeStruct(w.shape, F32)] * 4),
                  in_specs=[vm] * 5, out_specs=(vm, vm, vm, vm), compiler_params=_params())(c_all, dmod_cols, w, m, v)


def kernel(x, c, w_ada, b_ada, norm_pre, norm_post, w_in, pool_w, pool_scale, ssm_a_re, ssm_a_im, ssm_log_dt, ssm_b_re, ssm_b_im, ssm_c_re, ssm_c_im, ssm_d, glu_w, glu_b, w_branch_pool, w_branch_ssm, w_out, loss_target, m_w_ada, m_b_ada, m_norm_pre, m_norm_post, m_w_in, m_pool_w, m_pool_scale, m_ssm_a_re, m_ssm_a_im, m_ssm_log_dt, m_ssm_b_re, m_ssm_b_im, m_ssm_c_re, m_ssm_c_im, m_ssm_d, m_glu_w, m_glu_b, m_w_branch_pool, m_w_branch_ssm, m_w_out, v_w_ada, v_b_ada, v_norm_pre, v_norm_post, v_w_in, v_pool_w, v_pool_scale, v_ssm_a_re, v_ssm_a_im, v_ssm_log_dt, v_ssm_b_re, v_ssm_b_im, v_ssm_c_re, v_ssm_c_im, v_ssm_d, v_glu_w, v_glu_b, v_w_branch_pool, v_w_branch_ssm, v_w_out):
    given = dict(locals())
    me = _flat(_me())
    rows = x.shape[1]
    x2 = x[0]
    target = loss_target[0]
    ada_cols = w_ada.shape[2]

    b_ada_s = lax.dynamic_slice(b_ada, (0, me * ada_cols), (1, ada_cols))
    c_all, mod_rows = _ada_exchange(c, w_ada[0], b_ada_s)
    mod3 = mod_rows.reshape(3, D)

    shards = _cast_shards([w_in[0], pool_w[0], glu_w[0], w_branch_pool[0], w_branch_ssm[0], w_out[0]])

    tb_ssm = _tb(rows, 256)
    k_steps = tb_ssm // SUBLANES
    a_re, a_im = ssm_a_re[0], ssm_a_im[0]
    log_dt = ssm_log_dt[0].reshape(GROUPS, 1)
    b_re_t, b_im_t = ssm_b_re[0].transpose(0, 2, 1), ssm_b_im[0].transpose(0, 2, 1)
    wb, wct, pow_re, pow_im = _s5_prep(a_re, a_im, log_dt, b_re_t, b_im_t, ssm_c_re[0], ssm_c_im[0], k_steps)
    ptab = _state_layout(pow_re, pow_im)
    dvec = ssm_d[0].reshape(1, D)
    pm = _perm_matrix(tb_ssm)
    pmt = pm.T

    proj, w_in_g, pool_w_g, glu_g = _in_proj(x2, mod3, norm_pre, shards[0], shards[1:3])
    y_pool, pooled = _pool_fwd(proj, pool_w_g, pool_scale)
    y_ssm, ys_pre, carries, states, glu_gate, z_perm, u_perm, wbp_g, wbs_g, wout_g = _ssm_fwd(
        proj, pm, pmt, wb, wct, ptab, dvec, glu_g, glu_b, shards[3:])
    loss_part, dy, dyp, dys, dpg, dwbp, dwbs, dwout, head_vec = _head(
        x2, target, proj, y_pool, y_ssm, mod3, norm_post, wbp_g, wbs_g, wout_g)

    dpp, dpool_w, dpool_scale = _pool_bwd(dyp, pooled, proj, pool_w_g, pool_scale)
    dw_in_rest = _in_proj_bwd_w("in_proj_bwd_w_rest", x2, [dpp, dpg], mod3, norm_pre)
    dy_pre, dzs, dglu_w, dglu_b = _glu_bwd(dys, z_perm, ys_pre, glu_gate, pm, pmt, glu_g)
    dus, dbb, dcc, dabar, dd, p_glu, p_wbp, p_wbs, p_wout, p_pool_w, p_w_in = _ssm_bwd(
        dy_pre, u_perm, states, carries, pmt, wb, wct, ptab, dvec, [dglu_w, dwbp, dwbs, dwout], dpool_w,
        dw_in_rest)

    small32 = jnp.concatenate([head_vec, dpool_scale, dglu_b, dd, jnp.broadcast_to(loss_part, (1, D)),
                               jnp.zeros((2, D), F32), dabar.reshape(8, D)], axis=0)
    small16 = jnp.concatenate([dbb.reshape(2 * GROUPS, D), dcc.reshape(2 * GROUPS, D)], axis=0).astype(BF16)
    dw_in_ssm, p_small32, p_small16 = _in_proj_bwd_w("in_proj_bwd_w_ssm", x2, [dus, dzs], mod3, norm_pre,
                                                     gathered=(small32, small16))
    grad_x, p_w_in, p_pre = _in_proj_bwd_x(x2, dy, dpp, dus, dzs, dpg, mod3, norm_pre, w_in_g, dw_in_ssm, p_w_in)

    tot32, tot16, tot_pre = _sum_small([p_small32, p_small16, p_pre])
    d_abar_re, d_abar_im = _state_unlayout(tot32[8:16].reshape(N_STATE))
    d_bb_re, d_bb_im = tot16[0:64].reshape(GROUPS, G_H, G_P), tot16[64:128].reshape(GROUPS, G_H, G_P)
    g_a_re, g_a_im, g_log_dt, g_b_re_t, g_b_im_t = _s5_prep_bwd(
        a_re, a_im, log_dt, b_re_t, b_im_t, d_abar_re, d_abar_im, d_bb_re, d_bb_im)

    grads, deltas, new_m, new_v = {}, {}, {}, {}

    small = []

    def small_update(name, g2):
        small.append((name, g2))

    def shard_update(name, parts):
        shape = given[name].shape
        r2 = parts.shape[1:] if parts.ndim == 3 else (parts.shape[1] * parts.shape[2], parts.shape[3])
        w2, m2, v2 = (given[p + name].reshape(r2) for p in ("", "m_", "v_"))
        out = _adamw_reduce("adamw_" + name, parts.reshape((N_DEV,) + tuple(r2)), w2, m2, v2)
        grads[name], deltas[name], new_m[name], new_v[name] = (a.reshape(shape) for a in out)

    dmod_all = jnp.concatenate([p_pre[:, 0:2, :], p_small32[:, 0:1, :]], axis=1).reshape(N_DEV, 3 * D)
    dmod_cols = lax.dynamic_slice(dmod_all, (0, me * ada_cols), (N_DEV, ada_cols))
    out = _ada_update(c_all, dmod_cols, w_ada[0], m_w_ada[0], v_w_ada[0])
    grads['w_ada'], deltas['w_ada'], new_m['w_ada'], new_v['w_ada'] = (a.reshape(w_ada.shape) for a in out)

    small_update('b_ada', jnp.concatenate([tot_pre[0:2], tot32[0:1]], axis=0).reshape(1, 3 * D))
    small_update('norm_pre', tot_pre[2:3])
    small_update('norm_post', tot32[1:2])
    small_update('pool_scale', tot32[2:3])
    small_update('glu_b', tot32[3:4])
    small_update('ssm_d', tot32[4:5])
    small_update('ssm_a_re', g_a_re)
    small_update('ssm_a_im', g_a_im)
    small_update('ssm_log_dt', g_log_dt.reshape(1, GROUPS))
    small_update('ssm_b_re', g_b_re_t.transpose(0, 2, 1).reshape(GROUPS, G_P * G_H))
    small_update('ssm_b_im', g_b_im_t.transpose(0, 2, 1).reshape(GROUPS, G_P * G_H))
    small_update('ssm_c_re', tot16[128:192])
    small_update('ssm_c_im', -tot16[192:256])
    flat = _adamw_small([g2 for _, g2 in small],
                        *[[given[p + name].reshape(g2.shape) for name, g2 in small] for p in ("", "m_", "v_")])
    for t, (name, g2) in enumerate(small):
        shape = given[name].shape
        grads[name], deltas[name], new_m[name], new_v[name] = (
            a.reshape(shape) for a in (g2, *flat[3 * t:3 * t + 3]))
    shard_update('w_in', p_w_in)
    shard_update('pool_w', p_pool_w)
    shard_update('glu_w', p_glu)
    shard_update('w_branch_pool', p_wbp)
    shard_update('w_branch_ssm', p_wbs)
    shard_update('w_out', p_wout)

    return (tot32[5, 0], grad_x[None], *[grads[n] for n in WEIGHTS], *[deltas[n] for n in WEIGHTS],
            *[new_m[n] for n in WEIGHTS], *[new_v[n] for n in WEIGHTS])
```

```python
import math
from typing import Callable, NamedTuple, Optional

import jax
import jax.numpy as jnp
from jax import lax
from jax.experimental import pallas as pl
from jax.experimental.pallas import tpu as pltpu

F32 = jnp.float32
BF16 = jnp.bfloat16
MESH = pl.DeviceIdType.MESH

D = 1024
N_DEV = 8
N_IN = 6 * D
GROUPS = 64
G_H = 16
G_P = 64
N_Q = 4
Q_W = 2 * 16 * G_P
N_STATE = N_Q * Q_W
POOL_WINDOWS = (2, 4, 8, 16)
HALO = 16
RMS_EPS = 1e-6
SUBLANES = 8
LANE_CHUNK = 512
SCAN_UNROLL = 2
VMEM_LIMIT = 56 * 1024 * 1024

ADAM_LR = 0.001
ADAM_B1 = 0.9
ADAM_B2 = 0.999
ADAM_EPS = 1e-08
ADAM_WD = 0.01
ADAM_STEP = 10

WEIGHTS = ['w_ada', 'b_ada', 'norm_pre', 'norm_post', 'w_in', 'pool_w', 'pool_scale', 'ssm_a_re',
           'ssm_a_im', 'ssm_log_dt', 'ssm_b_re', 'ssm_b_im', 'ssm_c_re', 'ssm_c_im', 'ssm_d', 'glu_w',
           'glu_b', 'w_branch_pool', 'w_branch_ssm', 'w_out']


def _pcall(body, **kw):
    return pl.pallas_call(body, **kw)


def _params(sem=None, vmem=VMEM_LIMIT):
    return pltpu.CompilerParams(dimension_semantics=sem, vmem_limit_bytes=vmem)


def _tb(rows, pref):
    return pref if rows % pref == 0 and rows // pref >= 2 else rows // 2


def _full(shape, single=False):
    nd = len(shape)
    if single:
        return pl.BlockSpec(shape, lambda i: (0,) * nd, pipeline_mode=pl.Buffered(1))
    return pl.BlockSpec(shape, lambda i: (0,) * nd)


ANY = pl.BlockSpec(memory_space=pl.ANY)


def _me():
    return lax.axis_index("x"), lax.axis_index("y"), lax.axis_index("c")


def _flat(p):
    return 4 * p[0] + 2 * p[1] + p[2]


def _peer(k):
    x, y, c = _me()
    return (1 - x if k & 4 else x, 1 - y if k & 2 else y, 1 - c if k & 1 else c)


def _silu_parts(z):
    s = jax.nn.sigmoid(z)
    return z * s, s * (1.0 + z * (1.0 - s))


_GELU_C = math.sqrt(2.0 / math.pi)


def _gelu_parts(x):
    x2 = x * x
    t = jnp.tanh(_GELU_C * (x + 0.044715 * x * x2))
    g = 0.5 * x * (1.0 + t)
    dg = 0.5 * (1.0 + t) + 0.5 * x * (1.0 - t * t) * (_GELU_C * (1.0 + 3.0 * 0.044715 * x2))
    return g, dg


def _dot(a, b):
    return jnp.dot(a, b, preferred_element_type=F32)


def _dot_nt(a, b):
    return lax.dot_general(a, b, (((1,), (1,)), ((), ())), preferred_element_type=F32)


def _dot_tn(a, b):
    return lax.dot_general(a, b, (((0,), (0,)), ((), ())), preferred_element_type=F32)


def _rms_parts(x):
    r = lax.rsqrt(jnp.mean(x * x, axis=-1, keepdims=True) + RMS_EPS)
    return x * r, r


def _rms_bwd(dxn, xn, r):
    return r * (dxn - xn * jnp.mean(dxn * xn, axis=-1, keepdims=True))


def _ada_exchange(c, w_ada_s, b_ada_s):
    cols = w_ada_s.shape[1]

    def body(c_ref, w_ref, b_ref, call_ref, mod_ref, part_ref, ssem, rsem, lsem):
        me3 = _me()
        me = _flat(me3)
        mine = pltpu.make_async_copy(c_ref, call_ref.at[pl.ds(me, 1), :], lsem.at[0])
        mine.start()
        sends = []
        for k in range(1, N_DEV):
            cp = pltpu.make_async_remote_copy(src_ref=c_ref, dst_ref=call_ref.at[pl.ds(me, 1), :],
                                              send_sem=ssem.at[k - 1], recv_sem=rsem.at[k - 1],
                                              device_id=_peer(k), device_id_type=MESH)
            cp.start()
            sends.append(cp)
        mine.wait()
        for k in range(1, N_DEV):
            p = _flat(_peer(k))
            pltpu.make_async_remote_copy(src_ref=c_ref, dst_ref=call_ref.at[pl.ds(p, 1), :],
                                         send_sem=ssem.at[k - 1], recv_sem=rsem.at[k - 1],
                                         device_id=_peer(k), device_id_type=MESH).wait_recv()
        for cp in sends:
            cp.wait_send()
        ca = call_ref[...]
        act = ca * jax.nn.sigmoid(ca)
        part_ref[...] = jnp.dot(act, w_ref[...], preferred_element_type=F32,
                                precision=lax.Precision.HIGHEST) + b_ref[...]
        own = pltpu.make_async_copy(part_ref.at[pl.ds(me, 1), :], mod_ref.at[pl.ds(me, 1), :], lsem.at[1])
        own.start()
        sends = []
        for k in range(1, N_DEV):
            p = _flat(_peer(k))
            s = N_DEV - 1 + k - 1
            cp = pltpu.make_async_remote_copy(src_ref=part_ref.at[pl.ds(p, 1), :],
                                              dst_ref=mod_ref.at[pl.ds(me, 1), :],
                                              send_sem=ssem.at[s], recv_sem=rsem.at[s],
                                              device_id=_peer(k), device_id_type=MESH)
            cp.start()
            sends.append(cp)
        own.wait()
        for k in range(1, N_DEV):
            p = _flat(_peer(k))
            s = N_DEV - 1 + k - 1
            pltpu.make_async_remote_copy(src_ref=part_ref.at[pl.ds(p, 1), :],
                                         dst_ref=mod_ref.at[pl.ds(p, 1), :],
                                         send_sem=ssem.at[s], recv_sem=rsem.at[s],
                                         device_id=_peer(k), device_id_type=MESH).wait_recv()
        for cp in sends:
            cp.wait_send()

    vm = pl.BlockSpec(memory_space=pltpu.VMEM)
    return _pcall(
        body, name="ada_exchange",
        out_shape=(jax.ShapeDtypeStruct((N_DEV, D), F32), jax.ShapeDtypeStruct((N_DEV, cols), F32)),
        in_specs=[vm, vm, vm], out_specs=(vm, vm),
        scratch_shapes=[pltpu.VMEM((N_DEV, cols), F32),
                        pltpu.SemaphoreType.DMA((2 * (N_DEV - 1),)),
                        pltpu.SemaphoreType.DMA((2 * (N_DEV - 1),)),
                        pltpu.SemaphoreType.DMA((2,))],
    )(c, w_ada_s, b_ada_s)


class _Item(NamedTuple):
    src: int
    out: int
    src_view: Callable
    dst_view: Callable
    pred: Optional[Callable] = None


def _when(pred, dest, fn):
    if pred is None:
        fn()
    else:
        pl.when(pred(dest))(fn)


def _n_sems(items):
    return len(items) * (N_DEV - 1)


def _hosted_copies(items, srcs, outs, ssem, rsem, lsem, act):
    me = _flat(_me())
    for t, it in enumerate(items):
        local = lambda t=t, it=it: pltpu.make_async_copy(
            it.src_view(srcs[it.src], me), it.dst_view(outs[it.out], me), lsem.at[t])
        if act == "start":
            _when(it.pred, me, lambda local=local: local().start())
        else:
            _when(it.pred, me, lambda local=local: local().wait())
    for k in range(1, N_DEV):
        p3 = _peer(k)
        p = _flat(p3)
        for t, it in enumerate(items):
            s = t * (N_DEV - 1) + k - 1
            send = lambda it=it, s=s, p=p, p3=p3: pltpu.make_async_remote_copy(
                src_ref=it.src_view(srcs[it.src], p), dst_ref=it.dst_view(outs[it.out], me),
                send_sem=ssem.at[s], recv_sem=rsem.at[s], device_id=p3, device_id_type=MESH)
            recv = lambda it=it, s=s, p=p, p3=p3: pltpu.make_async_remote_copy(
                src_ref=it.src_view(srcs[it.src], p), dst_ref=it.dst_view(outs[it.out], p),
                send_sem=ssem.at[s], recv_sem=rsem.at[s], device_id=p3, device_id_type=MESH)
            if act == "start":
                _when(it.pred, p, lambda send=send: send().start())
            else:
                _when(it.pred, me, lambda recv=recv: recv().wait_recv())
                _when(it.pred, p, lambda send=send: send().wait_send())


def _sem_scratch(items):
    return [pltpu.SemaphoreType.DMA((_n_sems(items),)), pltpu.SemaphoreType.DMA((_n_sems(items),)),
            pltpu.SemaphoreType.DMA((len(items),))]


def _whole(ref, dest):
    return ref


def _slot(ref, sender):
    return ref.at[sender]


def _rows_of(rows):
    return lambda ref, dev: ref.at[pl.ds(dev * rows, rows), :]


def _pool_rows_of(rows):
    return lambda ref, dev: ref.at[:, pl.ds(dev * rows, rows), :]


def _gather_item(src, out, dst_view):
    return _Item(src, out, _whole, dst_view)


def _scatter_item(src, out, src_view):
    return _Item(src, out, src_view, _slot)


W_IN_BLOCK = 256
W_IN_SHARD = N_IN // N_DEV
SSM_BLOCKS = (2 * D // W_IN_BLOCK, 4 * D // W_IN_BLOCK)


def _w_in_block_item(src, out, j, ssm_part):
    def block(dest):
        return (W_IN_SHARD // W_IN_BLOCK) * dest + j

    def in_ssm(dest):
        b = block(dest)
        return (b >= SSM_BLOCKS[0]) & (b < SSM_BLOCKS[1])

    def src_view(ref, dest):
        b = block(dest)
        local = b - SSM_BLOCKS[0] if ssm_part else jnp.where(b < SSM_BLOCKS[0], b, b - (SSM_BLOCKS[1] - SSM_BLOCKS[0]))
        local = jnp.clip(local, 0, ref.shape[1] // W_IN_BLOCK - 1)
        return ref.at[:, pl.ds(local * W_IN_BLOCK, W_IN_BLOCK)]

    def dst_view(ref, sender):
        return ref.at[sender, :, pl.ds(j * W_IN_BLOCK, W_IN_BLOCK)]

    pred = in_ssm if ssm_part else (lambda dest: jnp.logical_not(in_ssm(dest)))
    return _Item(src, out, src_view, dst_view, pred)


def _cast_shards(arrs):
    def body(*refs):
        n = len(refs) // 2
        for i in range(n):
            refs[n + i][...] = refs[i][...].astype(BF16)

    vm = pl.BlockSpec(memory_space=pltpu.VMEM)
    return _pcall(body, name="cast_shards",
                  out_shape=tuple(jax.ShapeDtypeStruct(a.shape, BF16) for a in arrs),
                  in_specs=[vm] * len(arrs), out_specs=tuple([vm] * len(arrs)),
                  compiler_params=_params())(*arrs)


def _s5_discretise(a_re, a_im, log_dt, b_re_t, b_im_t):
    dt = jnp.exp(log_dt)
    lam_re = jnp.minimum(a_re, -1e-4)
    lam_im = a_im
    mag = jnp.exp(lam_re * dt)
    abar_re = mag * jnp.cos(lam_im * dt)
    abar_im = mag * jnp.sin(lam_im * dt)
    den = lam_re * lam_re + lam_im * lam_im
    num_re = abar_re - 1.0
    f_re = (num_re * lam_re + abar_im * lam_im) / den
    f_im = (abar_im * lam_re - num_re * lam_im) / den
    f_re, f_im = f_re[:, None, :], f_im[:, None, :]
    bb_re = f_re * b_re_t - f_im * b_im_t
    bb_im = f_re * b_im_t + f_im * b_re_t
    return abar_re, abar_im, bb_re, bb_im


def _group_masks():
    spread = lax.broadcasted_iota(jnp.int32, (G_P, 16 * G_P), 1) % G_P == lax.broadcasted_iota(
        jnp.int32, (G_P, 16 * G_P), 0)
    own = lax.broadcasted_iota(jnp.int32, (16 * G_H, 16 * G_P), 0) // G_H == lax.broadcasted_iota(
        jnp.int32, (16 * G_H, 16 * G_P), 1) // G_P
    return spread, own


def _s5_prep(a_re, a_im, log_dt, b_re_t, b_im_t, c_re, c_im, n_pow):
    def body(ar_ref, ai_ref, ld_ref, br_ref, bi_ref, cr_ref, ci_ref, wb_ref, wct_ref, pr_ref, pi_ref):
        abar_re, abar_im, bb_re, bb_im = _s5_discretise(ar_ref[...], ai_ref[...], ld_ref[...], br_ref[...], bi_ref[...])
        spread, own = _group_masks()
        spread = spread.astype(BF16)
        for ref, parts in ((wb_ref, (bb_re, bb_im)), (wct_ref, (cr_ref[...], -ci_ref[...]))):
            for half, t in enumerate(parts):
                for q in range(N_Q):
                    blocks = t[q * 16:(q + 1) * 16].reshape(16 * G_H, G_P).astype(BF16)
                    dense = jnp.where(own, _dot(blocks, spread), 0.0)
                    ref[q, :, half * (Q_W // 2):(half + 1) * (Q_W // 2)] = dense.astype(BF16)
        p_re, p_im = abar_re, abar_im
        pr_ref[0] = p_re
        pi_ref[0] = p_im
        for k in range(1, n_pow):
            p_re, p_im = p_re * abar_re - p_im * abar_im, p_re * abar_im + p_im * abar_re
            pr_ref[k] = p_re
            pi_ref[k] = p_im

    vm = pl.BlockSpec(memory_space=pltpu.VMEM)
    return _pcall(body, name="s5_prep",
                  out_shape=(jax.ShapeDtypeStruct((N_Q, 16 * G_H, Q_W), BF16),
                             jax.ShapeDtypeStruct((N_Q, 16 * G_H, Q_W), BF16),
                             jax.ShapeDtypeStruct((n_pow, GROUPS, G_P), F32),
                             jax.ShapeDtypeStruct((n_pow, GROUPS, G_P), F32)),
                  in_specs=[vm] * 7, out_specs=(vm, vm, vm, vm), compiler_params=_params(),
                  )(a_re, a_im, log_dt, b_re_t, b_im_t, c_re, c_im)


def _s5_prep_bwd(a_re, a_im, log_dt, b_re_t, b_im_t, d_abar_re, d_abar_im, d_bb_re, d_bb_im):
    def body(ar_ref, ai_ref, ld_ref, br_ref, bi_ref, dar_ref, dai_ref, dbr_ref, dbi_ref,
             gar_ref, gai_ref, gld_ref, gbr_ref, gbi_ref):
        _, vjp = jax.vjp(_s5_discretise, ar_ref[...], ai_ref[...], ld_ref[...], br_ref[...], bi_ref[...])
        g = vjp((dar_ref[...], dai_ref[...], dbr_ref[...], dbi_ref[...]))
        gar_ref[...] = g[0]
        gai_ref[...] = g[1]
        gld_ref[...] = g[2]
        gbr_ref[...] = g[3]
        gbi_ref[...] = g[4]

    vm = pl.BlockSpec(memory_space=pltpu.VMEM)
    ins = (a_re, a_im, log_dt, b_re_t, b_im_t)
    return _pcall(body, name="s5_prep_bwd",
                  out_shape=tuple(jax.ShapeDtypeStruct(a.shape, F32) for a in ins),
                  in_specs=[vm] * 9, out_specs=tuple([vm] * 5), compiler_params=_params(),
                  )(*ins, d_abar_re, d_abar_im, d_bb_re, d_bb_im)


def _state_layout(re, im):
    lead = re.shape[:-2]
    r = re.reshape(lead + (N_Q, 1, 16 * G_P))
    i = im.reshape(lead + (N_Q, 1, 16 * G_P))
    return jnp.concatenate([r, i], axis=-2).reshape(lead + (N_STATE,))


def _state_unlayout(v):
    v4 = v.reshape(N_Q, 2, 16, G_P)
    return v4[:, 0].reshape(GROUPS, G_P), v4[:, 1].reshape(GROUPS, G_P)


def _perm_matrix(tb):
    k_steps = tb // SUBLANES
    r = jnp.arange(tb)
    src = (r % SUBLANES) * k_steps + r // SUBLANES
    return (src[:, None] == jnp.arange(tb)[None, :]).astype(BF16)


def _lane_chunks(q):
    for lc in range(Q_W // 2 // LANE_CHUNK):
        re = q * Q_W + lc * LANE_CHUNK
        yield re, re + Q_W // 2


def _steps(lo, hi, body, init):
    if hi - lo <= SCAN_UNROLL:
        for k in range(lo, hi):
            init = body(k, init)
        return init
    trips = (hi - lo) // SCAN_UNROLL

    def trip(j, carry):
        for u in range(SCAN_UNROLL):
            carry = body(lo + j * SCAN_UNROLL + u, carry)
        return carry

    carry = lax.fori_loop(0, trips, trip, init)
    for k in range(lo + trips * SCAN_UNROLL, hi):
        carry = body(k, carry)
    return carry


def _tile(k):
    if isinstance(k, int):
        return pl.ds(k * SUBLANES, SUBLANES)
    return pl.ds(pl.multiple_of(k * SUBLANES, SUBLANES), SUBLANES)


def _scan_forward(q, s_ref, p_ref, carry_ref, enter_ref, fin_ref, k_steps):
    for re, im in _lane_chunks(q):
        lr, li = pl.ds(re, LANE_CHUNK), pl.ds(im, LANE_CHUNK)
        a_re = jnp.broadcast_to(p_ref[0:1, lr], (SUBLANES, LANE_CHUNK))
        a_im = jnp.broadcast_to(p_ref[0:1, li], (SUBLANES, LANE_CHUNK))

        def local(k, st):
            sr, si = st
            rows = _tile(k)
            nr = a_re * sr - a_im * si + s_ref[rows, lr]
            ni = a_re * si + a_im * sr + s_ref[rows, li]
            s_ref[rows, lr] = nr
            s_ref[rows, li] = ni
            return nr, ni

        zero = jnp.zeros((SUBLANES, LANE_CHUNK), F32)
        fr, fi = _steps(0, k_steps, local, (zero, zero))
        fin_ref[:, lr] = fr
        fin_ref[:, li] = fi
        ak_re, ak_im = p_ref[k_steps - 1:k_steps, lr], p_ref[k_steps - 1:k_steps, li]
        c_re, c_im = carry_ref[:, lr], carry_ref[:, li]
        for seg in range(SUBLANES):
            enter_ref[seg:seg + 1, lr] = c_re
            enter_ref[seg:seg + 1, li] = c_im
            f_re, f_im = fin_ref[seg:seg + 1, lr], fin_ref[seg:seg + 1, li]
            c_re, c_im = f_re + ak_re * c_re - ak_im * c_im, f_im + ak_re * c_im + ak_im * c_re
        carry_ref[:, lr] = c_re
        carry_ref[:, li] = c_im
        e_re, e_im = enter_ref[:, lr], enter_ref[:, li]

        def fix(k, _):
            rows = _tile(k)
            p_re = p_ref[pl.ds(k, 1), lr]
            p_im = p_ref[pl.ds(k, 1), li]
            s_ref[rows, lr] = s_ref[rows, lr] + (p_re * e_re - p_im * e_im)
            s_ref[rows, li] = s_ref[rows, li] + (p_re * e_im + p_im * e_re)
            return 0

        _steps(0, k_steps, fix, 0)


def _scan_backward(q, g_ref, s_ref, p_ref, carry_ref, s_in_ref, fin_ref, da_ref, k_steps):
    seg_id = lax.broadcasted_iota(jnp.int32, (SUBLANES, LANE_CHUNK), 0)
    for re, im in _lane_chunks(q):
        lr, li = pl.ds(re, LANE_CHUNK), pl.ds(im, LANE_CHUNK)
        a_re = jnp.broadcast_to(p_ref[0:1, lr], (SUBLANES, LANE_CHUNK))
        a_im = jnp.broadcast_to(p_ref[0:1, li], (SUBLANES, LANE_CHUNK))

        def local(j, st):
            sr, si = st
            rows = _tile(k_steps - 1 - j)
            nr = a_re * sr + a_im * si + g_ref[rows, lr]
            ni = a_re * si - a_im * sr + g_ref[rows, li]
            g_ref[rows, lr] = nr
            g_ref[rows, li] = ni
            return nr, ni

        zero = jnp.zeros((SUBLANES, LANE_CHUNK), F32)
        fr, fi = _steps(0, k_steps, local, (zero, zero))
        fin_ref[:, lr] = fr
        fin_ref[:, li] = fi
        ak_re, ak_im = p_ref[k_steps - 1:k_steps, lr], p_ref[k_steps - 1:k_steps, li]
        c_re, c_im = carry_ref[:, lr], carry_ref[:, li]
        lam_in = [None] * SUBLANES
        for seg in reversed(range(SUBLANES)):
            lam_in[seg] = (c_re, c_im)
            f_re, f_im = fin_ref[seg:seg + 1, lr], fin_ref[seg:seg + 1, li]
            c_re, c_im = f_re + ak_re * c_re + ak_im * c_im, f_im + ak_re * c_im - ak_im * c_re
        carry_ref[:, lr] = c_re
        carry_ref[:, li] = c_im
        for seg in range(SUBLANES):
            fin_ref[seg:seg + 1, lr] = lam_in[seg][0]
            fin_ref[seg:seg + 1, li] = lam_in[seg][1]
        e_re, e_im = fin_ref[:, lr], fin_ref[:, li]

        def fix_with(k, acc, sp_re, sp_im):
            acc_re, acc_im = acc
            rows = _tile(k)
            p_re = p_ref[pl.ds(k_steps - 1 - k, 1), lr]
            p_im = p_ref[pl.ds(k_steps - 1 - k, 1), li]
            l_re = g_ref[rows, lr] + (p_re * e_re + p_im * e_im)
            l_im = g_ref[rows, li] + (p_re * e_im - p_im * e_re)
            g_ref[rows, lr] = l_re
            g_ref[rows, li] = l_im
            return acc_re + (l_re * sp_re + l_im * sp_im), acc_im + (l_im * sp_re - l_re * sp_im)

        def fix(k, acc):
            prev = _tile(k - 1)
            return fix_with(k, acc, s_ref[prev, lr], s_ref[prev, li])

        last = _tile(k_steps - 1)
        before_re = jnp.where(seg_id == 0, s_in_ref[:, lr], pltpu.roll(s_ref[last, lr], 1, axis=0))
        before_im = jnp.where(seg_id == 0, s_in_ref[:, li], pltpu.roll(s_ref[last, li], 1, axis=0))
        acc = fix_with(0, (zero, zero), before_re, before_im)
        acc_re, acc_im = _steps(1, k_steps, fix, acc)
        da_ref[:, lr] = da_ref[:, lr] + jnp.sum(acc_re, axis=0, keepdims=True)
        da_ref[:, li] = da_ref[:, li] + jnp.sum(acc_im, axis=0, keepdims=True)


def _prenorm(x, mod3, norm_pre):
    xn, r = _rms_parts(x)
    return xn, r, xn * norm_pre * (1.0 + mod3[1:2, :]) + mod3[0:1, :]


CHIP_FLIPS = (4, 2, 6)


def _shard_order(me):
    flips = [0, 1] + [f + c for f in CHIP_FLIPS for c in (0, 1)]
    return jnp.stack([me ^ f for f in flips]).astype(jnp.int32)


def _in_proj(x, mod3, norm_pre, w_in_s, shards):
    rows = x.shape[0]
    tb = _tb(rows, 2048)
    nblk = rows // tb
    n_sh = len(shards)
    last_step = N_DEV - 1
    items = [_gather_item(0, 0, _pool_rows_of(shards[0].shape[1]))] + \
            [_gather_item(t, t, _rows_of(shards[t].shape[0])) for t in range(1, n_sh)]

    def body(order_ref, x_ref, mod_ref, np_ref, w_src, *rest):
        src_refs, proj_ref, w_full, out_refs = rest[:n_sh], rest[n_sh], rest[n_sh + 1], rest[n_sh + 2:2 * n_sh + 2]
        h_scr, wg, ssem, rsem, lsem, *sems = rest[2 * n_sh + 2:]
        s, i = pl.program_id(0), pl.program_id(1)
        me3 = _me()
        me = _flat(me3)
        sibling = _peer(1)

        def own_copy(slot, k):
            return pltpu.make_async_remote_copy(src_ref=w_src, dst_ref=wg.at[me], send_sem=ssem.at[slot],
                                                recv_sem=rsem.at[slot], device_id=_peer(k), device_id_type=MESH)

        def passed_copy(j):
            p = _flat(_peer(CHIP_FLIPS[j]))
            return pltpu.make_async_remote_copy(src_ref=wg.at[p], dst_ref=wg.at[p], send_sem=ssem.at[4 + j],
                                                recv_sem=rsem.at[4 + j], device_id=sibling, device_id_type=MESH)

        def arrival(slot, flip):
            p = _flat(_peer(flip))
            pltpu.make_async_remote_copy(src_ref=w_src, dst_ref=wg.at[p], send_sem=ssem.at[slot],
                                         recv_sem=rsem.at[slot], device_id=sibling, device_id_type=MESH).wait_recv()

        def keep(t):
            p = order_ref[t]
            return pltpu.make_async_copy(wg.at[p], w_full.at[:, pl.ds(p * W_IN_SHARD, W_IN_SHARD)], lsem.at[1 + t])

        first = i == 0
        for t in range(last_step):
            pl.when(first & (s == t + 1))(lambda t=t: keep(t).start())

        @pl.when(first & (s == 0))
        def _():
            mine = pltpu.make_async_copy(w_src, wg.at[me], lsem.at[0])
            mine.start()
            own_copy(0, 1).start()
            for j, f in enumerate(CHIP_FLIPS[:2]):
                own_copy(1 + j, f).start()
            mine.wait()

        @pl.when(first & (s == 1))
        def _():
            arrival(0, 1)

        for j, f in enumerate(CHIP_FLIPS):
            @pl.when(first & (s == 2 + 2 * j))
            def _(j=j, f=f):
                arrival(1 + j, f)
                passed_copy(j).start()
                if j == 0:
                    own_copy(3, CHIP_FLIPS[2]).start()

            @pl.when(first & (s == 3 + 2 * j))
            def _(j=j, f=f):
                arrival(4 + j, f + 1)

        @pl.when(first & (s == last_step - 1))
        def _():
            _hosted_copies(items, src_refs, out_refs, *sems, act="start")

        rows_i = pl.ds(pl.multiple_of(i * tb, tb), tb)

        @pl.when(s == 0)
        def _():
            _, _, h = _prenorm(x_ref[...], mod_ref[...], np_ref[...])
            h_scr[rows_i, :] = h.astype(BF16)

        proj_ref[...] = _dot(h_scr[rows_i, :], wg[order_ref[s]]).astype(BF16)

        @pl.when((s == last_step) & (i == nblk - 1))
        def _():
            own_copy(0, 1).wait_send()
            for j, f in enumerate(CHIP_FLIPS):
                own_copy(1 + j, f).wait_send()
                passed_copy(j).wait_send()
            keep(last_step).start()
            for t in range(N_DEV):
                keep(t).wait()
            _hosted_copies(items, src_refs, out_refs, *sems, act="wait")

    full = [jax.ShapeDtypeStruct((4, 256, 256), BF16)] + [jax.ShapeDtypeStruct((D, D), BF16)] * (n_sh - 1)
    grid_spec = pltpu.PrefetchScalarGridSpec(
        num_scalar_prefetch=1, grid=(N_DEV, nblk),
        in_specs=[pl.BlockSpec((tb, D), lambda s, i, order: (jnp.where(s == 0, i, nblk - 1), 0)),
                  pl.BlockSpec((3, D), lambda s, i, order: (0, 0)), pl.BlockSpec((1, D), lambda s, i, order: (0, 0)),
                  ANY] + [ANY] * n_sh,
        out_specs=(pl.BlockSpec((tb, W_IN_SHARD), lambda s, i, order: (i, order[s])), ANY, *([ANY] * n_sh)),
        scratch_shapes=[pltpu.VMEM((rows, D), BF16), pltpu.VMEM((N_DEV, D, W_IN_SHARD), BF16),
                        pltpu.SemaphoreType.DMA((N_DEV - 1,)), pltpu.SemaphoreType.DMA((N_DEV - 1,)),
                        pltpu.SemaphoreType.DMA((1 + N_DEV,))] + _sem_scratch(items))
    return _pcall(body, name="in_proj", grid_spec=grid_spec,
                  out_shape=(jax.ShapeDtypeStruct((rows, N_IN), BF16), jax.ShapeDtypeStruct((D, N_IN), BF16), *full),
                  compiler_params=_params(("arbitrary", "arbitrary")),
                  )(_shard_order(_flat(_me())), x, mod3, norm_pre, w_in_s, *shards)


def _pool_windows(ext, tb, first_row):
    inv_counts = _inv_counts(tb, first_row)
    pooled = []
    for g, w in enumerate(POOL_WINDOWS):
        acc = ext[:, g * 256:(g + 1) * 256]
        tok = acc[HALO:, :]
        s = 1
        while s < w:
            acc = acc + pltpu.roll(acc, s, axis=0)
            s *= 2
        pooled.append(acc[HALO:, :] * inv_counts[g] - tok)
    return pooled, inv_counts


def _inv_counts(tb, first_row):
    pos = (first_row + lax.broadcasted_iota(jnp.int32, (tb, 1), 0) + 1).astype(F32)
    return [1.0 / jnp.minimum(pos, float(w)) for w in POOL_WINDOWS]


def _pool_fwd(proj, pool_w, pool_scale):
    rows = proj.shape[0]
    tb = _tb(rows, 512)
    hb = tb // HALO

    def body(u_ref, halo_ref, z_ref, pw_ref, ps_ref, y_ref, pooled_ref):
        i = pl.program_id(0)
        u = u_ref[...].astype(F32)
        halo = jnp.where(i > 0, halo_ref[...].astype(F32), 0.0)
        pooled, _ = _pool_windows(jnp.concatenate([halo, u], axis=0), tb, i * tb)
        silu_z, _ = _silu_parts(z_ref[...].astype(F32))
        for g in range(4):
            cols = slice(g * 256, (g + 1) * 256)
            pooled_b = pooled[g].astype(BF16)
            pooled_ref[:, cols] = pooled_b
            mixed = _dot(pooled_b, pw_ref[g])
            y_ref[:, cols] = (mixed * ps_ref[:, cols] * silu_z[:, cols]).astype(BF16)

    blk = pl.BlockSpec((tb, D), lambda i: (i, 0))
    return _pcall(body, name="pool_fwd", grid=(rows // tb,),
                  out_shape=(jax.ShapeDtypeStruct((rows, D), BF16), jax.ShapeDtypeStruct((rows, D), BF16)),
                  in_specs=[blk, pl.BlockSpec((HALO, D), lambda i: (jnp.maximum(i * hb - 1, 0), 0)),
                            pl.BlockSpec((tb, D), lambda i: (i, 1)),
                            _full((4, 256, 256)), _full((1, D))],
                  out_specs=(blk, blk),
                  compiler_params=_params(("arbitrary",)))(proj, proj, proj, pool_w, pool_scale)


def _ssm_fwd(proj, pm, pmt, wb, wct, ptab, dvec, glu_w, glu_b, shards):
    rows = proj.shape[0]
    tb = pm.shape[0]
    k_steps = tb // SUBLANES
    nblk = rows // tb
    n_sh = len(shards)
    items = [_gather_item(t, t, _rows_of(shards[t].shape[0])) for t in range(n_sh)]

    def body(u_ref, z_ref, pm_ref, pmt_ref, wb_ref, wct_ref, p_ref, d_ref, gw_ref, gb_ref, *rest):
        src_refs = rest[:n_sh]
        y_ref, ys_ref, carry_out_ref, s_ref, gate_ref, zp_ref, up_ref = rest[n_sh:n_sh + 7]
        out_refs = rest[n_sh + 7:2 * n_sh + 7]
        carry_ref, enter_ref, fin_ref, *sems = rest[2 * n_sh + 7:]

        @pl.when(pl.program_id(0) == 0)
        def _():
            _hosted_copies(items, src_refs, out_refs, *sems, act="start")
            carry_ref[...] = jnp.zeros_like(carry_ref)

        carry_out_ref[...] = carry_ref[...]
        up = _dot(pm_ref[...], u_ref[...]).astype(BF16)
        up_ref[...] = up

        for q in range(N_Q):
            s_ref[:, q * Q_W:(q + 1) * Q_W] = _dot(up[:, q * 256:(q + 1) * 256], wb_ref[q])
        for q in range(N_Q):
            _scan_forward(q, s_ref, p_ref, carry_ref, enter_ref, fin_ref, k_steps)
        for q in range(N_Q):
            cols = slice(q * 256, (q + 1) * 256)
            y = _dot_nt(s_ref[:, q * Q_W:(q + 1) * Q_W].astype(BF16), wct_ref[q])
            ys_ref[:, cols] = y + d_ref[:, cols] * up[:, cols].astype(F32)
        yg, _ = _gelu_parts(ys_ref[...])
        gate = jax.nn.sigmoid(_dot(yg.astype(BF16), gw_ref[...]) + gb_ref[...])
        gate_ref[...] = gate
        zp = _dot(pm_ref[...], z_ref[...])
        zp_ref[...] = zp.astype(BF16)
        silu_z, _ = _silu_parts(zp)
        y_ref[...] = _dot(pmt_ref[...], (yg * gate * silu_z).astype(BF16)).astype(BF16)

        @pl.when(pl.program_id(0) == nblk - 1)
        def _():
            _hosted_copies(items, src_refs, out_refs, *sems, act="wait")

    return _pcall(body, name="ssm_fwd", grid=(nblk,),
                  out_shape=(jax.ShapeDtypeStruct((rows, D), BF16), jax.ShapeDtypeStruct((rows, D), F32),
                             jax.ShapeDtypeStruct((nblk, 1, N_STATE), F32),
                             jax.ShapeDtypeStruct((rows, N_STATE), F32),
                             jax.ShapeDtypeStruct((rows, D), F32), jax.ShapeDtypeStruct((rows, D), BF16),
                             jax.ShapeDtypeStruct((rows, D), BF16),
                             *[jax.ShapeDtypeStruct((D, D), BF16)] * n_sh),
                  in_specs=[pl.BlockSpec((tb, D), lambda i: (i, 2)), pl.BlockSpec((tb, D), lambda i: (i, 3)),
                            _full((tb, tb)), _full((tb, tb)),
                            _full((N_Q, 256, Q_W), single=True), _full((N_Q, 256, Q_W), single=True),
                            _full((k_steps, N_STATE)), _full((1, D)), _full((D, D), single=True), _full((1, D))] +
                           [ANY] * n_sh,
                  out_specs=(pl.BlockSpec((tb, D), lambda i: (i, 0)), pl.BlockSpec((tb, D), lambda i: (i, 0)),
                             pl.BlockSpec((None, 1, N_STATE), lambda i: (i, 0, 0)),
                             pl.BlockSpec((tb, N_STATE), lambda i: (i, 0)),
                             *[pl.BlockSpec((tb, D), lambda i: (i, 0))] * 3, *([ANY] * n_sh)),
                  scratch_shapes=[pltpu.VMEM((1, N_STATE), F32),
                                  pltpu.VMEM((SUBLANES, N_STATE), F32), pltpu.VMEM((SUBLANES, N_STATE), F32)] +
                                 _sem_scratch(items),
                  compiler_params=_params(("arbitrary",)))(proj, proj, pm, pmt, wb, wct, ptab, dvec, glu_w, glu_b,
                                                           *shards)


def _head(x, target, proj, y_pool, y_ssm, mod3, norm_post, wbp, wbs, wout):
    rows = x.shape[0]
    tb = _tb(rows, 256)
    nblk = rows // tb
    n_feat = float(D)

    def body(x_ref, t_ref, gp_ref, gs_ref, yp_ref, ys_ref, mod_ref, npost_ref, wbp_ref, wbs_ref, wout_ref,
             loss_ref, dy_ref, dyp_ref, dys_ref, dg_ref, dwbp_hbm, dwbs_hbm, dwout_hbm, vec_ref,
             acc_bp, acc_bs, acc_out, acc_loss, acc_vec):
        i = pl.program_id(0)

        @pl.when(i == 0)
        def _():
            acc_bp[...] = jnp.zeros_like(acc_bp)
            acc_bs[...] = jnp.zeros_like(acc_bs)
            acc_out[...] = jnp.zeros_like(acc_out)
            acc_loss[...] = jnp.zeros_like(acc_loss)
            acc_vec[...] = jnp.zeros_like(acc_vec)

        gate = mod_ref[2:3, :]
        npost = npost_ref[...]
        yp, ys = yp_ref[...], ys_ref[...]
        sgp = jax.nn.sigmoid(gp_ref[...].astype(F32))
        sgs = jax.nn.sigmoid(gs_ref[...].astype(F32))
        pb = _dot(yp, wbp_ref[...])
        psm = _dot(ys, wbs_ref[...])
        mb = (sgp * pb + sgs * psm).astype(BF16)
        out = _dot(mb, wout_ref[...])
        on, r = _rms_parts(out)
        normed = on * npost
        diff = x_ref[...] + gate * normed - t_ref[...]
        acc_loss[...] += jnp.sum(diff * diff, axis=0, keepdims=True)
        dy = diff * (1.0 / n_feat)
        dy_ref[...] = dy
        acc_vec[0:1, :] += jnp.sum(dy * normed, axis=0, keepdims=True)
        dn = dy * gate
        acc_vec[1:2, :] += jnp.sum(dn * on, axis=0, keepdims=True)
        dout = _rms_bwd(dn * npost, on, r).astype(BF16)
        dm = _dot_nt(dout, wout_ref[...])
        dpb = (dm * sgp).astype(BF16)
        dps = (dm * sgs).astype(BF16)
        dg_ref[:, :D] = (dm * pb * sgp * (1.0 - sgp)).astype(BF16)
        dg_ref[:, D:] = (dm * psm * sgs * (1.0 - sgs)).astype(BF16)
        dyp_ref[...] = _dot_nt(dpb, wbp_ref[...]).astype(BF16)
        dys_ref[...] = _dot_nt(dps, wbs_ref[...]).astype(BF16)
        acc_out[...] += _dot_tn(mb, dout)
        acc_bp[...] += _dot_tn(yp, dpb)
        acc_bs[...] += _dot_tn(ys, dps)

        @pl.when(i == nblk - 1)
        def _():
            loss_ref[...] = 0.5 / n_feat * jnp.sum(acc_loss[...], axis=1, keepdims=True)
            vec_ref[...] = acc_vec[...]
            pltpu.sync_copy(acc_bp, dwbp_hbm)
            pltpu.sync_copy(acc_bs, dwbs_hbm)
            pltpu.sync_copy(acc_out, dwout_hbm)

    row = lambda c: pl.BlockSpec((tb, D), lambda i: (i, c))
    w = _full((D, D), single=True)
    return _pcall(body, name="head", grid=(nblk,),
                  out_shape=(jax.ShapeDtypeStruct((1, 1), F32), jax.ShapeDtypeStruct((rows, D), F32),
                             jax.ShapeDtypeStruct((rows, D), BF16), jax.ShapeDtypeStruct((rows, D), BF16),
                             jax.ShapeDtypeStruct((rows, 2 * D), BF16),
                             jax.ShapeDtypeStruct((D, D), F32), jax.ShapeDtypeStruct((D, D), F32),
                             jax.ShapeDtypeStruct((D, D), F32), jax.ShapeDtypeStruct((2, D), F32)),
                  in_specs=[row(0), row(0), row(4), row(5), row(0), row(0), _full((3, D)), _full((1, D)), w, w, w],
                  out_specs=(_full((1, 1)), row(0), row(0), row(0), pl.BlockSpec((tb, 2 * D), lambda i: (i, 0)),
                             ANY, ANY, ANY, _full((2, D))),
                  scratch_shapes=[pltpu.VMEM((D, D), F32), pltpu.VMEM((D, D), F32), pltpu.VMEM((D, D), F32),
                                  pltpu.VMEM((1, D), F32), pltpu.VMEM((2, D), F32)],
                  compiler_params=_params(("arbitrary",)))(x, target, proj, proj, y_pool, y_ssm, mod3, norm_post,
                                                           wbp, wbs, wout)


def _glu_bwd(dys, zp, ys_pre, gate, pm, pmt, glu_w):
    rows = dys.shape[0]
    pb = pm.shape[0]
    per_step = 2 if rows % (2 * pb) == 0 and rows // pb >= 4 else 1
    tb = per_step * pb
    nblk = rows // tb
    parts = [slice(j * pb, (j + 1) * pb) for j in range(per_step)]

    def body(dys_ref, z_ref, ysp_ref, sg_ref, pm_ref, pmt_ref, gw_ref, dyp_ref, dz_ref, dgw_hbm, dgb_ref,
             acc_w, acc_b):
        i = pl.program_id(0)

        @pl.when(i == 0)
        def _():
            acc_w[...] = jnp.zeros_like(acc_w)
            acc_b[...] = jnp.zeros_like(acc_b)

        d_out = jnp.concatenate([_dot(pm_ref[...], dys_ref[rs, :]) for rs in parts], axis=0)
        yg, dgelu = _gelu_parts(ysp_ref[...])
        ygb = yg.astype(BF16)
        sg = sg_ref[...]
        silu_z, dsilu_z = _silu_parts(z_ref[...].astype(F32))
        dz = (d_out * (yg * sg) * dsilu_z).astype(BF16)
        for rs in parts:
            dz_ref[rs, :] = _dot(pmt_ref[...], dz[rs, :]).astype(BF16)
        dglu = d_out * silu_z
        dq = dglu * yg * sg * (1.0 - sg)
        dqb = dq.astype(BF16)
        acc_b[...] += jnp.sum(dq, axis=0, keepdims=True)
        acc_w[...] += _dot_tn(ygb, dqb)
        dyg = dglu * sg + _dot_nt(dqb, gw_ref[...])
        dyp_ref[...] = (dyg * dgelu).astype(BF16)

        @pl.when(i == nblk - 1)
        def _():
            dgb_ref[...] = acc_b[...]
            pltpu.sync_copy(acc_w, dgw_hbm)

    row = lambda c: pl.BlockSpec((tb, D), lambda i: (i, c))
    return _pcall(body, name="glu_bwd", grid=(nblk,),
                  out_shape=(jax.ShapeDtypeStruct((rows, D), BF16), jax.ShapeDtypeStruct((rows, D), BF16),
                             jax.ShapeDtypeStruct((D, D), F32), jax.ShapeDtypeStruct((1, D), F32)),
                  in_specs=[row(0), row(0), row(0), row(0), _full((pb, pb)), _full((pb, pb)),
                            _full((D, D), single=True)],
                  out_specs=(row(0), row(0), ANY, _full((1, D))),
                  scratch_shapes=[pltpu.VMEM((D, D), F32), pltpu.VMEM((1, D), F32)],
                  compiler_params=_params(("arbitrary",)))(dys, zp, ys_pre, gate, pm, pmt, glu_w)


def _ssm_bwd(dyp, up, states, carries, pmt, wb, wct, ptab, dvec, mat_grads, dpool_w, dw_in_rest):
    rows = dyp.shape[0]
    tb = pmt.shape[0]
    k_steps = tb // SUBLANES
    nblk = rows // tb
    n_mat = len(mat_grads)
    hosted = [*mat_grads, dpool_w, dw_in_rest]
    n_h = len(hosted)
    shard_rows = D // N_DEV
    pool_rows = dpool_w.shape[1] // N_DEV
    items = [_scatter_item(t, t, _rows_of(shard_rows)) for t in range(n_mat)] + \
            [_scatter_item(n_mat, n_mat, _pool_rows_of(pool_rows))] + \
            [_w_in_block_item(n_mat + 1, n_mat + 1, j, ssm_part=False) for j in range(W_IN_SHARD // W_IN_BLOCK)]
    n_in, n_out = 9, 5

    def body(*refs):
        dyp_ref, u_ref, s_ref, cin_ref, pmt_ref, wb_ref, wct_ref, p_ref, d_ref = refs[:n_in]
        src_refs = refs[n_in:n_in + n_h]
        du_ref, dbb_ref, dcc_ref, da_ref, dd_ref = refs[n_in + n_h:n_in + n_h + n_out]
        recv_refs = refs[n_in + n_h + n_out:n_in + 2 * n_h + n_out]
        (g_ref, carry_b, fin_ref, acc_wb, acc_wct, acc_da, acc_dd, dup_ref,
         *sems) = refs[n_in + 2 * n_h + n_out:]
        i = pl.program_id(0)

        @pl.when(i == 0)
        def _():
            _hosted_copies(items, src_refs, recv_refs, *sems, act="start")
            carry_b[...] = jnp.zeros_like(carry_b)
            acc_wb[...] = jnp.zeros_like(acc_wb)
            acc_wct[...] = jnp.zeros_like(acc_wct)
            acc_da[...] = jnp.zeros_like(acc_da)
            acc_dd[...] = jnp.zeros_like(acc_dd)

        def keep_own(acc, q, prod):
            for gl in range(16):
                r, c = slice(gl * G_H, (gl + 1) * G_H), (gl // 2) * 128
                acc[q, r, 0:128] += prod[r, c:c + 128]
                acc[q, r, 128:256] += prod[r, Q_W // 2 + c:Q_W // 2 + c + 128]

        dy = dyp_ref[...]
        up = u_ref[...]
        acc_dd[...] += jnp.sum(dy.astype(F32) * up.astype(F32), axis=0, keepdims=True)
        for q in range(N_Q):
            cols = slice(q * 256, (q + 1) * 256)
            g_ref[:, q * Q_W:(q + 1) * Q_W] = _dot(dy[:, cols], wct_ref[q])
            keep_own(acc_wct, q, _dot_tn(dy[:, cols], s_ref[:, q * Q_W:(q + 1) * Q_W].astype(BF16)))
        for q in range(N_Q):
            _scan_backward(q, g_ref, s_ref, p_ref, carry_b, cin_ref, fin_ref, acc_da, k_steps)
        for q in range(N_Q):
            cols = slice(q * 256, (q + 1) * 256)
            lam = g_ref[:, q * Q_W:(q + 1) * Q_W].astype(BF16)
            keep_own(acc_wb, q, _dot_tn(up[:, cols], lam))
            dup_ref[:, cols] = (_dot_nt(lam, wb_ref[q]) + d_ref[:, cols] * dy[:, cols].astype(F32)).astype(BF16)
        du_ref[...] = _dot(pmt_ref[...], dup_ref[...]).astype(BF16)

        @pl.when(i == nblk - 1)
        def _():
            da_ref[...] = acc_da[...]
            dd_ref[...] = acc_dd[...]
            lane = lax.broadcasted_iota(jnp.int32, (16 * G_H, 128), 1)
            row = lax.broadcasted_iota(jnp.int32, (16 * G_H, 128), 0)
            own = lane // G_P == (row // G_H) % 2
            spread = (lax.broadcasted_iota(jnp.int32, (G_P, 128), 1) % G_P ==
                      lax.broadcasted_iota(jnp.int32, (G_P, 128), 0)).astype(F32)
            for acc, out in ((acc_wb, dbb_ref), (acc_wct, dcc_ref)):
                for half in range(2):
                    for q in range(N_Q):
                        kept = jnp.where(own, acc[q, :, half * 128:(half + 1) * 128], 0.0)
                        out[half, q] = lax.dot_general(kept, spread, (((1,), (1,)), ((), ())),
                                                       preferred_element_type=F32, precision=lax.Precision.HIGHEST)
            _hosted_copies(items, src_refs, recv_refs, *sems, act="wait")

    rev = lambda c: pl.BlockSpec((tb, D), lambda i: (nblk - 1 - i, c))
    recv = [jax.ShapeDtypeStruct((N_DEV, shard_rows, D), F32)] * n_mat + \
           [jax.ShapeDtypeStruct((N_DEV, dpool_w.shape[0], pool_rows, dpool_w.shape[2]), F32),
            jax.ShapeDtypeStruct((N_DEV, D, W_IN_SHARD), BF16)]
    return _pcall(body, name="ssm_bwd", grid=(nblk,),
                  out_shape=(jax.ShapeDtypeStruct((rows, D), BF16),
                             jax.ShapeDtypeStruct((2, N_Q, 16 * G_H, G_P), F32),
                             jax.ShapeDtypeStruct((2, N_Q, 16 * G_H, G_P), F32),
                             jax.ShapeDtypeStruct((1, N_STATE), F32), jax.ShapeDtypeStruct((1, D), F32), *recv),
                  in_specs=[rev(0), rev(0), pl.BlockSpec((tb, N_STATE), lambda i: (nblk - 1 - i, 0)),
                            pl.BlockSpec((None, 1, N_STATE), lambda i: (nblk - 1 - i, 0, 0)),
                            _full((tb, tb)),
                            _full((N_Q, 256, Q_W), single=True), _full((N_Q, 256, Q_W), single=True),
                            _full((k_steps, N_STATE)), _full((1, D))] + [ANY] * n_h,
                  out_specs=(rev(0), _full((2, N_Q, 16 * G_H, G_P)), _full((2, N_Q, 16 * G_H, G_P)),
                             _full((1, N_STATE)), _full((1, D)), *([ANY] * n_h)),
                  scratch_shapes=[pltpu.VMEM((tb, N_STATE), F32), pltpu.VMEM((1, N_STATE), F32),
                                  pltpu.VMEM((SUBLANES, N_STATE), F32),
                                  pltpu.VMEM((N_Q, 16 * G_H, 256), F32), pltpu.VMEM((N_Q, 16 * G_H, 256), F32),
                                  pltpu.VMEM((1, N_STATE), F32), pltpu.VMEM((1, D), F32),
                                  pltpu.VMEM((tb, D), BF16)] + _sem_scratch(items),
                  compiler_params=_params(("arbitrary",), vmem=60 * 1024 * 1024),
                  )(dyp, up, states, carries, pmt, wb, wct, ptab, dvec, *hosted)


def _pool_bwd(dyp, pooled, proj, pool_w, pool_scale):
    rows = dyp.shape[0]
    tb = _tb(rows, 512)
    nblk = rows // tb

    def body(dy_ref, pooled_ref, z_ref, pw_ref, ps_ref, dp_ref, dpw_ref, dps_ref, ahead_ref):
        i = pl.program_id(0)
        blk = nblk - 1 - i

        @pl.when(i == 0)
        def _():
            ahead_ref[...] = jnp.zeros_like(ahead_ref)
            dpw_ref[...] = jnp.zeros_like(dpw_ref)
            dps_ref[...] = jnp.zeros_like(dps_ref)

        inv_counts = _inv_counts(tb, blk * tb)
        silu_z, dsilu_z = _silu_parts(z_ref[...].astype(F32))
        dy = dy_ref[...].astype(F32)
        for g, w in enumerate(POOL_WINDOWS):
            cols = slice(g * 256, (g + 1) * 256)
            pooled_b = pooled_ref[:, cols]
            mixed = _dot(pooled_b, pw_ref[g])
            scale = ps_ref[:, cols]
            dp_ref[:, D + g * 256:D + (g + 1) * 256] = (dy[:, cols] * (mixed * scale) * dsilu_z[:, cols]).astype(BF16)
            dms = dy[:, cols] * silu_z[:, cols]
            dps_ref[:, cols] += jnp.sum(dms * mixed, axis=0, keepdims=True)
            dmixed = (dms * scale).astype(BF16)
            dpw_ref[g] += _dot_tn(pooled_b, dmixed)
            dpooled = _dot_nt(dmixed, pw_ref[g])
            ratio = dpooled * inv_counts[g]
            acc = jnp.concatenate([ratio, ahead_ref[:, cols]], axis=0)
            ahead_ref[:, cols] = ratio[:HALO, :]
            s = 1
            while s < w:
                acc = acc + pltpu.roll(acc, tb + HALO - s, axis=0)
                s *= 2
            dp_ref[:, cols] = (acc[:tb, :] - dpooled).astype(BF16)

    rev = lambda c: pl.BlockSpec((tb, D), lambda i: (nblk - 1 - i, c))
    return _pcall(body, name="pool_bwd", grid=(nblk,),
                  out_shape=(jax.ShapeDtypeStruct((rows, 2 * D), BF16), jax.ShapeDtypeStruct((4, 256, 256), F32),
                             jax.ShapeDtypeStruct((1, D), F32)),
                  in_specs=[rev(0), rev(0), rev(1), _full((4, 256, 256)), _full((1, D))],
                  out_specs=(pl.BlockSpec((tb, 2 * D), lambda i: (nblk - 1 - i, 0)), _full((4, 256, 256)),
                             _full((1, D))),
                  scratch_shapes=[pltpu.VMEM((HALO, D), F32)],
                  compiler_params=_params(("arbitrary",)))(dyp, pooled, proj, pool_w, pool_scale)


def _dproj_specs(tb):
    return [pl.BlockSpec((tb, 2 * D), lambda i: (i, 0)), pl.BlockSpec((tb, D), lambda i: (i, 0)),
            pl.BlockSpec((tb, D), lambda i: (i, 0)), pl.BlockSpec((tb, 2 * D), lambda i: (i, 0))]


def _in_proj_bwd_x(x, dy, dpp, dus, dzs, dpg, mod3, norm_pre, w_in, dw_in_ssm, recv_w_in):
    rows = x.shape[0]
    tb = _tb(rows, 512)
    nblk = rows // tb
    items = [_w_in_block_item(0, 0, j, ssm_part=True) for j in range(W_IN_SHARD // W_IN_BLOCK)]
    sums_item = [_Item(0, 0, _whole, _slot)]

    def body(x_ref, dy_ref, dpp_ref, dus_ref, dzs_ref, dpg_ref, mod_ref, np_ref, w_ref,
             dw_src, _, gx_ref, recv_w, recv_sums, vec_ref, ssem, rsem, lsem, *sums_sems):
        src_refs, recv_refs, sems = (dw_src,), (recv_w,), (ssem, rsem, lsem)

        @pl.when(pl.program_id(0) == 0)
        def _():
            _hosted_copies(items, src_refs, recv_refs, *sems, act="start")
            vec_ref[...] = jnp.zeros_like(vec_ref)

        dh = _dot_nt(dpp_ref[...], w_ref[:, 0:2 * D])
        dh += _dot_nt(dus_ref[...], w_ref[:, 2 * D:3 * D])
        dh += _dot_nt(dzs_ref[...], w_ref[:, 3 * D:4 * D])
        dh += _dot_nt(dpg_ref[...], w_ref[:, 4 * D:6 * D])
        xn, r, _ = _prenorm(x_ref[...], mod_ref[...], np_ref[...])
        one_scale = 1.0 + mod_ref[1:2, :]
        vec_ref[0:1, :] += jnp.sum(dh, axis=0, keepdims=True)
        vec_ref[1:2, :] += jnp.sum(dh * xn, axis=0, keepdims=True) * np_ref[...]
        vec_ref[2:3, :] += jnp.sum(dh * xn, axis=0, keepdims=True) * one_scale
        gx_ref[...] = dy_ref[...] + _rms_bwd(dh * (np_ref[...] * one_scale), xn, r)

        @pl.when(pl.program_id(0) == nblk - 1)
        def _():
            _hosted_copies(sums_item, (vec_ref,), (recv_sums,), *sums_sems, act="start")
            _hosted_copies(items, src_refs, recv_refs, *sems, act="wait")
            _hosted_copies(sums_item, (vec_ref,), (recv_sums,), *sums_sems, act="wait")

    row = pl.BlockSpec((tb, D), lambda i: (i, 0))
    recv = (jax.ShapeDtypeStruct(recv_w_in.shape, recv_w_in.dtype), jax.ShapeDtypeStruct((N_DEV, 3, D), F32))
    return _pcall(body, name="in_proj_bwd_x", grid=(nblk,),
                  out_shape=(jax.ShapeDtypeStruct((rows, D), F32), *recv),
                  in_specs=[row, row] + _dproj_specs(tb) + [_full((3, D)), _full((1, D)),
                                                            _full((D, N_IN), single=True)] + [ANY] * 2,
                  out_specs=(row, ANY, ANY),
                  input_output_aliases={10: 1},
                  scratch_shapes=[pltpu.VMEM((3, D), F32)] + _sem_scratch(items) + _sem_scratch(sums_item),
                  compiler_params=_params(("arbitrary",)))(x, dy, dpp, dus, dzs, dpg, mod3, norm_pre, w_in,
                                                           dw_in_ssm, recv_w_in)


def _in_proj_bwd_w(name, x, dparts, mod3, norm_pre, gathered=()):
    rows = x.shape[0]
    tb = _tb(rows, 512)
    nblk = rows // tb
    widths = [p.shape[1] for p in dparts]
    n_p, n_g = len(dparts), len(gathered)
    items = [_Item(t, t, _whole, _slot) for t in range(n_g)]

    def body(x_ref, *rest):
        part_refs, (mod_ref, np_ref) = rest[:n_p], rest[n_p:n_p + 2]
        src_refs, dw_ref = rest[n_p + 2:n_p + 2 + n_g], rest[n_p + 2 + n_g]
        recv_refs, (acc, *sems) = rest[n_p + 3 + n_g:n_p + 3 + 2 * n_g], rest[n_p + 3 + 2 * n_g:]
        i = pl.program_id(0)

        @pl.when(i == 0)
        def _():
            if n_g:
                _hosted_copies(items, src_refs, recv_refs, *sems, act="start")
            acc[...] = jnp.zeros_like(acc)

        _, _, h = _prenorm(x_ref[...], mod_ref[...], np_ref[...])
        ht = h.astype(BF16)
        lo = 0
        for ref, w in zip(part_refs, widths):
            acc[:, lo:lo + w] += _dot_tn(ht, ref[...])
            lo += w

        @pl.when(i == nblk - 1)
        def _():
            dw_ref[...] = acc[...].astype(BF16)
            if n_g:
                _hosted_copies(items, src_refs, recv_refs, *sems, act="wait")

    row = pl.BlockSpec((tb, D), lambda i: (i, 0))
    out = _pcall(body, name=name, grid=(nblk,),
                 out_shape=(jax.ShapeDtypeStruct((D, sum(widths)), BF16),
                            *[jax.ShapeDtypeStruct((N_DEV,) + g.shape, g.dtype) for g in gathered]),
                 in_specs=[row] + [pl.BlockSpec((tb, w), lambda i: (i, 0)) for w in widths] +
                          [_full((3, D)), _full((1, D))] + [ANY] * n_g,
                 out_specs=(_full((D, sum(widths))), *([ANY] * n_g)),
                 scratch_shapes=[pltpu.VMEM((D, sum(widths)), F32)] + (_sem_scratch(items) if n_g else []),
                 compiler_params=_params(("arbitrary",)))(x, *dparts, mod3, norm_pre, *gathered)
    return out if n_g else out[0]


def _adamw_math(w, g, m, v):
    m = ADAM_B1 * m + (1.0 - ADAM_B1) * g
    v = ADAM_B2 * v + (1.0 - ADAM_B2) * (g * g)
    m_hat = m / (1.0 - ADAM_B1 ** ADAM_STEP)
    v_hat = v / (1.0 - ADAM_B2 ** ADAM_STEP)
    delta = -ADAM_LR * (m_hat / (jnp.sqrt(v_hat) + ADAM_EPS) + ADAM_WD * w)
    return delta, m, v


def _sum_sources(ref):
    g = ref[0].astype(F32)
    for s in range(1, N_DEV):
        g = g + ref[s].astype(F32)
    return g


def _adamw_reduce(name, parts, w, m, v):
    r, c = w.shape
    tr = r if r * c <= 256 * 1024 else max(8, (256 * 1024 // c) // 8 * 8)
    while r % tr:
        tr -= 8

    def body(p_ref, w_ref, m_ref, v_ref, g_ref, d_ref, nm_ref, nv_ref):
        g = _sum_sources(p_ref)
        g_ref[...] = g
        d_ref[...], nm_ref[...], nv_ref[...] = _adamw_math(w_ref[...], g, m_ref[...], v_ref[...])

    blk = pl.BlockSpec((tr, c), lambda i: (i, 0))
    return _pcall(body, name=name, grid=(r // tr,),
                  out_shape=tuple([jax.ShapeDtypeStruct((r, c), F32)] * 4),
                  in_specs=[pl.BlockSpec((N_DEV, tr, c), lambda i: (0, i, 0)), blk, blk, blk],
                  out_specs=(blk, blk, blk, blk),
                  compiler_params=_params(("arbitrary",)))(parts, w, m, v)


def _adamw_small(gs, ws, ms, vs):
    n = len(gs)

    def body(*refs):
        ins, outs = refs[:4 * n], refs[4 * n:]
        for t in range(n):
            g_ref, w_ref, m_ref, v_ref = ins[4 * t:4 * t + 4]
            outs[3 * t][...], outs[3 * t + 1][...], outs[3 * t + 2][...] = _adamw_math(
                w_ref[...], g_ref[...], m_ref[...], v_ref[...])

    vm = pl.BlockSpec(memory_space=pltpu.VMEM)
    flat = [a for t in range(n) for a in (gs[t], ws[t], ms[t], vs[t])]
    return _pcall(body, name="adamw_small",
                  out_shape=tuple(jax.ShapeDtypeStruct(w.shape, F32) for w in ws for _ in range(3)),
                  in_specs=[vm] * (4 * n), out_specs=tuple([vm] * (3 * n)), compiler_params=_params())(*flat)


def _sum_small(parts):
    n = len(parts)

    def body(*refs):
        for t in range(n):
            refs[n + t][...] = _sum_sources(refs[t])

    vm = pl.BlockSpec(memory_space=pltpu.VMEM)
    return _pcall(body, name="sum_small",
                  out_shape=tuple(jax.ShapeDtypeStruct(p.shape[1:], F32) for p in parts),
                  in_specs=[vm] * n, out_specs=tuple([vm] * n), compiler_params=_params())(*parts)


def _ada_update(c_all, dmod_cols, w, m, v):
    def body(c_ref, dm_ref, w_ref, m_ref, v_ref, g_ref, d_ref, nm_ref, nv_ref):
        ca = c_ref[...]
        g = lax.dot_general(ca * jax.nn.sigmoid(ca), dm_ref[...], (((0,), (0,)), ((), ())),
                            preferred_element_type=F32, precision=lax.Precision.HIGHEST)
        g_ref[...] = g
        d_ref[...], nm_ref[...], nv_ref[...] = _adamw_math(w_ref[...], g, m_ref[...], v_ref[...])

    vm = pl.BlockSpec(memory_space=pltpu.VMEM)
    return _pcall(body, name="ada_update", out_shape=tuple([jax.ShapeDtypeStruct(w.shape, F32)] * 4),
                  in_specs=[vm] * 5, out_specs=(vm, vm, vm, vm), compiler_params=_params())(c_all, dmod_cols, w, m, v)


def kernel(x, c, w_ada, b_ada, norm_pre, norm_post, w_in, pool_w, pool_scale, ssm_a_re, ssm_a_im, ssm_log_dt, ssm_b_re, ssm_b_im, ssm_c_re, ssm_c_im, ssm_d, glu_w, glu_b, w_branch_pool, w_branch_ssm, w_out, loss_target, m_w_ada, m_b_ada, m_norm_pre, m_norm_post, m_w_in, m_pool_w, m_pool_scale, m_ssm_a_re, m_ssm_a_im, m_ssm_log_dt, m_ssm_b_re, m_ssm_b_im, m_ssm_c_re, m_ssm_c_im, m_ssm_d, m_glu_w, m_glu_b, m_w_branch_pool, m_w_branch_ssm, m_w_out, v_w_ada, v_b_ada, v_norm_pre, v_norm_post, v_w_in, v_pool_w, v_pool_scale, v_ssm_a_re, v_ssm_a_im, v_ssm_log_dt, v_ssm_b_re, v_ssm_b_im, v_ssm_c_re, v_ssm_c_im, v_ssm_d, v_glu_w, v_glu_b, v_w_branch_pool, v_w_branch_ssm, v_w_out):
    given = dict(locals())
    me = _flat(_me())
    rows = x.shape[1]
    x2 = x[0]
    target = loss_target[0]
    ada_cols = w_ada.shape[2]

    b_ada_s = lax.dynamic_slice(b_ada, (0, me * ada_cols), (1, ada_cols))
    c_all, mod_rows = _ada_exchange(c, w_ada[0], b_ada_s)
    mod3 = mod_rows.reshape(3, D)

    shards = _cast_shards([w_in[0], pool_w[0], glu_w[0], w_branch_pool[0], w_branch_ssm[0], w_out[0]])

    tb_ssm = _tb(rows, 256)
    k_steps = tb_ssm // SUBLANES
    a_re, a_im = ssm_a_re[0], ssm_a_im[0]
    log_dt = ssm_log_dt[0].reshape(GROUPS, 1)
    b_re_t, b_im_t = ssm_b_re[0].transpose(0, 2, 1), ssm_b_im[0].transpose(0, 2, 1)
    wb, wct, pow_re, pow_im = _s5_prep(a_re, a_im, log_dt, b_re_t, b_im_t, ssm_c_re[0], ssm_c_im[0], k_steps)
    ptab = _state_layout(pow_re, pow_im)
    dvec = ssm_d[0].reshape(1, D)
    pm = _perm_matrix(tb_ssm)
    pmt = pm.T

    proj, w_in_g, pool_w_g, glu_g = _in_proj(x2, mod3, norm_pre, shards[0], shards[1:3])
    y_pool, pooled = _pool_fwd(proj, pool_w_g, pool_scale)
    y_ssm, ys_pre, carries, states, glu_gate, z_perm, u_perm, wbp_g, wbs_g, wout_g = _ssm_fwd(
        proj, pm, pmt, wb, wct, ptab, dvec, glu_g, glu_b, shards[3:])
    loss_part, dy, dyp, dys, dpg, dwbp, dwbs, dwout, head_vec = _head(
        x2, target, proj, y_pool, y_ssm, mod3, norm_post, wbp_g, wbs_g, wout_g)

    dpp, dpool_w, dpool_scale = _pool_bwd(dyp, pooled, proj, pool_w_g, pool_scale)
    dw_in_rest = _in_proj_bwd_w("in_proj_bwd_w_rest", x2, [dpp, dpg], mod3, norm_pre)
    dy_pre, dzs, dglu_w, dglu_b = _glu_bwd(dys, z_perm, ys_pre, glu_gate, pm, pmt, glu_g)
    dus, dbb, dcc, dabar, dd, p_glu, p_wbp, p_wbs, p_wout, p_pool_w, p_w_in = _ssm_bwd(
        dy_pre, u_perm, states, carries, pmt, wb, wct, ptab, dvec, [dglu_w, dwbp, dwbs, dwout], dpool_w,
        dw_in_rest)

    small32 = jnp.concatenate([head_vec, dpool_scale, dglu_b, dd, jnp.broadcast_to(loss_part, (1, D)),
                               jnp.zeros((2, D), F32), dabar.reshape(8, D)], axis=0)
    small16 = jnp.concatenate([dbb.reshape(2 * GROUPS, D), dcc.reshape(2 * GROUPS, D)], axis=0).astype(BF16)
    dw_in_ssm, p_small32, p_small16 = _in_proj_bwd_w("in_proj_bwd_w_ssm", x2, [dus, dzs], mod3, norm_pre,
                                                     gathered=(small32, small16))
    grad_x, p_w_in, p_pre = _in_proj_bwd_x(x2, dy, dpp, dus, dzs, dpg, mod3, norm_pre, w_in_g, dw_in_ssm, p_w_in)

    tot32, tot16, tot_pre = _sum_small([p_small32, p_small16, p_pre])
    d_abar_re, d_abar_im = _state_unlayout(tot32[8:16].reshape(N_STATE))
    d_bb_re, d_bb_im = tot16[0:64].reshape(GROUPS, G_H, G_P), tot16[64:128].reshape(GROUPS, G_H, G_P)
    g_a_re, g_a_im, g_log_dt, g_b_re_t, g_b_im_t = _s5_prep_bwd(
        a_re, a_im, log_dt, b_re_t, b_im_t, d_abar_re, d_abar_im, d_bb_re, d_bb_im)

    grads, deltas, new_m, new_v = {}, {}, {}, {}

    small = []

    def small_update(name, g2):
        small.append((name, g2))

    def shard_update(name, parts):
        shape = given[name].shape
        r2 = parts.shape[1:] if parts.ndim == 3 else (parts.shape[1] * parts.shape[2], parts.shape[3])
        w2, m2, v2 = (given[p + name].reshape(r2) for p in ("", "m_", "v_"))
        out = _adamw_reduce("adamw_" + name, parts.reshape((N_DEV,) + tuple(r2)), w2, m2, v2)
        grads[name], deltas[name], new_m[name], new_v[name] = (a.reshape(shape) for a in out)

    dmod_all = jnp.concatenate([p_pre[:, 0:2, :], p_small32[:, 0:1, :]], axis=1).reshape(N_DEV, 3 * D)
    dmod_cols = lax.dynamic_slice(dmod_all, (0, me * ada_cols), (N_DEV, ada_cols))
    out = _ada_update(c_all, dmod_cols, w_ada[0], m_w_ada[0], v_w_ada[0])
    grads['w_ada'], deltas['w_ada'], new_m['w_ada'], new_v['w_ada'] = (a.reshape(w_ada.shape) for a in out)

    small_update('b_ada', jnp.concatenate([tot_pre[0:2], tot32[0:1]], axis=0).reshape(1, 3 * D))
    small_update('norm_pre', tot_pre[2:3])
    small_update('norm_post', tot32[1:2])
    small_update('pool_scale', tot32[2:3])
    small_update('glu_b', tot32[3:4])
    small_update('ssm_d', tot32[4:5])
    small_update('ssm_a_re', g_a_re)
    small_update('ssm_a_im', g_a_im)
    small_update('ssm_log_dt', g_log_dt.reshape(1, GROUPS))
    small_update('ssm_b_re', g_b_re_t.transpose(0, 2, 1).reshape(GROUPS, G_P * G_H))
    small_update('ssm_b_im', g_b_im_t.transpose(0, 2, 1).reshape(GROUPS, G_P * G_H))
    small_update('ssm_c_re', tot16[128:192])
    small_update('ssm_c_im', -tot16[192:256])
    flat = _adamw_small([g2 for _, g2 in small],
                        *[[given[p + name].reshape(g2.shape) for name, g2 in small] for p in ("", "m_", "v_")])
    for t, (name, g2) in enumerate(small):
        shape = given[name].shape
        grads[name], deltas[name], new_m[name], new_v[name] = (
            a.reshape(shape) for a in (g2, *flat[3 * t:3 * t + 3]))
    shard_update('w_in', p_w_in)
    shard_update('pool_w', p_pool_w)
    shard_update('glu_w', p_glu)
    shard_update('w_branch_pool', p_wbp)
    shard_update('w_branch_ssm', p_wbs)
    shard_update('w_out', p_wout)

    return (tot32[5, 0], grad_x[None], *[grads[n] for n in WEIGHTS], *[deltas[n] for n in WEIGHTS],
            *[new_m[n] for n in WEIGHTS], *[new_v[n] for n in WEIGHTS])
```

```python
import math
from typing import Callable, NamedTuple, Optional

import jax
import jax.numpy as jnp
from jax import lax
from jax.experimental import pallas as pl
from jax.experimental.pallas import tpu as pltpu

F32 = jnp.float32
BF16 = jnp.bfloat16
MESH = pl.DeviceIdType.MESH

D = 1024
N_DEV = 8
N_IN = 6 * D
GROUPS = 64
G_H = 16
G_P = 64
N_Q = 4
Q_W = 2 * 16 * G_P
N_STATE = N_Q * Q_W
POOL_WINDOWS = (2, 4, 8, 16)
HALO = 16
RMS_EPS = 1e-6
SUBLANES = 8
LANE_CHUNK = 512
SCAN_UNROLL = 2
VMEM_LIMIT = 56 * 1024 * 1024

ADAM_LR = 0.001
ADAM_B1 = 0.9
ADAM_B2 = 0.999
ADAM_EPS = 1e-08
ADAM_WD = 0.01
ADAM_STEP = 10

WEIGHTS = ['w_ada', 'b_ada', 'norm_pre', 'norm_post', 'w_in', 'pool_w', 'pool_scale', 'ssm_a_re',
           'ssm_a_im', 'ssm_log_dt', 'ssm_b_re', 'ssm_b_im', 'ssm_c_re', 'ssm_c_im', 'ssm_d', 'glu_w',
           'glu_b', 'w_branch_pool', 'w_branch_ssm', 'w_out']


def _pcall(body, **kw):
    return pl.pallas_call(body, **kw)


def _params(sem=None, vmem=VMEM_LIMIT):
    return pltpu.CompilerParams(dimension_semantics=sem, vmem_limit_bytes=vmem)


def _tb(rows, pref):
    return pref if rows % pref == 0 and rows // pref >= 2 else rows // 2


def _full(shape, single=False):
    nd = len(shape)
    if single:
        return pl.BlockSpec(shape, lambda i: (0,) * nd, pipeline_mode=pl.Buffered(1))
    return pl.BlockSpec(shape, lambda i: (0,) * nd)


ANY = pl.BlockSpec(memory_space=pl.ANY)


def _me():
    return lax.axis_index("x"), lax.axis_index("y"), lax.axis_index("c")


def _flat(p):
    return 4 * p[0] + 2 * p[1] + p[2]


def _peer(k):
    x, y, c = _me()
    return (1 - x if k & 4 else x, 1 - y if k & 2 else y, 1 - c if k & 1 else c)


def _silu_parts(z):
    s = jax.nn.sigmoid(z)
    return z * s, s * (1.0 + z * (1.0 - s))


_GELU_C = math.sqrt(2.0 / math.pi)


def _gelu_parts(x):
    x2 = x * x
    t = jnp.tanh(_GELU_C * (x + 0.044715 * x * x2))
    g = 0.5 * x * (1.0 + t)
    dg = 0.5 * (1.0 + t) + 0.5 * x * (1.0 - t * t) * (_GELU_C * (1.0 + 3.0 * 0.044715 * x2))
    return g, dg


def _dot(a, b):
    return jnp.dot(a, b, preferred_element_type=F32)


def _dot_nt(a, b):
    return lax.dot_general(a, b, (((1,), (1,)), ((), ())), preferred_element_type=F32)


def _dot_tn(a, b):
    return lax.dot_general(a, b, (((0,), (0,)), ((), ())), preferred_element_type=F32)


def _rms_parts(x):
    r = lax.rsqrt(jnp.mean(x * x, axis=-1, keepdims=True) + RMS_EPS)
    return x * r, r


def _rms_bwd(dxn, xn, r):
    return r * (dxn - xn * jnp.mean(dxn * xn, axis=-1, keepdims=True))


def _ada_exchange(c, w_ada_s, b_ada_s):
    cols = w_ada_s.shape[1]

    def body(c_ref, w_ref, b_ref, call_ref, mod_ref, part_ref, ssem, rsem, lsem):
        me3 = _me()
        me = _flat(me3)
        mine = pltpu.make_async_copy(c_ref, call_ref.at[pl.ds(me, 1), :], lsem.at[0])
        mine.start()
        sends = []
        for k in range(1, N_DEV):
            cp = pltpu.make_async_remote_copy(src_ref=c_ref, dst_ref=call_ref.at[pl.ds(me, 1), :],
                                              send_sem=ssem.at[k - 1], recv_sem=rsem.at[k - 1],
                                              device_id=_peer(k), device_id_type=MESH)
            cp.start()
            sends.append(cp)
        mine.wait()
        for k in range(1, N_DEV):
            p = _flat(_peer(k))
            pltpu.make_async_remote_copy(src_ref=c_ref, dst_ref=call_ref.at[pl.ds(p, 1), :],
                                         send_sem=ssem.at[k - 1], recv_sem=rsem.at[k - 1],
                                         device_id=_peer(k), device_id_type=MESH).wait_recv()
        for cp in sends:
            cp.wait_send()
        ca = call_ref[...]
        act = ca * jax.nn.sigmoid(ca)
        part_ref[...] = jnp.dot(act, w_ref[...], preferred_element_type=F32,
                                precision=lax.Precision.HIGHEST) + b_ref[...]
        own = pltpu.make_async_copy(part_ref.at[pl.ds(me, 1), :], mod_ref.at[pl.ds(me, 1), :], lsem.at[1])
        own.start()
        sends = []
        for k in range(1, N_DEV):
            p = _flat(_peer(k))
            s = N_DEV - 1 + k - 1
            cp = pltpu.make_async_remote_copy(src_ref=part_ref.at[pl.ds(p, 1), :],
                                              dst_ref=mod_ref.at[pl.ds(me, 1), :],
                                              send_sem=ssem.at[s], recv_sem=rsem.at[s],
                                              device_id=_peer(k), device_id_type=MESH)
            cp.start()
            sends.append(cp)
        own.wait()
        for k in range(1, N_DEV):
            p = _flat(_peer(k))
            s = N_DEV - 1 + k - 1
            pltpu.make_async_remote_copy(src_ref=part_ref.at[pl.ds(p, 1), :],
                                         dst_ref=mod_ref.at[pl.ds(p, 1), :],
                                         send_sem=ssem.at[s], recv_sem=rsem.at[s],
                                         device_id=_peer(k), device_id_type=MESH).wait_recv()
        for cp in sends:
            cp.wait_send()

    vm = pl.BlockSpec(memory_space=pltpu.VMEM)
    return _pcall(
        body, name="ada_exchange",
        out_shape=(jax.ShapeDtypeStruct((N_DEV, D), F32), jax.ShapeDtypeStruct((N_DEV, cols), F32)),
        in_specs=[vm, vm, vm], out_specs=(vm, vm),
        scratch_shapes=[pltpu.VMEM((N_DEV, cols), F32),
                        pltpu.SemaphoreType.DMA((2 * (N_DEV - 1),)),
                        pltpu.SemaphoreType.DMA((2 * (N_DEV - 1),)),
                        pltpu.SemaphoreType.DMA((2,))],
    )(c, w_ada_s, b_ada_s)


class _Item(NamedTuple):
    src: int
    out: int
    src_view: Callable
    dst_view: Callable
    pred: Optional[Callable] = None


def _when(pred, dest, fn):
    if pred is None:
        fn()
    else:
        pl.when(pred(dest))(fn)


def _n_sems(items):
    return len(items) * (N_DEV - 1)


def _hosted_copies(items, srcs, outs, ssem, rsem, lsem, act):
    me = _flat(_me())
    for t, it in enumerate(items):
        local = lambda t=t, it=it: pltpu.make_async_copy(
            it.src_view(srcs[it.src], me), it.dst_view(outs[it.out], me), lsem.at[t])
        if act == "start":
            _when(it.pred, me, lambda local=local: local().start())
        else:
            _when(it.pred, me, lambda local=local: local().wait())
    for k in range(1, N_DEV):
        p3 = _peer(k)
        p = _flat(p3)
        for t, it in enumerate(items):
            s = t * (N_DEV - 1) + k - 1
            send = lambda it=it, s=s, p=p, p3=p3: pltpu.make_async_remote_copy(
                src_ref=it.src_view(srcs[it.src], p), dst_ref=it.dst_view(outs[it.out], me),
                send_sem=ssem.at[s], recv_sem=rsem.at[s], device_id=p3, device_id_type=MESH)
            recv = lambda it=it, s=s, p=p, p3=p3: pltpu.make_async_remote_copy(
                src_ref=it.src_view(srcs[it.src], p), dst_ref=it.dst_view(outs[it.out], p),
                send_sem=ssem.at[s], recv_sem=rsem.at[s], device_id=p3, device_id_type=MESH)
            if act == "start":
                _when(it.pred, p, lambda send=send: send().start())
            else:
                _when(it.pred, me, lambda recv=recv: recv().wait_recv())
                _when(it.pred, p, lambda send=send: send().wait_send())


def _sem_scratch(items):
    return [pltpu.SemaphoreType.DMA((_n_sems(items),)), pltpu.SemaphoreType.DMA((_n_sems(items),)),
            pltpu.SemaphoreType.DMA((len(items),))]


def _whole(ref, dest):
    return ref


def _slot(ref, sender):
    return ref.at[sender]


def _rows_of(rows):
    return lambda ref, dev: ref.at[pl.ds(dev * rows, rows), :]


def _pool_rows_of(rows):
    return lambda ref, dev: ref.at[:, pl.ds(dev * rows, rows), :]


def _gather_item(src, out, dst_view):
    return _Item(src, out, _whole, dst_view)


def _scatter_item(src, out, src_view):
    return _Item(src, out, src_view, _slot)


W_IN_BLOCK = 256
W_IN_SHARD = N_IN // N_DEV
SSM_BLOCKS = (2 * D // W_IN_BLOCK, 4 * D // W_IN_BLOCK)


def _w_in_block_item(src, out, j, ssm_part):
    def block(dest):
        return (W_IN_SHARD // W_IN_BLOCK) * dest + j

    def in_ssm(dest):
        b = block(dest)
        return (b >= SSM_BLOCKS[0]) & (b < SSM_BLOCKS[1])

    def src_view(ref, dest):
        b = block(dest)
        local = b - SSM_BLOCKS[0] if ssm_part else jnp.where(b < SSM_BLOCKS[0], b, b - (SSM_BLOCKS[1] - SSM_BLOCKS[0]))
        local = jnp.clip(local, 0, ref.shape[1] // W_IN_BLOCK - 1)
        return ref.at[:, pl.ds(local * W_IN_BLOCK, W_IN_BLOCK)]

    def dst_view(ref, sender):
        return ref.at[sender, :, pl.ds(j * W_IN_BLOCK, W_IN_BLOCK)]

    pred = in_ssm if ssm_part else (lambda dest: jnp.logical_not(in_ssm(dest)))
    return _Item(src, out, src_view, dst_view, pred)


def _cast_shards(arrs):
    def body(*refs):
        n = len(refs) // 2
        for i in range(n):
            refs[n + i][...] = refs[i][...].astype(BF16)

    vm = pl.BlockSpec(memory_space=pltpu.VMEM)
    return _pcall(body, name="cast_shards",
                  out_shape=tuple(jax.ShapeDtypeStruct(a.shape, BF16) for a in arrs),
                  in_specs=[vm] * len(arrs), out_specs=tuple([vm] * len(arrs)),
                  compiler_params=_params())(*arrs)


def _s5_discretise(a_re, a_im, log_dt, b_re_t, b_im_t):
    dt = jnp.exp(log_dt)
    lam_re = jnp.minimum(a_re, -1e-4)
    lam_im = a_im
    mag = jnp.exp(lam_re * dt)
    abar_re = mag * jnp.cos(lam_im * dt)
    abar_im = mag * jnp.sin(lam_im * dt)
    den = lam_re * lam_re + lam_im * lam_im
    num_re = abar_re - 1.0
    f_re = (num_re * lam_re + abar_im * lam_im) / den
    f_im = (abar_im * lam_re - num_re * lam_im) / den
    f_re, f_im = f_re[:, None, :], f_im[:, None, :]
    bb_re = f_re * b_re_t - f_im * b_im_t
    bb_im = f_re * b_im_t + f_im * b_re_t
    return abar_re, abar_im, bb_re, bb_im


def _group_masks():
    spread = lax.broadcasted_iota(jnp.int32, (G_P, 16 * G_P), 1) % G_P == lax.broadcasted_iota(
        jnp.int32, (G_P, 16 * G_P), 0)
    own = lax.broadcasted_iota(jnp.int32, (16 * G_H, 16 * G_P), 0) // G_H == lax.broadcasted_iota(
        jnp.int32, (16 * G_H, 16 * G_P), 1) // G_P
    return spread, own


def _s5_prep(a_re, a_im, log_dt, b_re_t, b_im_t, c_re, c_im, n_pow):
    def body(ar_ref, ai_ref, ld_ref, br_ref, bi_ref, cr_ref, ci_ref, wb_ref, wct_ref, pr_ref, pi_ref):
        abar_re, abar_im, bb_re, bb_im = _s5_discretise(ar_ref[...], ai_ref[...], ld_ref[...], br_ref[...], bi_ref[...])
        spread, own = _group_masks()
        spread = spread.astype(BF16)
        for ref, parts in ((wb_ref, (bb_re, bb_im)), (wct_ref, (cr_ref[...], -ci_ref[...]))):
            for half, t in enumerate(parts):
                for q in range(N_Q):
                    blocks = t[q * 16:(q + 1) * 16].reshape(16 * G_H, G_P).astype(BF16)
                    dense = jnp.where(own, _dot(blocks, spread), 0.0)
                    ref[q, :, half * (Q_W // 2):(half + 1) * (Q_W // 2)] = dense.astype(BF16)
        p_re, p_im = abar_re, abar_im
        pr_ref[0] = p_re
        pi_ref[0] = p_im
        for k in range(1, n_pow):
            p_re, p_im = p_re * abar_re - p_im * abar_im, p_re * abar_im + p_im * abar_re
            pr_ref[k] = p_re
            pi_ref[k] = p_im

    vm = pl.BlockSpec(memory_space=pltpu.VMEM)
    return _pcall(body, name="s5_prep",
                  out_shape=(jax.ShapeDtypeStruct((N_Q, 16 * G_H, Q_W), BF16),
                             jax.ShapeDtypeStruct((N_Q, 16 * G_H, Q_W), BF16),
                             jax.ShapeDtypeStruct((n_pow, GROUPS, G_P), F32),
                             jax.ShapeDtypeStruct((n_pow, GROUPS, G_P), F32)),
                  in_specs=[vm] * 7, out_specs=(vm, vm, vm, vm), compiler_params=_params(),
                  )(a_re, a_im, log_dt, b_re_t, b_im_t, c_re, c_im)


def _s5_prep_bwd(a_re, a_im, log_dt, b_re_t, b_im_t, d_abar_re, d_abar_im, d_bb_re, d_bb_im):
    def body(ar_ref, ai_ref, ld_ref, br_ref, bi_ref, dar_ref, dai_ref, dbr_ref, dbi_ref,
             gar_ref, gai_ref, gld_ref, gbr_ref, gbi_ref):
        _, vjp = jax.vjp(_s5_discretise, ar_ref[...], ai_ref[...], ld_ref[...], br_ref[...], bi_ref[...])
        g = vjp((dar_ref[...], dai_ref[...], dbr_ref[...], dbi_ref[...]))
        gar_ref[...] = g[0]
        gai_ref[...] = g[1]
        gld_ref[...] = g[2]
        gbr_ref[...] = g[3]
        gbi_ref[...] = g[4]

    vm = pl.BlockSpec(memory_space=pltpu.VMEM)
    ins = (a_re, a_im, log_dt, b_re_t, b_im_t)
    return _pcall(body, name="s5_prep_bwd",
                  out_shape=tuple(jax.ShapeDtypeStruct(a.shape, F32) for a in ins),
                  in_specs=[vm] * 9, out_specs=tuple([vm] * 5), compiler_params=_params(),
                  )(*ins, d_abar_re, d_abar_im, d_bb_re, d_bb_im)


def _state_layout(re, im):
    lead = re.shape[:-2]
    r = re.reshape(lead + (N_Q, 1, 16 * G_P))
    i = im.reshape(lead + (N_Q, 1, 16 * G_P))
    return jnp.concatenate([r, i], axis=-2).reshape(lead + (N_STATE,))


def _state_unlayout(v):
    v4 = v.reshape(N_Q, 2, 16, G_P)
    return v4[:, 0].reshape(GROUPS, G_P), v4[:, 1].reshape(GROUPS, G_P)


def _perm_matrix(tb):
    k_steps = tb // SUBLANES
    r = jnp.arange(tb)
    src = (r % SUBLANES) * k_steps + r // SUBLANES
    return (src[:, None] == jnp.arange(tb)[None, :]).astype(BF16)


def _lane_chunks(q):
    for lc in range(Q_W // 2 // LANE_CHUNK):
        re = q * Q_W + lc * LANE_CHUNK
        yield re, re + Q_W // 2


def _steps(lo, hi, body, init):
    if hi - lo <= SCAN_UNROLL:
        for k in range(lo, hi):
            init = body(k, init)
        return init
    trips = (hi - lo) // SCAN_UNROLL

    def trip(j, carry):
        for u in range(SCAN_UNROLL):
            carry = body(lo + j * SCAN_UNROLL + u, carry)
        return carry

    carry = lax.fori_loop(0, trips, trip, init)
    for k in range(lo + trips * SCAN_UNROLL, hi):
        carry = body(k, carry)
    return carry


def _tile(k):
    if isinstance(k, int):
        return pl.ds(k * SUBLANES, SUBLANES)
    return pl.ds(pl.multiple_of(k * SUBLANES, SUBLANES), SUBLANES)


def _scan_forward(q, s_ref, p_ref, carry_ref, enter_ref, fin_ref, k_steps):
    for re, im in _lane_chunks(q):
        lr, li = pl.ds(re, LANE_CHUNK), pl.ds(im, LANE_CHUNK)
        a_re = jnp.broadcast_to(p_ref[0:1, lr], (SUBLANES, LANE_CHUNK))
        a_im = jnp.broadcast_to(p_ref[0:1, li], (SUBLANES, LANE_CHUNK))

        def local(k, st):
            sr, si = st
            rows = _tile(k)
            nr = a_re * sr - a_im * si + s_ref[rows, lr]
            ni = a_re * si + a_im * sr + s_ref[rows, li]
            s_ref[rows, lr] = nr
            s_ref[rows, li] = ni
            return nr, ni

        zero = jnp.zeros((SUBLANES, LANE_CHUNK), F32)
        fr, fi = _steps(0, k_steps, local, (zero, zero))
        fin_ref[:, lr] = fr
        fin_ref[:, li] = fi
        ak_re, ak_im = p_ref[k_steps - 1:k_steps, lr], p_ref[k_steps - 1:k_steps, li]
        c_re, c_im = carry_ref[:, lr], carry_ref[:, li]
        for seg in range(SUBLANES):
            enter_ref[seg:seg + 1, lr] = c_re
            enter_ref[seg:seg + 1, li] = c_im
            f_re, f_im = fin_ref[seg:seg + 1, lr], fin_ref[seg:seg + 1, li]
            c_re, c_im = f_re + ak_re * c_re - ak_im * c_im, f_im + ak_re * c_im + ak_im * c_re
        carry_ref[:, lr] = c_re
        carry_ref[:, li] = c_im
        e_re, e_im = enter_ref[:, lr], enter_ref[:, li]

        def fix(k, _):
            rows = _tile(k)
            p_re = p_ref[pl.ds(k, 1), lr]
            p_im = p_ref[pl.ds(k, 1), li]
            s_ref[rows, lr] = s_ref[rows, lr] + (p_re * e_re - p_im * e_im)
            s_ref[rows, li] = s_ref[rows, li] + (p_re * e_im + p_im * e_re)
            return 0

        _steps(0, k_steps, fix, 0)


def _scan_backward(q, g_ref, s_ref, p_ref, carry_ref, s_in_ref, fin_ref, da_ref, k_steps):
    seg_id = lax.broadcasted_iota(jnp.int32, (SUBLANES, LANE_CHUNK), 0)
    for re, im in _lane_chunks(q):
        lr, li = pl.ds(re, LANE_CHUNK), pl.ds(im, LANE_CHUNK)
        a_re = jnp.broadcast_to(p_ref[0:1, lr], (SUBLANES, LANE_CHUNK))
        a_im = jnp.broadcast_to(p_ref[0:1, li], (SUBLANES, LANE_CHUNK))

        def local(j, st):
            sr, si = st
            rows = _tile(k_steps - 1 - j)
            nr = a_re * sr + a_im * si + g_ref[rows, lr]
            ni = a_re * si - a_im * sr + g_ref[rows, li]
            g_ref[rows, lr] = nr
            g_ref[rows, li] = ni
            return nr, ni

        zero = jnp.zeros((SUBLANES, LANE_CHUNK), F32)
        fr, fi = _steps(0, k_steps, local, (zero, zero))
        fin_ref[:, lr] = fr
        fin_ref[:, li] = fi
        ak_re, ak_im = p_ref[k_steps - 1:k_steps, lr], p_ref[k_steps - 1:k_steps, li]
        c_re, c_im = carry_ref[:, lr], carry_ref[:, li]
        lam_in = [None] * SUBLANES
        for seg in reversed(range(SUBLANES)):
            lam_in[seg] = (c_re, c_im)
            f_re, f_im = fin_ref[seg:seg + 1, lr], fin_ref[seg:seg + 1, li]
            c_re, c_im = f_re + ak_re * c_re + ak_im * c_im, f_im + ak_re * c_im - ak_im * c_re
        carry_ref[:, lr] = c_re
        carry_ref[:, li] = c_im
        for seg in range(SUBLANES):
            fin_ref[seg:seg + 1, lr] = lam_in[seg][0]
            fin_ref[seg:seg + 1, li] = lam_in[seg][1]
        e_re, e_im = fin_ref[:, lr], fin_ref[:, li]

        def fix_with(k, acc, sp_re, sp_im):
            acc_re, acc_im = acc
            rows = _tile(k)
            p_re = p_ref[pl.ds(k_steps - 1 - k, 1), lr]
            p_im = p_ref[pl.ds(k_steps - 1 - k, 1), li]
            l_re = g_ref[rows, lr] + (p_re * e_re + p_im * e_im)
            l_im = g_ref[rows, li] + (p_re * e_im - p_im * e_re)
            g_ref[rows, lr] = l_re
            g_ref[rows, li] = l_im
            return acc_re + (l_re * sp_re + l_im * sp_im), acc_im + (l_im * sp_re - l_re * sp_im)

        def fix(k, acc):
            prev = _tile(k - 1)
            return fix_with(k, acc, s_ref[prev, lr], s_ref[prev, li])

        last = _tile(k_steps - 1)
        before_re = jnp.where(seg_id == 0, s_in_ref[:, lr], pltpu.roll(s_ref[last, lr], 1, axis=0))
        before_im = jnp.where(seg_id == 0, s_in_ref[:, li], pltpu.roll(s_ref[last, li], 1, axis=0))
        acc = fix_with(0, (zero, zero), before_re, before_im)
        acc_re, acc_im = _steps(1, k_steps, fix, acc)
        da_ref[:, lr] = da_ref[:, lr] + jnp.sum(acc_re, axis=0, keepdims=True)
        da_ref[:, li] = da_ref[:, li] + jnp.sum(acc_im, axis=0, keepdims=True)


def _prenorm(x, mod3, norm_pre):
    xn, r = _rms_parts(x)
    return xn, r, xn * norm_pre * (1.0 + mod3[1:2, :]) + mod3[0:1, :]


CHIP_FLIPS = (4, 2, 6)


def _shard_order(me):
    flips = [0, 1] + [f + c for f in CHIP_FLIPS for c in (0, 1)]
    return jnp.stack([me ^ f for f in flips]).astype(jnp.int32)


def _in_proj(x, mod3, norm_pre, w_in_s, shards):
    rows = x.shape[0]
    tb = _tb(rows, 2048)
    nblk = rows // tb
    n_sh = len(shards)
    last_step = N_DEV - 1
    items = [_gather_item(0, 0, _pool_rows_of(shards[0].shape[1]))] + \
            [_gather_item(t, t, _rows_of(shards[t].shape[0])) for t in range(1, n_sh)]

    def body(order_ref, x_ref, mod_ref, np_ref, w_src, *rest):
        src_refs, proj_ref, w_full, out_refs = rest[:n_sh], rest[n_sh], rest[n_sh + 1], rest[n_sh + 2:2 * n_sh + 2]
        h_scr, wg, ssem, rsem, lsem, *sems = rest[2 * n_sh + 2:]
        s, i = pl.program_id(0), pl.program_id(1)
        me3 = _me()
        me = _flat(me3)
        sibling = _peer(1)

        def own_copy(slot, k):
            return pltpu.make_async_remote_copy(src_ref=w_src, dst_ref=wg.at[me], send_sem=ssem.at[slot],
                                                recv_sem=rsem.at[slot], device_id=_peer(k), device_id_type=MESH)

        def passed_copy(j):
            p = _flat(_peer(CHIP_FLIPS[j]))
            return pltpu.make_async_remote_copy(src_ref=wg.at[p], dst_ref=wg.at[p], send_sem=ssem.at[4 + j],
                                                recv_sem=rsem.at[4 + j], device_id=sibling, device_id_type=MESH)

        def arrival(slot, flip):
            p = _flat(_peer(flip))
            pltpu.make_async_remote_copy(src_ref=w_src, dst_ref=wg.at[p], send_sem=ssem.at[slot],
                                         recv_sem=rsem.at[slot], device_id=sibling, device_id_type=MESH).wait_recv()

        def keep(t):
            p = order_ref[t]
            return pltpu.make_async_copy(wg.at[p], w_full.at[:, pl.ds(p * W_IN_SHARD, W_IN_SHARD)], lsem.at[1 + t])

        first = i == 0
        for t in range(last_step):
            pl.when(first & (s == t + 1))(lambda t=t: keep(t).start())

        @pl.when(first & (s == 0))
        def _():
            mine = pltpu.make_async_copy(w_src, wg.at[me], lsem.at[0])
            mine.start()
            own_copy(0, 1).start()
            for j, f in enumerate(CHIP_FLIPS[:2]):
                own_copy(1 + j, f).start()
            mine.wait()

        @pl.when(first & (s == 1))
        def _():
            arrival(0, 1)

        for j, f in enumerate(CHIP_FLIPS):
            @pl.when(first & (s == 2 + 2 * j))
            def _(j=j, f=f):
                arrival(1 + j, f)
                passed_copy(j).start()
                if j == 0:
                    own_copy(3, CHIP_FLIPS[2]).start()

            @pl.when(first & (s == 3 + 2 * j))
            def _(j=j, f=f):
                arrival(4 + j, f + 1)

        @pl.when(first & (s == last_step - 1))
        def _():
            _hosted_copies(items, src_refs, out_refs, *sems, act="start")

        rows_i = pl.ds(pl.multiple_of(i * tb, tb), tb)

        @pl.when(s == 0)
        def _():
            _, _, h = _prenorm(x_ref[...], mod_ref[...], np_ref[...])
            h_scr[rows_i, :] = h.astype(BF16)

        proj_ref[...] = _dot(h_scr[rows_i, :], wg[order_ref[s]]).astype(BF16)

        @pl.when((s == last_step) & (i == nblk - 1))
        def _():
            own_copy(0, 1).wait_send()
            for j, f in enumerate(CHIP_FLIPS):
                own_copy(1 + j, f).wait_send()
                passed_copy(j).wait_send()
            keep(last_step).start()
            for t in range(N_DEV):
                keep(t).wait()
            _hosted_copies(items, src_refs, out_refs, *sems, act="wait")

    full = [jax.ShapeDtypeStruct((4, 256, 256), BF16)] + [jax.ShapeDtypeStruct((D, D), BF16)] * (n_sh - 1)
    grid_spec = pltpu.PrefetchScalarGridSpec(
        num_scalar_prefetch=1, grid=(N_DEV, nblk),
        in_specs=[pl.BlockSpec((tb, D), lambda s, i, order: (jnp.where(s == 0, i, nblk - 1), 0)),
                  pl.BlockSpec((3, D), lambda s, i, order: (0, 0)), pl.BlockSpec((1, D), lambda s, i, order: (0, 0)),
                  ANY] + [ANY] * n_sh,
        out_specs=(pl.BlockSpec((tb, W_IN_SHARD), lambda s, i, order: (i, order[s])), ANY, *([ANY] * n_sh)),
        scratch_shapes=[pltpu.VMEM((rows, D), BF16), pltpu.VMEM((N_DEV, D, W_IN_SHARD), BF16),
                        pltpu.SemaphoreType.DMA((N_DEV - 1,)), pltpu.SemaphoreType.DMA((N_DEV - 1,)),
                        pltpu.SemaphoreType.DMA((1 + N_DEV,))] + _sem_scratch(items))
    return _pcall(body, name="in_proj", grid_spec=grid_spec,
                  out_shape=(jax.ShapeDtypeStruct((rows, N_IN), BF16), jax.ShapeDtypeStruct((D, N_IN), BF16), *full),
                  compiler_params=_params(("arbitrary", "arbitrary")),
                  )(_shard_order(_flat(_me())), x, mod3, norm_pre, w_in_s, *shards)


def _pool_windows(ext, tb, first_row):
    inv_counts = _inv_counts(tb, first_row)
    pooled = []
    for g, w in enumerate(POOL_WINDOWS):
        acc = ext[:, g * 256:(g + 1) * 256]
        tok = acc[HALO:, :]
        s = 1
        while s < w:
            acc = acc + pltpu.roll(acc, s, axis=0)
            s *= 2
        pooled.append(acc[HALO:, :] * inv_counts[g] - tok)
    return pooled, inv_counts


def _inv_counts(tb, first_row):
    pos = (first_row + lax.broadcasted_iota(jnp.int32, (tb, 1), 0) + 1).astype(F32)
    return [1.0 / jnp.minimum(pos, float(w)) for w in POOL_WINDOWS]


def _pool_fwd(proj, pool_w, pool_scale):
    rows = proj.shape[0]
    tb = _tb(rows, 512)
    hb = tb // HALO

    def body(u_ref, halo_ref, z_ref, pw_ref, ps_ref, y_ref, pooled_ref):
        i = pl.program_id(0)
        u = u_ref[...].astype(F32)
        halo = jnp.where(i > 0, halo_ref[...].astype(F32), 0.0)
        pooled, _ = _pool_windows(jnp.concatenate([halo, u], axis=0), tb, i * tb)
        silu_z, _ = _silu_parts(z_ref[...].astype(F32))
        for g in range(4):
            cols = slice(g * 256, (g + 1) * 256)
            pooled_b = pooled[g].astype(BF16)
            pooled_ref[:, cols] = pooled_b
            mixed = _dot(pooled_b, pw_ref[g])
            y_ref[:, cols] = (mixed * ps_ref[:, cols] * silu_z[:, cols]).astype(BF16)

    blk = pl.BlockSpec((tb, D), lambda i: (i, 0))
    return _pcall(body, name="pool_fwd", grid=(rows // tb,),
                  out_shape=(jax.ShapeDtypeStruct((rows, D), BF16), jax.ShapeDtypeStruct((rows, D), BF16)),
                  in_specs=[blk, pl.BlockSpec((HALO, D), lambda i: (jnp.maximum(i * hb - 1, 0), 0)),
                            pl.BlockSpec((tb, D), lambda i: (i, 1)),
                            _full((4, 256, 256)), _full((1, D))],
                  out_specs=(blk, blk),
                  compiler_params=_params(("arbitrary",)))(proj, proj, proj, pool_w, pool_scale)


def _ssm_fwd(proj, pm, pmt, wb, wct, ptab, dvec, glu_w, glu_b, shards):
    rows = proj.shape[0]
    tb = pm.shape[0]
    k_steps = tb // SUBLANES
    nblk = rows // tb
    n_sh = len(shards)
    items = [_gather_item(t, t, _rows_of(shards[t].shape[0])) for t in range(n_sh)]

    def body(u_ref, z_ref, pm_ref, pmt_ref, wb_ref, wct_ref, p_ref, d_ref, gw_ref, gb_ref, *rest):
        src_refs = rest[:n_sh]
        y_ref, ys_ref, carry_out_ref, s_ref, gate_ref, zp_ref, up_ref = rest[n_sh:n_sh + 7]
        out_refs = rest[n_sh + 7:2 * n_sh + 7]
        carry_ref, enter_ref, fin_ref, *sems = rest[2 * n_sh + 7:]

        @pl.when(pl.program_id(0) == 0)
        def _():
            _hosted_copies(items, src_refs, out_refs, *sems, act="start")
            carry_ref[...] = jnp.zeros_like(carry_ref)

        carry_out_ref[...] = carry_ref[...]
        up = _dot(pm_ref[...], u_ref[...]).astype(BF16)
        up_ref[...] = up

        for q in range(N_Q):
            s_ref[:, q * Q_W:(q + 1) * Q_W] = _dot(up[:, q * 256:(q + 1) * 256], wb_ref[q])
        for q in range(N_Q):
            _scan_forward(q, s_ref, p_ref, carry_ref, enter_ref, fin_ref, k_steps)
        for q in range(N_Q):
            cols = slice(q * 256, (q + 1) * 256)
            y = _dot_nt(s_ref[:, q * Q_W:(q + 1) * Q_W].astype(BF16), wct_ref[q])
            ys_ref[:, cols] = y + d_ref[:, cols] * up[:, cols].astype(F32)
        yg, _ = _gelu_parts(ys_ref[...])
        gate = jax.nn.sigmoid(_dot(yg.astype(BF16), gw_ref[...]) + gb_ref[...])
        gate_ref[...] = gate
        zp = _dot(pm_ref[...], z_ref[...])
        zp_ref[...] = zp.astype(BF16)
        silu_z, _ = _silu_parts(zp)
        y_ref[...] = _dot(pmt_ref[...], (yg * gate * silu_z).astype(BF16)).astype(BF16)

        @pl.when(pl.program_id(0) == nblk - 1)
        def _():
            _hosted_copies(items, src_refs, out_refs, *sems, act="wait")

    return _pcall(body, name="ssm_fwd", grid=(nblk,),
                  out_shape=(jax.ShapeDtypeStruct((rows, D), BF16), jax.ShapeDtypeStruct((rows, D), F32),
                             jax.ShapeDtypeStruct((nblk, 1, N_STATE), F32),
                             jax.ShapeDtypeStruct((rows, N_STATE), F32),
                             jax.ShapeDtypeStruct((rows, D), F32), jax.ShapeDtypeStruct((rows, D), BF16),
                             jax.ShapeDtypeStruct((rows, D), BF16),
                             *[jax.ShapeDtypeStruct((D, D), BF16)] * n_sh),
                  in_specs=[pl.BlockSpec((tb, D), lambda i: (i, 2)), pl.BlockSpec((tb, D), lambda i: (i, 3)),
                            _full((tb, tb)), _full((tb, tb)),
                            _full((N_Q, 256, Q_W), single=True), _full((N_Q, 256, Q_W), single=True),
                            _full((k_steps, N_STATE)), _full((1, D)), _full((D, D), single=True), _full((1, D))] +
                           [ANY] * n_sh,
                  out_specs=(pl.BlockSpec((tb, D), lambda i: (i, 0)), pl.BlockSpec((tb, D), lambda i: (i, 0)),
                             pl.BlockSpec((None, 1, N_STATE), lambda i: (i, 0, 0)),
                             pl.BlockSpec((tb, N_STATE), lambda i: (i, 0)),
                             *[pl.BlockSpec((tb, D), lambda i: (i, 0))] * 3, *([ANY] * n_sh)),
                  scratch_shapes=[pltpu.VMEM((1, N_STATE), F32),
                                  pltpu.VMEM((SUBLANES, N_STATE), F32), pltpu.VMEM((SUBLANES, N_STATE), F32)] +
                                 _sem_scratch(items),
                  compiler_params=_params(("arbitrary",)))(proj, proj, pm, pmt, wb, wct, ptab, dvec, glu_w, glu_b,
                                                           *shards)


def _head(x, target, proj, y_pool, y_ssm, mod3, norm_post, wbp, wbs, wout):
    rows = x.shape[0]
    tb = _tb(rows, 256)
    nblk = rows // tb
    n_feat = float(D)

    def body(x_ref, t_ref, gp_ref, gs_ref, yp_ref, ys_ref, mod_ref, npost_ref, wbp_ref, wbs_ref, wout_ref,
             loss_ref, dy_ref, dyp_ref, dys_ref, dg_ref, dwbp_hbm, dwbs_hbm, dwout_hbm, vec_ref,
             acc_bp, acc_bs, acc_out, acc_loss, acc_vec):
        i = pl.program_id(0)

        @pl.when(i == 0)
        def _():
            acc_bp[...] = jnp.zeros_like(acc_bp)
            acc_bs[...] = jnp.zeros_like(acc_bs)
            acc_out[...] = jnp.zeros_like(acc_out)
            acc_loss[...] = jnp.zeros_like(acc_loss)
            acc_vec[...] = jnp.zeros_like(acc_vec)

        gate = mod_ref[2:3, :]
        npost = npost_ref[...]
        yp, ys = yp_ref[...], ys_ref[...]
        sgp = jax.nn.sigmoid(gp_ref[...].astype(F32))
        sgs = jax.nn.sigmoid(gs_ref[...].astype(F32))
        pb = _dot(yp, wbp_ref[...])
        psm = _dot(ys, wbs_ref[...])
        mb = (sgp * pb + sgs * psm).astype(BF16)
        out = _dot(mb, wout_ref[...])
        on, r = _rms_parts(out)
        normed = on * npost
        diff = x_ref[...] + gate * normed - t_ref[...]
        acc_loss[...] += jnp.sum(diff * diff, axis=0, keepdims=True)
        dy = diff * (1.0 / n_feat)
        dy_ref[...] = dy
        acc_vec[0:1, :] += jnp.sum(dy * normed, axis=0, keepdims=True)
        dn = dy * gate
        acc_vec[1:2, :] += jnp.sum(dn * on, axis=0, keepdims=True)
        dout = _rms_bwd(dn * npost, on, r).astype(BF16)
        dm = _dot_nt(dout, wout_ref[...])
        dpb = (dm * sgp).astype(BF16)
        dps = (dm * sgs).astype(BF16)
        dg_ref[:, :D] = (dm * pb * sgp * (1.0 - sgp)).astype(BF16)
        dg_ref[:, D:] = (dm * psm * sgs * (1.0 - sgs)).astype(BF16)
        dyp_ref[...] = _dot_nt(dpb, wbp_ref[...]).astype(BF16)
        dys_ref[...] = _dot_nt(dps, wbs_ref[...]).astype(BF16)
        acc_out[...] += _dot_tn(mb, dout)
        acc_bp[...] += _dot_tn(yp, dpb)
        acc_bs[...] += _dot_tn(ys, dps)

        @pl.when(i == nblk - 1)
        def _():
            loss_ref[...] = 0.5 / n_feat * jnp.sum(acc_loss[...], axis=1, keepdims=True)
            vec_ref[...] = acc_vec[...]
            pltpu.sync_copy(acc_bp, dwbp_hbm)
            pltpu.sync_copy(acc_bs, dwbs_hbm)
            pltpu.sync_copy(acc_out, dwout_hbm)

    row = lambda c: pl.BlockSpec((tb, D), lambda i: (i, c))
    w = _full((D, D), single=True)
    return _pcall(body, name="head", grid=(nblk,),
                  out_shape=(jax.ShapeDtypeStruct((1, 1), F32), jax.ShapeDtypeStruct((rows, D), F32),
                             jax.ShapeDtypeStruct((rows, D), BF16), jax.ShapeDtypeStruct((rows, D), BF16),
                             jax.ShapeDtypeStruct((rows, 2 * D), BF16),
                             jax.ShapeDtypeStruct((D, D), F32), jax.ShapeDtypeStruct((D, D), F32),
                             jax.ShapeDtypeStruct((D, D), F32), jax.ShapeDtypeStruct((2, D), F32)),
                  in_specs=[row(0), row(0), row(4), row(5), row(0), row(0), _full((3, D)), _full((1, D)), w, w, w],
                  out_specs=(_full((1, 1)), row(0), row(0), row(0), pl.BlockSpec((tb, 2 * D), lambda i: (i, 0)),
                             ANY, ANY, ANY, _full((2, D))),
                  scratch_shapes=[pltpu.VMEM((D, D), F32), pltpu.VMEM((D, D), F32), pltpu.VMEM((D, D), F32),
                                  pltpu.VMEM((1, D), F32), pltpu.VMEM((2, D), F32)],
                  compiler_params=_params(("arbitrary",)))(x, target, proj, proj, y_pool, y_ssm, mod3, norm_post,
                                                           wbp, wbs, wout)


def _glu_bwd(dys, zp, ys_pre, gate, pm, pmt, glu_w):
    rows = dys.shape[0]
    tb = pm.shape[0]
    nblk = rows // tb

    def body(dys_ref, z_ref, ysp_ref, sg_ref, pm_ref, pmt_ref, gw_ref, dyp_ref, dz_ref, dgw_hbm, dgb_ref,
             acc_w, acc_b):
        i = pl.program_id(0)

        @pl.when(i == 0)
        def _():
            acc_w[...] = jnp.zeros_like(acc_w)
            acc_b[...] = jnp.zeros_like(acc_b)

        d_out = _dot(pm_ref[...], dys_ref[...])
        yg, dgelu = _gelu_parts(ysp_ref[...])
        ygb = yg.astype(BF16)
        sg = sg_ref[...]
        silu_z, dsilu_z = _silu_parts(z_ref[...].astype(F32))
        dz = d_out * (yg * sg) * dsilu_z
        dz_ref[...] = _dot(pmt_ref[...], dz.astype(BF16)).astype(BF16)
        dglu = d_out * silu_z
        dq = dglu * yg * sg * (1.0 - sg)
        dqb = dq.astype(BF16)
        acc_b[...] += jnp.sum(dq, axis=0, keepdims=True)
        acc_w[...] += _dot_tn(ygb, dqb)
        dyg = dglu * sg + _dot_nt(dqb, gw_ref[...])
        dyp_ref[...] = (dyg * dgelu).astype(BF16)

        @pl.when(i == nblk - 1)
        def _():
            dgb_ref[...] = acc_b[...]
            pltpu.sync_copy(acc_w, dgw_hbm)

    row = lambda c: pl.BlockSpec((tb, D), lambda i: (i, c))
    return _pcall(body, name="glu_bwd", grid=(nblk,),
                  out_shape=(jax.ShapeDtypeStruct((rows, D), BF16), jax.ShapeDtypeStruct((rows, D), BF16),
                             jax.ShapeDtypeStruct((D, D), F32), jax.ShapeDtypeStruct((1, D), F32)),
                  in_specs=[row(0), row(0), row(0), row(0), _full((tb, tb)), _full((tb, tb)),
                            _full((D, D), single=True)],
                  out_specs=(row(0), row(0), ANY, _full((1, D))),
                  scratch_shapes=[pltpu.VMEM((D, D), F32), pltpu.VMEM((1, D), F32)],
                  compiler_params=_params(("arbitrary",)))(dys, zp, ys_pre, gate, pm, pmt, glu_w)


def _ssm_bwd(dyp, up, states, carries, pmt, wb, wct, ptab, dvec, mat_grads, dpool_w, dw_in_rest):
    rows = dyp.shape[0]
    tb = pmt.shape[0]
    k_steps = tb // SUBLANES
    nblk = rows // tb
    n_mat = len(mat_grads)
    hosted = [*mat_grads, dpool_w, dw_in_rest]
    n_h = len(hosted)
    shard_rows = D // N_DEV
    pool_rows = dpool_w.shape[1] // N_DEV
    items = [_scatter_item(t, t, _rows_of(shard_rows)) for t in range(n_mat)] + \
            [_scatter_item(n_mat, n_mat, _pool_rows_of(pool_rows))] + \
            [_w_in_block_item(n_mat + 1, n_mat + 1, j, ssm_part=False) for j in range(W_IN_SHARD // W_IN_BLOCK)]
    n_in, n_out = 9, 5

    def body(*refs):
        dyp_ref, u_ref, s_ref, cin_ref, pmt_ref, wb_ref, wct_ref, p_ref, d_ref = refs[:n_in]
        src_refs = refs[n_in:n_in + n_h]
        du_ref, dbb_ref, dcc_ref, da_ref, dd_ref = refs[n_in + n_h:n_in + n_h + n_out]
        recv_refs = refs[n_in + n_h + n_out:n_in + 2 * n_h + n_out]
        (g_ref, carry_b, fin_ref, acc_wb, acc_wct, acc_da, acc_dd, dup_ref,
         *sems) = refs[n_in + 2 * n_h + n_out:]
        i = pl.program_id(0)

        @pl.when(i == 0)
        def _():
            _hosted_copies(items, src_refs, recv_refs, *sems, act="start")
            carry_b[...] = jnp.zeros_like(carry_b)
            acc_wb[...] = jnp.zeros_like(acc_wb)
            acc_wct[...] = jnp.zeros_like(acc_wct)
            acc_da[...] = jnp.zeros_like(acc_da)
            acc_dd[...] = jnp.zeros_like(acc_dd)

        def keep_own(acc, q, prod):
            for gl in range(16):
                r, c = slice(gl * G_H, (gl + 1) * G_H), (gl // 2) * 128
                acc[q, r, 0:128] += prod[r, c:c + 128]
                acc[q, r, 128:256] += prod[r, Q_W // 2 + c:Q_W // 2 + c + 128]

        dy = dyp_ref[...]
        up = u_ref[...]
        acc_dd[...] += jnp.sum(dy.astype(F32) * up.astype(F32), axis=0, keepdims=True)
        for q in range(N_Q):
            cols = slice(q * 256, (q + 1) * 256)
            g_ref[:, q * Q_W:(q + 1) * Q_W] = _dot(dy[:, cols], wct_ref[q])
            keep_own(acc_wct, q, _dot_tn(dy[:, cols], s_ref[:, q * Q_W:(q + 1) * Q_W].astype(BF16)))
        for q in range(N_Q):
            _scan_backward(q, g_ref, s_ref, p_ref, carry_b, cin_ref, fin_ref, acc_da, k_steps)
        for q in range(N_Q):
            cols = slice(q * 256, (q + 1) * 256)
            lam = g_ref[:, q * Q_W:(q + 1) * Q_W].astype(BF16)
            keep_own(acc_wb, q, _dot_tn(up[:, cols], lam))
            dup_ref[:, cols] = (_dot_nt(lam, wb_ref[q]) + d_ref[:, cols] * dy[:, cols].astype(F32)).astype(BF16)
        du_ref[...] = _dot(pmt_ref[...], dup_ref[...]).astype(BF16)

        @pl.when(i == nblk - 1)
        def _():
            da_ref[...] = acc_da[...]
            dd_ref[...] = acc_dd[...]
            lane = lax.broadcasted_iota(jnp.int32, (16 * G_H, 128), 1)
            row = lax.broadcasted_iota(jnp.int32, (16 * G_H, 128), 0)
            own = lane // G_P == (row // G_H) % 2
            spread = (lax.broadcasted_iota(jnp.int32, (G_P, 128), 1) % G_P ==
                      lax.broadcasted_iota(jnp.int32, (G_P, 128), 0)).astype(F32)
            for acc, out in ((acc_wb, dbb_ref), (acc_wct, dcc_ref)):
                for half in range(2):
                    for q in range(N_Q):
                        kept = jnp.where(own, acc[q, :, half * 128:(half + 1) * 128], 0.0)
                        out[half, q] = lax.dot_general(kept, spread, (((1,), (1,)), ((), ())),
                                                       preferred_element_type=F32, precision=lax.Precision.HIGHEST)
            _hosted_copies(items, src_refs, recv_refs, *sems, act="wait")

    rev = lambda c: pl.BlockSpec((tb, D), lambda i: (nblk - 1 - i, c))
    recv = [jax.ShapeDtypeStruct((N_DEV, shard_rows, D), F32)] * n_mat + \
           [jax.ShapeDtypeStruct((N_DEV, dpool_w.shape[0], pool_rows, dpool_w.shape[2]), F32),
            jax.ShapeDtypeStruct((N_DEV, D, W_IN_SHARD), BF16)]
    return _pcall(body, name="ssm_bwd", grid=(nblk,),
                  out_shape=(jax.ShapeDtypeStruct((rows, D), BF16),
                             jax.ShapeDtypeStruct((2, N_Q, 16 * G_H, G_P), F32),
                             jax.ShapeDtypeStruct((2, N_Q, 16 * G_H, G_P), F32),
                             jax.ShapeDtypeStruct((1, N_STATE), F32), jax.ShapeDtypeStruct((1, D), F32), *recv),
                  in_specs=[rev(0), rev(0), pl.BlockSpec((tb, N_STATE), lambda i: (nblk - 1 - i, 0)),
                            pl.BlockSpec((None, 1, N_STATE), lambda i: (nblk - 1 - i, 0, 0)),
                            _full((tb, tb)),
                            _full((N_Q, 256, Q_W), single=True), _full((N_Q, 256, Q_W), single=True),
                            _full((k_steps, N_STATE)), _full((1, D))] + [ANY] * n_h,
                  out_specs=(rev(0), _full((2, N_Q, 16 * G_H, G_P)), _full((2, N_Q, 16 * G_H, G_P)),
                             _full((1, N_STATE)), _full((1, D)), *([ANY] * n_h)),
                  scratch_shapes=[pltpu.VMEM((tb, N_STATE), F32), pltpu.VMEM((1, N_STATE), F32),
                                  pltpu.VMEM((SUBLANES, N_STATE), F32),
                                  pltpu.VMEM((N_Q, 16 * G_H, 256), F32), pltpu.VMEM((N_Q, 16 * G_H, 256), F32),
                                  pltpu.VMEM((1, N_STATE), F32), pltpu.VMEM((1, D), F32),
                                  pltpu.VMEM((tb, D), BF16)] + _sem_scratch(items),
                  compiler_params=_params(("arbitrary",), vmem=60 * 1024 * 1024),
                  )(dyp, up, states, carries, pmt, wb, wct, ptab, dvec, *hosted)


def _pool_bwd(dyp, pooled, proj, pool_w, pool_scale):
    rows = dyp.shape[0]
    tb = _tb(rows, 512)
    nblk = rows // tb

    def body(dy_ref, pooled_ref, z_ref, pw_ref, ps_ref, dp_ref, dpw_ref, dps_ref, ahead_ref):
        i = pl.program_id(0)
        blk = nblk - 1 - i

        @pl.when(i == 0)
        def _():
            ahead_ref[...] = jnp.zeros_like(ahead_ref)
            dpw_ref[...] = jnp.zeros_like(dpw_ref)
            dps_ref[...] = jnp.zeros_like(dps_ref)

        inv_counts = _inv_counts(tb, blk * tb)
        silu_z, dsilu_z = _silu_parts(z_ref[...].astype(F32))
        dy = dy_ref[...].astype(F32)
        for g, w in enumerate(POOL_WINDOWS):
            cols = slice(g * 256, (g + 1) * 256)
            pooled_b = pooled_ref[:, cols]
            mixed = _dot(pooled_b, pw_ref[g])
            scale = ps_ref[:, cols]
            dp_ref[:, D + g * 256:D + (g + 1) * 256] = (dy[:, cols] * (mixed * scale) * dsilu_z[:, cols]).astype(BF16)
            dms = dy[:, cols] * silu_z[:, cols]
            dps_ref[:, cols] += jnp.sum(dms * mixed, axis=0, keepdims=True)
            dmixed = (dms * scale).astype(BF16)
            dpw_ref[g] += _dot_tn(pooled_b, dmixed)
            dpooled = _dot_nt(dmixed, pw_ref[g])
            ratio = dpooled * inv_counts[g]
            acc = jnp.concatenate([ratio, ahead_ref[:, cols]], axis=0)
            ahead_ref[:, cols] = ratio[:HALO, :]
            s = 1
            while s < w:
                acc = acc + pltpu.roll(acc, tb + HALO - s, axis=0)
                s *= 2
            dp_ref[:, cols] = (acc[:tb, :] - dpooled).astype(BF16)

    rev = lambda c: pl.BlockSpec((tb, D), lambda i: (nblk - 1 - i, c))
    return _pcall(body, name="pool_bwd", grid=(nblk,),
                  out_shape=(jax.ShapeDtypeStruct((rows, 2 * D), BF16), jax.ShapeDtypeStruct((4, 256, 256), F32),
                             jax.ShapeDtypeStruct((1, D), F32)),
                  in_specs=[rev(0), rev(0), rev(1), _full((4, 256, 256)), _full((1, D))],
                  out_specs=(pl.BlockSpec((tb, 2 * D), lambda i: (nblk - 1 - i, 0)), _full((4, 256, 256)),
                             _full((1, D))),
                  scratch_shapes=[pltpu.VMEM((HALO, D), F32)],
                  compiler_params=_params(("arbitrary",)))(dyp, pooled, proj, pool_w, pool_scale)


def _dproj_specs(tb):
    return [pl.BlockSpec((tb, 2 * D), lambda i: (i, 0)), pl.BlockSpec((tb, D), lambda i: (i, 0)),
            pl.BlockSpec((tb, D), lambda i: (i, 0)), pl.BlockSpec((tb, 2 * D), lambda i: (i, 0))]


def _in_proj_bwd_x(x, dy, dpp, dus, dzs, dpg, mod3, norm_pre, w_in, dw_in_ssm, recv_w_in):
    rows = x.shape[0]
    tb = _tb(rows, 512)
    nblk = rows // tb
    items = [_w_in_block_item(0, 0, j, ssm_part=True) for j in range(W_IN_SHARD // W_IN_BLOCK)]
    sums_item = [_Item(0, 0, _whole, _slot)]

    def body(x_ref, dy_ref, dpp_ref, dus_ref, dzs_ref, dpg_ref, mod_ref, np_ref, w_ref,
             dw_src, _, gx_ref, recv_w, recv_sums, vec_ref, ssem, rsem, lsem, *sums_sems):
        src_refs, recv_refs, sems = (dw_src,), (recv_w,), (ssem, rsem, lsem)

        @pl.when(pl.program_id(0) == 0)
        def _():
            _hosted_copies(items, src_refs, recv_refs, *sems, act="start")
            vec_ref[...] = jnp.zeros_like(vec_ref)

        dh = _dot_nt(dpp_ref[...], w_ref[:, 0:2 * D])
        dh += _dot_nt(dus_ref[...], w_ref[:, 2 * D:3 * D])
        dh += _dot_nt(dzs_ref[...], w_ref[:, 3 * D:4 * D])
        dh += _dot_nt(dpg_ref[...], w_ref[:, 4 * D:6 * D])
        xn, r, _ = _prenorm(x_ref[...], mod_ref[...], np_ref[...])
        one_scale = 1.0 + mod_ref[1:2, :]
        vec_ref[0:1, :] += jnp.sum(dh, axis=0, keepdims=True)
        vec_ref[1:2, :] += jnp.sum(dh * xn, axis=0, keepdims=True) * np_ref[...]
        vec_ref[2:3, :] += jnp.sum(dh * xn, axis=0, keepdims=True) * one_scale
        gx_ref[...] = dy_ref[...] + _rms_bwd(dh * (np_ref[...] * one_scale), xn, r)

        @pl.when(pl.program_id(0) == nblk - 1)
        def _():
            _hosted_copies(sums_item, (vec_ref,), (recv_sums,), *sums_sems, act="start")
            _hosted_copies(items, src_refs, recv_refs, *sems, act="wait")
            _hosted_copies(sums_item, (vec_ref,), (recv_sums,), *sums_sems, act="wait")

    row = pl.BlockSpec((tb, D), lambda i: (i, 0))
    recv = (jax.ShapeDtypeStruct(recv_w_in.shape, recv_w_in.dtype), jax.ShapeDtypeStruct((N_DEV, 3, D), F32))
    return _pcall(body, name="in_proj_bwd_x", grid=(nblk,),
                  out_shape=(jax.ShapeDtypeStruct((rows, D), F32), *recv),
                  in_specs=[row, row] + _dproj_specs(tb) + [_full((3, D)), _full((1, D)),
                                                            _full((D, N_IN), single=True)] + [ANY] * 2,
                  out_specs=(row, ANY, ANY),
                  input_output_aliases={10: 1},
                  scratch_shapes=[pltpu.VMEM((3, D), F32)] + _sem_scratch(items) + _sem_scratch(sums_item),
                  compiler_params=_params(("arbitrary",)))(x, dy, dpp, dus, dzs, dpg, mod3, norm_pre, w_in,
                                                           dw_in_ssm, recv_w_in)


def _in_proj_bwd_w(name, x, dparts, mod3, norm_pre, gathered=()):
    rows = x.shape[0]
    tb = _tb(rows, 512)
    nblk = rows // tb
    widths = [p.shape[1] for p in dparts]
    n_p, n_g = len(dparts), len(gathered)
    items = [_Item(t, t, _whole, _slot) for t in range(n_g)]

    def body(x_ref, *rest):
        part_refs, (mod_ref, np_ref) = rest[:n_p], rest[n_p:n_p + 2]
        src_refs, dw_ref = rest[n_p + 2:n_p + 2 + n_g], rest[n_p + 2 + n_g]
        recv_refs, (acc, *sems) = rest[n_p + 3 + n_g:n_p + 3 + 2 * n_g], rest[n_p + 3 + 2 * n_g:]
        i = pl.program_id(0)

        @pl.when(i == 0)
        def _():
            if n_g:
                _hosted_copies(items, src_refs, recv_refs, *sems, act="start")
            acc[...] = jnp.zeros_like(acc)

        _, _, h = _prenorm(x_ref[...], mod_ref[...], np_ref[...])
        ht = h.astype(BF16)
        lo = 0
        for ref, w in zip(part_refs, widths):
            acc[:, lo:lo + w] += _dot_tn(ht, ref[...])
            lo += w

        @pl.when(i == nblk - 1)
        def _():
            dw_ref[...] = acc[...].astype(BF16)
            if n_g:
                _hosted_copies(items, src_refs, recv_refs, *sems, act="wait")

    row = pl.BlockSpec((tb, D), lambda i: (i, 0))
    out = _pcall(body, name=name, grid=(nblk,),
                 out_shape=(jax.ShapeDtypeStruct((D, sum(widths)), BF16),
                            *[jax.ShapeDtypeStruct((N_DEV,) + g.shape, g.dtype) for g in gathered]),
                 in_specs=[row] + [pl.BlockSpec((tb, w), lambda i: (i, 0)) for w in widths] +
                          [_full((3, D)), _full((1, D))] + [ANY] * n_g,
                 out_specs=(_full((D, sum(widths))), *([ANY] * n_g)),
                 scratch_shapes=[pltpu.VMEM((D, sum(widths)), F32)] + (_sem_scratch(items) if n_g else []),
                 compiler_params=_params(("arbitrary",)))(x, *dparts, mod3, norm_pre, *gathered)
    return out if n_g else out[0]


def _adamw_math(w, g, m, v):
    m = ADAM_B1 * m + (1.0 - ADAM_B1) * g
    v = ADAM_B2 * v + (1.0 - ADAM_B2) * (g * g)
    m_hat = m / (1.0 - ADAM_B1 ** ADAM_STEP)
    v_hat = v / (1.0 - ADAM_B2 ** ADAM_STEP)
    delta = -ADAM_LR * (m_hat / (jnp.sqrt(v_hat) + ADAM_EPS) + ADAM_WD * w)
    return delta, m, v


def _sum_sources(ref):
    g = ref[0].astype(F32)
    for s in range(1, N_DEV):
        g = g + ref[s].astype(F32)
    return g


def _adamw_reduce(name, parts, w, m, v):
    r, c = w.shape
    tr = r if r * c <= 256 * 1024 else max(8, (256 * 1024 // c) // 8 * 8)
    while r % tr:
        tr -= 8

    def body(p_ref, w_ref, m_ref, v_ref, g_ref, d_ref, nm_ref, nv_ref):
        g = _sum_sources(p_ref)
        g_ref[...] = g
        d_ref[...], nm_ref[...], nv_ref[...] = _adamw_math(w_ref[...], g, m_ref[...], v_ref[...])

    blk = pl.BlockSpec((tr, c), lambda i: (i, 0))
    return _pcall(body, name=name, grid=(r // tr,),
                  out_shape=tuple([jax.ShapeDtypeStruct((r, c), F32)] * 4),
                  in_specs=[pl.BlockSpec((N_DEV, tr, c), lambda i: (0, i, 0)), blk, blk, blk],
                  out_specs=(blk, blk, blk, blk),
                  compiler_params=_params(("arbitrary",)))(parts, w, m, v)


def _adamw_small(gs, ws, ms, vs):
    n = len(gs)

    def body(*refs):
        ins, outs = refs[:4 * n], refs[4 * n:]
        for t in range(n):
            g_ref, w_ref, m_ref, v_ref = ins[4 * t:4 * t + 4]
            outs[3 * t][...], outs[3 * t + 1][...], outs[3 * t + 2][...] = _adamw_math(
                w_ref[...], g_ref[...], m_ref[...], v_ref[...])

    vm = pl.BlockSpec(memory_space=pltpu.VMEM)
    flat = [a for t in range(n) for a in (gs[t], ws[t], ms[t], vs[t])]
    return _pcall(body, name="adamw_small",
                  out_shape=tuple(jax.ShapeDtypeStruct(w.shape, F32) for w in ws for _ in range(3)),
                  in_specs=[vm] * (4 * n), out_specs=tuple([vm] * (3 * n)), compiler_params=_params())(*flat)


def _sum_small(parts):
    n = len(parts)

    def body(*refs):
        for t in range(n):
            refs[n + t][...] = _sum_sources(refs[t])

    vm = pl.BlockSpec(memory_space=pltpu.VMEM)
    return _pcall(body, name="sum_small",
                  out_shape=tuple(jax.ShapeDtypeStruct(p.shape[1:], F32) for p in parts),
                  in_specs=[vm] * n, out_specs=tuple([vm] * n), compiler_params=_params())(*parts)


def _ada_update(c_all, dmod_cols, w, m, v):
    def body(c_ref, dm_ref, w_ref, m_ref, v_ref, g_ref, d_ref, nm_ref, nv_ref):
        ca = c_ref[...]
        g = lax.dot_general(ca * jax.nn.sigmoid(ca), dm_ref[...], (((0,), (0,)), ((), ())),
                            preferred_element_type=F32, precision=lax.Precision.HIGHEST)
        g_ref[...] = g
        d_ref[...], nm_ref[...], nv_ref[...] = _adamw_math(w_ref[...], g, m_ref[...], v_ref[...])

    vm = pl.BlockSpec(memory_space=pltpu.VMEM)
    return _pcall(body, name="ada_update", out_shape=tuple([jax.ShapeDtypeStruct(w.shape, F32)] * 4),
                  in_specs=[vm] * 5, out_specs=(vm, vm, vm, vm), compiler_params=_params())(c_all, dmod_cols, w, m, v)


def kernel(x, c, w_ada, b_ada, norm_pre, norm_post, w_in, pool_w, pool_scale, ssm_a_re, ssm_a_im, ssm_log_dt, ssm_b_re, ssm_b_im, ssm_c_re, ssm_c_im, ssm_d, glu_w, glu_b, w_branch_pool, w_branch_ssm, w_out, loss_target, m_w_ada, m_b_ada, m_norm_pre, m_norm_post, m_w_in, m_pool_w, m_pool_scale, m_ssm_a_re, m_ssm_a_im, m_ssm_log_dt, m_ssm_b_re, m_ssm_b_im, m_ssm_c_re, m_ssm_c_im, m_ssm_d, m_glu_w, m_glu_b, m_w_branch_pool, m_w_branch_ssm, m_w_out, v_w_ada, v_b_ada, v_norm_pre, v_norm_post, v_w_in, v_pool_w, v_pool_scale, v_ssm_a_re, v_ssm_a_im, v_ssm_log_dt, v_ssm_b_re, v_ssm_b_im, v_ssm_c_re, v_ssm_c_im, v_ssm_d, v_glu_w, v_glu_b, v_w_branch_pool, v_w_branch_ssm, v_w_out):
    given = dict(locals())
    me = _flat(_me())
    rows = x.shape[1]
    x2 = x[0]
    target = loss_target[0]
    ada_cols = w_ada.shape[2]

    b_ada_s = lax.dynamic_slice(b_ada, (0, me * ada_cols), (1, ada_cols))
    c_all, mod_rows = _ada_exchange(c, w_ada[0], b_ada_s)
    mod3 = mod_rows.reshape(3, D)

    shards = _cast_shards([w_in[0], pool_w[0], glu_w[0], w_branch_pool[0], w_branch_ssm[0], w_out[0]])

    tb_ssm = _tb(rows, 256)
    k_steps = tb_ssm // SUBLANES
    a_re, a_im = ssm_a_re[0], ssm_a_im[0]
    log_dt = ssm_log_dt[0].reshape(GROUPS, 1)
    b_re_t, b_im_t = ssm_b_re[0].transpose(0, 2, 1), ssm_b_im[0].transpose(0, 2, 1)
    wb, wct, pow_re, pow_im = _s5_prep(a_re, a_im, log_dt, b_re_t, b_im_t, ssm_c_re[0], ssm_c_im[0], k_steps)
    ptab = _state_layout(pow_re, pow_im)
    dvec = ssm_d[0].reshape(1, D)
    pm = _perm_matrix(tb_ssm)
    pmt = pm.T

    proj, w_in_g, pool_w_g, glu_g = _in_proj(x2, mod3, norm_pre, shards[0], shards[1:3])
    y_pool, pooled = _pool_fwd(proj, pool_w_g, pool_scale)
    y_ssm, ys_pre, carries, states, glu_gate, z_perm, u_perm, wbp_g, wbs_g, wout_g = _ssm_fwd(
        proj, pm, pmt, wb, wct, ptab, dvec, glu_g, glu_b, shards[3:])
    loss_part, dy, dyp, dys, dpg, dwbp, dwbs, dwout, head_vec = _head(
        x2, target, proj, y_pool, y_ssm, mod3, norm_post, wbp_g, wbs_g, wout_g)

    dpp, dpool_w, dpool_scale = _pool_bwd(dyp, pooled, proj, pool_w_g, pool_scale)
    dw_in_rest = _in_proj_bwd_w("in_proj_bwd_w_rest", x2, [dpp, dpg], mod3, norm_pre)
    dy_pre, dzs, dglu_w, dglu_b = _glu_bwd(dys, z_perm, ys_pre, glu_gate, pm, pmt, glu_g)
    dus, dbb, dcc, dabar, dd, p_glu, p_wbp, p_wbs, p_wout, p_pool_w, p_w_in = _ssm_bwd(
        dy_pre, u_perm, states, carries, pmt, wb, wct, ptab, dvec, [dglu_w, dwbp, dwbs, dwout], dpool_w,
        dw_in_rest)

    small32 = jnp.concatenate([head_vec, dpool_scale, dglu_b, dd, jnp.broadcast_to(loss_part, (1, D)),
                               jnp.zeros((2, D), F32), dabar.reshape(8, D)], axis=0)
    small16 = jnp.concatenate([dbb.reshape(2 * GROUPS, D), dcc.reshape(2 * GROUPS, D)], axis=0).astype(BF16)
    dw_in_ssm, p_small32, p_small16 = _in_proj_bwd_w("in_proj_bwd_w_ssm", x2, [dus, dzs], mod3, norm_pre,
                                                     gathered=(small32, small16))
    grad_x, p_w_in, p_pre = _in_proj_bwd_x(x2, dy, dpp, dus, dzs, dpg, mod3, norm_pre, w_in_g, dw_in_ssm, p_w_in)

    tot32, tot16, tot_pre = _sum_small([p_small32, p_small16, p_pre])
    d_abar_re, d_abar_im = _state_unlayout(tot32[8:16].reshape(N_STATE))
    d_bb_re, d_bb_im = tot16[0:64].reshape(GROUPS, G_H, G_P), tot16[64:128].reshape(GROUPS, G_H, G_P)
    g_a_re, g_a_im, g_log_dt, g_b_re_t, g_b_im_t = _s5_prep_bwd(
        a_re, a_im, log_dt, b_re_t, b_im_t, d_abar_re, d_abar_im, d_bb_re, d_bb_im)

    grads, deltas, new_m, new_v = {}, {}, {}, {}

    small = []

    def small_update(name, g2):
        small.append((name, g2))

    def shard_update(name, parts):
        shape = given[name].shape
        r2 = parts.shape[1:] if parts.ndim == 3 else (parts.shape[1] * parts.shape[2], parts.shape[3])
        w2, m2, v2 = (given[p + name].reshape(r2) for p in ("", "m_", "v_"))
        out = _adamw_reduce("adamw_" + name, parts.reshape((N_DEV,) + tuple(r2)), w2, m2, v2)
        grads[name], deltas[name], new_m[name], new_v[name] = (a.reshape(shape) for a in out)

    dmod_all = jnp.concatenate([p_pre[:, 0:2, :], p_small32[:, 0:1, :]], axis=1).reshape(N_DEV, 3 * D)
    dmod_cols = lax.dynamic_slice(dmod_all, (0, me * ada_cols), (N_DEV, ada_cols))
    out = _ada_update(c_all, dmod_cols, w_ada[0], m_w_ada[0], v_w_ada[0])
    grads['w_ada'], deltas['w_ada'], new_m['w_ada'], new_v['w_ada'] = (a.reshape(w_ada.shape) for a in out)

    small_update('b_ada', jnp.concatenate([tot_pre[0:2], tot32[0:1]], axis=0).reshape(1, 3 * D))
    small_update('norm_pre', tot_pre[2:3])
    small_update('norm_post', tot32[1:2])
    small_update('pool_scale', tot32[2:3])
    small_update('glu_b', tot32[3:4])
    small_update('ssm_d', tot32[4:5])
    small_update('ssm_a_re', g_a_re)
    small_update('ssm_a_im', g_a_im)
    small_update('ssm_log_dt', g_log_dt.reshape(1, GROUPS))
    small_update('ssm_b_re', g_b_re_t.transpose(0, 2, 1).reshape(GROUPS, G_P * G_H))
    small_update('ssm_b_im', g_b_im_t.transpose(0, 2, 1).reshape(GROUPS, G_P * G_H))
    small_update('ssm_c_re', tot16[128:192])
    small_update('ssm_c_im', -tot16[192:256])
    flat = _adamw_small([g2 for _, g2 in small],
                        *[[given[p + name].reshape(g2.shape) for name, g2 in small] for p in ("", "m_", "v_")])
    for t, (name, g2) in enumerate(small):
        shape = given[name].shape
        grads[name], deltas[name], new_m[name], new_v[name] = (
            a.reshape(shape) for a in (g2, *flat[3 * t:3 * t + 3]))
    shard_update('w_in', p_w_in)
    shard_update('pool_w', p_pool_w)
    shard_update('glu_w', p_glu)
    shard_update('w_branch_pool', p_wbp)
    shard_update('w_branch_ssm', p_wbs)
    shard_update('w_out', p_wout)

    return (tot32[5, 0], grad_x[None], *[grads[n] for n in WEIGHTS], *[deltas[n] for n in WEIGHTS],
            *[new_m[n] for n in WEIGHTS], *[new_v[n] for n in WEIGHTS])
```

```python
import math
from typing import Callable, NamedTuple, Optional

import jax
import jax.numpy as jnp
from jax import lax
from jax.experimental import pallas as pl
from jax.experimental.pallas import tpu as pltpu

F32 = jnp.float32
BF16 = jnp.bfloat16
MESH = pl.DeviceIdType.MESH

D = 1024
N_DEV = 8
N_IN = 6 * D
GROUPS = 64
G_H = 16
G_P = 64
N_Q = 4
Q_W = 2 * 16 * G_P
N_STATE = N_Q * Q_W
POOL_WINDOWS = (2, 4, 8, 16)
HALO = 16
RMS_EPS = 1e-6
SUBLANES = 8
LANE_CHUNK = 512
SCAN_UNROLL = 2
VMEM_LIMIT = 56 * 1024 * 1024

ADAM_LR = 0.001
ADAM_B1 = 0.9
ADAM_B2 = 0.999
ADAM_EPS = 1e-08
ADAM_WD = 0.01
ADAM_STEP = 10

WEIGHTS = ['w_ada', 'b_ada', 'norm_pre', 'norm_post', 'w_in', 'pool_w', 'pool_scale', 'ssm_a_re',
           'ssm_a_im', 'ssm_log_dt', 'ssm_b_re', 'ssm_b_im', 'ssm_c_re', 'ssm_c_im', 'ssm_d', 'glu_w',
           'glu_b', 'w_branch_pool', 'w_branch_ssm', 'w_out']


def _pcall(body, **kw):
    return pl.pallas_call(body, **kw)


def _params(sem=None, vmem=VMEM_LIMIT):
    return pltpu.CompilerParams(dimension_semantics=sem, vmem_limit_bytes=vmem)


def _tb(rows, pref):
    return pref if rows % pref == 0 and rows // pref >= 2 else rows // 2


def _full(shape, single=False):
    nd = len(shape)
    if single:
        return pl.BlockSpec(shape, lambda i: (0,) * nd, pipeline_mode=pl.Buffered(1))
    return pl.BlockSpec(shape, lambda i: (0,) * nd)


ANY = pl.BlockSpec(memory_space=pl.ANY)


def _me():
    return lax.axis_index("x"), lax.axis_index("y"), lax.axis_index("c")


def _flat(p):
    return 4 * p[0] + 2 * p[1] + p[2]


def _peer(k):
    x, y, c = _me()
    return (1 - x if k & 4 else x, 1 - y if k & 2 else y, 1 - c if k & 1 else c)


def _silu_parts(z):
    s = jax.nn.sigmoid(z)
    return z * s, s * (1.0 + z * (1.0 - s))


_GELU_C = math.sqrt(2.0 / math.pi)


def _gelu_parts(x):
    x2 = x * x
    t = jnp.tanh(_GELU_C * (x + 0.044715 * x * x2))
    g = 0.5 * x * (1.0 + t)
    dg = 0.5 * (1.0 + t) + 0.5 * x * (1.0 - t * t) * (_GELU_C * (1.0 + 3.0 * 0.044715 * x2))
    return g, dg


def _dot(a, b):
    return jnp.dot(a, b, preferred_element_type=F32)


def _dot_nt(a, b):
    return lax.dot_general(a, b, (((1,), (1,)), ((), ())), preferred_element_type=F32)


def _dot_tn(a, b):
    return lax.dot_general(a, b, (((0,), (0,)), ((), ())), preferred_element_type=F32)


def _rms_parts(x):
    r = lax.rsqrt(jnp.mean(x * x, axis=-1, keepdims=True) + RMS_EPS)
    return x * r, r


def _rms_bwd(dxn, xn, r):
    return r * (dxn - xn * jnp.mean(dxn * xn, axis=-1, keepdims=True))


def _ada_exchange(c, w_ada_s, b_ada_s, local_ins, local_outs, local_work):
    cols = w_ada_s.shape[1]
    n_li, n_lo = len(local_ins), len(local_outs)

    def body(c_ref, w_ref, b_ref, *rest):
        li_refs, call_ref, mod_ref = rest[:n_li], rest[n_li], rest[n_li + 1]
        lo_refs, (part_ref, ssem, rsem, lsem) = rest[n_li + 2:n_li + 2 + n_lo], rest[n_li + 2 + n_lo:]
        me3 = _me()
        me = _flat(me3)
        mine = pltpu.make_async_copy(c_ref, call_ref.at[pl.ds(me, 1), :], lsem.at[0])
        mine.start()
        sends = []
        for k in range(1, N_DEV):
            cp = pltpu.make_async_remote_copy(src_ref=c_ref, dst_ref=call_ref.at[pl.ds(me, 1), :],
                                              send_sem=ssem.at[k - 1], recv_sem=rsem.at[k - 1],
                                              device_id=_peer(k), device_id_type=MESH)
            cp.start()
            sends.append(cp)
        local_work(li_refs, lo_refs)
        mine.wait()
        for k in range(1, N_DEV):
            p = _flat(_peer(k))
            pltpu.make_async_remote_copy(src_ref=c_ref, dst_ref=call_ref.at[pl.ds(p, 1), :],
                                         send_sem=ssem.at[k - 1], recv_sem=rsem.at[k - 1],
                                         device_id=_peer(k), device_id_type=MESH).wait_recv()
        for cp in sends:
            cp.wait_send()
        ca = call_ref[...]
        act = ca * jax.nn.sigmoid(ca)
        part_ref[...] = jnp.dot(act, w_ref[...], preferred_element_type=F32,
                                precision=lax.Precision.HIGHEST) + b_ref[...]
        own = pltpu.make_async_copy(part_ref.at[pl.ds(me, 1), :], mod_ref.at[pl.ds(me, 1), :], lsem.at[1])
        own.start()
        sends = []
        for k in range(1, N_DEV):
            p = _flat(_peer(k))
            s = N_DEV - 1 + k - 1
            cp = pltpu.make_async_remote_copy(src_ref=part_ref.at[pl.ds(p, 1), :],
                                              dst_ref=mod_ref.at[pl.ds(me, 1), :],
                                              send_sem=ssem.at[s], recv_sem=rsem.at[s],
                                              device_id=_peer(k), device_id_type=MESH)
            cp.start()
            sends.append(cp)
        own.wait()
        for k in range(1, N_DEV):
            p = _flat(_peer(k))
            s = N_DEV - 1 + k - 1
            pltpu.make_async_remote_copy(src_ref=part_ref.at[pl.ds(p, 1), :],
                                         dst_ref=mod_ref.at[pl.ds(p, 1), :],
                                         send_sem=ssem.at[s], recv_sem=rsem.at[s],
                                         device_id=_peer(k), device_id_type=MESH).wait_recv()
        for cp in sends:
            cp.wait_send()

    vm = pl.BlockSpec(memory_space=pltpu.VMEM)
    return _pcall(
        body, name="ada_exchange",
        out_shape=(jax.ShapeDtypeStruct((N_DEV, D), F32), jax.ShapeDtypeStruct((N_DEV, cols), F32), *local_outs),
        in_specs=[vm] * (3 + n_li), out_specs=tuple([vm] * (2 + n_lo)),
        scratch_shapes=[pltpu.VMEM((N_DEV, cols), F32),
                        pltpu.SemaphoreType.DMA((2 * (N_DEV - 1),)),
                        pltpu.SemaphoreType.DMA((2 * (N_DEV - 1),)),
                        pltpu.SemaphoreType.DMA((2,))],
        compiler_params=_params(),
    )(c, w_ada_s, b_ada_s, *local_ins)


class _Item(NamedTuple):
    src: int
    out: int
    src_view: Callable
    dst_view: Callable
    pred: Optional[Callable] = None


def _when(pred, dest, fn):
    if pred is None:
        fn()
    else:
        pl.when(pred(dest))(fn)


def _n_sems(items):
    return len(items) * (N_DEV - 1)


def _hosted_copies(items, srcs, outs, ssem, rsem, lsem, act):
    me = _flat(_me())
    for t, it in enumerate(items):
        local = lambda t=t, it=it: pltpu.make_async_copy(
            it.src_view(srcs[it.src], me), it.dst_view(outs[it.out], me), lsem.at[t])
        if act == "start":
            _when(it.pred, me, lambda local=local: local().start())
        else:
            _when(it.pred, me, lambda local=local: local().wait())
    for k in range(1, N_DEV):
        p3 = _peer(k)
        p = _flat(p3)
        for t, it in enumerate(items):
            s = t * (N_DEV - 1) + k - 1
            send = lambda it=it, s=s, p=p, p3=p3: pltpu.make_async_remote_copy(
                src_ref=it.src_view(srcs[it.src], p), dst_ref=it.dst_view(outs[it.out], me),
                send_sem=ssem.at[s], recv_sem=rsem.at[s], device_id=p3, device_id_type=MESH)
            recv = lambda it=it, s=s, p=p, p3=p3: pltpu.make_async_remote_copy(
                src_ref=it.src_view(srcs[it.src], p), dst_ref=it.dst_view(outs[it.out], p),
                send_sem=ssem.at[s], recv_sem=rsem.at[s], device_id=p3, device_id_type=MESH)
            if act == "start":
                _when(it.pred, p, lambda send=send: send().start())
            else:
                _when(it.pred, me, lambda recv=recv: recv().wait_recv())
                _when(it.pred, p, lambda send=send: send().wait_send())


def _sem_scratch(items):
    return [pltpu.SemaphoreType.DMA((_n_sems(items),)), pltpu.SemaphoreType.DMA((_n_sems(items),)),
            pltpu.SemaphoreType.DMA((len(items),))]


def _whole(ref, dest):
    return ref


def _slot(ref, sender):
    return ref.at[sender]


def _rows_of(rows):
    return lambda ref, dev: ref.at[pl.ds(dev * rows, rows), :]


def _pool_rows_of(rows):
    return lambda ref, dev: ref.at[:, pl.ds(dev * rows, rows), :]


def _gather_item(src, out, dst_view):
    return _Item(src, out, _whole, dst_view)


def _scatter_item(src, out, src_view):
    return _Item(src, out, src_view, _slot)


W_IN_BLOCK = 256
W_IN_SHARD = N_IN // N_DEV
SSM_BLOCKS = (2 * D // W_IN_BLOCK, 4 * D // W_IN_BLOCK)


def _w_in_block_item(src, out, j, ssm_part):
    def block(dest):
        return (W_IN_SHARD // W_IN_BLOCK) * dest + j

    def in_ssm(dest):
        b = block(dest)
        return (b >= SSM_BLOCKS[0]) & (b < SSM_BLOCKS[1])

    def src_view(ref, dest):
        b = block(dest)
        local = b - SSM_BLOCKS[0] if ssm_part else jnp.where(b < SSM_BLOCKS[0], b, b - (SSM_BLOCKS[1] - SSM_BLOCKS[0]))
        local = jnp.clip(local, 0, ref.shape[1] // W_IN_BLOCK - 1)
        return ref.at[:, pl.ds(local * W_IN_BLOCK, W_IN_BLOCK)]

    def dst_view(ref, sender):
        return ref.at[sender, :, pl.ds(j * W_IN_BLOCK, W_IN_BLOCK)]

    pred = in_ssm if ssm_part else (lambda dest: jnp.logical_not(in_ssm(dest)))
    return _Item(src, out, src_view, dst_view, pred)


def _s5_discretise(a_re, a_im, log_dt, b_re_t, b_im_t):
    dt = jnp.exp(log_dt)
    lam_re = jnp.minimum(a_re, -1e-4)
    lam_im = a_im
    mag = jnp.exp(lam_re * dt)
    abar_re = mag * jnp.cos(lam_im * dt)
    abar_im = mag * jnp.sin(lam_im * dt)
    den = lam_re * lam_re + lam_im * lam_im
    num_re = abar_re - 1.0
    f_re = (num_re * lam_re + abar_im * lam_im) / den
    f_im = (abar_im * lam_re - num_re * lam_im) / den
    f_re, f_im = f_re[:, None, :], f_im[:, None, :]
    bb_re = f_re * b_re_t - f_im * b_im_t
    bb_im = f_re * b_im_t + f_im * b_re_t
    return abar_re, abar_im, bb_re, bb_im


def _group_masks():
    spread = lax.broadcasted_iota(jnp.int32, (G_P, 16 * G_P), 1) % G_P == lax.broadcasted_iota(
        jnp.int32, (G_P, 16 * G_P), 0)
    own = lax.broadcasted_iota(jnp.int32, (16 * G_H, 16 * G_P), 0) // G_H == lax.broadcasted_iota(
        jnp.int32, (16 * G_H, 16 * G_P), 1) // G_P
    return spread, own


def _s5_prep_structs(n_pow):
    return (jax.ShapeDtypeStruct((N_Q, 16 * G_H, Q_W), BF16), jax.ShapeDtypeStruct((N_Q, 16 * G_H, Q_W), BF16),
            jax.ShapeDtypeStruct((n_pow, GROUPS, G_P), F32), jax.ShapeDtypeStruct((n_pow, GROUPS, G_P), F32))


def _s5_prep_body(ar_ref, ai_ref, ld_ref, br_ref, bi_ref, cr_ref, ci_ref, wb_ref, wct_ref, pr_ref, pi_ref):
    abar_re, abar_im, bb_re, bb_im = _s5_discretise(ar_ref[...], ai_ref[...], ld_ref[...], br_ref[...], bi_ref[...])
    spread, own = _group_masks()
    spread = spread.astype(BF16)
    for ref, parts in ((wb_ref, (bb_re, bb_im)), (wct_ref, (cr_ref[...], -ci_ref[...]))):
        for half, t in enumerate(parts):
            for q in range(N_Q):
                blocks = t[q * 16:(q + 1) * 16].reshape(16 * G_H, G_P).astype(BF16)
                dense = jnp.where(own, _dot(blocks, spread), 0.0)
                ref[q, :, half * (Q_W // 2):(half + 1) * (Q_W // 2)] = dense.astype(BF16)
    p_re, p_im = abar_re, abar_im
    pr_ref[0] = p_re
    pi_ref[0] = p_im
    for k in range(1, pr_ref.shape[0]):
        p_re, p_im = p_re * abar_re - p_im * abar_im, p_re * abar_im + p_im * abar_re
        pr_ref[k] = p_re
        pi_ref[k] = p_im


def _s5_prep_bwd(a_re, a_im, log_dt, b_re_t, b_im_t, d_abar_re, d_abar_im, d_bb_re, d_bb_im):
    def body(ar_ref, ai_ref, ld_ref, br_ref, bi_ref, dar_ref, dai_ref, dbr_ref, dbi_ref,
             gar_ref, gai_ref, gld_ref, gbr_ref, gbi_ref):
        _, vjp = jax.vjp(_s5_discretise, ar_ref[...], ai_ref[...], ld_ref[...], br_ref[...], bi_ref[...])
        g = vjp((dar_ref[...], dai_ref[...], dbr_ref[...], dbi_ref[...]))
        gar_ref[...] = g[0]
        gai_ref[...] = g[1]
        gld_ref[...] = g[2]
        gbr_ref[...] = g[3]
        gbi_ref[...] = g[4]

    vm = pl.BlockSpec(memory_space=pltpu.VMEM)
    ins = (a_re, a_im, log_dt, b_re_t, b_im_t)
    return _pcall(body, name="s5_prep_bwd",
                  out_shape=tuple(jax.ShapeDtypeStruct(a.shape, F32) for a in ins),
                  in_specs=[vm] * 9, out_specs=tuple([vm] * 5), compiler_params=_params(),
                  )(*ins, d_abar_re, d_abar_im, d_bb_re, d_bb_im)


def _state_layout(re, im):
    lead = re.shape[:-2]
    r = re.reshape(lead + (N_Q, 1, 16 * G_P))
    i = im.reshape(lead + (N_Q, 1, 16 * G_P))
    return jnp.concatenate([r, i], axis=-2).reshape(lead + (N_STATE,))


def _state_unlayout(v):
    v4 = v.reshape(N_Q, 2, 16, G_P)
    return v4[:, 0].reshape(GROUPS, G_P), v4[:, 1].reshape(GROUPS, G_P)


def _perm_matrix(tb):
    k_steps = tb // SUBLANES
    r = jnp.arange(tb)
    src = (r % SUBLANES) * k_steps + r // SUBLANES
    return (src[:, None] == jnp.arange(tb)[None, :]).astype(BF16)


def _lane_chunks(q):
    for lc in range(Q_W // 2 // LANE_CHUNK):
        re = q * Q_W + lc * LANE_CHUNK
        yield re, re + Q_W // 2


def _steps(lo, hi, body, init):
    if hi - lo <= SCAN_UNROLL:
        for k in range(lo, hi):
            init = body(k, init)
        return init
    trips = (hi - lo) // SCAN_UNROLL

    def trip(j, carry):
        for u in range(SCAN_UNROLL):
            carry = body(lo + j * SCAN_UNROLL + u, carry)
        return carry

    carry = lax.fori_loop(0, trips, trip, init)
    for k in range(lo + trips * SCAN_UNROLL, hi):
        carry = body(k, carry)
    return carry


def _tile(k):
    if isinstance(k, int):
        return pl.ds(k * SUBLANES, SUBLANES)
    return pl.ds(pl.multiple_of(k * SUBLANES, SUBLANES), SUBLANES)


def _scan_forward(q, s_ref, p_ref, carry_ref, enter_ref, fin_ref, k_steps):
    for re, im in _lane_chunks(q):
        lr, li = pl.ds(re, LANE_CHUNK), pl.ds(im, LANE_CHUNK)
        a_re = jnp.broadcast_to(p_ref[0:1, lr], (SUBLANES, LANE_CHUNK))
        a_im = jnp.broadcast_to(p_ref[0:1, li], (SUBLANES, LANE_CHUNK))

        def local(k, st):
            sr, si = st
            rows = _tile(k)
            nr = a_re * sr - a_im * si + s_ref[rows, lr]
            ni = a_re * si + a_im * sr + s_ref[rows, li]
            s_ref[rows, lr] = nr
            s_ref[rows, li] = ni
            return nr, ni

        zero = jnp.zeros((SUBLANES, LANE_CHUNK), F32)
        fr, fi = _steps(0, k_steps, local, (zero, zero))
        fin_ref[:, lr] = fr
        fin_ref[:, li] = fi
        ak_re, ak_im = p_ref[k_steps - 1:k_steps, lr], p_ref[k_steps - 1:k_steps, li]
        c_re, c_im = carry_ref[:, lr], carry_ref[:, li]
        for seg in range(SUBLANES):
            enter_ref[seg:seg + 1, lr] = c_re
            enter_ref[seg:seg + 1, li] = c_im
            f_re, f_im = fin_ref[seg:seg + 1, lr], fin_ref[seg:seg + 1, li]
            c_re, c_im = f_re + ak_re * c_re - ak_im * c_im, f_im + ak_re * c_im + ak_im * c_re
        carry_ref[:, lr] = c_re
        carry_ref[:, li] = c_im
        e_re, e_im = enter_ref[:, lr], enter_ref[:, li]

        def fix(k, _):
            rows = _tile(k)
            p_re = p_ref[pl.ds(k, 1), lr]
            p_im = p_ref[pl.ds(k, 1), li]
            s_ref[rows, lr] = s_ref[rows, lr] + (p_re * e_re - p_im * e_im)
            s_ref[rows, li] = s_ref[rows, li] + (p_re * e_im + p_im * e_re)
            return 0

        _steps(0, k_steps, fix, 0)


def _scan_backward(q, g_ref, s_ref, p_ref, carry_ref, s_in_ref, fin_ref, da_ref, k_steps):
    seg_id = lax.broadcasted_iota(jnp.int32, (SUBLANES, LANE_CHUNK), 0)
    for re, im in _lane_chunks(q):
        lr, li = pl.ds(re, LANE_CHUNK), pl.ds(im, LANE_CHUNK)
        a_re = jnp.broadcast_to(p_ref[0:1, lr], (SUBLANES, LANE_CHUNK))
        a_im = jnp.broadcast_to(p_ref[0:1, li], (SUBLANES, LANE_CHUNK))

        def local(j, st):
            sr, si = st
            rows = _tile(k_steps - 1 - j)
            nr = a_re * sr + a_im * si + g_ref[rows, lr]
            ni = a_re * si - a_im * sr + g_ref[rows, li]
            g_ref[rows, lr] = nr
            g_ref[rows, li] = ni
            return nr, ni

        zero = jnp.zeros((SUBLANES, LANE_CHUNK), F32)
        fr, fi = _steps(0, k_steps, local, (zero, zero))
        fin_ref[:, lr] = fr
        fin_ref[:, li] = fi
        ak_re, ak_im = p_ref[k_steps - 1:k_steps, lr], p_ref[k_steps - 1:k_steps, li]
        c_re, c_im = carry_ref[:, lr], carry_ref[:, li]
        lam_in = [None] * SUBLANES
        for seg in reversed(range(SUBLANES)):
            lam_in[seg] = (c_re, c_im)
            f_re, f_im = fin_ref[seg:seg + 1, lr], fin_ref[seg:seg + 1, li]
            c_re, c_im = f_re + ak_re * c_re + ak_im * c_im, f_im + ak_re * c_im - ak_im * c_re
        carry_ref[:, lr] = c_re
        carry_ref[:, li] = c_im
        for seg in range(SUBLANES):
            fin_ref[seg:seg + 1, lr] = lam_in[seg][0]
            fin_ref[seg:seg + 1, li] = lam_in[seg][1]
        e_re, e_im = fin_ref[:, lr], fin_ref[:, li]

        def fix_with(k, acc, sp_re, sp_im):
            acc_re, acc_im = acc
            rows = _tile(k)
            p_re = p_ref[pl.ds(k_steps - 1 - k, 1), lr]
            p_im = p_ref[pl.ds(k_steps - 1 - k, 1), li]
            l_re = g_ref[rows, lr] + (p_re * e_re + p_im * e_im)
            l_im = g_ref[rows, li] + (p_re * e_im - p_im * e_re)
            g_ref[rows, lr] = l_re
            g_ref[rows, li] = l_im
            return acc_re + (l_re * sp_re + l_im * sp_im), acc_im + (l_im * sp_re - l_re * sp_im)

        def fix(k, acc):
            prev = _tile(k - 1)
            return fix_with(k, acc, s_ref[prev, lr], s_ref[prev, li])

        last = _tile(k_steps - 1)
        before_re = jnp.where(seg_id == 0, s_in_ref[:, lr], pltpu.roll(s_ref[last, lr], 1, axis=0))
        before_im = jnp.where(seg_id == 0, s_in_ref[:, li], pltpu.roll(s_ref[last, li], 1, axis=0))
        acc = fix_with(0, (zero, zero), before_re, before_im)
        acc_re, acc_im = _steps(1, k_steps, fix, acc)
        da_ref[:, lr] = da_ref[:, lr] + jnp.sum(acc_re, axis=0, keepdims=True)
        da_ref[:, li] = da_ref[:, li] + jnp.sum(acc_im, axis=0, keepdims=True)


def _prenorm(x, mod3, norm_pre):
    xn, r = _rms_parts(x)
    return xn, r, xn * norm_pre * (1.0 + mod3[1:2, :]) + mod3[0:1, :]


CHIP_FLIPS = (4, 2, 6)


def _shard_order(me):
    flips = [0, 1] + [f + c for f in CHIP_FLIPS for c in (0, 1)]
    return jnp.stack([me ^ f for f in flips]).astype(jnp.int32)


def _in_proj(x, mod3, norm_pre, w_in_s, shards):
    rows = x.shape[0]
    tb = _tb(rows, 2048)
    nblk = rows // tb
    n_sh = len(shards)
    last_step = N_DEV - 1
    items = [_gather_item(0, 0, _pool_rows_of(shards[0].shape[1]))] + \
            [_gather_item(t, t, _rows_of(shards[t].shape[0])) for t in range(1, n_sh)]

    def body(order_ref, x_ref, mod_ref, np_ref, w_src, *rest):
        src_refs, proj_ref, w_full, out_refs = rest[:n_sh], rest[n_sh], rest[n_sh + 1], rest[n_sh + 2:2 * n_sh + 2]
        h_scr, wg, ssem, rsem, lsem, *sems = rest[2 * n_sh + 2:]
        s, i = pl.program_id(0), pl.program_id(1)
        me3 = _me()
        me = _flat(me3)
        sibling = _peer(1)

        def own_copy(slot, k):
            return pltpu.make_async_remote_copy(src_ref=w_src, dst_ref=wg.at[me], send_sem=ssem.at[slot],
                                                recv_sem=rsem.at[slot], device_id=_peer(k), device_id_type=MESH)

        def passed_copy(j):
            p = _flat(_peer(CHIP_FLIPS[j]))
            return pltpu.make_async_remote_copy(src_ref=wg.at[p], dst_ref=wg.at[p], send_sem=ssem.at[4 + j],
                                                recv_sem=rsem.at[4 + j], device_id=sibling, device_id_type=MESH)

        def arrival(slot, flip):
            p = _flat(_peer(flip))
            pltpu.make_async_remote_copy(src_ref=w_src, dst_ref=wg.at[p], send_sem=ssem.at[slot],
                                         recv_sem=rsem.at[slot], device_id=sibling, device_id_type=MESH).wait_recv()

        def keep(t):
            p = order_ref[t]
            return pltpu.make_async_copy(wg.at[p], w_full.at[:, pl.ds(p * W_IN_SHARD, W_IN_SHARD)], lsem.at[1 + t])

        first = i == 0
        for t in range(last_step):
            pl.when(first & (s == t + 1))(lambda t=t: keep(t).start())

        @pl.when(first & (s == 0))
        def _():
            mine = pltpu.make_async_copy(w_src, wg.at[me], lsem.at[0])
            mine.start()
            own_copy(0, 1).start()
            for j, f in enumerate(CHIP_FLIPS[:2]):
                own_copy(1 + j, f).start()
            mine.wait()

        @pl.when(first & (s == 1))
        def _():
            arrival(0, 1)

        for j, f in enumerate(CHIP_FLIPS):
            @pl.when(first & (s == 2 + 2 * j))
            def _(j=j, f=f):
                arrival(1 + j, f)
                passed_copy(j).start()
                if j == 0:
                    own_copy(3, CHIP_FLIPS[2]).start()

            @pl.when(first & (s == 3 + 2 * j))
            def _(j=j, f=f):
                arrival(4 + j, f + 1)

        @pl.when(first & (s == last_step - 1))
        def _():
            _hosted_copies(items, src_refs, out_refs, *sems, act="start")

        rows_i = pl.ds(pl.multiple_of(i * tb, tb), tb)

        @pl.when(s == 0)
        def _():
            _, _, h = _prenorm(x_ref[...], mod_ref[...], np_ref[...])
            h_scr[rows_i, :] = h.astype(BF16)

        proj_ref[...] = _dot(h_scr[rows_i, :], wg[order_ref[s]]).astype(BF16)

        @pl.when((s == last_step) & (i == nblk - 1))
        def _():
            own_copy(0, 1).wait_send()
            for j, f in enumerate(CHIP_FLIPS):
                own_copy(1 + j, f).wait_send()
                passed_copy(j).wait_send()
            keep(last_step).start()
            for t in range(N_DEV):
                keep(t).wait()
            _hosted_copies(items, src_refs, out_refs, *sems, act="wait")

    full = [jax.ShapeDtypeStruct((4, 256, 256), BF16)] + [jax.ShapeDtypeStruct((D, D), BF16)] * (n_sh - 1)
    grid_spec = pltpu.PrefetchScalarGridSpec(
        num_scalar_prefetch=1, grid=(N_DEV, nblk),
        in_specs=[pl.BlockSpec((tb, D), lambda s, i, order: (jnp.where(s == 0, i, nblk - 1), 0)),
                  pl.BlockSpec((3, D), lambda s, i, order: (0, 0)), pl.BlockSpec((1, D), lambda s, i, order: (0, 0)),
                  ANY] + [ANY] * n_sh,
        out_specs=(pl.BlockSpec((tb, W_IN_SHARD), lambda s, i, order: (i, order[s])), ANY, *([ANY] * n_sh)),
        scratch_shapes=[pltpu.VMEM((rows, D), BF16), pltpu.VMEM((N_DEV, D, W_IN_SHARD), BF16),
                        pltpu.SemaphoreType.DMA((N_DEV - 1,)), pltpu.SemaphoreType.DMA((N_DEV - 1,)),
                        pltpu.SemaphoreType.DMA((1 + N_DEV,))] + _sem_scratch(items))
    return _pcall(body, name="in_proj", grid_spec=grid_spec,
                  out_shape=(jax.ShapeDtypeStruct((rows, N_IN), BF16), jax.ShapeDtypeStruct((D, N_IN), BF16), *full),
                  compiler_params=_params(("arbitrary", "arbitrary")),
                  )(_shard_order(_flat(_me())), x, mod3, norm_pre, w_in_s, *shards)


def _pool_windows(ext, tb, first_row):
    inv_counts = _inv_counts(tb, first_row)
    pooled = []
    for g, w in enumerate(POOL_WINDOWS):
        acc = ext[:, g * 256:(g + 1) * 256]
        tok = acc[HALO:, :]
        s = 1
        while s < w:
            acc = acc + pltpu.roll(acc, s, axis=0)
            s *= 2
        pooled.append(acc[HALO:, :] * inv_counts[g] - tok)
    return pooled, inv_counts


def _inv_counts(tb, first_row):
    pos = (first_row + lax.broadcasted_iota(jnp.int32, (tb, 1), 0) + 1).astype(F32)
    return [1.0 / jnp.minimum(pos, float(w)) for w in POOL_WINDOWS]


def _pool_fwd(proj, pool_w, pool_scale):
    rows = proj.shape[0]
    tb = _tb(rows, 512)
    hb = tb // HALO

    def body(u_ref, halo_ref, z_ref, pw_ref, ps_ref, y_ref, pooled_ref):
        i = pl.program_id(0)
        u = u_ref[...].astype(F32)
        halo = jnp.where(i > 0, halo_ref[...].astype(F32), 0.0)
        pooled, _ = _pool_windows(jnp.concatenate([halo, u], axis=0), tb, i * tb)
        silu_z, _ = _silu_parts(z_ref[...].astype(F32))
        for g in range(4):
            cols = slice(g * 256, (g + 1) * 256)
            pooled_b = pooled[g].astype(BF16)
            pooled_ref[:, cols] = pooled_b
            mixed = _dot(pooled_b, pw_ref[g])
            y_ref[:, cols] = (mixed * ps_ref[:, cols] * silu_z[:, cols]).astype(BF16)

    blk = pl.BlockSpec((tb, D), lambda i: (i, 0))
    return _pcall(body, name="pool_fwd", grid=(rows // tb,),
                  out_shape=(jax.ShapeDtypeStruct((rows, D), BF16), jax.ShapeDtypeStruct((rows, D), BF16)),
                  in_specs=[blk, pl.BlockSpec((HALO, D), lambda i: (jnp.maximum(i * hb - 1, 0), 0)),
                            pl.BlockSpec((tb, D), lambda i: (i, 1)),
                            _full((4, 256, 256)), _full((1, D))],
                  out_specs=(blk, blk),
                  compiler_params=_params(("arbitrary",)))(proj, proj, proj, pool_w, pool_scale)


def _ssm_fwd(proj, pm, pmt, wb, wct, ptab, dvec, glu_w, glu_b, shards):
    rows = proj.shape[0]
    tb = pm.shape[0]
    k_steps = tb // SUBLANES
    nblk = rows // tb
    n_sh = len(shards)
    items = [_gather_item(t, t, _rows_of(shards[t].shape[0])) for t in range(n_sh)]

    def body(u_ref, z_ref, pm_ref, pmt_ref, wb_ref, wct_ref, p_ref, d_ref, gw_ref, gb_ref, *rest):
        src_refs = rest[:n_sh]
        y_ref, ys_ref, carry_out_ref, s_ref, gate_ref, zp_ref, up_ref = rest[n_sh:n_sh + 7]
        out_refs = rest[n_sh + 7:2 * n_sh + 7]
        carry_ref, enter_ref, fin_ref, *sems = rest[2 * n_sh + 7:]

        @pl.when(pl.program_id(0) == 0)
        def _():
            _hosted_copies(items, src_refs, out_refs, *sems, act="start")
            carry_ref[...] = jnp.zeros_like(carry_ref)

        carry_out_ref[...] = carry_ref[...]
        up = _dot(pm_ref[...], u_ref[...]).astype(BF16)
        up_ref[...] = up

        for q in range(N_Q):
            s_ref[:, q * Q_W:(q + 1) * Q_W] = _dot(up[:, q * 256:(q + 1) * 256], wb_ref[q])
        for q in range(N_Q):
            _scan_forward(q, s_ref, p_ref, carry_ref, enter_ref, fin_ref, k_steps)
        for q in range(N_Q):
            cols = slice(q * 256, (q + 1) * 256)
            y = _dot_nt(s_ref[:, q * Q_W:(q + 1) * Q_W].astype(BF16), wct_ref[q])
            ys_ref[:, cols] = y + d_ref[:, cols] * up[:, cols].astype(F32)
        yg, _ = _gelu_parts(ys_ref[...])
        gate = jax.nn.sigmoid(_dot(yg.astype(BF16), gw_ref[...]) + gb_ref[...])
        gate_ref[...] = gate
        zp = _dot(pm_ref[...], z_ref[...])
        zp_ref[...] = zp.astype(BF16)
        silu_z, _ = _silu_parts(zp)
        y_ref[...] = _dot(pmt_ref[...], (yg * gate * silu_z).astype(BF16)).astype(BF16)

        @pl.when(pl.program_id(0) == nblk - 1)
        def _():
            _hosted_copies(items, src_refs, out_refs, *sems, act="wait")

    return _pcall(body, name="ssm_fwd", grid=(nblk,),
                  out_shape=(jax.ShapeDtypeStruct((rows, D), BF16), jax.ShapeDtypeStruct((rows, D), F32),
                             jax.ShapeDtypeStruct((nblk, 1, N_STATE), F32),
                             jax.ShapeDtypeStruct((rows, N_STATE), F32),
                             jax.ShapeDtypeStruct((rows, D), F32), jax.ShapeDtypeStruct((rows, D), BF16),
                             jax.ShapeDtypeStruct((rows, D), BF16),
                             *[jax.ShapeDtypeStruct((D, D), BF16)] * n_sh),
                  in_specs=[pl.BlockSpec((tb, D), lambda i: (i, 2)), pl.BlockSpec((tb, D), lambda i: (i, 3)),
                            _full((tb, tb)), _full((tb, tb)),
                            _full((N_Q, 256, Q_W), single=True), _full((N_Q, 256, Q_W), single=True),
                            _full((k_steps, N_STATE)), _full((1, D)), _full((D, D), single=True), _full((1, D))] +
                           [ANY] * n_sh,
                  out_specs=(pl.BlockSpec((tb, D), lambda i: (i, 0)), pl.BlockSpec((tb, D), lambda i: (i, 0)),
                             pl.BlockSpec((None, 1, N_STATE), lambda i: (i, 0, 0)),
                             pl.BlockSpec((tb, N_STATE), lambda i: (i, 0)),
                             *[pl.BlockSpec((tb, D), lambda i: (i, 0))] * 3, *([ANY] * n_sh)),
                  scratch_shapes=[pltpu.VMEM((1, N_STATE), F32),
                                  pltpu.VMEM((SUBLANES, N_STATE), F32), pltpu.VMEM((SUBLANES, N_STATE), F32)] +
                                 _sem_scratch(items),
                  compiler_params=_params(("arbitrary",)))(proj, proj, pm, pmt, wb, wct, ptab, dvec, glu_w, glu_b,
                                                           *shards)


def _head(x, target, proj, y_pool, y_ssm, mod3, norm_post, wbp, wbs, wout):
    rows = x.shape[0]
    tb = _tb(rows, 256)
    nblk = rows // tb
    n_feat = float(D)

    def body(x_ref, t_ref, gp_ref, gs_ref, yp_ref, ys_ref, mod_ref, npost_ref, wbp_ref, wbs_ref, wout_ref,
             loss_ref, dy_ref, dyp_ref, dys_ref, dg_ref, dwbp_hbm, dwbs_hbm, dwout_hbm, vec_ref,
             acc_bp, acc_bs, acc_out, acc_loss, acc_vec):
        i = pl.program_id(0)

        @pl.when(i == 0)
        def _():
            acc_bp[...] = jnp.zeros_like(acc_bp)
            acc_bs[...] = jnp.zeros_like(acc_bs)
            acc_out[...] = jnp.zeros_like(acc_out)
            acc_loss[...] = jnp.zeros_like(acc_loss)
            acc_vec[...] = jnp.zeros_like(acc_vec)

        gate = mod_ref[2:3, :]
        npost = npost_ref[...]
        yp, ys = yp_ref[...], ys_ref[...]
        sgp = jax.nn.sigmoid(gp_ref[...].astype(F32))
        sgs = jax.nn.sigmoid(gs_ref[...].astype(F32))
        pb = _dot(yp, wbp_ref[...])
        psm = _dot(ys, wbs_ref[...])
        mb = (sgp * pb + sgs * psm).astype(BF16)
        out = _dot(mb, wout_ref[...])
        on, r = _rms_parts(out)
        normed = on * npost
        diff = x_ref[...] + gate * normed - t_ref[...]
        acc_loss[...] += jnp.sum(diff * diff, axis=0, keepdims=True)
        dy = diff * (1.0 / n_feat)
        dy_ref[...] = dy
        acc_vec[0:1, :] += jnp.sum(dy * normed, axis=0, keepdims=True)
        dn = dy * gate
        acc_vec[1:2, :] += jnp.sum(dn * on, axis=0, keepdims=True)
        dout = _rms_bwd(dn * npost, on, r).astype(BF16)
        dm = _dot_nt(dout, wout_ref[...])
        dpb = (dm * sgp).astype(BF16)
        dps = (dm * sgs).astype(BF16)
        dg_ref[:, :D] = (dm * pb * sgp * (1.0 - sgp)).astype(BF16)
        dg_ref[:, D:] = (dm * psm * sgs * (1.0 - sgs)).astype(BF16)
        dyp_ref[...] = _dot_nt(dpb, wbp_ref[...]).astype(BF16)
        dys_ref[...] = _dot_nt(dps, wbs_ref[...]).astype(BF16)
        acc_out[...] += _dot_tn(mb, dout)
        acc_bp[...] += _dot_tn(yp, dpb)
        acc_bs[...] += _dot_tn(ys, dps)

        @pl.when(i == nblk - 1)
        def _():
            loss_ref[...] = 0.5 / n_feat * jnp.sum(acc_loss[...], axis=1, keepdims=True)
            vec_ref[...] = acc_vec[...]
            pltpu.sync_copy(acc_bp, dwbp_hbm)
            pltpu.sync_copy(acc_bs, dwbs_hbm)
            pltpu.sync_copy(acc_out, dwout_hbm)

    row = lambda c: pl.BlockSpec((tb, D), lambda i: (i, c))
    w = _full((D, D), single=True)
    return _pcall(body, name="head", grid=(nblk,),
                  out_shape=(jax.ShapeDtypeStruct((1, 1), F32), jax.ShapeDtypeStruct((rows, D), F32),
                             jax.ShapeDtypeStruct((rows, D), BF16), jax.ShapeDtypeStruct((rows, D), BF16),
                             jax.ShapeDtypeStruct((rows, 2 * D), BF16),
                             jax.ShapeDtypeStruct((D, D), F32), jax.ShapeDtypeStruct((D, D), F32),
                             jax.ShapeDtypeStruct((D, D), F32), jax.ShapeDtypeStruct((2, D), F32)),
                  in_specs=[row(0), row(0), row(4), row(5), row(0), row(0), _full((3, D)), _full((1, D)), w, w, w],
                  out_specs=(_full((1, 1)), row(0), row(0), row(0), pl.BlockSpec((tb, 2 * D), lambda i: (i, 0)),
                             ANY, ANY, ANY, _full((2, D))),
                  scratch_shapes=[pltpu.VMEM((D, D), F32), pltpu.VMEM((D, D), F32), pltpu.VMEM((D, D), F32),
                                  pltpu.VMEM((1, D), F32), pltpu.VMEM((2, D), F32)],
                  compiler_params=_params(("arbitrary",)))(x, target, proj, proj, y_pool, y_ssm, mod3, norm_post,
                                                           wbp, wbs, wout)


def _glu_bwd(dys, zp, ys_pre, gate, pm, pmt, glu_w):
    rows = dys.shape[0]
    tb = pm.shape[0]
    nblk = rows // tb

    def body(dys_ref, z_ref, ysp_ref, sg_ref, pm_ref, pmt_ref, gw_ref, dyp_ref, dz_ref, dgw_hbm, dgb_ref,
             acc_w, acc_b):
        i = pl.program_id(0)

        @pl.when(i == 0)
        def _():
            acc_w[...] = jnp.zeros_like(acc_w)
            acc_b[...] = jnp.zeros_like(acc_b)

        d_out = _dot(pm_ref[...], dys_ref[...])
        yg, dgelu = _gelu_parts(ysp_ref[...])
        ygb = yg.astype(BF16)
        sg = sg_ref[...]
        silu_z, dsilu_z = _silu_parts(z_ref[...].astype(F32))
        dz = d_out * (yg * sg) * dsilu_z
        dz_ref[...] = _dot(pmt_ref[...], dz.astype(BF16)).astype(BF16)
        dglu = d_out * silu_z
        dq = dglu * yg * sg * (1.0 - sg)
        dqb = dq.astype(BF16)
        acc_b[...] += jnp.sum(dq, axis=0, keepdims=True)
        acc_w[...] += _dot_tn(ygb, dqb)
        dyg = dglu * sg + _dot_nt(dqb, gw_ref[...])
        dyp_ref[...] = (dyg * dgelu).astype(BF16)

        @pl.when(i == nblk - 1)
        def _():
            dgb_ref[...] = acc_b[...]
            pltpu.sync_copy(acc_w, dgw_hbm)

    row = lambda c: pl.BlockSpec((tb, D), lambda i: (i, c))
    return _pcall(body, name="glu_bwd", grid=(nblk,),
                  out_shape=(jax.ShapeDtypeStruct((rows, D), BF16), jax.ShapeDtypeStruct((rows, D), BF16),
                             jax.ShapeDtypeStruct((D, D), F32), jax.ShapeDtypeStruct((1, D), F32)),
                  in_specs=[row(0), row(0), row(0), row(0), _full((tb, tb)), _full((tb, tb)),
                            _full((D, D), single=True)],
                  out_specs=(row(0), row(0), ANY, _full((1, D))),
                  scratch_shapes=[pltpu.VMEM((D, D), F32), pltpu.VMEM((1, D), F32)],
                  compiler_params=_params(("arbitrary",)))(dys, zp, ys_pre, gate, pm, pmt, glu_w)


def _ssm_bwd(dyp, up, states, carries, pmt, wb, wct, ptab, dvec, mat_grads, dpool_w, dw_in_rest):
    rows = dyp.shape[0]
    tb = pmt.shape[0]
    k_steps = tb // SUBLANES
    nblk = rows // tb
    n_mat = len(mat_grads)
    hosted = [*mat_grads, dpool_w, dw_in_rest]
    n_h = len(hosted)
    shard_rows = D // N_DEV
    pool_rows = dpool_w.shape[1] // N_DEV
    items = [_scatter_item(t, t, _rows_of(shard_rows)) for t in range(n_mat)] + \
            [_scatter_item(n_mat, n_mat, _pool_rows_of(pool_rows))] + \
            [_w_in_block_item(n_mat + 1, n_mat + 1, j, ssm_part=False) for j in range(W_IN_SHARD // W_IN_BLOCK)]
    n_in, n_out = 9, 5

    def body(*refs):
        dyp_ref, u_ref, s_ref, cin_ref, pmt_ref, wb_ref, wct_ref, p_ref, d_ref = refs[:n_in]
        src_refs = refs[n_in:n_in + n_h]
        du_ref, dbb_ref, dcc_ref, da_ref, dd_ref = refs[n_in + n_h:n_in + n_h + n_out]
        recv_refs = refs[n_in + n_h + n_out:n_in + 2 * n_h + n_out]
        (g_ref, carry_b, fin_ref, acc_wb, acc_wct, acc_da, acc_dd, dup_ref,
         *sems) = refs[n_in + 2 * n_h + n_out:]
        i = pl.program_id(0)

        @pl.when(i == 0)
        def _():
            _hosted_copies(items, src_refs, recv_refs, *sems, act="start")
            carry_b[...] = jnp.zeros_like(carry_b)
            acc_wb[...] = jnp.zeros_like(acc_wb)
            acc_wct[...] = jnp.zeros_like(acc_wct)
            acc_da[...] = jnp.zeros_like(acc_da)
            acc_dd[...] = jnp.zeros_like(acc_dd)

        def keep_own(acc, q, prod):
            for gl in range(16):
                r, c = slice(gl * G_H, (gl + 1) * G_H), (gl // 2) * 128
                acc[q, r, 0:128] += prod[r, c:c + 128]
                acc[q, r, 128:256] += prod[r, Q_W // 2 + c:Q_W // 2 + c + 128]

        dy = dyp_ref[...]
        up = u_ref[...]
        acc_dd[...] += jnp.sum(dy.astype(F32) * up.astype(F32), axis=0, keepdims=True)
        for q in range(N_Q):
            cols = slice(q * 256, (q + 1) * 256)
            g_ref[:, q * Q_W:(q + 1) * Q_W] = _dot(dy[:, cols], wct_ref[q])
            keep_own(acc_wct, q, _dot_tn(dy[:, cols], s_ref[:, q * Q_W:(q + 1) * Q_W].astype(BF16)))
        for q in range(N_Q):
            _scan_backward(q, g_ref, s_ref, p_ref, carry_b, cin_ref, fin_ref, acc_da, k_steps)
        for q in range(N_Q):
            cols = slice(q * 256, (q + 1) * 256)
            lam = g_ref[:, q * Q_W:(q + 1) * Q_W].astype(BF16)
            keep_own(acc_wb, q, _dot_tn(up[:, cols], lam))
            dup_ref[:, cols] = (_dot_nt(lam, wb_ref[q]) + d_ref[:, cols] * dy[:, cols].astype(F32)).astype(BF16)
        du_ref[...] = _dot(pmt_ref[...], dup_ref[...]).astype(BF16)

        @pl.when(i == nblk - 1)
        def _():
            da_ref[...] = acc_da[...]
            dd_ref[...] = acc_dd[...]
            lane = lax.broadcasted_iota(jnp.int32, (16 * G_H, 128), 1)
            row = lax.broadcasted_iota(jnp.int32, (16 * G_H, 128), 0)
            own = lane // G_P == (row // G_H) % 2
            spread = (lax.broadcasted_iota(jnp.int32, (G_P, 128), 1) % G_P ==
                      lax.broadcasted_iota(jnp.int32, (G_P, 128), 0)).astype(F32)
            for acc, out in ((acc_wb, dbb_ref), (acc_wct, dcc_ref)):
                for half in range(2):
                    for q in range(N_Q):
                        kept = jnp.where(own, acc[q, :, half * 128:(half + 1) * 128], 0.0)
                        out[half, q] = lax.dot_general(kept, spread, (((1,), (1,)), ((), ())),
                                                       preferred_element_type=F32, precision=lax.Precision.HIGHEST)
            _hosted_copies(items, src_refs, recv_refs, *sems, act="wait")

    rev = lambda c: pl.BlockSpec((tb, D), lambda i: (nblk - 1 - i, c))
    recv = [jax.ShapeDtypeStruct((N_DEV, shard_rows, D), F32)] * n_mat + \
           [jax.ShapeDtypeStruct((N_DEV, dpool_w.shape[0], pool_rows, dpool_w.shape[2]), F32),
            jax.ShapeDtypeStruct((N_DEV, D, W_IN_SHARD), BF16)]
    return _pcall(body, name="ssm_bwd", grid=(nblk,),
                  out_shape=(jax.ShapeDtypeStruct((rows, D), BF16),
                             jax.ShapeDtypeStruct((2, N_Q, 16 * G_H, G_P), F32),
                             jax.ShapeDtypeStruct((2, N_Q, 16 * G_H, G_P), F32),
                             jax.ShapeDtypeStruct((1, N_STATE), F32), jax.ShapeDtypeStruct((1, D), F32), *recv),
                  in_specs=[rev(0), rev(0), pl.BlockSpec((tb, N_STATE), lambda i: (nblk - 1 - i, 0)),
                            pl.BlockSpec((None, 1, N_STATE), lambda i: (nblk - 1 - i, 0, 0)),
                            _full((tb, tb)),
                            _full((N_Q, 256, Q_W), single=True), _full((N_Q, 256, Q_W), single=True),
                            _full((k_steps, N_STATE)), _full((1, D))] + [ANY] * n_h,
                  out_specs=(rev(0), _full((2, N_Q, 16 * G_H, G_P)), _full((2, N_Q, 16 * G_H, G_P)),
                             _full((1, N_STATE)), _full((1, D)), *([ANY] * n_h)),
                  scratch_shapes=[pltpu.VMEM((tb, N_STATE), F32), pltpu.VMEM((1, N_STATE), F32),
                                  pltpu.VMEM((SUBLANES, N_STATE), F32),
                                  pltpu.VMEM((N_Q, 16 * G_H, 256), F32), pltpu.VMEM((N_Q, 16 * G_H, 256), F32),
                                  pltpu.VMEM((1, N_STATE), F32), pltpu.VMEM((1, D), F32),
                                  pltpu.VMEM((tb, D), BF16)] + _sem_scratch(items),
                  compiler_params=_params(("arbitrary",), vmem=60 * 1024 * 1024),
                  )(dyp, up, states, carries, pmt, wb, wct, ptab, dvec, *hosted)


def _pool_bwd(dyp, pooled, proj, pool_w, pool_scale):
    rows = dyp.shape[0]
    tb = _tb(rows, 512)
    nblk = rows // tb

    def body(dy_ref, pooled_ref, z_ref, pw_ref, ps_ref, dp_ref, dpw_ref, dps_ref, ahead_ref):
        i = pl.program_id(0)
        blk = nblk - 1 - i

        @pl.when(i == 0)
        def _():
            ahead_ref[...] = jnp.zeros_like(ahead_ref)
            dpw_ref[...] = jnp.zeros_like(dpw_ref)
            dps_ref[...] = jnp.zeros_like(dps_ref)

        inv_counts = _inv_counts(tb, blk * tb)
        silu_z, dsilu_z = _silu_parts(z_ref[...].astype(F32))
        dy = dy_ref[...].astype(F32)
        for g, w in enumerate(POOL_WINDOWS):
            cols = slice(g * 256, (g + 1) * 256)
            pooled_b = pooled_ref[:, cols]
            mixed = _dot(pooled_b, pw_ref[g])
            scale = ps_ref[:, cols]
            dp_ref[:, D + g * 256:D + (g + 1) * 256] = (dy[:, cols] * (mixed * scale) * dsilu_z[:, cols]).astype(BF16)
            dms = dy[:, cols] * silu_z[:, cols]
            dps_ref[:, cols] += jnp.sum(dms * mixed, axis=0, keepdims=True)
            dmixed = (dms * scale).astype(BF16)
            dpw_ref[g] += _dot_tn(pooled_b, dmixed)
            dpooled = _dot_nt(dmixed, pw_ref[g])
            ratio = dpooled * inv_counts[g]
            acc = jnp.concatenate([ratio, ahead_ref[:, cols]], axis=0)
            ahead_ref[:, cols] = ratio[:HALO, :]
            s = 1
            while s < w:
                acc = acc + pltpu.roll(acc, tb + HALO - s, axis=0)
                s *= 2
            dp_ref[:, cols] = (acc[:tb, :] - dpooled).astype(BF16)

    rev = lambda c: pl.BlockSpec((tb, D), lambda i: (nblk - 1 - i, c))
    return _pcall(body, name="pool_bwd", grid=(nblk,),
                  out_shape=(jax.ShapeDtypeStruct((rows, 2 * D), BF16), jax.ShapeDtypeStruct((4, 256, 256), F32),
                             jax.ShapeDtypeStruct((1, D), F32)),
                  in_specs=[rev(0), rev(0), rev(1), _full((4, 256, 256)), _full((1, D))],
                  out_specs=(pl.BlockSpec((tb, 2 * D), lambda i: (nblk - 1 - i, 0)), _full((4, 256, 256)),
                             _full((1, D))),
                  scratch_shapes=[pltpu.VMEM((HALO, D), F32)],
                  compiler_params=_params(("arbitrary",)))(dyp, pooled, proj, pool_w, pool_scale)


def _dproj_specs(tb):
    return [pl.BlockSpec((tb, 2 * D), lambda i: (i, 0)), pl.BlockSpec((tb, D), lambda i: (i, 0)),
            pl.BlockSpec((tb, D), lambda i: (i, 0)), pl.BlockSpec((tb, 2 * D), lambda i: (i, 0))]


def _in_proj_bwd_x(x, dy, dpp, dus, dzs, dpg, mod3, norm_pre, w_in, dw_in_ssm, recv_w_in):
    rows = x.shape[0]
    tb = _tb(rows, 512)
    nblk = rows // tb
    items = [_w_in_block_item(0, 0, j, ssm_part=True) for j in range(W_IN_SHARD // W_IN_BLOCK)]
    sums_item = [_Item(0, 0, _whole, _slot)]

    def body(x_ref, dy_ref, dpp_ref, dus_ref, dzs_ref, dpg_ref, mod_ref, np_ref, w_ref,
             dw_src, _, gx_ref, recv_w, recv_sums, vec_ref, ssem, rsem, lsem, *sums_sems):
        src_refs, recv_refs, sems = (dw_src,), (recv_w,), (ssem, rsem, lsem)

        @pl.when(pl.program_id(0) == 0)
        def _():
            _hosted_copies(items, src_refs, recv_refs, *sems, act="start")
            vec_ref[...] = jnp.zeros_like(vec_ref)

        dh = _dot_nt(dpp_ref[...], w_ref[:, 0:2 * D])
        dh += _dot_nt(dus_ref[...], w_ref[:, 2 * D:3 * D])
        dh += _dot_nt(dzs_ref[...], w_ref[:, 3 * D:4 * D])
        dh += _dot_nt(dpg_ref[...], w_ref[:, 4 * D:6 * D])
        xn, r, _ = _prenorm(x_ref[...], mod_ref[...], np_ref[...])
        one_scale = 1.0 + mod_ref[1:2, :]
        vec_ref[0:1, :] += jnp.sum(dh, axis=0, keepdims=True)
        vec_ref[1:2, :] += jnp.sum(dh * xn, axis=0, keepdims=True) * np_ref[...]
        vec_ref[2:3, :] += jnp.sum(dh * xn, axis=0, keepdims=True) * one_scale
        gx_ref[...] = dy_ref[...] + _rms_bwd(dh * (np_ref[...] * one_scale), xn, r)

        @pl.when(pl.program_id(0) == nblk - 1)
        def _():
            _hosted_copies(sums_item, (vec_ref,), (recv_sums,), *sums_sems, act="start")
            _hosted_copies(items, src_refs, recv_refs, *sems, act="wait")
            _hosted_copies(sums_item, (vec_ref,), (recv_sums,), *sums_sems, act="wait")

    row = pl.BlockSpec((tb, D), lambda i: (i, 0))
    recv = (jax.ShapeDtypeStruct(recv_w_in.shape, recv_w_in.dtype), jax.ShapeDtypeStruct((N_DEV, 3, D), F32))
    return _pcall(body, name="in_proj_bwd_x", grid=(nblk,),
                  out_shape=(jax.ShapeDtypeStruct((rows, D), F32), *recv),
                  in_specs=[row, row] + _dproj_specs(tb) + [_full((3, D)), _full((1, D)),
                                                            _full((D, N_IN), single=True)] + [ANY] * 2,
                  out_specs=(row, ANY, ANY),
                  input_output_aliases={10: 1},
                  scratch_shapes=[pltpu.VMEM((3, D), F32)] + _sem_scratch(items) + _sem_scratch(sums_item),
                  compiler_params=_params(("arbitrary",)))(x, dy, dpp, dus, dzs, dpg, mod3, norm_pre, w_in,
                                                           dw_in_ssm, recv_w_in)


def _in_proj_bwd_w(name, x, dparts, mod3, norm_pre, gathered=()):
    rows = x.shape[0]
    tb = _tb(rows, 512)
    nblk = rows // tb
    widths = [p.shape[1] for p in dparts]
    n_p, n_g = len(dparts), len(gathered)
    items = [_Item(t, t, _whole, _slot) for t in range(n_g)]

    def body(x_ref, *rest):
        part_refs, (mod_ref, np_ref) = rest[:n_p], rest[n_p:n_p + 2]
        src_refs, dw_ref = rest[n_p + 2:n_p + 2 + n_g], rest[n_p + 2 + n_g]
        recv_refs, (acc, *sems) = rest[n_p + 3 + n_g:n_p + 3 + 2 * n_g], rest[n_p + 3 + 2 * n_g:]
        i = pl.program_id(0)

        @pl.when(i == 0)
        def _():
            if n_g:
                _hosted_copies(items, src_refs, recv_refs, *sems, act="start")
            acc[...] = jnp.zeros_like(acc)

        _, _, h = _prenorm(x_ref[...], mod_ref[...], np_ref[...])
        ht = h.astype(BF16)
        lo = 0
        for ref, w in zip(part_refs, widths):
            acc[:, lo:lo + w] += _dot_tn(ht, ref[...])
            lo += w

        @pl.when(i == nblk - 1)
        def _():
            dw_ref[...] = acc[...].astype(BF16)
            if n_g:
                _hosted_copies(items, src_refs, recv_refs, *sems, act="wait")

    row = pl.BlockSpec((tb, D), lambda i: (i, 0))
    out = _pcall(body, name=name, grid=(nblk,),
                 out_shape=(jax.ShapeDtypeStruct((D, sum(widths)), BF16),
                            *[jax.ShapeDtypeStruct((N_DEV,) + g.shape, g.dtype) for g in gathered]),
                 in_specs=[row] + [pl.BlockSpec((tb, w), lambda i: (i, 0)) for w in widths] +
                          [_full((3, D)), _full((1, D))] + [ANY] * n_g,
                 out_specs=(_full((D, sum(widths))), *([ANY] * n_g)),
                 scratch_shapes=[pltpu.VMEM((D, sum(widths)), F32)] + (_sem_scratch(items) if n_g else []),
                 compiler_params=_params(("arbitrary",)))(x, *dparts, mod3, norm_pre, *gathered)
    return out if n_g else out[0]


def _adamw_math(w, g, m, v):
    m = ADAM_B1 * m + (1.0 - ADAM_B1) * g
    v = ADAM_B2 * v + (1.0 - ADAM_B2) * (g * g)
    m_hat = m / (1.0 - ADAM_B1 ** ADAM_STEP)
    v_hat = v / (1.0 - ADAM_B2 ** ADAM_STEP)
    delta = -ADAM_LR * (m_hat / (jnp.sqrt(v_hat) + ADAM_EPS) + ADAM_WD * w)
    return delta, m, v


def _sum_sources(ref):
    g = ref[0].astype(F32)
    for s in range(1, N_DEV):
        g = g + ref[s].astype(F32)
    return g


def _adamw_reduce(name, parts, w, m, v):
    r, c = w.shape
    tr = r if r * c <= 256 * 1024 else max(8, (256 * 1024 // c) // 8 * 8)
    while r % tr:
        tr -= 8

    def body(p_ref, w_ref, m_ref, v_ref, g_ref, d_ref, nm_ref, nv_ref):
        g = _sum_sources(p_ref)
        g_ref[...] = g
        d_ref[...], nm_ref[...], nv_ref[...] = _adamw_math(w_ref[...], g, m_ref[...], v_ref[...])

    blk = pl.BlockSpec((tr, c), lambda i: (i, 0))
    return _pcall(body, name=name, grid=(r // tr,),
                  out_shape=tuple([jax.ShapeDtypeStruct((r, c), F32)] * 4),
                  in_specs=[pl.BlockSpec((N_DEV, tr, c), lambda i: (0, i, 0)), blk, blk, blk],
                  out_specs=(blk, blk, blk, blk),
                  compiler_params=_params(("arbitrary",)))(parts, w, m, v)


def _adamw_small(gs, ws, ms, vs):
    n = len(gs)

    def body(*refs):
        ins, outs = refs[:4 * n], refs[4 * n:]
        for t in range(n):
            g_ref, w_ref, m_ref, v_ref = ins[4 * t:4 * t + 4]
            outs[3 * t][...], outs[3 * t + 1][...], outs[3 * t + 2][...] = _adamw_math(
                w_ref[...], g_ref[...], m_ref[...], v_ref[...])

    vm = pl.BlockSpec(memory_space=pltpu.VMEM)
    flat = [a for t in range(n) for a in (gs[t], ws[t], ms[t], vs[t])]
    return _pcall(body, name="adamw_small",
                  out_shape=tuple(jax.ShapeDtypeStruct(w.shape, F32) for w in ws for _ in range(3)),
                  in_specs=[vm] * (4 * n), out_specs=tuple([vm] * (3 * n)), compiler_params=_params())(*flat)


def _sum_small(parts):
    n = len(parts)

    def body(*refs):
        for t in range(n):
            refs[n + t][...] = _sum_sources(refs[t])

    vm = pl.BlockSpec(memory_space=pltpu.VMEM)
    return _pcall(body, name="sum_small",
                  out_shape=tuple(jax.ShapeDtypeStruct(p.shape[1:], F32) for p in parts),
                  in_specs=[vm] * n, out_specs=tuple([vm] * n), compiler_params=_params())(*parts)


def _ada_update(c_all, dmod_cols, w, m, v):
    def body(c_ref, dm_ref, w_ref, m_ref, v_ref, g_ref, d_ref, nm_ref, nv_ref):
        ca = c_ref[...]
        g = lax.dot_general(ca * jax.nn.sigmoid(ca), dm_ref[...], (((0,), (0,)), ((), ())),
                            preferred_element_type=F32, precision=lax.Precision.HIGHEST)
        g_ref[...] = g
        d_ref[...], nm_ref[...], nv_ref[...] = _adamw_math(w_ref[...], g, m_ref[...], v_ref[...])

    vm = pl.BlockSpec(memory_space=pltpu.VMEM)
    return _pcall(body, name="ada_update", out_shape=tuple([jax.ShapeDtypeStruct(w.shape, F32)] * 4),
                  in_specs=[vm] * 5, out_specs=(vm, vm, vm, vm), compiler_params=_params())(c_all, dmod_cols, w, m, v)


def kernel(x, c, w_ada, b_ada, norm_pre, norm_post, w_in, pool_w, pool_scale, ssm_a_re, ssm_a_im, ssm_log_dt, ssm_b_re, ssm_b_im, ssm_c_re, ssm_c_im, ssm_d, glu_w, glu_b, w_branch_pool, w_branch_ssm, w_out, loss_target, m_w_ada, m_b_ada, m_norm_pre, m_norm_post, m_w_in, m_pool_w, m_pool_scale, m_ssm_a_re, m_ssm_a_im, m_ssm_log_dt, m_ssm_b_re, m_ssm_b_im, m_ssm_c_re, m_ssm_c_im, m_ssm_d, m_glu_w, m_glu_b, m_w_branch_pool, m_w_branch_ssm, m_w_out, v_w_ada, v_b_ada, v_norm_pre, v_norm_post, v_w_in, v_pool_w, v_pool_scale, v_ssm_a_re, v_ssm_a_im, v_ssm_log_dt, v_ssm_b_re, v_ssm_b_im, v_ssm_c_re, v_ssm_c_im, v_ssm_d, v_glu_w, v_glu_b, v_w_branch_pool, v_w_branch_ssm, v_w_out):
    given = dict(locals())
    me = _flat(_me())
    rows = x.shape[1]
    x2 = x[0]
    target = loss_target[0]
    ada_cols = w_ada.shape[2]

    tb_ssm = _tb(rows, 256)
    k_steps = tb_ssm // SUBLANES
    a_re, a_im = ssm_a_re[0], ssm_a_im[0]
    log_dt = ssm_log_dt[0].reshape(GROUPS, 1)
    b_re_t, b_im_t = ssm_b_re[0].transpose(0, 2, 1), ssm_b_im[0].transpose(0, 2, 1)
    s5_params = [a_re, a_im, log_dt, b_re_t, b_im_t, ssm_c_re[0], ssm_c_im[0]]

    f32_shards = [w_in[0], pool_w[0], glu_w[0], w_branch_pool[0], w_branch_ssm[0], w_out[0]]
    n_sh = len(f32_shards)

    def local_work(ins, outs):
        for src, dst in zip(ins[:n_sh], outs[:n_sh]):
            dst[...] = src[...].astype(BF16)
        _s5_prep_body(*ins[n_sh:], *outs[n_sh:])

    b_ada_s = lax.dynamic_slice(b_ada, (0, me * ada_cols), (1, ada_cols))
    c_all, mod_rows, *local = _ada_exchange(
        c, w_ada[0], b_ada_s, f32_shards + s5_params,
        [jax.ShapeDtypeStruct(a.shape, BF16) for a in f32_shards] + list(_s5_prep_structs(k_steps)), local_work)
    mod3 = mod_rows.reshape(3, D)
    shards, (wb, wct, pow_re, pow_im) = local[:n_sh], local[n_sh:]
    ptab = _state_layout(pow_re, pow_im)
    dvec = ssm_d[0].reshape(1, D)
    pm = _perm_matrix(tb_ssm)
    pmt = pm.T

    proj, w_in_g, pool_w_g, glu_g = _in_proj(x2, mod3, norm_pre, shards[0], shards[1:3])
    y_pool, pooled = _pool_fwd(proj, pool_w_g, pool_scale)
    y_ssm, ys_pre, carries, states, glu_gate, z_perm, u_perm, wbp_g, wbs_g, wout_g = _ssm_fwd(
        proj, pm, pmt, wb, wct, ptab, dvec, glu_g, glu_b, shards[3:])
    loss_part, dy, dyp, dys, dpg, dwbp, dwbs, dwout, head_vec = _head(
        x2, target, proj, y_pool, y_ssm, mod3, norm_post, wbp_g, wbs_g, wout_g)

    dpp, dpool_w, dpool_scale = _pool_bwd(dyp, pooled, proj, pool_w_g, pool_scale)
    dw_in_rest = _in_proj_bwd_w("in_proj_bwd_w_rest", x2, [dpp, dpg], mod3, norm_pre)
    dy_pre, dzs, dglu_w, dglu_b = _glu_bwd(dys, z_perm, ys_pre, glu_gate, pm, pmt, glu_g)
    dus, dbb, dcc, dabar, dd, p_glu, p_wbp, p_wbs, p_wout, p_pool_w, p_w_in = _ssm_bwd(
        dy_pre, u_perm, states, carries, pmt, wb, wct, ptab, dvec, [dglu_w, dwbp, dwbs, dwout], dpool_w,
        dw_in_rest)

    small32 = jnp.concatenate([head_vec, dpool_scale, dglu_b, dd, jnp.broadcast_to(loss_part, (1, D)),
                               jnp.zeros((2, D), F32), dabar.reshape(8, D)], axis=0)
    small16 = jnp.concatenate([dbb.reshape(2 * GROUPS, D), dcc.reshape(2 * GROUPS, D)], axis=0).astype(BF16)
    dw_in_ssm, p_small32, p_small16 = _in_proj_bwd_w("in_proj_bwd_w_ssm", x2, [dus, dzs], mod3, norm_pre,
                                                     gathered=(small32, small16))
    grad_x, p_w_in, p_pre = _in_proj_bwd_x(x2, dy, dpp, dus, dzs, dpg, mod3, norm_pre, w_in_g, dw_in_ssm, p_w_in)

    tot32, tot16, tot_pre = _sum_small([p_small32, p_small16, p_pre])
    d_abar_re, d_abar_im = _state_unlayout(tot32[8:16].reshape(N_STATE))
    d_bb_re, d_bb_im = tot16[0:64].reshape(GROUPS, G_H, G_P), tot16[64:128].reshape(GROUPS, G_H, G_P)
    g_a_re, g_a_im, g_log_dt, g_b_re_t, g_b_im_t = _s5_prep_bwd(
        a_re, a_im, log_dt, b_re_t, b_im_t, d_abar_re, d_abar_im, d_bb_re, d_bb_im)

    grads, deltas, new_m, new_v = {}, {}, {}, {}

    small = []

    def small_update(name, g2):
        small.append((name, g2))

    def shard_update(name, parts):
        shape = given[name].shape
        r2 = parts.shape[1:] if parts.ndim == 3 else (parts.shape[1] * parts.shape[2], parts.shape[3])
        w2, m2, v2 = (given[p + name].reshape(r2) for p in ("", "m_", "v_"))
        out = _adamw_reduce("adamw_" + name, parts.reshape((N_DEV,) + tuple(r2)), w2, m2, v2)
        grads[name], deltas[name], new_m[name], new_v[name] = (a.reshape(shape) for a in out)

    dmod_all = jnp.concatenate([p_pre[:, 0:2, :], p_small32[:, 0:1, :]], axis=1).reshape(N_DEV, 3 * D)
    dmod_cols = lax.dynamic_slice(dmod_all, (0, me * ada_cols), (N_DEV, ada_cols))
    out = _ada_update(c_all, dmod_cols, w_ada[0], m_w_ada[0], v_w_ada[0])
    grads['w_ada'], deltas['w_ada'], new_m['w_ada'], new_v['w_ada'] = (a.reshape(w_ada.shape) for a in out)

    small_update('b_ada', jnp.concatenate([tot_pre[0:2], tot32[0:1]], axis=0).reshape(1, 3 * D))
    small_update('norm_pre', tot_pre[2:3])
    small_update('norm_post', tot32[1:2])
    small_update('pool_scale', tot32[2:3])
    small_update('glu_b', tot32[3:4])
    small_update('ssm_d', tot32[4:5])
    small_update('ssm_a_re', g_a_re)
    small_update('ssm_a_im', g_a_im)
    small_update('ssm_log_dt', g_log_dt.reshape(1, GROUPS))
    small_update('ssm_b_re', g_b_re_t.transpose(0, 2, 1).reshape(GROUPS, G_P * G_H))
    small_update('ssm_b_im', g_b_im_t.transpose(0, 2, 1).reshape(GROUPS, G_P * G_H))
    small_update('ssm_c_re', tot16[128:192])
    small_update('ssm_c_im', -tot16[192:256])
    flat = _adamw_small([g2 for _, g2 in small],
                        *[[given[p + name].reshape(g2.shape) for name, g2 in small] for p in ("", "m_", "v_")])
    for t, (name, g2) in enumerate(small):
        shape = given[name].shape
        grads[name], deltas[name], new_m[name], new_v[name] = (
            a.reshape(shape) for a in (g2, *flat[3 * t:3 * t + 3]))
    shard_update('w_in', p_w_in)
    shard_update('pool_w', p_pool_w)
    shard_update('glu_w', p_glu)
    shard_update('w_branch_pool', p_wbp)
    shard_update('w_branch_ssm', p_wbs)
    shard_update('w_out', p_wout)

    return (tot32[5, 0], grad_x[None], *[grads[n] for n in WEIGHTS], *[deltas[n] for n in WEIGHTS],
            *[new_m[n] for n in WEIGHTS], *[new_v[n] for n in WEIGHTS])
```

```python
import math
from typing import Callable, NamedTuple, Optional

import jax
import jax.numpy as jnp
from jax import lax
from jax.experimental import pallas as pl
from jax.experimental.pallas import tpu as pltpu

F32 = jnp.float32
BF16 = jnp.bfloat16
MESH = pl.DeviceIdType.MESH

D = 1024
N_DEV = 8
N_IN = 6 * D
GROUPS = 64
G_H = 16
G_P = 64
N_Q = 4
Q_W = 2 * 16 * G_P
N_STATE = N_Q * Q_W
POOL_WINDOWS = (2, 4, 8, 16)
HALO = 16
RMS_EPS = 1e-6
SUBLANES = 8
LANE_CHUNK = 512
SCAN_UNROLL = 2
VMEM_LIMIT = 56 * 1024 * 1024

ADAM_LR = 0.001
ADAM_B1 = 0.9
ADAM_B2 = 0.999
ADAM_EPS = 1e-08
ADAM_WD = 0.01
ADAM_STEP = 10

WEIGHTS = ['w_ada', 'b_ada', 'norm_pre', 'norm_post', 'w_in', 'pool_w', 'pool_scale', 'ssm_a_re',
           'ssm_a_im', 'ssm_log_dt', 'ssm_b_re', 'ssm_b_im', 'ssm_c_re', 'ssm_c_im', 'ssm_d', 'glu_w',
           'glu_b', 'w_branch_pool', 'w_branch_ssm', 'w_out']


def _pcall(body, **kw):
    return pl.pallas_call(body, **kw)


def _params(sem=None, vmem=VMEM_LIMIT):
    return pltpu.CompilerParams(dimension_semantics=sem, vmem_limit_bytes=vmem)


def _tb(rows, pref):
    return pref if rows % pref == 0 and rows // pref >= 2 else rows // 2


def _full(shape, single=False):
    nd = len(shape)
    if single:
        return pl.BlockSpec(shape, lambda i: (0,) * nd, pipeline_mode=pl.Buffered(1))
    return pl.BlockSpec(shape, lambda i: (0,) * nd)


ANY = pl.BlockSpec(memory_space=pl.ANY)


def _me():
    return lax.axis_index("x"), lax.axis_index("y"), lax.axis_index("c")


def _flat(p):
    return 4 * p[0] + 2 * p[1] + p[2]


def _peer(k):
    x, y, c = _me()
    return (1 - x if k & 4 else x, 1 - y if k & 2 else y, 1 - c if k & 1 else c)


def _silu_parts(z):
    s = jax.nn.sigmoid(z)
    return z * s, s * (1.0 + z * (1.0 - s))


_GELU_C = math.sqrt(2.0 / math.pi)


def _gelu_parts(x):
    x2 = x * x
    t = jnp.tanh(_GELU_C * (x + 0.044715 * x * x2))
    g = 0.5 * x * (1.0 + t)
    dg = 0.5 * (1.0 + t) + 0.5 * x * (1.0 - t * t) * (_GELU_C * (1.0 + 3.0 * 0.044715 * x2))
    return g, dg


def _dot(a, b):
    return jnp.dot(a, b, preferred_element_type=F32)


def _dot_nt(a, b):
    return lax.dot_general(a, b, (((1,), (1,)), ((), ())), preferred_element_type=F32)


def _dot_tn(a, b):
    return lax.dot_general(a, b, (((0,), (0,)), ((), ())), preferred_element_type=F32)


def _rms_parts(x):
    r = lax.rsqrt(jnp.mean(x * x, axis=-1, keepdims=True) + RMS_EPS)
    return x * r, r


def _rms_bwd(dxn, xn, r):
    return r * (dxn - xn * jnp.mean(dxn * xn, axis=-1, keepdims=True))


def _ada_exchange(c, w_ada_s, b_ada_s, local_ins, local_outs, local_work):
    cols = w_ada_s.shape[1]
    n_li, n_lo = len(local_ins), len(local_outs)

    def body(c_ref, w_ref, b_ref, *rest):
        li_refs, call_ref, mod_ref = rest[:n_li], rest[n_li], rest[n_li + 1]
        lo_refs, (part_ref, ssem, rsem, lsem) = rest[n_li + 2:n_li + 2 + n_lo], rest[n_li + 2 + n_lo:]
        me3 = _me()
        me = _flat(me3)
        mine = pltpu.make_async_copy(c_ref, call_ref.at[pl.ds(me, 1), :], lsem.at[0])
        mine.start()
        sends = []
        for k in range(1, N_DEV):
            cp = pltpu.make_async_remote_copy(src_ref=c_ref, dst_ref=call_ref.at[pl.ds(me, 1), :],
                                              send_sem=ssem.at[k - 1], recv_sem=rsem.at[k - 1],
                                              device_id=_peer(k), device_id_type=MESH)
            cp.start()
            sends.append(cp)
        local_work(li_refs, lo_refs)
        mine.wait()
        for k in range(1, N_DEV):
            p = _flat(_peer(k))
            pltpu.make_async_remote_copy(src_ref=c_ref, dst_ref=call_ref.at[pl.ds(p, 1), :],
                                         send_sem=ssem.at[k - 1], recv_sem=rsem.at[k - 1],
                                         device_id=_peer(k), device_id_type=MESH).wait_recv()
        for cp in sends:
            cp.wait_send()
        ca = call_ref[...]
        act = ca * jax.nn.sigmoid(ca)
        part_ref[...] = jnp.dot(act, w_ref[...], preferred_element_type=F32,
                                precision=lax.Precision.HIGHEST) + b_ref[...]
        own = pltpu.make_async_copy(part_ref.at[pl.ds(me, 1), :], mod_ref.at[pl.ds(me, 1), :], lsem.at[1])
        own.start()
        sends = []
        for k in range(1, N_DEV):
            p = _flat(_peer(k))
            s = N_DEV - 1 + k - 1
            cp = pltpu.make_async_remote_copy(src_ref=part_ref.at[pl.ds(p, 1), :],
                                              dst_ref=mod_ref.at[pl.ds(me, 1), :],
                                              send_sem=ssem.at[s], recv_sem=rsem.at[s],
                                              device_id=_peer(k), device_id_type=MESH)
            cp.start()
            sends.append(cp)
        own.wait()
        for k in range(1, N_DEV):
            p = _flat(_peer(k))
            s = N_DEV - 1 + k - 1
            pltpu.make_async_remote_copy(src_ref=part_ref.at[pl.ds(p, 1), :],
                                         dst_ref=mod_ref.at[pl.ds(p, 1), :],
                                         send_sem=ssem.at[s], recv_sem=rsem.at[s],
                                         device_id=_peer(k), device_id_type=MESH).wait_recv()
        for cp in sends:
            cp.wait_send()

    vm = pl.BlockSpec(memory_space=pltpu.VMEM)
    return _pcall(
        body, name="ada_exchange",
        out_shape=(jax.ShapeDtypeStruct((N_DEV, D), F32), jax.ShapeDtypeStruct((N_DEV, cols), F32), *local_outs),
        in_specs=[vm] * (3 + n_li), out_specs=tuple([vm] * (2 + n_lo)),
        scratch_shapes=[pltpu.VMEM((N_DEV, cols), F32),
                        pltpu.SemaphoreType.DMA((2 * (N_DEV - 1),)),
                        pltpu.SemaphoreType.DMA((2 * (N_DEV - 1),)),
                        pltpu.SemaphoreType.DMA((2,))],
        compiler_params=_params(),
    )(c, w_ada_s, b_ada_s, *local_ins)


class _Item(NamedTuple):
    src: int
    out: int
    src_view: Callable
    dst_view: Callable
    pred: Optional[Callable] = None


def _when(pred, dest, fn):
    if pred is None:
        fn()
    else:
        pl.when(pred(dest))(fn)


def _n_sems(items):
    return len(items) * (N_DEV - 1)


def _hosted_copies(items, srcs, outs, ssem, rsem, lsem, act):
    me = _flat(_me())
    for t, it in enumerate(items):
        local = lambda t=t, it=it: pltpu.make_async_copy(
            it.src_view(srcs[it.src], me), it.dst_view(outs[it.out], me), lsem.at[t])
        if act == "start":
            _when(it.pred, me, lambda local=local: local().start())
        else:
            _when(it.pred, me, lambda local=local: local().wait())
    for k in range(1, N_DEV):
        p3 = _peer(k)
        p = _flat(p3)
        for t, it in enumerate(items):
            s = t * (N_DEV - 1) + k - 1
            send = lambda it=it, s=s, p=p, p3=p3: pltpu.make_async_remote_copy(
                src_ref=it.src_view(srcs[it.src], p), dst_ref=it.dst_view(outs[it.out], me),
                send_sem=ssem.at[s], recv_sem=rsem.at[s], device_id=p3, device_id_type=MESH)
            recv = lambda it=it, s=s, p=p, p3=p3: pltpu.make_async_remote_copy(
                src_ref=it.src_view(srcs[it.src], p), dst_ref=it.dst_view(outs[it.out], p),
                send_sem=ssem.at[s], recv_sem=rsem.at[s], device_id=p3, device_id_type=MESH)
            if act == "start":
                _when(it.pred, p, lambda send=send: send().start())
            else:
                _when(it.pred, me, lambda recv=recv: recv().wait_recv())
                _when(it.pred, p, lambda send=send: send().wait_send())


def _sem_scratch(items):
    return [pltpu.SemaphoreType.DMA((_n_sems(items),)), pltpu.SemaphoreType.DMA((_n_sems(items),)),
            pltpu.SemaphoreType.DMA((len(items),))]


def _whole(ref, dest):
    return ref


def _slot(ref, sender):
    return ref.at[sender]


def _rows_of(rows):
    return lambda ref, dev: ref.at[pl.ds(dev * rows, rows), :]


def _pool_rows_of(rows):
    return lambda ref, dev: ref.at[:, pl.ds(dev * rows, rows), :]


def _gather_item(src, out, dst_view):
    return _Item(src, out, _whole, dst_view)


def _scatter_item(src, out, src_view):
    return _Item(src, out, src_view, _slot)


W_IN_BLOCK = 256
W_IN_SHARD = N_IN // N_DEV
SSM_BLOCKS = (2 * D // W_IN_BLOCK, 4 * D // W_IN_BLOCK)


def _w_in_block_item(src, out, j, ssm_part):
    def block(dest):
        return (W_IN_SHARD // W_IN_BLOCK) * dest + j

    def in_ssm(dest):
        b = block(dest)
        return (b >= SSM_BLOCKS[0]) & (b < SSM_BLOCKS[1])

    def src_view(ref, dest):
        b = block(dest)
        local = b - SSM_BLOCKS[0] if ssm_part else jnp.where(b < SSM_BLOCKS[0], b, b - (SSM_BLOCKS[1] - SSM_BLOCKS[0]))
        local = jnp.clip(local, 0, ref.shape[1] // W_IN_BLOCK - 1)
        return ref.at[:, pl.ds(local * W_IN_BLOCK, W_IN_BLOCK)]

    def dst_view(ref, sender):
        return ref.at[sender, :, pl.ds(j * W_IN_BLOCK, W_IN_BLOCK)]

    pred = in_ssm if ssm_part else (lambda dest: jnp.logical_not(in_ssm(dest)))
    return _Item(src, out, src_view, dst_view, pred)


def _s5_discretise(a_re, a_im, log_dt, b_re_t, b_im_t):
    dt = jnp.exp(log_dt)
    lam_re = jnp.minimum(a_re, -1e-4)
    lam_im = a_im
    mag = jnp.exp(lam_re * dt)
    abar_re = mag * jnp.cos(lam_im * dt)
    abar_im = mag * jnp.sin(lam_im * dt)
    den = lam_re * lam_re + lam_im * lam_im
    num_re = abar_re - 1.0
    f_re = (num_re * lam_re + abar_im * lam_im) / den
    f_im = (abar_im * lam_re - num_re * lam_im) / den
    f_re, f_im = f_re[:, None, :], f_im[:, None, :]
    bb_re = f_re * b_re_t - f_im * b_im_t
    bb_im = f_re * b_im_t + f_im * b_re_t
    return abar_re, abar_im, bb_re, bb_im


def _group_masks():
    spread = lax.broadcasted_iota(jnp.int32, (G_P, 16 * G_P), 1) % G_P == lax.broadcasted_iota(
        jnp.int32, (G_P, 16 * G_P), 0)
    own = lax.broadcasted_iota(jnp.int32, (16 * G_H, 16 * G_P), 0) // G_H == lax.broadcasted_iota(
        jnp.int32, (16 * G_H, 16 * G_P), 1) // G_P
    return spread, own


def _s5_prep_structs(n_pow):
    return (jax.ShapeDtypeStruct((N_Q, 16 * G_H, Q_W), BF16), jax.ShapeDtypeStruct((N_Q, 16 * G_H, Q_W), BF16),
            jax.ShapeDtypeStruct((n_pow, GROUPS, G_P), F32), jax.ShapeDtypeStruct((n_pow, GROUPS, G_P), F32))


def _s5_prep_body(ar_ref, ai_ref, ld_ref, br_ref, bi_ref, cr_ref, ci_ref, wb_ref, wct_ref, pr_ref, pi_ref):
    abar_re, abar_im, bb_re, bb_im = _s5_discretise(ar_ref[...], ai_ref[...], ld_ref[...], br_ref[...], bi_ref[...])
    spread, own = _group_masks()
    spread = spread.astype(BF16)
    for ref, parts in ((wb_ref, (bb_re, bb_im)), (wct_ref, (cr_ref[...], -ci_ref[...]))):
        for half, t in enumerate(parts):
            for q in range(N_Q):
                blocks = t[q * 16:(q + 1) * 16].reshape(16 * G_H, G_P).astype(BF16)
                dense = jnp.where(own, _dot(blocks, spread), 0.0)
                ref[q, :, half * (Q_W // 2):(half + 1) * (Q_W // 2)] = dense.astype(BF16)
    p_re, p_im = abar_re, abar_im
    pr_ref[0] = p_re
    pi_ref[0] = p_im
    for k in range(1, pr_ref.shape[0]):
        p_re, p_im = p_re * abar_re - p_im * abar_im, p_re * abar_im + p_im * abar_re
        pr_ref[k] = p_re
        pi_ref[k] = p_im


def _s5_prep_bwd(a_re, a_im, log_dt, b_re_t, b_im_t, d_abar_re, d_abar_im, d_bb_re, d_bb_im):
    def body(ar_ref, ai_ref, ld_ref, br_ref, bi_ref, dar_ref, dai_ref, dbr_ref, dbi_ref,
             gar_ref, gai_ref, gld_ref, gbr_ref, gbi_ref):
        _, vjp = jax.vjp(_s5_discretise, ar_ref[...], ai_ref[...], ld_ref[...], br_ref[...], bi_ref[...])
        g = vjp((dar_ref[...], dai_ref[...], dbr_ref[...], dbi_ref[...]))
        gar_ref[...] = g[0]
        gai_ref[...] = g[1]
        gld_ref[...] = g[2]
        gbr_ref[...] = g[3]
        gbi_ref[...] = g[4]

    vm = pl.BlockSpec(memory_space=pltpu.VMEM)
    ins = (a_re, a_im, log_dt, b_re_t, b_im_t)
    return _pcall(body, name="s5_prep_bwd",
                  out_shape=tuple(jax.ShapeDtypeStruct(a.shape, F32) for a in ins),
                  in_specs=[vm] * 9, out_specs=tuple([vm] * 5), compiler_params=_params(),
                  )(*ins, d_abar_re, d_abar_im, d_bb_re, d_bb_im)


def _state_layout(re, im):
    lead = re.shape[:-2]
    r = re.reshape(lead + (N_Q, 1, 16 * G_P))
    i = im.reshape(lead + (N_Q, 1, 16 * G_P))
    return jnp.concatenate([r, i], axis=-2).reshape(lead + (N_STATE,))


def _state_unlayout(v):
    v4 = v.reshape(N_Q, 2, 16, G_P)
    return v4[:, 0].reshape(GROUPS, G_P), v4[:, 1].reshape(GROUPS, G_P)


def _perm_matrix(tb):
    k_steps = tb // SUBLANES
    r = jnp.arange(tb)
    src = (r % SUBLANES) * k_steps + r // SUBLANES
    return (src[:, None] == jnp.arange(tb)[None, :]).astype(BF16)


def _lane_chunks(q):
    for lc in range(Q_W // 2 // LANE_CHUNK):
        re = q * Q_W + lc * LANE_CHUNK
        yield re, re + Q_W // 2


def _steps(lo, hi, body, init):
    if hi - lo <= SCAN_UNROLL:
        for k in range(lo, hi):
            init = body(k, init)
        return init
    trips = (hi - lo) // SCAN_UNROLL

    def trip(j, carry):
        for u in range(SCAN_UNROLL):
            carry = body(lo + j * SCAN_UNROLL + u, carry)
        return carry

    carry = lax.fori_loop(0, trips, trip, init)
    for k in range(lo + trips * SCAN_UNROLL, hi):
        carry = body(k, carry)
    return carry


def _tile(k):
    if isinstance(k, int):
        return pl.ds(k * SUBLANES, SUBLANES)
    return pl.ds(pl.multiple_of(k * SUBLANES, SUBLANES), SUBLANES)


def _scan_forward(q, s_ref, p_ref, carry_ref, enter_ref, fin_ref, k_steps):
    for re, im in _lane_chunks(q):
        lr, li = pl.ds(re, LANE_CHUNK), pl.ds(im, LANE_CHUNK)
        a_re = jnp.broadcast_to(p_ref[0:1, lr], (SUBLANES, LANE_CHUNK))
        a_im = jnp.broadcast_to(p_ref[0:1, li], (SUBLANES, LANE_CHUNK))

        def local(k, st):
            sr, si = st
            rows = _tile(k)
            nr = a_re * sr - a_im * si + s_ref[rows, lr]
            ni = a_re * si + a_im * sr + s_ref[rows, li]
            s_ref[rows, lr] = nr
            s_ref[rows, li] = ni
            return nr, ni

        zero = jnp.zeros((SUBLANES, LANE_CHUNK), F32)
        fr, fi = _steps(0, k_steps, local, (zero, zero))
        fin_ref[:, lr] = fr
        fin_ref[:, li] = fi
        ak_re, ak_im = p_ref[k_steps - 1:k_steps, lr], p_ref[k_steps - 1:k_steps, li]
        c_re, c_im = carry_ref[:, lr], carry_ref[:, li]
        for seg in range(SUBLANES):
            enter_ref[seg:seg + 1, lr] = c_re
            enter_ref[seg:seg + 1, li] = c_im
            f_re, f_im = fin_ref[seg:seg + 1, lr], fin_ref[seg:seg + 1, li]
            c_re, c_im = f_re + ak_re * c_re - ak_im * c_im, f_im + ak_re * c_im + ak_im * c_re
        carry_ref[:, lr] = c_re
        carry_ref[:, li] = c_im
        e_re, e_im = enter_ref[:, lr], enter_ref[:, li]

        def fix(k, _):
            rows = _tile(k)
            p_re = p_ref[pl.ds(k, 1), lr]
            p_im = p_ref[pl.ds(k, 1), li]
            s_ref[rows, lr] = s_ref[rows, lr] + (p_re * e_re - p_im * e_im)
            s_ref[rows, li] = s_ref[rows, li] + (p_re * e_im + p_im * e_re)
            return 0

        _steps(0, k_steps, fix, 0)


def _scan_backward(q, g_ref, s_ref, p_ref, carry_ref, s_in_ref, fin_ref, da_ref, k_steps):
    seg_id = lax.broadcasted_iota(jnp.int32, (SUBLANES, LANE_CHUNK), 0)
    for re, im in _lane_chunks(q):
        lr, li = pl.ds(re, LANE_CHUNK), pl.ds(im, LANE_CHUNK)
        a_re = jnp.broadcast_to(p_ref[0:1, lr], (SUBLANES, LANE_CHUNK))
        a_im = jnp.broadcast_to(p_ref[0:1, li], (SUBLANES, LANE_CHUNK))

        def local(j, st):
            sr, si = st
            rows = _tile(k_steps - 1 - j)
            nr = a_re * sr + a_im * si + g_ref[rows, lr]
            ni = a_re * si - a_im * sr + g_ref[rows, li]
            g_ref[rows, lr] = nr
            g_ref[rows, li] = ni
            return nr, ni

        zero = jnp.zeros((SUBLANES, LANE_CHUNK), F32)
        fr, fi = _steps(0, k_steps, local, (zero, zero))
        fin_ref[:, lr] = fr
        fin_ref[:, li] = fi
        ak_re, ak_im = p_ref[k_steps - 1:k_steps, lr], p_ref[k_steps - 1:k_steps, li]
        c_re, c_im = carry_ref[:, lr], carry_ref[:, li]
        lam_in = [None] * SUBLANES
        for seg in reversed(range(SUBLANES)):
            lam_in[seg] = (c_re, c_im)
            f_re, f_im = fin_ref[seg:seg + 1, lr], fin_ref[seg:seg + 1, li]
            c_re, c_im = f_re + ak_re * c_re + ak_im * c_im, f_im + ak_re * c_im - ak_im * c_re
        carry_ref[:, lr] = c_re
        carry_ref[:, li] = c_im
        for seg in range(SUBLANES):
            fin_ref[seg:seg + 1, lr] = lam_in[seg][0]
            fin_ref[seg:seg + 1, li] = lam_in[seg][1]
        e_re, e_im = fin_ref[:, lr], fin_ref[:, li]

        def fix_with(k, acc, sp_re, sp_im):
            acc_re, acc_im = acc
            rows = _tile(k)
            p_re = p_ref[pl.ds(k_steps - 1 - k, 1), lr]
            p_im = p_ref[pl.ds(k_steps - 1 - k, 1), li]
            l_re = g_ref[rows, lr] + (p_re * e_re + p_im * e_im)
            l_im = g_ref[rows, li] + (p_re * e_im - p_im * e_re)
            g_ref[rows, lr] = l_re
            g_ref[rows, li] = l_im
            return acc_re + (l_re * sp_re + l_im * sp_im), acc_im + (l_im * sp_re - l_re * sp_im)

        def fix(k, acc):
            prev = _tile(k - 1)
            return fix_with(k, acc, s_ref[prev, lr], s_ref[prev, li])

        last = _tile(k_steps - 1)
        before_re = jnp.where(seg_id == 0, s_in_ref[:, lr], pltpu.roll(s_ref[last, lr], 1, axis=0))
        before_im = jnp.where(seg_id == 0, s_in_ref[:, li], pltpu.roll(s_ref[last, li], 1, axis=0))
        acc = fix_with(0, (zero, zero), before_re, before_im)
        acc_re, acc_im = _steps(1, k_steps, fix, acc)
        da_ref[:, lr] = da_ref[:, lr] + jnp.sum(acc_re, axis=0, keepdims=True)
        da_ref[:, li] = da_ref[:, li] + jnp.sum(acc_im, axis=0, keepdims=True)


def _prenorm(x, mod3, norm_pre):
    xn, r = _rms_parts(x)
    return xn, r, xn * norm_pre * (1.0 + mod3[1:2, :]) + mod3[0:1, :]


CHIP_FLIPS = (4, 2, 6)


def _shard_order(me):
    flips = [0, 1] + [f + c for f in CHIP_FLIPS for c in (0, 1)]
    return jnp.stack([me ^ f for f in flips]).astype(jnp.int32)


def _in_proj(x, mod3, norm_pre, w_in_s, shards):
    rows = x.shape[0]
    tb = _tb(rows, 2048)
    nblk = rows // tb
    n_sh = len(shards)
    last_step = N_DEV - 1
    items = [_gather_item(0, 0, _pool_rows_of(shards[0].shape[1]))] + \
            [_gather_item(t, t, _rows_of(shards[t].shape[0])) for t in range(1, n_sh)]

    def body(order_ref, x_ref, mod_ref, np_ref, w_src, *rest):
        src_refs, proj_ref, w_full, out_refs = rest[:n_sh], rest[n_sh], rest[n_sh + 1], rest[n_sh + 2:2 * n_sh + 2]
        h_scr, wg, ssem, rsem, lsem, *sems = rest[2 * n_sh + 2:]
        s, i = pl.program_id(0), pl.program_id(1)
        me3 = _me()
        me = _flat(me3)
        sibling = _peer(1)

        def own_copy(slot, k):
            return pltpu.make_async_remote_copy(src_ref=w_src, dst_ref=wg.at[me], send_sem=ssem.at[slot],
                                                recv_sem=rsem.at[slot], device_id=_peer(k), device_id_type=MESH)

        def passed_copy(j):
            p = _flat(_peer(CHIP_FLIPS[j]))
            return pltpu.make_async_remote_copy(src_ref=wg.at[p], dst_ref=wg.at[p], send_sem=ssem.at[4 + j],
                                                recv_sem=rsem.at[4 + j], device_id=sibling, device_id_type=MESH)

        def arrival(slot, flip):
            p = _flat(_peer(flip))
            pltpu.make_async_remote_copy(src_ref=w_src, dst_ref=wg.at[p], send_sem=ssem.at[slot],
                                         recv_sem=rsem.at[slot], device_id=sibling, device_id_type=MESH).wait_recv()

        def keep(t):
            p = order_ref[t]
            return pltpu.make_async_copy(wg.at[p], w_full.at[:, pl.ds(p * W_IN_SHARD, W_IN_SHARD)], lsem.at[1 + t])

        first = i == 0
        for t in range(last_step):
            pl.when(first & (s == t + 1))(lambda t=t: keep(t).start())

        @pl.when(first & (s == 0))
        def _():
            mine = pltpu.make_async_copy(w_src, wg.at[me], lsem.at[0])
            mine.start()
            own_copy(0, 1).start()
            for j, f in enumerate(CHIP_FLIPS[:2]):
                own_copy(1 + j, f).start()
            mine.wait()

        @pl.when(first & (s == 1))
        def _():
            arrival(0, 1)

        for j, f in enumerate(CHIP_FLIPS):
            @pl.when(first & (s == 2 + 2 * j))
            def _(j=j, f=f):
                arrival(1 + j, f)
                passed_copy(j).start()
                if j == 0:
                    own_copy(3, CHIP_FLIPS[2]).start()

            @pl.when(first & (s == 3 + 2 * j))
            def _(j=j, f=f):
                arrival(4 + j, f + 1)

        @pl.when(first & (s == last_step - 1))
        def _():
            _hosted_copies(items, src_refs, out_refs, *sems, act="start")

        rows_i = pl.ds(pl.multiple_of(i * tb, tb), tb)

        @pl.when(s == 0)
        def _():
            _, _, h = _prenorm(x_ref[...], mod_ref[...], np_ref[...])
            h_scr[rows_i, :] = h.astype(BF16)

        proj_ref[...] = _dot(h_scr[rows_i, :], wg[order_ref[s]]).astype(BF16)

        @pl.when((s == last_step) & (i == nblk - 1))
        def _():
            own_copy(0, 1).wait_send()
            for j, f in enumerate(CHIP_FLIPS):
                own_copy(1 + j, f).wait_send()
                passed_copy(j).wait_send()
            keep(last_step).start()
            for t in range(N_DEV):
                keep(t).wait()
            _hosted_copies(items, src_refs, out_refs, *sems, act="wait")

    full = [jax.ShapeDtypeStruct((4, 256, 256), BF16)] + [jax.ShapeDtypeStruct((D, D), BF16)] * (n_sh - 1)
    grid_spec = pltpu.PrefetchScalarGridSpec(
        num_scalar_prefetch=1, grid=(N_DEV, nblk),
        in_specs=[pl.BlockSpec((tb, D), lambda s, i, order: (jnp.where(s == 0, i, nblk - 1), 0)),
                  pl.BlockSpec((3, D), lambda s, i, order: (0, 0)), pl.BlockSpec((1, D), lambda s, i, order: (0, 0)),
                  ANY] + [ANY] * n_sh,
        out_specs=(pl.BlockSpec((tb, W_IN_SHARD), lambda s, i, order: (i, order[s])), ANY, *([ANY] * n_sh)),
        scratch_shapes=[pltpu.VMEM((rows, D), BF16), pltpu.VMEM((N_DEV, D, W_IN_SHARD), BF16),
                        pltpu.SemaphoreType.DMA((N_DEV - 1,)), pltpu.SemaphoreType.DMA((N_DEV - 1,)),
                        pltpu.SemaphoreType.DMA((1 + N_DEV,))] + _sem_scratch(items))
    return _pcall(body, name="in_proj", grid_spec=grid_spec,
                  out_shape=(jax.ShapeDtypeStruct((rows, N_IN), BF16), jax.ShapeDtypeStruct((D, N_IN), BF16), *full),
                  compiler_params=_params(("arbitrary", "arbitrary")),
                  )(_shard_order(_flat(_me())), x, mod3, norm_pre, w_in_s, *shards)


def _pool_windows(ext, tb, first_row):
    inv_counts = _inv_counts(tb, first_row)
    pooled = []
    for g, w in enumerate(POOL_WINDOWS):
        acc = ext[:, g * 256:(g + 1) * 256]
        tok = acc[HALO:, :]
        s = 1
        while s < w:
            acc = acc + pltpu.roll(acc, s, axis=0)
            s *= 2
        pooled.append(acc[HALO:, :] * inv_counts[g] - tok)
    return pooled, inv_counts


def _inv_counts(tb, first_row):
    pos = (first_row + lax.broadcasted_iota(jnp.int32, (tb, 1), 0) + 1).astype(F32)
    return [1.0 / jnp.minimum(pos, float(w)) for w in POOL_WINDOWS]


def _pool_fwd(proj, pool_w, pool_scale):
    rows = proj.shape[0]
    tb = _tb(rows, 1024)
    hb = tb // HALO

    def body(u_ref, halo_ref, z_ref, pw_ref, ps_ref, y_ref, pooled_ref):
        i = pl.program_id(0)
        u = u_ref[...].astype(F32)
        halo = jnp.where(i > 0, halo_ref[...].astype(F32), 0.0)
        pooled, _ = _pool_windows(jnp.concatenate([halo, u], axis=0), tb, i * tb)
        silu_z, _ = _silu_parts(z_ref[...].astype(F32))
        for g in range(4):
            cols = slice(g * 256, (g + 1) * 256)
            pooled_b = pooled[g].astype(BF16)
            pooled_ref[:, cols] = pooled_b
            mixed = _dot(pooled_b, pw_ref[g])
            y_ref[:, cols] = (mixed * ps_ref[:, cols] * silu_z[:, cols]).astype(BF16)

    blk = pl.BlockSpec((tb, D), lambda i: (i, 0))
    return _pcall(body, name="pool_fwd", grid=(rows // tb,),
                  out_shape=(jax.ShapeDtypeStruct((rows, D), BF16), jax.ShapeDtypeStruct((rows, D), BF16)),
                  in_specs=[blk, pl.BlockSpec((HALO, D), lambda i: (jnp.maximum(i * hb - 1, 0), 0)),
                            pl.BlockSpec((tb, D), lambda i: (i, 1)),
                            _full((4, 256, 256)), _full((1, D))],
                  out_specs=(blk, blk),
                  compiler_params=_params(("arbitrary",)))(proj, proj, proj, pool_w, pool_scale)


def _ssm_fwd(proj, pm, pmt, wb, wct, ptab, dvec, glu_w, glu_b, shards):
    rows = proj.shape[0]
    tb = pm.shape[0]
    k_steps = tb // SUBLANES
    nblk = rows // tb
    n_sh = len(shards)
    items = [_gather_item(t, t, _rows_of(shards[t].shape[0])) for t in range(n_sh)]

    def body(u_ref, z_ref, pm_ref, pmt_ref, wb_ref, wct_ref, p_ref, d_ref, gw_ref, gb_ref, *rest):
        src_refs = rest[:n_sh]
        y_ref, ys_ref, carry_out_ref, s_ref, gate_ref, zp_ref, up_ref = rest[n_sh:n_sh + 7]
        out_refs = rest[n_sh + 7:2 * n_sh + 7]
        carry_ref, enter_ref, fin_ref, *sems = rest[2 * n_sh + 7:]

        @pl.when(pl.program_id(0) == 0)
        def _():
            _hosted_copies(items, src_refs, out_refs, *sems, act="start")
            carry_ref[...] = jnp.zeros_like(carry_ref)

        carry_out_ref[...] = carry_ref[...]
        up = _dot(pm_ref[...], u_ref[...]).astype(BF16)
        up_ref[...] = up

        for q in range(N_Q):
            s_ref[:, q * Q_W:(q + 1) * Q_W] = _dot(up[:, q * 256:(q + 1) * 256], wb_ref[q])
        for q in range(N_Q):
            _scan_forward(q, s_ref, p_ref, carry_ref, enter_ref, fin_ref, k_steps)
        for q in range(N_Q):
            cols = slice(q * 256, (q + 1) * 256)
            y = _dot_nt(s_ref[:, q * Q_W:(q + 1) * Q_W].astype(BF16), wct_ref[q])
            ys_ref[:, cols] = y + d_ref[:, cols] * up[:, cols].astype(F32)
        yg, _ = _gelu_parts(ys_ref[...])
        gate = jax.nn.sigmoid(_dot(yg.astype(BF16), gw_ref[...]) + gb_ref[...])
        gate_ref[...] = gate
        zp = _dot(pm_ref[...], z_ref[...])
        zp_ref[...] = zp.astype(BF16)
        silu_z, _ = _silu_parts(zp)
        y_ref[...] = _dot(pmt_ref[...], (yg * gate * silu_z).astype(BF16)).astype(BF16)

        @pl.when(pl.program_id(0) == nblk - 1)
        def _():
            _hosted_copies(items, src_refs, out_refs, *sems, act="wait")

    return _pcall(body, name="ssm_fwd", grid=(nblk,),
                  out_shape=(jax.ShapeDtypeStruct((rows, D), BF16), jax.ShapeDtypeStruct((rows, D), F32),
                             jax.ShapeDtypeStruct((nblk, 1, N_STATE), F32),
                             jax.ShapeDtypeStruct((rows, N_STATE), F32),
                             jax.ShapeDtypeStruct((rows, D), F32), jax.ShapeDtypeStruct((rows, D), BF16),
                             jax.ShapeDtypeStruct((rows, D), BF16),
                             *[jax.ShapeDtypeStruct((D, D), BF16)] * n_sh),
                  in_specs=[pl.BlockSpec((tb, D), lambda i: (i, 2)), pl.BlockSpec((tb, D), lambda i: (i, 3)),
                            _full((tb, tb)), _full((tb, tb)),
                            _full((N_Q, 256, Q_W), single=True), _full((N_Q, 256, Q_W), single=True),
                            _full((k_steps, N_STATE)), _full((1, D)), _full((D, D), single=True), _full((1, D))] +
                           [ANY] * n_sh,
                  out_specs=(pl.BlockSpec((tb, D), lambda i: (i, 0)), pl.BlockSpec((tb, D), lambda i: (i, 0)),
                             pl.BlockSpec((None, 1, N_STATE), lambda i: (i, 0, 0)),
                             pl.BlockSpec((tb, N_STATE), lambda i: (i, 0)),
                             *[pl.BlockSpec((tb, D), lambda i: (i, 0))] * 3, *([ANY] * n_sh)),
                  scratch_shapes=[pltpu.VMEM((1, N_STATE), F32),
                                  pltpu.VMEM((SUBLANES, N_STATE), F32), pltpu.VMEM((SUBLANES, N_STATE), F32)] +
                                 _sem_scratch(items),
                  compiler_params=_params(("arbitrary",)))(proj, proj, pm, pmt, wb, wct, ptab, dvec, glu_w, glu_b,
                                                           *shards)


def _head(x, target, proj, y_pool, y_ssm, mod3, norm_post, wbp, wbs, wout):
    rows = x.shape[0]
    tb = _tb(rows, 256)
    nblk = rows // tb
    n_feat = float(D)

    def body(x_ref, t_ref, gp_ref, gs_ref, yp_ref, ys_ref, mod_ref, npost_ref, wbp_ref, wbs_ref, wout_ref,
             loss_ref, dy_ref, dyp_ref, dys_ref, dg_ref, dwbp_hbm, dwbs_hbm, dwout_hbm, vec_ref,
             acc_bp, acc_bs, acc_out, acc_loss, acc_vec):
        i = pl.program_id(0)

        @pl.when(i == 0)
        def _():
            acc_bp[...] = jnp.zeros_like(acc_bp)
            acc_bs[...] = jnp.zeros_like(acc_bs)
            acc_out[...] = jnp.zeros_like(acc_out)
            acc_loss[...] = jnp.zeros_like(acc_loss)
            acc_vec[...] = jnp.zeros_like(acc_vec)

        gate = mod_ref[2:3, :]
        npost = npost_ref[...]
        yp, ys = yp_ref[...], ys_ref[...]
        sgp = jax.nn.sigmoid(gp_ref[...].astype(F32))
        sgs = jax.nn.sigmoid(gs_ref[...].astype(F32))
        pb = _dot(yp, wbp_ref[...])
        psm = _dot(ys, wbs_ref[...])
        mb = (sgp * pb + sgs * psm).astype(BF16)
        out = _dot(mb, wout_ref[...])
        on, r = _rms_parts(out)
        normed = on * npost
        diff = x_ref[...] + gate * normed - t_ref[...]
        acc_loss[...] += jnp.sum(diff * diff, axis=0, keepdims=True)
        dy = diff * (1.0 / n_feat)
        dy_ref[...] = dy
        acc_vec[0:1, :] += jnp.sum(dy * normed, axis=0, keepdims=True)
        dn = dy * gate
        acc_vec[1:2, :] += jnp.sum(dn * on, axis=0, keepdims=True)
        dout = _rms_bwd(dn * npost, on, r).astype(BF16)
        dm = _dot_nt(dout, wout_ref[...])
        dpb = (dm * sgp).astype(BF16)
        dps = (dm * sgs).astype(BF16)
        dg_ref[:, :D] = (dm * pb * sgp * (1.0 - sgp)).astype(BF16)
        dg_ref[:, D:] = (dm * psm * sgs * (1.0 - sgs)).astype(BF16)
        dyp_ref[...] = _dot_nt(dpb, wbp_ref[...]).astype(BF16)
        dys_ref[...] = _dot_nt(dps, wbs_ref[...]).astype(BF16)
        acc_out[...] += _dot_tn(mb, dout)
        acc_bp[...] += _dot_tn(yp, dpb)
        acc_bs[...] += _dot_tn(ys, dps)

        @pl.when(i == nblk - 1)
        def _():
            loss_ref[...] = 0.5 / n_feat * jnp.sum(acc_loss[...], axis=1, keepdims=True)
            vec_ref[...] = acc_vec[...]
            pltpu.sync_copy(acc_bp, dwbp_hbm)
            pltpu.sync_copy(acc_bs, dwbs_hbm)
            pltpu.sync_copy(acc_out, dwout_hbm)

    row = lambda c: pl.BlockSpec((tb, D), lambda i: (i, c))
    w = _full((D, D), single=True)
    return _pcall(body, name="head", grid=(nblk,),
                  out_shape=(jax.ShapeDtypeStruct((1, 1), F32), jax.ShapeDtypeStruct((rows, D), F32),
                             jax.ShapeDtypeStruct((rows, D), BF16), jax.ShapeDtypeStruct((rows, D), BF16),
                             jax.ShapeDtypeStruct((rows, 2 * D), BF16),
                             jax.ShapeDtypeStruct((D, D), F32), jax.ShapeDtypeStruct((D, D), F32),
                             jax.ShapeDtypeStruct((D, D), F32), jax.ShapeDtypeStruct((2, D), F32)),
                  in_specs=[row(0), row(0), row(4), row(5), row(0), row(0), _full((3, D)), _full((1, D)), w, w, w],
                  out_specs=(_full((1, 1)), row(0), row(0), row(0), pl.BlockSpec((tb, 2 * D), lambda i: (i, 0)),
                             ANY, ANY, ANY, _full((2, D))),
                  scratch_shapes=[pltpu.VMEM((D, D), F32), pltpu.VMEM((D, D), F32), pltpu.VMEM((D, D), F32),
                                  pltpu.VMEM((1, D), F32), pltpu.VMEM((2, D), F32)],
                  compiler_params=_params(("arbitrary",)))(x, target, proj, proj, y_pool, y_ssm, mod3, norm_post,
                                                           wbp, wbs, wout)


def _glu_bwd(dys, zp, ys_pre, gate, pm, pmt, glu_w):
    rows = dys.shape[0]
    tb = pm.shape[0]
    nblk = rows // tb

    def body(dys_ref, z_ref, ysp_ref, sg_ref, pm_ref, pmt_ref, gw_ref, dyp_ref, dz_ref, dgw_hbm, dgb_ref,
             acc_w, acc_b):
        i = pl.program_id(0)

        @pl.when(i == 0)
        def _():
            acc_w[...] = jnp.zeros_like(acc_w)
            acc_b[...] = jnp.zeros_like(acc_b)

        d_out = _dot(pm_ref[...], dys_ref[...])
        yg, dgelu = _gelu_parts(ysp_ref[...])
        ygb = yg.astype(BF16)
        sg = sg_ref[...]
        silu_z, dsilu_z = _silu_parts(z_ref[...].astype(F32))
        dz = d_out * (yg * sg) * dsilu_z
        dz_ref[...] = _dot(pmt_ref[...], dz.astype(BF16)).astype(BF16)
        dglu = d_out * silu_z
        dq = dglu * yg * sg * (1.0 - sg)
        dqb = dq.astype(BF16)
        acc_b[...] += jnp.sum(dq, axis=0, keepdims=True)
        acc_w[...] += _dot_tn(ygb, dqb)
        dyg = dglu * sg + _dot_nt(dqb, gw_ref[...])
        dyp_ref[...] = (dyg * dgelu).astype(BF16)

        @pl.when(i == nblk - 1)
        def _():
            dgb_ref[...] = acc_b[...]
            pltpu.sync_copy(acc_w, dgw_hbm)

    row = lambda c: pl.BlockSpec((tb, D), lambda i: (i, c))
    return _pcall(body, name="glu_bwd", grid=(nblk,),
                  out_shape=(jax.ShapeDtypeStruct((rows, D), BF16), jax.ShapeDtypeStruct((rows, D), BF16),
                             jax.ShapeDtypeStruct((D, D), F32), jax.ShapeDtypeStruct((1, D), F32)),
                  in_specs=[row(0), row(0), row(0), row(0), _full((tb, tb)), _full((tb, tb)),
                            _full((D, D), single=True)],
                  out_specs=(row(0), row(0), ANY, _full((1, D))),
                  scratch_shapes=[pltpu.VMEM((D, D), F32), pltpu.VMEM((1, D), F32)],
                  compiler_params=_params(("arbitrary",)))(dys, zp, ys_pre, gate, pm, pmt, glu_w)


def _ssm_bwd(dyp, up, states, carries, pmt, wb, wct, ptab, dvec, mat_grads, dpool_w, dw_in_rest):
    rows = dyp.shape[0]
    tb = pmt.shape[0]
    k_steps = tb // SUBLANES
    nblk = rows // tb
    n_mat = len(mat_grads)
    hosted = [*mat_grads, dpool_w, dw_in_rest]
    n_h = len(hosted)
    shard_rows = D // N_DEV
    pool_rows = dpool_w.shape[1] // N_DEV
    items = [_scatter_item(t, t, _rows_of(shard_rows)) for t in range(n_mat)] + \
            [_scatter_item(n_mat, n_mat, _pool_rows_of(pool_rows))] + \
            [_w_in_block_item(n_mat + 1, n_mat + 1, j, ssm_part=False) for j in range(W_IN_SHARD // W_IN_BLOCK)]
    n_in, n_out = 9, 5

    def body(*refs):
        dyp_ref, u_ref, s_ref, cin_ref, pmt_ref, wb_ref, wct_ref, p_ref, d_ref = refs[:n_in]
        src_refs = refs[n_in:n_in + n_h]
        du_ref, dbb_ref, dcc_ref, da_ref, dd_ref = refs[n_in + n_h:n_in + n_h + n_out]
        recv_refs = refs[n_in + n_h + n_out:n_in + 2 * n_h + n_out]
        (g_ref, carry_b, fin_ref, acc_wb, acc_wct, acc_da, acc_dd, dup_ref,
         *sems) = refs[n_in + 2 * n_h + n_out:]
        i = pl.program_id(0)

        @pl.when(i == 0)
        def _():
            _hosted_copies(items, src_refs, recv_refs, *sems, act="start")
            carry_b[...] = jnp.zeros_like(carry_b)
            acc_wb[...] = jnp.zeros_like(acc_wb)
            acc_wct[...] = jnp.zeros_like(acc_wct)
            acc_da[...] = jnp.zeros_like(acc_da)
            acc_dd[...] = jnp.zeros_like(acc_dd)

        def keep_own(acc, q, prod):
            for gl in range(16):
                r, c = slice(gl * G_H, (gl + 1) * G_H), (gl // 2) * 128
                acc[q, r, 0:128] += prod[r, c:c + 128]
                acc[q, r, 128:256] += prod[r, Q_W // 2 + c:Q_W // 2 + c + 128]

        dy = dyp_ref[...]
        up = u_ref[...]
        acc_dd[...] += jnp.sum(dy.astype(F32) * up.astype(F32), axis=0, keepdims=True)
        for q in range(N_Q):
            cols = slice(q * 256, (q + 1) * 256)
            g_ref[:, q * Q_W:(q + 1) * Q_W] = _dot(dy[:, cols], wct_ref[q])
            keep_own(acc_wct, q, _dot_tn(dy[:, cols], s_ref[:, q * Q_W:(q + 1) * Q_W].astype(BF16)))
        for q in range(N_Q):
            _scan_backward(q, g_ref, s_ref, p_ref, carry_b, cin_ref, fin_ref, acc_da, k_steps)
        for q in range(N_Q):
            cols = slice(q * 256, (q + 1) * 256)
            lam = g_ref[:, q * Q_W:(q + 1) * Q_W].astype(BF16)
            keep_own(acc_wb, q, _dot_tn(up[:, cols], lam))
            dup_ref[:, cols] = (_dot_nt(lam, wb_ref[q]) + d_ref[:, cols] * dy[:, cols].astype(F32)).astype(BF16)
        du_ref[...] = _dot(pmt_ref[...], dup_ref[...]).astype(BF16)

        @pl.when(i == nblk - 1)
        def _():
            da_ref[...] = acc_da[...]
            dd_ref[...] = acc_dd[...]
            lane = lax.broadcasted_iota(jnp.int32, (16 * G_H, 128), 1)
            row = lax.broadcasted_iota(jnp.int32, (16 * G_H, 128), 0)
            own = lane // G_P == (row // G_H) % 2
            spread = (lax.broadcasted_iota(jnp.int32, (G_P, 128), 1) % G_P ==
                      lax.broadcasted_iota(jnp.int32, (G_P, 128), 0)).astype(F32)
            for acc, out in ((acc_wb, dbb_ref), (acc_wct, dcc_ref)):
                for half in range(2):
                    for q in range(N_Q):
                        kept = jnp.where(own, acc[q, :, half * 128:(half + 1) * 128], 0.0)
                        out[half, q] = lax.dot_general(kept, spread, (((1,), (1,)), ((), ())),
                                                       preferred_element_type=F32, precision=lax.Precision.HIGHEST)
            _hosted_copies(items, src_refs, recv_refs, *sems, act="wait")

    rev = lambda c: pl.BlockSpec((tb, D), lambda i: (nblk - 1 - i, c))
    recv = [jax.ShapeDtypeStruct((N_DEV, shard_rows, D), F32)] * n_mat + \
           [jax.ShapeDtypeStruct((N_DEV, dpool_w.shape[0], pool_rows, dpool_w.shape[2]), F32),
            jax.ShapeDtypeStruct((N_DEV, D, W_IN_SHARD), BF16)]
    return _pcall(body, name="ssm_bwd", grid=(nblk,),
                  out_shape=(jax.ShapeDtypeStruct((rows, D), BF16),
                             jax.ShapeDtypeStruct((2, N_Q, 16 * G_H, G_P), F32),
                             jax.ShapeDtypeStruct((2, N_Q, 16 * G_H, G_P), F32),
                             jax.ShapeDtypeStruct((1, N_STATE), F32), jax.ShapeDtypeStruct((1, D), F32), *recv),
                  in_specs=[rev(0), rev(0), pl.BlockSpec((tb, N_STATE), lambda i: (nblk - 1 - i, 0)),
                            pl.BlockSpec((None, 1, N_STATE), lambda i: (nblk - 1 - i, 0, 0)),
                            _full((tb, tb)),
                            _full((N_Q, 256, Q_W), single=True), _full((N_Q, 256, Q_W), single=True),
                            _full((k_steps, N_STATE)), _full((1, D))] + [ANY] * n_h,
                  out_specs=(rev(0), _full((2, N_Q, 16 * G_H, G_P)), _full((2, N_Q, 16 * G_H, G_P)),
                             _full((1, N_STATE)), _full((1, D)), *([ANY] * n_h)),
                  scratch_shapes=[pltpu.VMEM((tb, N_STATE), F32), pltpu.VMEM((1, N_STATE), F32),
                                  pltpu.VMEM((SUBLANES, N_STATE), F32),
                                  pltpu.VMEM((N_Q, 16 * G_H, 256), F32), pltpu.VMEM((N_Q, 16 * G_H, 256), F32),
                                  pltpu.VMEM((1, N_STATE), F32), pltpu.VMEM((1, D), F32),
                                  pltpu.VMEM((tb, D), BF16)] + _sem_scratch(items),
                  compiler_params=_params(("arbitrary",), vmem=60 * 1024 * 1024),
                  )(dyp, up, states, carries, pmt, wb, wct, ptab, dvec, *hosted)


def _pool_bwd(dyp, pooled, proj, pool_w, pool_scale):
    rows = dyp.shape[0]
    tb = _tb(rows, 1024)
    nblk = rows // tb

    def body(dy_ref, pooled_ref, z_ref, pw_ref, ps_ref, dp_ref, dpw_ref, dps_ref, ahead_ref):
        i = pl.program_id(0)
        blk = nblk - 1 - i

        @pl.when(i == 0)
        def _():
            ahead_ref[...] = jnp.zeros_like(ahead_ref)
            dpw_ref[...] = jnp.zeros_like(dpw_ref)
            dps_ref[...] = jnp.zeros_like(dps_ref)

        inv_counts = _inv_counts(tb, blk * tb)
        silu_z, dsilu_z = _silu_parts(z_ref[...].astype(F32))
        dy = dy_ref[...].astype(F32)
        for g, w in enumerate(POOL_WINDOWS):
            cols = slice(g * 256, (g + 1) * 256)
            pooled_b = pooled_ref[:, cols]
            mixed = _dot(pooled_b, pw_ref[g])
            scale = ps_ref[:, cols]
            dp_ref[:, D + g * 256:D + (g + 1) * 256] = (dy[:, cols] * (mixed * scale) * dsilu_z[:, cols]).astype(BF16)
            dms = dy[:, cols] * silu_z[:, cols]
            dps_ref[:, cols] += jnp.sum(dms * mixed, axis=0, keepdims=True)
            dmixed = (dms * scale).astype(BF16)
            dpw_ref[g] += _dot_tn(pooled_b, dmixed)
            dpooled = _dot_nt(dmixed, pw_ref[g])
            ratio = dpooled * inv_counts[g]
            acc = jnp.concatenate([ratio, ahead_ref[:, cols]], axis=0)
            ahead_ref[:, cols] = ratio[:HALO, :]
            s = 1
            while s < w:
                acc = acc + pltpu.roll(acc, tb + HALO - s, axis=0)
                s *= 2
            dp_ref[:, cols] = (acc[:tb, :] - dpooled).astype(BF16)

    rev = lambda c: pl.BlockSpec((tb, D), lambda i: (nblk - 1 - i, c))
    return _pcall(body, name="pool_bwd", grid=(nblk,),
                  out_shape=(jax.ShapeDtypeStruct((rows, 2 * D), BF16), jax.ShapeDtypeStruct((4, 256, 256), F32),
                             jax.ShapeDtypeStruct((1, D), F32)),
                  in_specs=[rev(0), rev(0), rev(1), _full((4, 256, 256)), _full((1, D))],
                  out_specs=(pl.BlockSpec((tb, 2 * D), lambda i: (nblk - 1 - i, 0)), _full((4, 256, 256)),
                             _full((1, D))),
                  scratch_shapes=[pltpu.VMEM((HALO, D), F32)],
                  compiler_params=_params(("arbitrary",)))(dyp, pooled, proj, pool_w, pool_scale)


def _dproj_specs(tb):
    return [pl.BlockSpec((tb, 2 * D), lambda i: (i, 0)), pl.BlockSpec((tb, D), lambda i: (i, 0)),
            pl.BlockSpec((tb, D), lambda i: (i, 0)), pl.BlockSpec((tb, 2 * D), lambda i: (i, 0))]


def _in_proj_bwd_x(x, dy, dpp, dus, dzs, dpg, mod3, norm_pre, w_in, dw_in_ssm, recv_w_in):
    rows = x.shape[0]
    tb = _tb(rows, 512)
    nblk = rows // tb
    items = [_w_in_block_item(0, 0, j, ssm_part=True) for j in range(W_IN_SHARD // W_IN_BLOCK)]
    sums_item = [_Item(0, 0, _whole, _slot)]

    def body(x_ref, dy_ref, dpp_ref, dus_ref, dzs_ref, dpg_ref, mod_ref, np_ref, w_ref,
             dw_src, _, gx_ref, recv_w, recv_sums, vec_ref, ssem, rsem, lsem, *sums_sems):
        src_refs, recv_refs, sems = (dw_src,), (recv_w,), (ssem, rsem, lsem)

        @pl.when(pl.program_id(0) == 0)
        def _():
            _hosted_copies(items, src_refs, recv_refs, *sems, act="start")
            vec_ref[...] = jnp.zeros_like(vec_ref)

        dh = _dot_nt(dpp_ref[...], w_ref[:, 0:2 * D])
        dh += _dot_nt(dus_ref[...], w_ref[:, 2 * D:3 * D])
        dh += _dot_nt(dzs_ref[...], w_ref[:, 3 * D:4 * D])
        dh += _dot_nt(dpg_ref[...], w_ref[:, 4 * D:6 * D])
        xn, r, _ = _prenorm(x_ref[...], mod_ref[...], np_ref[...])
        one_scale = 1.0 + mod_ref[1:2, :]
        vec_ref[0:1, :] += jnp.sum(dh, axis=0, keepdims=True)
        vec_ref[1:2, :] += jnp.sum(dh * xn, axis=0, keepdims=True) * np_ref[...]
        vec_ref[2:3, :] += jnp.sum(dh * xn, axis=0, keepdims=True) * one_scale
        gx_ref[...] = dy_ref[...] + _rms_bwd(dh * (np_ref[...] * one_scale), xn, r)

        @pl.when(pl.program_id(0) == nblk - 1)
        def _():
            _hosted_copies(sums_item, (vec_ref,), (recv_sums,), *sums_sems, act="start")
            _hosted_copies(items, src_refs, recv_refs, *sems, act="wait")
            _hosted_copies(sums_item, (vec_ref,), (recv_sums,), *sums_sems, act="wait")

    row = pl.BlockSpec((tb, D), lambda i: (i, 0))
    recv = (jax.ShapeDtypeStruct(recv_w_in.shape, recv_w_in.dtype), jax.ShapeDtypeStruct((N_DEV, 3, D), F32))
    return _pcall(body, name="in_proj_bwd_x", grid=(nblk,),
                  out_shape=(jax.ShapeDtypeStruct((rows, D), F32), *recv),
                  in_specs=[row, row] + _dproj_specs(tb) + [_full((3, D)), _full((1, D)),
                                                            _full((D, N_IN), single=True)] + [ANY] * 2,
                  out_specs=(row, ANY, ANY),
                  input_output_aliases={10: 1},
                  scratch_shapes=[pltpu.VMEM((3, D), F32)] + _sem_scratch(items) + _sem_scratch(sums_item),
                  compiler_params=_params(("arbitrary",)))(x, dy, dpp, dus, dzs, dpg, mod3, norm_pre, w_in,
                                                           dw_in_ssm, recv_w_in)


def _in_proj_bwd_w(name, x, dparts, mod3, norm_pre, gathered=()):
    rows = x.shape[0]
    tb = _tb(rows, 512)
    nblk = rows // tb
    widths = [p.shape[1] for p in dparts]
    n_p, n_g = len(dparts), len(gathered)
    items = [_Item(t, t, _whole, _slot) for t in range(n_g)]

    def body(x_ref, *rest):
        part_refs, (mod_ref, np_ref) = rest[:n_p], rest[n_p:n_p + 2]
        src_refs, dw_ref = rest[n_p + 2:n_p + 2 + n_g], rest[n_p + 2 + n_g]
        recv_refs, (acc, *sems) = rest[n_p + 3 + n_g:n_p + 3 + 2 * n_g], rest[n_p + 3 + 2 * n_g:]
        i = pl.program_id(0)

        @pl.when(i == 0)
        def _():
            if n_g:
                _hosted_copies(items, src_refs, recv_refs, *sems, act="start")
            acc[...] = jnp.zeros_like(acc)

        _, _, h = _prenorm(x_ref[...], mod_ref[...], np_ref[...])
        ht = h.astype(BF16)
        lo = 0
        for ref, w in zip(part_refs, widths):
            acc[:, lo:lo + w] += _dot_tn(ht, ref[...])
            lo += w

        @pl.when(i == nblk - 1)
        def _():
            dw_ref[...] = acc[...].astype(BF16)
            if n_g:
                _hosted_copies(items, src_refs, recv_refs, *sems, act="wait")

    row = pl.BlockSpec((tb, D), lambda i: (i, 0))
    out = _pcall(body, name=name, grid=(nblk,),
                 out_shape=(jax.ShapeDtypeStruct((D, sum(widths)), BF16),
                            *[jax.ShapeDtypeStruct((N_DEV,) + g.shape, g.dtype) for g in gathered]),
                 in_specs=[row] + [pl.BlockSpec((tb, w), lambda i: (i, 0)) for w in widths] +
                          [_full((3, D)), _full((1, D))] + [ANY] * n_g,
                 out_specs=(_full((D, sum(widths))), *([ANY] * n_g)),
                 scratch_shapes=[pltpu.VMEM((D, sum(widths)), F32)] + (_sem_scratch(items) if n_g else []),
                 compiler_params=_params(("arbitrary",)))(x, *dparts, mod3, norm_pre, *gathered)
    return out if n_g else out[0]


def _adamw_math(w, g, m, v):
    m = ADAM_B1 * m + (1.0 - ADAM_B1) * g
    v = ADAM_B2 * v + (1.0 - ADAM_B2) * (g * g)
    m_hat = m / (1.0 - ADAM_B1 ** ADAM_STEP)
    v_hat = v / (1.0 - ADAM_B2 ** ADAM_STEP)
    delta = -ADAM_LR * (m_hat / (jnp.sqrt(v_hat) + ADAM_EPS) + ADAM_WD * w)
    return delta, m, v


def _sum_sources(ref):
    g = ref[0].astype(F32)
    for s in range(1, N_DEV):
        g = g + ref[s].astype(F32)
    return g


def _adamw_reduce(name, parts, w, m, v):
    r, c = w.shape
    tr = r if r * c <= 256 * 1024 else max(8, (256 * 1024 // c) // 8 * 8)
    while r % tr:
        tr -= 8

    def body(p_ref, w_ref, m_ref, v_ref, g_ref, d_ref, nm_ref, nv_ref):
        g = _sum_sources(p_ref)
        g_ref[...] = g
        d_ref[...], nm_ref[...], nv_ref[...] = _adamw_math(w_ref[...], g, m_ref[...], v_ref[...])

    blk = pl.BlockSpec((tr, c), lambda i: (i, 0))
    return _pcall(body, name=name, grid=(r // tr,),
                  out_shape=tuple([jax.ShapeDtypeStruct((r, c), F32)] * 4),
                  in_specs=[pl.BlockSpec((N_DEV, tr, c), lambda i: (0, i, 0)), blk, blk, blk],
                  out_specs=(blk, blk, blk, blk),
                  compiler_params=_params(("arbitrary",)))(parts, w, m, v)


def _adamw_small(gs, ws, ms, vs):
    n = len(gs)

    def body(*refs):
        ins, outs = refs[:4 * n], refs[4 * n:]
        for t in range(n):
            g_ref, w_ref, m_ref, v_ref = ins[4 * t:4 * t + 4]
            outs[3 * t][...], outs[3 * t + 1][...], outs[3 * t + 2][...] = _adamw_math(
                w_ref[...], g_ref[...], m_ref[...], v_ref[...])

    vm = pl.BlockSpec(memory_space=pltpu.VMEM)
    flat = [a for t in range(n) for a in (gs[t], ws[t], ms[t], vs[t])]
    return _pcall(body, name="adamw_small",
                  out_shape=tuple(jax.ShapeDtypeStruct(w.shape, F32) for w in ws for _ in range(3)),
                  in_specs=[vm] * (4 * n), out_specs=tuple([vm] * (3 * n)), compiler_params=_params())(*flat)


def _sum_small(parts):
    n = len(parts)

    def body(*refs):
        for t in range(n):
            refs[n + t][...] = _sum_sources(refs[t])

    vm = pl.BlockSpec(memory_space=pltpu.VMEM)
    return _pcall(body, name="sum_small",
                  out_shape=tuple(jax.ShapeDtypeStruct(p.shape[1:], F32) for p in parts),
                  in_specs=[vm] * n, out_specs=tuple([vm] * n), compiler_params=_params())(*parts)


def _ada_update(c_all, dmod_cols, w, m, v):
    def body(c_ref, dm_ref, w_ref, m_ref, v_ref, g_ref, d_ref, nm_ref, nv_ref):
        ca = c_ref[...]
        g = lax.dot_general(ca * jax.nn.sigmoid(ca), dm_ref[...], (((0,), (0,)), ((), ())),
                            preferred_element_type=F32, precision=lax.Precision.HIGHEST)
        g_ref[...] = g
        d_ref[...], nm_ref[...], nv_ref[...] = _adamw_math(w_ref[...], g, m_ref[...], v_ref[...])

    vm = pl.BlockSpec(memory_space=pltpu.VMEM)
    return _pcall(body, name="ada_update", out_shape=tuple([jax.ShapeDtypeStruct(w.shape, F32)] * 4),
                  in_specs=[vm] * 5, out_specs=(vm, vm, vm, vm), compiler_params=_params())(c_all, dmod_cols, w, m, v)


def kernel(x, c, w_ada, b_ada, norm_pre, norm_post, w_in, pool_w, pool_scale, ssm_a_re, ssm_a_im, ssm_log_dt, ssm_b_re, ssm_b_im, ssm_c_re, ssm_c_im, ssm_d, glu_w, glu_b, w_branch_pool, w_branch_ssm, w_out, loss_target, m_w_ada, m_b_ada, m_norm_pre, m_norm_post, m_w_in, m_pool_w, m_pool_scale, m_ssm_a_re, m_ssm_a_im, m_ssm_log_dt, m_ssm_b_re, m_ssm_b_im, m_ssm_c_re, m_ssm_c_im, m_ssm_d, m_glu_w, m_glu_b, m_w_branch_pool, m_w_branch_ssm, m_w_out, v_w_ada, v_b_ada, v_norm_pre, v_norm_post, v_w_in, v_pool_w, v_pool_scale, v_ssm_a_re, v_ssm_a_im, v_ssm_log_dt, v_ssm_b_re, v_ssm_b_im, v_ssm_c_re, v_ssm_c_im, v_ssm_d, v_glu_w, v_glu_b, v_w_branch_pool, v_w_branch_ssm, v_w_out):
    given = dict(locals())
    me = _flat(_me())
    rows = x.shape[1]
    x2 = x[0]
    target = loss_target[0]
    ada_cols = w_ada.shape[2]

    tb_ssm = _tb(rows, 256)
    k_steps = tb_ssm // SUBLANES
    a_re, a_im = ssm_a_re[0], ssm_a_im[0]
    log_dt = ssm_log_dt[0].reshape(GROUPS, 1)
    b_re_t, b_im_t = ssm_b_re[0].transpose(0, 2, 1), ssm_b_im[0].transpose(0, 2, 1)
    s5_params = [a_re, a_im, log_dt, b_re_t, b_im_t, ssm_c_re[0], ssm_c_im[0]]

    f32_shards = [w_in[0], pool_w[0], glu_w[0], w_branch_pool[0], w_branch_ssm[0], w_out[0]]
    n_sh = len(f32_shards)

    def local_work(ins, outs):
        for src, dst in zip(ins[:n_sh], outs[:n_sh]):
            dst[...] = src[...].astype(BF16)
        _s5_prep_body(*ins[n_sh:], *outs[n_sh:])

    b_ada_s = lax.dynamic_slice(b_ada, (0, me * ada_cols), (1, ada_cols))
    c_all, mod_rows, *local = _ada_exchange(
        c, w_ada[0], b_ada_s, f32_shards + s5_params,
        [jax.ShapeDtypeStruct(a.shape, BF16) for a in f32_shards] + list(_s5_prep_structs(k_steps)), local_work)
    mod3 = mod_rows.reshape(3, D)
    shards, (wb, wct, pow_re, pow_im) = local[:n_sh], local[n_sh:]
    ptab = _state_layout(pow_re, pow_im)
    dvec = ssm_d[0].reshape(1, D)
    pm = _perm_matrix(tb_ssm)
    pmt = pm.T

    proj, w_in_g, pool_w_g, glu_g = _in_proj(x2, mod3, norm_pre, shards[0], shards[1:3])
    y_pool, pooled = _pool_fwd(proj, pool_w_g, pool_scale)
    y_ssm, ys_pre, carries, states, glu_gate, z_perm, u_perm, wbp_g, wbs_g, wout_g = _ssm_fwd(
        proj, pm, pmt, wb, wct, ptab, dvec, glu_g, glu_b, shards[3:])
    loss_part, dy, dyp, dys, dpg, dwbp, dwbs, dwout, head_vec = _head(
        x2, target, proj, y_pool, y_ssm, mod3, norm_post, wbp_g, wbs_g, wout_g)

    dpp, dpool_w, dpool_scale = _pool_bwd(dyp, pooled, proj, pool_w_g, pool_scale)
    dw_in_rest = _in_proj_bwd_w("in_proj_bwd_w_rest", x2, [dpp, dpg], mod3, norm_pre)
    dy_pre, dzs, dglu_w, dglu_b = _glu_bwd(dys, z_perm, ys_pre, glu_gate, pm, pmt, glu_g)
    dus, dbb, dcc, dabar, dd, p_glu, p_wbp, p_wbs, p_wout, p_pool_w, p_w_in = _ssm_bwd(
        dy_pre, u_perm, states, carries, pmt, wb, wct, ptab, dvec, [dglu_w, dwbp, dwbs, dwout], dpool_w,
        dw_in_rest)

    small32 = jnp.concatenate([head_vec, dpool_scale, dglu_b, dd, jnp.broadcast_to(loss_part, (1, D)),
                               jnp.zeros((2, D), F32), dabar.reshape(8, D)], axis=0)
    small16 = jnp.concatenate([dbb.reshape(2 * GROUPS, D), dcc.reshape(2 * GROUPS, D)], axis=0).astype(BF16)
    dw_in_ssm, p_small32, p_small16 = _in_proj_bwd_w("in_proj_bwd_w_ssm", x2, [dus, dzs], mod3, norm_pre,
                                                     gathered=(small32, small16))
    grad_x, p_w_in, p_pre = _in_proj_bwd_x(x2, dy, dpp, dus, dzs, dpg, mod3, norm_pre, w_in_g, dw_in_ssm, p_w_in)

    tot32, tot16, tot_pre = _sum_small([p_small32, p_small16, p_pre])
    d_abar_re, d_abar_im = _state_unlayout(tot32[8:16].reshape(N_STATE))
    d_bb_re, d_bb_im = tot16[0:64].reshape(GROUPS, G_H, G_P), tot16[64:128].reshape(GROUPS, G_H, G_P)
    g_a_re, g_a_im, g_log_dt, g_b_re_t, g_b_im_t = _s5_prep_bwd(
        a_re, a_im, log_dt, b_re_t, b_im_t, d_abar_re, d_abar_im, d_bb_re, d_bb_im)

    grads, deltas, new_m, new_v = {}, {}, {}, {}

    small = []

    def small_update(name, g2):
        small.append((name, g2))

    def shard_update(name, parts):
        shape = given[name].shape
        r2 = parts.shape[1:] if parts.ndim == 3 else (parts.shape[1] * parts.shape[2], parts.shape[3])
        w2, m2, v2 = (given[p + name].reshape(r2) for p in ("", "m_", "v_"))
        out = _adamw_reduce("adamw_" + name, parts.reshape((N_DEV,) + tuple(r2)), w2, m2, v2)
        grads[name], deltas[name], new_m[name], new_v[name] = (a.reshape(shape) for a in out)

    dmod_all = jnp.concatenate([p_pre[:, 0:2, :], p_small32[:, 0:1, :]], axis=1).reshape(N_DEV, 3 * D)
    dmod_cols = lax.dynamic_slice(dmod_all, (0, me * ada_cols), (N_DEV, ada_cols))
    out = _ada_update(c_all, dmod_cols, w_ada[0], m_w_ada[0], v_w_ada[0])
    grads['w_ada'], deltas['w_ada'], new_m['w_ada'], new_v['w_ada'] = (a.reshape(w_ada.shape) for a in out)

    small_update('b_ada', jnp.concatenate([tot_pre[0:2], tot32[0:1]], axis=0).reshape(1, 3 * D))
    small_update('norm_pre', tot_pre[2:3])
    small_update('norm_post', tot32[1:2])
    small_update('pool_scale', tot32[2:3])
    small_update('glu_b', tot32[3:4])
    small_update('ssm_d', tot32[4:5])
    small_update('ssm_a_re', g_a_re)
    small_update('ssm_a_im', g_a_im)
    small_update('ssm_log_dt', g_log_dt.reshape(1, GROUPS))
    small_update('ssm_b_re', g_b_re_t.transpose(0, 2, 1).reshape(GROUPS, G_P * G_H))
    small_update('ssm_b_im', g_b_im_t.transpose(0, 2, 1).reshape(GROUPS, G_P * G_H))
    small_update('ssm_c_re', tot16[128:192])
    small_update('ssm_c_im', -tot16[192:256])
    flat = _adamw_small([g2 for _, g2 in small],
                        *[[given[p + name].reshape(g2.shape) for name, g2 in small] for p in ("", "m_", "v_")])
    for t, (name, g2) in enumerate(small):
        shape = given[name].shape
        grads[name], deltas[name], new_m[name], new_v[name] = (
            a.reshape(shape) for a in (g2, *flat[3 * t:3 * t + 3]))
    shard_update('w_in', p_w_in)
    shard_update('pool_w', p_pool_w)
    shard_update('glu_w', p_glu)
    shard_update('w_branch_pool', p_wbp)
    shard_update('w_branch_ssm', p_wbs)
    shard_update('w_out', p_wout)

    return (tot32[5, 0], grad_x[None], *[grads[n] for n in WEIGHTS], *[deltas[n] for n in WEIGHTS],
            *[new_m[n] for n in WEIGHTS], *[new_v[n] for n in WEIGHTS])
```

```python
import math
from typing import Callable, NamedTuple, Optional

import jax
import jax.numpy as jnp
from jax import lax
from jax.experimental import pallas as pl
from jax.experimental.pallas import tpu as pltpu

F32 = jnp.float32
BF16 = jnp.bfloat16
MESH = pl.DeviceIdType.MESH

D = 1024
N_DEV = 8
N_IN = 6 * D
GROUPS = 64
G_H = 16
G_P = 64
N_Q = 4
Q_W = 2 * 16 * G_P
N_STATE = N_Q * Q_W
POOL_WINDOWS = (2, 4, 8, 16)
HALO = 16
RMS_EPS = 1e-6
SUBLANES = 8
LANE_CHUNK = 512
SCAN_UNROLL = 2
VMEM_LIMIT = 56 * 1024 * 1024

ADAM_LR = 0.001
ADAM_B1 = 0.9
ADAM_B2 = 0.999
ADAM_EPS = 1e-08
ADAM_WD = 0.01
ADAM_STEP = 10

WEIGHTS = ['w_ada', 'b_ada', 'norm_pre', 'norm_post', 'w_in', 'pool_w', 'pool_scale', 'ssm_a_re',
           'ssm_a_im', 'ssm_log_dt', 'ssm_b_re', 'ssm_b_im', 'ssm_c_re', 'ssm_c_im', 'ssm_d', 'glu_w',
           'glu_b', 'w_branch_pool', 'w_branch_ssm', 'w_out']


def _pcall(body, **kw):
    return pl.pallas_call(body, **kw)


def _params(sem=None, vmem=VMEM_LIMIT):
    return pltpu.CompilerParams(dimension_semantics=sem, vmem_limit_bytes=vmem)


def _tb(rows, pref):
    return pref if rows % pref == 0 and rows // pref >= 2 else rows // 2


def _full(shape, single=False):
    nd = len(shape)
    if single:
        return pl.BlockSpec(shape, lambda i: (0,) * nd, pipeline_mode=pl.Buffered(1))
    return pl.BlockSpec(shape, lambda i: (0,) * nd)


ANY = pl.BlockSpec(memory_space=pl.ANY)


def _me():
    return lax.axis_index("x"), lax.axis_index("y"), lax.axis_index("c")


def _flat(p):
    return 4 * p[0] + 2 * p[1] + p[2]


def _peer(k):
    x, y, c = _me()
    return (1 - x if k & 4 else x, 1 - y if k & 2 else y, 1 - c if k & 1 else c)


def _silu_parts(z):
    s = jax.nn.sigmoid(z)
    return z * s, s * (1.0 + z * (1.0 - s))


_GELU_C = math.sqrt(2.0 / math.pi)


def _gelu_parts(x):
    x2 = x * x
    t = jnp.tanh(_GELU_C * (x + 0.044715 * x * x2))
    g = 0.5 * x * (1.0 + t)
    dg = 0.5 * (1.0 + t) + 0.5 * x * (1.0 - t * t) * (_GELU_C * (1.0 + 3.0 * 0.044715 * x2))
    return g, dg


def _dot(a, b):
    return jnp.dot(a, b, preferred_element_type=F32)


def _dot_nt(a, b):
    return lax.dot_general(a, b, (((1,), (1,)), ((), ())), preferred_element_type=F32)


def _dot_tn(a, b):
    return lax.dot_general(a, b, (((0,), (0,)), ((), ())), preferred_element_type=F32)


def _rms_parts(x):
    r = lax.rsqrt(jnp.mean(x * x, axis=-1, keepdims=True) + RMS_EPS)
    return x * r, r


def _rms_bwd(dxn, xn, r):
    return r * (dxn - xn * jnp.mean(dxn * xn, axis=-1, keepdims=True))


def _ada_exchange(c, w_ada_s, b_ada_s, local_ins, local_outs, local_work):
    cols = w_ada_s.shape[1]
    n_li, n_lo = len(local_ins), len(local_outs)

    def body(c_ref, w_ref, b_ref, *rest):
        li_refs, call_ref, mod_ref = rest[:n_li], rest[n_li], rest[n_li + 1]
        lo_refs, (part_ref, ssem, rsem, lsem) = rest[n_li + 2:n_li + 2 + n_lo], rest[n_li + 2 + n_lo:]
        me3 = _me()
        me = _flat(me3)
        mine = pltpu.make_async_copy(c_ref, call_ref.at[pl.ds(me, 1), :], lsem.at[0])
        mine.start()
        sends = []
        for k in range(1, N_DEV):
            cp = pltpu.make_async_remote_copy(src_ref=c_ref, dst_ref=call_ref.at[pl.ds(me, 1), :],
                                              send_sem=ssem.at[k - 1], recv_sem=rsem.at[k - 1],
                                              device_id=_peer(k), device_id_type=MESH)
            cp.start()
            sends.append(cp)
        local_work(li_refs, lo_refs)
        mine.wait()
        for k in range(1, N_DEV):
            p = _flat(_peer(k))
            pltpu.make_async_remote_copy(src_ref=c_ref, dst_ref=call_ref.at[pl.ds(p, 1), :],
                                         send_sem=ssem.at[k - 1], recv_sem=rsem.at[k - 1],
                                         device_id=_peer(k), device_id_type=MESH).wait_recv()
        for cp in sends:
            cp.wait_send()
        ca = call_ref[...]
        act = ca * jax.nn.sigmoid(ca)
        part_ref[...] = jnp.dot(act, w_ref[...], preferred_element_type=F32,
                                precision=lax.Precision.HIGHEST) + b_ref[...]
        own = pltpu.make_async_copy(part_ref.at[pl.ds(me, 1), :], mod_ref.at[pl.ds(me, 1), :], lsem.at[1])
        own.start()
        sends = []
        for k in range(1, N_DEV):
            p = _flat(_peer(k))
            s = N_DEV - 1 + k - 1
            cp = pltpu.make_async_remote_copy(src_ref=part_ref.at[pl.ds(p, 1), :],
                                              dst_ref=mod_ref.at[pl.ds(me, 1), :],
                                              send_sem=ssem.at[s], recv_sem=rsem.at[s],
                                              device_id=_peer(k), device_id_type=MESH)
            cp.start()
            sends.append(cp)
        own.wait()
        for k in range(1, N_DEV):
            p = _flat(_peer(k))
            s = N_DEV - 1 + k - 1
            pltpu.make_async_remote_copy(src_ref=part_ref.at[pl.ds(p, 1), :],
                                         dst_ref=mod_ref.at[pl.ds(p, 1), :],
                                         send_sem=ssem.at[s], recv_sem=rsem.at[s],
                                         device_id=_peer(k), device_id_type=MESH).wait_recv()
        for cp in sends:
            cp.wait_send()

    vm = pl.BlockSpec(memory_space=pltpu.VMEM)
    return _pcall(
        body, name="ada_exchange",
        out_shape=(jax.ShapeDtypeStruct((N_DEV, D), F32), jax.ShapeDtypeStruct((N_DEV, cols), F32), *local_outs),
        in_specs=[vm] * (3 + n_li), out_specs=tuple([vm] * (2 + n_lo)),
        scratch_shapes=[pltpu.VMEM((N_DEV, cols), F32),
                        pltpu.SemaphoreType.DMA((2 * (N_DEV - 1),)),
                        pltpu.SemaphoreType.DMA((2 * (N_DEV - 1),)),
                        pltpu.SemaphoreType.DMA((2,))],
        compiler_params=_params(),
    )(c, w_ada_s, b_ada_s, *local_ins)


class _Item(NamedTuple):
    src: int
    out: int
    src_view: Callable
    dst_view: Callable
    pred: Optional[Callable] = None


def _when(pred, dest, fn):
    if pred is None:
        fn()
    else:
        pl.when(pred(dest))(fn)


def _n_sems(items):
    return len(items) * (N_DEV - 1)


def _hosted_copies(items, srcs, outs, ssem, rsem, lsem, act):
    me = _flat(_me())
    for t, it in enumerate(items):
        local = lambda t=t, it=it: pltpu.make_async_copy(
            it.src_view(srcs[it.src], me), it.dst_view(outs[it.out], me), lsem.at[t])
        if act == "start":
            _when(it.pred, me, lambda local=local: local().start())
        else:
            _when(it.pred, me, lambda local=local: local().wait())
    for k in range(1, N_DEV):
        p3 = _peer(k)
        p = _flat(p3)
        for t, it in enumerate(items):
            s = t * (N_DEV - 1) + k - 1
            send = lambda it=it, s=s, p=p, p3=p3: pltpu.make_async_remote_copy(
                src_ref=it.src_view(srcs[it.src], p), dst_ref=it.dst_view(outs[it.out], me),
                send_sem=ssem.at[s], recv_sem=rsem.at[s], device_id=p3, device_id_type=MESH)
            recv = lambda it=it, s=s, p=p, p3=p3: pltpu.make_async_remote_copy(
                src_ref=it.src_view(srcs[it.src], p), dst_ref=it.dst_view(outs[it.out], p),
                send_sem=ssem.at[s], recv_sem=rsem.at[s], device_id=p3, device_id_type=MESH)
            if act == "start":
                _when(it.pred, p, lambda send=send: send().start())
            else:
                _when(it.pred, me, lambda recv=recv: recv().wait_recv())
                _when(it.pred, p, lambda send=send: send().wait_send())


def _sem_scratch(items):
    return [pltpu.SemaphoreType.DMA((_n_sems(items),)), pltpu.SemaphoreType.DMA((_n_sems(items),)),
            pltpu.SemaphoreType.DMA((len(items),))]


def _whole(ref, dest):
    return ref


def _slot(ref, sender):
    return ref.at[sender]


def _rows_of(rows):
    return lambda ref, dev: ref.at[pl.ds(dev * rows, rows), :]


def _pool_rows_of(rows):
    return lambda ref, dev: ref.at[:, pl.ds(dev * rows, rows), :]


def _gather_item(src, out, dst_view):
    return _Item(src, out, _whole, dst_view)


def _scatter_item(src, out, src_view):
    return _Item(src, out, src_view, _slot)


W_IN_BLOCK = 256
W_IN_SHARD = N_IN // N_DEV
SSM_BLOCKS = (2 * D // W_IN_BLOCK, 4 * D // W_IN_BLOCK)


def _w_in_block_item(src, out, j, ssm_part):
    def block(dest):
        return (W_IN_SHARD // W_IN_BLOCK) * dest + j

    def in_ssm(dest):
        b = block(dest)
        return (b >= SSM_BLOCKS[0]) & (b < SSM_BLOCKS[1])

    def src_view(ref, dest):
        b = block(dest)
        local = b - SSM_BLOCKS[0] if ssm_part else jnp.where(b < SSM_BLOCKS[0], b, b - (SSM_BLOCKS[1] - SSM_BLOCKS[0]))
        local = jnp.clip(local, 0, ref.shape[1] // W_IN_BLOCK - 1)
        return ref.at[:, pl.ds(local * W_IN_BLOCK, W_IN_BLOCK)]

    def dst_view(ref, sender):
        return ref.at[sender, :, pl.ds(j * W_IN_BLOCK, W_IN_BLOCK)]

    pred = in_ssm if ssm_part else (lambda dest: jnp.logical_not(in_ssm(dest)))
    return _Item(src, out, src_view, dst_view, pred)


def _s5_discretise(a_re, a_im, log_dt, b_re_t, b_im_t):
    dt = jnp.exp(log_dt)
    lam_re = jnp.minimum(a_re, -1e-4)
    lam_im = a_im
    mag = jnp.exp(lam_re * dt)
    abar_re = mag * jnp.cos(lam_im * dt)
    abar_im = mag * jnp.sin(lam_im * dt)
    den = lam_re * lam_re + lam_im * lam_im
    num_re = abar_re - 1.0
    f_re = (num_re * lam_re + abar_im * lam_im) / den
    f_im = (abar_im * lam_re - num_re * lam_im) / den
    f_re, f_im = f_re[:, None, :], f_im[:, None, :]
    bb_re = f_re * b_re_t - f_im * b_im_t
    bb_im = f_re * b_im_t + f_im * b_re_t
    return abar_re, abar_im, bb_re, bb_im


def _group_masks():
    spread = lax.broadcasted_iota(jnp.int32, (G_P, 16 * G_P), 1) % G_P == lax.broadcasted_iota(
        jnp.int32, (G_P, 16 * G_P), 0)
    own = lax.broadcasted_iota(jnp.int32, (16 * G_H, 16 * G_P), 0) // G_H == lax.broadcasted_iota(
        jnp.int32, (16 * G_H, 16 * G_P), 1) // G_P
    return spread, own


def _s5_prep_structs(n_pow):
    return (jax.ShapeDtypeStruct((N_Q, 16 * G_H, Q_W), BF16), jax.ShapeDtypeStruct((N_Q, 16 * G_H, Q_W), BF16),
            jax.ShapeDtypeStruct((n_pow, GROUPS, G_P), F32), jax.ShapeDtypeStruct((n_pow, GROUPS, G_P), F32))


def _s5_prep_body(ar_ref, ai_ref, ld_ref, br_ref, bi_ref, cr_ref, ci_ref, wb_ref, wct_ref, pr_ref, pi_ref):
    abar_re, abar_im, bb_re, bb_im = _s5_discretise(ar_ref[...], ai_ref[...], ld_ref[...], br_ref[...], bi_ref[...])
    spread, own = _group_masks()
    spread = spread.astype(BF16)
    for ref, parts in ((wb_ref, (bb_re, bb_im)), (wct_ref, (cr_ref[...], -ci_ref[...]))):
        for half, t in enumerate(parts):
            for q in range(N_Q):
                blocks = t[q * 16:(q + 1) * 16].reshape(16 * G_H, G_P).astype(BF16)
                dense = jnp.where(own, _dot(blocks, spread), 0.0)
                ref[q, :, half * (Q_W // 2):(half + 1) * (Q_W // 2)] = dense.astype(BF16)
    p_re, p_im = abar_re, abar_im
    pr_ref[0] = p_re
    pi_ref[0] = p_im
    for k in range(1, pr_ref.shape[0]):
        p_re, p_im = p_re * abar_re - p_im * abar_im, p_re * abar_im + p_im * abar_re
        pr_ref[k] = p_re
        pi_ref[k] = p_im


def _s5_prep_bwd(a_re, a_im, log_dt, b_re_t, b_im_t, d_abar_re, d_abar_im, d_bb_re, d_bb_im):
    def body(ar_ref, ai_ref, ld_ref, br_ref, bi_ref, dar_ref, dai_ref, dbr_ref, dbi_ref,
             gar_ref, gai_ref, gld_ref, gbr_ref, gbi_ref):
        _, vjp = jax.vjp(_s5_discretise, ar_ref[...], ai_ref[...], ld_ref[...], br_ref[...], bi_ref[...])
        g = vjp((dar_ref[...], dai_ref[...], dbr_ref[...], dbi_ref[...]))
        gar_ref[...] = g[0]
        gai_ref[...] = g[1]
        gld_ref[...] = g[2]
        gbr_ref[...] = g[3]
        gbi_ref[...] = g[4]

    vm = pl.BlockSpec(memory_space=pltpu.VMEM)
    ins = (a_re, a_im, log_dt, b_re_t, b_im_t)
    return _pcall(body, name="s5_prep_bwd",
                  out_shape=tuple(jax.ShapeDtypeStruct(a.shape, F32) for a in ins),
                  in_specs=[vm] * 9, out_specs=tuple([vm] * 5), compiler_params=_params(),
                  )(*ins, d_abar_re, d_abar_im, d_bb_re, d_bb_im)


def _state_layout(re, im):
    lead = re.shape[:-2]
    r = re.reshape(lead + (N_Q, 1, 16 * G_P))
    i = im.reshape(lead + (N_Q, 1, 16 * G_P))
    return jnp.concatenate([r, i], axis=-2).reshape(lead + (N_STATE,))


def _state_unlayout(v):
    v4 = v.reshape(N_Q, 2, 16, G_P)
    return v4[:, 0].reshape(GROUPS, G_P), v4[:, 1].reshape(GROUPS, G_P)


def _perm_matrix(tb):
    k_steps = tb // SUBLANES
    r = jnp.arange(tb)
    src = (r % SUBLANES) * k_steps + r // SUBLANES
    return (src[:, None] == jnp.arange(tb)[None, :]).astype(BF16)


def _lane_chunks(q):
    for lc in range(Q_W // 2 // LANE_CHUNK):
        re = q * Q_W + lc * LANE_CHUNK
        yield re, re + Q_W // 2


def _steps(lo, hi, body, init):
    if hi - lo <= SCAN_UNROLL:
        for k in range(lo, hi):
            init = body(k, init)
        return init
    trips = (hi - lo) // SCAN_UNROLL

    def trip(j, carry):
        for u in range(SCAN_UNROLL):
            carry = body(lo + j * SCAN_UNROLL + u, carry)
        return carry

    carry = lax.fori_loop(0, trips, trip, init)
    for k in range(lo + trips * SCAN_UNROLL, hi):
        carry = body(k, carry)
    return carry


def _tile(k):
    if isinstance(k, int):
        return pl.ds(k * SUBLANES, SUBLANES)
    return pl.ds(pl.multiple_of(k * SUBLANES, SUBLANES), SUBLANES)


def _scan_forward(q, s_ref, p_ref, carry_ref, enter_ref, fin_ref, k_steps):
    for re, im in _lane_chunks(q):
        lr, li = pl.ds(re, LANE_CHUNK), pl.ds(im, LANE_CHUNK)
        a_re = jnp.broadcast_to(p_ref[0:1, lr], (SUBLANES, LANE_CHUNK))
        a_im = jnp.broadcast_to(p_ref[0:1, li], (SUBLANES, LANE_CHUNK))

        def local(k, st):
            sr, si = st
            rows = _tile(k)
            nr = a_re * sr - a_im * si + s_ref[rows, lr]
            ni = a_re * si + a_im * sr + s_ref[rows, li]
            s_ref[rows, lr] = nr
            s_ref[rows, li] = ni
            return nr, ni

        zero = jnp.zeros((SUBLANES, LANE_CHUNK), F32)
        fr, fi = _steps(0, k_steps, local, (zero, zero))
        fin_ref[:, lr] = fr
        fin_ref[:, li] = fi
        ak_re, ak_im = p_ref[k_steps - 1:k_steps, lr], p_ref[k_steps - 1:k_steps, li]
        c_re, c_im = carry_ref[:, lr], carry_ref[:, li]
        for seg in range(SUBLANES):
            enter_ref[seg:seg + 1, lr] = c_re
            enter_ref[seg:seg + 1, li] = c_im
            f_re, f_im = fin_ref[seg:seg + 1, lr], fin_ref[seg:seg + 1, li]
            c_re, c_im = f_re + ak_re * c_re - ak_im * c_im, f_im + ak_re * c_im + ak_im * c_re
        carry_ref[:, lr] = c_re
        carry_ref[:, li] = c_im
        e_re, e_im = enter_ref[:, lr], enter_ref[:, li]

        def fix(k, _):
            rows = _tile(k)
            p_re = p_ref[pl.ds(k, 1), lr]
            p_im = p_ref[pl.ds(k, 1), li]
            s_ref[rows, lr] = s_ref[rows, lr] + (p_re * e_re - p_im * e_im)
            s_ref[rows, li] = s_ref[rows, li] + (p_re * e_im + p_im * e_re)
            return 0

        _steps(0, k_steps, fix, 0)


def _scan_backward(q, g_ref, s_ref, p_ref, carry_ref, s_in_ref, fin_ref, da_ref, k_steps):
    seg_id = lax.broadcasted_iota(jnp.int32, (SUBLANES, LANE_CHUNK), 0)
    for re, im in _lane_chunks(q):
        lr, li = pl.ds(re, LANE_CHUNK), pl.ds(im, LANE_CHUNK)
        a_re = jnp.broadcast_to(p_ref[0:1, lr], (SUBLANES, LANE_CHUNK))
        a_im = jnp.broadcast_to(p_ref[0:1, li], (SUBLANES, LANE_CHUNK))

        def local(j, st):
            sr, si = st
            rows = _tile(k_steps - 1 - j)
            nr = a_re * sr + a_im * si + g_ref[rows, lr]
            ni = a_re * si - a_im * sr + g_ref[rows, li]
            g_ref[rows, lr] = nr
            g_ref[rows, li] = ni
            return nr, ni

        zero = jnp.zeros((SUBLANES, LANE_CHUNK), F32)
        fr, fi = _steps(0, k_steps, local, (zero, zero))
        fin_ref[:, lr] = fr
        fin_ref[:, li] = fi
        ak_re, ak_im = p_ref[k_steps - 1:k_steps, lr], p_ref[k_steps - 1:k_steps, li]
        c_re, c_im = carry_ref[:, lr], carry_ref[:, li]
        lam_in = [None] * SUBLANES
        for seg in reversed(range(SUBLANES)):
            lam_in[seg] = (c_re, c_im)
            f_re, f_im = fin_ref[seg:seg + 1, lr], fin_ref[seg:seg + 1, li]
            c_re, c_im = f_re + ak_re * c_re + ak_im * c_im, f_im + ak_re * c_im - ak_im * c_re
        carry_ref[:, lr] = c_re
        carry_ref[:, li] = c_im
        for seg in range(SUBLANES):
            fin_ref[seg:seg + 1, lr] = lam_in[seg][0]
            fin_ref[seg:seg + 1, li] = lam_in[seg][1]
        e_re, e_im = fin_ref[:, lr], fin_ref[:, li]

        def fix_with(k, acc, sp_re, sp_im):
            acc_re, acc_im = acc
            rows = _tile(k)
            p_re = p_ref[pl.ds(k_steps - 1 - k, 1), lr]
            p_im = p_ref[pl.ds(k_steps - 1 - k, 1), li]
            l_re = g_ref[rows, lr] + (p_re * e_re + p_im * e_im)
            l_im = g_ref[rows, li] + (p_re * e_im - p_im * e_re)
            g_ref[rows, lr] = l_re
            g_ref[rows, li] = l_im
            return acc_re + (l_re * sp_re + l_im * sp_im), acc_im + (l_im * sp_re - l_re * sp_im)

        def fix(k, acc):
            prev = _tile(k - 1)
            return fix_with(k, acc, s_ref[prev, lr], s_ref[prev, li])

        last = _tile(k_steps - 1)
        before_re = jnp.where(seg_id == 0, s_in_ref[:, lr], pltpu.roll(s_ref[last, lr], 1, axis=0))
        before_im = jnp.where(seg_id == 0, s_in_ref[:, li], pltpu.roll(s_ref[last, li], 1, axis=0))
        acc = fix_with(0, (zero, zero), before_re, before_im)
        acc_re, acc_im = _steps(1, k_steps, fix, acc)
        da_ref[:, lr] = da_ref[:, lr] + jnp.sum(acc_re, axis=0, keepdims=True)
        da_ref[:, li] = da_ref[:, li] + jnp.sum(acc_im, axis=0, keepdims=True)


def _prenorm(x, mod3, norm_pre):
    xn, r = _rms_parts(x)
    return xn, r, xn * norm_pre * (1.0 + mod3[1:2, :]) + mod3[0:1, :]


CHIP_FLIPS = (4, 2, 6)


def _shard_order(me):
    flips = [0, 1] + [f + c for f in CHIP_FLIPS for c in (0, 1)]
    return jnp.stack([me ^ f for f in flips]).astype(jnp.int32)


def _in_proj(x, mod3, norm_pre, w_in_s, shards):
    rows = x.shape[0]
    tb = _tb(rows, 2048)
    nblk = rows // tb
    n_sh = len(shards)
    last_step = N_DEV - 1
    items = [_gather_item(0, 0, _pool_rows_of(shards[0].shape[1]))] + \
            [_gather_item(t, t, _rows_of(shards[t].shape[0])) for t in range(1, n_sh)]

    def body(order_ref, x_ref, mod_ref, np_ref, w_src, *rest):
        src_refs, proj_ref, w_full, out_refs = rest[:n_sh], rest[n_sh], rest[n_sh + 1], rest[n_sh + 2:2 * n_sh + 2]
        h_scr, wg, ssem, rsem, lsem, *sems = rest[2 * n_sh + 2:]
        s, i = pl.program_id(0), pl.program_id(1)
        me3 = _me()
        me = _flat(me3)
        sibling = _peer(1)

        def own_copy(slot, k):
            return pltpu.make_async_remote_copy(src_ref=w_src, dst_ref=wg.at[me], send_sem=ssem.at[slot],
                                                recv_sem=rsem.at[slot], device_id=_peer(k), device_id_type=MESH)

        def passed_copy(j):
            p = _flat(_peer(CHIP_FLIPS[j]))
            return pltpu.make_async_remote_copy(src_ref=wg.at[p], dst_ref=wg.at[p], send_sem=ssem.at[4 + j],
                                                recv_sem=rsem.at[4 + j], device_id=sibling, device_id_type=MESH)

        def arrival(slot, flip):
            p = _flat(_peer(flip))
            pltpu.make_async_remote_copy(src_ref=w_src, dst_ref=wg.at[p], send_sem=ssem.at[slot],
                                         recv_sem=rsem.at[slot], device_id=sibling, device_id_type=MESH).wait_recv()

        def keep(t):
            p = order_ref[t]
            return pltpu.make_async_copy(wg.at[p], w_full.at[:, pl.ds(p * W_IN_SHARD, W_IN_SHARD)], lsem.at[1 + t])

        first = i == 0
        for t in range(last_step):
            pl.when(first & (s == t + 1))(lambda t=t: keep(t).start())

        @pl.when(first & (s == 0))
        def _():
            mine = pltpu.make_async_copy(w_src, wg.at[me], lsem.at[0])
            mine.start()
            own_copy(0, 1).start()
            for j, f in enumerate(CHIP_FLIPS[:2]):
                own_copy(1 + j, f).start()
            mine.wait()

        @pl.when(first & (s == 1))
        def _():
            arrival(0, 1)

        for j, f in enumerate(CHIP_FLIPS):
            @pl.when(first & (s == 2 + 2 * j))
            def _(j=j, f=f):
                arrival(1 + j, f)
                passed_copy(j).start()
                if j == 0:
                    own_copy(3, CHIP_FLIPS[2]).start()

            @pl.when(first & (s == 3 + 2 * j))
            def _(j=j, f=f):
                arrival(4 + j, f + 1)

        @pl.when(first & (s == last_step - 1))
        def _():
            _hosted_copies(items, src_refs, out_refs, *sems, act="start")

        rows_i = pl.ds(pl.multiple_of(i * tb, tb), tb)

        @pl.when(s == 0)
        def _():
            _, _, h = _prenorm(x_ref[...], mod_ref[...], np_ref[...])
            h_scr[rows_i, :] = h.astype(BF16)

        proj_ref[...] = _dot(h_scr[rows_i, :], wg[order_ref[s]]).astype(BF16)

        @pl.when((s == last_step) & (i == nblk - 1))
        def _():
            own_copy(0, 1).wait_send()
            for j, f in enumerate(CHIP_FLIPS):
                own_copy(1 + j, f).wait_send()
                passed_copy(j).wait_send()
            keep(last_step).start()
            for t in range(N_DEV):
                keep(t).wait()
            _hosted_copies(items, src_refs, out_refs, *sems, act="wait")

    full = [jax.ShapeDtypeStruct((4, 256, 256), BF16)] + [jax.ShapeDtypeStruct((D, D), BF16)] * (n_sh - 1)
    grid_spec = pltpu.PrefetchScalarGridSpec(
        num_scalar_prefetch=1, grid=(N_DEV, nblk),
        in_specs=[pl.BlockSpec((tb, D), lambda s, i, order: (jnp.where(s == 0, i, nblk - 1), 0)),
                  pl.BlockSpec((3, D), lambda s, i, order: (0, 0)), pl.BlockSpec((1, D), lambda s, i, order: (0, 0)),
                  ANY] + [ANY] * n_sh,
        out_specs=(pl.BlockSpec((tb, W_IN_SHARD), lambda s, i, order: (i, order[s])), ANY, *([ANY] * n_sh)),
        scratch_shapes=[pltpu.VMEM((rows, D), BF16), pltpu.VMEM((N_DEV, D, W_IN_SHARD), BF16),
                        pltpu.SemaphoreType.DMA((N_DEV - 1,)), pltpu.SemaphoreType.DMA((N_DEV - 1,)),
                        pltpu.SemaphoreType.DMA((1 + N_DEV,))] + _sem_scratch(items))
    return _pcall(body, name="in_proj", grid_spec=grid_spec,
                  out_shape=(jax.ShapeDtypeStruct((rows, N_IN), BF16), jax.ShapeDtypeStruct((D, N_IN), BF16), *full),
                  compiler_params=_params(("arbitrary", "arbitrary")),
                  )(_shard_order(_flat(_me())), x, mod3, norm_pre, w_in_s, *shards)


def _pool_windows(ext, tb, first_row):
    inv_counts = _inv_counts(tb, first_row)
    pooled = []
    for g, w in enumerate(POOL_WINDOWS):
        acc = ext[:, g * 256:(g + 1) * 256]
        tok = acc[HALO:, :]
        s = 1
        while s < w:
            acc = acc + pltpu.roll(acc, s, axis=0)
            s *= 2
        pooled.append(acc[HALO:, :] * inv_counts[g] - tok)
    return pooled, inv_counts


def _inv_counts(tb, first_row):
    pos = (first_row + lax.broadcasted_iota(jnp.int32, (tb, 1), 0) + 1).astype(F32)
    return [1.0 / jnp.minimum(pos, float(w)) for w in POOL_WINDOWS]


def _pool_fwd(proj, pool_w, pool_scale):
    rows = proj.shape[0]
    tb = _tb(rows, 1024)
    hb = tb // HALO

    def body(u_ref, halo_ref, z_ref, pw_ref, ps_ref, y_ref, pooled_ref):
        i = pl.program_id(0)
        u = u_ref[...].astype(F32)
        halo = jnp.where(i > 0, halo_ref[...].astype(F32), 0.0)
        pooled, _ = _pool_windows(jnp.concatenate([halo, u], axis=0), tb, i * tb)
        silu_z, _ = _silu_parts(z_ref[...].astype(F32))
        for g in range(4):
            cols = slice(g * 256, (g + 1) * 256)
            pooled_b = pooled[g].astype(BF16)
            pooled_ref[:, cols] = pooled_b
            mixed = _dot(pooled_b, pw_ref[g])
            y_ref[:, cols] = (mixed * ps_ref[:, cols] * silu_z[:, cols]).astype(BF16)

    blk = pl.BlockSpec((tb, D), lambda i: (i, 0))
    return _pcall(body, name="pool_fwd", grid=(rows // tb,),
                  out_shape=(jax.ShapeDtypeStruct((rows, D), BF16), jax.ShapeDtypeStruct((rows, D), BF16)),
                  in_specs=[blk, pl.BlockSpec((HALO, D), lambda i: (jnp.maximum(i * hb - 1, 0), 0)),
                            pl.BlockSpec((tb, D), lambda i: (i, 1)),
                            _full((4, 256, 256)), _full((1, D))],
                  out_specs=(blk, blk),
                  compiler_params=_params(("arbitrary",)))(proj, proj, proj, pool_w, pool_scale)


def _ssm_fwd(proj, pm, pmt, wb, wct, ptab, dvec, glu_w, glu_b, shards):
    rows = proj.shape[0]
    tb = pm.shape[0]
    k_steps = tb // SUBLANES
    nblk = rows // tb
    n_sh = len(shards)
    items = [_gather_item(t, t, _rows_of(shards[t].shape[0])) for t in range(n_sh)]

    def body(u_ref, z_ref, pm_ref, pmt_ref, wb_ref, wct_ref, p_ref, d_ref, gw_ref, gb_ref, *rest):
        src_refs = rest[:n_sh]
        y_ref, ys_ref, carry_out_ref, s_ref, gate_ref, zp_ref, up_ref = rest[n_sh:n_sh + 7]
        out_refs = rest[n_sh + 7:2 * n_sh + 7]
        carry_ref, enter_ref, fin_ref, *sems = rest[2 * n_sh + 7:]

        @pl.when(pl.program_id(0) == 0)
        def _():
            _hosted_copies(items, src_refs, out_refs, *sems, act="start")
            carry_ref[...] = jnp.zeros_like(carry_ref)

        carry_out_ref[...] = carry_ref[...]
        up = _dot(pm_ref[...], u_ref[...]).astype(BF16)
        up_ref[...] = up

        for q in range(N_Q):
            s_ref[:, q * Q_W:(q + 1) * Q_W] = _dot(up[:, q * 256:(q + 1) * 256], wb_ref[q])
        for q in range(N_Q):
            _scan_forward(q, s_ref, p_ref, carry_ref, enter_ref, fin_ref, k_steps)
        for q in range(N_Q):
            cols = slice(q * 256, (q + 1) * 256)
            y = _dot_nt(s_ref[:, q * Q_W:(q + 1) * Q_W].astype(BF16), wct_ref[q])
            ys_ref[:, cols] = y + d_ref[:, cols] * up[:, cols].astype(F32)
        yg, _ = _gelu_parts(ys_ref[...])
        gate = jax.nn.sigmoid(_dot(yg.astype(BF16), gw_ref[...]) + gb_ref[...])
        gate_ref[...] = gate
        zp = _dot(pm_ref[...], z_ref[...])
        zp_ref[...] = zp.astype(BF16)
        silu_z, _ = _silu_parts(zp)
        y_ref[...] = _dot(pmt_ref[...], (yg * gate * silu_z).astype(BF16)).astype(BF16)

        @pl.when(pl.program_id(0) == nblk - 1)
        def _():
            _hosted_copies(items, src_refs, out_refs, *sems, act="wait")

    return _pcall(body, name="ssm_fwd", grid=(nblk,),
                  out_shape=(jax.ShapeDtypeStruct((rows, D), BF16), jax.ShapeDtypeStruct((rows, D), F32),
                             jax.ShapeDtypeStruct((nblk, 1, N_STATE), F32),
                             jax.ShapeDtypeStruct((rows, N_STATE), F32),
                             jax.ShapeDtypeStruct((rows, D), F32), jax.ShapeDtypeStruct((rows, D), BF16),
                             jax.ShapeDtypeStruct((rows, D), BF16),
                             *[jax.ShapeDtypeStruct((D, D), BF16)] * n_sh),
                  in_specs=[pl.BlockSpec((tb, D), lambda i: (i, 2)), pl.BlockSpec((tb, D), lambda i: (i, 3)),
                            _full((tb, tb)), _full((tb, tb)),
                            _full((N_Q, 256, Q_W), single=True), _full((N_Q, 256, Q_W), single=True),
                            _full((k_steps, N_STATE)), _full((1, D)), _full((D, D), single=True), _full((1, D))] +
                           [ANY] * n_sh,
                  out_specs=(pl.BlockSpec((tb, D), lambda i: (i, 0)), pl.BlockSpec((tb, D), lambda i: (i, 0)),
                             pl.BlockSpec((None, 1, N_STATE), lambda i: (i, 0, 0)),
                             pl.BlockSpec((tb, N_STATE), lambda i: (i, 0)),
                             *[pl.BlockSpec((tb, D), lambda i: (i, 0))] * 3, *([ANY] * n_sh)),
                  scratch_shapes=[pltpu.VMEM((1, N_STATE), F32),
                                  pltpu.VMEM((SUBLANES, N_STATE), F32), pltpu.VMEM((SUBLANES, N_STATE), F32)] +
                                 _sem_scratch(items),
                  compiler_params=_params(("arbitrary",)))(proj, proj, pm, pmt, wb, wct, ptab, dvec, glu_w, glu_b,
                                                           *shards)


def _head(x, target, proj, y_pool, y_ssm, mod3, norm_post, wbp, wbs, wout):
    rows = x.shape[0]
    tb = _tb(rows, 256)
    nblk = rows // tb
    n_feat = float(D)

    def body(x_ref, t_ref, gp_ref, gs_ref, yp_ref, ys_ref, mod_ref, npost_ref, wbp_ref, wbs_ref, wout_ref,
             loss_ref, dy_ref, dyp_ref, dys_ref, dg_ref, dwbp_hbm, dwbs_hbm, dwout_hbm, vec_ref,
             acc_bp, acc_bs, acc_out, acc_loss, acc_vec):
        i = pl.program_id(0)

        @pl.when(i == 0)
        def _():
            acc_bp[...] = jnp.zeros_like(acc_bp)
            acc_bs[...] = jnp.zeros_like(acc_bs)
            acc_out[...] = jnp.zeros_like(acc_out)
            acc_loss[...] = jnp.zeros_like(acc_loss)
            acc_vec[...] = jnp.zeros_like(acc_vec)

        gate = mod_ref[2:3, :]
        npost = npost_ref[...]
        yp, ys = yp_ref[...], ys_ref[...]
        sgp = jax.nn.sigmoid(gp_ref[...].astype(F32))
        sgs = jax.nn.sigmoid(gs_ref[...].astype(F32))
        pb = _dot(yp, wbp_ref[...])
        psm = _dot(ys, wbs_ref[...])
        mb = (sgp * pb + sgs * psm).astype(BF16)
        out = _dot(mb, wout_ref[...])
        on, r = _rms_parts(out)
        normed = on * npost
        diff = x_ref[...] + gate * normed - t_ref[...]
        acc_loss[...] += jnp.sum(diff * diff, axis=0, keepdims=True)
        dy = diff * (1.0 / n_feat)
        dy_ref[...] = dy
        acc_vec[0:1, :] += jnp.sum(dy * normed, axis=0, keepdims=True)
        dn = dy * gate
        acc_vec[1:2, :] += jnp.sum(dn * on, axis=0, keepdims=True)
        dout = _rms_bwd(dn * npost, on, r).astype(BF16)
        dm = _dot_nt(dout, wout_ref[...])
        dpb = (dm * sgp).astype(BF16)
        dps = (dm * sgs).astype(BF16)
        dg_ref[:, :D] = (dm * pb * sgp * (1.0 - sgp)).astype(BF16)
        dg_ref[:, D:] = (dm * psm * sgs * (1.0 - sgs)).astype(BF16)
        dyp_ref[...] = _dot_nt(dpb, wbp_ref[...]).astype(BF16)
        dys_ref[...] = _dot_nt(dps, wbs_ref[...]).astype(BF16)
        acc_out[...] += _dot_tn(mb, dout)
        acc_bp[...] += _dot_tn(yp, dpb)
        acc_bs[...] += _dot_tn(ys, dps)

        @pl.when(i == nblk - 1)
        def _():
            loss_ref[...] = 0.5 / n_feat * jnp.sum(acc_loss[...], axis=1, keepdims=True)
            vec_ref[...] = acc_vec[...]
            pltpu.sync_copy(acc_bp, dwbp_hbm)
            pltpu.sync_copy(acc_bs, dwbs_hbm)
            pltpu.sync_copy(acc_out, dwout_hbm)

    row = lambda c: pl.BlockSpec((tb, D), lambda i: (i, c))
    w = _full((D, D), single=True)
    return _pcall(body, name="head", grid=(nblk,),
                  out_shape=(jax.ShapeDtypeStruct((1, 1), F32), jax.ShapeDtypeStruct((rows, D), F32),
                             jax.ShapeDtypeStruct((rows, D), BF16), jax.ShapeDtypeStruct((rows, D), BF16),
                             jax.ShapeDtypeStruct((rows, 2 * D), BF16),
                             jax.ShapeDtypeStruct((D, D), F32), jax.ShapeDtypeStruct((D, D), F32),
                             jax.ShapeDtypeStruct((D, D), F32), jax.ShapeDtypeStruct((2, D), F32)),
                  in_specs=[row(0), row(0), row(4), row(5), row(0), row(0), _full((3, D)), _full((1, D)), w, w, w],
                  out_specs=(_full((1, 1)), row(0), row(0), row(0), pl.BlockSpec((tb, 2 * D), lambda i: (i, 0)),
                             ANY, ANY, ANY, _full((2, D))),
                  scratch_shapes=[pltpu.VMEM((D, D), F32), pltpu.VMEM((D, D), F32), pltpu.VMEM((D, D), F32),
                                  pltpu.VMEM((1, D), F32), pltpu.VMEM((2, D), F32)],
                  compiler_params=_params(("arbitrary",)))(x, target, proj, proj, y_pool, y_ssm, mod3, norm_post,
                                                           wbp, wbs, wout)


def _glu_bwd(dys, zp, ys_pre, gate, pm, pmt, glu_w):
    rows = dys.shape[0]
    tb = pm.shape[0]
    nblk = rows // tb

    def body(dys_ref, z_ref, ysp_ref, sg_ref, pm_ref, pmt_ref, gw_ref, dyp_ref, dz_ref, dgw_hbm, dgb_ref,
             acc_w, acc_b):
        i = pl.program_id(0)

        @pl.when(i == 0)
        def _():
            acc_w[...] = jnp.zeros_like(acc_w)
            acc_b[...] = jnp.zeros_like(acc_b)

        d_out = _dot(pm_ref[...], dys_ref[...])
        yg, dgelu = _gelu_parts(ysp_ref[...])
        ygb = yg.astype(BF16)
        sg = sg_ref[...]
        silu_z, dsilu_z = _silu_parts(z_ref[...].astype(F32))
        dz = d_out * (yg * sg) * dsilu_z
        dz_ref[...] = _dot(pmt_ref[...], dz.astype(BF16)).astype(BF16)
        dglu = d_out * silu_z
        dq = dglu * yg * sg * (1.0 - sg)
        dqb = dq.astype(BF16)
        acc_b[...] += jnp.sum(dq, axis=0, keepdims=True)
        acc_w[...] += _dot_tn(ygb, dqb)
        dyg = dglu * sg + _dot_nt(dqb, gw_ref[...])
        dyp_ref[...] = (dyg * dgelu).astype(BF16)

        @pl.when(i == nblk - 1)
        def _():
            dgb_ref[...] = acc_b[...]
            pltpu.sync_copy(acc_w, dgw_hbm)

    row = lambda c: pl.BlockSpec((tb, D), lambda i: (i, c))
    return _pcall(body, name="glu_bwd", grid=(nblk,),
                  out_shape=(jax.ShapeDtypeStruct((rows, D), BF16), jax.ShapeDtypeStruct((rows, D), BF16),
                             jax.ShapeDtypeStruct((D, D), F32), jax.ShapeDtypeStruct((1, D), F32)),
                  in_specs=[row(0), row(0), row(0), row(0), _full((tb, tb)), _full((tb, tb)),
                            _full((D, D), single=True)],
                  out_specs=(row(0), row(0), ANY, _full((1, D))),
                  scratch_shapes=[pltpu.VMEM((D, D), F32), pltpu.VMEM((1, D), F32)],
                  compiler_params=_params(("arbitrary",)))(dys, zp, ys_pre, gate, pm, pmt, glu_w)


def _ssm_bwd(dyp, up, states, carries, pmt, wb, wct, ptab, dvec, mat_grads, dpool_w, dw_in_rest):
    rows = dyp.shape[0]
    tb = pmt.shape[0]
    k_steps = tb // SUBLANES
    nblk = rows // tb
    n_mat = len(mat_grads)
    hosted = [*mat_grads, dpool_w, dw_in_rest]
    n_h = len(hosted)
    shard_rows = D // N_DEV
    pool_rows = dpool_w.shape[1] // N_DEV
    items = [_scatter_item(t, t, _rows_of(shard_rows)) for t in range(n_mat)] + \
            [_scatter_item(n_mat, n_mat, _pool_rows_of(pool_rows))] + \
            [_w_in_block_item(n_mat + 1, n_mat + 1, j, ssm_part=False) for j in range(W_IN_SHARD // W_IN_BLOCK)]
    n_in, n_out = 9, 5

    def body(*refs):
        dyp_ref, u_ref, s_ref, cin_ref, pmt_ref, wb_ref, wct_ref, p_ref, d_ref = refs[:n_in]
        src_refs = refs[n_in:n_in + n_h]
        du_ref, dbb_ref, dcc_ref, da_ref, dd_ref = refs[n_in + n_h:n_in + n_h + n_out]
        recv_refs = refs[n_in + n_h + n_out:n_in + 2 * n_h + n_out]
        (g_ref, carry_b, fin_ref, acc_wb, acc_wct, acc_da, acc_dd, dup_ref,
         *sems) = refs[n_in + 2 * n_h + n_out:]
        i = pl.program_id(0)

        @pl.when(i == 0)
        def _():
            _hosted_copies(items, src_refs, recv_refs, *sems, act="start")
            carry_b[...] = jnp.zeros_like(carry_b)
            acc_wb[...] = jnp.zeros_like(acc_wb)
            acc_wct[...] = jnp.zeros_like(acc_wct)
            acc_da[...] = jnp.zeros_like(acc_da)
            acc_dd[...] = jnp.zeros_like(acc_dd)

        def own_products(acc, q, chan, state_ref):
            chan_t = chan.T
            for j in range(16 // 2):
                r = slice(j * 2 * G_H, (j + 1) * 2 * G_H)
                re = q * Q_W + j * 128
                slab = jnp.concatenate([state_ref[:, re:re + 128], state_ref[:, re + Q_W // 2:re + Q_W // 2 + 128]],
                                       axis=1).astype(BF16)
                acc[q, r, :] += _dot(chan_t[r, :], slab)

        dy = dyp_ref[...]
        up = u_ref[...]
        acc_dd[...] += jnp.sum(dy.astype(F32) * up.astype(F32), axis=0, keepdims=True)
        for q in range(N_Q):
            cols = slice(q * 256, (q + 1) * 256)
            g_ref[:, q * Q_W:(q + 1) * Q_W] = _dot(dy[:, cols], wct_ref[q])
            own_products(acc_wct, q, dy[:, cols], s_ref)
        for q in range(N_Q):
            _scan_backward(q, g_ref, s_ref, p_ref, carry_b, cin_ref, fin_ref, acc_da, k_steps)
        for q in range(N_Q):
            cols = slice(q * 256, (q + 1) * 256)
            own_products(acc_wb, q, up[:, cols], g_ref)
            lam = g_ref[:, q * Q_W:(q + 1) * Q_W].astype(BF16)
            dup_ref[:, cols] = (_dot_nt(lam, wb_ref[q]) + d_ref[:, cols] * dy[:, cols].astype(F32)).astype(BF16)
        du_ref[...] = _dot(pmt_ref[...], dup_ref[...]).astype(BF16)

        @pl.when(i == nblk - 1)
        def _():
            da_ref[...] = acc_da[...]
            dd_ref[...] = acc_dd[...]
            lane = lax.broadcasted_iota(jnp.int32, (16 * G_H, 128), 1)
            row = lax.broadcasted_iota(jnp.int32, (16 * G_H, 128), 0)
            own = lane // G_P == (row // G_H) % 2
            spread = (lax.broadcasted_iota(jnp.int32, (G_P, 128), 1) % G_P ==
                      lax.broadcasted_iota(jnp.int32, (G_P, 128), 0)).astype(F32)
            for acc, out in ((acc_wb, dbb_ref), (acc_wct, dcc_ref)):
                for half in range(2):
                    for q in range(N_Q):
                        kept = jnp.where(own, acc[q, :, half * 128:(half + 1) * 128], 0.0)
                        out[half, q] = lax.dot_general(kept, spread, (((1,), (1,)), ((), ())),
                                                       preferred_element_type=F32, precision=lax.Precision.HIGHEST)
            _hosted_copies(items, src_refs, recv_refs, *sems, act="wait")

    rev = lambda c: pl.BlockSpec((tb, D), lambda i: (nblk - 1 - i, c))
    recv = [jax.ShapeDtypeStruct((N_DEV, shard_rows, D), F32)] * n_mat + \
           [jax.ShapeDtypeStruct((N_DEV, dpool_w.shape[0], pool_rows, dpool_w.shape[2]), F32),
            jax.ShapeDtypeStruct((N_DEV, D, W_IN_SHARD), BF16)]
    return _pcall(body, name="ssm_bwd", grid=(nblk,),
                  out_shape=(jax.ShapeDtypeStruct((rows, D), BF16),
                             jax.ShapeDtypeStruct((2, N_Q, 16 * G_H, G_P), F32),
                             jax.ShapeDtypeStruct((2, N_Q, 16 * G_H, G_P), F32),
                             jax.ShapeDtypeStruct((1, N_STATE), F32), jax.ShapeDtypeStruct((1, D), F32), *recv),
                  in_specs=[rev(0), rev(0), pl.BlockSpec((tb, N_STATE), lambda i: (nblk - 1 - i, 0)),
                            pl.BlockSpec((None, 1, N_STATE), lambda i: (nblk - 1 - i, 0, 0)),
                            _full((tb, tb)),
                            _full((N_Q, 256, Q_W), single=True), _full((N_Q, 256, Q_W), single=True),
                            _full((k_steps, N_STATE)), _full((1, D))] + [ANY] * n_h,
                  out_specs=(rev(0), _full((2, N_Q, 16 * G_H, G_P)), _full((2, N_Q, 16 * G_H, G_P)),
                             _full((1, N_STATE)), _full((1, D)), *([ANY] * n_h)),
                  scratch_shapes=[pltpu.VMEM((tb, N_STATE), F32), pltpu.VMEM((1, N_STATE), F32),
                                  pltpu.VMEM((SUBLANES, N_STATE), F32),
                                  pltpu.VMEM((N_Q, 16 * G_H, 256), F32), pltpu.VMEM((N_Q, 16 * G_H, 256), F32),
                                  pltpu.VMEM((1, N_STATE), F32), pltpu.VMEM((1, D), F32),
                                  pltpu.VMEM((tb, D), BF16)] + _sem_scratch(items),
                  compiler_params=_params(("arbitrary",), vmem=60 * 1024 * 1024),
                  )(dyp, up, states, carries, pmt, wb, wct, ptab, dvec, *hosted)


def _pool_bwd(dyp, pooled, proj, pool_w, pool_scale):
    rows = dyp.shape[0]
    tb = _tb(rows, 1024)
    nblk = rows // tb

    def body(dy_ref, pooled_ref, z_ref, pw_ref, ps_ref, dp_ref, dpw_ref, dps_ref, ahead_ref):
        i = pl.program_id(0)
        blk = nblk - 1 - i

        @pl.when(i == 0)
        def _():
            ahead_ref[...] = jnp.zeros_like(ahead_ref)
            dpw_ref[...] = jnp.zeros_like(dpw_ref)
            dps_ref[...] = jnp.zeros_like(dps_ref)

        inv_counts = _inv_counts(tb, blk * tb)
        silu_z, dsilu_z = _silu_parts(z_ref[...].astype(F32))
        dy = dy_ref[...].astype(F32)
        for g, w in enumerate(POOL_WINDOWS):
            cols = slice(g * 256, (g + 1) * 256)
            pooled_b = pooled_ref[:, cols]
            mixed = _dot(pooled_b, pw_ref[g])
            scale = ps_ref[:, cols]
            dp_ref[:, D + g * 256:D + (g + 1) * 256] = (dy[:, cols] * (mixed * scale) * dsilu_z[:, cols]).astype(BF16)
            dms = dy[:, cols] * silu_z[:, cols]
            dps_ref[:, cols] += jnp.sum(dms * mixed, axis=0, keepdims=True)
            dmixed = (dms * scale).astype(BF16)
            dpw_ref[g] += _dot_tn(pooled_b, dmixed)
            dpooled = _dot_nt(dmixed, pw_ref[g])
            ratio = dpooled * inv_counts[g]
            acc = jnp.concatenate([ratio, ahead_ref[:, cols]], axis=0)
            ahead_ref[:, cols] = ratio[:HALO, :]
            s = 1
            while s < w:
                acc = acc + pltpu.roll(acc, tb + HALO - s, axis=0)
                s *= 2
            dp_ref[:, cols] = (acc[:tb, :] - dpooled).astype(BF16)

    rev = lambda c: pl.BlockSpec((tb, D), lambda i: (nblk - 1 - i, c))
    return _pcall(body, name="pool_bwd", grid=(nblk,),
                  out_shape=(jax.ShapeDtypeStruct((rows, 2 * D), BF16), jax.ShapeDtypeStruct((4, 256, 256), F32),
                             jax.ShapeDtypeStruct((1, D), F32)),
                  in_specs=[rev(0), rev(0), rev(1), _full((4, 256, 256)), _full((1, D))],
                  out_specs=(pl.BlockSpec((tb, 2 * D), lambda i: (nblk - 1 - i, 0)), _full((4, 256, 256)),
                             _full((1, D))),
                  scratch_shapes=[pltpu.VMEM((HALO, D), F32)],
                  compiler_params=_params(("arbitrary",)))(dyp, pooled, proj, pool_w, pool_scale)


def _dproj_specs(tb):
    return [pl.BlockSpec((tb, 2 * D), lambda i: (i, 0)), pl.BlockSpec((tb, D), lambda i: (i, 0)),
            pl.BlockSpec((tb, D), lambda i: (i, 0)), pl.BlockSpec((tb, 2 * D), lambda i: (i, 0))]


def _in_proj_bwd_x(x, dy, dpp, dus, dzs, dpg, mod3, norm_pre, w_in, dw_in_ssm, recv_w_in):
    rows = x.shape[0]
    tb = _tb(rows, 512)
    nblk = rows // tb
    items = [_w_in_block_item(0, 0, j, ssm_part=True) for j in range(W_IN_SHARD // W_IN_BLOCK)]
    sums_item = [_Item(0, 0, _whole, _slot)]

    def body(x_ref, dy_ref, dpp_ref, dus_ref, dzs_ref, dpg_ref, mod_ref, np_ref, w_ref,
             dw_src, _, gx_ref, recv_w, recv_sums, vec_ref, ssem, rsem, lsem, *sums_sems):
        src_refs, recv_refs, sems = (dw_src,), (recv_w,), (ssem, rsem, lsem)

        @pl.when(pl.program_id(0) == 0)
        def _():
            _hosted_copies(items, src_refs, recv_refs, *sems, act="start")
            vec_ref[...] = jnp.zeros_like(vec_ref)

        dh = _dot_nt(dpp_ref[...], w_ref[:, 0:2 * D])
        dh += _dot_nt(dus_ref[...], w_ref[:, 2 * D:3 * D])
        dh += _dot_nt(dzs_ref[...], w_ref[:, 3 * D:4 * D])
        dh += _dot_nt(dpg_ref[...], w_ref[:, 4 * D:6 * D])
        xn, r, _ = _prenorm(x_ref[...], mod_ref[...], np_ref[...])
        one_scale = 1.0 + mod_ref[1:2, :]
        vec_ref[0:1, :] += jnp.sum(dh, axis=0, keepdims=True)
        vec_ref[1:2, :] += jnp.sum(dh * xn, axis=0, keepdims=True) * np_ref[...]
        vec_ref[2:3, :] += jnp.sum(dh * xn, axis=0, keepdims=True) * one_scale
        gx_ref[...] = dy_ref[...] + _rms_bwd(dh * (np_ref[...] * one_scale), xn, r)

        @pl.when(pl.program_id(0) == nblk - 1)
        def _():
            _hosted_copies(sums_item, (vec_ref,), (recv_sums,), *sums_sems, act="start")
            _hosted_copies(items, src_refs, recv_refs, *sems, act="wait")
            _hosted_copies(sums_item, (vec_ref,), (recv_sums,), *sums_sems, act="wait")

    row = pl.BlockSpec((tb, D), lambda i: (i, 0))
    recv = (jax.ShapeDtypeStruct(recv_w_in.shape, recv_w_in.dtype), jax.ShapeDtypeStruct((N_DEV, 3, D), F32))
    return _pcall(body, name="in_proj_bwd_x", grid=(nblk,),
                  out_shape=(jax.ShapeDtypeStruct((rows, D), F32), *recv),
                  in_specs=[row, row] + _dproj_specs(tb) + [_full((3, D)), _full((1, D)),
                                                            _full((D, N_IN), single=True)] + [ANY] * 2,
                  out_specs=(row, ANY, ANY),
                  input_output_aliases={10: 1},
                  scratch_shapes=[pltpu.VMEM((3, D), F32)] + _sem_scratch(items) + _sem_scratch(sums_item),
                  compiler_params=_params(("arbitrary",)))(x, dy, dpp, dus, dzs, dpg, mod3, norm_pre, w_in,
                                                           dw_in_ssm, recv_w_in)


def _in_proj_bwd_w(name, x, dparts, mod3, norm_pre, gathered=()):
    rows = x.shape[0]
    tb = _tb(rows, 512)
    nblk = rows // tb
    widths = [p.shape[1] for p in dparts]
    n_p, n_g = len(dparts), len(gathered)
    items = [_Item(t, t, _whole, _slot) for t in range(n_g)]

    def body(x_ref, *rest):
        part_refs, (mod_ref, np_ref) = rest[:n_p], rest[n_p:n_p + 2]
        src_refs, dw_ref = rest[n_p + 2:n_p + 2 + n_g], rest[n_p + 2 + n_g]
        recv_refs, (acc, *sems) = rest[n_p + 3 + n_g:n_p + 3 + 2 * n_g], rest[n_p + 3 + 2 * n_g:]
        i = pl.program_id(0)

        @pl.when(i == 0)
        def _():
            if n_g:
                _hosted_copies(items, src_refs, recv_refs, *sems, act="start")
            acc[...] = jnp.zeros_like(acc)

        _, _, h = _prenorm(x_ref[...], mod_ref[...], np_ref[...])
        ht = h.astype(BF16)
        lo = 0
        for ref, w in zip(part_refs, widths):
            acc[:, lo:lo + w] += _dot_tn(ht, ref[...])
            lo += w

        @pl.when(i == nblk - 1)
        def _():
            dw_ref[...] = acc[...].astype(BF16)
            if n_g:
                _hosted_copies(items, src_refs, recv_refs, *sems, act="wait")

    row = pl.BlockSpec((tb, D), lambda i: (i, 0))
    out = _pcall(body, name=name, grid=(nblk,),
                 out_shape=(jax.ShapeDtypeStruct((D, sum(widths)), BF16),
                            *[jax.ShapeDtypeStruct((N_DEV,) + g.shape, g.dtype) for g in gathered]),
                 in_specs=[row] + [pl.BlockSpec((tb, w), lambda i: (i, 0)) for w in widths] +
                          [_full((3, D)), _full((1, D))] + [ANY] * n_g,
                 out_specs=(_full((D, sum(widths))), *([ANY] * n_g)),
                 scratch_shapes=[pltpu.VMEM((D, sum(widths)), F32)] + (_sem_scratch(items) if n_g else []),
                 compiler_params=_params(("arbitrary",)))(x, *dparts, mod3, norm_pre, *gathered)
    return out if n_g else out[0]


def _adamw_math(w, g, m, v):
    m = ADAM_B1 * m + (1.0 - ADAM_B1) * g
    v = ADAM_B2 * v + (1.0 - ADAM_B2) * (g * g)
    m_hat = m / (1.0 - ADAM_B1 ** ADAM_STEP)
    v_hat = v / (1.0 - ADAM_B2 ** ADAM_STEP)
    delta = -ADAM_LR * (m_hat / (jnp.sqrt(v_hat) + ADAM_EPS) + ADAM_WD * w)
    return delta, m, v


def _sum_sources(ref):
    g = ref[0].astype(F32)
    for s in range(1, N_DEV):
        g = g + ref[s].astype(F32)
    return g


def _adamw_reduce(name, parts, w, m, v):
    r, c = w.shape
    tr = r if r * c <= 256 * 1024 else max(8, (256 * 1024 // c) // 8 * 8)
    while r % tr:
        tr -= 8

    def body(p_ref, w_ref, m_ref, v_ref, g_ref, d_ref, nm_ref, nv_ref):
        g = _sum_sources(p_ref)
        g_ref[...] = g
        d_ref[...], nm_ref[...], nv_ref[...] = _adamw_math(w_ref[...], g, m_ref[...], v_ref[...])

    blk = pl.BlockSpec((tr, c), lambda i: (i, 0))
    return _pcall(body, name=name, grid=(r // tr,),
                  out_shape=tuple([jax.ShapeDtypeStruct((r, c), F32)] * 4),
                  in_specs=[pl.BlockSpec((N_DEV, tr, c), lambda i: (0, i, 0)), blk, blk, blk],
                  out_specs=(blk, blk, blk, blk),
                  compiler_params=_params(("arbitrary",)))(parts, w, m, v)


def _adamw_small(gs, ws, ms, vs):
    n = len(gs)

    def body(*refs):
        ins, outs = refs[:4 * n], refs[4 * n:]
        for t in range(n):
            g_ref, w_ref, m_ref, v_ref = ins[4 * t:4 * t + 4]
            outs[3 * t][...], outs[3 * t + 1][...], outs[3 * t + 2][...] = _adamw_math(
                w_ref[...], g_ref[...], m_ref[...], v_ref[...])

    vm = pl.BlockSpec(memory_space=pltpu.VMEM)
    flat = [a for t in range(n) for a in (gs[t], ws[t], ms[t], vs[t])]
    return _pcall(body, name="adamw_small",
                  out_shape=tuple(jax.ShapeDtypeStruct(w.shape, F32) for w in ws for _ in range(3)),
                  in_specs=[vm] * (4 * n), out_specs=tuple([vm] * (3 * n)), compiler_params=_params())(*flat)


def _sum_small(parts):
    n = len(parts)

    def body(*refs):
        for t in range(n):
            refs[n + t][...] = _sum_sources(refs[t])

    vm = pl.BlockSpec(memory_space=pltpu.VMEM)
    return _pcall(body, name="sum_small",
                  out_shape=tuple(jax.ShapeDtypeStruct(p.shape[1:], F32) for p in parts),
                  in_specs=[vm] * n, out_specs=tuple([vm] * n), compiler_params=_params())(*parts)


def _ada_update(c_all, dmod_cols, w, m, v):
    def body(c_ref, dm_ref, w_ref, m_ref, v_ref, g_ref, d_ref, nm_ref, nv_ref):
        ca = c_ref[...]
        g = lax.dot_general(ca * jax.nn.sigmoid(ca), dm_ref[...], (((0,), (0,)), ((), ())),
                            preferred_element_type=F32, precision=lax.Precision.HIGHEST)
        g_ref[...] = g
        d_ref[...], nm_ref[...], nv_ref[...] = _adamw_math(w_ref[...], g, m_ref[...], v_ref[...])

    vm = pl.BlockSpec(memory_space=pltpu.VMEM)
    return _pcall(body, name="ada_update", out_shape=tuple([jax.ShapeDtypeStruct(w.shape, F32)] * 4),
                  in_specs=[vm] * 5, out_specs=(vm, vm, vm, vm), compiler_params=_params())(c_all, dmod_cols, w, m, v)


def kernel(x, c, w_ada, b_ada, norm_pre, norm_post, w_in, pool_w, pool_scale, ssm_a_re, ssm_a_im, ssm_log_dt, ssm_b_re, ssm_b_im, ssm_c_re, ssm_c_im, ssm_d, glu_w, glu_b, w_branch_pool, w_branch_ssm, w_out, loss_target, m_w_ada, m_b_ada, m_norm_pre, m_norm_post, m_w_in, m_pool_w, m_pool_scale, m_ssm_a_re, m_ssm_a_im, m_ssm_log_dt, m_ssm_b_re, m_ssm_b_im, m_ssm_c_re, m_ssm_c_im, m_ssm_d, m_glu_w, m_glu_b, m_w_branch_pool, m_w_branch_ssm, m_w_out, v_w_ada, v_b_ada, v_norm_pre, v_norm_post, v_w_in, v_pool_w, v_pool_scale, v_ssm_a_re, v_ssm_a_im, v_ssm_log_dt, v_ssm_b_re, v_ssm_b_im, v_ssm_c_re, v_ssm_c_im, v_ssm_d, v_glu_w, v_glu_b, v_w_branch_pool, v_w_branch_ssm, v_w_out):
    given = dict(locals())
    me = _flat(_me())
    rows = x.shape[1]
    x2 = x[0]
    target = loss_target[0]
    ada_cols = w_ada.shape[2]

    tb_ssm = _tb(rows, 256)
    k_steps = tb_ssm // SUBLANES
    a_re, a_im = ssm_a_re[0], ssm_a_im[0]
    log_dt = ssm_log_dt[0].reshape(GROUPS, 1)
    b_re_t, b_im_t = ssm_b_re[0].transpose(0, 2, 1), ssm_b_im[0].transpose(0, 2, 1)
    s5_params = [a_re, a_im, log_dt, b_re_t, b_im_t, ssm_c_re[0], ssm_c_im[0]]

    f32_shards = [w_in[0], pool_w[0], glu_w[0], w_branch_pool[0], w_branch_ssm[0], w_out[0]]
    n_sh = len(f32_shards)

    def local_work(ins, outs):
        for src, dst in zip(ins[:n_sh], outs[:n_sh]):
            dst[...] = src[...].astype(BF16)
        _s5_prep_body(*ins[n_sh:], *outs[n_sh:])

    b_ada_s = lax.dynamic_slice(b_ada, (0, me * ada_cols), (1, ada_cols))
    c_all, mod_rows, *local = _ada_exchange(
        c, w_ada[0], b_ada_s, f32_shards + s5_params,
        [jax.ShapeDtypeStruct(a.shape, BF16) for a in f32_shards] + list(_s5_prep_structs(k_steps)), local_work)
    mod3 = mod_rows.reshape(3, D)
    shards, (wb, wct, pow_re, pow_im) = local[:n_sh], local[n_sh:]
    ptab = _state_layout(pow_re, pow_im)
    dvec = ssm_d[0].reshape(1, D)
    pm = _perm_matrix(tb_ssm)
    pmt = pm.T

    proj, w_in_g, pool_w_g, glu_g = _in_proj(x2, mod3, norm_pre, shards[0], shards[1:3])
    y_pool, pooled = _pool_fwd(proj, pool_w_g, pool_scale)
    y_ssm, ys_pre, carries, states, glu_gate, z_perm, u_perm, wbp_g, wbs_g, wout_g = _ssm_fwd(
        proj, pm, pmt, wb, wct, ptab, dvec, glu_g, glu_b, shards[3:])
    loss_part, dy, dyp, dys, dpg, dwbp, dwbs, dwout, head_vec = _head(
        x2, target, proj, y_pool, y_ssm, mod3, norm_post, wbp_g, wbs_g, wout_g)

    dpp, dpool_w, dpool_scale = _pool_bwd(dyp, pooled, proj, pool_w_g, pool_scale)
    dw_in_rest = _in_proj_bwd_w("in_proj_bwd_w_rest", x2, [dpp, dpg], mod3, norm_pre)
    dy_pre, dzs, dglu_w, dglu_b = _glu_bwd(dys, z_perm, ys_pre, glu_gate, pm, pmt, glu_g)
    dus, dbb, dcc, dabar, dd, p_glu, p_wbp, p_wbs, p_wout, p_pool_w, p_w_in = _ssm_bwd(
        dy_pre, u_perm, states, carries, pmt, wb, wct, ptab, dvec, [dglu_w, dwbp, dwbs, dwout], dpool_w,
        dw_in_rest)

    small32 = jnp.concatenate([head_vec, dpool_scale, dglu_b, dd, jnp.broadcast_to(loss_part, (1, D)),
                               jnp.zeros((2, D), F32), dabar.reshape(8, D)], axis=0)
    small16 = jnp.concatenate([dbb.reshape(2 * GROUPS, D), dcc.reshape(2 * GROUPS, D)], axis=0).astype(BF16)
    dw_in_ssm, p_small32, p_small16 = _in_proj_bwd_w("in_proj_bwd_w_ssm", x2, [dus, dzs], mod3, norm_pre,
                                                     gathered=(small32, small16))
    grad_x, p_w_in, p_pre = _in_proj_bwd_x(x2, dy, dpp, dus, dzs, dpg, mod3, norm_pre, w_in_g, dw_in_ssm, p_w_in)

    tot32, tot16, tot_pre = _sum_small([p_small32, p_small16, p_pre])
    d_abar_re, d_abar_im = _state_unlayout(tot32[8:16].reshape(N_STATE))
    d_bb_re, d_bb_im = tot16[0:64].reshape(GROUPS, G_H, G_P), tot16[64:128].reshape(GROUPS, G_H, G_P)
    g_a_re, g_a_im, g_log_dt, g_b_re_t, g_b_im_t = _s5_prep_bwd(
        a_re, a_im, log_dt, b_re_t, b_im_t, d_abar_re, d_abar_im, d_bb_re, d_bb_im)

    grads, deltas, new_m, new_v = {}, {}, {}, {}

    small = []

    def small_update(name, g2):
        small.append((name, g2))

    def shard_update(name, parts):
        shape = given[name].shape
        r2 = parts.shape[1:] if parts.ndim == 3 else (parts.shape[1] * parts.shape[2], parts.shape[3])
        w2, m2, v2 = (given[p + name].reshape(r2) for p in ("", "m_", "v_"))
        out = _adamw_reduce("adamw_" + name, parts.reshape((N_DEV,) + tuple(r2)), w2, m2, v2)
        grads[name], deltas[name], new_m[name], new_v[name] = (a.reshape(shape) for a in out)

    dmod_all = jnp.concatenate([p_pre[:, 0:2, :], p_small32[:, 0:1, :]], axis=1).reshape(N_DEV, 3 * D)
    dmod_cols = lax.dynamic_slice(dmod_all, (0, me * ada_cols), (N_DEV, ada_cols))
    out = _ada_update(c_all, dmod_cols, w_ada[0], m_w_ada[0], v_w_ada[0])
    grads['w_ada'], deltas['w_ada'], new_m['w_ada'], new_v['w_ada'] = (a.reshape(w_ada.shape) for a in out)

    small_update('b_ada', jnp.concatenate([tot_pre[0:2], tot32[0:1]], axis=0).reshape(1, 3 * D))
    small_update('norm_pre', tot_pre[2:3])
    small_update('norm_post', tot32[1:2])
    small_update('pool_scale', tot32[2:3])
    small_update('glu_b', tot32[3:4])
    small_update('ssm_d', tot32[4:5])
    small_update('ssm_a_re', g_a_re)
    small_update('ssm_a_im', g_a_im)
    small_update('ssm_log_dt', g_log_dt.reshape(1, GROUPS))
    small_update('ssm_b_re', g_b_re_t.transpose(0, 2, 1).reshape(GROUPS, G_P * G_H))
    small_update('ssm_b_im', g_b_im_t.transpose(0, 2, 1).reshape(GROUPS, G_P * G_H))
    small_update('ssm_c_re', tot16[128:192])
    small_update('ssm_c_im', -tot16[192:256])
    flat = _adamw_small([g2 for _, g2 in small],
                        *[[given[p + name].reshape(g2.shape) for name, g2 in small] for p in ("", "m_", "v_")])
    for t, (name, g2) in enumerate(small):
        shape = given[name].shape
        grads[name], deltas[name], new_m[name], new_v[name] = (
            a.reshape(shape) for a in (g2, *flat[3 * t:3 * t + 3]))
    shard_update('w_in', p_w_in)
    shard_update('pool_w', p_pool_w)
    shard_update('glu_w', p_glu)
    shard_update('w_branch_pool', p_wbp)
    shard_update('w_branch_ssm', p_wbs)
    shard_update('w_out', p_wout)

    return (tot32[5, 0], grad_x[None], *[grads[n] for n in WEIGHTS], *[deltas[n] for n in WEIGHTS],
            *[new_m[n] for n in WEIGHTS], *[new_v[n] for n in WEIGHTS])
```

```python
import math
from typing import Callable, NamedTuple, Optional

import jax
import jax.numpy as jnp
from jax import lax
from jax.experimental import pallas as pl
from jax.experimental.pallas import tpu as pltpu

F32 = jnp.float32
BF16 = jnp.bfloat16
MESH = pl.DeviceIdType.MESH

D = 1024
N_DEV = 8
N_IN = 6 * D
GROUPS = 64
G_H = 16
G_P = 64
N_Q = 4
Q_W = 2 * 16 * G_P
N_STATE = N_Q * Q_W
POOL_WINDOWS = (2, 4, 8, 16)
HALO = 16
RMS_EPS = 1e-6
SUBLANES = 8
LANE_CHUNK = 512
SCAN_UNROLL = 2
VMEM_LIMIT = 56 * 1024 * 1024

ADAM_LR = 0.001
ADAM_B1 = 0.9
ADAM_B2 = 0.999
ADAM_EPS = 1e-08
ADAM_WD = 0.01
ADAM_STEP = 10

WEIGHTS = ['w_ada', 'b_ada', 'norm_pre', 'norm_post', 'w_in', 'pool_w', 'pool_scale', 'ssm_a_re',
           'ssm_a_im', 'ssm_log_dt', 'ssm_b_re', 'ssm_b_im', 'ssm_c_re', 'ssm_c_im', 'ssm_d', 'glu_w',
           'glu_b', 'w_branch_pool', 'w_branch_ssm', 'w_out']


def _pcall(body, **kw):
    return pl.pallas_call(body, **kw)


def _params(sem=None, vmem=VMEM_LIMIT):
    return pltpu.CompilerParams(dimension_semantics=sem, vmem_limit_bytes=vmem)


def _tb(rows, pref):
    return pref if rows % pref == 0 and rows // pref >= 2 else rows // 2


def _full(shape, single=False):
    nd = len(shape)
    if single:
        return pl.BlockSpec(shape, lambda i: (0,) * nd, pipeline_mode=pl.Buffered(1))
    return pl.BlockSpec(shape, lambda i: (0,) * nd)


ANY = pl.BlockSpec(memory_space=pl.ANY)


def _me():
    return lax.axis_index("x"), lax.axis_index("y"), lax.axis_index("c")


def _flat(p):
    return 4 * p[0] + 2 * p[1] + p[2]


def _peer(k):
    x, y, c = _me()
    return (1 - x if k & 4 else x, 1 - y if k & 2 else y, 1 - c if k & 1 else c)


def _silu_parts(z):
    s = jax.nn.sigmoid(z)
    return z * s, s * (1.0 + z * (1.0 - s))


_GELU_C = math.sqrt(2.0 / math.pi)


def _gelu_parts(x):
    x2 = x * x
    t = jnp.tanh(_GELU_C * (x + 0.044715 * x * x2))
    g = 0.5 * x * (1.0 + t)
    dg = 0.5 * (1.0 + t) + 0.5 * x * (1.0 - t * t) * (_GELU_C * (1.0 + 3.0 * 0.044715 * x2))
    return g, dg


def _dot(a, b):
    return jnp.dot(a, b, preferred_element_type=F32)


def _dot_nt(a, b):
    return lax.dot_general(a, b, (((1,), (1,)), ((), ())), preferred_element_type=F32)


def _dot_tn(a, b):
    return lax.dot_general(a, b, (((0,), (0,)), ((), ())), preferred_element_type=F32)


def _rms_parts(x):
    r = lax.rsqrt(jnp.mean(x * x, axis=-1, keepdims=True) + RMS_EPS)
    return x * r, r


def _rms_bwd(dxn, xn, r):
    return r * (dxn - xn * jnp.mean(dxn * xn, axis=-1, keepdims=True))


def _ada_exchange(c, w_ada_s, b_ada_s, local_ins, local_outs, local_work):
    cols = w_ada_s.shape[1]
    n_li, n_lo = len(local_ins), len(local_outs)

    def body(c_ref, w_ref, b_ref, *rest):
        li_refs, call_ref, mod_ref = rest[:n_li], rest[n_li], rest[n_li + 1]
        lo_refs, (part_ref, ssem, rsem, lsem) = rest[n_li + 2:n_li + 2 + n_lo], rest[n_li + 2 + n_lo:]
        me3 = _me()
        me = _flat(me3)
        mine = pltpu.make_async_copy(c_ref, call_ref.at[pl.ds(me, 1), :], lsem.at[0])
        mine.start()
        sends = []
        for k in range(1, N_DEV):
            cp = pltpu.make_async_remote_copy(src_ref=c_ref, dst_ref=call_ref.at[pl.ds(me, 1), :],
                                              send_sem=ssem.at[k - 1], recv_sem=rsem.at[k - 1],
                                              device_id=_peer(k), device_id_type=MESH)
            cp.start()
            sends.append(cp)
        local_work(li_refs, lo_refs)
        mine.wait()
        for k in range(1, N_DEV):
            p = _flat(_peer(k))
            pltpu.make_async_remote_copy(src_ref=c_ref, dst_ref=call_ref.at[pl.ds(p, 1), :],
                                         send_sem=ssem.at[k - 1], recv_sem=rsem.at[k - 1],
                                         device_id=_peer(k), device_id_type=MESH).wait_recv()
        for cp in sends:
            cp.wait_send()
        ca = call_ref[...]
        act = ca * jax.nn.sigmoid(ca)
        part_ref[...] = jnp.dot(act, w_ref[...], preferred_element_type=F32,
                                precision=lax.Precision.HIGHEST) + b_ref[...]
        own = pltpu.make_async_copy(part_ref.at[pl.ds(me, 1), :], mod_ref.at[pl.ds(me, 1), :], lsem.at[1])
        own.start()
        sends = []
        for k in range(1, N_DEV):
            p = _flat(_peer(k))
            s = N_DEV - 1 + k - 1
            cp = pltpu.make_async_remote_copy(src_ref=part_ref.at[pl.ds(p, 1), :],
                                              dst_ref=mod_ref.at[pl.ds(me, 1), :],
                                              send_sem=ssem.at[s], recv_sem=rsem.at[s],
                                              device_id=_peer(k), device_id_type=MESH)
            cp.start()
            sends.append(cp)
        own.wait()
        for k in range(1, N_DEV):
            p = _flat(_peer(k))
            s = N_DEV - 1 + k - 1
            pltpu.make_async_remote_copy(src_ref=part_ref.at[pl.ds(p, 1), :],
                                         dst_ref=mod_ref.at[pl.ds(p, 1), :],
                                         send_sem=ssem.at[s], recv_sem=rsem.at[s],
                                         device_id=_peer(k), device_id_type=MESH).wait_recv()
        for cp in sends:
            cp.wait_send()

    vm = pl.BlockSpec(memory_space=pltpu.VMEM)
    return _pcall(
        body, name="ada_exchange",
        out_shape=(jax.ShapeDtypeStruct((N_DEV, D), F32), jax.ShapeDtypeStruct((N_DEV, cols), F32), *local_outs),
        in_specs=[vm] * (3 + n_li), out_specs=tuple([vm] * (2 + n_lo)),
        scratch_shapes=[pltpu.VMEM((N_DEV, cols), F32),
                        pltpu.SemaphoreType.DMA((2 * (N_DEV - 1),)),
                        pltpu.SemaphoreType.DMA((2 * (N_DEV - 1),)),
                        pltpu.SemaphoreType.DMA((2,))],
        compiler_params=_params(),
    )(c, w_ada_s, b_ada_s, *local_ins)


class _Item(NamedTuple):
    src: int
    out: int
    src_view: Callable
    dst_view: Callable
    pred: Optional[Callable] = None


def _when(pred, dest, fn):
    if pred is None:
        fn()
    else:
        pl.when(pred(dest))(fn)


def _n_sems(items):
    return len(items) * (N_DEV - 1)


def _hosted_copies(items, srcs, outs, ssem, rsem, lsem, act):
    me = _flat(_me())
    for t, it in enumerate(items):
        local = lambda t=t, it=it: pltpu.make_async_copy(
            it.src_view(srcs[it.src], me), it.dst_view(outs[it.out], me), lsem.at[t])
        if act == "start":
            _when(it.pred, me, lambda local=local: local().start())
        else:
            _when(it.pred, me, lambda local=local: local().wait())
    for k in range(1, N_DEV):
        p3 = _peer(k)
        p = _flat(p3)
        for t, it in enumerate(items):
            s = t * (N_DEV - 1) + k - 1
            send = lambda it=it, s=s, p=p, p3=p3: pltpu.make_async_remote_copy(
                src_ref=it.src_view(srcs[it.src], p), dst_ref=it.dst_view(outs[it.out], me),
                send_sem=ssem.at[s], recv_sem=rsem.at[s], device_id=p3, device_id_type=MESH)
            recv = lambda it=it, s=s, p=p, p3=p3: pltpu.make_async_remote_copy(
                src_ref=it.src_view(srcs[it.src], p), dst_ref=it.dst_view(outs[it.out], p),
                send_sem=ssem.at[s], recv_sem=rsem.at[s], device_id=p3, device_id_type=MESH)
            if act == "start":
                _when(it.pred, p, lambda send=send: send().start())
            else:
                _when(it.pred, me, lambda recv=recv: recv().wait_recv())
                _when(it.pred, p, lambda send=send: send().wait_send())


def _sem_scratch(items):
    return [pltpu.SemaphoreType.DMA((_n_sems(items),)), pltpu.SemaphoreType.DMA((_n_sems(items),)),
            pltpu.SemaphoreType.DMA((len(items),))]


def _whole(ref, dest):
    return ref


def _slot(ref, sender):
    return ref.at[sender]


def _rows_of(rows):
    return lambda ref, dev: ref.at[pl.ds(dev * rows, rows), :]


def _pool_rows_of(rows):
    return lambda ref, dev: ref.at[:, pl.ds(dev * rows, rows), :]


def _gather_item(src, out, dst_view):
    return _Item(src, out, _whole, dst_view)


def _scatter_item(src, out, src_view):
    return _Item(src, out, src_view, _slot)


W_IN_BLOCK = 256
W_IN_SHARD = N_IN // N_DEV
SSM_BLOCKS = (2 * D // W_IN_BLOCK, 4 * D // W_IN_BLOCK)


def _w_in_block_item(src, out, j, ssm_part):
    def block(dest):
        return (W_IN_SHARD // W_IN_BLOCK) * dest + j

    def in_ssm(dest):
        b = block(dest)
        return (b >= SSM_BLOCKS[0]) & (b < SSM_BLOCKS[1])

    def src_view(ref, dest):
        b = block(dest)
        local = b - SSM_BLOCKS[0] if ssm_part else jnp.where(b < SSM_BLOCKS[0], b, b - (SSM_BLOCKS[1] - SSM_BLOCKS[0]))
        local = jnp.clip(local, 0, ref.shape[1] // W_IN_BLOCK - 1)
        return ref.at[:, pl.ds(local * W_IN_BLOCK, W_IN_BLOCK)]

    def dst_view(ref, sender):
        return ref.at[sender, :, pl.ds(j * W_IN_BLOCK, W_IN_BLOCK)]

    pred = in_ssm if ssm_part else (lambda dest: jnp.logical_not(in_ssm(dest)))
    return _Item(src, out, src_view, dst_view, pred)


def _s5_discretise(a_re, a_im, log_dt, b_re_t, b_im_t):
    dt = jnp.exp(log_dt)
    lam_re = jnp.minimum(a_re, -1e-4)
    lam_im = a_im
    mag = jnp.exp(lam_re * dt)
    abar_re = mag * jnp.cos(lam_im * dt)
    abar_im = mag * jnp.sin(lam_im * dt)
    den = lam_re * lam_re + lam_im * lam_im
    num_re = abar_re - 1.0
    f_re = (num_re * lam_re + abar_im * lam_im) / den
    f_im = (abar_im * lam_re - num_re * lam_im) / den
    f_re, f_im = f_re[:, None, :], f_im[:, None, :]
    bb_re = f_re * b_re_t - f_im * b_im_t
    bb_im = f_re * b_im_t + f_im * b_re_t
    return abar_re, abar_im, bb_re, bb_im


def _group_masks():
    spread = lax.broadcasted_iota(jnp.int32, (G_P, 16 * G_P), 1) % G_P == lax.broadcasted_iota(
        jnp.int32, (G_P, 16 * G_P), 0)
    own = lax.broadcasted_iota(jnp.int32, (16 * G_H, 16 * G_P), 0) // G_H == lax.broadcasted_iota(
        jnp.int32, (16 * G_H, 16 * G_P), 1) // G_P
    return spread, own


def _s5_prep_structs(n_pow):
    return (jax.ShapeDtypeStruct((N_Q, 16 * G_H, Q_W), BF16), jax.ShapeDtypeStruct((N_Q, 16 * G_H, Q_W), BF16),
            jax.ShapeDtypeStruct((n_pow, GROUPS, G_P), F32), jax.ShapeDtypeStruct((n_pow, GROUPS, G_P), F32))


def _s5_prep_body(ar_ref, ai_ref, ld_ref, br_ref, bi_ref, cr_ref, ci_ref, wb_ref, wct_ref, pr_ref, pi_ref):
    abar_re, abar_im, bb_re, bb_im = _s5_discretise(ar_ref[...], ai_ref[...], ld_ref[...], br_ref[...], bi_ref[...])
    spread, own = _group_masks()
    spread = spread.astype(BF16)
    for ref, parts in ((wb_ref, (bb_re, bb_im)), (wct_ref, (cr_ref[...], -ci_ref[...]))):
        for half, t in enumerate(parts):
            for q in range(N_Q):
                blocks = t[q * 16:(q + 1) * 16].reshape(16 * G_H, G_P).astype(BF16)
                dense = jnp.where(own, _dot(blocks, spread), 0.0)
                ref[q, :, half * (Q_W // 2):(half + 1) * (Q_W // 2)] = dense.astype(BF16)
    p_re, p_im = abar_re, abar_im
    pr_ref[0] = p_re
    pi_ref[0] = p_im
    for k in range(1, pr_ref.shape[0]):
        p_re, p_im = p_re * abar_re - p_im * abar_im, p_re * abar_im + p_im * abar_re
        pr_ref[k] = p_re
        pi_ref[k] = p_im


def _s5_prep_bwd(a_re, a_im, log_dt, b_re_t, b_im_t, d_abar_re, d_abar_im, d_bb_re, d_bb_im):
    def body(ar_ref, ai_ref, ld_ref, br_ref, bi_ref, dar_ref, dai_ref, dbr_ref, dbi_ref,
             gar_ref, gai_ref, gld_ref, gbr_ref, gbi_ref):
        _, vjp = jax.vjp(_s5_discretise, ar_ref[...], ai_ref[...], ld_ref[...], br_ref[...], bi_ref[...])
        g = vjp((dar_ref[...], dai_ref[...], dbr_ref[...], dbi_ref[...]))
        gar_ref[...] = g[0]
        gai_ref[...] = g[1]
        gld_ref[...] = g[2]
        gbr_ref[...] = g[3]
        gbi_ref[...] = g[4]

    vm = pl.BlockSpec(memory_space=pltpu.VMEM)
    ins = (a_re, a_im, log_dt, b_re_t, b_im_t)
    return _pcall(body, name="s5_prep_bwd",
                  out_shape=tuple(jax.ShapeDtypeStruct(a.shape, F32) for a in ins),
                  in_specs=[vm] * 9, out_specs=tuple([vm] * 5), compiler_params=_params(),
                  )(*ins, d_abar_re, d_abar_im, d_bb_re, d_bb_im)


def _state_layout(re, im):
    lead = re.shape[:-2]
    r = re.reshape(lead + (N_Q, 1, 16 * G_P))
    i = im.reshape(lead + (N_Q, 1, 16 * G_P))
    return jnp.concatenate([r, i], axis=-2).reshape(lead + (N_STATE,))


def _state_unlayout(v):
    v4 = v.reshape(N_Q, 2, 16, G_P)
    return v4[:, 0].reshape(GROUPS, G_P), v4[:, 1].reshape(GROUPS, G_P)


def _perm_matrix(tb):
    k_steps = tb // SUBLANES
    r = jnp.arange(tb)
    src = (r % SUBLANES) * k_steps + r // SUBLANES
    return (src[:, None] == jnp.arange(tb)[None, :]).astype(BF16)


def _lane_chunks(q):
    for lc in range(Q_W // 2 // LANE_CHUNK):
        re = q * Q_W + lc * LANE_CHUNK
        yield re, re + Q_W // 2


def _steps(lo, hi, body, init):
    if hi - lo <= SCAN_UNROLL:
        for k in range(lo, hi):
            init = body(k, init)
        return init
    trips = (hi - lo) // SCAN_UNROLL

    def trip(j, carry):
        for u in range(SCAN_UNROLL):
            carry = body(lo + j * SCAN_UNROLL + u, carry)
        return carry

    carry = lax.fori_loop(0, trips, trip, init)
    for k in range(lo + trips * SCAN_UNROLL, hi):
        carry = body(k, carry)
    return carry


def _tile(k):
    if isinstance(k, int):
        return pl.ds(k * SUBLANES, SUBLANES)
    return pl.ds(pl.multiple_of(k * SUBLANES, SUBLANES), SUBLANES)


def _drive(dst_ref, q, chan, w_ref):
    half = Q_W // 2
    for m in range(2):
        rows = slice(m * 128, (m + 1) * 128)
        lhs = chan[:, rows]
        for part in range(2):
            lo = part * half + m * (half // 2)
            dst_ref[:, q * Q_W + lo:q * Q_W + lo + half // 2] = _dot(lhs, w_ref[q, rows, lo:lo + half // 2])


def _scan_forward(q, s_ref, p_ref, carry_ref, enter_ref, fin_ref, k_steps):
    for re, im in _lane_chunks(q):
        lr, li = pl.ds(re, LANE_CHUNK), pl.ds(im, LANE_CHUNK)
        a_re = jnp.broadcast_to(p_ref[0:1, lr], (SUBLANES, LANE_CHUNK))
        a_im = jnp.broadcast_to(p_ref[0:1, li], (SUBLANES, LANE_CHUNK))

        def local(k, st):
            sr, si = st
            rows = _tile(k)
            nr = a_re * sr - a_im * si + s_ref[rows, lr]
            ni = a_re * si + a_im * sr + s_ref[rows, li]
            s_ref[rows, lr] = nr
            s_ref[rows, li] = ni
            return nr, ni

        zero = jnp.zeros((SUBLANES, LANE_CHUNK), F32)
        fr, fi = _steps(0, k_steps, local, (zero, zero))
        fin_ref[:, lr] = fr
        fin_ref[:, li] = fi
        ak_re, ak_im = p_ref[k_steps - 1:k_steps, lr], p_ref[k_steps - 1:k_steps, li]
        c_re, c_im = carry_ref[:, lr], carry_ref[:, li]
        for seg in range(SUBLANES):
            enter_ref[seg:seg + 1, lr] = c_re
            enter_ref[seg:seg + 1, li] = c_im
            f_re, f_im = fin_ref[seg:seg + 1, lr], fin_ref[seg:seg + 1, li]
            c_re, c_im = f_re + ak_re * c_re - ak_im * c_im, f_im + ak_re * c_im + ak_im * c_re
        carry_ref[:, lr] = c_re
        carry_ref[:, li] = c_im
        e_re, e_im = enter_ref[:, lr], enter_ref[:, li]

        def fix(k, _):
            rows = _tile(k)
            p_re = p_ref[pl.ds(k, 1), lr]
            p_im = p_ref[pl.ds(k, 1), li]
            s_ref[rows, lr] = s_ref[rows, lr] + (p_re * e_re - p_im * e_im)
            s_ref[rows, li] = s_ref[rows, li] + (p_re * e_im + p_im * e_re)
            return 0

        _steps(0, k_steps, fix, 0)


def _scan_backward(q, g_ref, s_ref, p_ref, carry_ref, s_in_ref, fin_ref, da_ref, k_steps):
    seg_id = lax.broadcasted_iota(jnp.int32, (SUBLANES, LANE_CHUNK), 0)
    for re, im in _lane_chunks(q):
        lr, li = pl.ds(re, LANE_CHUNK), pl.ds(im, LANE_CHUNK)
        a_re = jnp.broadcast_to(p_ref[0:1, lr], (SUBLANES, LANE_CHUNK))
        a_im = jnp.broadcast_to(p_ref[0:1, li], (SUBLANES, LANE_CHUNK))

        def local(j, st):
            sr, si = st
            rows = _tile(k_steps - 1 - j)
            nr = a_re * sr + a_im * si + g_ref[rows, lr]
            ni = a_re * si - a_im * sr + g_ref[rows, li]
            g_ref[rows, lr] = nr
            g_ref[rows, li] = ni
            return nr, ni

        zero = jnp.zeros((SUBLANES, LANE_CHUNK), F32)
        fr, fi = _steps(0, k_steps, local, (zero, zero))
        fin_ref[:, lr] = fr
        fin_ref[:, li] = fi
        ak_re, ak_im = p_ref[k_steps - 1:k_steps, lr], p_ref[k_steps - 1:k_steps, li]
        c_re, c_im = carry_ref[:, lr], carry_ref[:, li]
        lam_in = [None] * SUBLANES
        for seg in reversed(range(SUBLANES)):
            lam_in[seg] = (c_re, c_im)
            f_re, f_im = fin_ref[seg:seg + 1, lr], fin_ref[seg:seg + 1, li]
            c_re, c_im = f_re + ak_re * c_re + ak_im * c_im, f_im + ak_re * c_im - ak_im * c_re
        carry_ref[:, lr] = c_re
        carry_ref[:, li] = c_im
        for seg in range(SUBLANES):
            fin_ref[seg:seg + 1, lr] = lam_in[seg][0]
            fin_ref[seg:seg + 1, li] = lam_in[seg][1]
        e_re, e_im = fin_ref[:, lr], fin_ref[:, li]

        def fix_with(k, acc, sp_re, sp_im):
            acc_re, acc_im = acc
            rows = _tile(k)
            p_re = p_ref[pl.ds(k_steps - 1 - k, 1), lr]
            p_im = p_ref[pl.ds(k_steps - 1 - k, 1), li]
            l_re = g_ref[rows, lr] + (p_re * e_re + p_im * e_im)
            l_im = g_ref[rows, li] + (p_re * e_im - p_im * e_re)
            g_ref[rows, lr] = l_re
            g_ref[rows, li] = l_im
            return acc_re + (l_re * sp_re + l_im * sp_im), acc_im + (l_im * sp_re - l_re * sp_im)

        def fix(k, acc):
            prev = _tile(k - 1)
            return fix_with(k, acc, s_ref[prev, lr], s_ref[prev, li])

        last = _tile(k_steps - 1)
        before_re = jnp.where(seg_id == 0, s_in_ref[:, lr], pltpu.roll(s_ref[last, lr], 1, axis=0))
        before_im = jnp.where(seg_id == 0, s_in_ref[:, li], pltpu.roll(s_ref[last, li], 1, axis=0))
        acc = fix_with(0, (zero, zero), before_re, before_im)
        acc_re, acc_im = _steps(1, k_steps, fix, acc)
        da_ref[:, lr] = da_ref[:, lr] + jnp.sum(acc_re, axis=0, keepdims=True)
        da_ref[:, li] = da_ref[:, li] + jnp.sum(acc_im, axis=0, keepdims=True)


def _prenorm(x, mod3, norm_pre):
    xn, r = _rms_parts(x)
    return xn, r, xn * norm_pre * (1.0 + mod3[1:2, :]) + mod3[0:1, :]


CHIP_FLIPS = (4, 2, 6)


def _shard_order(me):
    flips = [0, 1] + [f + c for f in CHIP_FLIPS for c in (0, 1)]
    return jnp.stack([me ^ f for f in flips]).astype(jnp.int32)


def _in_proj(x, mod3, norm_pre, w_in_s, shards):
    rows = x.shape[0]
    tb = _tb(rows, 2048)
    nblk = rows // tb
    n_sh = len(shards)
    last_step = N_DEV - 1
    items = [_gather_item(0, 0, _pool_rows_of(shards[0].shape[1]))] + \
            [_gather_item(t, t, _rows_of(shards[t].shape[0])) for t in range(1, n_sh)]

    def body(order_ref, x_ref, mod_ref, np_ref, w_src, *rest):
        src_refs, proj_ref, w_full, out_refs = rest[:n_sh], rest[n_sh], rest[n_sh + 1], rest[n_sh + 2:2 * n_sh + 2]
        h_scr, wg, ssem, rsem, lsem, *sems = rest[2 * n_sh + 2:]
        s, i = pl.program_id(0), pl.program_id(1)
        me3 = _me()
        me = _flat(me3)
        sibling = _peer(1)

        def own_copy(slot, k):
            return pltpu.make_async_remote_copy(src_ref=w_src, dst_ref=wg.at[me], send_sem=ssem.at[slot],
                                                recv_sem=rsem.at[slot], device_id=_peer(k), device_id_type=MESH)

        def passed_copy(j):
            p = _flat(_peer(CHIP_FLIPS[j]))
            return pltpu.make_async_remote_copy(src_ref=wg.at[p], dst_ref=wg.at[p], send_sem=ssem.at[4 + j],
                                                recv_sem=rsem.at[4 + j], device_id=sibling, device_id_type=MESH)

        def arrival(slot, flip):
            p = _flat(_peer(flip))
            pltpu.make_async_remote_copy(src_ref=w_src, dst_ref=wg.at[p], send_sem=ssem.at[slot],
                                         recv_sem=rsem.at[slot], device_id=sibling, device_id_type=MESH).wait_recv()

        def keep(t):
            p = order_ref[t]
            return pltpu.make_async_copy(wg.at[p], w_full.at[:, pl.ds(p * W_IN_SHARD, W_IN_SHARD)], lsem.at[1 + t])

        first = i == 0
        for t in range(last_step):
            pl.when(first & (s == t + 1))(lambda t=t: keep(t).start())

        @pl.when(first & (s == 0))
        def _():
            mine = pltpu.make_async_copy(w_src, wg.at[me], lsem.at[0])
            mine.start()
            own_copy(0, 1).start()
            for j, f in enumerate(CHIP_FLIPS[:2]):
                own_copy(1 + j, f).start()
            mine.wait()

        @pl.when(first & (s == 1))
        def _():
            arrival(0, 1)

        for j, f in enumerate(CHIP_FLIPS):
            @pl.when(first & (s == 2 + 2 * j))
            def _(j=j, f=f):
                arrival(1 + j, f)
                passed_copy(j).start()
                if j == 0:
                    own_copy(3, CHIP_FLIPS[2]).start()

            @pl.when(first & (s == 3 + 2 * j))
            def _(j=j, f=f):
                arrival(4 + j, f + 1)

        @pl.when(first & (s == last_step - 1))
        def _():
            _hosted_copies(items, src_refs, out_refs, *sems, act="start")

        rows_i = pl.ds(pl.multiple_of(i * tb, tb), tb)

        @pl.when(s == 0)
        def _():
            _, _, h = _prenorm(x_ref[...], mod_ref[...], np_ref[...])
            h_scr[rows_i, :] = h.astype(BF16)

        proj_ref[...] = _dot(h_scr[rows_i, :], wg[order_ref[s]]).astype(BF16)

        @pl.when((s == last_step) & (i == nblk - 1))
        def _():
            own_copy(0, 1).wait_send()
            for j, f in enumerate(CHIP_FLIPS):
                own_copy(1 + j, f).wait_send()
                passed_copy(j).wait_send()
            keep(last_step).start()
            for t in range(N_DEV):
                keep(t).wait()
            _hosted_copies(items, src_refs, out_refs, *sems, act="wait")

    full = [jax.ShapeDtypeStruct((4, 256, 256), BF16)] + [jax.ShapeDtypeStruct((D, D), BF16)] * (n_sh - 1)
    grid_spec = pltpu.PrefetchScalarGridSpec(
        num_scalar_prefetch=1, grid=(N_DEV, nblk),
        in_specs=[pl.BlockSpec((tb, D), lambda s, i, order: (jnp.where(s == 0, i, nblk - 1), 0)),
                  pl.BlockSpec((3, D), lambda s, i, order: (0, 0)), pl.BlockSpec((1, D), lambda s, i, order: (0, 0)),
                  ANY] + [ANY] * n_sh,
        out_specs=(pl.BlockSpec((tb, W_IN_SHARD), lambda s, i, order: (i, order[s])), ANY, *([ANY] * n_sh)),
        scratch_shapes=[pltpu.VMEM((rows, D), BF16), pltpu.VMEM((N_DEV, D, W_IN_SHARD), BF16),
                        pltpu.SemaphoreType.DMA((N_DEV - 1,)), pltpu.SemaphoreType.DMA((N_DEV - 1,)),
                        pltpu.SemaphoreType.DMA((1 + N_DEV,))] + _sem_scratch(items))
    return _pcall(body, name="in_proj", grid_spec=grid_spec,
                  out_shape=(jax.ShapeDtypeStruct((rows, N_IN), BF16), jax.ShapeDtypeStruct((D, N_IN), BF16), *full),
                  compiler_params=_params(("arbitrary", "arbitrary")),
                  )(_shard_order(_flat(_me())), x, mod3, norm_pre, w_in_s, *shards)


def _pool_windows(ext, tb, first_row):
    inv_counts = _inv_counts(tb, first_row)
    pooled = []
    for g, w in enumerate(POOL_WINDOWS):
        acc = ext[:, g * 256:(g + 1) * 256]
        tok = acc[HALO:, :]
        s = 1
        while s < w:
            acc = acc + pltpu.roll(acc, s, axis=0)
            s *= 2
        pooled.append(acc[HALO:, :] * inv_counts[g] - tok)
    return pooled, inv_counts


def _inv_counts(tb, first_row):
    pos = (first_row + lax.broadcasted_iota(jnp.int32, (tb, 1), 0) + 1).astype(F32)
    return [1.0 / jnp.minimum(pos, float(w)) for w in POOL_WINDOWS]


def _pool_fwd(proj, pool_w, pool_scale):
    rows = proj.shape[0]
    tb = _tb(rows, 1024)
    hb = tb // HALO

    def body(u_ref, halo_ref, z_ref, pw_ref, ps_ref, y_ref, pooled_ref):
        i = pl.program_id(0)
        u = u_ref[...].astype(F32)
        halo = jnp.where(i > 0, halo_ref[...].astype(F32), 0.0)
        pooled, _ = _pool_windows(jnp.concatenate([halo, u], axis=0), tb, i * tb)
        silu_z, _ = _silu_parts(z_ref[...].astype(F32))
        for g in range(4):
            cols = slice(g * 256, (g + 1) * 256)
            pooled_b = pooled[g].astype(BF16)
            pooled_ref[:, cols] = pooled_b
            mixed = _dot(pooled_b, pw_ref[g])
            y_ref[:, cols] = (mixed * ps_ref[:, cols] * silu_z[:, cols]).astype(BF16)

    blk = pl.BlockSpec((tb, D), lambda i: (i, 0))
    return _pcall(body, name="pool_fwd", grid=(rows // tb,),
                  out_shape=(jax.ShapeDtypeStruct((rows, D), BF16), jax.ShapeDtypeStruct((rows, D), BF16)),
                  in_specs=[blk, pl.BlockSpec((HALO, D), lambda i: (jnp.maximum(i * hb - 1, 0), 0)),
                            pl.BlockSpec((tb, D), lambda i: (i, 1)),
                            _full((4, 256, 256)), _full((1, D))],
                  out_specs=(blk, blk),
                  compiler_params=_params(("arbitrary",)))(proj, proj, proj, pool_w, pool_scale)


def _ssm_fwd(proj, pm, pmt, wb, wct, ptab, dvec, glu_w, glu_b, shards):
    rows = proj.shape[0]
    tb = pm.shape[0]
    k_steps = tb // SUBLANES
    nblk = rows // tb
    n_sh = len(shards)
    items = [_gather_item(t, t, _rows_of(shards[t].shape[0])) for t in range(n_sh)]

    def body(u_ref, z_ref, pm_ref, pmt_ref, wb_ref, wct_ref, p_ref, d_ref, gw_ref, gb_ref, *rest):
        src_refs = rest[:n_sh]
        y_ref, ys_ref, carry_out_ref, s_ref, gate_ref, zp_ref, up_ref = rest[n_sh:n_sh + 7]
        out_refs = rest[n_sh + 7:2 * n_sh + 7]
        carry_ref, enter_ref, fin_ref, *sems = rest[2 * n_sh + 7:]

        @pl.when(pl.program_id(0) == 0)
        def _():
            _hosted_copies(items, src_refs, out_refs, *sems, act="start")
            carry_ref[...] = jnp.zeros_like(carry_ref)

        carry_out_ref[...] = carry_ref[...]
        up = _dot(pm_ref[...], u_ref[...]).astype(BF16)
        up_ref[...] = up

        for q in range(N_Q):
            _drive(s_ref, q, up[:, q * 256:(q + 1) * 256], wb_ref)
        for q in range(N_Q):
            _scan_forward(q, s_ref, p_ref, carry_ref, enter_ref, fin_ref, k_steps)
        for q in range(N_Q):
            cols = slice(q * 256, (q + 1) * 256)
            y = _dot_nt(s_ref[:, q * Q_W:(q + 1) * Q_W].astype(BF16), wct_ref[q])
            ys_ref[:, cols] = y + d_ref[:, cols] * up[:, cols].astype(F32)
        yg, _ = _gelu_parts(ys_ref[...])
        gate = jax.nn.sigmoid(_dot(yg.astype(BF16), gw_ref[...]) + gb_ref[...])
        gate_ref[...] = gate
        zp = _dot(pm_ref[...], z_ref[...])
        zp_ref[...] = zp.astype(BF16)
        silu_z, _ = _silu_parts(zp)
        y_ref[...] = _dot(pmt_ref[...], (yg * gate * silu_z).astype(BF16)).astype(BF16)

        @pl.when(pl.program_id(0) == nblk - 1)
        def _():
            _hosted_copies(items, src_refs, out_refs, *sems, act="wait")

    return _pcall(body, name="ssm_fwd", grid=(nblk,),
                  out_shape=(jax.ShapeDtypeStruct((rows, D), BF16), jax.ShapeDtypeStruct((rows, D), F32),
                             jax.ShapeDtypeStruct((nblk, 1, N_STATE), F32),
                             jax.ShapeDtypeStruct((rows, N_STATE), F32),
                             jax.ShapeDtypeStruct((rows, D), F32), jax.ShapeDtypeStruct((rows, D), BF16),
                             jax.ShapeDtypeStruct((rows, D), BF16),
                             *[jax.ShapeDtypeStruct((D, D), BF16)] * n_sh),
                  in_specs=[pl.BlockSpec((tb, D), lambda i: (i, 2)), pl.BlockSpec((tb, D), lambda i: (i, 3)),
                            _full((tb, tb)), _full((tb, tb)),
                            _full((N_Q, 256, Q_W), single=True), _full((N_Q, 256, Q_W), single=True),
                            _full((k_steps, N_STATE)), _full((1, D)), _full((D, D), single=True), _full((1, D))] +
                           [ANY] * n_sh,
                  out_specs=(pl.BlockSpec((tb, D), lambda i: (i, 0)), pl.BlockSpec((tb, D), lambda i: (i, 0)),
                             pl.BlockSpec((None, 1, N_STATE), lambda i: (i, 0, 0)),
                             pl.BlockSpec((tb, N_STATE), lambda i: (i, 0)),
                             *[pl.BlockSpec((tb, D), lambda i: (i, 0))] * 3, *([ANY] * n_sh)),
                  scratch_shapes=[pltpu.VMEM((1, N_STATE), F32),
                                  pltpu.VMEM((SUBLANES, N_STATE), F32), pltpu.VMEM((SUBLANES, N_STATE), F32)] +
                                 _sem_scratch(items),
                  compiler_params=_params(("arbitrary",)))(proj, proj, pm, pmt, wb, wct, ptab, dvec, glu_w, glu_b,
                                                           *shards)


def _head(x, target, proj, y_pool, y_ssm, mod3, norm_post, wbp, wbs, wout):
    rows = x.shape[0]
    tb = _tb(rows, 256)
    nblk = rows // tb
    n_feat = float(D)

    def body(x_ref, t_ref, gp_ref, gs_ref, yp_ref, ys_ref, mod_ref, npost_ref, wbp_ref, wbs_ref, wout_ref,
             loss_ref, dy_ref, dyp_ref, dys_ref, dg_ref, dwbp_hbm, dwbs_hbm, dwout_hbm, vec_ref,
             acc_bp, acc_bs, acc_out, acc_loss, acc_vec):
        i = pl.program_id(0)

        @pl.when(i == 0)
        def _():
            acc_bp[...] = jnp.zeros_like(acc_bp)
            acc_bs[...] = jnp.zeros_like(acc_bs)
            acc_out[...] = jnp.zeros_like(acc_out)
            acc_loss[...] = jnp.zeros_like(acc_loss)
            acc_vec[...] = jnp.zeros_like(acc_vec)

        gate = mod_ref[2:3, :]
        npost = npost_ref[...]
        yp, ys = yp_ref[...], ys_ref[...]
        sgp = jax.nn.sigmoid(gp_ref[...].astype(F32))
        sgs = jax.nn.sigmoid(gs_ref[...].astype(F32))
        pb = _dot(yp, wbp_ref[...])
        psm = _dot(ys, wbs_ref[...])
        mb = (sgp * pb + sgs * psm).astype(BF16)
        out = _dot(mb, wout_ref[...])
        on, r = _rms_parts(out)
        normed = on * npost
        diff = x_ref[...] + gate * normed - t_ref[...]
        acc_loss[...] += jnp.sum(diff * diff, axis=0, keepdims=True)
        dy = diff * (1.0 / n_feat)
        dy_ref[...] = dy
        acc_vec[0:1, :] += jnp.sum(dy * normed, axis=0, keepdims=True)
        dn = dy * gate
        acc_vec[1:2, :] += jnp.sum(dn * on, axis=0, keepdims=True)
        dout = _rms_bwd(dn * npost, on, r).astype(BF16)
        dm = _dot_nt(dout, wout_ref[...])
        dpb = (dm * sgp).astype(BF16)
        dps = (dm * sgs).astype(BF16)
        dg_ref[:, :D] = (dm * pb * sgp * (1.0 - sgp)).astype(BF16)
        dg_ref[:, D:] = (dm * psm * sgs * (1.0 - sgs)).astype(BF16)
        dyp_ref[...] = _dot_nt(dpb, wbp_ref[...]).astype(BF16)
        dys_ref[...] = _dot_nt(dps, wbs_ref[...]).astype(BF16)
        acc_out[...] += _dot_tn(mb, dout)
        acc_bp[...] += _dot_tn(yp, dpb)
        acc_bs[...] += _dot_tn(ys, dps)

        @pl.when(i == nblk - 1)
        def _():
            loss_ref[...] = 0.5 / n_feat * jnp.sum(acc_loss[...], axis=1, keepdims=True)
            vec_ref[...] = acc_vec[...]
            pltpu.sync_copy(acc_bp, dwbp_hbm)
            pltpu.sync_copy(acc_bs, dwbs_hbm)
            pltpu.sync_copy(acc_out, dwout_hbm)

    row = lambda c: pl.BlockSpec((tb, D), lambda i: (i, c))
    w = _full((D, D), single=True)
    return _pcall(body, name="head", grid=(nblk,),
                  out_shape=(jax.ShapeDtypeStruct((1, 1), F32), jax.ShapeDtypeStruct((rows, D), F32),
                             jax.ShapeDtypeStruct((rows, D), BF16), jax.ShapeDtypeStruct((rows, D), BF16),
                             jax.ShapeDtypeStruct((rows, 2 * D), BF16),
                             jax.ShapeDtypeStruct((D, D), F32), jax.ShapeDtypeStruct((D, D), F32),
                             jax.ShapeDtypeStruct((D, D), F32), jax.ShapeDtypeStruct((2, D), F32)),
                  in_specs=[row(0), row(0), row(4), row(5), row(0), row(0), _full((3, D)), _full((1, D)), w, w, w],
                  out_specs=(_full((1, 1)), row(0), row(0), row(0), pl.BlockSpec((tb, 2 * D), lambda i: (i, 0)),
                             ANY, ANY, ANY, _full((2, D))),
                  scratch_shapes=[pltpu.VMEM((D, D), F32), pltpu.VMEM((D, D), F32), pltpu.VMEM((D, D), F32),
                                  pltpu.VMEM((1, D), F32), pltpu.VMEM((2, D), F32)],
                  compiler_params=_params(("arbitrary",)))(x, target, proj, proj, y_pool, y_ssm, mod3, norm_post,
                                                           wbp, wbs, wout)


def _glu_bwd(dys, zp, ys_pre, gate, pm, pmt, glu_w):
    rows = dys.shape[0]
    tb = pm.shape[0]
    nblk = rows // tb

    def body(dys_ref, z_ref, ysp_ref, sg_ref, pm_ref, pmt_ref, gw_ref, dyp_ref, dz_ref, dgw_hbm, dgb_ref,
             acc_w, acc_b):
        i = pl.program_id(0)

        @pl.when(i == 0)
        def _():
            acc_w[...] = jnp.zeros_like(acc_w)
            acc_b[...] = jnp.zeros_like(acc_b)

        d_out = _dot(pm_ref[...], dys_ref[...])
        yg, dgelu = _gelu_parts(ysp_ref[...])
        ygb = yg.astype(BF16)
        sg = sg_ref[...]
        silu_z, dsilu_z = _silu_parts(z_ref[...].astype(F32))
        dz = d_out * (yg * sg) * dsilu_z
        dz_ref[...] = _dot(pmt_ref[...], dz.astype(BF16)).astype(BF16)
        dglu = d_out * silu_z
        dq = dglu * yg * sg * (1.0 - sg)
        dqb = dq.astype(BF16)
        acc_b[...] += jnp.sum(dq, axis=0, keepdims=True)
        acc_w[...] += _dot_tn(ygb, dqb)
        dyg = dglu * sg + _dot_nt(dqb, gw_ref[...])
        dyp_ref[...] = (dyg * dgelu).astype(BF16)

        @pl.when(i == nblk - 1)
        def _():
            dgb_ref[...] = acc_b[...]
            pltpu.sync_copy(acc_w, dgw_hbm)

    row = lambda c: pl.BlockSpec((tb, D), lambda i: (i, c))
    return _pcall(body, name="glu_bwd", grid=(nblk,),
                  out_shape=(jax.ShapeDtypeStruct((rows, D), BF16), jax.ShapeDtypeStruct((rows, D), BF16),
                             jax.ShapeDtypeStruct((D, D), F32), jax.ShapeDtypeStruct((1, D), F32)),
                  in_specs=[row(0), row(0), row(0), row(0), _full((tb, tb)), _full((tb, tb)),
                            _full((D, D), single=True)],
                  out_specs=(row(0), row(0), ANY, _full((1, D))),
                  scratch_shapes=[pltpu.VMEM((D, D), F32), pltpu.VMEM((1, D), F32)],
                  compiler_params=_params(("arbitrary",)))(dys, zp, ys_pre, gate, pm, pmt, glu_w)


def _ssm_bwd(dyp, up, states, carries, pmt, wb, wct, ptab, dvec, mat_grads, dpool_w, dw_in_rest):
    rows = dyp.shape[0]
    tb = pmt.shape[0]
    k_steps = tb // SUBLANES
    nblk = rows // tb
    n_mat = len(mat_grads)
    hosted = [*mat_grads, dpool_w, dw_in_rest]
    n_h = len(hosted)
    shard_rows = D // N_DEV
    pool_rows = dpool_w.shape[1] // N_DEV
    items = [_scatter_item(t, t, _rows_of(shard_rows)) for t in range(n_mat)] + \
            [_scatter_item(n_mat, n_mat, _pool_rows_of(pool_rows))] + \
            [_w_in_block_item(n_mat + 1, n_mat + 1, j, ssm_part=False) for j in range(W_IN_SHARD // W_IN_BLOCK)]
    n_in, n_out = 9, 5

    def body(*refs):
        dyp_ref, u_ref, s_ref, cin_ref, pmt_ref, wb_ref, wct_ref, p_ref, d_ref = refs[:n_in]
        src_refs = refs[n_in:n_in + n_h]
        du_ref, dbb_ref, dcc_ref, da_ref, dd_ref = refs[n_in + n_h:n_in + n_h + n_out]
        recv_refs = refs[n_in + n_h + n_out:n_in + 2 * n_h + n_out]
        (g_ref, carry_b, fin_ref, acc_wb, acc_wct, acc_da, acc_dd, dup_ref,
         *sems) = refs[n_in + 2 * n_h + n_out:]
        i = pl.program_id(0)

        @pl.when(i == 0)
        def _():
            _hosted_copies(items, src_refs, recv_refs, *sems, act="start")
            carry_b[...] = jnp.zeros_like(carry_b)
            acc_wb[...] = jnp.zeros_like(acc_wb)
            acc_wct[...] = jnp.zeros_like(acc_wct)
            acc_da[...] = jnp.zeros_like(acc_da)
            acc_dd[...] = jnp.zeros_like(acc_dd)

        def own_products(acc, q, chan, state_ref):
            chan_t = chan.T
            for j in range(16 // 2):
                r = slice(j * 2 * G_H, (j + 1) * 2 * G_H)
                re = q * Q_W + j * 128
                slab = jnp.concatenate([state_ref[:, re:re + 128], state_ref[:, re + Q_W // 2:re + Q_W // 2 + 128]],
                                       axis=1).astype(BF16)
                acc[q, r, :] += _dot(chan_t[r, :], slab)

        dy = dyp_ref[...]
        up = u_ref[...]
        acc_dd[...] += jnp.sum(dy.astype(F32) * up.astype(F32), axis=0, keepdims=True)
        for q in range(N_Q):
            cols = slice(q * 256, (q + 1) * 256)
            _drive(g_ref, q, dy[:, cols], wct_ref)
            own_products(acc_wct, q, dy[:, cols], s_ref)
        for q in range(N_Q):
            _scan_backward(q, g_ref, s_ref, p_ref, carry_b, cin_ref, fin_ref, acc_da, k_steps)
        for q in range(N_Q):
            cols = slice(q * 256, (q + 1) * 256)
            own_products(acc_wb, q, up[:, cols], g_ref)
            lam = g_ref[:, q * Q_W:(q + 1) * Q_W].astype(BF16)
            dup_ref[:, cols] = (_dot_nt(lam, wb_ref[q]) + d_ref[:, cols] * dy[:, cols].astype(F32)).astype(BF16)
        du_ref[...] = _dot(pmt_ref[...], dup_ref[...]).astype(BF16)

        @pl.when(i == nblk - 1)
        def _():
            da_ref[...] = acc_da[...]
            dd_ref[...] = acc_dd[...]
            lane = lax.broadcasted_iota(jnp.int32, (16 * G_H, 128), 1)
            row = lax.broadcasted_iota(jnp.int32, (16 * G_H, 128), 0)
            own = lane // G_P == (row // G_H) % 2
            spread = (lax.broadcasted_iota(jnp.int32, (G_P, 128), 1) % G_P ==
                      lax.broadcasted_iota(jnp.int32, (G_P, 128), 0)).astype(F32)
            for acc, out in ((acc_wb, dbb_ref), (acc_wct, dcc_ref)):
                for half in range(2):
                    for q in range(N_Q):
                        kept = jnp.where(own, acc[q, :, half * 128:(half + 1) * 128], 0.0)
                        out[half, q] = lax.dot_general(kept, spread, (((1,), (1,)), ((), ())),
                                                       preferred_element_type=F32, precision=lax.Precision.HIGHEST)
            _hosted_copies(items, src_refs, recv_refs, *sems, act="wait")

    rev = lambda c: pl.BlockSpec((tb, D), lambda i: (nblk - 1 - i, c))
    recv = [jax.ShapeDtypeStruct((N_DEV, shard_rows, D), F32)] * n_mat + \
           [jax.ShapeDtypeStruct((N_DEV, dpool_w.shape[0], pool_rows, dpool_w.shape[2]), F32),
            jax.ShapeDtypeStruct((N_DEV, D, W_IN_SHARD), BF16)]
    return _pcall(body, name="ssm_bwd", grid=(nblk,),
                  out_shape=(jax.ShapeDtypeStruct((rows, D), BF16),
                             jax.ShapeDtypeStruct((2, N_Q, 16 * G_H, G_P), F32),
                             jax.ShapeDtypeStruct((2, N_Q, 16 * G_H, G_P), F32),
                             jax.ShapeDtypeStruct((1, N_STATE), F32), jax.ShapeDtypeStruct((1, D), F32), *recv),
                  in_specs=[rev(0), rev(0), pl.BlockSpec((tb, N_STATE), lambda i: (nblk - 1 - i, 0)),
                            pl.BlockSpec((None, 1, N_STATE), lambda i: (nblk - 1 - i, 0, 0)),
                            _full((tb, tb)),
                            _full((N_Q, 256, Q_W), single=True), _full((N_Q, 256, Q_W), single=True),
                            _full((k_steps, N_STATE)), _full((1, D))] + [ANY] * n_h,
                  out_specs=(rev(0), _full((2, N_Q, 16 * G_H, G_P)), _full((2, N_Q, 16 * G_H, G_P)),
                             _full((1, N_STATE)), _full((1, D)), *([ANY] * n_h)),
                  scratch_shapes=[pltpu.VMEM((tb, N_STATE), F32), pltpu.VMEM((1, N_STATE), F32),
                                  pltpu.VMEM((SUBLANES, N_STATE), F32),
                                  pltpu.VMEM((N_Q, 16 * G_H, 256), F32), pltpu.VMEM((N_Q, 16 * G_H, 256), F32),
                                  pltpu.VMEM((1, N_STATE), F32), pltpu.VMEM((1, D), F32),
                                  pltpu.VMEM((tb, D), BF16)] + _sem_scratch(items),
                  compiler_params=_params(("arbitrary",), vmem=60 * 1024 * 1024),
                  )(dyp, up, states, carries, pmt, wb, wct, ptab, dvec, *hosted)


def _pool_bwd(dyp, pooled, proj, pool_w, pool_scale):
    rows = dyp.shape[0]
    tb = _tb(rows, 1024)
    nblk = rows // tb

    def body(dy_ref, pooled_ref, z_ref, pw_ref, ps_ref, dp_ref, dpw_ref, dps_ref, ahead_ref):
        i = pl.program_id(0)
        blk = nblk - 1 - i

        @pl.when(i == 0)
        def _():
            ahead_ref[...] = jnp.zeros_like(ahead_ref)
            dpw_ref[...] = jnp.zeros_like(dpw_ref)
            dps_ref[...] = jnp.zeros_like(dps_ref)

        inv_counts = _inv_counts(tb, blk * tb)
        silu_z, dsilu_z = _silu_parts(z_ref[...].astype(F32))
        dy = dy_ref[...].astype(F32)
        for g, w in enumerate(POOL_WINDOWS):
            cols = slice(g * 256, (g + 1) * 256)
            pooled_b = pooled_ref[:, cols]
            mixed = _dot(pooled_b, pw_ref[g])
            scale = ps_ref[:, cols]
            dp_ref[:, D + g * 256:D + (g + 1) * 256] = (dy[:, cols] * (mixed * scale) * dsilu_z[:, cols]).astype(BF16)
            dms = dy[:, cols] * silu_z[:, cols]
            dps_ref[:, cols] += jnp.sum(dms * mixed, axis=0, keepdims=True)
            dmixed = (dms * scale).astype(BF16)
            dpw_ref[g] += _dot_tn(pooled_b, dmixed)
            dpooled = _dot_nt(dmixed, pw_ref[g])
            ratio = dpooled * inv_counts[g]
            acc = jnp.concatenate([ratio, ahead_ref[:, cols]], axis=0)
            ahead_ref[:, cols] = ratio[:HALO, :]
            s = 1
            while s < w:
                acc = acc + pltpu.roll(acc, tb + HALO - s, axis=0)
                s *= 2
            dp_ref[:, cols] = (acc[:tb, :] - dpooled).astype(BF16)

    rev = lambda c: pl.BlockSpec((tb, D), lambda i: (nblk - 1 - i, c))
    return _pcall(body, name="pool_bwd", grid=(nblk,),
                  out_shape=(jax.ShapeDtypeStruct((rows, 2 * D), BF16), jax.ShapeDtypeStruct((4, 256, 256), F32),
                             jax.ShapeDtypeStruct((1, D), F32)),
                  in_specs=[rev(0), rev(0), rev(1), _full((4, 256, 256)), _full((1, D))],
                  out_specs=(pl.BlockSpec((tb, 2 * D), lambda i: (nblk - 1 - i, 0)), _full((4, 256, 256)),
                             _full((1, D))),
                  scratch_shapes=[pltpu.VMEM((HALO, D), F32)],
                  compiler_params=_params(("arbitrary",)))(dyp, pooled, proj, pool_w, pool_scale)


def _dproj_specs(tb):
    return [pl.BlockSpec((tb, 2 * D), lambda i: (i, 0)), pl.BlockSpec((tb, D), lambda i: (i, 0)),
            pl.BlockSpec((tb, D), lambda i: (i, 0)), pl.BlockSpec((tb, 2 * D), lambda i: (i, 0))]


def _in_proj_bwd_x(x, dy, dpp, dus, dzs, dpg, mod3, norm_pre, w_in, dw_in_ssm, recv_w_in):
    rows = x.shape[0]
    tb = _tb(rows, 512)
    nblk = rows // tb
    items = [_w_in_block_item(0, 0, j, ssm_part=True) for j in range(W_IN_SHARD // W_IN_BLOCK)]
    sums_item = [_Item(0, 0, _whole, _slot)]

    def body(x_ref, dy_ref, dpp_ref, dus_ref, dzs_ref, dpg_ref, mod_ref, np_ref, w_ref,
             dw_src, _, gx_ref, recv_w, recv_sums, vec_ref, ssem, rsem, lsem, *sums_sems):
        src_refs, recv_refs, sems = (dw_src,), (recv_w,), (ssem, rsem, lsem)

        @pl.when(pl.program_id(0) == 0)
        def _():
            _hosted_copies(items, src_refs, recv_refs, *sems, act="start")
            vec_ref[...] = jnp.zeros_like(vec_ref)

        dh = _dot_nt(dpp_ref[...], w_ref[:, 0:2 * D])
        dh += _dot_nt(dus_ref[...], w_ref[:, 2 * D:3 * D])
        dh += _dot_nt(dzs_ref[...], w_ref[:, 3 * D:4 * D])
        dh += _dot_nt(dpg_ref[...], w_ref[:, 4 * D:6 * D])
        xn, r, _ = _prenorm(x_ref[...], mod_ref[...], np_ref[...])
        one_scale = 1.0 + mod_ref[1:2, :]
        vec_ref[0:1, :] += jnp.sum(dh, axis=0, keepdims=True)
        vec_ref[1:2, :] += jnp.sum(dh * xn, axis=0, keepdims=True) * np_ref[...]
        vec_ref[2:3, :] += jnp.sum(dh * xn, axis=0, keepdims=True) * one_scale
        gx_ref[...] = dy_ref[...] + _rms_bwd(dh * (np_ref[...] * one_scale), xn, r)

        @pl.when(pl.program_id(0) == nblk - 1)
        def _():
            _hosted_copies(sums_item, (vec_ref,), (recv_sums,), *sums_sems, act="start")
            _hosted_copies(items, src_refs, recv_refs, *sems, act="wait")
            _hosted_copies(sums_item, (vec_ref,), (recv_sums,), *sums_sems, act="wait")

    row = pl.BlockSpec((tb, D), lambda i: (i, 0))
    recv = (jax.ShapeDtypeStruct(recv_w_in.shape, recv_w_in.dtype), jax.ShapeDtypeStruct((N_DEV, 3, D), F32))
    return _pcall(body, name="in_proj_bwd_x", grid=(nblk,),
                  out_shape=(jax.ShapeDtypeStruct((rows, D), F32), *recv),
                  in_specs=[row, row] + _dproj_specs(tb) + [_full((3, D)), _full((1, D)),
                                                            _full((D, N_IN), single=True)] + [ANY] * 2,
                  out_specs=(row, ANY, ANY),
                  input_output_aliases={10: 1},
                  scratch_shapes=[pltpu.VMEM((3, D), F32)] + _sem_scratch(items) + _sem_scratch(sums_item),
                  compiler_params=_params(("arbitrary",)))(x, dy, dpp, dus, dzs, dpg, mod3, norm_pre, w_in,
                                                           dw_in_ssm, recv_w_in)


def _in_proj_bwd_w(name, x, dparts, mod3, norm_pre, gathered=()):
    rows = x.shape[0]
    tb = _tb(rows, 512)
    nblk = rows // tb
    widths = [p.shape[1] for p in dparts]
    n_p, n_g = len(dparts), len(gathered)
    items = [_Item(t, t, _whole, _slot) for t in range(n_g)]

    def body(x_ref, *rest):
        part_refs, (mod_ref, np_ref) = rest[:n_p], rest[n_p:n_p + 2]
        src_refs, dw_ref = rest[n_p + 2:n_p + 2 + n_g], rest[n_p + 2 + n_g]
        recv_refs, (acc, *sems) = rest[n_p + 3 + n_g:n_p + 3 + 2 * n_g], rest[n_p + 3 + 2 * n_g:]
        i = pl.program_id(0)

        @pl.when(i == 0)
        def _():
            if n_g:
                _hosted_copies(items, src_refs, recv_refs, *sems, act="start")
            acc[...] = jnp.zeros_like(acc)

        _, _, h = _prenorm(x_ref[...], mod_ref[...], np_ref[...])
        ht = h.astype(BF16)
        lo = 0
        for ref, w in zip(part_refs, widths):
            acc[:, lo:lo + w] += _dot_tn(ht, ref[...])
            lo += w

        @pl.when(i == nblk - 1)
        def _():
            dw_ref[...] = acc[...].astype(BF16)
            if n_g:
                _hosted_copies(items, src_refs, recv_refs, *sems, act="wait")

    row = pl.BlockSpec((tb, D), lambda i: (i, 0))
    out = _pcall(body, name=name, grid=(nblk,),
                 out_shape=(jax.ShapeDtypeStruct((D, sum(widths)), BF16),
                            *[jax.ShapeDtypeStruct((N_DEV,) + g.shape, g.dtype) for g in gathered]),
                 in_specs=[row] + [pl.BlockSpec((tb, w), lambda i: (i, 0)) for w in widths] +
                          [_full((3, D)), _full((1, D))] + [ANY] * n_g,
                 out_specs=(_full((D, sum(widths))), *([ANY] * n_g)),
                 scratch_shapes=[pltpu.VMEM((D, sum(widths)), F32)] + (_sem_scratch(items) if n_g else []),
                 compiler_params=_params(("arbitrary",)))(x, *dparts, mod3, norm_pre, *gathered)
    return out if n_g else out[0]


def _adamw_math(w, g, m, v):
    m = ADAM_B1 * m + (1.0 - ADAM_B1) * g
    v = ADAM_B2 * v + (1.0 - ADAM_B2) * (g * g)
    m_hat = m / (1.0 - ADAM_B1 ** ADAM_STEP)
    v_hat = v / (1.0 - ADAM_B2 ** ADAM_STEP)
    delta = -ADAM_LR * (m_hat / (jnp.sqrt(v_hat) + ADAM_EPS) + ADAM_WD * w)
    return delta, m, v


def _sum_sources(ref):
    g = ref[0].astype(F32)
    for s in range(1, N_DEV):
        g = g + ref[s].astype(F32)
    return g


def _adamw_reduce(name, parts, w, m, v):
    r, c = w.shape
    tr = r if r * c <= 256 * 1024 else max(8, (256 * 1024 // c) // 8 * 8)
    while r % tr:
        tr -= 8

    def body(p_ref, w_ref, m_ref, v_ref, g_ref, d_ref, nm_ref, nv_ref):
        g = _sum_sources(p_ref)
        g_ref[...] = g
        d_ref[...], nm_ref[...], nv_ref[...] = _adamw_math(w_ref[...], g, m_ref[...], v_ref[...])

    blk = pl.BlockSpec((tr, c), lambda i: (i, 0))
    return _pcall(body, name=name, grid=(r // tr,),
                  out_shape=tuple([jax.ShapeDtypeStruct((r, c), F32)] * 4),
                  in_specs=[pl.BlockSpec((N_DEV, tr, c), lambda i: (0, i, 0)), blk, blk, blk],
                  out_specs=(blk, blk, blk, blk),
                  compiler_params=_params(("arbitrary",)))(parts, w, m, v)


def _adamw_small(gs, ws, ms, vs):
    n = len(gs)

    def body(*refs):
        ins, outs = refs[:4 * n], refs[4 * n:]
        for t in range(n):
            g_ref, w_ref, m_ref, v_ref = ins[4 * t:4 * t + 4]
            outs[3 * t][...], outs[3 * t + 1][...], outs[3 * t + 2][...] = _adamw_math(
                w_ref[...], g_ref[...], m_ref[...], v_ref[...])

    vm = pl.BlockSpec(memory_space=pltpu.VMEM)
    flat = [a for t in range(n) for a in (gs[t], ws[t], ms[t], vs[t])]
    return _pcall(body, name="adamw_small",
                  out_shape=tuple(jax.ShapeDtypeStruct(w.shape, F32) for w in ws for _ in range(3)),
                  in_specs=[vm] * (4 * n), out_specs=tuple([vm] * (3 * n)), compiler_params=_params())(*flat)


def _sum_small(parts):
    n = len(parts)

    def body(*refs):
        for t in range(n):
            refs[n + t][...] = _sum_sources(refs[t])

    vm = pl.BlockSpec(memory_space=pltpu.VMEM)
    return _pcall(body, name="sum_small",
                  out_shape=tuple(jax.ShapeDtypeStruct(p.shape[1:], F32) for p in parts),
                  in_specs=[vm] * n, out_specs=tuple([vm] * n), compiler_params=_params())(*parts)


def _ada_update(c_all, dmod_cols, w, m, v):
    def body(c_ref, dm_ref, w_ref, m_ref, v_ref, g_ref, d_ref, nm_ref, nv_ref):
        ca = c_ref[...]
        g = lax.dot_general(ca * jax.nn.sigmoid(ca), dm_ref[...], (((0,), (0,)), ((), ())),
                            preferred_element_type=F32, precision=lax.Precision.HIGHEST)
        g_ref[...] = g
        d_ref[...], nm_ref[...], nv_ref[...] = _adamw_math(w_ref[...], g, m_ref[...], v_ref[...])

    vm = pl.BlockSpec(memory_space=pltpu.VMEM)
    return _pcall(body, name="ada_update", out_shape=tuple([jax.ShapeDtypeStruct(w.shape, F32)] * 4),
                  in_specs=[vm] * 5, out_specs=(vm, vm, vm, vm), compiler_params=_params())(c_all, dmod_cols, w, m, v)


def kernel(x, c, w_ada, b_ada, norm_pre, norm_post, w_in, pool_w, pool_scale, ssm_a_re, ssm_a_im, ssm_log_dt, ssm_b_re, ssm_b_im, ssm_c_re, ssm_c_im, ssm_d, glu_w, glu_b, w_branch_pool, w_branch_ssm, w_out, loss_target, m_w_ada, m_b_ada, m_norm_pre, m_norm_post, m_w_in, m_pool_w, m_pool_scale, m_ssm_a_re, m_ssm_a_im, m_ssm_log_dt, m_ssm_b_re, m_ssm_b_im, m_ssm_c_re, m_ssm_c_im, m_ssm_d, m_glu_w, m_glu_b, m_w_branch_pool, m_w_branch_ssm, m_w_out, v_w_ada, v_b_ada, v_norm_pre, v_norm_post, v_w_in, v_pool_w, v_pool_scale, v_ssm_a_re, v_ssm_a_im, v_ssm_log_dt, v_ssm_b_re, v_ssm_b_im, v_ssm_c_re, v_ssm_c_im, v_ssm_d, v_glu_w, v_glu_b, v_w_branch_pool, v_w_branch_ssm, v_w_out):
    given = dict(locals())
    me = _flat(_me())
    rows = x.shape[1]
    x2 = x[0]
    target = loss_target[0]
    ada_cols = w_ada.shape[2]

    tb_ssm = _tb(rows, 256)
    k_steps = tb_ssm // SUBLANES
    a_re, a_im = ssm_a_re[0], ssm_a_im[0]
    log_dt = ssm_log_dt[0].reshape(GROUPS, 1)
    b_re_t, b_im_t = ssm_b_re[0].transpose(0, 2, 1), ssm_b_im[0].transpose(0, 2, 1)
    s5_params = [a_re, a_im, log_dt, b_re_t, b_im_t, ssm_c_re[0], ssm_c_im[0]]

    f32_shards = [w_in[0], pool_w[0], glu_w[0], w_branch_pool[0], w_branch_ssm[0], w_out[0]]
    n_sh = len(f32_shards)

    def local_work(ins, outs):
        for src, dst in zip(ins[:n_sh], outs[:n_sh]):
            dst[...] = src[...].astype(BF16)
        _s5_prep_body(*ins[n_sh:], *outs[n_sh:])

    b_ada_s = lax.dynamic_slice(b_ada, (0, me * ada_cols), (1, ada_cols))
    c_all, mod_rows, *local = _ada_exchange(
        c, w_ada[0], b_ada_s, f32_shards + s5_params,
        [jax.ShapeDtypeStruct(a.shape, BF16) for a in f32_shards] + list(_s5_prep_structs(k_steps)), local_work)
    mod3 = mod_rows.reshape(3, D)
    shards, (wb, wct, pow_re, pow_im) = local[:n_sh], local[n_sh:]
    ptab = _state_layout(pow_re, pow_im)
    dvec = ssm_d[0].reshape(1, D)
    pm = _perm_matrix(tb_ssm)
    pmt = pm.T

    proj, w_in_g, pool_w_g, glu_g = _in_proj(x2, mod3, norm_pre, shards[0], shards[1:3])
    y_pool, pooled = _pool_fwd(proj, pool_w_g, pool_scale)
    y_ssm, ys_pre, carries, states, glu_gate, z_perm, u_perm, wbp_g, wbs_g, wout_g = _ssm_fwd(
        proj, pm, pmt, wb, wct, ptab, dvec, glu_g, glu_b, shards[3:])
    loss_part, dy, dyp, dys, dpg, dwbp, dwbs, dwout, head_vec = _head(
        x2, target, proj, y_pool, y_ssm, mod3, norm_post, wbp_g, wbs_g, wout_g)

    dpp, dpool_w, dpool_scale = _pool_bwd(dyp, pooled, proj, pool_w_g, pool_scale)
    dw_in_rest = _in_proj_bwd_w("in_proj_bwd_w_rest", x2, [dpp, dpg], mod3, norm_pre)
    dy_pre, dzs, dglu_w, dglu_b = _glu_bwd(dys, z_perm, ys_pre, glu_gate, pm, pmt, glu_g)
    dus, dbb, dcc, dabar, dd, p_glu, p_wbp, p_wbs, p_wout, p_pool_w, p_w_in = _ssm_bwd(
        dy_pre, u_perm, states, carries, pmt, wb, wct, ptab, dvec, [dglu_w, dwbp, dwbs, dwout], dpool_w,
        dw_in_rest)

    small32 = jnp.concatenate([head_vec, dpool_scale, dglu_b, dd, jnp.broadcast_to(loss_part, (1, D)),
                               jnp.zeros((2, D), F32), dabar.reshape(8, D)], axis=0)
    small16 = jnp.concatenate([dbb.reshape(2 * GROUPS, D), dcc.reshape(2 * GROUPS, D)], axis=0).astype(BF16)
    dw_in_ssm, p_small32, p_small16 = _in_proj_bwd_w("in_proj_bwd_w_ssm", x2, [dus, dzs], mod3, norm_pre,
                                                     gathered=(small32, small16))
    grad_x, p_w_in, p_pre = _in_proj_bwd_x(x2, dy, dpp, dus, dzs, dpg, mod3, norm_pre, w_in_g, dw_in_ssm, p_w_in)

    tot32, tot16, tot_pre = _sum_small([p_small32, p_small16, p_pre])
    d_abar_re, d_abar_im = _state_unlayout(tot32[8:16].reshape(N_STATE))
    d_bb_re, d_bb_im = tot16[0:64].reshape(GROUPS, G_H, G_P), tot16[64:128].reshape(GROUPS, G_H, G_P)
    g_a_re, g_a_im, g_log_dt, g_b_re_t, g_b_im_t = _s5_prep_bwd(
        a_re, a_im, log_dt, b_re_t, b_im_t, d_abar_re, d_abar_im, d_bb_re, d_bb_im)

    grads, deltas, new_m, new_v = {}, {}, {}, {}

    small = []

    def small_update(name, g2):
        small.append((name, g2))

    def shard_update(name, parts):
        shape = given[name].shape
        r2 = parts.shape[1:] if parts.ndim == 3 else (parts.shape[1] * parts.shape[2], parts.shape[3])
        w2, m2, v2 = (given[p + name].reshape(r2) for p in ("", "m_", "v_"))
        out = _adamw_reduce("adamw_" + name, parts.reshape((N_DEV,) + tuple(r2)), w2, m2, v2)
        grads[name], deltas[name], new_m[name], new_v[name] = (a.reshape(shape) for a in out)

    dmod_all = jnp.concatenate([p_pre[:, 0:2, :], p_small32[:, 0:1, :]], axis=1).reshape(N_DEV, 3 * D)
    dmod_cols = lax.dynamic_slice(dmod_all, (0, me * ada_cols), (N_DEV, ada_cols))
    out = _ada_update(c_all, dmod_cols, w_ada[0], m_w_ada[0], v_w_ada[0])
    grads['w_ada'], deltas['w_ada'], new_m['w_ada'], new_v['w_ada'] = (a.reshape(w_ada.shape) for a in out)

    small_update('b_ada', jnp.concatenate([tot_pre[0:2], tot32[0:1]], axis=0).reshape(1, 3 * D))
    small_update('norm_pre', tot_pre[2:3])
    small_update('norm_post', tot32[1:2])
    small_update('pool_scale', tot32[2:3])
    small_update('glu_b', tot32[3:4])
    small_update('ssm_d', tot32[4:5])
    small_update('ssm_a_re', g_a_re)
    small_update('ssm_a_im', g_a_im)
    small_update('ssm_log_dt', g_log_dt.reshape(1, GROUPS))
    small_update('ssm_b_re', g_b_re_t.transpose(0, 2, 1).reshape(GROUPS, G_P * G_H))
    small_update('ssm_b_im', g_b_im_t.transpose(0, 2, 1).reshape(GROUPS, G_P * G_H))
    small_update('ssm_c_re', tot16[128:192])
    small_update('ssm_c_im', -tot16[192:256])
    flat = _adamw_small([g2 for _, g2 in small],
                        *[[given[p + name].reshape(g2.shape) for name, g2 in small] for p in ("", "m_", "v_")])
    for t, (name, g2) in enumerate(small):
        shape = given[name].shape
        grads[name], deltas[name], new_m[name], new_v[name] = (
            a.reshape(shape) for a in (g2, *flat[3 * t:3 * t + 3]))
    shard_update('w_in', p_w_in)
    shard_update('pool_w', p_pool_w)
    shard_update('glu_w', p_glu)
    shard_update('w_branch_pool', p_wbp)
    shard_update('w_branch_ssm', p_wbs)
    shard_update('w_out', p_wout)

    return (tot32[5, 0], grad_x[None], *[grads[n] for n in WEIGHTS], *[deltas[n] for n in WEIGHTS],
            *[new_m[n] for n in WEIGHTS], *[new_v[n] for n in WEIGHTS])
```

```python
import math
from typing import Callable, NamedTuple, Optional

import jax
import jax.numpy as jnp
from jax import lax
from jax.experimental import pallas as pl
from jax.experimental.pallas import tpu as pltpu

F32 = jnp.float32
BF16 = jnp.bfloat16
MESH = pl.DeviceIdType.MESH

D = 1024
N_DEV = 8
N_IN = 6 * D
GROUPS = 64
G_H = 16
G_P = 64
N_Q = 4
Q_W = 2 * 16 * G_P
N_STATE = N_Q * Q_W
POOL_WINDOWS = (2, 4, 8, 16)
HALO = 16
RMS_EPS = 1e-6
SUBLANES = 8
LANE_CHUNK = 512
SCAN_UNROLL = 2
VMEM_LIMIT = 56 * 1024 * 1024

ADAM_LR = 0.001
ADAM_B1 = 0.9
ADAM_B2 = 0.999
ADAM_EPS = 1e-08
ADAM_WD = 0.01
ADAM_STEP = 10

WEIGHTS = ['w_ada', 'b_ada', 'norm_pre', 'norm_post', 'w_in', 'pool_w', 'pool_scale', 'ssm_a_re',
           'ssm_a_im', 'ssm_log_dt', 'ssm_b_re', 'ssm_b_im', 'ssm_c_re', 'ssm_c_im', 'ssm_d', 'glu_w',
           'glu_b', 'w_branch_pool', 'w_branch_ssm', 'w_out']


def _pcall(body, **kw):
    return pl.pallas_call(body, **kw)


def _params(sem=None, vmem=VMEM_LIMIT):
    return pltpu.CompilerParams(dimension_semantics=sem, vmem_limit_bytes=vmem)


def _tb(rows, pref):
    return pref if rows % pref == 0 and rows // pref >= 2 else rows // 2


def _full(shape, single=False):
    nd = len(shape)
    if single:
        return pl.BlockSpec(shape, lambda i: (0,) * nd, pipeline_mode=pl.Buffered(1))
    return pl.BlockSpec(shape, lambda i: (0,) * nd)


ANY = pl.BlockSpec(memory_space=pl.ANY)


def _me():
    return lax.axis_index("x"), lax.axis_index("y"), lax.axis_index("c")


def _flat(p):
    return 4 * p[0] + 2 * p[1] + p[2]


def _peer(k):
    x, y, c = _me()
    return (1 - x if k & 4 else x, 1 - y if k & 2 else y, 1 - c if k & 1 else c)


def _silu_parts(z):
    s = jax.nn.sigmoid(z)
    return z * s, s * (1.0 + z * (1.0 - s))


_GELU_C = math.sqrt(2.0 / math.pi)


def _gelu_parts(x):
    x2 = x * x
    t = jnp.tanh(_GELU_C * (x + 0.044715 * x * x2))
    g = 0.5 * x * (1.0 + t)
    dg = 0.5 * (1.0 + t) + 0.5 * x * (1.0 - t * t) * (_GELU_C * (1.0 + 3.0 * 0.044715 * x2))
    return g, dg


def _dot(a, b):
    return jnp.dot(a, b, preferred_element_type=F32)


def _dot_nt(a, b):
    return lax.dot_general(a, b, (((1,), (1,)), ((), ())), preferred_element_type=F32)


def _dot_tn(a, b):
    return lax.dot_general(a, b, (((0,), (0,)), ((), ())), preferred_element_type=F32)


def _rms_parts(x):
    r = lax.rsqrt(jnp.mean(x * x, axis=-1, keepdims=True) + RMS_EPS)
    return x * r, r


def _rms_bwd(dxn, xn, r):
    return r * (dxn - xn * jnp.mean(dxn * xn, axis=-1, keepdims=True))


def _ada_exchange(c, w_ada_s, b_ada_s, local_ins, local_outs, local_work):
    cols = w_ada_s.shape[1]
    n_li, n_lo = len(local_ins), len(local_outs)

    def body(c_ref, w_ref, b_ref, *rest):
        li_refs, call_ref, mod_ref = rest[:n_li], rest[n_li], rest[n_li + 1]
        lo_refs, (part_ref, ssem, rsem, lsem) = rest[n_li + 2:n_li + 2 + n_lo], rest[n_li + 2 + n_lo:]
        me3 = _me()
        me = _flat(me3)
        mine = pltpu.make_async_copy(c_ref, call_ref.at[pl.ds(me, 1), :], lsem.at[0])
        mine.start()
        sends = []
        for k in range(1, N_DEV):
            cp = pltpu.make_async_remote_copy(src_ref=c_ref, dst_ref=call_ref.at[pl.ds(me, 1), :],
                                              send_sem=ssem.at[k - 1], recv_sem=rsem.at[k - 1],
                                              device_id=_peer(k), device_id_type=MESH)
            cp.start()
            sends.append(cp)
        local_work(li_refs, lo_refs)
        mine.wait()
        for k in range(1, N_DEV):
            p = _flat(_peer(k))
            pltpu.make_async_remote_copy(src_ref=c_ref, dst_ref=call_ref.at[pl.ds(p, 1), :],
                                         send_sem=ssem.at[k - 1], recv_sem=rsem.at[k - 1],
                                         device_id=_peer(k), device_id_type=MESH).wait_recv()
        for cp in sends:
            cp.wait_send()
        ca = call_ref[...]
        act = ca * jax.nn.sigmoid(ca)
        part_ref[...] = jnp.dot(act, w_ref[...], preferred_element_type=F32,
                                precision=lax.Precision.HIGHEST) + b_ref[...]
        own = pltpu.make_async_copy(part_ref.at[pl.ds(me, 1), :], mod_ref.at[pl.ds(me, 1), :], lsem.at[1])
        own.start()
        sends = []
        for k in range(1, N_DEV):
            p = _flat(_peer(k))
            s = N_DEV - 1 + k - 1
            cp = pltpu.make_async_remote_copy(src_ref=part_ref.at[pl.ds(p, 1), :],
                                              dst_ref=mod_ref.at[pl.ds(me, 1), :],
                                              send_sem=ssem.at[s], recv_sem=rsem.at[s],
                                              device_id=_peer(k), device_id_type=MESH)
            cp.start()
            sends.append(cp)
        own.wait()
        for k in range(1, N_DEV):
            p = _flat(_peer(k))
            s = N_DEV - 1 + k - 1
            pltpu.make_async_remote_copy(src_ref=part_ref.at[pl.ds(p, 1), :],
                                         dst_ref=mod_ref.at[pl.ds(p, 1), :],
                                         send_sem=ssem.at[s], recv_sem=rsem.at[s],
                                         device_id=_peer(k), device_id_type=MESH).wait_recv()
        for cp in sends:
            cp.wait_send()

    vm = pl.BlockSpec(memory_space=pltpu.VMEM)
    return _pcall(
        body, name="ada_exchange",
        out_shape=(jax.ShapeDtypeStruct((N_DEV, D), F32), jax.ShapeDtypeStruct((N_DEV, cols), F32), *local_outs),
        in_specs=[vm] * (3 + n_li), out_specs=tuple([vm] * (2 + n_lo)),
        scratch_shapes=[pltpu.VMEM((N_DEV, cols), F32),
                        pltpu.SemaphoreType.DMA((2 * (N_DEV - 1),)),
                        pltpu.SemaphoreType.DMA((2 * (N_DEV - 1),)),
                        pltpu.SemaphoreType.DMA((2,))],
        compiler_params=_params(),
    )(c, w_ada_s, b_ada_s, *local_ins)


class _Item(NamedTuple):
    src: int
    out: int
    src_view: Callable
    dst_view: Callable
    pred: Optional[Callable] = None


def _when(pred, dest, fn):
    if pred is None:
        fn()
    else:
        pl.when(pred(dest))(fn)


def _n_sems(items):
    return len(items) * (N_DEV - 1)


def _hosted_copies(items, srcs, outs, ssem, rsem, lsem, act):
    me = _flat(_me())
    for t, it in enumerate(items):
        local = lambda t=t, it=it: pltpu.make_async_copy(
            it.src_view(srcs[it.src], me), it.dst_view(outs[it.out], me), lsem.at[t])
        if act == "start":
            _when(it.pred, me, lambda local=local: local().start())
        else:
            _when(it.pred, me, lambda local=local: local().wait())
    for k in range(1, N_DEV):
        p3 = _peer(k)
        p = _flat(p3)
        for t, it in enumerate(items):
            s = t * (N_DEV - 1) + k - 1
            send = lambda it=it, s=s, p=p, p3=p3: pltpu.make_async_remote_copy(
                src_ref=it.src_view(srcs[it.src], p), dst_ref=it.dst_view(outs[it.out], me),
                send_sem=ssem.at[s], recv_sem=rsem.at[s], device_id=p3, device_id_type=MESH)
            recv = lambda it=it, s=s, p=p, p3=p3: pltpu.make_async_remote_copy(
                src_ref=it.src_view(srcs[it.src], p), dst_ref=it.dst_view(outs[it.out], p),
                send_sem=ssem.at[s], recv_sem=rsem.at[s], device_id=p3, device_id_type=MESH)
            if act == "start":
                _when(it.pred, p, lambda send=send: send().start())
            else:
                _when(it.pred, me, lambda recv=recv: recv().wait_recv())
                _when(it.pred, p, lambda send=send: send().wait_send())


def _sem_scratch(items):
    return [pltpu.SemaphoreType.DMA((_n_sems(items),)), pltpu.SemaphoreType.DMA((_n_sems(items),)),
            pltpu.SemaphoreType.DMA((len(items),))]


def _whole(ref, dest):
    return ref


def _slot(ref, sender):
    return ref.at[sender]


def _rows_of(rows):
    return lambda ref, dev: ref.at[pl.ds(dev * rows, rows), :]


def _pool_rows_of(rows):
    return lambda ref, dev: ref.at[:, pl.ds(dev * rows, rows), :]


def _gather_item(src, out, dst_view):
    return _Item(src, out, _whole, dst_view)


def _scatter_item(src, out, src_view):
    return _Item(src, out, src_view, _slot)


W_IN_BLOCK = 256
W_IN_SHARD = N_IN // N_DEV
SSM_BLOCKS = (2 * D // W_IN_BLOCK, 4 * D // W_IN_BLOCK)


def _w_in_block_item(src, out, j, ssm_part):
    def block(dest):
        return (W_IN_SHARD // W_IN_BLOCK) * dest + j

    def in_ssm(dest):
        b = block(dest)
        return (b >= SSM_BLOCKS[0]) & (b < SSM_BLOCKS[1])

    def src_view(ref, dest):
        b = block(dest)
        local = b - SSM_BLOCKS[0] if ssm_part else jnp.where(b < SSM_BLOCKS[0], b, b - (SSM_BLOCKS[1] - SSM_BLOCKS[0]))
        local = jnp.clip(local, 0, ref.shape[1] // W_IN_BLOCK - 1)
        return ref.at[:, pl.ds(local * W_IN_BLOCK, W_IN_BLOCK)]

    def dst_view(ref, sender):
        return ref.at[sender, :, pl.ds(j * W_IN_BLOCK, W_IN_BLOCK)]

    pred = in_ssm if ssm_part else (lambda dest: jnp.logical_not(in_ssm(dest)))
    return _Item(src, out, src_view, dst_view, pred)


def _s5_discretise(a_re, a_im, log_dt, b_re_t, b_im_t):
    dt = jnp.exp(log_dt)
    lam_re = jnp.minimum(a_re, -1e-4)
    lam_im = a_im
    mag = jnp.exp(lam_re * dt)
    abar_re = mag * jnp.cos(lam_im * dt)
    abar_im = mag * jnp.sin(lam_im * dt)
    den = lam_re * lam_re + lam_im * lam_im
    num_re = abar_re - 1.0
    f_re = (num_re * lam_re + abar_im * lam_im) / den
    f_im = (abar_im * lam_re - num_re * lam_im) / den
    f_re, f_im = f_re[:, None, :], f_im[:, None, :]
    bb_re = f_re * b_re_t - f_im * b_im_t
    bb_im = f_re * b_im_t + f_im * b_re_t
    return abar_re, abar_im, bb_re, bb_im


def _group_masks():
    spread = lax.broadcasted_iota(jnp.int32, (G_P, 16 * G_P), 1) % G_P == lax.broadcasted_iota(
        jnp.int32, (G_P, 16 * G_P), 0)
    own = lax.broadcasted_iota(jnp.int32, (16 * G_H, 16 * G_P), 0) // G_H == lax.broadcasted_iota(
        jnp.int32, (16 * G_H, 16 * G_P), 1) // G_P
    return spread, own


def _s5_prep_structs(n_pow):
    return (jax.ShapeDtypeStruct((N_Q, 16 * G_H, Q_W), BF16), jax.ShapeDtypeStruct((N_Q, 16 * G_H, Q_W), BF16),
            jax.ShapeDtypeStruct((n_pow, GROUPS, G_P), F32), jax.ShapeDtypeStruct((n_pow, GROUPS, G_P), F32))


def _s5_prep_body(ar_ref, ai_ref, ld_ref, br_ref, bi_ref, cr_ref, ci_ref, wb_ref, wct_ref, pr_ref, pi_ref):
    abar_re, abar_im, bb_re, bb_im = _s5_discretise(ar_ref[...], ai_ref[...], ld_ref[...], br_ref[...], bi_ref[...])
    spread, own = _group_masks()
    spread = spread.astype(BF16)
    for ref, parts in ((wb_ref, (bb_re, bb_im)), (wct_ref, (cr_ref[...], -ci_ref[...]))):
        for half, t in enumerate(parts):
            for q in range(N_Q):
                blocks = t[q * 16:(q + 1) * 16].reshape(16 * G_H, G_P).astype(BF16)
                dense = jnp.where(own, _dot(blocks, spread), 0.0)
                ref[q, :, half * (Q_W // 2):(half + 1) * (Q_W // 2)] = dense.astype(BF16)
    p_re, p_im = abar_re, abar_im
    pr_ref[0] = p_re
    pi_ref[0] = p_im
    for k in range(1, pr_ref.shape[0]):
        p_re, p_im = p_re * abar_re - p_im * abar_im, p_re * abar_im + p_im * abar_re
        pr_ref[k] = p_re
        pi_ref[k] = p_im


def _s5_prep_bwd(a_re, a_im, log_dt, b_re_t, b_im_t, d_abar_re, d_abar_im, d_bb_re, d_bb_im):
    def body(ar_ref, ai_ref, ld_ref, br_ref, bi_ref, dar_ref, dai_ref, dbr_ref, dbi_ref,
             gar_ref, gai_ref, gld_ref, gbr_ref, gbi_ref):
        _, vjp = jax.vjp(_s5_discretise, ar_ref[...], ai_ref[...], ld_ref[...], br_ref[...], bi_ref[...])
        g = vjp((dar_ref[...], dai_ref[...], dbr_ref[...], dbi_ref[...]))
        gar_ref[...] = g[0]
        gai_ref[...] = g[1]
        gld_ref[...] = g[2]
        gbr_ref[...] = g[3]
        gbi_ref[...] = g[4]

    vm = pl.BlockSpec(memory_space=pltpu.VMEM)
    ins = (a_re, a_im, log_dt, b_re_t, b_im_t)
    return _pcall(body, name="s5_prep_bwd",
                  out_shape=tuple(jax.ShapeDtypeStruct(a.shape, F32) for a in ins),
                  in_specs=[vm] * 9, out_specs=tuple([vm] * 5), compiler_params=_params(),
                  )(*ins, d_abar_re, d_abar_im, d_bb_re, d_bb_im)


def _state_layout(re, im):
    lead = re.shape[:-2]
    r = re.reshape(lead + (N_Q, 1, 16 * G_P))
    i = im.reshape(lead + (N_Q, 1, 16 * G_P))
    return jnp.concatenate([r, i], axis=-2).reshape(lead + (N_STATE,))


def _state_unlayout(v):
    v4 = v.reshape(N_Q, 2, 16, G_P)
    return v4[:, 0].reshape(GROUPS, G_P), v4[:, 1].reshape(GROUPS, G_P)


def _perm_matrix(tb):
    k_steps = tb // SUBLANES
    r = jnp.arange(tb)
    src = (r % SUBLANES) * k_steps + r // SUBLANES
    return (src[:, None] == jnp.arange(tb)[None, :]).astype(BF16)


def _lane_chunks(q):
    for lc in range(Q_W // 2 // LANE_CHUNK):
        re = q * Q_W + lc * LANE_CHUNK
        yield re, re + Q_W // 2


def _steps(lo, hi, body, init):
    if hi - lo <= SCAN_UNROLL:
        for k in range(lo, hi):
            init = body(k, init)
        return init
    trips = (hi - lo) // SCAN_UNROLL

    def trip(j, carry):
        for u in range(SCAN_UNROLL):
            carry = body(lo + j * SCAN_UNROLL + u, carry)
        return carry

    carry = lax.fori_loop(0, trips, trip, init)
    for k in range(lo + trips * SCAN_UNROLL, hi):
        carry = body(k, carry)
    return carry


def _tile(k):
    if isinstance(k, int):
        return pl.ds(k * SUBLANES, SUBLANES)
    return pl.ds(pl.multiple_of(k * SUBLANES, SUBLANES), SUBLANES)


def _scan_forward(q, s_ref, p_ref, carry_ref, enter_ref, fin_ref, k_steps):
    for re, im in _lane_chunks(q):
        lr, li = pl.ds(re, LANE_CHUNK), pl.ds(im, LANE_CHUNK)
        a_re = jnp.broadcast_to(p_ref[0:1, lr], (SUBLANES, LANE_CHUNK))
        a_im = jnp.broadcast_to(p_ref[0:1, li], (SUBLANES, LANE_CHUNK))

        def local(k, st):
            sr, si = st
            rows = _tile(k)
            nr = a_re * sr - a_im * si + s_ref[rows, lr]
            ni = a_re * si + a_im * sr + s_ref[rows, li]
            s_ref[rows, lr] = nr
            s_ref[rows, li] = ni
            return nr, ni

        zero = jnp.zeros((SUBLANES, LANE_CHUNK), F32)
        fr, fi = _steps(0, k_steps, local, (zero, zero))
        fin_ref[:, lr] = fr
        fin_ref[:, li] = fi
        ak_re, ak_im = p_ref[k_steps - 1:k_steps, lr], p_ref[k_steps - 1:k_steps, li]
        c_re, c_im = carry_ref[:, lr], carry_ref[:, li]
        for seg in range(SUBLANES):
            enter_ref[seg:seg + 1, lr] = c_re
            enter_ref[seg:seg + 1, li] = c_im
            f_re, f_im = fin_ref[seg:seg + 1, lr], fin_ref[seg:seg + 1, li]
            c_re, c_im = f_re + ak_re * c_re - ak_im * c_im, f_im + ak_re * c_im + ak_im * c_re
        carry_ref[:, lr] = c_re
        carry_ref[:, li] = c_im
        e_re, e_im = enter_ref[:, lr], enter_ref[:, li]

        def fix(k, _):
            rows = _tile(k)
            p_re = p_ref[pl.ds(k, 1), lr]
            p_im = p_ref[pl.ds(k, 1), li]
            s_ref[rows, lr] = s_ref[rows, lr] + (p_re * e_re - p_im * e_im)
            s_ref[rows, li] = s_ref[rows, li] + (p_re * e_im + p_im * e_re)
            return 0

        _steps(0, k_steps, fix, 0)


def _scan_backward(q, g_ref, s_ref, p_ref, carry_ref, s_in_ref, fin_ref, da_ref, k_steps):
    seg_id = lax.broadcasted_iota(jnp.int32, (SUBLANES, LANE_CHUNK), 0)
    for re, im in _lane_chunks(q):
        lr, li = pl.ds(re, LANE_CHUNK), pl.ds(im, LANE_CHUNK)
        a_re = jnp.broadcast_to(p_ref[0:1, lr], (SUBLANES, LANE_CHUNK))
        a_im = jnp.broadcast_to(p_ref[0:1, li], (SUBLANES, LANE_CHUNK))

        def local(j, st):
            sr, si = st
            rows = _tile(k_steps - 1 - j)
            nr = a_re * sr + a_im * si + g_ref[rows, lr]
            ni = a_re * si - a_im * sr + g_ref[rows, li]
            g_ref[rows, lr] = nr
            g_ref[rows, li] = ni
            return nr, ni

        zero = jnp.zeros((SUBLANES, LANE_CHUNK), F32)
        fr, fi = _steps(0, k_steps, local, (zero, zero))
        fin_ref[:, lr] = fr
        fin_ref[:, li] = fi
        ak_re, ak_im = p_ref[k_steps - 1:k_steps, lr], p_ref[k_steps - 1:k_steps, li]
        c_re, c_im = carry_ref[:, lr], carry_ref[:, li]
        lam_in = [None] * SUBLANES
        for seg in reversed(range(SUBLANES)):
            lam_in[seg] = (c_re, c_im)
            f_re, f_im = fin_ref[seg:seg + 1, lr], fin_ref[seg:seg + 1, li]
            c_re, c_im = f_re + ak_re * c_re + ak_im * c_im, f_im + ak_re * c_im - ak_im * c_re
        carry_ref[:, lr] = c_re
        carry_ref[:, li] = c_im
        for seg in range(SUBLANES):
            fin_ref[seg:seg + 1, lr] = lam_in[seg][0]
            fin_ref[seg:seg + 1, li] = lam_in[seg][1]
        e_re, e_im = fin_ref[:, lr], fin_ref[:, li]

        def fix_with(k, acc, sp_re, sp_im):
            acc_re, acc_im = acc
            rows = _tile(k)
            p_re = p_ref[pl.ds(k_steps - 1 - k, 1), lr]
            p_im = p_ref[pl.ds(k_steps - 1 - k, 1), li]
            l_re = g_ref[rows, lr] + (p_re * e_re + p_im * e_im)
            l_im = g_ref[rows, li] + (p_re * e_im - p_im * e_re)
            g_ref[rows, lr] = l_re
            g_ref[rows, li] = l_im
            return acc_re + (l_re * sp_re + l_im * sp_im), acc_im + (l_im * sp_re - l_re * sp_im)

        def fix(k, acc):
            prev = _tile(k - 1)
            return fix_with(k, acc, s_ref[prev, lr], s_ref[prev, li])

        last = _tile(k_steps - 1)
        before_re = jnp.where(seg_id == 0, s_in_ref[:, lr], pltpu.roll(s_ref[last, lr], 1, axis=0))
        before_im = jnp.where(seg_id == 0, s_in_ref[:, li], pltpu.roll(s_ref[last, li], 1, axis=0))
        acc = fix_with(0, (zero, zero), before_re, before_im)
        acc_re, acc_im = _steps(1, k_steps, fix, acc)
        da_ref[:, lr] = da_ref[:, lr] + jnp.sum(acc_re, axis=0, keepdims=True)
        da_ref[:, li] = da_ref[:, li] + jnp.sum(acc_im, axis=0, keepdims=True)


def _prenorm(x, mod3, norm_pre):
    xn, r = _rms_parts(x)
    return xn, r, xn * norm_pre * (1.0 + mod3[1:2, :]) + mod3[0:1, :]


CHIP_FLIPS = (4, 2, 6)


def _shard_order(me):
    flips = [0, 1] + [f + c for f in CHIP_FLIPS for c in (0, 1)]
    return jnp.stack([me ^ f for f in flips]).astype(jnp.int32)


def _in_proj(x, mod3, norm_pre, w_in_s, shards):
    rows = x.shape[0]
    tb = _tb(rows, 2048)
    nblk = rows // tb
    n_sh = len(shards)
    last_step = N_DEV - 1
    items = [_gather_item(0, 0, _pool_rows_of(shards[0].shape[1]))] + \
            [_gather_item(t, t, _rows_of(shards[t].shape[0])) for t in range(1, n_sh)]

    def body(order_ref, x_ref, mod_ref, np_ref, w_src, *rest):
        src_refs, proj_ref, w_full, out_refs = rest[:n_sh], rest[n_sh], rest[n_sh + 1], rest[n_sh + 2:2 * n_sh + 2]
        h_scr, wg, ssem, rsem, lsem, *sems = rest[2 * n_sh + 2:]
        s, i = pl.program_id(0), pl.program_id(1)
        me3 = _me()
        me = _flat(me3)
        sibling = _peer(1)

        def own_copy(slot, k):
            return pltpu.make_async_remote_copy(src_ref=w_src, dst_ref=wg.at[me], send_sem=ssem.at[slot],
                                                recv_sem=rsem.at[slot], device_id=_peer(k), device_id_type=MESH)

        def passed_copy(j):
            p = _flat(_peer(CHIP_FLIPS[j]))
            return pltpu.make_async_remote_copy(src_ref=wg.at[p], dst_ref=wg.at[p], send_sem=ssem.at[4 + j],
                                                recv_sem=rsem.at[4 + j], device_id=sibling, device_id_type=MESH)

        def arrival(slot, flip):
            p = _flat(_peer(flip))
            pltpu.make_async_remote_copy(src_ref=w_src, dst_ref=wg.at[p], send_sem=ssem.at[slot],
                                         recv_sem=rsem.at[slot], device_id=sibling, device_id_type=MESH).wait_recv()

        def keep(t):
            p = order_ref[t]
            return pltpu.make_async_copy(wg.at[p], w_full.at[:, pl.ds(p * W_IN_SHARD, W_IN_SHARD)], lsem.at[1 + t])

        first = i == 0
        for t in range(last_step):
            pl.when(first & (s == t + 1))(lambda t=t: keep(t).start())

        @pl.when(first & (s == 0))
        def _():
            mine = pltpu.make_async_copy(w_src, wg.at[me], lsem.at[0])
            mine.start()
            own_copy(0, 1).start()
            for j, f in enumerate(CHIP_FLIPS[:2]):
                own_copy(1 + j, f).start()
            mine.wait()

        @pl.when(first & (s == 1))
        def _():
            arrival(0, 1)

        for j, f in enumerate(CHIP_FLIPS):
            @pl.when(first & (s == 2 + 2 * j))
            def _(j=j, f=f):
                arrival(1 + j, f)
                passed_copy(j).start()
                if j == 0:
                    own_copy(3, CHIP_FLIPS[2]).start()

            @pl.when(first & (s == 3 + 2 * j))
            def _(j=j, f=f):
                arrival(4 + j, f + 1)

        @pl.when(first & (s == last_step - 1))
        def _():
            _hosted_copies(items, src_refs, out_refs, *sems, act="start")

        rows_i = pl.ds(pl.multiple_of(i * tb, tb), tb)

        @pl.when(s == 0)
        def _():
            _, _, h = _prenorm(x_ref[...], mod_ref[...], np_ref[...])
            h_scr[rows_i, :] = h.astype(BF16)

        proj_ref[...] = _dot(h_scr[rows_i, :], wg[order_ref[s]]).astype(BF16)

        @pl.when((s == last_step) & (i == nblk - 1))
        def _():
            own_copy(0, 1).wait_send()
            for j, f in enumerate(CHIP_FLIPS):
                own_copy(1 + j, f).wait_send()
                passed_copy(j).wait_send()
            keep(last_step).start()
            for t in range(N_DEV):
                keep(t).wait()
            _hosted_copies(items, src_refs, out_refs, *sems, act="wait")

    full = [jax.ShapeDtypeStruct((4, 256, 256), BF16)] + [jax.ShapeDtypeStruct((D, D), BF16)] * (n_sh - 1)
    grid_spec = pltpu.PrefetchScalarGridSpec(
        num_scalar_prefetch=1, grid=(N_DEV, nblk),
        in_specs=[pl.BlockSpec((tb, D), lambda s, i, order: (jnp.where(s == 0, i, nblk - 1), 0)),
                  pl.BlockSpec((3, D), lambda s, i, order: (0, 0)), pl.BlockSpec((1, D), lambda s, i, order: (0, 0)),
                  ANY] + [ANY] * n_sh,
        out_specs=(pl.BlockSpec((tb, W_IN_SHARD), lambda s, i, order: (i, order[s])), ANY, *([ANY] * n_sh)),
        scratch_shapes=[pltpu.VMEM((rows, D), BF16), pltpu.VMEM((N_DEV, D, W_IN_SHARD), BF16),
                        pltpu.SemaphoreType.DMA((N_DEV - 1,)), pltpu.SemaphoreType.DMA((N_DEV - 1,)),
                        pltpu.SemaphoreType.DMA((1 + N_DEV,))] + _sem_scratch(items))
    return _pcall(body, name="in_proj", grid_spec=grid_spec,
                  out_shape=(jax.ShapeDtypeStruct((rows, N_IN), BF16), jax.ShapeDtypeStruct((D, N_IN), BF16), *full),
                  compiler_params=_params(("arbitrary", "arbitrary")),
                  )(_shard_order(_flat(_me())), x, mod3, norm_pre, w_in_s, *shards)


def _pool_windows(ext, tb, first_row):
    inv_counts = _inv_counts(tb, first_row)
    pooled = []
    for g, w in enumerate(POOL_WINDOWS):
        acc = ext[:, g * 256:(g + 1) * 256]
        tok = acc[HALO:, :]
        s = 1
        while s < w:
            acc = acc + pltpu.roll(acc, s, axis=0)
            s *= 2
        pooled.append(acc[HALO:, :] * inv_counts[g] - tok)
    return pooled, inv_counts


def _inv_counts(tb, first_row):
    pos = (first_row + lax.broadcasted_iota(jnp.int32, (tb, 1), 0) + 1).astype(F32)
    return [1.0 / jnp.minimum(pos, float(w)) for w in POOL_WINDOWS]


def _pool_fwd(proj, pool_w, pool_scale):
    rows = proj.shape[0]
    tb = _tb(rows, 1024)
    hb = tb // HALO

    def body(u_ref, halo_ref, z_ref, pw_ref, ps_ref, y_ref, pooled_ref):
        i = pl.program_id(0)
        u = u_ref[...].astype(F32)
        halo = jnp.where(i > 0, halo_ref[...].astype(F32), 0.0)
        pooled, _ = _pool_windows(jnp.concatenate([halo, u], axis=0), tb, i * tb)
        silu_z, _ = _silu_parts(z_ref[...].astype(F32))
        for g in range(4):
            cols = slice(g * 256, (g + 1) * 256)
            pooled_b = pooled[g].astype(BF16)
            pooled_ref[:, cols] = pooled_b
            mixed = _dot(pooled_b, pw_ref[g])
            y_ref[:, cols] = (mixed * ps_ref[:, cols] * silu_z[:, cols]).astype(BF16)

    blk = pl.BlockSpec((tb, D), lambda i: (i, 0))
    return _pcall(body, name="pool_fwd", grid=(rows // tb,),
                  out_shape=(jax.ShapeDtypeStruct((rows, D), BF16), jax.ShapeDtypeStruct((rows, D), BF16)),
                  in_specs=[blk, pl.BlockSpec((HALO, D), lambda i: (jnp.maximum(i * hb - 1, 0), 0)),
                            pl.BlockSpec((tb, D), lambda i: (i, 1)),
                            _full((4, 256, 256)), _full((1, D))],
                  out_specs=(blk, blk),
                  compiler_params=_params(("arbitrary",)))(proj, proj, proj, pool_w, pool_scale)


def _ssm_fwd(proj, pm, pmt, wb, wct, ptab, dvec, glu_w, glu_b, shards):
    rows = proj.shape[0]
    tb = pm.shape[0]
    k_steps = tb // SUBLANES
    nblk = rows // tb
    n_sh = len(shards)
    items = [_gather_item(t, t, _rows_of(shards[t].shape[0])) for t in range(n_sh)]

    def body(u_ref, z_ref, pm_ref, pmt_ref, wb_ref, wct_ref, p_ref, d_ref, gw_ref, gb_ref, *rest):
        src_refs = rest[:n_sh]
        y_ref, ys_ref, carry_out_ref, s_ref, gate_ref, zp_ref, up_ref = rest[n_sh:n_sh + 7]
        out_refs = rest[n_sh + 7:2 * n_sh + 7]
        carry_ref, enter_ref, fin_ref, *sems = rest[2 * n_sh + 7:]

        @pl.when(pl.program_id(0) == 0)
        def _():
            _hosted_copies(items, src_refs, out_refs, *sems, act="start")
            carry_ref[...] = jnp.zeros_like(carry_ref)

        carry_out_ref[...] = carry_ref[...]
        up = _dot(pm_ref[...], u_ref[...]).astype(BF16)
        up_ref[...] = up

        for q in range(N_Q):
            s_ref[:, q * Q_W:(q + 1) * Q_W] = _dot(up[:, q * 256:(q + 1) * 256], wb_ref[q])
        for q in range(N_Q):
            _scan_forward(q, s_ref, p_ref, carry_ref, enter_ref, fin_ref, k_steps)
        for q in range(N_Q):
            cols = slice(q * 256, (q + 1) * 256)
            y = _dot_nt(s_ref[:, q * Q_W:(q + 1) * Q_W].astype(BF16), wct_ref[q])
            ys_ref[:, cols] = y + d_ref[:, cols] * up[:, cols].astype(F32)
        yg, _ = _gelu_parts(ys_ref[...])
        gate = jax.nn.sigmoid(_dot(yg.astype(BF16), gw_ref[...]) + gb_ref[...])
        gate_ref[...] = gate
        zp = _dot(pm_ref[...], z_ref[...])
        zp_ref[...] = zp.astype(BF16)
        silu_z, _ = _silu_parts(zp)
        y_ref[...] = _dot(pmt_ref[...], (yg * gate * silu_z).astype(BF16)).astype(BF16)

        @pl.when(pl.program_id(0) == nblk - 1)
        def _():
            _hosted_copies(items, src_refs, out_refs, *sems, act="wait")

    return _pcall(body, name="ssm_fwd", grid=(nblk,),
                  out_shape=(jax.ShapeDtypeStruct((rows, D), BF16), jax.ShapeDtypeStruct((rows, D), F32),
                             jax.ShapeDtypeStruct((nblk, 1, N_STATE), F32),
                             jax.ShapeDtypeStruct((rows, N_STATE), F32),
                             jax.ShapeDtypeStruct((rows, D), F32), jax.ShapeDtypeStruct((rows, D), BF16),
                             jax.ShapeDtypeStruct((rows, D), BF16),
                             *[jax.ShapeDtypeStruct((D, D), BF16)] * n_sh),
                  in_specs=[pl.BlockSpec((tb, D), lambda i: (i, 2)), pl.BlockSpec((tb, D), lambda i: (i, 3)),
                            _full((tb, tb)), _full((tb, tb)),
                            _full((N_Q, 256, Q_W), single=True), _full((N_Q, 256, Q_W), single=True),
                            _full((k_steps, N_STATE)), _full((1, D)), _full((D, D), single=True), _full((1, D))] +
                           [ANY] * n_sh,
                  out_specs=(pl.BlockSpec((tb, D), lambda i: (i, 0)), pl.BlockSpec((tb, D), lambda i: (i, 0)),
                             pl.BlockSpec((None, 1, N_STATE), lambda i: (i, 0, 0)),
                             pl.BlockSpec((tb, N_STATE), lambda i: (i, 0)),
                             *[pl.BlockSpec((tb, D), lambda i: (i, 0))] * 3, *([ANY] * n_sh)),
                  scratch_shapes=[pltpu.VMEM((1, N_STATE), F32),
                                  pltpu.VMEM((SUBLANES, N_STATE), F32), pltpu.VMEM((SUBLANES, N_STATE), F32)] +
                                 _sem_scratch(items),
                  compiler_params=_params(("arbitrary",)))(proj, proj, pm, pmt, wb, wct, ptab, dvec, glu_w, glu_b,
                                                           *shards)


def _head(x, target, proj, y_pool, y_ssm, mod3, norm_post, wbp, wbs, wout):
    rows = x.shape[0]
    tb = _tb(rows, 256)
    nblk = rows // tb
    n_feat = float(D)

    def body(x_ref, t_ref, gp_ref, gs_ref, yp_ref, ys_ref, mod_ref, npost_ref, wbp_ref, wbs_ref, wout_ref,
             loss_ref, dy_ref, dyp_ref, dys_ref, dg_ref, dwbp_hbm, dwbs_hbm, dwout_hbm, vec_ref,
             acc_bp, acc_bs, acc_out, acc_loss, acc_vec):
        i = pl.program_id(0)

        @pl.when(i == 0)
        def _():
            acc_bp[...] = jnp.zeros_like(acc_bp)
            acc_bs[...] = jnp.zeros_like(acc_bs)
            acc_out[...] = jnp.zeros_like(acc_out)
            acc_loss[...] = jnp.zeros_like(acc_loss)
            acc_vec[...] = jnp.zeros_like(acc_vec)

        gate = mod_ref[2:3, :]
        npost = npost_ref[...]
        yp, ys = yp_ref[...], ys_ref[...]
        sgp = jax.nn.sigmoid(gp_ref[...].astype(F32))
        sgs = jax.nn.sigmoid(gs_ref[...].astype(F32))
        pb = _dot(yp, wbp_ref[...])
        psm = _dot(ys, wbs_ref[...])
        mb = (sgp * pb + sgs * psm).astype(BF16)
        out = _dot(mb, wout_ref[...])
        on, r = _rms_parts(out)
        normed = on * npost
        diff = x_ref[...] + gate * normed - t_ref[...]
        acc_loss[...] += jnp.sum(diff * diff, axis=0, keepdims=True)
        dy = diff * (1.0 / n_feat)
        dy_ref[...] = dy
        acc_vec[0:1, :] += jnp.sum(dy * normed, axis=0, keepdims=True)
        dn = dy * gate
        acc_vec[1:2, :] += jnp.sum(dn * on, axis=0, keepdims=True)
        dout = _rms_bwd(dn * npost, on, r).astype(BF16)
        dm = _dot_nt(dout, wout_ref[...])
        dpb = (dm * sgp).astype(BF16)
        dps = (dm * sgs).astype(BF16)
        dg_ref[:, :D] = (dm * pb * sgp * (1.0 - sgp)).astype(BF16)
        dg_ref[:, D:] = (dm * psm * sgs * (1.0 - sgs)).astype(BF16)
        dyp_ref[...] = _dot_nt(dpb, wbp_ref[...]).astype(BF16)
        dys_ref[...] = _dot_nt(dps, wbs_ref[...]).astype(BF16)
        acc_out[...] += _dot_tn(mb, dout)
        acc_bp[...] += _dot_tn(yp, dpb)
        acc_bs[...] += _dot_tn(ys, dps)

        @pl.when(i == nblk - 1)
        def _():
            loss_ref[...] = 0.5 / n_feat * jnp.sum(acc_loss[...], axis=1, keepdims=True)
            vec_ref[...] = acc_vec[...]
            pltpu.sync_copy(acc_bp, dwbp_hbm)
            pltpu.sync_copy(acc_bs, dwbs_hbm)
            pltpu.sync_copy(acc_out, dwout_hbm)

    row = lambda c: pl.BlockSpec((tb, D), lambda i: (i, c))
    w = _full((D, D), single=True)
    return _pcall(body, name="head", grid=(nblk,),
                  out_shape=(jax.ShapeDtypeStruct((1, 1), F32), jax.ShapeDtypeStruct((rows, D), F32),
                             jax.ShapeDtypeStruct((rows, D), BF16), jax.ShapeDtypeStruct((rows, D), BF16),
                             jax.ShapeDtypeStruct((rows, 2 * D), BF16),
                             jax.ShapeDtypeStruct((D, D), F32), jax.ShapeDtypeStruct((D, D), F32),
                             jax.ShapeDtypeStruct((D, D), F32), jax.ShapeDtypeStruct((2, D), F32)),
                  in_specs=[row(0), row(0), row(4), row(5), row(0), row(0), _full((3, D)), _full((1, D)), w, w, w],
                  out_specs=(_full((1, 1)), row(0), row(0), row(0), pl.BlockSpec((tb, 2 * D), lambda i: (i, 0)),
                             ANY, ANY, ANY, _full((2, D))),
                  scratch_shapes=[pltpu.VMEM((D, D), F32), pltpu.VMEM((D, D), F32), pltpu.VMEM((D, D), F32),
                                  pltpu.VMEM((1, D), F32), pltpu.VMEM((2, D), F32)],
                  compiler_params=_params(("arbitrary",)))(x, target, proj, proj, y_pool, y_ssm, mod3, norm_post,
                                                           wbp, wbs, wout)


def _glu_bwd(dys, zp, ys_pre, gate, pm, pmt, glu_w):
    rows = dys.shape[0]
    tb = pm.shape[0]
    nblk = rows // tb

    def body(dys_ref, z_ref, ysp_ref, sg_ref, pm_ref, pmt_ref, gw_ref, dyp_ref, dz_ref, dgw_hbm, dgb_ref,
             acc_w, acc_b):
        i = pl.program_id(0)

        @pl.when(i == 0)
        def _():
            acc_w[...] = jnp.zeros_like(acc_w)
            acc_b[...] = jnp.zeros_like(acc_b)

        d_out = _dot(pm_ref[...], dys_ref[...])
        yg, dgelu = _gelu_parts(ysp_ref[...])
        ygb = yg.astype(BF16)
        sg = sg_ref[...]
        silu_z, dsilu_z = _silu_parts(z_ref[...].astype(F32))
        dz = d_out * (yg * sg) * dsilu_z
        dz_ref[...] = _dot(pmt_ref[...], dz.astype(BF16)).astype(BF16)
        dglu = d_out * silu_z
        dq = dglu * yg * sg * (1.0 - sg)
        dqb = dq.astype(BF16)
        acc_b[...] += jnp.sum(dq, axis=0, keepdims=True)
        acc_w[...] += _dot_tn(ygb, dqb)
        dyg = dglu * sg + _dot_nt(dqb, gw_ref[...])
        dyp_ref[...] = (dyg * dgelu).astype(BF16)

        @pl.when(i == nblk - 1)
        def _():
            dgb_ref[...] = acc_b[...]
            pltpu.sync_copy(acc_w, dgw_hbm)

    row = lambda c: pl.BlockSpec((tb, D), lambda i: (i, c))
    return _pcall(body, name="glu_bwd", grid=(nblk,),
                  out_shape=(jax.ShapeDtypeStruct((rows, D), BF16), jax.ShapeDtypeStruct((rows, D), BF16),
                             jax.ShapeDtypeStruct((D, D), F32), jax.ShapeDtypeStruct((1, D), F32)),
                  in_specs=[row(0), row(0), row(0), row(0), _full((tb, tb)), _full((tb, tb)),
                            _full((D, D), single=True)],
                  out_specs=(row(0), row(0), ANY, _full((1, D))),
                  scratch_shapes=[pltpu.VMEM((D, D), F32), pltpu.VMEM((1, D), F32)],
                  compiler_params=_params(("arbitrary",)))(dys, zp, ys_pre, gate, pm, pmt, glu_w)


def _ssm_bwd(dyp, up, states, carries, pmt, wb, wct, ptab, dvec, mat_grads, dpool_w, dw_in_rest):
    rows = dyp.shape[0]
    tb = pmt.shape[0]
    k_steps = tb // SUBLANES
    nblk = rows // tb
    n_mat = len(mat_grads)
    hosted = [*mat_grads, dpool_w, dw_in_rest]
    n_h = len(hosted)
    shard_rows = D // N_DEV
    pool_rows = dpool_w.shape[1] // N_DEV
    items = [_scatter_item(t, t, _rows_of(shard_rows)) for t in range(n_mat)] + \
            [_scatter_item(n_mat, n_mat, _pool_rows_of(pool_rows))] + \
            [_w_in_block_item(n_mat + 1, n_mat + 1, j, ssm_part=False) for j in range(W_IN_SHARD // W_IN_BLOCK)]
    n_in, n_out = 9, 5

    def body(*refs):
        dyp_ref, u_ref, s_ref, cin_ref, pmt_ref, wb_ref, wct_ref, p_ref, d_ref = refs[:n_in]
        src_refs = refs[n_in:n_in + n_h]
        du_ref, dbb_ref, dcc_ref, da_ref, dd_ref = refs[n_in + n_h:n_in + n_h + n_out]
        recv_refs = refs[n_in + n_h + n_out:n_in + 2 * n_h + n_out]
        (g_ref, carry_b, fin_ref, acc_wb, acc_wct, acc_da, acc_dd, dup_ref,
         *sems) = refs[n_in + 2 * n_h + n_out:]
        i = pl.program_id(0)

        @pl.when(i == 0)
        def _():
            _hosted_copies(items, src_refs, recv_refs, *sems, act="start")
            carry_b[...] = jnp.zeros_like(carry_b)
            acc_wb[...] = jnp.zeros_like(acc_wb)
            acc_wct[...] = jnp.zeros_like(acc_wct)
            acc_da[...] = jnp.zeros_like(acc_da)
            acc_dd[...] = jnp.zeros_like(acc_dd)

        def own_products(acc, q, chan, state, base):
            chan_t = chan.T
            for j in range(16 // 2):
                r = slice(j * 2 * G_H, (j + 1) * 2 * G_H)
                re = base + j * 128
                slab = jnp.concatenate([state[:, re:re + 128], state[:, re + Q_W // 2:re + Q_W // 2 + 128]],
                                       axis=1).astype(BF16)
                acc[q, r, :] += _dot(chan_t[r, :], slab)

        dy = dyp_ref[...]
        up = u_ref[...]
        acc_dd[...] += jnp.sum(dy.astype(F32) * up.astype(F32), axis=0, keepdims=True)
        for q in range(N_Q):
            cols = slice(q * 256, (q + 1) * 256)
            g_ref[:, q * Q_W:(q + 1) * Q_W] = _dot(dy[:, cols], wct_ref[q])
            own_products(acc_wct, q, dy[:, cols], s_ref, q * Q_W)
        for q in range(N_Q):
            _scan_backward(q, g_ref, s_ref, p_ref, carry_b, cin_ref, fin_ref, acc_da, k_steps)
        for q in range(N_Q):
            cols = slice(q * 256, (q + 1) * 256)
            lam = g_ref[:, q * Q_W:(q + 1) * Q_W].astype(BF16)
            own_products(acc_wb, q, up[:, cols], lam, 0)
            dup_ref[:, cols] = (_dot_nt(lam, wb_ref[q]) + d_ref[:, cols] * dy[:, cols].astype(F32)).astype(BF16)
        du_ref[...] = _dot(pmt_ref[...], dup_ref[...]).astype(BF16)

        @pl.when(i == nblk - 1)
        def _():
            da_ref[...] = acc_da[...]
            dd_ref[...] = acc_dd[...]
            lane = lax.broadcasted_iota(jnp.int32, (16 * G_H, 128), 1)
            row = lax.broadcasted_iota(jnp.int32, (16 * G_H, 128), 0)
            own = lane // G_P == (row // G_H) % 2
            spread = (lax.broadcasted_iota(jnp.int32, (G_P, 128), 1) % G_P ==
                      lax.broadcasted_iota(jnp.int32, (G_P, 128), 0)).astype(F32)
            for acc, out in ((acc_wb, dbb_ref), (acc_wct, dcc_ref)):
                for half in range(2):
                    for q in range(N_Q):
                        kept = jnp.where(own, acc[q, :, half * 128:(half + 1) * 128], 0.0)
                        out[half, q] = lax.dot_general(kept, spread, (((1,), (1,)), ((), ())),
                                                       preferred_element_type=F32, precision=lax.Precision.HIGHEST)
            _hosted_copies(items, src_refs, recv_refs, *sems, act="wait")

    rev = lambda c: pl.BlockSpec((tb, D), lambda i: (nblk - 1 - i, c))
    recv = [jax.ShapeDtypeStruct((N_DEV, shard_rows, D), F32)] * n_mat + \
           [jax.ShapeDtypeStruct((N_DEV, dpool_w.shape[0], pool_rows, dpool_w.shape[2]), F32),
            jax.ShapeDtypeStruct((N_DEV, D, W_IN_SHARD), BF16)]
    return _pcall(body, name="ssm_bwd", grid=(nblk,),
                  out_shape=(jax.ShapeDtypeStruct((rows, D), BF16),
                             jax.ShapeDtypeStruct((2, N_Q, 16 * G_H, G_P), F32),
                             jax.ShapeDtypeStruct((2, N_Q, 16 * G_H, G_P), F32),
                             jax.ShapeDtypeStruct((1, N_STATE), F32), jax.ShapeDtypeStruct((1, D), F32), *recv),
                  in_specs=[rev(0), rev(0), pl.BlockSpec((tb, N_STATE), lambda i: (nblk - 1 - i, 0)),
                            pl.BlockSpec((None, 1, N_STATE), lambda i: (nblk - 1 - i, 0, 0)),
                            _full((tb, tb)),
                            _full((N_Q, 256, Q_W), single=True), _full((N_Q, 256, Q_W), single=True),
                            _full((k_steps, N_STATE)), _full((1, D))] + [ANY] * n_h,
                  out_specs=(rev(0), _full((2, N_Q, 16 * G_H, G_P)), _full((2, N_Q, 16 * G_H, G_P)),
                             _full((1, N_STATE)), _full((1, D)), *([ANY] * n_h)),
                  scratch_shapes=[pltpu.VMEM((tb, N_STATE), F32), pltpu.VMEM((1, N_STATE), F32),
                                  pltpu.VMEM((SUBLANES, N_STATE), F32),
                                  pltpu.VMEM((N_Q, 16 * G_H, 256), F32), pltpu.VMEM((N_Q, 16 * G_H, 256), F32),
                                  pltpu.VMEM((1, N_STATE), F32), pltpu.VMEM((1, D), F32),
                                  pltpu.VMEM((tb, D), BF16)] + _sem_scratch(items),
                  compiler_params=_params(("arbitrary",), vmem=60 * 1024 * 1024),
                  )(dyp, up, states, carries, pmt, wb, wct, ptab, dvec, *hosted)


def _pool_bwd(dyp, pooled, proj, pool_w, pool_scale):
    rows = dyp.shape[0]
    tb = _tb(rows, 1024)
    nblk = rows // tb

    def body(dy_ref, pooled_ref, z_ref, pw_ref, ps_ref, dp_ref, dpw_ref, dps_ref, ahead_ref):
        i = pl.program_id(0)
        blk = nblk - 1 - i

        @pl.when(i == 0)
        def _():
            ahead_ref[...] = jnp.zeros_like(ahead_ref)
            dpw_ref[...] = jnp.zeros_like(dpw_ref)
            dps_ref[...] = jnp.zeros_like(dps_ref)

        inv_counts = _inv_counts(tb, blk * tb)
        silu_z, dsilu_z = _silu_parts(z_ref[...].astype(F32))
        dy = dy_ref[...].astype(F32)
        for g, w in enumerate(POOL_WINDOWS):
            cols = slice(g * 256, (g + 1) * 256)
            pooled_b = pooled_ref[:, cols]
            mixed = _dot(pooled_b, pw_ref[g])
            scale = ps_ref[:, cols]
            dp_ref[:, D + g * 256:D + (g + 1) * 256] = (dy[:, cols] * (mixed * scale) * dsilu_z[:, cols]).astype(BF16)
            dms = dy[:, cols] * silu_z[:, cols]
            dps_ref[:, cols] += jnp.sum(dms * mixed, axis=0, keepdims=True)
            dmixed = (dms * scale).astype(BF16)
            dpw_ref[g] += _dot_tn(pooled_b, dmixed)
            dpooled = _dot_nt(dmixed, pw_ref[g])
            ratio = dpooled * inv_counts[g]
            acc = jnp.concatenate([ratio, ahead_ref[:, cols]], axis=0)
            ahead_ref[:, cols] = ratio[:HALO, :]
            s = 1
            while s < w:
                acc = acc + pltpu.roll(acc, tb + HALO - s, axis=0)
                s *= 2
            dp_ref[:, cols] = (acc[:tb, :] - dpooled).astype(BF16)

    rev = lambda c: pl.BlockSpec((tb, D), lambda i: (nblk - 1 - i, c))
    return _pcall(body, name="pool_bwd", grid=(nblk,),
                  out_shape=(jax.ShapeDtypeStruct((rows, 2 * D), BF16), jax.ShapeDtypeStruct((4, 256, 256), F32),
                             jax.ShapeDtypeStruct((1, D), F32)),
                  in_specs=[rev(0), rev(0), rev(1), _full((4, 256, 256)), _full((1, D))],
                  out_specs=(pl.BlockSpec((tb, 2 * D), lambda i: (nblk - 1 - i, 0)), _full((4, 256, 256)),
                             _full((1, D))),
                  scratch_shapes=[pltpu.VMEM((HALO, D), F32)],
                  compiler_params=_params(("arbitrary",)))(dyp, pooled, proj, pool_w, pool_scale)


def _dproj_specs(tb):
    return [pl.BlockSpec((tb, 2 * D), lambda i: (i, 0)), pl.BlockSpec((tb, D), lambda i: (i, 0)),
            pl.BlockSpec((tb, D), lambda i: (i, 0)), pl.BlockSpec((tb, 2 * D), lambda i: (i, 0))]


def _in_proj_bwd_x(x, dy, dpp, dus, dzs, dpg, mod3, norm_pre, w_in, dw_in_ssm, recv_w_in):
    rows = x.shape[0]
    tb = _tb(rows, 512)
    nblk = rows // tb
    items = [_w_in_block_item(0, 0, j, ssm_part=True) for j in range(W_IN_SHARD // W_IN_BLOCK)]
    sums_item = [_Item(0, 0, _whole, _slot)]

    def body(x_ref, dy_ref, dpp_ref, dus_ref, dzs_ref, dpg_ref, mod_ref, np_ref, w_ref,
             dw_src, _, gx_ref, recv_w, recv_sums, vec_ref, ssem, rsem, lsem, *sums_sems):
        src_refs, recv_refs, sems = (dw_src,), (recv_w,), (ssem, rsem, lsem)

        @pl.when(pl.program_id(0) == 0)
        def _():
            _hosted_copies(items, src_refs, recv_refs, *sems, act="start")
            vec_ref[...] = jnp.zeros_like(vec_ref)

        dh = _dot_nt(dpp_ref[...], w_ref[:, 0:2 * D])
        dh += _dot_nt(dus_ref[...], w_ref[:, 2 * D:3 * D])
        dh += _dot_nt(dzs_ref[...], w_ref[:, 3 * D:4 * D])
        dh += _dot_nt(dpg_ref[...], w_ref[:, 4 * D:6 * D])
        xn, r, _ = _prenorm(x_ref[...], mod_ref[...], np_ref[...])
        one_scale = 1.0 + mod_ref[1:2, :]
        vec_ref[0:1, :] += jnp.sum(dh, axis=0, keepdims=True)
        vec_ref[1:2, :] += jnp.sum(dh * xn, axis=0, keepdims=True) * np_ref[...]
        vec_ref[2:3, :] += jnp.sum(dh * xn, axis=0, keepdims=True) * one_scale
        gx_ref[...] = dy_ref[...] + _rms_bwd(dh * (np_ref[...] * one_scale), xn, r)

        @pl.when(pl.program_id(0) == nblk - 1)
        def _():
            _hosted_copies(sums_item, (vec_ref,), (recv_sums,), *sums_sems, act="start")
            _hosted_copies(items, src_refs, recv_refs, *sems, act="wait")
            _hosted_copies(sums_item, (vec_ref,), (recv_sums,), *sums_sems, act="wait")

    row = pl.BlockSpec((tb, D), lambda i: (i, 0))
    recv = (jax.ShapeDtypeStruct(recv_w_in.shape, recv_w_in.dtype), jax.ShapeDtypeStruct((N_DEV, 3, D), F32))
    return _pcall(body, name="in_proj_bwd_x", grid=(nblk,),
                  out_shape=(jax.ShapeDtypeStruct((rows, D), F32), *recv),
                  in_specs=[row, row] + _dproj_specs(tb) + [_full((3, D)), _full((1, D)),
                                                            _full((D, N_IN), single=True)] + [ANY] * 2,
                  out_specs=(row, ANY, ANY),
                  input_output_aliases={10: 1},
                  scratch_shapes=[pltpu.VMEM((3, D), F32)] + _sem_scratch(items) + _sem_scratch(sums_item),
                  compiler_params=_params(("arbitrary",)))(x, dy, dpp, dus, dzs, dpg, mod3, norm_pre, w_in,
                                                           dw_in_ssm, recv_w_in)


def _in_proj_bwd_w(name, x, dparts, mod3, norm_pre, gathered=()):
    rows = x.shape[0]
    tb = _tb(rows, 512)
    nblk = rows // tb
    widths = [p.shape[1] for p in dparts]
    n_p, n_g = len(dparts), len(gathered)
    items = [_Item(t, t, _whole, _slot) for t in range(n_g)]

    def body(x_ref, *rest):
        part_refs, (mod_ref, np_ref) = rest[:n_p], rest[n_p:n_p + 2]
        src_refs, dw_ref = rest[n_p + 2:n_p + 2 + n_g], rest[n_p + 2 + n_g]
        recv_refs, (acc, *sems) = rest[n_p + 3 + n_g:n_p + 3 + 2 * n_g], rest[n_p + 3 + 2 * n_g:]
        i = pl.program_id(0)

        @pl.when(i == 0)
        def _():
            if n_g:
                _hosted_copies(items, src_refs, recv_refs, *sems, act="start")
            acc[...] = jnp.zeros_like(acc)

        _, _, h = _prenorm(x_ref[...], mod_ref[...], np_ref[...])
        ht = h.astype(BF16)
        lo = 0
        for ref, w in zip(part_refs, widths):
            acc[:, lo:lo + w] += _dot_tn(ht, ref[...])
            lo += w

        @pl.when(i == nblk - 1)
        def _():
            dw_ref[...] = acc[...].astype(BF16)
            if n_g:
                _hosted_copies(items, src_refs, recv_refs, *sems, act="wait")

    row = pl.BlockSpec((tb, D), lambda i: (i, 0))
    out = _pcall(body, name=name, grid=(nblk,),
                 out_shape=(jax.ShapeDtypeStruct((D, sum(widths)), BF16),
                            *[jax.ShapeDtypeStruct((N_DEV,) + g.shape, g.dtype) for g in gathered]),
                 in_specs=[row] + [pl.BlockSpec((tb, w), lambda i: (i, 0)) for w in widths] +
                          [_full((3, D)), _full((1, D))] + [ANY] * n_g,
                 out_specs=(_full((D, sum(widths))), *([ANY] * n_g)),
                 scratch_shapes=[pltpu.VMEM((D, sum(widths)), F32)] + (_sem_scratch(items) if n_g else []),
                 compiler_params=_params(("arbitrary",)))(x, *dparts, mod3, norm_pre, *gathered)
    return out if n_g else out[0]


def _adamw_math(w, g, m, v):
    m = ADAM_B1 * m + (1.0 - ADAM_B1) * g
    v = ADAM_B2 * v + (1.0 - ADAM_B2) * (g * g)
    m_hat = m / (1.0 - ADAM_B1 ** ADAM_STEP)
    v_hat = v / (1.0 - ADAM_B2 ** ADAM_STEP)
    delta = -ADAM_LR * (m_hat / (jnp.sqrt(v_hat) + ADAM_EPS) + ADAM_WD * w)
    return delta, m, v


def _sum_sources(ref):
    g = ref[0].astype(F32)
    for s in range(1, N_DEV):
        g = g + ref[s].astype(F32)
    return g


def _adamw_reduce(name, parts, w, m, v):
    r, c = w.shape
    tr = r if r * c <= 256 * 1024 else max(8, (256 * 1024 // c) // 8 * 8)
    while r % tr:
        tr -= 8

    def body(p_ref, w_ref, m_ref, v_ref, g_ref, d_ref, nm_ref, nv_ref):
        g = _sum_sources(p_ref)
        g_ref[...] = g
        d_ref[...], nm_ref[...], nv_ref[...] = _adamw_math(w_ref[...], g, m_ref[...], v_ref[...])

    blk = pl.BlockSpec((tr, c), lambda i: (i, 0))
    return _pcall(body, name=name, grid=(r // tr,),
                  out_shape=tuple([jax.ShapeDtypeStruct((r, c), F32)] * 4),
                  in_specs=[pl.BlockSpec((N_DEV, tr, c), lambda i: (0, i, 0)), blk, blk, blk],
                  out_specs=(blk, blk, blk, blk),
                  compiler_params=_params(("arbitrary",)))(parts, w, m, v)


def _adamw_small(gs, ws, ms, vs):
    n = len(gs)

    def body(*refs):
        ins, outs = refs[:4 * n], refs[4 * n:]
        for t in range(n):
            g_ref, w_ref, m_ref, v_ref = ins[4 * t:4 * t + 4]
            outs[3 * t][...], outs[3 * t + 1][...], outs[3 * t + 2][...] = _adamw_math(
                w_ref[...], g_ref[...], m_ref[...], v_ref[...])

    vm = pl.BlockSpec(memory_space=pltpu.VMEM)
    flat = [a for t in range(n) for a in (gs[t], ws[t], ms[t], vs[t])]
    return _pcall(body, name="adamw_small",
                  out_shape=tuple(jax.ShapeDtypeStruct(w.shape, F32) for w in ws for _ in range(3)),
                  in_specs=[vm] * (4 * n), out_specs=tuple([vm] * (3 * n)), compiler_params=_params())(*flat)


def _sum_small(parts):
    n = len(parts)

    def body(*refs):
        for t in range(n):
            refs[n + t][...] = _sum_sources(refs[t])

    vm = pl.BlockSpec(memory_space=pltpu.VMEM)
    return _pcall(body, name="sum_small",
                  out_shape=tuple(jax.ShapeDtypeStruct(p.shape[1:], F32) for p in parts),
                  in_specs=[vm] * n, out_specs=tuple([vm] * n), compiler_params=_params())(*parts)


def _ada_update(c_all, dmod_cols, w, m, v):
    def body(c_ref, dm_ref, w_ref, m_ref, v_ref, g_ref, d_ref, nm_ref, nv_ref):
        ca = c_ref[...]
        g = lax.dot_general(ca * jax.nn.sigmoid(ca), dm_ref[...], (((0,), (0,)), ((), ())),
                            preferred_element_type=F32, precision=lax.Precision.HIGHEST)
        g_ref[...] = g
        d_ref[...], nm_ref[...], nv_ref[...] = _adamw_math(w_ref[...], g, m_ref[...], v_ref[...])

    vm = pl.BlockSpec(memory_space=pltpu.VMEM)
    return _pcall(body, name="ada_update", out_shape=tuple([jax.ShapeDtypeStruct(w.shape, F32)] * 4),
                  in_specs=[vm] * 5, out_specs=(vm, vm, vm, vm), compiler_params=_params())(c_all, dmod_cols, w, m, v)


def kernel(x, c, w_ada, b_ada, norm_pre, norm_post, w_in, pool_w, pool_scale, ssm_a_re, ssm_a_im, ssm_log_dt, ssm_b_re, ssm_b_im, ssm_c_re, ssm_c_im, ssm_d, glu_w, glu_b, w_branch_pool, w_branch_ssm, w_out, loss_target, m_w_ada, m_b_ada, m_norm_pre, m_norm_post, m_w_in, m_pool_w, m_pool_scale, m_ssm_a_re, m_ssm_a_im, m_ssm_log_dt, m_ssm_b_re, m_ssm_b_im, m_ssm_c_re, m_ssm_c_im, m_ssm_d, m_glu_w, m_glu_b, m_w_branch_pool, m_w_branch_ssm, m_w_out, v_w_ada, v_b_ada, v_norm_pre, v_norm_post, v_w_in, v_pool_w, v_pool_scale, v_ssm_a_re, v_ssm_a_im, v_ssm_log_dt, v_ssm_b_re, v_ssm_b_im, v_ssm_c_re, v_ssm_c_im, v_ssm_d, v_glu_w, v_glu_b, v_w_branch_pool, v_w_branch_ssm, v_w_out):
    given = dict(locals())
    me = _flat(_me())
    rows = x.shape[1]
    x2 = x[0]
    target = loss_target[0]
    ada_cols = w_ada.shape[2]

    tb_ssm = _tb(rows, 256)
    k_steps = tb_ssm // SUBLANES
    a_re, a_im = ssm_a_re[0], ssm_a_im[0]
    log_dt = ssm_log_dt[0].reshape(GROUPS, 1)
    b_re_t, b_im_t = ssm_b_re[0].transpose(0, 2, 1), ssm_b_im[0].transpose(0, 2, 1)
    s5_params = [a_re, a_im, log_dt, b_re_t, b_im_t, ssm_c_re[0], ssm_c_im[0]]

    f32_shards = [w_in[0], pool_w[0], glu_w[0], w_branch_pool[0], w_branch_ssm[0], w_out[0]]
    n_sh = len(f32_shards)

    def local_work(ins, outs):
        for src, dst in zip(ins[:n_sh], outs[:n_sh]):
            dst[...] = src[...].astype(BF16)
        _s5_prep_body(*ins[n_sh:], *outs[n_sh:])

    b_ada_s = lax.dynamic_slice(b_ada, (0, me * ada_cols), (1, ada_cols))
    c_all, mod_rows, *local = _ada_exchange(
        c, w_ada[0], b_ada_s, f32_shards + s5_params,
        [jax.ShapeDtypeStruct(a.shape, BF16) for a in f32_shards] + list(_s5_prep_structs(k_steps)), local_work)
    mod3 = mod_rows.reshape(3, D)
    shards, (wb, wct, pow_re, pow_im) = local[:n_sh], local[n_sh:]
    ptab = _state_layout(pow_re, pow_im)
    dvec = ssm_d[0].reshape(1, D)
    pm = _perm_matrix(tb_ssm)
    pmt = pm.T

    proj, w_in_g, pool_w_g, glu_g = _in_proj(x2, mod3, norm_pre, shards[0], shards[1:3])
    y_pool, pooled = _pool_fwd(proj, pool_w_g, pool_scale)
    y_ssm, ys_pre, carries, states, glu_gate, z_perm, u_perm, wbp_g, wbs_g, wout_g = _ssm_fwd(
        proj, pm, pmt, wb, wct, ptab, dvec, glu_g, glu_b, shards[3:])
    loss_part, dy, dyp, dys, dpg, dwbp, dwbs, dwout, head_vec = _head(
        x2, target, proj, y_pool, y_ssm, mod3, norm_post, wbp_g, wbs_g, wout_g)

    dpp, dpool_w, dpool_scale = _pool_bwd(dyp, pooled, proj, pool_w_g, pool_scale)
    dw_in_rest = _in_proj_bwd_w("in_proj_bwd_w_rest", x2, [dpp, dpg], mod3, norm_pre)
    dy_pre, dzs, dglu_w, dglu_b = _glu_bwd(dys, z_perm, ys_pre, glu_gate, pm, pmt, glu_g)
    dus, dbb, dcc, dabar, dd, p_glu, p_wbp, p_wbs, p_wout, p_pool_w, p_w_in = _ssm_bwd(
        dy_pre, u_perm, states, carries, pmt, wb, wct, ptab, dvec, [dglu_w, dwbp, dwbs, dwout], dpool_w,
        dw_in_rest)

    small32 = jnp.concatenate([head_vec, dpool_scale, dglu_b, dd, jnp.broadcast_to(loss_part, (1, D)),
                               jnp.zeros((2, D), F32), dabar.reshape(8, D)], axis=0)
    small16 = jnp.concatenate([dbb.reshape(2 * GROUPS, D), dcc.reshape(2 * GROUPS, D)], axis=0).astype(BF16)
    dw_in_ssm, p_small32, p_small16 = _in_proj_bwd_w("in_proj_bwd_w_ssm", x2, [dus, dzs], mod3, norm_pre,
                                                     gathered=(small32, small16))
    grad_x, p_w_in, p_pre = _in_proj_bwd_x(x2, dy, dpp, dus, dzs, dpg, mod3, norm_pre, w_in_g, dw_in_ssm, p_w_in)

    tot32, tot16, tot_pre = _sum_small([p_small32, p_small16, p_pre])
    d_abar_re, d_abar_im = _state_unlayout(tot32[8:16].reshape(N_STATE))
    d_bb_re, d_bb_im = tot16[0:64].reshape(GROUPS, G_H, G_P), tot16[64:128].reshape(GROUPS, G_H, G_P)
    g_a_re, g_a_im, g_log_dt, g_b_re_t, g_b_im_t = _s5_prep_bwd(
        a_re, a_im, log_dt, b_re_t, b_im_t, d_abar_re, d_abar_im, d_bb_re, d_bb_im)

    grads, deltas, new_m, new_v = {}, {}, {}, {}

    small = []

    def small_update(name, g2):
        small.append((name, g2))

    def shard_update(name, parts):
        shape = given[name].shape
        r2 = parts.shape[1:] if parts.ndim == 3 else (parts.shape[1] * parts.shape[2], parts.shape[3])
        w2, m2, v2 = (given[p + name].reshape(r2) for p in ("", "m_", "v_"))
        out = _adamw_reduce("adamw_" + name, parts.reshape((N_DEV,) + tuple(r2)), w2, m2, v2)
        grads[name], deltas[name], new_m[name], new_v[name] = (a.reshape(shape) for a in out)

    dmod_all = jnp.concatenate([p_pre[:, 0:2, :], p_small32[:, 0:1, :]], axis=1).reshape(N_DEV, 3 * D)
    dmod_cols = lax.dynamic_slice(dmod_all, (0, me * ada_cols), (N_DEV, ada_cols))
    out = _ada_update(c_all, dmod_cols, w_ada[0], m_w_ada[0], v_w_ada[0])
    grads['w_ada'], deltas['w_ada'], new_m['w_ada'], new_v['w_ada'] = (a.reshape(w_ada.shape) for a in out)

    small_update('b_ada', jnp.concatenate([tot_pre[0:2], tot32[0:1]], axis=0).reshape(1, 3 * D))
    small_update('norm_pre', tot_pre[2:3])
    small_update('norm_post', tot32[1:2])
    small_update('pool_scale', tot32[2:3])
    small_update('glu_b', tot32[3:4])
    small_update('ssm_d', tot32[4:5])
    small_update('ssm_a_re', g_a_re)
    small_update('ssm_a_im', g_a_im)
    small_update('ssm_log_dt', g_log_dt.reshape(1, GROUPS))
    small_update('ssm_b_re', g_b_re_t.transpose(0, 2, 1).reshape(GROUPS, G_P * G_H))
    small_update('ssm_b_im', g_b_im_t.transpose(0, 2, 1).reshape(GROUPS, G_P * G_H))
    small_update('ssm_c_re', tot16[128:192])
    small_update('ssm_c_im', -tot16[192:256])
    flat = _adamw_small([g2 for _, g2 in small],
                        *[[given[p + name].reshape(g2.shape) for name, g2 in small] for p in ("", "m_", "v_")])
    for t, (name, g2) in enumerate(small):
        shape = given[name].shape
        grads[name], deltas[name], new_m[name], new_v[name] = (
            a.reshape(shape) for a in (g2, *flat[3 * t:3 * t + 3]))
    shard_update('w_in', p_w_in)
    shard_update('pool_w', p_pool_w)
    shard_update('glu_w', p_glu)
    shard_update('w_branch_pool', p_wbp)
    shard_update('w_branch_ssm', p_wbs)
    shard_update('w_out', p_wout)

    return (tot32[5, 0], grad_x[None], *[grads[n] for n in WEIGHTS], *[deltas[n] for n in WEIGHTS],
            *[new_m[n] for n in WEIGHTS], *[new_v[n] for n in WEIGHTS])
```

```python
import math
from typing import Callable, NamedTuple, Optional

import jax
import jax.numpy as jnp
from jax import lax
from jax.experimental import pallas as pl
from jax.experimental.pallas import tpu as pltpu
from jax.experimental.pallas import tpu_sc as plsc

F32 = jnp.float32
BF16 = jnp.bfloat16
MESH = pl.DeviceIdType.MESH

D = 1024
N_DEV = 8
N_IN = 6 * D
GROUPS = 64
G_H = 16
G_P = 64
N_Q = 4
Q_W = 2 * 16 * G_P
N_STATE = N_Q * Q_W
POOL_WINDOWS = (2, 4, 8, 16)
HALO = 16
RMS_EPS = 1e-6
SUBLANES = 8
LANE_CHUNK = 512
SCAN_UNROLL = 2
VMEM_LIMIT = 56 * 1024 * 1024

ADAM_LR = 0.001
ADAM_B1 = 0.9
ADAM_B2 = 0.999
ADAM_EPS = 1e-08
ADAM_WD = 0.01
ADAM_STEP = 10

WEIGHTS = ['w_ada', 'b_ada', 'norm_pre', 'norm_post', 'w_in', 'pool_w', 'pool_scale', 'ssm_a_re',
           'ssm_a_im', 'ssm_log_dt', 'ssm_b_re', 'ssm_b_im', 'ssm_c_re', 'ssm_c_im', 'ssm_d', 'glu_w',
           'glu_b', 'w_branch_pool', 'w_branch_ssm', 'w_out']


def _pcall(body, **kw):
    return pl.pallas_call(body, **kw)


def _params(sem=None, vmem=VMEM_LIMIT):
    return pltpu.CompilerParams(dimension_semantics=sem, vmem_limit_bytes=vmem)


def _tb(rows, pref):
    return pref if rows % pref == 0 and rows // pref >= 2 else rows // 2


def _full(shape, single=False):
    nd = len(shape)
    if single:
        return pl.BlockSpec(shape, lambda i: (0,) * nd, pipeline_mode=pl.Buffered(1))
    return pl.BlockSpec(shape, lambda i: (0,) * nd)


ANY = pl.BlockSpec(memory_space=pl.ANY)


def _me():
    return lax.axis_index("x"), lax.axis_index("y"), lax.axis_index("c")


def _flat(p):
    return 4 * p[0] + 2 * p[1] + p[2]


def _peer(k):
    x, y, c = _me()
    return (1 - x if k & 4 else x, 1 - y if k & 2 else y, 1 - c if k & 1 else c)


def _silu_parts(z):
    s = jax.nn.sigmoid(z)
    return z * s, s * (1.0 + z * (1.0 - s))


_GELU_C = math.sqrt(2.0 / math.pi)


def _gelu_parts(x):
    x2 = x * x
    t = jnp.tanh(_GELU_C * (x + 0.044715 * x * x2))
    g = 0.5 * x * (1.0 + t)
    dg = 0.5 * (1.0 + t) + 0.5 * x * (1.0 - t * t) * (_GELU_C * (1.0 + 3.0 * 0.044715 * x2))
    return g, dg


def _dot(a, b):
    return jnp.dot(a, b, preferred_element_type=F32)


def _dot_nt(a, b):
    return lax.dot_general(a, b, (((1,), (1,)), ((), ())), preferred_element_type=F32)


def _dot_tn(a, b):
    return lax.dot_general(a, b, (((0,), (0,)), ((), ())), preferred_element_type=F32)


def _rms_parts(x):
    r = lax.rsqrt(jnp.mean(x * x, axis=-1, keepdims=True) + RMS_EPS)
    return x * r, r


def _rms_bwd(dxn, xn, r):
    return r * (dxn - xn * jnp.mean(dxn * xn, axis=-1, keepdims=True))


def _ada_exchange(c, w_ada_s, b_ada_s, local_ins, local_outs, local_work):
    cols = w_ada_s.shape[1]
    n_li, n_lo = len(local_ins), len(local_outs)

    def body(c_ref, w_ref, b_ref, *rest):
        li_refs, call_ref, mod_ref = rest[:n_li], rest[n_li], rest[n_li + 1]
        lo_refs, (part_ref, ssem, rsem, lsem) = rest[n_li + 2:n_li + 2 + n_lo], rest[n_li + 2 + n_lo:]
        me3 = _me()
        me = _flat(me3)
        mine = pltpu.make_async_copy(c_ref, call_ref.at[pl.ds(me, 1), :], lsem.at[0])
        mine.start()
        sends = []
        for k in range(1, N_DEV):
            cp = pltpu.make_async_remote_copy(src_ref=c_ref, dst_ref=call_ref.at[pl.ds(me, 1), :],
                                              send_sem=ssem.at[k - 1], recv_sem=rsem.at[k - 1],
                                              device_id=_peer(k), device_id_type=MESH)
            cp.start()
            sends.append(cp)
        local_work(li_refs, lo_refs)
        mine.wait()
        for k in range(1, N_DEV):
            p = _flat(_peer(k))
            pltpu.make_async_remote_copy(src_ref=c_ref, dst_ref=call_ref.at[pl.ds(p, 1), :],
                                         send_sem=ssem.at[k - 1], recv_sem=rsem.at[k - 1],
                                         device_id=_peer(k), device_id_type=MESH).wait_recv()
        for cp in sends:
            cp.wait_send()
        ca = call_ref[...]
        act = ca * jax.nn.sigmoid(ca)
        part_ref[...] = jnp.dot(act, w_ref[...], preferred_element_type=F32,
                                precision=lax.Precision.HIGHEST) + b_ref[...]
        own = pltpu.make_async_copy(part_ref.at[pl.ds(me, 1), :], mod_ref.at[pl.ds(me, 1), :], lsem.at[1])
        own.start()
        sends = []
        for k in range(1, N_DEV):
            p = _flat(_peer(k))
            s = N_DEV - 1 + k - 1
            cp = pltpu.make_async_remote_copy(src_ref=part_ref.at[pl.ds(p, 1), :],
                                              dst_ref=mod_ref.at[pl.ds(me, 1), :],
                                              send_sem=ssem.at[s], recv_sem=rsem.at[s],
                                              device_id=_peer(k), device_id_type=MESH)
            cp.start()
            sends.append(cp)
        own.wait()
        for k in range(1, N_DEV):
            p = _flat(_peer(k))
            s = N_DEV - 1 + k - 1
            pltpu.make_async_remote_copy(src_ref=part_ref.at[pl.ds(p, 1), :],
                                         dst_ref=mod_ref.at[pl.ds(p, 1), :],
                                         send_sem=ssem.at[s], recv_sem=rsem.at[s],
                                         device_id=_peer(k), device_id_type=MESH).wait_recv()
        for cp in sends:
            cp.wait_send()

    vm = pl.BlockSpec(memory_space=pltpu.VMEM)
    return _pcall(
        body, name="ada_exchange",
        out_shape=(jax.ShapeDtypeStruct((N_DEV, D), F32), jax.ShapeDtypeStruct((N_DEV, cols), F32), *local_outs),
        in_specs=[vm] * (3 + n_li), out_specs=tuple([vm] * (2 + n_lo)),
        scratch_shapes=[pltpu.VMEM((N_DEV, cols), F32),
                        pltpu.SemaphoreType.DMA((2 * (N_DEV - 1),)),
                        pltpu.SemaphoreType.DMA((2 * (N_DEV - 1),)),
                        pltpu.SemaphoreType.DMA((2,))],
        compiler_params=_params(),
    )(c, w_ada_s, b_ada_s, *local_ins)


class _Item(NamedTuple):
    src: int
    out: int
    src_view: Callable
    dst_view: Callable
    pred: Optional[Callable] = None


def _when(pred, dest, fn):
    if pred is None:
        fn()
    else:
        pl.when(pred(dest))(fn)


def _n_sems(items):
    return len(items) * (N_DEV - 1)


def _hosted_copies(items, srcs, outs, ssem, rsem, lsem, act):
    me = _flat(_me())
    for t, it in enumerate(items):
        local = lambda t=t, it=it: pltpu.make_async_copy(
            it.src_view(srcs[it.src], me), it.dst_view(outs[it.out], me), lsem.at[t])
        if act == "start":
            _when(it.pred, me, lambda local=local: local().start())
        else:
            _when(it.pred, me, lambda local=local: local().wait())
    for k in range(1, N_DEV):
        p3 = _peer(k)
        p = _flat(p3)
        for t, it in enumerate(items):
            s = t * (N_DEV - 1) + k - 1
            send = lambda it=it, s=s, p=p, p3=p3: pltpu.make_async_remote_copy(
                src_ref=it.src_view(srcs[it.src], p), dst_ref=it.dst_view(outs[it.out], me),
                send_sem=ssem.at[s], recv_sem=rsem.at[s], device_id=p3, device_id_type=MESH)
            recv = lambda it=it, s=s, p=p, p3=p3: pltpu.make_async_remote_copy(
                src_ref=it.src_view(srcs[it.src], p), dst_ref=it.dst_view(outs[it.out], p),
                send_sem=ssem.at[s], recv_sem=rsem.at[s], device_id=p3, device_id_type=MESH)
            if act == "start":
                _when(it.pred, p, lambda send=send: send().start())
            else:
                _when(it.pred, me, lambda recv=recv: recv().wait_recv())
                _when(it.pred, p, lambda send=send: send().wait_send())


def _sem_scratch(items):
    return [pltpu.SemaphoreType.DMA((_n_sems(items),)), pltpu.SemaphoreType.DMA((_n_sems(items),)),
            pltpu.SemaphoreType.DMA((len(items),))]


def _whole(ref, dest):
    return ref


def _slot(ref, sender):
    return ref.at[sender]


def _rows_of(rows):
    return lambda ref, dev: ref.at[pl.ds(dev * rows, rows), :]


def _pool_rows_of(rows):
    return lambda ref, dev: ref.at[:, pl.ds(dev * rows, rows), :]


def _gather_item(src, out, dst_view):
    return _Item(src, out, _whole, dst_view)


def _scatter_item(src, out, src_view):
    return _Item(src, out, src_view, _slot)


W_IN_BLOCK = 256
W_IN_SHARD = N_IN // N_DEV
SSM_BLOCKS = (2 * D // W_IN_BLOCK, 4 * D // W_IN_BLOCK)


def _w_in_block_item(src, out, j, ssm_part):
    def block(dest):
        return (W_IN_SHARD // W_IN_BLOCK) * dest + j

    def in_ssm(dest):
        b = block(dest)
        return (b >= SSM_BLOCKS[0]) & (b < SSM_BLOCKS[1])

    def src_view(ref, dest):
        b = block(dest)
        local = b - SSM_BLOCKS[0] if ssm_part else jnp.where(b < SSM_BLOCKS[0], b, b - (SSM_BLOCKS[1] - SSM_BLOCKS[0]))
        local = jnp.clip(local, 0, ref.shape[1] // W_IN_BLOCK - 1)
        return ref.at[:, pl.ds(local * W_IN_BLOCK, W_IN_BLOCK)]

    def dst_view(ref, sender):
        return ref.at[sender, :, pl.ds(j * W_IN_BLOCK, W_IN_BLOCK)]

    pred = in_ssm if ssm_part else (lambda dest: jnp.logical_not(in_ssm(dest)))
    return _Item(src, out, src_view, dst_view, pred)


def _s5_discretise(a_re, a_im, log_dt, b_re_t, b_im_t):
    dt = jnp.exp(log_dt)
    lam_re = jnp.minimum(a_re, -1e-4)
    lam_im = a_im
    mag = jnp.exp(lam_re * dt)
    abar_re = mag * jnp.cos(lam_im * dt)
    abar_im = mag * jnp.sin(lam_im * dt)
    den = lam_re * lam_re + lam_im * lam_im
    num_re = abar_re - 1.0
    f_re = (num_re * lam_re + abar_im * lam_im) / den
    f_im = (abar_im * lam_re - num_re * lam_im) / den
    f_re, f_im = f_re[:, None, :], f_im[:, None, :]
    bb_re = f_re * b_re_t - f_im * b_im_t
    bb_im = f_re * b_im_t + f_im * b_re_t
    return abar_re, abar_im, bb_re, bb_im


def _group_masks():
    spread = lax.broadcasted_iota(jnp.int32, (G_P, 16 * G_P), 1) % G_P == lax.broadcasted_iota(
        jnp.int32, (G_P, 16 * G_P), 0)
    own = lax.broadcasted_iota(jnp.int32, (16 * G_H, 16 * G_P), 0) // G_H == lax.broadcasted_iota(
        jnp.int32, (16 * G_H, 16 * G_P), 1) // G_P
    return spread, own


def _s5_prep_structs(n_pow):
    return (jax.ShapeDtypeStruct((N_Q, 16 * G_H, Q_W), BF16), jax.ShapeDtypeStruct((N_Q, 16 * G_H, Q_W), BF16),
            jax.ShapeDtypeStruct((n_pow, GROUPS, G_P), F32), jax.ShapeDtypeStruct((n_pow, GROUPS, G_P), F32))


def _s5_prep_body(ar_ref, ai_ref, ld_ref, br_ref, bi_ref, cr_ref, ci_ref, wb_ref, wct_ref, pr_ref, pi_ref):
    abar_re, abar_im, bb_re, bb_im = _s5_discretise(ar_ref[...], ai_ref[...], ld_ref[...], br_ref[...], bi_ref[...])
    spread, own = _group_masks()
    spread = spread.astype(BF16)
    for ref, parts in ((wb_ref, (bb_re, bb_im)), (wct_ref, (cr_ref[...], -ci_ref[...]))):
        for half, t in enumerate(parts):
            for q in range(N_Q):
                blocks = t[q * 16:(q + 1) * 16].reshape(16 * G_H, G_P).astype(BF16)
                dense = jnp.where(own, _dot(blocks, spread), 0.0)
                ref[q, :, half * (Q_W // 2):(half + 1) * (Q_W // 2)] = dense.astype(BF16)
    p_re, p_im = abar_re, abar_im
    pr_ref[0] = p_re
    pi_ref[0] = p_im
    for k in range(1, pr_ref.shape[0]):
        p_re, p_im = p_re * abar_re - p_im * abar_im, p_re * abar_im + p_im * abar_re
        pr_ref[k] = p_re
        pi_ref[k] = p_im


def _s5_prep_bwd(a_re, a_im, log_dt, b_re_t, b_im_t, d_abar_re, d_abar_im, d_bb_re, d_bb_im):
    def body(ar_ref, ai_ref, ld_ref, br_ref, bi_ref, dar_ref, dai_ref, dbr_ref, dbi_ref,
             gar_ref, gai_ref, gld_ref, gbr_ref, gbi_ref):
        _, vjp = jax.vjp(_s5_discretise, ar_ref[...], ai_ref[...], ld_ref[...], br_ref[...], bi_ref[...])
        g = vjp((dar_ref[...], dai_ref[...], dbr_ref[...], dbi_ref[...]))
        gar_ref[...] = g[0]
        gai_ref[...] = g[1]
        gld_ref[...] = g[2]
        gbr_ref[...] = g[3]
        gbi_ref[...] = g[4]

    vm = pl.BlockSpec(memory_space=pltpu.VMEM)
    ins = (a_re, a_im, log_dt, b_re_t, b_im_t)
    return _pcall(body, name="s5_prep_bwd",
                  out_shape=tuple(jax.ShapeDtypeStruct(a.shape, F32) for a in ins),
                  in_specs=[vm] * 9, out_specs=tuple([vm] * 5), compiler_params=_params(),
                  )(*ins, d_abar_re, d_abar_im, d_bb_re, d_bb_im)


def _state_layout(re, im):
    lead = re.shape[:-2]
    r = re.reshape(lead + (N_Q, 1, 16 * G_P))
    i = im.reshape(lead + (N_Q, 1, 16 * G_P))
    return jnp.concatenate([r, i], axis=-2).reshape(lead + (N_STATE,))


def _state_unlayout(v):
    v4 = v.reshape(N_Q, 2, 16, G_P)
    return v4[:, 0].reshape(GROUPS, G_P), v4[:, 1].reshape(GROUPS, G_P)


def _perm_matrix(tb):
    k_steps = tb // SUBLANES
    r = jnp.arange(tb)
    src = (r % SUBLANES) * k_steps + r // SUBLANES
    return (src[:, None] == jnp.arange(tb)[None, :]).astype(BF16)


def _lane_chunks(q):
    for lc in range(Q_W // 2 // LANE_CHUNK):
        re = q * Q_W + lc * LANE_CHUNK
        yield re, re + Q_W // 2


def _steps(lo, hi, body, init):
    if hi - lo <= SCAN_UNROLL:
        for k in range(lo, hi):
            init = body(k, init)
        return init
    trips = (hi - lo) // SCAN_UNROLL

    def trip(j, carry):
        for u in range(SCAN_UNROLL):
            carry = body(lo + j * SCAN_UNROLL + u, carry)
        return carry

    carry = lax.fori_loop(0, trips, trip, init)
    for k in range(lo + trips * SCAN_UNROLL, hi):
        carry = body(k, carry)
    return carry


def _tile(k):
    if isinstance(k, int):
        return pl.ds(k * SUBLANES, SUBLANES)
    return pl.ds(pl.multiple_of(k * SUBLANES, SUBLANES), SUBLANES)


def _scan_forward(q, s_ref, p_ref, carry_ref, enter_ref, fin_ref, k_steps):
    for re, im in _lane_chunks(q):
        lr, li = pl.ds(re, LANE_CHUNK), pl.ds(im, LANE_CHUNK)
        a_re = jnp.broadcast_to(p_ref[0:1, lr], (SUBLANES, LANE_CHUNK))
        a_im = jnp.broadcast_to(p_ref[0:1, li], (SUBLANES, LANE_CHUNK))

        def local(k, st):
            sr, si = st
            rows = _tile(k)
            nr = a_re * sr - a_im * si + s_ref[rows, lr]
            ni = a_re * si + a_im * sr + s_ref[rows, li]
            s_ref[rows, lr] = nr
            s_ref[rows, li] = ni
            return nr, ni

        zero = jnp.zeros((SUBLANES, LANE_CHUNK), F32)
        fr, fi = _steps(0, k_steps, local, (zero, zero))
        fin_ref[:, lr] = fr
        fin_ref[:, li] = fi
        ak_re, ak_im = p_ref[k_steps - 1:k_steps, lr], p_ref[k_steps - 1:k_steps, li]
        c_re, c_im = carry_ref[:, lr], carry_ref[:, li]
        for seg in range(SUBLANES):
            enter_ref[seg:seg + 1, lr] = c_re
            enter_ref[seg:seg + 1, li] = c_im
            f_re, f_im = fin_ref[seg:seg + 1, lr], fin_ref[seg:seg + 1, li]
            c_re, c_im = f_re + ak_re * c_re - ak_im * c_im, f_im + ak_re * c_im + ak_im * c_re
        carry_ref[:, lr] = c_re
        carry_ref[:, li] = c_im
        e_re, e_im = enter_ref[:, lr], enter_ref[:, li]

        def fix(k, _):
            rows = _tile(k)
            p_re = p_ref[pl.ds(k, 1), lr]
            p_im = p_ref[pl.ds(k, 1), li]
            s_ref[rows, lr] = s_ref[rows, lr] + (p_re * e_re - p_im * e_im)
            s_ref[rows, li] = s_ref[rows, li] + (p_re * e_im + p_im * e_re)
            return 0

        _steps(0, k_steps, fix, 0)


def _scan_backward(q, g_ref, s_ref, p_ref, carry_ref, s_in_ref, fin_ref, da_ref, k_steps):
    seg_id = lax.broadcasted_iota(jnp.int32, (SUBLANES, LANE_CHUNK), 0)
    for re, im in _lane_chunks(q):
        lr, li = pl.ds(re, LANE_CHUNK), pl.ds(im, LANE_CHUNK)
        a_re = jnp.broadcast_to(p_ref[0:1, lr], (SUBLANES, LANE_CHUNK))
        a_im = jnp.broadcast_to(p_ref[0:1, li], (SUBLANES, LANE_CHUNK))

        def local(j, st):
            sr, si = st
            rows = _tile(k_steps - 1 - j)
            nr = a_re * sr + a_im * si + g_ref[rows, lr]
            ni = a_re * si - a_im * sr + g_ref[rows, li]
            g_ref[rows, lr] = nr
            g_ref[rows, li] = ni
            return nr, ni

        zero = jnp.zeros((SUBLANES, LANE_CHUNK), F32)
        fr, fi = _steps(0, k_steps, local, (zero, zero))
        fin_ref[:, lr] = fr
        fin_ref[:, li] = fi
        ak_re, ak_im = p_ref[k_steps - 1:k_steps, lr], p_ref[k_steps - 1:k_steps, li]
        c_re, c_im = carry_ref[:, lr], carry_ref[:, li]
        lam_in = [None] * SUBLANES
        for seg in reversed(range(SUBLANES)):
            lam_in[seg] = (c_re, c_im)
            f_re, f_im = fin_ref[seg:seg + 1, lr], fin_ref[seg:seg + 1, li]
            c_re, c_im = f_re + ak_re * c_re + ak_im * c_im, f_im + ak_re * c_im - ak_im * c_re
        carry_ref[:, lr] = c_re
        carry_ref[:, li] = c_im
        for seg in range(SUBLANES):
            fin_ref[seg:seg + 1, lr] = lam_in[seg][0]
            fin_ref[seg:seg + 1, li] = lam_in[seg][1]
        e_re, e_im = fin_ref[:, lr], fin_ref[:, li]

        def fix_with(k, acc, sp_re, sp_im):
            acc_re, acc_im = acc
            rows = _tile(k)
            p_re = p_ref[pl.ds(k_steps - 1 - k, 1), lr]
            p_im = p_ref[pl.ds(k_steps - 1 - k, 1), li]
            l_re = g_ref[rows, lr] + (p_re * e_re + p_im * e_im)
            l_im = g_ref[rows, li] + (p_re * e_im - p_im * e_re)
            g_ref[rows, lr] = l_re
            g_ref[rows, li] = l_im
            return acc_re + (l_re * sp_re + l_im * sp_im), acc_im + (l_im * sp_re - l_re * sp_im)

        def fix(k, acc):
            prev = _tile(k - 1)
            return fix_with(k, acc, s_ref[prev, lr], s_ref[prev, li])

        last = _tile(k_steps - 1)
        before_re = jnp.where(seg_id == 0, s_in_ref[:, lr], pltpu.roll(s_ref[last, lr], 1, axis=0))
        before_im = jnp.where(seg_id == 0, s_in_ref[:, li], pltpu.roll(s_ref[last, li], 1, axis=0))
        acc = fix_with(0, (zero, zero), before_re, before_im)
        acc_re, acc_im = _steps(1, k_steps, fix, acc)
        da_ref[:, lr] = da_ref[:, lr] + jnp.sum(acc_re, axis=0, keepdims=True)
        da_ref[:, li] = da_ref[:, li] + jnp.sum(acc_im, axis=0, keepdims=True)


def _prenorm(x, mod3, norm_pre):
    xn, r = _rms_parts(x)
    return xn, r, xn * norm_pre * (1.0 + mod3[1:2, :]) + mod3[0:1, :]


CHIP_FLIPS = (4, 2, 6)


def _shard_order(me):
    flips = [0, 1] + [f + c for f in CHIP_FLIPS for c in (0, 1)]
    return jnp.stack([me ^ f for f in flips]).astype(jnp.int32)


def _in_proj(x, mod3, norm_pre, w_in_s, shards):
    rows = x.shape[0]
    tb = _tb(rows, 2048)
    nblk = rows // tb
    n_sh = len(shards)
    last_step = N_DEV - 1
    items = [_gather_item(0, 0, _pool_rows_of(shards[0].shape[1]))] + \
            [_gather_item(t, t, _rows_of(shards[t].shape[0])) for t in range(1, n_sh)]

    def body(order_ref, x_ref, mod_ref, np_ref, w_src, *rest):
        src_refs, proj_ref, w_full, out_refs = rest[:n_sh], rest[n_sh], rest[n_sh + 1], rest[n_sh + 2:2 * n_sh + 2]
        h_scr, wg, ssem, rsem, lsem, *sems = rest[2 * n_sh + 2:]
        s, i = pl.program_id(0), pl.program_id(1)
        me3 = _me()
        me = _flat(me3)
        sibling = _peer(1)

        def own_copy(slot, k):
            return pltpu.make_async_remote_copy(src_ref=w_src, dst_ref=wg.at[me], send_sem=ssem.at[slot],
                                                recv_sem=rsem.at[slot], device_id=_peer(k), device_id_type=MESH)

        def passed_copy(j):
            p = _flat(_peer(CHIP_FLIPS[j]))
            return pltpu.make_async_remote_copy(src_ref=wg.at[p], dst_ref=wg.at[p], send_sem=ssem.at[4 + j],
                                                recv_sem=rsem.at[4 + j], device_id=sibling, device_id_type=MESH)

        def arrival(slot, flip):
            p = _flat(_peer(flip))
            pltpu.make_async_remote_copy(src_ref=w_src, dst_ref=wg.at[p], send_sem=ssem.at[slot],
                                         recv_sem=rsem.at[slot], device_id=sibling, device_id_type=MESH).wait_recv()

        def keep(t):
            p = order_ref[t]
            return pltpu.make_async_copy(wg.at[p], w_full.at[:, pl.ds(p * W_IN_SHARD, W_IN_SHARD)], lsem.at[1 + t])

        first = i == 0
        for t in range(last_step):
            pl.when(first & (s == t + 1))(lambda t=t: keep(t).start())

        @pl.when(first & (s == 0))
        def _():
            mine = pltpu.make_async_copy(w_src, wg.at[me], lsem.at[0])
            mine.start()
            own_copy(0, 1).start()
            for j, f in enumerate(CHIP_FLIPS[:2]):
                own_copy(1 + j, f).start()
            mine.wait()

        @pl.when(first & (s == 1))
        def _():
            arrival(0, 1)

        for j, f in enumerate(CHIP_FLIPS):
            @pl.when(first & (s == 2 + 2 * j))
            def _(j=j, f=f):
                arrival(1 + j, f)
                passed_copy(j).start()
                if j == 0:
                    own_copy(3, CHIP_FLIPS[2]).start()

            @pl.when(first & (s == 3 + 2 * j))
            def _(j=j, f=f):
                arrival(4 + j, f + 1)

        @pl.when(first & (s == last_step - 1))
        def _():
            _hosted_copies(items, src_refs, out_refs, *sems, act="start")

        rows_i = pl.ds(pl.multiple_of(i * tb, tb), tb)

        @pl.when(s == 0)
        def _():
            _, _, h = _prenorm(x_ref[...], mod_ref[...], np_ref[...])
            h_scr[rows_i, :] = h.astype(BF16)

        proj_ref[...] = _dot(h_scr[rows_i, :], wg[order_ref[s]]).astype(BF16)

        @pl.when((s == last_step) & (i == nblk - 1))
        def _():
            own_copy(0, 1).wait_send()
            for j, f in enumerate(CHIP_FLIPS):
                own_copy(1 + j, f).wait_send()
                passed_copy(j).wait_send()
            keep(last_step).start()
            for t in range(N_DEV):
                keep(t).wait()
            _hosted_copies(items, src_refs, out_refs, *sems, act="wait")

    full = [jax.ShapeDtypeStruct((4, 256, 256), BF16)] + [jax.ShapeDtypeStruct((D, D), BF16)] * (n_sh - 1)
    grid_spec = pltpu.PrefetchScalarGridSpec(
        num_scalar_prefetch=1, grid=(N_DEV, nblk),
        in_specs=[pl.BlockSpec((tb, D), lambda s, i, order: (jnp.where(s == 0, i, nblk - 1), 0)),
                  pl.BlockSpec((3, D), lambda s, i, order: (0, 0)), pl.BlockSpec((1, D), lambda s, i, order: (0, 0)),
                  ANY] + [ANY] * n_sh,
        out_specs=(pl.BlockSpec((tb, W_IN_SHARD), lambda s, i, order: (i, order[s])), ANY, *([ANY] * n_sh)),
        scratch_shapes=[pltpu.VMEM((rows, D), BF16), pltpu.VMEM((N_DEV, D, W_IN_SHARD), BF16),
                        pltpu.SemaphoreType.DMA((N_DEV - 1,)), pltpu.SemaphoreType.DMA((N_DEV - 1,)),
                        pltpu.SemaphoreType.DMA((1 + N_DEV,))] + _sem_scratch(items))
    return _pcall(body, name="in_proj", grid_spec=grid_spec,
                  out_shape=(jax.ShapeDtypeStruct((rows, N_IN), BF16), jax.ShapeDtypeStruct((D, N_IN), BF16), *full),
                  compiler_params=_params(("arbitrary", "arbitrary")),
                  )(_shard_order(_flat(_me())), x, mod3, norm_pre, w_in_s, *shards)


def _pool_windows(ext, tb, first_row):
    inv_counts = _inv_counts(tb, first_row)
    pooled = []
    for g, w in enumerate(POOL_WINDOWS):
        acc = ext[:, g * 256:(g + 1) * 256]
        tok = acc[HALO:, :]
        s = 1
        while s < w:
            acc = acc + pltpu.roll(acc, s, axis=0)
            s *= 2
        pooled.append(acc[HALO:, :] * inv_counts[g] - tok)
    return pooled, inv_counts


def _inv_counts(tb, first_row):
    pos = (first_row + lax.broadcasted_iota(jnp.int32, (tb, 1), 0) + 1).astype(F32)
    return [1.0 / jnp.minimum(pos, float(w)) for w in POOL_WINDOWS]


def _pool_fwd(proj, pool_w, pool_scale):
    rows = proj.shape[0]
    tb = _tb(rows, 1024)
    hb = tb // HALO

    def body(u_ref, halo_ref, z_ref, pw_ref, ps_ref, y_ref, pooled_ref):
        i = pl.program_id(0)
        u = u_ref[...].astype(F32)
        halo = jnp.where(i > 0, halo_ref[...].astype(F32), 0.0)
        pooled, _ = _pool_windows(jnp.concatenate([halo, u], axis=0), tb, i * tb)
        silu_z, _ = _silu_parts(z_ref[...].astype(F32))
        for g in range(4):
            cols = slice(g * 256, (g + 1) * 256)
            pooled_b = pooled[g].astype(BF16)
            pooled_ref[:, cols] = pooled_b
            mixed = _dot(pooled_b, pw_ref[g])
            y_ref[:, cols] = (mixed * ps_ref[:, cols] * silu_z[:, cols]).astype(BF16)

    blk = pl.BlockSpec((tb, D), lambda i: (i, 0))
    return _pcall(body, name="pool_fwd", grid=(rows // tb,),
                  out_shape=(jax.ShapeDtypeStruct((rows, D), BF16), jax.ShapeDtypeStruct((rows, D), BF16)),
                  in_specs=[blk, pl.BlockSpec((HALO, D), lambda i: (jnp.maximum(i * hb - 1, 0), 0)),
                            pl.BlockSpec((tb, D), lambda i: (i, 1)),
                            _full((4, 256, 256)), _full((1, D))],
                  out_specs=(blk, blk),
                  compiler_params=_params(("arbitrary",)))(proj, proj, proj, pool_w, pool_scale)


def _ssm_fwd(proj, pm, pmt, wb, wct, ptab, dvec, glu_w, glu_b, shards):
    rows = proj.shape[0]
    tb = pm.shape[0]
    k_steps = tb // SUBLANES
    nblk = rows // tb
    n_sh = len(shards)
    items = [_gather_item(t, t, _rows_of(shards[t].shape[0])) for t in range(n_sh)]

    def body(u_ref, z_ref, pm_ref, pmt_ref, wb_ref, wct_ref, p_ref, d_ref, gw_ref, gb_ref, *rest):
        src_refs = rest[:n_sh]
        y_ref, ys_ref, carry_out_ref, s_ref, gate_ref, zp_ref, up_ref = rest[n_sh:n_sh + 7]
        out_refs = rest[n_sh + 7:2 * n_sh + 7]
        carry_ref, enter_ref, fin_ref, *sems = rest[2 * n_sh + 7:]

        @pl.when(pl.program_id(0) == 0)
        def _():
            _hosted_copies(items, src_refs, out_refs, *sems, act="start")
            carry_ref[...] = jnp.zeros_like(carry_ref)

        carry_out_ref[...] = carry_ref[...]
        up = _dot(pm_ref[...], u_ref[...]).astype(BF16)
        up_ref[...] = up

        for q in range(N_Q):
            s_ref[:, q * Q_W:(q + 1) * Q_W] = _dot(up[:, q * 256:(q + 1) * 256], wb_ref[q])
        for q in range(N_Q):
            _scan_forward(q, s_ref, p_ref, carry_ref, enter_ref, fin_ref, k_steps)
        for q in range(N_Q):
            cols = slice(q * 256, (q + 1) * 256)
            y = _dot_nt(s_ref[:, q * Q_W:(q + 1) * Q_W].astype(BF16), wct_ref[q])
            ys_ref[:, cols] = y + d_ref[:, cols] * up[:, cols].astype(F32)
        yg, _ = _gelu_parts(ys_ref[...])
        gate = jax.nn.sigmoid(_dot(yg.astype(BF16), gw_ref[...]) + gb_ref[...])
        gate_ref[...] = gate
        zp = _dot(pm_ref[...], z_ref[...])
        zp_ref[...] = zp.astype(BF16)
        silu_z, _ = _silu_parts(zp)
        y_ref[...] = _dot(pmt_ref[...], (yg * gate * silu_z).astype(BF16)).astype(BF16)

        @pl.when(pl.program_id(0) == nblk - 1)
        def _():
            _hosted_copies(items, src_refs, out_refs, *sems, act="wait")

    return _pcall(body, name="ssm_fwd", grid=(nblk,),
                  out_shape=(jax.ShapeDtypeStruct((rows, D), BF16), jax.ShapeDtypeStruct((rows, D), F32),
                             jax.ShapeDtypeStruct((nblk, 1, N_STATE), F32),
                             jax.ShapeDtypeStruct((rows, N_STATE), F32),
                             jax.ShapeDtypeStruct((rows, D), F32), jax.ShapeDtypeStruct((rows, D), BF16),
                             jax.ShapeDtypeStruct((rows, D), BF16),
                             *[jax.ShapeDtypeStruct((D, D), BF16)] * n_sh),
                  in_specs=[pl.BlockSpec((tb, D), lambda i: (i, 2)), pl.BlockSpec((tb, D), lambda i: (i, 3)),
                            _full((tb, tb)), _full((tb, tb)),
                            _full((N_Q, 256, Q_W), single=True), _full((N_Q, 256, Q_W), single=True),
                            _full((k_steps, N_STATE)), _full((1, D)), _full((D, D), single=True), _full((1, D))] +
                           [ANY] * n_sh,
                  out_specs=(pl.BlockSpec((tb, D), lambda i: (i, 0)), pl.BlockSpec((tb, D), lambda i: (i, 0)),
                             pl.BlockSpec((None, 1, N_STATE), lambda i: (i, 0, 0)),
                             pl.BlockSpec((tb, N_STATE), lambda i: (i, 0)),
                             *[pl.BlockSpec((tb, D), lambda i: (i, 0))] * 3, *([ANY] * n_sh)),
                  scratch_shapes=[pltpu.VMEM((1, N_STATE), F32),
                                  pltpu.VMEM((SUBLANES, N_STATE), F32), pltpu.VMEM((SUBLANES, N_STATE), F32)] +
                                 _sem_scratch(items),
                  compiler_params=_params(("arbitrary",)))(proj, proj, pm, pmt, wb, wct, ptab, dvec, glu_w, glu_b,
                                                           *shards)


def _head(x, target, proj, y_pool, y_ssm, mod3, norm_post, wbp, wbs, wout):
    rows = x.shape[0]
    tb = _tb(rows, 256)
    nblk = rows // tb
    n_feat = float(D)

    def body(x_ref, t_ref, gp_ref, gs_ref, yp_ref, ys_ref, mod_ref, npost_ref, wbp_ref, wbs_ref, wout_ref,
             loss_ref, dy_ref, dyp_ref, dys_ref, dg_ref, dwbp_hbm, dwbs_hbm, dwout_hbm, vec_ref,
             acc_bp, acc_bs, acc_out, acc_loss, acc_vec):
        i = pl.program_id(0)

        @pl.when(i == 0)
        def _():
            acc_bp[...] = jnp.zeros_like(acc_bp)
            acc_bs[...] = jnp.zeros_like(acc_bs)
            acc_out[...] = jnp.zeros_like(acc_out)
            acc_loss[...] = jnp.zeros_like(acc_loss)
            acc_vec[...] = jnp.zeros_like(acc_vec)

        gate = mod_ref[2:3, :]
        npost = npost_ref[...]
        yp, ys = yp_ref[...], ys_ref[...]
        sgp = jax.nn.sigmoid(gp_ref[...].astype(F32))
        sgs = jax.nn.sigmoid(gs_ref[...].astype(F32))
        pb = _dot(yp, wbp_ref[...])
        psm = _dot(ys, wbs_ref[...])
        mb = (sgp * pb + sgs * psm).astype(BF16)
        out = _dot(mb, wout_ref[...])
        on, r = _rms_parts(out)
        normed = on * npost
        diff = x_ref[...] + gate * normed - t_ref[...]
        acc_loss[...] += jnp.sum(diff * diff, axis=0, keepdims=True)
        dy = diff * (1.0 / n_feat)
        dy_ref[...] = dy
        acc_vec[0:1, :] += jnp.sum(dy * normed, axis=0, keepdims=True)
        dn = dy * gate
        acc_vec[1:2, :] += jnp.sum(dn * on, axis=0, keepdims=True)
        dout = _rms_bwd(dn * npost, on, r).astype(BF16)
        dm = _dot_nt(dout, wout_ref[...])
        dpb = (dm * sgp).astype(BF16)
        dps = (dm * sgs).astype(BF16)
        dg_ref[:, :D] = (dm * pb * sgp * (1.0 - sgp)).astype(BF16)
        dg_ref[:, D:] = (dm * psm * sgs * (1.0 - sgs)).astype(BF16)
        dyp_ref[...] = _dot_nt(dpb, wbp_ref[...]).astype(BF16)
        dys_ref[...] = _dot_nt(dps, wbs_ref[...]).astype(BF16)
        acc_out[...] += _dot_tn(mb, dout)
        acc_bp[...] += _dot_tn(yp, dpb)
        acc_bs[...] += _dot_tn(ys, dps)

        @pl.when(i == nblk - 1)
        def _():
            loss_ref[...] = 0.5 / n_feat * jnp.sum(acc_loss[...], axis=1, keepdims=True)
            vec_ref[...] = acc_vec[...]
            pltpu.sync_copy(acc_bp, dwbp_hbm)
            pltpu.sync_copy(acc_bs, dwbs_hbm)
            pltpu.sync_copy(acc_out, dwout_hbm)

    row = lambda c: pl.BlockSpec((tb, D), lambda i: (i, c))
    w = _full((D, D), single=True)
    return _pcall(body, name="head", grid=(nblk,),
                  out_shape=(jax.ShapeDtypeStruct((1, 1), F32), jax.ShapeDtypeStruct((rows, D), F32),
                             jax.ShapeDtypeStruct((rows, D), BF16), jax.ShapeDtypeStruct((rows, D), BF16),
                             jax.ShapeDtypeStruct((rows, 2 * D), BF16),
                             jax.ShapeDtypeStruct((D, D), F32), jax.ShapeDtypeStruct((D, D), F32),
                             jax.ShapeDtypeStruct((D, D), F32), jax.ShapeDtypeStruct((2, D), F32)),
                  in_specs=[row(0), row(0), row(4), row(5), row(0), row(0), _full((3, D)), _full((1, D)), w, w, w],
                  out_specs=(_full((1, 1)), row(0), row(0), row(0), pl.BlockSpec((tb, 2 * D), lambda i: (i, 0)),
                             ANY, ANY, ANY, _full((2, D))),
                  scratch_shapes=[pltpu.VMEM((D, D), F32), pltpu.VMEM((D, D), F32), pltpu.VMEM((D, D), F32),
                                  pltpu.VMEM((1, D), F32), pltpu.VMEM((2, D), F32)],
                  compiler_params=_params(("arbitrary",)))(x, target, proj, proj, y_pool, y_ssm, mod3, norm_post,
                                                           wbp, wbs, wout)


def _glu_bwd(dys, zp, ys_pre, gate, pm, pmt, glu_w):
    rows = dys.shape[0]
    tb = pm.shape[0]
    nblk = rows // tb

    def body(dys_ref, z_ref, ysp_ref, sg_ref, pm_ref, pmt_ref, gw_ref, dyp_ref, dz_ref, dgw_hbm, dgb_ref,
             acc_w, acc_b):
        i = pl.program_id(0)

        @pl.when(i == 0)
        def _():
            acc_w[...] = jnp.zeros_like(acc_w)
            acc_b[...] = jnp.zeros_like(acc_b)

        d_out = _dot(pm_ref[...], dys_ref[...])
        yg, dgelu = _gelu_parts(ysp_ref[...])
        ygb = yg.astype(BF16)
        sg = sg_ref[...]
        silu_z, dsilu_z = _silu_parts(z_ref[...].astype(F32))
        dz = d_out * (yg * sg) * dsilu_z
        dz_ref[...] = _dot(pmt_ref[...], dz.astype(BF16)).astype(BF16)
        dglu = d_out * silu_z
        dq = dglu * yg * sg * (1.0 - sg)
        dqb = dq.astype(BF16)
        acc_b[...] += jnp.sum(dq, axis=0, keepdims=True)
        acc_w[...] += _dot_tn(ygb, dqb)
        dyg = dglu * sg + _dot_nt(dqb, gw_ref[...])
        dyp_ref[...] = (dyg * dgelu).astype(BF16)

        @pl.when(i == nblk - 1)
        def _():
            dgb_ref[...] = acc_b[...]
            pltpu.sync_copy(acc_w, dgw_hbm)

    row = lambda c: pl.BlockSpec((tb, D), lambda i: (i, c))
    return _pcall(body, name="glu_bwd", grid=(nblk,),
                  out_shape=(jax.ShapeDtypeStruct((rows, D), BF16), jax.ShapeDtypeStruct((rows, D), BF16),
                             jax.ShapeDtypeStruct((D, D), F32), jax.ShapeDtypeStruct((1, D), F32)),
                  in_specs=[row(0), row(0), row(0), row(0), _full((tb, tb)), _full((tb, tb)),
                            _full((D, D), single=True)],
                  out_specs=(row(0), row(0), ANY, _full((1, D))),
                  scratch_shapes=[pltpu.VMEM((D, D), F32), pltpu.VMEM((1, D), F32)],
                  compiler_params=_params(("arbitrary",)))(dys, zp, ys_pre, gate, pm, pmt, glu_w)


def _ssm_bwd(dyp, up, states, carries, pmt, wb, wct, ptab, dvec, mat_grads, dpool_w, dw_in_rest):
    rows = dyp.shape[0]
    tb = pmt.shape[0]
    k_steps = tb // SUBLANES
    nblk = rows // tb
    n_mat = len(mat_grads)
    hosted = [*mat_grads, dpool_w, dw_in_rest]
    n_h = len(hosted)
    shard_rows = D // N_DEV
    pool_rows = dpool_w.shape[1] // N_DEV
    items = [_scatter_item(t, t, _rows_of(shard_rows)) for t in range(n_mat)] + \
            [_scatter_item(n_mat, n_mat, _pool_rows_of(pool_rows))] + \
            [_w_in_block_item(n_mat + 1, n_mat + 1, j, ssm_part=False) for j in range(W_IN_SHARD // W_IN_BLOCK)]
    n_in, n_out = 9, 5

    def body(*refs):
        dyp_ref, u_ref, s_ref, cin_ref, pmt_ref, wb_ref, wct_ref, p_ref, d_ref = refs[:n_in]
        src_refs = refs[n_in:n_in + n_h]
        du_ref, dbb_ref, dcc_ref, da_ref, dd_ref = refs[n_in + n_h:n_in + n_h + n_out]
        recv_refs = refs[n_in + n_h + n_out:n_in + 2 * n_h + n_out]
        (g_ref, carry_b, fin_ref, acc_wb, acc_wct, acc_da, acc_dd, dup_ref,
         *sems) = refs[n_in + 2 * n_h + n_out:]
        i = pl.program_id(0)

        @pl.when(i == 0)
        def _():
            _hosted_copies(items, src_refs, recv_refs, *sems, act="start")
            carry_b[...] = jnp.zeros_like(carry_b)
            acc_wb[...] = jnp.zeros_like(acc_wb)
            acc_wct[...] = jnp.zeros_like(acc_wct)
            acc_da[...] = jnp.zeros_like(acc_da)
            acc_dd[...] = jnp.zeros_like(acc_dd)

        def own_products(acc, q, chan, state, base):
            chan_t = chan.T
            for j in range(16 // 2):
                r = slice(j * 2 * G_H, (j + 1) * 2 * G_H)
                re = base + j * 128
                slab = jnp.concatenate([state[:, re:re + 128], state[:, re + Q_W // 2:re + Q_W // 2 + 128]],
                                       axis=1).astype(BF16)
                acc[q, r, :] += _dot(chan_t[r, :], slab)

        dy = dyp_ref[...]
        up = u_ref[...]
        acc_dd[...] += jnp.sum(dy.astype(F32) * up.astype(F32), axis=0, keepdims=True)
        for q in range(N_Q):
            cols = slice(q * 256, (q + 1) * 256)
            g_ref[:, q * Q_W:(q + 1) * Q_W] = _dot(dy[:, cols], wct_ref[q])
            own_products(acc_wct, q, dy[:, cols], s_ref, q * Q_W)
        for q in range(N_Q):
            _scan_backward(q, g_ref, s_ref, p_ref, carry_b, cin_ref, fin_ref, acc_da, k_steps)
        for q in range(N_Q):
            cols = slice(q * 256, (q + 1) * 256)
            lam = g_ref[:, q * Q_W:(q + 1) * Q_W].astype(BF16)
            own_products(acc_wb, q, up[:, cols], lam, 0)
            dup_ref[:, cols] = (_dot_nt(lam, wb_ref[q]) + d_ref[:, cols] * dy[:, cols].astype(F32)).astype(BF16)
        du_ref[...] = _dot(pmt_ref[...], dup_ref[...]).astype(BF16)

        @pl.when(i == nblk - 1)
        def _():
            da_ref[...] = acc_da[...]
            dd_ref[...] = acc_dd[...]
            lane = lax.broadcasted_iota(jnp.int32, (16 * G_H, 128), 1)
            row = lax.broadcasted_iota(jnp.int32, (16 * G_H, 128), 0)
            own = lane // G_P == (row // G_H) % 2
            spread = (lax.broadcasted_iota(jnp.int32, (G_P, 128), 1) % G_P ==
                      lax.broadcasted_iota(jnp.int32, (G_P, 128), 0)).astype(F32)
            for acc, out in ((acc_wb, dbb_ref), (acc_wct, dcc_ref)):
                for half in range(2):
                    for q in range(N_Q):
                        kept = jnp.where(own, acc[q, :, half * 128:(half + 1) * 128], 0.0)
                        out[half, q] = lax.dot_general(kept, spread, (((1,), (1,)), ((), ())),
                                                       preferred_element_type=F32, precision=lax.Precision.HIGHEST)
            _hosted_copies(items, src_refs, recv_refs, *sems, act="wait")

    rev = lambda c: pl.BlockSpec((tb, D), lambda i: (nblk - 1 - i, c))
    recv = [jax.ShapeDtypeStruct((N_DEV, shard_rows, D), F32)] * n_mat + \
           [jax.ShapeDtypeStruct((N_DEV, dpool_w.shape[0], pool_rows, dpool_w.shape[2]), F32),
            jax.ShapeDtypeStruct((N_DEV, D, W_IN_SHARD), BF16)]
    return _pcall(body, name="ssm_bwd", grid=(nblk,),
                  out_shape=(jax.ShapeDtypeStruct((rows, D), BF16),
                             jax.ShapeDtypeStruct((2, N_Q, 16 * G_H, G_P), F32),
                             jax.ShapeDtypeStruct((2, N_Q, 16 * G_H, G_P), F32),
                             jax.ShapeDtypeStruct((1, N_STATE), F32), jax.ShapeDtypeStruct((1, D), F32), *recv),
                  in_specs=[rev(0), rev(0), pl.BlockSpec((tb, N_STATE), lambda i: (nblk - 1 - i, 0)),
                            pl.BlockSpec((None, 1, N_STATE), lambda i: (nblk - 1 - i, 0, 0)),
                            _full((tb, tb)),
                            _full((N_Q, 256, Q_W), single=True), _full((N_Q, 256, Q_W), single=True),
                            _full((k_steps, N_STATE)), _full((1, D))] + [ANY] * n_h,
                  out_specs=(rev(0), _full((2, N_Q, 16 * G_H, G_P)), _full((2, N_Q, 16 * G_H, G_P)),
                             _full((1, N_STATE)), _full((1, D)), *([ANY] * n_h)),
                  scratch_shapes=[pltpu.VMEM((tb, N_STATE), F32), pltpu.VMEM((1, N_STATE), F32),
                                  pltpu.VMEM((SUBLANES, N_STATE), F32),
                                  pltpu.VMEM((N_Q, 16 * G_H, 256), F32), pltpu.VMEM((N_Q, 16 * G_H, 256), F32),
                                  pltpu.VMEM((1, N_STATE), F32), pltpu.VMEM((1, D), F32),
                                  pltpu.VMEM((tb, D), BF16)] + _sem_scratch(items),
                  compiler_params=_params(("arbitrary",), vmem=60 * 1024 * 1024),
                  )(dyp, up, states, carries, pmt, wb, wct, ptab, dvec, *hosted)


def _pool_bwd(dyp, pooled, proj, pool_w, pool_scale):
    rows = dyp.shape[0]
    tb = _tb(rows, 1024)
    nblk = rows // tb

    def body(dy_ref, pooled_ref, z_ref, pw_ref, ps_ref, dp_ref, dpw_ref, dps_ref, ahead_ref):
        i = pl.program_id(0)
        blk = nblk - 1 - i

        @pl.when(i == 0)
        def _():
            ahead_ref[...] = jnp.zeros_like(ahead_ref)
            dpw_ref[...] = jnp.zeros_like(dpw_ref)
            dps_ref[...] = jnp.zeros_like(dps_ref)

        inv_counts = _inv_counts(tb, blk * tb)
        silu_z, dsilu_z = _silu_parts(z_ref[...].astype(F32))
        dy = dy_ref[...].astype(F32)
        for g, w in enumerate(POOL_WINDOWS):
            cols = slice(g * 256, (g + 1) * 256)
            pooled_b = pooled_ref[:, cols]
            mixed = _dot(pooled_b, pw_ref[g])
            scale = ps_ref[:, cols]
            dp_ref[:, D + g * 256:D + (g + 1) * 256] = (dy[:, cols] * (mixed * scale) * dsilu_z[:, cols]).astype(BF16)
            dms = dy[:, cols] * silu_z[:, cols]
            dps_ref[:, cols] += jnp.sum(dms * mixed, axis=0, keepdims=True)
            dmixed = (dms * scale).astype(BF16)
            dpw_ref[g] += _dot_tn(pooled_b, dmixed)
            dpooled = _dot_nt(dmixed, pw_ref[g])
            ratio = dpooled * inv_counts[g]
            acc = jnp.concatenate([ratio, ahead_ref[:, cols]], axis=0)
            ahead_ref[:, cols] = ratio[:HALO, :]
            s = 1
            while s < w:
                acc = acc + pltpu.roll(acc, tb + HALO - s, axis=0)
                s *= 2
            dp_ref[:, cols] = (acc[:tb, :] - dpooled).astype(BF16)

    rev = lambda c: pl.BlockSpec((tb, D), lambda i: (nblk - 1 - i, c))
    return _pcall(body, name="pool_bwd", grid=(nblk,),
                  out_shape=(jax.ShapeDtypeStruct((rows, 2 * D), BF16), jax.ShapeDtypeStruct((4, 256, 256), F32),
                             jax.ShapeDtypeStruct((1, D), F32)),
                  in_specs=[rev(0), rev(0), rev(1), _full((4, 256, 256)), _full((1, D))],
                  out_specs=(pl.BlockSpec((tb, 2 * D), lambda i: (nblk - 1 - i, 0)), _full((4, 256, 256)),
                             _full((1, D))),
                  scratch_shapes=[pltpu.VMEM((HALO, D), F32)],
                  compiler_params=_params(("arbitrary",)))(dyp, pooled, proj, pool_w, pool_scale)


def _dproj_specs(tb):
    return [pl.BlockSpec((tb, 2 * D), lambda i: (i, 0)), pl.BlockSpec((tb, D), lambda i: (i, 0)),
            pl.BlockSpec((tb, D), lambda i: (i, 0)), pl.BlockSpec((tb, 2 * D), lambda i: (i, 0))]


def _in_proj_bwd_x(x, dy, dpp, dus, dzs, dpg, mod3, norm_pre, w_in, dw_in_ssm, recv_w_in):
    rows = x.shape[0]
    tb = _tb(rows, 512)
    nblk = rows // tb
    items = [_w_in_block_item(0, 0, j, ssm_part=True) for j in range(W_IN_SHARD // W_IN_BLOCK)]
    sums_item = [_Item(0, 0, _whole, _slot)]

    def body(x_ref, dy_ref, dpp_ref, dus_ref, dzs_ref, dpg_ref, mod_ref, np_ref, w_ref,
             dw_src, _, gx_ref, recv_w, recv_sums, vec_ref, ssem, rsem, lsem, *sums_sems):
        src_refs, recv_refs, sems = (dw_src,), (recv_w,), (ssem, rsem, lsem)

        @pl.when(pl.program_id(0) == 0)
        def _():
            _hosted_copies(items, src_refs, recv_refs, *sems, act="start")
            vec_ref[...] = jnp.zeros_like(vec_ref)

        dh = _dot_nt(dpp_ref[...], w_ref[:, 0:2 * D])
        dh += _dot_nt(dus_ref[...], w_ref[:, 2 * D:3 * D])
        dh += _dot_nt(dzs_ref[...], w_ref[:, 3 * D:4 * D])
        dh += _dot_nt(dpg_ref[...], w_ref[:, 4 * D:6 * D])
        xn, r, _ = _prenorm(x_ref[...], mod_ref[...], np_ref[...])
        one_scale = 1.0 + mod_ref[1:2, :]
        vec_ref[0:1, :] += jnp.sum(dh, axis=0, keepdims=True)
        vec_ref[1:2, :] += jnp.sum(dh * xn, axis=0, keepdims=True) * np_ref[...]
        vec_ref[2:3, :] += jnp.sum(dh * xn, axis=0, keepdims=True) * one_scale
        gx_ref[...] = dy_ref[...] + _rms_bwd(dh * (np_ref[...] * one_scale), xn, r)

        @pl.when(pl.program_id(0) == nblk - 1)
        def _():
            _hosted_copies(sums_item, (vec_ref,), (recv_sums,), *sums_sems, act="start")
            _hosted_copies(items, src_refs, recv_refs, *sems, act="wait")
            _hosted_copies(sums_item, (vec_ref,), (recv_sums,), *sums_sems, act="wait")

    row = pl.BlockSpec((tb, D), lambda i: (i, 0))
    recv = (jax.ShapeDtypeStruct(recv_w_in.shape, recv_w_in.dtype), jax.ShapeDtypeStruct((N_DEV, 3, D), F32))
    return _pcall(body, name="in_proj_bwd_x", grid=(nblk,),
                  out_shape=(jax.ShapeDtypeStruct((rows, D), F32), *recv),
                  in_specs=[row, row] + _dproj_specs(tb) + [_full((3, D)), _full((1, D)),
                                                            _full((D, N_IN), single=True)] + [ANY] * 2,
                  out_specs=(row, ANY, ANY),
                  input_output_aliases={10: 1},
                  scratch_shapes=[pltpu.VMEM((3, D), F32)] + _sem_scratch(items) + _sem_scratch(sums_item),
                  compiler_params=_params(("arbitrary",)))(x, dy, dpp, dus, dzs, dpg, mod3, norm_pre, w_in,
                                                           dw_in_ssm, recv_w_in)


def _in_proj_bwd_w(name, x, dparts, mod3, norm_pre, gathered=()):
    rows = x.shape[0]
    tb = _tb(rows, 512)
    nblk = rows // tb
    widths = [p.shape[1] for p in dparts]
    n_p, n_g = len(dparts), len(gathered)
    items = [_Item(t, t, _whole, _slot) for t in range(n_g)]

    def body(x_ref, *rest):
        part_refs, (mod_ref, np_ref) = rest[:n_p], rest[n_p:n_p + 2]
        src_refs, dw_ref = rest[n_p + 2:n_p + 2 + n_g], rest[n_p + 2 + n_g]
        recv_refs, (acc, *sems) = rest[n_p + 3 + n_g:n_p + 3 + 2 * n_g], rest[n_p + 3 + 2 * n_g:]
        i = pl.program_id(0)

        @pl.when(i == 0)
        def _():
            if n_g:
                _hosted_copies(items, src_refs, recv_refs, *sems, act="start")
            acc[...] = jnp.zeros_like(acc)

        _, _, h = _prenorm(x_ref[...], mod_ref[...], np_ref[...])
        ht = h.astype(BF16)
        lo = 0
        for ref, w in zip(part_refs, widths):
            acc[:, lo:lo + w] += _dot_tn(ht, ref[...])
            lo += w

        @pl.when(i == nblk - 1)
        def _():
            dw_ref[...] = acc[...].astype(BF16)
            if n_g:
                _hosted_copies(items, src_refs, recv_refs, *sems, act="wait")

    row = pl.BlockSpec((tb, D), lambda i: (i, 0))
    out = _pcall(body, name=name, grid=(nblk,),
                 out_shape=(jax.ShapeDtypeStruct((D, sum(widths)), BF16),
                            *[jax.ShapeDtypeStruct((N_DEV,) + g.shape, g.dtype) for g in gathered]),
                 in_specs=[row] + [pl.BlockSpec((tb, w), lambda i: (i, 0)) for w in widths] +
                          [_full((3, D)), _full((1, D))] + [ANY] * n_g,
                 out_specs=(_full((D, sum(widths))), *([ANY] * n_g)),
                 scratch_shapes=[pltpu.VMEM((D, sum(widths)), F32)] + (_sem_scratch(items) if n_g else []),
                 compiler_params=_params(("arbitrary",)))(x, *dparts, mod3, norm_pre, *gathered)
    return out if n_g else out[0]


def _adamw_math(w, g, m, v):
    m = ADAM_B1 * m + (1.0 - ADAM_B1) * g
    v = ADAM_B2 * v + (1.0 - ADAM_B2) * (g * g)
    m_hat = m / (1.0 - ADAM_B1 ** ADAM_STEP)
    v_hat = v / (1.0 - ADAM_B2 ** ADAM_STEP)
    delta = -ADAM_LR * (m_hat / (jnp.sqrt(v_hat) + ADAM_EPS) + ADAM_WD * w)
    return delta, m, v


def _sum_sources(ref):
    g = ref[0].astype(F32)
    for s in range(1, N_DEV):
        g = g + ref[s].astype(F32)
    return g


def _adamw_reduce(name, parts, w, m, v):
    r, c = w.shape
    tr = r if r * c <= 256 * 1024 else max(8, (256 * 1024 // c) // 8 * 8)
    while r % tr:
        tr -= 8

    def body(p_ref, w_ref, m_ref, v_ref, g_ref, d_ref, nm_ref, nv_ref):
        g = _sum_sources(p_ref)
        g_ref[...] = g
        d_ref[...], nm_ref[...], nv_ref[...] = _adamw_math(w_ref[...], g, m_ref[...], v_ref[...])

    blk = pl.BlockSpec((tr, c), lambda i: (i, 0))
    return _pcall(body, name=name, grid=(r // tr,),
                  out_shape=tuple([jax.ShapeDtypeStruct((r, c), F32)] * 4),
                  in_specs=[pl.BlockSpec((N_DEV, tr, c), lambda i: (0, i, 0)), blk, blk, blk],
                  out_specs=(blk, blk, blk, blk),
                  compiler_params=_params(("arbitrary",)))(parts, w, m, v)


SC_TILES = 32
SC_LANES = 16


def _adamw_reduce_sc(name, parts, w, m, v):
    rows, cols = w.shape
    per = rows // SC_TILES

    def body(p_hbm, w_hbm, m_hbm, v_hbm, g_hbm, d_hbm, nm_hbm, nv_hbm, pbuf, wbuf, mbuf, vbuf):
        tile = lax.axis_index("subcore") * 2 + lax.axis_index("core")
        mine = pl.ds(tile * per, per)
        for s in range(N_DEV):
            pltpu.sync_copy(p_hbm.at[s, mine, :], pbuf.at[s])
        pltpu.sync_copy(w_hbm.at[mine, :], wbuf)
        pltpu.sync_copy(m_hbm.at[mine, :], mbuf)
        pltpu.sync_copy(v_hbm.at[mine, :], vbuf)
        for r in range(per):
            @pl.loop(0, cols, step=SC_LANES)
            def _(c, r=r):
                lanes = pl.ds(c, SC_LANES)
                g = pbuf[0, r, lanes]
                for s in range(1, N_DEV):
                    g = g + pbuf[s, r, lanes]
                delta, nm, nv = _adamw_math(wbuf[r, lanes], g, mbuf[r, lanes], vbuf[r, lanes])
                pbuf[0, r, lanes] = g
                wbuf[r, lanes] = delta
                mbuf[r, lanes] = nm
                vbuf[r, lanes] = nv
        pltpu.sync_copy(pbuf.at[0], g_hbm.at[mine, :])
        pltpu.sync_copy(wbuf, d_hbm.at[mine, :])
        pltpu.sync_copy(mbuf, nm_hbm.at[mine, :])
        pltpu.sync_copy(vbuf, nv_hbm.at[mine, :])

    return pl.kernel(
        body, name=name, out_type=[jax.ShapeDtypeStruct((rows, cols), F32)] * 4,
        mesh=plsc.VectorSubcoreMesh(core_axis_name="core", subcore_axis_name="subcore"),
        scratch_types=[pltpu.VMEM((N_DEV, per, cols), F32), pltpu.VMEM((per, cols), F32),
                       pltpu.VMEM((per, cols), F32), pltpu.VMEM((per, cols), F32)],
    )(parts, w, m, v)


def _adamw_small(gs, ws, ms, vs):
    n = len(gs)

    def body(*refs):
        ins, outs = refs[:4 * n], refs[4 * n:]
        for t in range(n):
            g_ref, w_ref, m_ref, v_ref = ins[4 * t:4 * t + 4]
            outs[3 * t][...], outs[3 * t + 1][...], outs[3 * t + 2][...] = _adamw_math(
                w_ref[...], g_ref[...], m_ref[...], v_ref[...])

    vm = pl.BlockSpec(memory_space=pltpu.VMEM)
    flat = [a for t in range(n) for a in (gs[t], ws[t], ms[t], vs[t])]
    return _pcall(body, name="adamw_small",
                  out_shape=tuple(jax.ShapeDtypeStruct(w.shape, F32) for w in ws for _ in range(3)),
                  in_specs=[vm] * (4 * n), out_specs=tuple([vm] * (3 * n)), compiler_params=_params())(*flat)


def _sum_small(parts):
    n = len(parts)

    def body(*refs):
        for t in range(n):
            refs[n + t][...] = _sum_sources(refs[t])

    vm = pl.BlockSpec(memory_space=pltpu.VMEM)
    return _pcall(body, name="sum_small",
                  out_shape=tuple(jax.ShapeDtypeStruct(p.shape[1:], F32) for p in parts),
                  in_specs=[vm] * n, out_specs=tuple([vm] * n), compiler_params=_params())(*parts)


def _ada_update(c_all, dmod_cols, w, m, v):
    def body(c_ref, dm_ref, w_ref, m_ref, v_ref, g_ref, d_ref, nm_ref, nv_ref):
        ca = c_ref[...]
        g = lax.dot_general(ca * jax.nn.sigmoid(ca), dm_ref[...], (((0,), (0,)), ((), ())),
                            preferred_element_type=F32, precision=lax.Precision.HIGHEST)
        g_ref[...] = g
        d_ref[...], nm_ref[...], nv_ref[...] = _adamw_math(w_ref[...], g, m_ref[...], v_ref[...])

    vm = pl.BlockSpec(memory_space=pltpu.VMEM)
    return _pcall(body, name="ada_update", out_shape=tuple([jax.ShapeDtypeStruct(w.shape, F32)] * 4),
                  in_specs=[vm] * 5, out_specs=(vm, vm, vm, vm), compiler_params=_params())(c_all, dmod_cols, w, m, v)


def kernel(x, c, w_ada, b_ada, norm_pre, norm_post, w_in, pool_w, pool_scale, ssm_a_re, ssm_a_im, ssm_log_dt, ssm_b_re, ssm_b_im, ssm_c_re, ssm_c_im, ssm_d, glu_w, glu_b, w_branch_pool, w_branch_ssm, w_out, loss_target, m_w_ada, m_b_ada, m_norm_pre, m_norm_post, m_w_in, m_pool_w, m_pool_scale, m_ssm_a_re, m_ssm_a_im, m_ssm_log_dt, m_ssm_b_re, m_ssm_b_im, m_ssm_c_re, m_ssm_c_im, m_ssm_d, m_glu_w, m_glu_b, m_w_branch_pool, m_w_branch_ssm, m_w_out, v_w_ada, v_b_ada, v_norm_pre, v_norm_post, v_w_in, v_pool_w, v_pool_scale, v_ssm_a_re, v_ssm_a_im, v_ssm_log_dt, v_ssm_b_re, v_ssm_b_im, v_ssm_c_re, v_ssm_c_im, v_ssm_d, v_glu_w, v_glu_b, v_w_branch_pool, v_w_branch_ssm, v_w_out):
    given = dict(locals())
    me = _flat(_me())
    rows = x.shape[1]
    x2 = x[0]
    target = loss_target[0]
    ada_cols = w_ada.shape[2]

    tb_ssm = _tb(rows, 256)
    k_steps = tb_ssm // SUBLANES
    a_re, a_im = ssm_a_re[0], ssm_a_im[0]
    log_dt = ssm_log_dt[0].reshape(GROUPS, 1)
    b_re_t, b_im_t = ssm_b_re[0].transpose(0, 2, 1), ssm_b_im[0].transpose(0, 2, 1)
    s5_params = [a_re, a_im, log_dt, b_re_t, b_im_t, ssm_c_re[0], ssm_c_im[0]]

    f32_shards = [w_in[0], pool_w[0], glu_w[0], w_branch_pool[0], w_branch_ssm[0], w_out[0]]
    n_sh = len(f32_shards)

    def local_work(ins, outs):
        for src, dst in zip(ins[:n_sh], outs[:n_sh]):
            dst[...] = src[...].astype(BF16)
        _s5_prep_body(*ins[n_sh:], *outs[n_sh:])

    b_ada_s = lax.dynamic_slice(b_ada, (0, me * ada_cols), (1, ada_cols))
    c_all, mod_rows, *local = _ada_exchange(
        c, w_ada[0], b_ada_s, f32_shards + s5_params,
        [jax.ShapeDtypeStruct(a.shape, BF16) for a in f32_shards] + list(_s5_prep_structs(k_steps)), local_work)
    mod3 = mod_rows.reshape(3, D)
    shards, (wb, wct, pow_re, pow_im) = local[:n_sh], local[n_sh:]
    ptab = _state_layout(pow_re, pow_im)
    dvec = ssm_d[0].reshape(1, D)
    pm = _perm_matrix(tb_ssm)
    pmt = pm.T

    proj, w_in_g, pool_w_g, glu_g = _in_proj(x2, mod3, norm_pre, shards[0], shards[1:3])
    y_pool, pooled = _pool_fwd(proj, pool_w_g, pool_scale)
    y_ssm, ys_pre, carries, states, glu_gate, z_perm, u_perm, wbp_g, wbs_g, wout_g = _ssm_fwd(
        proj, pm, pmt, wb, wct, ptab, dvec, glu_g, glu_b, shards[3:])
    loss_part, dy, dyp, dys, dpg, dwbp, dwbs, dwout, head_vec = _head(
        x2, target, proj, y_pool, y_ssm, mod3, norm_post, wbp_g, wbs_g, wout_g)

    dpp, dpool_w, dpool_scale = _pool_bwd(dyp, pooled, proj, pool_w_g, pool_scale)
    dw_in_rest = _in_proj_bwd_w("in_proj_bwd_w_rest", x2, [dpp, dpg], mod3, norm_pre)
    dy_pre, dzs, dglu_w, dglu_b = _glu_bwd(dys, z_perm, ys_pre, glu_gate, pm, pmt, glu_g)
    dus, dbb, dcc, dabar, dd, p_glu, p_wbp, p_wbs, p_wout, p_pool_w, p_w_in = _ssm_bwd(
        dy_pre, u_perm, states, carries, pmt, wb, wct, ptab, dvec, [dglu_w, dwbp, dwbs, dwout], dpool_w,
        dw_in_rest)

    sc_updates = {
        name: _adamw_reduce_sc("adamw_sc_" + name, parts, given[name][0], given["m_" + name][0], given["v_" + name][0])
        for name, parts in (("glu_w", p_glu), ("w_branch_pool", p_wbp), ("w_branch_ssm", p_wbs), ("w_out", p_wout))}

    small32 = jnp.concatenate([head_vec, dpool_scale, dglu_b, dd, jnp.broadcast_to(loss_part, (1, D)),
                               jnp.zeros((2, D), F32), dabar.reshape(8, D)], axis=0)
    small16 = jnp.concatenate([dbb.reshape(2 * GROUPS, D), dcc.reshape(2 * GROUPS, D)], axis=0).astype(BF16)
    dw_in_ssm, p_small32, p_small16 = _in_proj_bwd_w("in_proj_bwd_w_ssm", x2, [dus, dzs], mod3, norm_pre,
                                                     gathered=(small32, small16))
    grad_x, p_w_in, p_pre = _in_proj_bwd_x(x2, dy, dpp, dus, dzs, dpg, mod3, norm_pre, w_in_g, dw_in_ssm, p_w_in)

    tot32, tot16, tot_pre = _sum_small([p_small32, p_small16, p_pre])
    d_abar_re, d_abar_im = _state_unlayout(tot32[8:16].reshape(N_STATE))
    d_bb_re, d_bb_im = tot16[0:64].reshape(GROUPS, G_H, G_P), tot16[64:128].reshape(GROUPS, G_H, G_P)
    g_a_re, g_a_im, g_log_dt, g_b_re_t, g_b_im_t = _s5_prep_bwd(
        a_re, a_im, log_dt, b_re_t, b_im_t, d_abar_re, d_abar_im, d_bb_re, d_bb_im)

    grads, deltas, new_m, new_v = {}, {}, {}, {}

    small = []

    def small_update(name, g2):
        small.append((name, g2))

    def shard_update(name, parts):
        shape = given[name].shape
        r2 = parts.shape[1:] if parts.ndim == 3 else (parts.shape[1] * parts.shape[2], parts.shape[3])
        w2, m2, v2 = (given[p + name].reshape(r2) for p in ("", "m_", "v_"))
        out = _adamw_reduce("adamw_" + name, parts.reshape((N_DEV,) + tuple(r2)), w2, m2, v2)
        grads[name], deltas[name], new_m[name], new_v[name] = (a.reshape(shape) for a in out)

    dmod_all = jnp.concatenate([p_pre[:, 0:2, :], p_small32[:, 0:1, :]], axis=1).reshape(N_DEV, 3 * D)
    dmod_cols = lax.dynamic_slice(dmod_all, (0, me * ada_cols), (N_DEV, ada_cols))
    out = _ada_update(c_all, dmod_cols, w_ada[0], m_w_ada[0], v_w_ada[0])
    grads['w_ada'], deltas['w_ada'], new_m['w_ada'], new_v['w_ada'] = (a.reshape(w_ada.shape) for a in out)

    small_update('b_ada', jnp.concatenate([tot_pre[0:2], tot32[0:1]], axis=0).reshape(1, 3 * D))
    small_update('norm_pre', tot_pre[2:3])
    small_update('norm_post', tot32[1:2])
    small_update('pool_scale', tot32[2:3])
    small_update('glu_b', tot32[3:4])
    small_update('ssm_d', tot32[4:5])
    small_update('ssm_a_re', g_a_re)
    small_update('ssm_a_im', g_a_im)
    small_update('ssm_log_dt', g_log_dt.reshape(1, GROUPS))
    small_update('ssm_b_re', g_b_re_t.transpose(0, 2, 1).reshape(GROUPS, G_P * G_H))
    small_update('ssm_b_im', g_b_im_t.transpose(0, 2, 1).reshape(GROUPS, G_P * G_H))
    small_update('ssm_c_re', tot16[128:192])
    small_update('ssm_c_im', -tot16[192:256])
    flat = _adamw_small([g2 for _, g2 in small],
                        *[[given[p + name].reshape(g2.shape) for name, g2 in small] for p in ("", "m_", "v_")])
    for t, (name, g2) in enumerate(small):
        shape = given[name].shape
        grads[name], deltas[name], new_m[name], new_v[name] = (
            a.reshape(shape) for a in (g2, *flat[3 * t:3 * t + 3]))
    shard_update('w_in', p_w_in)
    shard_update('pool_w', p_pool_w)
    for name, out in sc_updates.items():
        grads[name], deltas[name], new_m[name], new_v[name] = (a.reshape(given[name].shape) for a in out)

    return (tot32[5, 0], grad_x[None], *[grads[n] for n in WEIGHTS], *[deltas[n] for n in WEIGHTS],
            *[new_m[n] for n in WEIGHTS], *[new_v[n] for n in WEIGHTS])
```

```python
import math
from typing import Callable, NamedTuple, Optional

import jax
import jax.numpy as jnp
from jax import lax
from jax.experimental import pallas as pl
from jax.experimental.pallas import tpu as pltpu
from jax.experimental.pallas import tpu_sc as plsc

F32 = jnp.float32
BF16 = jnp.bfloat16
MESH = pl.DeviceIdType.MESH

D = 1024
N_DEV = 8
N_IN = 6 * D
GROUPS = 64
G_H = 16
G_P = 64
N_Q = 4
Q_W = 2 * 16 * G_P
N_STATE = N_Q * Q_W
POOL_WINDOWS = (2, 4, 8, 16)
HALO = 16
RMS_EPS = 1e-6
SUBLANES = 8
LANE_CHUNK = 512
SCAN_UNROLL = 2
VMEM_LIMIT = 56 * 1024 * 1024

ADAM_LR = 0.001
ADAM_B1 = 0.9
ADAM_B2 = 0.999
ADAM_EPS = 1e-08
ADAM_WD = 0.01
ADAM_STEP = 10

WEIGHTS = ['w_ada', 'b_ada', 'norm_pre', 'norm_post', 'w_in', 'pool_w', 'pool_scale', 'ssm_a_re',
           'ssm_a_im', 'ssm_log_dt', 'ssm_b_re', 'ssm_b_im', 'ssm_c_re', 'ssm_c_im', 'ssm_d', 'glu_w',
           'glu_b', 'w_branch_pool', 'w_branch_ssm', 'w_out']


def _pcall(body, **kw):
    return pl.pallas_call(body, **kw)


def _params(sem=None, vmem=VMEM_LIMIT):
    return pltpu.CompilerParams(dimension_semantics=sem, vmem_limit_bytes=vmem)


def _tb(rows, pref):
    return pref if rows % pref == 0 and rows // pref >= 2 else rows // 2


def _full(shape, single=False):
    nd = len(shape)
    if single:
        return pl.BlockSpec(shape, lambda i: (0,) * nd, pipeline_mode=pl.Buffered(1))
    return pl.BlockSpec(shape, lambda i: (0,) * nd)


ANY = pl.BlockSpec(memory_space=pl.ANY)


def _me():
    return lax.axis_index("x"), lax.axis_index("y"), lax.axis_index("c")


def _flat(p):
    return 4 * p[0] + 2 * p[1] + p[2]


def _peer(k):
    x, y, c = _me()
    return (1 - x if k & 4 else x, 1 - y if k & 2 else y, 1 - c if k & 1 else c)


def _silu_parts(z):
    s = jax.nn.sigmoid(z)
    return z * s, s * (1.0 + z * (1.0 - s))


_GELU_C = math.sqrt(2.0 / math.pi)


def _gelu_parts(x):
    x2 = x * x
    t = jnp.tanh(_GELU_C * (x + 0.044715 * x * x2))
    g = 0.5 * x * (1.0 + t)
    dg = 0.5 * (1.0 + t) + 0.5 * x * (1.0 - t * t) * (_GELU_C * (1.0 + 3.0 * 0.044715 * x2))
    return g, dg


def _dot(a, b):
    return jnp.dot(a, b, preferred_element_type=F32)


def _dot_nt(a, b):
    return lax.dot_general(a, b, (((1,), (1,)), ((), ())), preferred_element_type=F32)


def _dot_tn(a, b):
    return lax.dot_general(a, b, (((0,), (0,)), ((), ())), preferred_element_type=F32)


def _rms_parts(x):
    r = lax.rsqrt(jnp.mean(x * x, axis=-1, keepdims=True) + RMS_EPS)
    return x * r, r


def _rms_bwd(dxn, xn, r):
    return r * (dxn - xn * jnp.mean(dxn * xn, axis=-1, keepdims=True))


def _ada_exchange(c, w_ada_s, b_ada_s, local_ins, local_outs, local_work):
    cols = w_ada_s.shape[1]
    n_li, n_lo = len(local_ins), len(local_outs)

    def body(c_ref, w_ref, b_ref, *rest):
        li_refs, call_ref, mod_ref = rest[:n_li], rest[n_li], rest[n_li + 1]
        lo_refs, (part_ref, ssem, rsem, lsem) = rest[n_li + 2:n_li + 2 + n_lo], rest[n_li + 2 + n_lo:]
        me3 = _me()
        me = _flat(me3)
        mine = pltpu.make_async_copy(c_ref, call_ref.at[pl.ds(me, 1), :], lsem.at[0])
        mine.start()
        sends = []
        for k in range(1, N_DEV):
            cp = pltpu.make_async_remote_copy(src_ref=c_ref, dst_ref=call_ref.at[pl.ds(me, 1), :],
                                              send_sem=ssem.at[k - 1], recv_sem=rsem.at[k - 1],
                                              device_id=_peer(k), device_id_type=MESH)
            cp.start()
            sends.append(cp)
        local_work(li_refs, lo_refs)
        mine.wait()
        for k in range(1, N_DEV):
            p = _flat(_peer(k))
            pltpu.make_async_remote_copy(src_ref=c_ref, dst_ref=call_ref.at[pl.ds(p, 1), :],
                                         send_sem=ssem.at[k - 1], recv_sem=rsem.at[k - 1],
                                         device_id=_peer(k), device_id_type=MESH).wait_recv()
        for cp in sends:
            cp.wait_send()
        ca = call_ref[...]
        act = ca * jax.nn.sigmoid(ca)
        part_ref[...] = jnp.dot(act, w_ref[...], preferred_element_type=F32,
                                precision=lax.Precision.HIGHEST) + b_ref[...]
        own = pltpu.make_async_copy(part_ref.at[pl.ds(me, 1), :], mod_ref.at[pl.ds(me, 1), :], lsem.at[1])
        own.start()
        sends = []
        for k in range(1, N_DEV):
            p = _flat(_peer(k))
            s = N_DEV - 1 + k - 1
            cp = pltpu.make_async_remote_copy(src_ref=part_ref.at[pl.ds(p, 1), :],
                                              dst_ref=mod_ref.at[pl.ds(me, 1), :],
                                              send_sem=ssem.at[s], recv_sem=rsem.at[s],
                                              device_id=_peer(k), device_id_type=MESH)
            cp.start()
            sends.append(cp)
        own.wait()
        for k in range(1, N_DEV):
            p = _flat(_peer(k))
            s = N_DEV - 1 + k - 1
            pltpu.make_async_remote_copy(src_ref=part_ref.at[pl.ds(p, 1), :],
                                         dst_ref=mod_ref.at[pl.ds(p, 1), :],
                                         send_sem=ssem.at[s], recv_sem=rsem.at[s],
                                         device_id=_peer(k), device_id_type=MESH).wait_recv()
        for cp in sends:
            cp.wait_send()

    vm = pl.BlockSpec(memory_space=pltpu.VMEM)
    return _pcall(
        body, name="ada_exchange",
        out_shape=(jax.ShapeDtypeStruct((N_DEV, D), F32), jax.ShapeDtypeStruct((N_DEV, cols), F32), *local_outs),
        in_specs=[vm] * (3 + n_li), out_specs=tuple([vm] * (2 + n_lo)),
        scratch_shapes=[pltpu.VMEM((N_DEV, cols), F32),
                        pltpu.SemaphoreType.DMA((2 * (N_DEV - 1),)),
                        pltpu.SemaphoreType.DMA((2 * (N_DEV - 1),)),
                        pltpu.SemaphoreType.DMA((2,))],
        compiler_params=_params(),
    )(c, w_ada_s, b_ada_s, *local_ins)


class _Item(NamedTuple):
    src: int
    out: int
    src_view: Callable
    dst_view: Callable
    pred: Optional[Callable] = None


def _when(pred, dest, fn):
    if pred is None:
        fn()
    else:
        pl.when(pred(dest))(fn)


def _n_sems(items):
    return len(items) * (N_DEV - 1)


def _hosted_copies(items, srcs, outs, ssem, rsem, lsem, act):
    me = _flat(_me())
    for t, it in enumerate(items):
        local = lambda t=t, it=it: pltpu.make_async_copy(
            it.src_view(srcs[it.src], me), it.dst_view(outs[it.out], me), lsem.at[t])
        if act == "start":
            _when(it.pred, me, lambda local=local: local().start())
        else:
            _when(it.pred, me, lambda local=local: local().wait())
    for k in range(1, N_DEV):
        p3 = _peer(k)
        p = _flat(p3)
        for t, it in enumerate(items):
            s = t * (N_DEV - 1) + k - 1
            send = lambda it=it, s=s, p=p, p3=p3: pltpu.make_async_remote_copy(
                src_ref=it.src_view(srcs[it.src], p), dst_ref=it.dst_view(outs[it.out], me),
                send_sem=ssem.at[s], recv_sem=rsem.at[s], device_id=p3, device_id_type=MESH)
            recv = lambda it=it, s=s, p=p, p3=p3: pltpu.make_async_remote_copy(
                src_ref=it.src_view(srcs[it.src], p), dst_ref=it.dst_view(outs[it.out], p),
                send_sem=ssem.at[s], recv_sem=rsem.at[s], device_id=p3, device_id_type=MESH)
            if act == "start":
                _when(it.pred, p, lambda send=send: send().start())
            else:
                _when(it.pred, me, lambda recv=recv: recv().wait_recv())
                _when(it.pred, p, lambda send=send: send().wait_send())


def _sem_scratch(items):
    return [pltpu.SemaphoreType.DMA((_n_sems(items),)), pltpu.SemaphoreType.DMA((_n_sems(items),)),
            pltpu.SemaphoreType.DMA((len(items),))]


def _whole(ref, dest):
    return ref


def _slot(ref, sender):
    return ref.at[sender]


def _rows_of(rows):
    return lambda ref, dev: ref.at[pl.ds(dev * rows, rows), :]


def _pool_rows_of(rows):
    return lambda ref, dev: ref.at[:, pl.ds(dev * rows, rows), :]


def _gather_item(src, out, dst_view):
    return _Item(src, out, _whole, dst_view)


def _scatter_item(src, out, src_view):
    return _Item(src, out, src_view, _slot)


W_IN_BLOCK = 256
W_IN_SHARD = N_IN // N_DEV
SSM_BLOCKS = (2 * D // W_IN_BLOCK, 4 * D // W_IN_BLOCK)


def _w_in_block_item(src, out, j, ssm_part):
    def block(dest):
        return (W_IN_SHARD // W_IN_BLOCK) * dest + j

    def in_ssm(dest):
        b = block(dest)
        return (b >= SSM_BLOCKS[0]) & (b < SSM_BLOCKS[1])

    def src_view(ref, dest):
        b = block(dest)
        local = b - SSM_BLOCKS[0] if ssm_part else jnp.where(b < SSM_BLOCKS[0], b, b - (SSM_BLOCKS[1] - SSM_BLOCKS[0]))
        local = jnp.clip(local, 0, ref.shape[1] // W_IN_BLOCK - 1)
        return ref.at[:, pl.ds(local * W_IN_BLOCK, W_IN_BLOCK)]

    def dst_view(ref, sender):
        return ref.at[sender, :, pl.ds(j * W_IN_BLOCK, W_IN_BLOCK)]

    pred = in_ssm if ssm_part else (lambda dest: jnp.logical_not(in_ssm(dest)))
    return _Item(src, out, src_view, dst_view, pred)


def _s5_discretise(a_re, a_im, log_dt, b_re_t, b_im_t):
    dt = jnp.exp(log_dt)
    lam_re = jnp.minimum(a_re, -1e-4)
    lam_im = a_im
    mag = jnp.exp(lam_re * dt)
    abar_re = mag * jnp.cos(lam_im * dt)
    abar_im = mag * jnp.sin(lam_im * dt)
    den = lam_re * lam_re + lam_im * lam_im
    num_re = abar_re - 1.0
    f_re = (num_re * lam_re + abar_im * lam_im) / den
    f_im = (abar_im * lam_re - num_re * lam_im) / den
    f_re, f_im = f_re[:, None, :], f_im[:, None, :]
    bb_re = f_re * b_re_t - f_im * b_im_t
    bb_im = f_re * b_im_t + f_im * b_re_t
    return abar_re, abar_im, bb_re, bb_im


def _group_masks():
    spread = lax.broadcasted_iota(jnp.int32, (G_P, 16 * G_P), 1) % G_P == lax.broadcasted_iota(
        jnp.int32, (G_P, 16 * G_P), 0)
    own = lax.broadcasted_iota(jnp.int32, (16 * G_H, 16 * G_P), 0) // G_H == lax.broadcasted_iota(
        jnp.int32, (16 * G_H, 16 * G_P), 1) // G_P
    return spread, own


def _s5_prep_structs(n_pow):
    return (jax.ShapeDtypeStruct((N_Q, 16 * G_H, Q_W), BF16), jax.ShapeDtypeStruct((N_Q, 16 * G_H, Q_W), BF16),
            jax.ShapeDtypeStruct((n_pow, GROUPS, G_P), F32), jax.ShapeDtypeStruct((n_pow, GROUPS, G_P), F32))


def _s5_prep_body(ar_ref, ai_ref, ld_ref, br_ref, bi_ref, cr_ref, ci_ref, wb_ref, wct_ref, pr_ref, pi_ref):
    abar_re, abar_im, bb_re, bb_im = _s5_discretise(ar_ref[...], ai_ref[...], ld_ref[...], br_ref[...], bi_ref[...])
    spread, own = _group_masks()
    spread = spread.astype(BF16)
    for ref, parts in ((wb_ref, (bb_re, bb_im)), (wct_ref, (cr_ref[...], -ci_ref[...]))):
        for half, t in enumerate(parts):
            for q in range(N_Q):
                blocks = t[q * 16:(q + 1) * 16].reshape(16 * G_H, G_P).astype(BF16)
                dense = jnp.where(own, _dot(blocks, spread), 0.0)
                ref[q, :, half * (Q_W // 2):(half + 1) * (Q_W // 2)] = dense.astype(BF16)
    p_re, p_im = abar_re, abar_im
    pr_ref[0] = p_re
    pi_ref[0] = p_im
    for k in range(1, pr_ref.shape[0]):
        p_re, p_im = p_re * abar_re - p_im * abar_im, p_re * abar_im + p_im * abar_re
        pr_ref[k] = p_re
        pi_ref[k] = p_im


def _s5_prep_bwd(a_re, a_im, log_dt, b_re_t, b_im_t, d_abar_re, d_abar_im, d_bb_re, d_bb_im):
    def body(ar_ref, ai_ref, ld_ref, br_ref, bi_ref, dar_ref, dai_ref, dbr_ref, dbi_ref,
             gar_ref, gai_ref, gld_ref, gbr_ref, gbi_ref):
        _, vjp = jax.vjp(_s5_discretise, ar_ref[...], ai_ref[...], ld_ref[...], br_ref[...], bi_ref[...])
        g = vjp((dar_ref[...], dai_ref[...], dbr_ref[...], dbi_ref[...]))
        gar_ref[...] = g[0]
        gai_ref[...] = g[1]
        gld_ref[...] = g[2]
        gbr_ref[...] = g[3]
        gbi_ref[...] = g[4]

    vm = pl.BlockSpec(memory_space=pltpu.VMEM)
    ins = (a_re, a_im, log_dt, b_re_t, b_im_t)
    return _pcall(body, name="s5_prep_bwd",
                  out_shape=tuple(jax.ShapeDtypeStruct(a.shape, F32) for a in ins),
                  in_specs=[vm] * 9, out_specs=tuple([vm] * 5), compiler_params=_params(),
                  )(*ins, d_abar_re, d_abar_im, d_bb_re, d_bb_im)


def _state_layout(re, im):
    lead = re.shape[:-2]
    r = re.reshape(lead + (N_Q, 1, 16 * G_P))
    i = im.reshape(lead + (N_Q, 1, 16 * G_P))
    return jnp.concatenate([r, i], axis=-2).reshape(lead + (N_STATE,))


def _state_unlayout(v):
    v4 = v.reshape(N_Q, 2, 16, G_P)
    return v4[:, 0].reshape(GROUPS, G_P), v4[:, 1].reshape(GROUPS, G_P)


def _perm_matrix(tb):
    k_steps = tb // SUBLANES
    r = jnp.arange(tb)
    src = (r % SUBLANES) * k_steps + r // SUBLANES
    return (src[:, None] == jnp.arange(tb)[None, :]).astype(BF16)


def _lane_chunks(q):
    for lc in range(Q_W // 2 // LANE_CHUNK):
        re = q * Q_W + lc * LANE_CHUNK
        yield re, re + Q_W // 2


def _steps(lo, hi, body, init):
    if hi - lo <= SCAN_UNROLL:
        for k in range(lo, hi):
            init = body(k, init)
        return init
    trips = (hi - lo) // SCAN_UNROLL

    def trip(j, carry):
        for u in range(SCAN_UNROLL):
            carry = body(lo + j * SCAN_UNROLL + u, carry)
        return carry

    carry = lax.fori_loop(0, trips, trip, init)
    for k in range(lo + trips * SCAN_UNROLL, hi):
        carry = body(k, carry)
    return carry


def _tile(k):
    if isinstance(k, int):
        return pl.ds(k * SUBLANES, SUBLANES)
    return pl.ds(pl.multiple_of(k * SUBLANES, SUBLANES), SUBLANES)


def _scan_forward(q, s_ref, p_ref, carry_ref, enter_ref, fin_ref, k_steps):
    for re, im in _lane_chunks(q):
        lr, li = pl.ds(re, LANE_CHUNK), pl.ds(im, LANE_CHUNK)
        a_re = jnp.broadcast_to(p_ref[0:1, lr], (SUBLANES, LANE_CHUNK))
        a_im = jnp.broadcast_to(p_ref[0:1, li], (SUBLANES, LANE_CHUNK))

        def local(k, st):
            sr, si = st
            rows = _tile(k)
            nr = a_re * sr - a_im * si + s_ref[rows, lr]
            ni = a_re * si + a_im * sr + s_ref[rows, li]
            s_ref[rows, lr] = nr
            s_ref[rows, li] = ni
            return nr, ni

        zero = jnp.zeros((SUBLANES, LANE_CHUNK), F32)
        fr, fi = _steps(0, k_steps, local, (zero, zero))
        fin_ref[:, lr] = fr
        fin_ref[:, li] = fi
        ak_re, ak_im = p_ref[k_steps - 1:k_steps, lr], p_ref[k_steps - 1:k_steps, li]
        c_re, c_im = carry_ref[:, lr], carry_ref[:, li]
        for seg in range(SUBLANES):
            enter_ref[seg:seg + 1, lr] = c_re
            enter_ref[seg:seg + 1, li] = c_im
            f_re, f_im = fin_ref[seg:seg + 1, lr], fin_ref[seg:seg + 1, li]
            c_re, c_im = f_re + ak_re * c_re - ak_im * c_im, f_im + ak_re * c_im + ak_im * c_re
        carry_ref[:, lr] = c_re
        carry_ref[:, li] = c_im
        e_re, e_im = enter_ref[:, lr], enter_ref[:, li]

        def fix(k, _):
            rows = _tile(k)
            p_re = p_ref[pl.ds(k, 1), lr]
            p_im = p_ref[pl.ds(k, 1), li]
            s_ref[rows, lr] = s_ref[rows, lr] + (p_re * e_re - p_im * e_im)
            s_ref[rows, li] = s_ref[rows, li] + (p_re * e_im + p_im * e_re)
            return 0

        _steps(0, k_steps, fix, 0)


def _scan_backward(q, g_ref, s_ref, p_ref, carry_ref, s_in_ref, fin_ref, da_ref, k_steps):
    seg_id = lax.broadcasted_iota(jnp.int32, (SUBLANES, LANE_CHUNK), 0)
    for re, im in _lane_chunks(q):
        lr, li = pl.ds(re, LANE_CHUNK), pl.ds(im, LANE_CHUNK)
        a_re = jnp.broadcast_to(p_ref[0:1, lr], (SUBLANES, LANE_CHUNK))
        a_im = jnp.broadcast_to(p_ref[0:1, li], (SUBLANES, LANE_CHUNK))

        def local(j, st):
            sr, si = st
            rows = _tile(k_steps - 1 - j)
            nr = a_re * sr + a_im * si + g_ref[rows, lr]
            ni = a_re * si - a_im * sr + g_ref[rows, li]
            g_ref[rows, lr] = nr
            g_ref[rows, li] = ni
            return nr, ni

        zero = jnp.zeros((SUBLANES, LANE_CHUNK), F32)
        fr, fi = _steps(0, k_steps, local, (zero, zero))
        fin_ref[:, lr] = fr
        fin_ref[:, li] = fi
        ak_re, ak_im = p_ref[k_steps - 1:k_steps, lr], p_ref[k_steps - 1:k_steps, li]
        c_re, c_im = carry_ref[:, lr], carry_ref[:, li]
        lam_in = [None] * SUBLANES
        for seg in reversed(range(SUBLANES)):
            lam_in[seg] = (c_re, c_im)
            f_re, f_im = fin_ref[seg:seg + 1, lr], fin_ref[seg:seg + 1, li]
            c_re, c_im = f_re + ak_re * c_re + ak_im * c_im, f_im + ak_re * c_im - ak_im * c_re
        carry_ref[:, lr] = c_re
        carry_ref[:, li] = c_im
        for seg in range(SUBLANES):
            fin_ref[seg:seg + 1, lr] = lam_in[seg][0]
            fin_ref[seg:seg + 1, li] = lam_in[seg][1]
        e_re, e_im = fin_ref[:, lr], fin_ref[:, li]

        def fix_with(k, acc, sp_re, sp_im):
            acc_re, acc_im = acc
            rows = _tile(k)
            p_re = p_ref[pl.ds(k_steps - 1 - k, 1), lr]
            p_im = p_ref[pl.ds(k_steps - 1 - k, 1), li]
            l_re = g_ref[rows, lr] + (p_re * e_re + p_im * e_im)
            l_im = g_ref[rows, li] + (p_re * e_im - p_im * e_re)
            g_ref[rows, lr] = l_re
            g_ref[rows, li] = l_im
            return acc_re + (l_re * sp_re + l_im * sp_im), acc_im + (l_im * sp_re - l_re * sp_im)

        def fix(k, acc):
            prev = _tile(k - 1)
            return fix_with(k, acc, s_ref[prev, lr], s_ref[prev, li])

        last = _tile(k_steps - 1)
        before_re = jnp.where(seg_id == 0, s_in_ref[:, lr], pltpu.roll(s_ref[last, lr], 1, axis=0))
        before_im = jnp.where(seg_id == 0, s_in_ref[:, li], pltpu.roll(s_ref[last, li], 1, axis=0))
        acc = fix_with(0, (zero, zero), before_re, before_im)
        acc_re, acc_im = _steps(1, k_steps, fix, acc)
        da_ref[:, lr] = da_ref[:, lr] + jnp.sum(acc_re, axis=0, keepdims=True)
        da_ref[:, li] = da_ref[:, li] + jnp.sum(acc_im, axis=0, keepdims=True)


def _prenorm(x, mod3, norm_pre):
    xn, r = _rms_parts(x)
    return xn, r, xn * norm_pre * (1.0 + mod3[1:2, :]) + mod3[0:1, :]


CHIP_FLIPS = (4, 2, 6)


def _shard_order(me):
    flips = [0, 1] + [f + c for f in CHIP_FLIPS for c in (0, 1)]
    return jnp.stack([me ^ f for f in flips]).astype(jnp.int32)


def _in_proj(x, mod3, norm_pre, w_in_s, shards):
    rows = x.shape[0]
    tb = _tb(rows, 2048)
    nblk = rows // tb
    n_sh = len(shards)
    last_step = N_DEV - 1
    items = [_gather_item(0, 0, _pool_rows_of(shards[0].shape[1]))] + \
            [_gather_item(t, t, _rows_of(shards[t].shape[0])) for t in range(1, n_sh)]

    def body(order_ref, x_ref, mod_ref, np_ref, w_src, *rest):
        src_refs, proj_ref, w_full, out_refs = rest[:n_sh], rest[n_sh], rest[n_sh + 1], rest[n_sh + 2:2 * n_sh + 2]
        h_scr, wg, ssem, rsem, lsem, *sems = rest[2 * n_sh + 2:]
        s, i = pl.program_id(0), pl.program_id(1)
        me3 = _me()
        me = _flat(me3)
        sibling = _peer(1)

        def own_copy(slot, k):
            return pltpu.make_async_remote_copy(src_ref=w_src, dst_ref=wg.at[me], send_sem=ssem.at[slot],
                                                recv_sem=rsem.at[slot], device_id=_peer(k), device_id_type=MESH)

        def passed_copy(j):
            p = _flat(_peer(CHIP_FLIPS[j]))
            return pltpu.make_async_remote_copy(src_ref=wg.at[p], dst_ref=wg.at[p], send_sem=ssem.at[4 + j],
                                                recv_sem=rsem.at[4 + j], device_id=sibling, device_id_type=MESH)

        def arrival(slot, flip):
            p = _flat(_peer(flip))
            pltpu.make_async_remote_copy(src_ref=w_src, dst_ref=wg.at[p], send_sem=ssem.at[slot],
                                         recv_sem=rsem.at[slot], device_id=sibling, device_id_type=MESH).wait_recv()

        def keep(t):
            p = order_ref[t]
            return pltpu.make_async_copy(wg.at[p], w_full.at[:, pl.ds(p * W_IN_SHARD, W_IN_SHARD)], lsem.at[1 + t])

        first = i == 0
        for t in range(last_step):
            pl.when(first & (s == t + 1))(lambda t=t: keep(t).start())

        @pl.when(first & (s == 0))
        def _():
            mine = pltpu.make_async_copy(w_src, wg.at[me], lsem.at[0])
            mine.start()
            own_copy(0, 1).start()
            for j, f in enumerate(CHIP_FLIPS[:2]):
                own_copy(1 + j, f).start()
            mine.wait()

        @pl.when(first & (s == 1))
        def _():
            arrival(0, 1)

        for j, f in enumerate(CHIP_FLIPS):
            @pl.when(first & (s == 2 + 2 * j))
            def _(j=j, f=f):
                arrival(1 + j, f)
                passed_copy(j).start()
                if j == 0:
                    own_copy(3, CHIP_FLIPS[2]).start()

            @pl.when(first & (s == 3 + 2 * j))
            def _(j=j, f=f):
                arrival(4 + j, f + 1)

        @pl.when(first & (s == last_step - 1))
        def _():
            _hosted_copies(items, src_refs, out_refs, *sems, act="start")

        rows_i = pl.ds(pl.multiple_of(i * tb, tb), tb)

        @pl.when(s == 0)
        def _():
            _, _, h = _prenorm(x_ref[...], mod_ref[...], np_ref[...])
            h_scr[rows_i, :] = h.astype(BF16)

        proj_ref[...] = _dot(h_scr[rows_i, :], wg[order_ref[s]]).astype(BF16)

        @pl.when((s == last_step) & (i == nblk - 1))
        def _():
            own_copy(0, 1).wait_send()
            for j, f in enumerate(CHIP_FLIPS):
                own_copy(1 + j, f).wait_send()
                passed_copy(j).wait_send()
            keep(last_step).start()
            for t in range(N_DEV):
                keep(t).wait()
            _hosted_copies(items, src_refs, out_refs, *sems, act="wait")

    full = [jax.ShapeDtypeStruct((4, 256, 256), BF16)] + [jax.ShapeDtypeStruct((D, D), BF16)] * (n_sh - 1)
    grid_spec = pltpu.PrefetchScalarGridSpec(
        num_scalar_prefetch=1, grid=(N_DEV, nblk),
        in_specs=[pl.BlockSpec((tb, D), lambda s, i, order: (jnp.where(s == 0, i, nblk - 1), 0)),
                  pl.BlockSpec((3, D), lambda s, i, order: (0, 0)), pl.BlockSpec((1, D), lambda s, i, order: (0, 0)),
                  ANY] + [ANY] * n_sh,
        out_specs=(pl.BlockSpec((tb, W_IN_SHARD), lambda s, i, order: (i, order[s])), ANY, *([ANY] * n_sh)),
        scratch_shapes=[pltpu.VMEM((rows, D), BF16), pltpu.VMEM((N_DEV, D, W_IN_SHARD), BF16),
                        pltpu.SemaphoreType.DMA((N_DEV - 1,)), pltpu.SemaphoreType.DMA((N_DEV - 1,)),
                        pltpu.SemaphoreType.DMA((1 + N_DEV,))] + _sem_scratch(items))
    return _pcall(body, name="in_proj", grid_spec=grid_spec,
                  out_shape=(jax.ShapeDtypeStruct((rows, N_IN), BF16), jax.ShapeDtypeStruct((D, N_IN), BF16), *full),
                  compiler_params=_params(("arbitrary", "arbitrary")),
                  )(_shard_order(_flat(_me())), x, mod3, norm_pre, w_in_s, *shards)


def _pool_windows(ext, tb, first_row):
    inv_counts = _inv_counts(tb, first_row)
    pooled = []
    for g, w in enumerate(POOL_WINDOWS):
        acc = ext[:, g * 256:(g + 1) * 256]
        tok = acc[HALO:, :]
        s = 1
        while s < w:
            acc = acc + pltpu.roll(acc, s, axis=0)
            s *= 2
        pooled.append(acc[HALO:, :] * inv_counts[g] - tok)
    return pooled, inv_counts


def _inv_counts(tb, first_row):
    pos = (first_row + lax.broadcasted_iota(jnp.int32, (tb, 1), 0) + 1).astype(F32)
    return [1.0 / jnp.minimum(pos, float(w)) for w in POOL_WINDOWS]


def _pool_fwd(proj, pool_w, pool_scale):
    rows = proj.shape[0]
    tb = _tb(rows, 1024)
    hb = tb // HALO

    def body(u_ref, halo_ref, z_ref, pw_ref, ps_ref, y_ref, pooled_ref):
        i = pl.program_id(0)
        u = u_ref[...].astype(F32)
        halo = jnp.where(i > 0, halo_ref[...].astype(F32), 0.0)
        pooled, _ = _pool_windows(jnp.concatenate([halo, u], axis=0), tb, i * tb)
        silu_z, _ = _silu_parts(z_ref[...].astype(F32))
        for g in range(4):
            cols = slice(g * 256, (g + 1) * 256)
            pooled_b = pooled[g].astype(BF16)
            pooled_ref[:, cols] = pooled_b
            mixed = _dot(pooled_b, pw_ref[g])
            y_ref[:, cols] = (mixed * ps_ref[:, cols] * silu_z[:, cols]).astype(BF16)

    blk = pl.BlockSpec((tb, D), lambda i: (i, 0))
    return _pcall(body, name="pool_fwd", grid=(rows // tb,),
                  out_shape=(jax.ShapeDtypeStruct((rows, D), BF16), jax.ShapeDtypeStruct((rows, D), BF16)),
                  in_specs=[blk, pl.BlockSpec((HALO, D), lambda i: (jnp.maximum(i * hb - 1, 0), 0)),
                            pl.BlockSpec((tb, D), lambda i: (i, 1)),
                            _full((4, 256, 256)), _full((1, D))],
                  out_specs=(blk, blk),
                  compiler_params=_params(("arbitrary",)))(proj, proj, proj, pool_w, pool_scale)


def _ssm_fwd(proj, pm, pmt, wb, wct, ptab, dvec, glu_w, glu_b, shards):
    rows = proj.shape[0]
    tb = pm.shape[0]
    k_steps = tb // SUBLANES
    nblk = rows // tb
    n_sh = len(shards)
    items = [_gather_item(t, t, _rows_of(shards[t].shape[0])) for t in range(n_sh)]

    def body(u_ref, z_ref, pm_ref, pmt_ref, wb_ref, wct_ref, p_ref, d_ref, gw_ref, gb_ref, *rest):
        src_refs = rest[:n_sh]
        y_ref, ys_ref, carry_out_ref, s_ref, gate_ref, zp_ref, up_ref = rest[n_sh:n_sh + 7]
        out_refs = rest[n_sh + 7:2 * n_sh + 7]
        carry_ref, enter_ref, fin_ref, *sems = rest[2 * n_sh + 7:]

        @pl.when(pl.program_id(0) == 0)
        def _():
            _hosted_copies(items, src_refs, out_refs, *sems, act="start")
            carry_ref[...] = jnp.zeros_like(carry_ref)

        carry_out_ref[...] = carry_ref[...]
        up = _dot(pm_ref[...], u_ref[...]).astype(BF16)
        up_ref[...] = up

        for q in range(N_Q):
            s_ref[:, q * Q_W:(q + 1) * Q_W] = _dot(up[:, q * 256:(q + 1) * 256], wb_ref[q])
        for q in range(N_Q):
            _scan_forward(q, s_ref, p_ref, carry_ref, enter_ref, fin_ref, k_steps)
        for q in range(N_Q):
            cols = slice(q * 256, (q + 1) * 256)
            y = _dot_nt(s_ref[:, q * Q_W:(q + 1) * Q_W].astype(BF16), wct_ref[q])
            ys_ref[:, cols] = y + d_ref[:, cols] * up[:, cols].astype(F32)
        yg, _ = _gelu_parts(ys_ref[...])
        gate = jax.nn.sigmoid(_dot(yg.astype(BF16), gw_ref[...]) + gb_ref[...])
        gate_ref[...] = gate
        zp = _dot(pm_ref[...], z_ref[...])
        zp_ref[...] = zp.astype(BF16)
        silu_z, _ = _silu_parts(zp)
        y_ref[...] = _dot(pmt_ref[...], (yg * gate * silu_z).astype(BF16)).astype(BF16)

        @pl.when(pl.program_id(0) == nblk - 1)
        def _():
            _hosted_copies(items, src_refs, out_refs, *sems, act="wait")

    return _pcall(body, name="ssm_fwd", grid=(nblk,),
                  out_shape=(jax.ShapeDtypeStruct((rows, D), BF16), jax.ShapeDtypeStruct((rows, D), F32),
                             jax.ShapeDtypeStruct((nblk, 1, N_STATE), F32),
                             jax.ShapeDtypeStruct((rows, N_STATE), F32),
                             jax.ShapeDtypeStruct((rows, D), F32), jax.ShapeDtypeStruct((rows, D), BF16),
                             jax.ShapeDtypeStruct((rows, D), BF16),
                             *[jax.ShapeDtypeStruct((D, D), BF16)] * n_sh),
                  in_specs=[pl.BlockSpec((tb, D), lambda i: (i, 2)), pl.BlockSpec((tb, D), lambda i: (i, 3)),
                            _full((tb, tb)), _full((tb, tb)),
                            _full((N_Q, 256, Q_W), single=True), _full((N_Q, 256, Q_W), single=True),
                            _full((k_steps, N_STATE)), _full((1, D)), _full((D, D), single=True), _full((1, D))] +
                           [ANY] * n_sh,
                  out_specs=(pl.BlockSpec((tb, D), lambda i: (i, 0)), pl.BlockSpec((tb, D), lambda i: (i, 0)),
                             pl.BlockSpec((None, 1, N_STATE), lambda i: (i, 0, 0)),
                             pl.BlockSpec((tb, N_STATE), lambda i: (i, 0)),
                             *[pl.BlockSpec((tb, D), lambda i: (i, 0))] * 3, *([ANY] * n_sh)),
                  scratch_shapes=[pltpu.VMEM((1, N_STATE), F32),
                                  pltpu.VMEM((SUBLANES, N_STATE), F32), pltpu.VMEM((SUBLANES, N_STATE), F32)] +
                                 _sem_scratch(items),
                  compiler_params=_params(("arbitrary",)))(proj, proj, pm, pmt, wb, wct, ptab, dvec, glu_w, glu_b,
                                                           *shards)


def _head(x, target, proj, y_pool, y_ssm, mod3, norm_post, wbp, wbs, wout):
    rows = x.shape[0]
    tb = _tb(rows, 256)
    nblk = rows // tb
    n_feat = float(D)

    def body(x_ref, t_ref, gp_ref, gs_ref, yp_ref, ys_ref, mod_ref, npost_ref, wbp_ref, wbs_ref, wout_ref,
             loss_ref, dy_ref, dyp_ref, dys_ref, dg_ref, dwbp_hbm, dwbs_hbm, dwout_hbm, vec_ref,
             acc_bp, acc_bs, acc_out, acc_loss, acc_vec):
        i = pl.program_id(0)

        @pl.when(i == 0)
        def _():
            acc_bp[...] = jnp.zeros_like(acc_bp)
            acc_bs[...] = jnp.zeros_like(acc_bs)
            acc_out[...] = jnp.zeros_like(acc_out)
            acc_loss[...] = jnp.zeros_like(acc_loss)
            acc_vec[...] = jnp.zeros_like(acc_vec)

        gate = mod_ref[2:3, :]
        npost = npost_ref[...]
        yp, ys = yp_ref[...], ys_ref[...]
        sgp = jax.nn.sigmoid(gp_ref[...].astype(F32))
        sgs = jax.nn.sigmoid(gs_ref[...].astype(F32))
        pb = _dot(yp, wbp_ref[...])
        psm = _dot(ys, wbs_ref[...])
        mb = (sgp * pb + sgs * psm).astype(BF16)
        out = _dot(mb, wout_ref[...])
        on, r = _rms_parts(out)
        normed = on * npost
        diff = x_ref[...] + gate * normed - t_ref[...]
        acc_loss[...] += jnp.sum(diff * diff, axis=0, keepdims=True)
        dy = diff * (1.0 / n_feat)
        dy_ref[...] = dy
        acc_vec[0:1, :] += jnp.sum(dy * normed, axis=0, keepdims=True)
        dn = dy * gate
        acc_vec[1:2, :] += jnp.sum(dn * on, axis=0, keepdims=True)
        dout = _rms_bwd(dn * npost, on, r).astype(BF16)
        dm = _dot_nt(dout, wout_ref[...])
        dpb = (dm * sgp).astype(BF16)
        dps = (dm * sgs).astype(BF16)
        dg_ref[:, :D] = (dm * pb * sgp * (1.0 - sgp)).astype(BF16)
        dg_ref[:, D:] = (dm * psm * sgs * (1.0 - sgs)).astype(BF16)
        dyp_ref[...] = _dot_nt(dpb, wbp_ref[...]).astype(BF16)
        dys_ref[...] = _dot_nt(dps, wbs_ref[...]).astype(BF16)
        acc_out[...] += _dot_tn(mb, dout)
        acc_bp[...] += _dot_tn(yp, dpb)
        acc_bs[...] += _dot_tn(ys, dps)

        @pl.when(i == nblk - 1)
        def _():
            loss_ref[...] = 0.5 / n_feat * jnp.sum(acc_loss[...], axis=1, keepdims=True)
            vec_ref[...] = acc_vec[...]
            pltpu.sync_copy(acc_bp, dwbp_hbm)
            pltpu.sync_copy(acc_bs, dwbs_hbm)
            pltpu.sync_copy(acc_out, dwout_hbm)

    row = lambda c: pl.BlockSpec((tb, D), lambda i: (i, c))
    w = _full((D, D), single=True)
    return _pcall(body, name="head", grid=(nblk,),
                  out_shape=(jax.ShapeDtypeStruct((1, 1), F32), jax.ShapeDtypeStruct((rows, D), F32),
                             jax.ShapeDtypeStruct((rows, D), BF16), jax.ShapeDtypeStruct((rows, D), BF16),
                             jax.ShapeDtypeStruct((rows, 2 * D), BF16),
                             jax.ShapeDtypeStruct((D, D), F32), jax.ShapeDtypeStruct((D, D), F32),
                             jax.ShapeDtypeStruct((D, D), F32), jax.ShapeDtypeStruct((2, D), F32)),
                  in_specs=[row(0), row(0), row(4), row(5), row(0), row(0), _full((3, D)), _full((1, D)), w, w, w],
                  out_specs=(_full((1, 1)), row(0), row(0), row(0), pl.BlockSpec((tb, 2 * D), lambda i: (i, 0)),
                             ANY, ANY, ANY, _full((2, D))),
                  scratch_shapes=[pltpu.VMEM((D, D), F32), pltpu.VMEM((D, D), F32), pltpu.VMEM((D, D), F32),
                                  pltpu.VMEM((1, D), F32), pltpu.VMEM((2, D), F32)],
                  compiler_params=_params(("arbitrary",)))(x, target, proj, proj, y_pool, y_ssm, mod3, norm_post,
                                                           wbp, wbs, wout)


def _glu_bwd(dys, zp, ys_pre, gate, pm, pmt, glu_w):
    rows = dys.shape[0]
    tb = pm.shape[0]
    nblk = rows // tb

    def body(dys_ref, z_ref, ysp_ref, sg_ref, pm_ref, pmt_ref, gw_ref, dyp_ref, dz_ref, dgw_hbm, dgb_ref,
             acc_w, acc_b):
        i = pl.program_id(0)

        @pl.when(i == 0)
        def _():
            acc_w[...] = jnp.zeros_like(acc_w)
            acc_b[...] = jnp.zeros_like(acc_b)

        d_out = _dot(pm_ref[...], dys_ref[...])
        yg, dgelu = _gelu_parts(ysp_ref[...])
        ygb = yg.astype(BF16)
        sg = sg_ref[...]
        silu_z, dsilu_z = _silu_parts(z_ref[...].astype(F32))
        dz = d_out * (yg * sg) * dsilu_z
        dz_ref[...] = _dot(pmt_ref[...], dz.astype(BF16)).astype(BF16)
        dglu = d_out * silu_z
        dq = dglu * yg * sg * (1.0 - sg)
        dqb = dq.astype(BF16)
        acc_b[...] += jnp.sum(dq, axis=0, keepdims=True)
        acc_w[...] += _dot_tn(ygb, dqb)
        dyg = dglu * sg + _dot_nt(dqb, gw_ref[...])
        dyp_ref[...] = (dyg * dgelu).astype(BF16)

        @pl.when(i == nblk - 1)
        def _():
            dgb_ref[...] = acc_b[...]
            pltpu.sync_copy(acc_w, dgw_hbm)

    row = lambda c: pl.BlockSpec((tb, D), lambda i: (i, c))
    return _pcall(body, name="glu_bwd", grid=(nblk,),
                  out_shape=(jax.ShapeDtypeStruct((rows, D), BF16), jax.ShapeDtypeStruct((rows, D), BF16),
                             jax.ShapeDtypeStruct((D, D), F32), jax.ShapeDtypeStruct((1, D), F32)),
                  in_specs=[row(0), row(0), row(0), row(0), _full((tb, tb)), _full((tb, tb)),
                            _full((D, D), single=True)],
                  out_specs=(row(0), row(0), ANY, _full((1, D))),
                  scratch_shapes=[pltpu.VMEM((D, D), F32), pltpu.VMEM((1, D), F32)],
                  compiler_params=_params(("arbitrary",)))(dys, zp, ys_pre, gate, pm, pmt, glu_w)


def _ssm_bwd(dyp, up, states, carries, pmt, wb, wct, ptab, dvec, mat_grads, dpool_w, dw_in_rest):
    rows = dyp.shape[0]
    tb = pmt.shape[0]
    k_steps = tb // SUBLANES
    nblk = rows // tb
    n_mat = len(mat_grads)
    hosted = [*mat_grads, dpool_w, dw_in_rest]
    n_h = len(hosted)
    shard_rows = D // N_DEV
    pool_rows = dpool_w.shape[1] // N_DEV
    items = [_scatter_item(t, t, _rows_of(shard_rows)) for t in range(n_mat)] + \
            [_scatter_item(n_mat, n_mat, _pool_rows_of(pool_rows))] + \
            [_w_in_block_item(n_mat + 1, n_mat + 1, j, ssm_part=False) for j in range(W_IN_SHARD // W_IN_BLOCK)]
    n_in, n_out = 9, 5

    def body(*refs):
        dyp_ref, u_ref, s_ref, cin_ref, pmt_ref, wb_ref, wct_ref, p_ref, d_ref = refs[:n_in]
        src_refs = refs[n_in:n_in + n_h]
        du_ref, dbb_ref, dcc_ref, da_ref, dd_ref = refs[n_in + n_h:n_in + n_h + n_out]
        recv_refs = refs[n_in + n_h + n_out:n_in + 2 * n_h + n_out]
        (g_ref, carry_b, fin_ref, acc_wb, acc_wct, acc_da, acc_dd, dup_ref,
         *sems) = refs[n_in + 2 * n_h + n_out:]
        i = pl.program_id(0)

        @pl.when(i == 0)
        def _():
            _hosted_copies(items, src_refs, recv_refs, *sems, act="start")
            carry_b[...] = jnp.zeros_like(carry_b)
            acc_wb[...] = jnp.zeros_like(acc_wb)
            acc_wct[...] = jnp.zeros_like(acc_wct)
            acc_da[...] = jnp.zeros_like(acc_da)
            acc_dd[...] = jnp.zeros_like(acc_dd)

        def own_products(acc, q, chan, state, base):
            chan_t = chan.T
            for j in range(16 // 2):
                r = slice(j * 2 * G_H, (j + 1) * 2 * G_H)
                re = base + j * 128
                slab = jnp.concatenate([state[:, re:re + 128], state[:, re + Q_W // 2:re + Q_W // 2 + 128]],
                                       axis=1).astype(BF16)
                acc[q, r, :] += _dot(chan_t[r, :], slab)

        dy = dyp_ref[...]
        up = u_ref[...]
        acc_dd[...] += jnp.sum(dy.astype(F32) * up.astype(F32), axis=0, keepdims=True)
        for q in range(N_Q):
            cols = slice(q * 256, (q + 1) * 256)
            g_ref[:, q * Q_W:(q + 1) * Q_W] = _dot(dy[:, cols], wct_ref[q])
            own_products(acc_wct, q, dy[:, cols], s_ref, q * Q_W)
        for q in range(N_Q):
            _scan_backward(q, g_ref, s_ref, p_ref, carry_b, cin_ref, fin_ref, acc_da, k_steps)
        for q in range(N_Q):
            cols = slice(q * 256, (q + 1) * 256)
            lam = g_ref[:, q * Q_W:(q + 1) * Q_W].astype(BF16)
            own_products(acc_wb, q, up[:, cols], lam, 0)
            dup_ref[:, cols] = (_dot_nt(lam, wb_ref[q]) + d_ref[:, cols] * dy[:, cols].astype(F32)).astype(BF16)
        du_ref[...] = _dot(pmt_ref[...], dup_ref[...]).astype(BF16)

        @pl.when(i == nblk - 1)
        def _():
            da_ref[...] = acc_da[...]
            dd_ref[...] = acc_dd[...]
            lane = lax.broadcasted_iota(jnp.int32, (16 * G_H, 128), 1)
            row = lax.broadcasted_iota(jnp.int32, (16 * G_H, 128), 0)
            own = lane // G_P == (row // G_H) % 2
            spread = (lax.broadcasted_iota(jnp.int32, (G_P, 128), 1) % G_P ==
                      lax.broadcasted_iota(jnp.int32, (G_P, 128), 0)).astype(F32)
            for acc, out in ((acc_wb, dbb_ref), (acc_wct, dcc_ref)):
                for half in range(2):
                    for q in range(N_Q):
                        kept = jnp.where(own, acc[q, :, half * 128:(half + 1) * 128], 0.0)
                        out[half, q] = lax.dot_general(kept, spread, (((1,), (1,)), ((), ())),
                                                       preferred_element_type=F32, precision=lax.Precision.HIGHEST)
            _hosted_copies(items, src_refs, recv_refs, *sems, act="wait")

    rev = lambda c: pl.BlockSpec((tb, D), lambda i: (nblk - 1 - i, c))
    recv = [jax.ShapeDtypeStruct((N_DEV, shard_rows, D), F32)] * n_mat + \
           [jax.ShapeDtypeStruct((N_DEV, dpool_w.shape[0], pool_rows, dpool_w.shape[2]), F32),
            jax.ShapeDtypeStruct((N_DEV, D, W_IN_SHARD), BF16)]
    return _pcall(body, name="ssm_bwd", grid=(nblk,),
                  out_shape=(jax.ShapeDtypeStruct((rows, D), BF16),
                             jax.ShapeDtypeStruct((2, N_Q, 16 * G_H, G_P), F32),
                             jax.ShapeDtypeStruct((2, N_Q, 16 * G_H, G_P), F32),
                             jax.ShapeDtypeStruct((1, N_STATE), F32), jax.ShapeDtypeStruct((1, D), F32), *recv),
                  in_specs=[rev(0), rev(0), pl.BlockSpec((tb, N_STATE), lambda i: (nblk - 1 - i, 0)),
                            pl.BlockSpec((None, 1, N_STATE), lambda i: (nblk - 1 - i, 0, 0)),
                            _full((tb, tb)),
                            _full((N_Q, 256, Q_W), single=True), _full((N_Q, 256, Q_W), single=True),
                            _full((k_steps, N_STATE)), _full((1, D))] + [ANY] * n_h,
                  out_specs=(rev(0), _full((2, N_Q, 16 * G_H, G_P)), _full((2, N_Q, 16 * G_H, G_P)),
                             _full((1, N_STATE)), _full((1, D)), *([ANY] * n_h)),
                  scratch_shapes=[pltpu.VMEM((tb, N_STATE), F32), pltpu.VMEM((1, N_STATE), F32),
                                  pltpu.VMEM((SUBLANES, N_STATE), F32),
                                  pltpu.VMEM((N_Q, 16 * G_H, 256), F32), pltpu.VMEM((N_Q, 16 * G_H, 256), F32),
                                  pltpu.VMEM((1, N_STATE), F32), pltpu.VMEM((1, D), F32),
                                  pltpu.VMEM((tb, D), BF16)] + _sem_scratch(items),
                  compiler_params=_params(("arbitrary",), vmem=60 * 1024 * 1024),
                  )(dyp, up, states, carries, pmt, wb, wct, ptab, dvec, *hosted)


def _pool_bwd(dyp, pooled, proj, pool_w, pool_scale):
    rows = dyp.shape[0]
    tb = _tb(rows, 1024)
    nblk = rows // tb

    def body(dy_ref, pooled_ref, z_ref, pw_ref, ps_ref, dp_ref, dpw_ref, dps_ref, ahead_ref):
        i = pl.program_id(0)
        blk = nblk - 1 - i

        @pl.when(i == 0)
        def _():
            ahead_ref[...] = jnp.zeros_like(ahead_ref)
            dpw_ref[...] = jnp.zeros_like(dpw_ref)
            dps_ref[...] = jnp.zeros_like(dps_ref)

        inv_counts = _inv_counts(tb, blk * tb)
        silu_z, dsilu_z = _silu_parts(z_ref[...].astype(F32))
        dy = dy_ref[...].astype(F32)
        for g, w in enumerate(POOL_WINDOWS):
            cols = slice(g * 256, (g + 1) * 256)
            pooled_b = pooled_ref[:, cols]
            mixed = _dot(pooled_b, pw_ref[g])
            scale = ps_ref[:, cols]
            dp_ref[:, D + g * 256:D + (g + 1) * 256] = (dy[:, cols] * (mixed * scale) * dsilu_z[:, cols]).astype(BF16)
            dms = dy[:, cols] * silu_z[:, cols]
            dps_ref[:, cols] += jnp.sum(dms * mixed, axis=0, keepdims=True)
            dmixed = (dms * scale).astype(BF16)
            dpw_ref[g] += _dot_tn(pooled_b, dmixed)
            dpooled = _dot_nt(dmixed, pw_ref[g])
            ratio = dpooled * inv_counts[g]
            acc = jnp.concatenate([ratio, ahead_ref[:, cols]], axis=0)
            ahead_ref[:, cols] = ratio[:HALO, :]
            s = 1
            while s < w:
                acc = acc + pltpu.roll(acc, tb + HALO - s, axis=0)
                s *= 2
            dp_ref[:, cols] = (acc[:tb, :] - dpooled).astype(BF16)

    rev = lambda c: pl.BlockSpec((tb, D), lambda i: (nblk - 1 - i, c))
    return _pcall(body, name="pool_bwd", grid=(nblk,),
                  out_shape=(jax.ShapeDtypeStruct((rows, 2 * D), BF16), jax.ShapeDtypeStruct((4, 256, 256), F32),
                             jax.ShapeDtypeStruct((1, D), F32)),
                  in_specs=[rev(0), rev(0), rev(1), _full((4, 256, 256)), _full((1, D))],
                  out_specs=(pl.BlockSpec((tb, 2 * D), lambda i: (nblk - 1 - i, 0)), _full((4, 256, 256)),
                             _full((1, D))),
                  scratch_shapes=[pltpu.VMEM((HALO, D), F32)],
                  compiler_params=_params(("arbitrary",)))(dyp, pooled, proj, pool_w, pool_scale)


def _dproj_specs(tb):
    return [pl.BlockSpec((tb, 2 * D), lambda i: (i, 0)), pl.BlockSpec((tb, D), lambda i: (i, 0)),
            pl.BlockSpec((tb, D), lambda i: (i, 0)), pl.BlockSpec((tb, 2 * D), lambda i: (i, 0))]


def _in_proj_bwd_x(x, dy, dpp, dus, dzs, dpg, mod3, norm_pre, w_in, dw_in_ssm, recv_w_in):
    rows = x.shape[0]
    tb = _tb(rows, 512)
    nblk = rows // tb
    items = [_w_in_block_item(0, 0, j, ssm_part=True) for j in range(W_IN_SHARD // W_IN_BLOCK)]
    sums_item = [_Item(0, 0, _whole, _slot)]

    def body(x_ref, dy_ref, dpp_ref, dus_ref, dzs_ref, dpg_ref, mod_ref, np_ref, w_ref,
             dw_src, _, gx_ref, recv_w, recv_sums, vec_ref, ssem, rsem, lsem, *sums_sems):
        src_refs, recv_refs, sems = (dw_src,), (recv_w,), (ssem, rsem, lsem)

        @pl.when(pl.program_id(0) == 0)
        def _():
            _hosted_copies(items, src_refs, recv_refs, *sems, act="start")
            vec_ref[...] = jnp.zeros_like(vec_ref)

        dh = _dot_nt(dpp_ref[...], w_ref[:, 0:2 * D])
        dh += _dot_nt(dus_ref[...], w_ref[:, 2 * D:3 * D])
        dh += _dot_nt(dzs_ref[...], w_ref[:, 3 * D:4 * D])
        dh += _dot_nt(dpg_ref[...], w_ref[:, 4 * D:6 * D])
        xn, r, _ = _prenorm(x_ref[...], mod_ref[...], np_ref[...])
        one_scale = 1.0 + mod_ref[1:2, :]
        vec_ref[0:1, :] += jnp.sum(dh, axis=0, keepdims=True)
        vec_ref[1:2, :] += jnp.sum(dh * xn, axis=0, keepdims=True) * np_ref[...]
        vec_ref[2:3, :] += jnp.sum(dh * xn, axis=0, keepdims=True) * one_scale
        gx_ref[...] = dy_ref[...] + _rms_bwd(dh * (np_ref[...] * one_scale), xn, r)

        @pl.when(pl.program_id(0) == nblk - 1)
        def _():
            _hosted_copies(sums_item, (vec_ref,), (recv_sums,), *sums_sems, act="start")
            _hosted_copies(items, src_refs, recv_refs, *sems, act="wait")
            _hosted_copies(sums_item, (vec_ref,), (recv_sums,), *sums_sems, act="wait")

    row = pl.BlockSpec((tb, D), lambda i: (i, 0))
    recv = (jax.ShapeDtypeStruct(recv_w_in.shape, recv_w_in.dtype), jax.ShapeDtypeStruct((N_DEV, 3, D), F32))
    return _pcall(body, name="in_proj_bwd_x", grid=(nblk,),
                  out_shape=(jax.ShapeDtypeStruct((rows, D), F32), *recv),
                  in_specs=[row, row] + _dproj_specs(tb) + [_full((3, D)), _full((1, D)),
                                                            _full((D, N_IN), single=True)] + [ANY] * 2,
                  out_specs=(row, ANY, ANY),
                  input_output_aliases={10: 1},
                  scratch_shapes=[pltpu.VMEM((3, D), F32)] + _sem_scratch(items) + _sem_scratch(sums_item),
                  compiler_params=_params(("arbitrary",)))(x, dy, dpp, dus, dzs, dpg, mod3, norm_pre, w_in,
                                                           dw_in_ssm, recv_w_in)


def _in_proj_bwd_w(name, x, dparts, mod3, norm_pre, gathered=()):
    rows = x.shape[0]
    tb = _tb(rows, 512)
    nblk = rows // tb
    widths = [p.shape[1] for p in dparts]
    n_p, n_g = len(dparts), len(gathered)
    items = [_Item(t, t, _whole, _slot) for t in range(n_g)]

    def body(x_ref, *rest):
        part_refs, (mod_ref, np_ref) = rest[:n_p], rest[n_p:n_p + 2]
        src_refs, dw_ref = rest[n_p + 2:n_p + 2 + n_g], rest[n_p + 2 + n_g]
        recv_refs, (acc, *sems) = rest[n_p + 3 + n_g:n_p + 3 + 2 * n_g], rest[n_p + 3 + 2 * n_g:]
        i = pl.program_id(0)

        @pl.when(i == 0)
        def _():
            if n_g:
                _hosted_copies(items, src_refs, recv_refs, *sems, act="start")
            acc[...] = jnp.zeros_like(acc)

        _, _, h = _prenorm(x_ref[...], mod_ref[...], np_ref[...])
        ht = h.astype(BF16)
        lo = 0
        for ref, w in zip(part_refs, widths):
            acc[:, lo:lo + w] += _dot_tn(ht, ref[...])
            lo += w

        @pl.when(i == nblk - 1)
        def _():
            dw_ref[...] = acc[...].astype(BF16)
            if n_g:
                _hosted_copies(items, src_refs, recv_refs, *sems, act="wait")

    row = pl.BlockSpec((tb, D), lambda i: (i, 0))
    out = _pcall(body, name=name, grid=(nblk,),
                 out_shape=(jax.ShapeDtypeStruct((D, sum(widths)), BF16),
                            *[jax.ShapeDtypeStruct((N_DEV,) + g.shape, g.dtype) for g in gathered]),
                 in_specs=[row] + [pl.BlockSpec((tb, w), lambda i: (i, 0)) for w in widths] +
                          [_full((3, D)), _full((1, D))] + [ANY] * n_g,
                 out_specs=(_full((D, sum(widths))), *([ANY] * n_g)),
                 scratch_shapes=[pltpu.VMEM((D, sum(widths)), F32)] + (_sem_scratch(items) if n_g else []),
                 compiler_params=_params(("arbitrary",)))(x, *dparts, mod3, norm_pre, *gathered)
    return out if n_g else out[0]


def _adamw_math(w, g, m, v):
    m = ADAM_B1 * m + (1.0 - ADAM_B1) * g
    v = ADAM_B2 * v + (1.0 - ADAM_B2) * (g * g)
    m_hat = m / (1.0 - ADAM_B1 ** ADAM_STEP)
    v_hat = v / (1.0 - ADAM_B2 ** ADAM_STEP)
    delta = -ADAM_LR * (m_hat / (jnp.sqrt(v_hat) + ADAM_EPS) + ADAM_WD * w)
    return delta, m, v


def _sum_sources(ref):
    g = ref[0].astype(F32)
    for s in range(1, N_DEV):
        g = g + ref[s].astype(F32)
    return g


def _adamw_reduce(name, parts, w, m, v):
    r, c = w.shape
    tr = r if r * c <= 256 * 1024 else max(8, (256 * 1024 // c) // 8 * 8)
    while r % tr:
        tr -= 8

    def body(p_ref, w_ref, m_ref, v_ref, g_ref, d_ref, nm_ref, nv_ref):
        g = _sum_sources(p_ref)
        g_ref[...] = g
        d_ref[...], nm_ref[...], nv_ref[...] = _adamw_math(w_ref[...], g, m_ref[...], v_ref[...])

    blk = pl.BlockSpec((tr, c), lambda i: (i, 0))
    return _pcall(body, name=name, grid=(r // tr,),
                  out_shape=tuple([jax.ShapeDtypeStruct((r, c), F32)] * 4),
                  in_specs=[pl.BlockSpec((N_DEV, tr, c), lambda i: (0, i, 0)), blk, blk, blk],
                  out_specs=(blk, blk, blk, blk),
                  compiler_params=_params(("arbitrary",)))(parts, w, m, v)


SC_TILES = 32
SC_LANES = 16


def _adamw_reduce_sc(name, parts, w, m, v):
    rows, cols = w.shape
    per = rows // SC_TILES

    def body(p_hbm, w_hbm, m_hbm, v_hbm, g_hbm, d_hbm, nm_hbm, nv_hbm, pbuf, wbuf, mbuf, vbuf):
        tile = lax.axis_index("subcore") * 2 + lax.axis_index("core")
        mine = pl.ds(tile * per, per)
        for s in range(N_DEV):
            pltpu.sync_copy(p_hbm.at[s, mine, :], pbuf.at[s])
        pltpu.sync_copy(w_hbm.at[mine, :], wbuf)
        pltpu.sync_copy(m_hbm.at[mine, :], mbuf)
        pltpu.sync_copy(v_hbm.at[mine, :], vbuf)
        for r in range(per):
            @pl.loop(0, cols, step=SC_LANES)
            def _(c, r=r):
                lanes = pl.ds(c, SC_LANES)
                g = pbuf[0, r, lanes]
                for s in range(1, N_DEV):
                    g = g + pbuf[s, r, lanes]
                delta, nm, nv = _adamw_math(wbuf[r, lanes], g, mbuf[r, lanes], vbuf[r, lanes])
                pbuf[0, r, lanes] = g
                wbuf[r, lanes] = delta
                mbuf[r, lanes] = nm
                vbuf[r, lanes] = nv
        pltpu.sync_copy(pbuf.at[0], g_hbm.at[mine, :])
        pltpu.sync_copy(wbuf, d_hbm.at[mine, :])
        pltpu.sync_copy(mbuf, nm_hbm.at[mine, :])
        pltpu.sync_copy(vbuf, nv_hbm.at[mine, :])

    return pl.kernel(
        body, name=name, out_type=[jax.ShapeDtypeStruct((rows, cols), F32)] * 4,
        mesh=plsc.VectorSubcoreMesh(core_axis_name="core", subcore_axis_name="subcore"),
        scratch_types=[pltpu.VMEM((N_DEV, per, cols), F32), pltpu.VMEM((per, cols), F32),
                       pltpu.VMEM((per, cols), F32), pltpu.VMEM((per, cols), F32)],
    )(parts, w, m, v)


def _adamw_small(gs, ws, ms, vs):
    n = len(gs)

    def body(*refs):
        ins, outs = refs[:4 * n], refs[4 * n:]
        for t in range(n):
            g_ref, w_ref, m_ref, v_ref = ins[4 * t:4 * t + 4]
            outs[3 * t][...], outs[3 * t + 1][...], outs[3 * t + 2][...] = _adamw_math(
                w_ref[...], g_ref[...], m_ref[...], v_ref[...])

    vm = pl.BlockSpec(memory_space=pltpu.VMEM)
    flat = [a for t in range(n) for a in (gs[t], ws[t], ms[t], vs[t])]
    return _pcall(body, name="adamw_small",
                  out_shape=tuple(jax.ShapeDtypeStruct(w.shape, F32) for w in ws for _ in range(3)),
                  in_specs=[vm] * (4 * n), out_specs=tuple([vm] * (3 * n)), compiler_params=_params())(*flat)


def _sum_small(parts):
    n = len(parts)

    def body(*refs):
        for t in range(n):
            refs[n + t][...] = _sum_sources(refs[t])

    vm = pl.BlockSpec(memory_space=pltpu.VMEM)
    return _pcall(body, name="sum_small",
                  out_shape=tuple(jax.ShapeDtypeStruct(p.shape[1:], F32) for p in parts),
                  in_specs=[vm] * n, out_specs=tuple([vm] * n), compiler_params=_params())(*parts)


def _ada_update(c_all, dmod_cols, w, m, v):
    def body(c_ref, dm_ref, w_ref, m_ref, v_ref, g_ref, d_ref, nm_ref, nv_ref):
        ca = c_ref[...]
        g = lax.dot_general(ca * jax.nn.sigmoid(ca), dm_ref[...], (((0,), (0,)), ((), ())),
                            preferred_element_type=F32, precision=lax.Precision.HIGHEST)
        g_ref[...] = g
        d_ref[...], nm_ref[...], nv_ref[...] = _adamw_math(w_ref[...], g, m_ref[...], v_ref[...])

    vm = pl.BlockSpec(memory_space=pltpu.VMEM)
    return _pcall(body, name="ada_update", out_shape=tuple([jax.ShapeDtypeStruct(w.shape, F32)] * 4),
                  in_specs=[vm] * 5, out_specs=(vm, vm, vm, vm), compiler_params=_params())(c_all, dmod_cols, w, m, v)


def kernel(x, c, w_ada, b_ada, norm_pre, norm_post, w_in, pool_w, pool_scale, ssm_a_re, ssm_a_im, ssm_log_dt, ssm_b_re, ssm_b_im, ssm_c_re, ssm_c_im, ssm_d, glu_w, glu_b, w_branch_pool, w_branch_ssm, w_out, loss_target, m_w_ada, m_b_ada, m_norm_pre, m_norm_post, m_w_in, m_pool_w, m_pool_scale, m_ssm_a_re, m_ssm_a_im, m_ssm_log_dt, m_ssm_b_re, m_ssm_b_im, m_ssm_c_re, m_ssm_c_im, m_ssm_d, m_glu_w, m_glu_b, m_w_branch_pool, m_w_branch_ssm, m_w_out, v_w_ada, v_b_ada, v_norm_pre, v_norm_post, v_w_in, v_pool_w, v_pool_scale, v_ssm_a_re, v_ssm_a_im, v_ssm_log_dt, v_ssm_b_re, v_ssm_b_im, v_ssm_c_re, v_ssm_c_im, v_ssm_d, v_glu_w, v_glu_b, v_w_branch_pool, v_w_branch_ssm, v_w_out):
    given = dict(locals())
    me = _flat(_me())
    rows = x.shape[1]
    x2 = x[0]
    target = loss_target[0]
    ada_cols = w_ada.shape[2]

    tb_ssm = _tb(rows, 256)
    k_steps = tb_ssm // SUBLANES
    a_re, a_im = ssm_a_re[0], ssm_a_im[0]
    log_dt = ssm_log_dt[0].reshape(GROUPS, 1)
    b_re_t, b_im_t = ssm_b_re[0].transpose(0, 2, 1), ssm_b_im[0].transpose(0, 2, 1)
    s5_params = [a_re, a_im, log_dt, b_re_t, b_im_t, ssm_c_re[0], ssm_c_im[0]]

    f32_shards = [w_in[0], pool_w[0], glu_w[0], w_branch_pool[0], w_branch_ssm[0], w_out[0]]
    n_sh = len(f32_shards)

    def local_work(ins, outs):
        for src, dst in zip(ins[:n_sh], outs[:n_sh]):
            dst[...] = src[...].astype(BF16)
        _s5_prep_body(*ins[n_sh:], *outs[n_sh:])

    b_ada_s = lax.dynamic_slice(b_ada, (0, me * ada_cols), (1, ada_cols))
    c_all, mod_rows, *local = _ada_exchange(
        c, w_ada[0], b_ada_s, f32_shards + s5_params,
        [jax.ShapeDtypeStruct(a.shape, BF16) for a in f32_shards] + list(_s5_prep_structs(k_steps)), local_work)
    mod3 = mod_rows.reshape(3, D)
    shards, (wb, wct, pow_re, pow_im) = local[:n_sh], local[n_sh:]
    ptab = _state_layout(pow_re, pow_im)
    dvec = ssm_d[0].reshape(1, D)
    pm = _perm_matrix(tb_ssm)
    pmt = pm.T

    proj, w_in_g, pool_w_g, glu_g = _in_proj(x2, mod3, norm_pre, shards[0], shards[1:3])
    y_pool, pooled = _pool_fwd(proj, pool_w_g, pool_scale)
    y_ssm, ys_pre, carries, states, glu_gate, z_perm, u_perm, wbp_g, wbs_g, wout_g = _ssm_fwd(
        proj, pm, pmt, wb, wct, ptab, dvec, glu_g, glu_b, shards[3:])
    loss_part, dy, dyp, dys, dpg, dwbp, dwbs, dwout, head_vec = _head(
        x2, target, proj, y_pool, y_ssm, mod3, norm_post, wbp_g, wbs_g, wout_g)

    dpp, dpool_w, dpool_scale = _pool_bwd(dyp, pooled, proj, pool_w_g, pool_scale)
    dw_in_rest = _in_proj_bwd_w("in_proj_bwd_w_rest", x2, [dpp, dpg], mod3, norm_pre)
    dy_pre, dzs, dglu_w, dglu_b = _glu_bwd(dys, z_perm, ys_pre, glu_gate, pm, pmt, glu_g)
    dus, dbb, dcc, dabar, dd, p_glu, p_wbp, p_wbs, p_wout, p_pool_w, p_w_in = _ssm_bwd(
        dy_pre, u_perm, states, carries, pmt, wb, wct, ptab, dvec, [dglu_w, dwbp, dwbs, dwout], dpool_w,
        dw_in_rest)

    sc_updates = {
        name: _adamw_reduce_sc("adamw_sc_" + name, parts, given[name][0], given["m_" + name][0], given["v_" + name][0])
        for name, parts in (("w_branch_ssm", p_wbs), ("w_out", p_wout))}

    small32 = jnp.concatenate([head_vec, dpool_scale, dglu_b, dd, jnp.broadcast_to(loss_part, (1, D)),
                               jnp.zeros((2, D), F32), dabar.reshape(8, D)], axis=0)
    small16 = jnp.concatenate([dbb.reshape(2 * GROUPS, D), dcc.reshape(2 * GROUPS, D)], axis=0).astype(BF16)
    dw_in_ssm, p_small32, p_small16 = _in_proj_bwd_w("in_proj_bwd_w_ssm", x2, [dus, dzs], mod3, norm_pre,
                                                     gathered=(small32, small16))
    grad_x, p_w_in, p_pre = _in_proj_bwd_x(x2, dy, dpp, dus, dzs, dpg, mod3, norm_pre, w_in_g, dw_in_ssm, p_w_in)

    tot32, tot16, tot_pre = _sum_small([p_small32, p_small16, p_pre])
    d_abar_re, d_abar_im = _state_unlayout(tot32[8:16].reshape(N_STATE))
    d_bb_re, d_bb_im = tot16[0:64].reshape(GROUPS, G_H, G_P), tot16[64:128].reshape(GROUPS, G_H, G_P)
    g_a_re, g_a_im, g_log_dt, g_b_re_t, g_b_im_t = _s5_prep_bwd(
        a_re, a_im, log_dt, b_re_t, b_im_t, d_abar_re, d_abar_im, d_bb_re, d_bb_im)

    grads, deltas, new_m, new_v = {}, {}, {}, {}

    small = []

    def small_update(name, g2):
        small.append((name, g2))

    def shard_update(name, parts):
        shape = given[name].shape
        r2 = parts.shape[1:] if parts.ndim == 3 else (parts.shape[1] * parts.shape[2], parts.shape[3])
        w2, m2, v2 = (given[p + name].reshape(r2) for p in ("", "m_", "v_"))
        out = _adamw_reduce("adamw_" + name, parts.reshape((N_DEV,) + tuple(r2)), w2, m2, v2)
        grads[name], deltas[name], new_m[name], new_v[name] = (a.reshape(shape) for a in out)

    dmod_all = jnp.concatenate([p_pre[:, 0:2, :], p_small32[:, 0:1, :]], axis=1).reshape(N_DEV, 3 * D)
    dmod_cols = lax.dynamic_slice(dmod_all, (0, me * ada_cols), (N_DEV, ada_cols))
    out = _ada_update(c_all, dmod_cols, w_ada[0], m_w_ada[0], v_w_ada[0])
    grads['w_ada'], deltas['w_ada'], new_m['w_ada'], new_v['w_ada'] = (a.reshape(w_ada.shape) for a in out)

    small_update('b_ada', jnp.concatenate([tot_pre[0:2], tot32[0:1]], axis=0).reshape(1, 3 * D))
    small_update('norm_pre', tot_pre[2:3])
    small_update('norm_post', tot32[1:2])
    small_update('pool_scale', tot32[2:3])
    small_update('glu_b', tot32[3:4])
    small_update('ssm_d', tot32[4:5])
    small_update('ssm_a_re', g_a_re)
    small_update('ssm_a_im', g_a_im)
    small_update('ssm_log_dt', g_log_dt.reshape(1, GROUPS))
    small_update('ssm_b_re', g_b_re_t.transpose(0, 2, 1).reshape(GROUPS, G_P * G_H))
    small_update('ssm_b_im', g_b_im_t.transpose(0, 2, 1).reshape(GROUPS, G_P * G_H))
    small_update('ssm_c_re', tot16[128:192])
    small_update('ssm_c_im', -tot16[192:256])
    flat = _adamw_small([g2 for _, g2 in small],
                        *[[given[p + name].reshape(g2.shape) for name, g2 in small] for p in ("", "m_", "v_")])
    for t, (name, g2) in enumerate(small):
        shape = given[name].shape
        grads[name], deltas[name], new_m[name], new_v[name] = (
            a.reshape(shape) for a in (g2, *flat[3 * t:3 * t + 3]))
    shard_update('w_in', p_w_in)
    shard_update('pool_w', p_pool_w)
    shard_update('glu_w', p_glu)
    shard_update('w_branch_pool', p_wbp)
    for name, out in sc_updates.items():
        grads[name], deltas[name], new_m[name], new_v[name] = (a.reshape(given[name].shape) for a in out)

    return (tot32[5, 0], grad_x[None], *[grads[n] for n in WEIGHTS], *[deltas[n] for n in WEIGHTS],
            *[new_m[n] for n in WEIGHTS], *[new_v[n] for n in WEIGHTS])
```
